```python
import math
import jax, jax.numpy as jnp
from jax import lax
import numpy as np

D_MODEL = 2048
BATCH = 2
SEQ = 4096
DEPTH = 2
DEC_BATCH = 8
DEC_SEQ = 1
PAST_LEN = 16384
PAGE_SIZE = 128

HEAD_DIM = 128
NSA_HEADS = 8
NSA_KV = 2
NSA_HPG = NSA_HEADS // NSA_KV
NSA_CMP_LEN = 32
NSA_CMP_STRIDE = 16
NSA_CMP_HIDDEN = 2 * HEAD_DIM
NSA_SEL_BLOCK = 64
NSA_SEL_COUNT = 16
NSA_WINDOW = 512
NSA_SEL_FORCE = 1.0e4
MOBA_HEADS = 8
MOBA_BLOCK = 256
MOBA_TOPK = 3
Q_CHUNK = 32
FFN_DENSE = 5632
POOL_WIDTH = D_MODEL // 2
POOL_WINDOWS = (2, 4, 8, 16)
POOL_GROUP = POOL_WIDTH // len(POOL_WINDOWS)
POOL_BUF = max(POOL_WINDOWS) - 1
CONV_CH = D_MODEL // 2
CONV_WIDTH = 31
CONV_BUF = CONV_WIDTH - 1
N_EXPERTS = 8
TOP_K = 2
FFN_EXPERT = 7 * D_MODEL // 2
N_ATTN_LAYERS = (DEPTH + 1) // 2
N_CONV_LAYERS = DEPTH // 2
NORM_EPS = 1e-6

NSA_Q_W = NSA_HEADS * HEAD_DIM
NSA_KV_W = NSA_KV * HEAD_DIM
MOBA_W = MOBA_HEADS * HEAD_DIM
ATTN_IN_SPLITS = (NSA_Q_W,) + (NSA_KV_W,) * 6 + (3 * NSA_HEADS,) + (MOBA_W,) * 3
ATTN_IN_W = sum(ATTN_IN_SPLITS)
ATTN_MIX_W = NSA_Q_W + MOBA_W
CONV_IN_W = POOL_WIDTH + 2 * CONV_CH
CONV_MIX_W = POOL_WIDTH + CONV_CH

kernel_name = 'nsa_moba_pool_conformer_hybrid_step'


def rmsnorm(x, g):
    xf = x.astype(jnp.float32)
    y = xf * lax.rsqrt(jnp.mean(xf * xf, axis=-1, keepdims=True) + NORM_EPS)
    return (y * g.astype(jnp.float32)).astype(x.dtype)


def layernorm(x, g, b):
    xf = x.astype(jnp.float32)
    mu = jnp.mean(xf, axis=-1, keepdims=True)
    xc = xf - mu
    y = xc * lax.rsqrt(jnp.mean(xc * xc, axis=-1, keepdims=True) + NORM_EPS)
    return (y * g.astype(jnp.float32) + b.astype(jnp.float32)).astype(x.dtype)


def masked_softmax(s, mask):
    s = jnp.where(mask, s.astype(jnp.float32), -jnp.inf)
    m = jnp.max(s, axis=-1, keepdims=True)
    m = jnp.where(jnp.isfinite(m), m, 0.0)
    p = jnp.exp(s - m)
    d = jnp.sum(p, axis=-1, keepdims=True)
    return p / jnp.where(d > 0, d, 1.0)


def split_cols(y, sizes):
    offs = [int(o) for o in np.cumsum(sizes)[:-1]]
    return jnp.split(y, offs, axis=-1)


def pad_time(a, mult):
    pad = (-a.shape[1]) % mult
    return jnp.pad(a, [(0, 0), (0, pad)] + [(0, 0)] * (a.ndim - 2))


def gather_blocks(blocks, idx):
    return jax.vmap(jax.vmap(lambda b_, i_: b_[i_]))(blocks, idx)


def chunk_plan(tq):
    qc = math.gcd(tq, Q_CHUNK)
    return qc, tq // qc


def to_chunks(a, qc, nc):
    return a.reshape((a.shape[0], nc, qc) + a.shape[2:]).swapaxes(0, 1)


def from_chunks(a):
    a = a.swapaxes(0, 1)
    return a.reshape((a.shape[0], a.shape[1] * a.shape[2]) + a.shape[3:])


def paged_rows(pool, li, page_table):
    g = pool[li, page_table]
    return g.reshape((g.shape[0], g.shape[1] * g.shape[2]) + g.shape[3:])


def swiglu(h, wg, wu, wd):
    return (jax.nn.silu(h @ wg) * (h @ wu)) @ wd


def nsa_compress(raw, pe, w1, w2):
    B, L = raw.shape[:2]
    S = NSA_CMP_STRIDE
    r = NSA_CMP_LEN // S
    nsub = L // S
    n_c = nsub - r + 1
    sub = raw[:, :nsub * S].reshape(B, nsub, S, NSA_KV, HEAD_DIM)
    w1r = w1.reshape(r, S, HEAD_DIM, NSA_CMP_HIDDEN)
    per = pe.reshape(r, S, HEAD_DIM)
    h = sum(jnp.einsum('bnsgd,sdh->bngh', sub[:, m:m + n_c], w1r[m]) + jnp.einsum('sd,sdh->h', per[m], w1r[m])
            for m in range(r))
    return jnp.einsum('bngh,hd->bngd', jax.nn.silu(h), w2)


def nsa_attend(q, pos0, kc, vc, ks, vs, kw, vw, kw_start):
    B, Tq = q.shape[:2]
    L = ks.shape[1]
    n_c = kc.shape[1]
    scale = HEAD_DIM ** -0.5
    r = NSA_CMP_LEN // NSA_CMP_STRIDE
    sub_per_sel = NSA_SEL_BLOCK // NSA_CMP_STRIDE
    P = NSA_SEL_BLOCK
    cmp_end = jnp.arange(n_c, dtype=jnp.int32) * NSA_CMP_STRIDE + NSA_CMP_LEN - 1
    n_sel_full = L // P
    k_sel = min(NSA_SEL_COUNT - 1, n_sel_full)
    ks_p, vs_p = pad_time(ks, P), pad_time(vs, P)
    ks_blk = ks_p.reshape(B, -1, P, NSA_KV, HEAD_DIM).transpose(0, 3, 1, 2, 4)
    vs_blk = vs_p.reshape(B, -1, P, NSA_KV, HEAD_DIM).transpose(0, 3, 1, 2, 4)
    win_pad = [(0, 0), (NSA_WINDOW, 0), (0, 0), (0, 0)]
    kw_p, vw_p = jnp.pad(kw, win_pad), jnp.pad(vw, win_pad)
    qc, nc = chunk_plan(Tq)
    jb = jnp.arange(n_sel_full, dtype=jnp.int32)

    def one_chunk(args):
        q_c, p0 = args
        t = p0 + jnp.arange(qc, dtype=jnp.int32)
        qg = q_c.reshape(B, qc, NSA_KV, NSA_HPG, HEAD_DIM)
        s_c = jnp.einsum('bqgzd,bngd->bqgzn', qg, kc) * scale
        m_c = (cmp_end[None, :] <= t[:, None])[None, :, None, None, :]
        p_c = masked_softmax(s_c, m_c)
        o_c = jnp.einsum('bqgzn,bngd->bqgzd', p_c.astype(vc.dtype), vc)
        imp = sum(jnp.pad(p_c, [(0, 0)] * 4 + [(m, r - 1 - m)]) for m in range(r)) / r
        imp = jnp.sum(imp, axis=3)[..., :n_sel_full * sub_per_sel]
        imp = imp.reshape(B, qc, NSA_KV, n_sel_full, sub_per_sel).sum(-1)
        own = t // P
        past = (jb[None, :] < own[:, None])[None, :, None, :]
        forced = ((jb[None, :] == 0) | (jb[None, :] == own[:, None] - 1))[None, :, None, :]
        score = jnp.where(past, jnp.where(forced, NSA_SEL_FORCE, imp), -jnp.inf)
        _, idx = lax.top_k(score, k_sel)
        ok = idx < own[None, :, None, None]
        idx_t = idx.transpose(0, 2, 1, 3)
        g_k = gather_blocks(ks_blk, idx_t)
        g_v = gather_blocks(vs_blk, idx_t)
        s_s = (jnp.einsum('bqgzd,bgqkpd->bqgzkp', qg, g_k) * scale).reshape(B, qc, NSA_KV, NSA_HPG, k_sel * P)
        m_s = jnp.broadcast_to(ok[:, :, :, None, :, None], (B, qc, NSA_KV, 1, k_sel, P)).reshape(B, qc, NSA_KV, 1, k_sel * P)
        ob = (p0 // P) * P
        k_o = lax.dynamic_slice_in_dim(ks_p, ob, P, axis=1)
        v_o = lax.dynamic_slice_in_dim(vs_p, ob, P, axis=1)
        s_o = jnp.einsum('bqgzd,bpgd->bqgzp', qg, k_o) * scale
        m_o = ((ob + jnp.arange(P, dtype=jnp.int32))[None, :] <= t[:, None])[None, :, None, None, :]
        p_so = masked_softmax(jnp.concatenate([s_s, s_o], axis=-1),
                              jnp.concatenate([jnp.broadcast_to(m_s, s_s.shape), jnp.broadcast_to(m_o, s_o.shape)], axis=-1))
        p_s = p_so[..., :k_sel * P].reshape(B, qc, NSA_KV, NSA_HPG, k_sel, P).astype(vs.dtype)
        p_o = p_so[..., k_sel * P:].astype(vs.dtype)
        o_s = jnp.einsum('bqgzkp,bgqkpd->bqgzd', p_s, g_v) + jnp.einsum('bqgzp,bpgd->bqgzd', p_o, v_o)
        s0 = p0 - kw_start
        k_w = lax.dynamic_slice_in_dim(kw_p, s0, NSA_WINDOW + qc, axis=1)
        v_w = lax.dynamic_slice_in_dim(vw_p, s0, NSA_WINDOW + qc, axis=1)
        kp = p0 - NSA_WINDOW + jnp.arange(NSA_WINDOW + qc, dtype=jnp.int32)
        m_w = ((kp[None, :] >= kw_start) & (kp[None, :] <= t[:, None]) & (kp[None, :] > t[:, None] - NSA_WINDOW))[None, :, None, None, :]
        p_w = masked_softmax(jnp.einsum('bqgzd,bpgd->bqgzp', qg, k_w) * scale, m_w)
        o_w = jnp.einsum('bqgzp,bpgd->bqgzd', p_w.astype(vw.dtype), v_w)
        return jnp.stack([o_c, o_s, o_w], axis=-2).reshape(B, qc, NSA_HEADS, 3, HEAD_DIM)

    p0s = pos0 + jnp.arange(nc, dtype=jnp.int32) * qc
    return from_chunks(lax.map(one_chunk, (to_chunks(q, qc, nc), p0s)))


def moba_attend(q, pos0, k, v):
    B, Tq = q.shape[:2]
    L = k.shape[1]
    P = MOBA_BLOCK
    scale = HEAD_DIM ** -0.5
    n_full = L // P
    k_top = min(MOBA_TOPK, n_full)
    k_p, v_p = pad_time(k, P), pad_time(v, P)
    k_blk = k_p.reshape(B, -1, P, MOBA_HEADS, HEAD_DIM).transpose(0, 3, 1, 2, 4)
    v_blk = v_p.reshape(B, -1, P, MOBA_HEADS, HEAD_DIM).transpose(0, 3, 1, 2, 4)
    means = jnp.mean(k[:, :n_full * P].reshape(B, n_full, P, MOBA_HEADS, HEAD_DIM).astype(jnp.float32), axis=2).astype(q.dtype)
    qc, nc = chunk_plan(Tq)
    jb = jnp.arange(n_full, dtype=jnp.int32)

    def one_chunk(args):
        q_c, p0 = args
        t = p0 + jnp.arange(qc, dtype=jnp.int32)
        own = t // P
        ob = (p0 // P) * P
        k_o = lax.dynamic_slice_in_dim(k_p, ob, P, axis=1)
        v_o = lax.dynamic_slice_in_dim(v_p, ob, P, axis=1)
        s_o = jnp.einsum('bqhd,bphd->bqhp', q_c, k_o) * scale
        m_o = jnp.broadcast_to(((ob + jnp.arange(P, dtype=jnp.int32))[None, :] <= t[:, None])[None, :, None, :], s_o.shape)
        if k_top == 0:
            p = masked_softmax(s_o, m_o)
            return jnp.einsum('bqhp,bphd->bqhd', p.astype(v.dtype), v_o)
        gsc = jnp.einsum('bqhd,bnhd->bqhn', q_c, means).astype(jnp.float32)
        gsc = jnp.where((jb[None, :] < own[:, None])[None, :, None, :], gsc, -jnp.inf)
        _, idx = lax.top_k(gsc, k_top)
        ok = idx < own[None, :, None, None]
        idx_t = idx.transpose(0, 2, 1, 3)
        g_k = gather_blocks(k_blk, idx_t)
        g_v = gather_blocks(v_blk, idx_t)
        s_s = (jnp.einsum('bqhd,bhqkpd->bqhkp', q_c, g_k) * scale).reshape(B, qc, MOBA_HEADS, k_top * P)
        m_s = jnp.broadcast_to(ok[..., None], (B, qc, MOBA_HEADS, k_top, P)).reshape(B, qc, MOBA_HEADS, k_top * P)
        p = masked_softmax(jnp.concatenate([s_s, s_o], axis=-1), jnp.concatenate([m_s, m_o], axis=-1))
        p_s = p[..., :k_top * P].reshape(B, qc, MOBA_HEADS, k_top, P).astype(v.dtype)
        p_o = p[..., k_top * P:].astype(v.dtype)
        return jnp.einsum('bqhkp,bhqkpd->bqhd', p_s, g_v) + jnp.einsum('bqhp,bphd->bqhd', p_o, v_o)

    p0s = pos0 + jnp.arange(nc, dtype=jnp.int32) * qc
    return from_chunks(lax.map(one_chunk, (to_chunks(q, qc, nc), p0s)))


def attn_mixer(h, pos0, past, w_in, qk_g, cmp_pe, cmp_w1, cmp_w2, moba_g, w_out):
    B, T = h.shape[:2]
    q_a, kc, vc, ks, vs, kw, vw, gate, q_b, k_b, v_b = split_cols(h @ w_in, ATTN_IN_SPLITS)

    def heads(a, n):
        return a.reshape(B, T, n, HEAD_DIM)

    q_a = rmsnorm(heads(q_a, NSA_HEADS), qk_g[0])
    q_b = rmsnorm(heads(q_b, MOBA_HEADS), moba_g[0])
    cmp_rows = jnp.stack([heads(kc, NSA_KV), heads(vc, NSA_KV)], axis=2)
    sel_rows = jnp.stack([rmsnorm(heads(ks, NSA_KV), qk_g[2]), heads(vs, NSA_KV)], axis=2)
    win_rows = jnp.stack([rmsnorm(heads(kw, NSA_KV), qk_g[3]), heads(vw, NSA_KV)], axis=2)
    moba_rows = jnp.stack([rmsnorm(heads(k_b, MOBA_HEADS), moba_g[1]), heads(v_b, MOBA_HEADS)], axis=2)
    if past is None:
        cmp_all, sel_all, win_all, moba_all = cmp_rows, sel_rows, win_rows, moba_rows
        win_start = 0
        new_win = win_rows[:, T - min(NSA_WINDOW, T):]
    else:
        p_cmp, p_sel, p_win, p_moba = past
        cmp_all = jnp.concatenate([p_cmp, cmp_rows], axis=1)
        sel_all = jnp.concatenate([p_sel, sel_rows], axis=1)
        win_all = jnp.concatenate([p_win, win_rows], axis=1)
        moba_all = jnp.concatenate([p_moba, moba_rows], axis=1)
        win_start = pos0 - p_win.shape[1]
        keep = min(NSA_WINDOW, pos0 + T)
        new_win = win_all[:, win_all.shape[1] - keep:]
    kc_t = rmsnorm(nsa_compress(cmp_all[:, :, 0], cmp_pe[0], cmp_w1[0], cmp_w2[0]), qk_g[1])
    vc_t = nsa_compress(cmp_all[:, :, 1], cmp_pe[1], cmp_w1[1], cmp_w2[1])
    o3 = nsa_attend(q_a, pos0, kc_t, vc_t, sel_all[:, :, 0], sel_all[:, :, 1], win_all[:, :, 0], win_all[:, :, 1], win_start)
    g = jax.nn.sigmoid(gate).reshape(B, T, NSA_HEADS, 3)
    o_a = jnp.einsum('bthc,bthcd->bthd', g, o3)
    o_b = moba_attend(q_b, pos0, moba_all[:, :, 0], moba_all[:, :, 1])
    mixed = jnp.concatenate([o_a.reshape(B, T, NSA_Q_W), o_b.reshape(B, T, MOBA_W)], axis=-1)
    return mixed @ w_out, cmp_rows, sel_rows, new_win, moba_rows


def pool_mix(z, n_prev, pool_w, pool_scale):
    B, Lz, C = z.shape
    zf = z.astype(jnp.float32)
    cs = jnp.concatenate([jnp.zeros((B, 1, C), jnp.float32), jnp.cumsum(zf, axis=1)], axis=1)
    i = jnp.arange(n_prev, Lz)
    outs = []
    for gi, w in enumerate(POOL_WINDOWS):
        sl = slice(gi * POOL_GROUP, (gi + 1) * POOL_GROUP)
        lo = jnp.maximum(i + 1 - w, 0)
        cnt = (i + 1 - lo).astype(jnp.float32)
        mean = (cs[:, i + 1, sl] - cs[:, lo, sl]) / cnt[None, :, None]
        outs.append(mean - zf[:, n_prev:, sl])
    y = jnp.stack(outs, axis=2).astype(z.dtype)
    y = jnp.einsum('btgc,gcd->btgd', y, pool_w).reshape(B, Lz - n_prev, POOL_WIDTH)
    return y * pool_scale


def conformer_conv(z, dw, dw_b, ln_g, ln_b, pw):
    c = lax.conv_general_dilated(z, dw[:, None, :].astype(z.dtype), window_strides=(1,), padding='VALID',
                                 dimension_numbers=('NWC', 'WIO', 'NWC'), feature_group_count=z.shape[-1]) + dw_b
    return jax.nn.silu(layernorm(c, ln_g, ln_b)) @ pw


def conv_mixer(h, past, w_in, pool_w, pool_scale, dw, dw_b, ln_g, ln_b, pw, w_out):
    B, T = h.shape[:2]
    proj = h @ w_in
    u_pool = proj[..., :POOL_WIDTH]
    a = proj[..., POOL_WIDTH:]
    u_conv = a[..., :CONV_CH] * jax.nn.sigmoid(a[..., CONV_CH:])
    if past is None:
        prev_pool = u_pool[:, :0]
        prev_conv = jnp.zeros((B, CONV_BUF, CONV_CH), u_conv.dtype)
    else:
        prev_pool, prev_conv = past
    z_pool = jnp.concatenate([prev_pool, u_pool], axis=1)
    z_conv = jnp.concatenate([prev_conv, u_conv], axis=1)
    o_pool = pool_mix(z_pool, prev_pool.shape[1], pool_w, pool_scale)
    o_conv = conformer_conv(z_conv, dw, dw_b, ln_g, ln_b, pw)
    y = jnp.concatenate([o_pool, o_conv], axis=-1) @ w_out
    return y, z_pool[:, z_pool.shape[1] - POOL_BUF:], z_conv[:, z_conv.shape[1] - CONV_BUF:]


def moe_ffn(h, router_w, wg, wu, wd):
    logits = (h @ router_w).astype(jnp.float32)
    top_v, top_i = lax.top_k(logits, TOP_K)
    wts = jax.nn.softmax(top_v, axis=-1)
    gate = jnp.sum(jax.nn.one_hot(top_i, N_EXPERTS, dtype=jnp.float32) * wts[..., None], axis=-2).astype(h.dtype)
    y = jnp.zeros_like(h)
    for e in range(N_EXPERTS):
        y = y + gate[..., e:e + 1] * swiglu(h, wg[e], wu[e], wd[e])
    return y


def setup_inputs(seed: int = 0) -> dict:
    key = jax.random.key(seed)
    ks = iter(jax.random.split(key, 40))

    def nrm(shape, scale=1.0):
        return jax.random.normal(next(ks), shape, jnp.float32) * scale

    def gain(shape):
        return 1.0 + nrm(shape, 0.05)

    n_pages = PAST_LEN // PAGE_SIZE
    n_used = DEC_BATCH * n_pages
    n_phys = n_used + (n_used + 3) // 4
    win_buf = min(NSA_WINDOW, PAST_LEN)
    NA, NC = N_ATTN_LAYERS, N_CONV_LAYERS
    d = {}
    d['x_prompt'] = nrm((BATCH, SEQ, D_MODEL))
    d['x_sample'] = nrm((DEC_BATCH, DEC_SEQ, D_MODEL))
    d['cache_nsa_cmp'] = nrm((NA, n_phys, PAGE_SIZE, 2, NSA_KV, HEAD_DIM))
    d['cache_nsa_sel'] = nrm((NA, n_phys, PAGE_SIZE, 2, NSA_KV, HEAD_DIM))
    d['state_nsa_win'] = nrm((NA, DEC_BATCH, win_buf, 2, NSA_KV, HEAD_DIM))
    d['cache_moba'] = nrm((NA, n_phys, PAGE_SIZE, 2, MOBA_HEADS, HEAD_DIM))
    d['state_pool'] = nrm((NC, DEC_BATCH, POOL_BUF, POOL_WIDTH))
    d['state_conv'] = nrm((NC, DEC_BATCH, CONV_BUF, CONV_CH), 0.5)
    d['page_table'] = jax.random.permutation(next(ks), n_phys)[:n_used].reshape(DEC_BATCH, n_pages).astype(jnp.int32)
    d['attn_norm'] = gain((NA, D_MODEL))
    d['w_attn_in'] = nrm((NA, D_MODEL, ATTN_IN_W), D_MODEL ** -0.5)
    d['nsa_qk_norm'] = gain((NA, 4, HEAD_DIM))
    d['nsa_cmp_pe'] = nrm((NA, 2, NSA_CMP_LEN, HEAD_DIM), 0.2)
    d['nsa_cmp_w1'] = nrm((NA, 2, NSA_CMP_LEN * HEAD_DIM, NSA_CMP_HIDDEN), (NSA_CMP_LEN * HEAD_DIM) ** -0.5)
    d['nsa_cmp_w2'] = nrm((NA, 2, NSA_CMP_HIDDEN, HEAD_DIM), NSA_CMP_HIDDEN ** -0.5)
    d['moba_qk_norm'] = gain((NA, 2, HEAD_DIM))
    d['w_attn_out'] = nrm((NA, ATTN_MIX_W, D_MODEL), ATTN_MIX_W ** -0.5)
    d['ffn_norm'] = gain((NA, D_MODEL))
    d['ffn_w_gate'] = nrm((NA, D_MODEL, FFN_DENSE), D_MODEL ** -0.5)
    d['ffn_w_up'] = nrm((NA, D_MODEL, FFN_DENSE), D_MODEL ** -0.5)
    d['ffn_w_down'] = nrm((NA, FFN_DENSE, D_MODEL), FFN_DENSE ** -0.5)
    d['conv_norm'] = gain((NC, D_MODEL))
    d['w_conv_in'] = nrm((NC, D_MODEL, CONV_IN_W), D_MODEL ** -0.5)
    d['pool_w'] = nrm((NC, len(POOL_WINDOWS), POOL_GROUP, POOL_GROUP), POOL_GROUP ** -0.5)
    d['pool_scale'] = gain((NC, POOL_WIDTH))
    d['conv_dw'] = nrm((NC, CONV_WIDTH, CONV_CH), CONV_WIDTH ** -0.5)
    d['conv_dw_b'] = nrm((NC, CONV_CH), 0.02)
    d['conv_ln_g'] = gain((NC, CONV_CH))
    d['conv_ln_b'] = nrm((NC, CONV_CH), 0.02)
    d['conv_pw'] = nrm((NC, CONV_CH, CONV_CH), CONV_CH ** -0.5)
    d['w_conv_out'] = nrm((NC, CONV_MIX_W, D_MODEL), CONV_MIX_W ** -0.5)
    d['moe_norm'] = gain((NC, D_MODEL))
    d['router_w'] = nrm((NC, D_MODEL, N_EXPERTS), D_MODEL ** -0.5)
    d['moe_w_gate'] = nrm((NC, N_EXPERTS, D_MODEL, FFN_EXPERT), D_MODEL ** -0.5)
    d['moe_w_up'] = nrm((NC, N_EXPERTS, D_MODEL, FFN_EXPERT), D_MODEL ** -0.5)
    d['moe_w_down'] = nrm((NC, N_EXPERTS, FFN_EXPERT, D_MODEL), FFN_EXPERT ** -0.5)
    return d


def reference(x_prompt, x_sample, cache_nsa_cmp, cache_nsa_sel, state_nsa_win, cache_moba, state_pool, state_conv,
              page_table, attn_norm, w_attn_in, nsa_qk_norm, nsa_cmp_pe, nsa_cmp_w1, nsa_cmp_w2, moba_qk_norm,
              w_attn_out, ffn_norm, ffn_w_gate, ffn_w_up, ffn_w_down, conv_norm, w_conv_in, pool_w, pool_scale,
              conv_dw, conv_dw_b, conv_ln_g, conv_ln_b, conv_pw, w_conv_out, moe_norm, router_w, moe_w_gate,
              moe_w_up, moe_w_down):
    past_len = page_table.shape[1] * cache_moba.shape[2]
    xp, xs = x_prompt, x_sample
    cmp_p, cmp_s, sel_p, sel_s, win_p, win_s, moba_p, moba_s, pool_p, pool_s, conv_p, conv_s = ([] for _ in range(12))
    for layer in range(DEPTH):
        li = layer // 2
        if layer % 2 == 0:
            prm = (w_attn_in[li], nsa_qk_norm[li], nsa_cmp_pe[li], nsa_cmp_w1[li], nsa_cmp_w2[li], moba_qk_norm[li], w_attn_out[li])
            past = (paged_rows(cache_nsa_cmp, li, page_table), paged_rows(cache_nsa_sel, li, page_table),
                    state_nsa_win[li], paged_rows(cache_moba, li, page_table))
            yp, c1, s1, w1_, m1 = attn_mixer(rmsnorm(xp, attn_norm[li]), 0, None, *prm)
            ys, c2, s2, w2_, m2 = attn_mixer(rmsnorm(xs, attn_norm[li]), past_len, past, *prm)
            cmp_p.append(c1); cmp_s.append(c2); sel_p.append(s1); sel_s.append(s2)
            win_p.append(w1_); win_s.append(w2_); moba_p.append(m1); moba_s.append(m2)
            xp = xp + yp
            xs = xs + ys
            ffn = (ffn_w_gate[li], ffn_w_up[li], ffn_w_down[li])
            xp = xp + swiglu(rmsnorm(xp, ffn_norm[li]), *ffn)
            xs = xs + swiglu(rmsnorm(xs, ffn_norm[li]), *ffn)
        else:
            prm = (w_conv_in[li], pool_w[li], pool_scale[li], conv_dw[li], conv_dw_b[li], conv_ln_g[li], conv_ln_b[li],
                   conv_pw[li], w_conv_out[li])
            yp, pp, cp = conv_mixer(rmsnorm(xp, conv_norm[li]), None, *prm)
            ys, ps, cs_ = conv_mixer(rmsnorm(xs, conv_norm[li]), (state_pool[li], state_conv[li]), *prm)
            pool_p.append(pp); pool_s.append(ps); conv_p.append(cp); conv_s.append(cs_)
            xp = xp + yp
            xs = xs + ys
            moe = (router_w[li], moe_w_gate[li], moe_w_up[li], moe_w_down[li])
            xp = xp + moe_ffn(rmsnorm(xp, moe_norm[li]), *moe)
            xs = xs + moe_ffn(rmsnorm(xs, moe_norm[li]), *moe)
    return (xp, xs, jnp.stack(cmp_p), jnp.stack(cmp_s), jnp.stack(sel_p), jnp.stack(sel_s), jnp.stack(win_p),
            jnp.stack(win_s), jnp.stack(moba_p), jnp.stack(moba_s), jnp.stack(pool_p), jnp.stack(pool_s),
            jnp.stack(conv_p), jnp.stack(conv_s))
```

```python
import functools
import math

import jax
import jax.numpy as jnp
from jax import lax
from jax.experimental import pallas as pl
from jax.experimental.pallas import tpu as pltpu

F32 = jnp.float32
BF = jnp.bfloat16
NEG_INF = float("-inf")

HEAD_DIM = 128
NORM_EPS = 1e-6
NSA_HEADS = 8
NSA_KV = 2
NSA_HPG = NSA_HEADS // NSA_KV
NSA_CMP_LEN = 32
NSA_CMP_STRIDE = 16
NSA_CMP_HIDDEN = 2 * HEAD_DIM
NSA_SEL_BLOCK = 64
NSA_SEL_COUNT = 16
NSA_WINDOW = 512
NSA_SEL_FORCE = 1.0e4
MOBA_HEADS = 8
MOBA_BLOCK = 256
MOBA_TOPK = 3
POOL_WINDOWS = (2, 4, 8, 16)
CONV_WIDTH = 31
N_EXPERTS = 8
TOP_K = 2

LANE = 128
SUBLANE = 8
VMEM_LIMIT = 58 * 1024 * 1024

CH_QA, CH_KC, CH_VC, CH_KS, CH_VS, CH_KW, CH_VW, CH_QB, CH_KB, CH_VB, CH_GATE = 0, 8, 10, 12, 14, 16, 18, 20, 28, 36, 44
ATTN_CHUNKS = 45

NT_DIMS = (((1,), (1,)), ((), ()))


def _cparams(sem):
    return pltpu.CompilerParams(dimension_semantics=sem, vmem_limit_bytes=VMEM_LIMIT)


def _split_bf16(a):
    hi = a.astype(BF)
    lo = (a - hi.astype(F32)).astype(BF)
    return hi, lo


def _dot3(a, b, dims=(((1,), (0,)), ((), ()))):
    ah, al = _split_bf16(a)
    bh, bl = _split_bf16(b)
    d = lambda x, y: lax.dot_general(x, y, dims, preferred_element_type=F32)
    return d(ah, bh) + d(ah, bl) + d(al, bh)


def _masked_softmax(s, mask, axis):
    s = jnp.where(mask, s, NEG_INF)
    m = jnp.max(s, axis=axis, keepdims=True)
    m = jnp.where(m == NEG_INF, 0.0, m)
    p = jnp.exp(s - m)
    d = jnp.sum(p, axis=axis, keepdims=True)
    return p / jnp.where(d > 0, d, 1.0)


def _rmsnorm_body(x_ref, g_ref, o_ref):
    x = x_ref[...]
    ms = jnp.mean(x * x, axis=-1, keepdims=True)
    o_ref[...] = (x * lax.rsqrt(ms + NORM_EPS) * g_ref[...]).astype(o_ref.dtype)


def _rmsnorm(x, g, tm=256):
    m, d = x.shape
    return pl.pallas_call(
        _rmsnorm_body,
        grid=(pl.cdiv(m, tm),),
        in_specs=[pl.BlockSpec((tm, d), lambda i: (i, 0)), pl.BlockSpec((1, d), lambda i: (0, 0))],
        out_specs=pl.BlockSpec((tm, d), lambda i: (i, 0)),
        out_shape=jax.ShapeDtypeStruct((m, d), BF),
        compiler_params=_cparams(("parallel",)),
        name="rmsnorm",
    )(x, g.reshape(1, d))


def _gmm_body(texp_ref, tval_ref, *refs, n_w, epi, has_resid, tn):
    x_ref = refs[0]
    w_refs = refs[1:1 + n_w]
    pos = 1 + n_w
    if epi == "headnorm":
        gain_ref, flag_ref = refs[pos], refs[pos + 1]
        pos += 2
    if has_resid:
        resid_ref = refs[pos]
        pos += 1
    o_ref = refs[pos]
    wb_refs = refs[pos + 1:pos + 1 + n_w]

    i = pl.program_id(1)
    changed = jnp.logical_or(i == 0, texp_ref[i] != texp_ref[jnp.maximum(i - 1, 0)])

    @pl.when(changed)
    def _():
        for k in range(n_w):
            wb_refs[k][...] = w_refs[k][...].astype(BF)

    @pl.when(tval_ref[i] == 0)
    def _():
        o_ref[...] = jnp.zeros(o_ref.shape, o_ref.dtype)

    @pl.when(tval_ref[i] > 0)
    def _():
        x = x_ref[...]
        a = jnp.dot(x, wb_refs[0][...], preferred_element_type=F32)
        if epi == "swiglu":
            b = jnp.dot(x, wb_refs[1][...], preferred_element_type=F32)
            y = a * jax.nn.sigmoid(a) * b
        elif epi == "glu":
            b = jnp.dot(x, wb_refs[1][...], preferred_element_type=F32)
            y = a * jax.nn.sigmoid(b)
        else:
            y = a
        if has_resid:
            y = y + resid_ref[...]
        if epi == "headnorm":
            for c in range(tn // LANE):
                sl = slice(c * LANE, (c + 1) * LANE)
                yc = y[:, sl]
                r = lax.rsqrt(jnp.mean(yc * yc, axis=-1, keepdims=True) + NORM_EPS)
                f = flag_ref[:, sl]
                o_ref[:, sl] = (yc * (f * r + (1.0 - f)) * gain_ref[:, sl]).astype(o_ref.dtype)
        else:
            o_ref[...] = y.astype(o_ref.dtype)


def _gmm(x, ws, n_out, *, epi="none", texp=None, tval=None, resid=None, gain=None, flag=None,
         out_dtype=F32, tm=256, tn=512, name="gmm"):
    m, kdim = x.shape
    n_m = pl.cdiv(m, tm)
    n_n = n_out // tn
    assert n_n * tn == n_out
    if texp is None:
        texp = jnp.zeros((n_m,), jnp.int32)
        tval = jnp.ones((n_m,), jnp.int32)
    n_w = len(ws)
    in_specs = [pl.BlockSpec((tm, kdim), lambda j, i, te, tv: (i, 0))]
    args = [x]
    for w, off in ws:
        assert off % tn == 0 and w.shape[1] == kdim
        ob = off // tn
        in_specs.append(pl.BlockSpec((None, kdim, tn), lambda j, i, te, tv, ob=ob: (te[i], 0, j + ob)))
        args.append(w)
    if epi == "headnorm":
        in_specs += [pl.BlockSpec((1, tn), lambda j, i, te, tv: (0, j))] * 2
        args += [gain, flag]
    if resid is not None:
        in_specs.append(pl.BlockSpec((tm, tn), lambda j, i, te, tv: (i, j)))
        args.append(resid)
    body = functools.partial(_gmm_body, n_w=n_w, epi=epi, has_resid=resid is not None, tn=tn)
    return pl.pallas_call(
        body,
        grid_spec=pltpu.PrefetchScalarGridSpec(
            num_scalar_prefetch=2,
            grid=(n_n, n_m),
            in_specs=in_specs,
            out_specs=pl.BlockSpec((tm, tn), lambda j, i, te, tv: (i, j)),
            scratch_shapes=[pltpu.VMEM((kdim, tn), BF) for _ in range(n_w)],
        ),
        out_shape=jax.ShapeDtypeStruct((m, n_out), out_dtype),
        compiler_params=_cparams(("arbitrary", "arbitrary")),
        name=name,
    )(texp, tval, *args)


def _c1_body(*refs, n_in, rpb, kvh, n_prefetch):
    refs = refs[n_prefetch:]
    x_refs = refs[:n_in]
    w_ref = refs[n_in]
    o_ref = refs[n_in + 1]
    nsb = rpb // NSA_CMP_STRIDE
    acc = jnp.zeros((n_in * nsb * kvh, 2 * NSA_CMP_HIDDEN), F32)
    for s in range(NSA_CMP_STRIDE):
        parts = []
        for xr in x_refs:
            if kvh == 1:
                parts.append(xr[pl.ds(s, nsb, stride=NSA_CMP_STRIDE), :])
            else:
                parts.append(xr[pl.ds(s, nsb, stride=NSA_CMP_STRIDE), :, :].reshape(nsb * kvh, HEAD_DIM))
        xs = parts[0] if n_in == 1 else jnp.concatenate(parts, axis=0)
        acc = acc + jnp.dot(xs.astype(BF), w_ref[s], preferred_element_type=F32)
    if kvh == 1:
        o_ref[...] = acc
    else:
        scr = refs[n_in + 2]
        for c in range(scr.shape[0]):
            scr[c] = acc[:, c * LANE:(c + 1) * LANE]
        for g in range(kvh):
            for c in range(scr.shape[0]):
                o_ref[g, :, c * LANE:(c + 1) * LANE] = scr[c, pl.ds(g, n_in * nsb, stride=kvh), :]


def _compress_c1_prompt(y, wcat, nb, t):
    nsub = t // NSA_CMP_STRIDE
    body = functools.partial(_c1_body, n_in=1, rpb=t, kvh=1, n_prefetch=0)
    return pl.pallas_call(
        body,
        grid=(nb, 2, NSA_KV),
        in_specs=[pl.BlockSpec((t, LANE), lambda b, kv, g: (b, CH_KC + 2 * kv + g)),
                  pl.BlockSpec((None, NSA_CMP_STRIDE, HEAD_DIM, 2 * NSA_CMP_HIDDEN), lambda b, kv, g: (kv, 0, 0, 0))],
        out_specs=pl.BlockSpec((None, None, None, nsub, 2 * NSA_CMP_HIDDEN), lambda b, kv, g: (b, kv, g, 0, 0)),
        out_shape=jax.ShapeDtypeStruct((nb, 2, NSA_KV, nsub, 2 * NSA_CMP_HIDDEN), F32),
        compiler_params=_cparams(("parallel", "parallel", "parallel")),
        name="cmp_c1_prompt",
    )(y, wcat)


C1_PAGES = 16


def _compress_c1_sample(cache, page_table, wcat):
    nb, n_pages = page_table.shape
    page = cache.shape[2]
    npp = min(C1_PAGES, n_pages)
    assert n_pages % npp == 0
    nsb = page // NSA_CMP_STRIDE
    nsub = n_pages * nsb
    body = functools.partial(_c1_body, n_in=npp, rpb=page, kvh=NSA_KV, n_prefetch=1)
    in_specs = [
        pl.BlockSpec((None, None, page, None, NSA_KV, HEAD_DIM),
                     lambda b, kv, j, pt, i=i: (0, pt[b, j * npp + i], 0, kv, 0, 0))
        for i in range(npp)
    ]
    in_specs.append(pl.BlockSpec((None, NSA_CMP_STRIDE, HEAD_DIM, 2 * NSA_CMP_HIDDEN), lambda b, kv, j, pt: (kv, 0, 0, 0)))
    return pl.pallas_call(
        body,
        grid_spec=pltpu.PrefetchScalarGridSpec(
            num_scalar_prefetch=1,
            grid=(nb, 2, n_pages // npp),
            in_specs=in_specs,
            out_specs=pl.BlockSpec((None, None, NSA_KV, npp * nsb, 2 * NSA_CMP_HIDDEN), lambda b, kv, j, pt: (b, kv, 0, j, 0)),
            scratch_shapes=[pltpu.VMEM((2 * NSA_CMP_HIDDEN // LANE, npp * nsb * NSA_KV, LANE), F32)],
        ),
        out_shape=jax.ShapeDtypeStruct((nb, 2, NSA_KV, nsub, 2 * NSA_CMP_HIDDEN), F32),
        compiler_params=_cparams(("parallel", "parallel", "parallel")),
        name="cmp_c1_sample",
    )(page_table, *([cache] * npp), wcat)


def _c2_body(p_ref, pe_ref, wcat_ref, w2_ref, g_ref, o_ref, *, nsub, n_c):
    kv = pl.program_id(1)
    pep = jnp.zeros((SUBLANE, 2 * NSA_CMP_HIDDEN), F32)
    for s in range(NSA_CMP_STRIDE):
        pep = pep + jnp.dot(pe_ref[s], wcat_ref[s], preferred_element_type=F32)
    bias = pep[0:1, :NSA_CMP_HIDDEN] + pep[1:2, NSA_CMP_HIDDEN:]
    nxt = pltpu.roll(p_ref[:, NSA_CMP_HIDDEN:], nsub - 1, 0)
    h = p_ref[:, :NSA_CMP_HIDDEN] + nxt + bias
    h = h * jax.nn.sigmoid(h)
    o = jnp.dot(h.astype(BF), w2_ref[...], preferred_element_type=F32)
    r = lax.rsqrt(jnp.mean(o * o, axis=-1, keepdims=True) + NORM_EPS)
    o = jnp.where(kv == 0, o * r * g_ref[...], o)
    row = lax.broadcasted_iota(jnp.int32, (nsub, 1), 0)
    o_ref[...] = jnp.where(row < n_c, o, 0.0)


def _compress_c2(p, pe_rows, wcat, w2, gk):
    nb, _, _, nsub, _ = p.shape
    n_c = nsub - NSA_CMP_LEN // NSA_CMP_STRIDE + 1
    body = functools.partial(_c2_body, nsub=nsub, n_c=n_c)
    return pl.pallas_call(
        body,
        grid=(nb, 2, NSA_KV),
        in_specs=[pl.BlockSpec((None, None, None, nsub, 2 * NSA_CMP_HIDDEN), lambda b, kv, g: (b, kv, g, 0, 0)),
                  pl.BlockSpec((None, NSA_CMP_STRIDE, SUBLANE, HEAD_DIM), lambda b, kv, g: (kv, 0, 0, 0)),
                  pl.BlockSpec((None, NSA_CMP_STRIDE, HEAD_DIM, 2 * NSA_CMP_HIDDEN), lambda b, kv, g: (kv, 0, 0, 0)),
                  pl.BlockSpec((None, NSA_CMP_HIDDEN, HEAD_DIM), lambda b, kv, g: (kv, 0, 0)),
                  pl.BlockSpec((1, HEAD_DIM), lambda b, kv, g: (0, 0))],
        out_specs=pl.BlockSpec((None, None, None, nsub, HEAD_DIM), lambda b, kv, g: (b, kv, g, 0, 0)),
        out_shape=jax.ShapeDtypeStruct((nb, 2, NSA_KV, nsub, HEAD_DIM), F32),
        compiler_params=_cparams(("parallel", "parallel", "parallel")),
        name="cmp_c2",
    )(p, pe_rows, wcat, w2, gk)


def _rank_select(sc_ref, n_iter, k):
    score = sc_ref[...]
    brow = lax.broadcasted_iota(jnp.int32, score.shape, 0)

    def body(j, rank):
        r = sc_ref[pl.ds(j, 1), :]
        beats = jnp.logical_or(r > score, jnp.logical_and(r == score, j < brow))
        return rank + jnp.where(beats, 1.0, 0.0)

    rank = lax.fori_loop(0, n_iter, body, jnp.zeros(score.shape, F32))
    return rank < k


def _cattn_body(q_ref, kc_ref, vc_ref, oc_ref, sel_ref, sc_ref, *, tq, ncp, n_c, ns, nsp, pos0, ksel):
    t0 = pos0 + pl.program_id(2) * tq
    scale = HEAD_DIM ** -0.5
    kc = kc_ref[...].astype(BF)
    vc = vc_ref[...].astype(BF)
    trow = t0 + lax.broadcasted_iota(jnp.int32, (tq, 1), 0)
    ncol = lax.broadcasted_iota(jnp.int32, (1, ncp), 1)
    valid = jnp.logical_and(ncol * NSA_CMP_STRIDE + (NSA_CMP_LEN - 1) <= trow, ncol < n_c)
    tcol = t0 + lax.broadcasted_iota(jnp.int32, (1, tq), 1)
    nrow = lax.broadcasted_iota(jnp.int32, (ncp, 1), 0)
    valid_t = jnp.logical_and(nrow * NSA_CMP_STRIDE + (NSA_CMP_LEN - 1) <= tcol, nrow < n_c)
    psum_t = jnp.zeros((ncp, tq), F32)
    for z in range(NSA_HPG):
        sl = slice(z * HEAD_DIM, (z + 1) * HEAD_DIM)
        q = (q_ref[:, sl] * scale).astype(BF)
        s = lax.dot_general(q, kc, NT_DIMS, preferred_element_type=F32)
        p = _masked_softmax(s, valid, -1)
        oc_ref[:, sl] = jnp.dot(p.astype(BF), vc, preferred_element_type=F32)
        s_t = lax.dot_general(kc, q, NT_DIMS, preferred_element_type=F32)
        psum_t = psum_t + _masked_softmax(s_t, valid_t, 0)
    r = NSA_SEL_BLOCK // NSA_CMP_STRIDE
    brow = lax.broadcasted_iota(jnp.int32, (nsp, 1), 0)
    lo = r * brow - 1
    inside = jnp.logical_and(ncol >= lo, ncol <= lo + r)
    edge = jnp.logical_or(ncol == lo, ncol == lo + r)
    m_t = jnp.where(inside, jnp.where(edge, 0.5, 1.0), 0.0).astype(BF)
    hi = psum_t.astype(BF)
    mid = (psum_t - hi.astype(F32)).astype(BF)
    low = (psum_t - hi.astype(F32) - mid.astype(F32)).astype(BF)
    imp_t = (jnp.dot(m_t, hi, preferred_element_type=F32) + jnp.dot(m_t, mid, preferred_element_type=F32)
             + jnp.dot(m_t, low, preferred_element_type=F32))
    own = tcol // NSA_SEL_BLOCK
    past = jnp.logical_and(brow < own, brow < ns)
    forced = jnp.logical_or(brow == 0, brow == own - 1)
    sc_ref[...] = jnp.where(past, jnp.where(forced, NSA_SEL_FORCE, imp_t), NEG_INF)
    chosen = jnp.logical_and(past, _rank_select(sc_ref, ns, ksel))
    sel = jnp.where(jnp.logical_or(chosen, brow == own), 1.0, 0.0)
    rows = sel_ref.shape[0]
    if rows > nsp:
        sel_ref[...] = jnp.zeros(sel_ref.shape, F32)
    sel_ref[0:nsp, :] = sel


def _cattn(q_arr, q_row_blk0, q_col_blk0, kvc, nb, tlen, tq, length, pos0, out_rows):
    nq = tlen // tq
    ncp = kvc.shape[3]
    n_c = ncp - NSA_CMP_LEN // NSA_CMP_STRIDE + 1
    ns = length // NSA_SEL_BLOCK
    ksel = min(NSA_SEL_COUNT - 1, ns)
    nsp = -(-ns // SUBLANE) * SUBLANE
    sel_rows = max(nsp, LANE)
    body = functools.partial(_cattn_body, tq=tq, ncp=ncp, n_c=n_c, ns=ns, nsp=nsp, pos0=pos0, ksel=ksel)
    gw = NSA_HPG * HEAD_DIM
    return pl.pallas_call(
        body,
        grid=(nb, NSA_KV, nq),
        in_specs=[pl.BlockSpec((tq, gw), lambda b, g, qi: (q_row_blk0 + b * nq + qi, q_col_blk0 + g)),
                  pl.BlockSpec((None, None, None, ncp, HEAD_DIM), lambda b, g, qi: (b, 0, g, 0, 0)),
                  pl.BlockSpec((None, None, None, ncp, HEAD_DIM), lambda b, g, qi: (b, 1, g, 0, 0))],
        out_specs=[pl.BlockSpec((tq, gw), lambda b, g, qi: (b * nq + qi, g)),
                   pl.BlockSpec((None, None, sel_rows, tq), lambda b, g, qi: (b, g, 0, qi))],
        out_shape=[jax.ShapeDtypeStruct((out_rows, NSA_KV * gw), F32),
                   jax.ShapeDtypeStruct((nb, NSA_KV, sel_rows, tlen), F32)],
        scratch_shapes=[pltpu.VMEM((nsp, tq), F32)],
        compiler_params=_cparams(("parallel", "parallel", "parallel")),
        name="nsa_cmp_attn",
    )(q_arr, kvc, kvc)


def _gate_body(q_ref, k_ref, sel_ref, mean_ref, sc_ref, *, tq, nb, nbp, pos0, from_means):
    qi = pl.program_id(2)
    if from_means:
        means = k_ref[...]
    else:
        @pl.when(qi == 0)
        def _():
            mean_ref[...] = jnp.zeros(mean_ref.shape, F32)
            for j in range(nb):
                blk = k_ref[j * MOBA_BLOCK:(j + 1) * MOBA_BLOCK, :]
                mean_ref[j:j + 1, :] = jnp.sum(blk, axis=0, keepdims=True) * (1.0 / MOBA_BLOCK)

        means = mean_ref[...]
    g_t = _dot3(means, q_ref[...], NT_DIMS)
    tcol = pos0 + qi * tq + lax.broadcasted_iota(jnp.int32, (1, tq), 1)
    own = tcol // MOBA_BLOCK
    brow = lax.broadcasted_iota(jnp.int32, (nbp, 1), 0)
    past = jnp.logical_and(brow < own, brow < nb)
    sc_ref[...] = jnp.where(past, g_t, NEG_INF)
    chosen = jnp.logical_and(past, _rank_select(sc_ref, nb, min(MOBA_TOPK, nb)))
    sel = jnp.where(jnp.logical_or(chosen, brow == own), 1.0, 0.0)
    if sel_ref.shape[0] > nbp:
        sel_ref[...] = jnp.zeros(sel_ref.shape, F32)
    sel_ref[0:nbp, :] = sel


def _moba_gate_prompt(y, nb, t, tq):
    nq = t // tq
    n_full = t // MOBA_BLOCK
    nbp = -(-n_full // SUBLANE) * SUBLANE
    body = functools.partial(_gate_body, tq=tq, nb=n_full, nbp=nbp, pos0=0, from_means=False)
    return pl.pallas_call(
        body,
        grid=(nb, MOBA_HEADS, nq),
        in_specs=[pl.BlockSpec((tq, LANE), lambda b, h, qi: (b * nq + qi, CH_QB + h)),
                  pl.BlockSpec((t, LANE), lambda b, h, qi: (b, CH_KB + h))],
        out_specs=pl.BlockSpec((None, None, LANE, tq), lambda b, h, qi: (b, h, 0, qi)),
        out_shape=jax.ShapeDtypeStruct((nb, MOBA_HEADS, LANE, t), F32),
        scratch_shapes=[pltpu.VMEM((nbp, HEAD_DIM), F32), pltpu.VMEM((nbp, tq), F32)],
        compiler_params=_cparams(("parallel", "parallel", "arbitrary")),
        name="moba_gate_prompt",
    )(y, y)


def _moba_gate_sample(q_pad, means, pos0):
    nb, _, n_full, _ = means.shape
    tq = SUBLANE
    body = functools.partial(_gate_body, tq=tq, nb=n_full, nbp=n_full, pos0=pos0, from_means=True)
    return pl.pallas_call(
        body,
        grid=(nb, MOBA_HEADS, 1),
        in_specs=[pl.BlockSpec((tq, LANE), lambda b, h, qi: (b, h)),
                  pl.BlockSpec((None, None, n_full, HEAD_DIM), lambda b, h, qi: (b, h, 0, 0))],
        out_specs=pl.BlockSpec((None, None, max(n_full, LANE), tq), lambda b, h, qi: (b, h, 0, 0)),
        out_shape=jax.ShapeDtypeStruct((nb, MOBA_HEADS, max(n_full, LANE), tq), F32),
        scratch_shapes=[pltpu.VMEM((SUBLANE, HEAD_DIM), F32), pltpu.VMEM((n_full, tq), F32)],
        compiler_params=_cparams(("parallel", "parallel", "arbitrary")),
        name="moba_gate_sample",
    )(q_pad, means)


def _means_body(pt_ref, c_ref, o_ref, acc_ref, *, ppb):
    j = pl.program_id(1)

    @pl.when(j % ppb == 0)
    def _():
        acc_ref[...] = jnp.zeros(acc_ref.shape, F32)

    acc_ref[...] += jnp.sum(c_ref[...], axis=0)

    @pl.when(j % ppb == ppb - 1)
    def _():
        m = acc_ref[...] * (1.0 / MOBA_BLOCK)
        blk = j // ppb
        for h in range(MOBA_HEADS):
            o_ref[h, pl.ds(blk, 1), :] = m[h:h + 1, :]


def _moba_means_sample(cache, page_table):
    nb, n_pages = page_table.shape
    page = cache.shape[2]
    ppb = MOBA_BLOCK // page
    n_full = n_pages // ppb
    body = functools.partial(_means_body, ppb=ppb)
    return pl.pallas_call(
        body,
        grid_spec=pltpu.PrefetchScalarGridSpec(
            num_scalar_prefetch=1,
            grid=(nb, n_full * ppb),
            in_specs=[pl.BlockSpec((None, None, page, None, MOBA_HEADS, HEAD_DIM), lambda b, j, pt: (0, pt[b, j], 0, 0, 0, 0))],
            out_specs=pl.BlockSpec((None, MOBA_HEADS, n_full, HEAD_DIM), lambda b, j, pt: (b, 0, 0, 0)),
            scratch_shapes=[pltpu.VMEM((MOBA_HEADS, HEAD_DIM), F32)],
        ),
        out_shape=jax.ShapeDtypeStruct((nb, MOBA_HEADS, n_full, HEAD_DIM), F32),
        compiler_params=_cparams(("parallel", "arbitrary")),
        name="moba_means_sample",
    )(page_table, cache)


def _flash_body(*refs, tq, tk, nh, blk, window, masked):
    q_ref, k_ref, v_ref = refs[:3]
    pos = 3
    if masked:
        sel_ref = refs[pos]
        pos += 1
    o_ref, qs_ref, m_ref, l_ref, acc_ref = refs[pos:pos + 5]
    t0 = pl.program_id(2) * tq
    scale = HEAD_DIM ** -0.5
    for z in range(nh):
        qs_ref[z] = (q_ref[:, z * HEAD_DIM:(z + 1) * HEAD_DIM] * scale).astype(BF)
    m_ref[...] = jnp.full(m_ref.shape, NEG_INF, F32)
    l_ref[...] = jnp.zeros(l_ref.shape, F32)
    acc_ref[...] = jnp.zeros(acc_ref.shape, F32)
    if masked:
        selb = jnp.transpose(sel_ref[...]).astype(BF)
    trow = t0 + lax.broadcasted_iota(jnp.int32, (tq, 1), 0)
    c_hi = (t0 + tq + tk - 1) // tk
    c_lo = jnp.maximum(t0 - window + 1, 0) // tk if window else 0

    def chunk(c, carry):
        ks = pl.multiple_of(c * tk, tk)
        kc = k_ref[pl.ds(ks, tk), :].astype(BF)
        vc = v_ref[pl.ds(ks, tk), :].astype(BF)
        pcol = ks + lax.broadcasted_iota(jnp.int32, (1, tk), 1)
        valid = pcol <= trow
        if window:
            valid = jnp.logical_and(valid, pcol > trow - window)
        if masked:
            jrow = lax.broadcasted_iota(jnp.int32, (LANE, 1), 0)
            expand = jnp.where(pcol // blk == jrow, 1.0, 0.0).astype(BF)
            valid = jnp.logical_and(valid, jnp.dot(selb, expand, preferred_element_type=F32) > 0.5)
        for z in range(nh):
            s = lax.dot_general(qs_ref[z], kc, NT_DIMS, preferred_element_type=F32)
            s = jnp.where(valid, s, NEG_INF)
            m_old = m_ref[z]
            m_new = jnp.maximum(m_old, jnp.max(s, axis=-1, keepdims=True))
            m_safe = jnp.where(m_new == NEG_INF, 0.0, m_new)
            alpha = jnp.exp(m_old - m_safe)
            p = jnp.exp(s - m_safe)
            l_ref[z] = alpha * l_ref[z] + jnp.sum(p, axis=-1, keepdims=True)
            acc_ref[z] = alpha * acc_ref[z] + jnp.dot(p.astype(BF), vc, preferred_element_type=F32)
            m_ref[z] = m_new
        return carry

    lax.fori_loop(c_lo, c_hi, chunk, 0)
    for z in range(nh):
        l = l_ref[z]
        o_ref[:, z * HEAD_DIM:(z + 1) * HEAD_DIM] = acc_ref[z] / jnp.where(l > 0, l, 1.0)


def _flash(y, sel, nb, t, *, q_ch, k_ch, v_ch, kvh, nh, blk, window, out_rows, tq=256, tk=512):
    tq = min(tq, t)
    tk = min(tk, t)
    nq = t // tq
    masked = sel is not None
    gw = nh * HEAD_DIM
    assert q_ch % nh == 0
    in_specs = [pl.BlockSpec((tq, gw), lambda b, h, qi: (b * nq + qi, q_ch // nh + h)),
                pl.BlockSpec((t, LANE), lambda b, h, qi: (b, k_ch + h)),
                pl.BlockSpec((t, LANE), lambda b, h, qi: (b, v_ch + h))]
    args = [y, y, y]
    if masked:
        in_specs.append(pl.BlockSpec((None, None, LANE, tq), lambda b, h, qi: (b, h, 0, qi)))
        args.append(sel)
    body = functools.partial(_flash_body, tq=tq, tk=tk, nh=nh, blk=blk, window=window, masked=masked)
    return pl.pallas_call(
        body,
        grid=(nb, kvh, nq),
        in_specs=in_specs,
        out_specs=pl.BlockSpec((tq, gw), lambda b, h, qi: (b * nq + qi, h)),
        out_shape=jax.ShapeDtypeStruct((out_rows, kvh * gw), F32),
        scratch_shapes=[pltpu.VMEM((nh, tq, HEAD_DIM), BF), pltpu.VMEM((nh, tq, 1), F32),
                        pltpu.VMEM((nh, tq, 1), F32), pltpu.VMEM((nh, tq, HEAD_DIM), F32)],
        compiler_params=_cparams(("parallel", "parallel", "parallel")),
        name="flash_" + ("win" if window else "blk%d" % blk),
    )(*args)


def _dec_body(tbl_ref, vld_ref, q_ref, k_ref, v_ref, kn_ref, vn_ref, o_ref, m_ref, l_ref, acc_ref,
              *, nh, rows, kvh, n_steps, r_min):
    b = pl.program_id(0)
    h = pl.program_id(1)
    j = pl.program_id(2)
    scale = HEAD_DIM ** -0.5
    qrow = q_ref[pl.ds(b, 1), :] * scale
    zrow = lax.broadcasted_iota(jnp.int32, (SUBLANE, 1), 0)
    qm = jnp.zeros((SUBLANE, HEAD_DIM), F32)
    for z in range(nh):
        qm = jnp.where(zrow == z, qrow[:, z * HEAD_DIM:(z + 1) * HEAD_DIM], qm)

    @pl.when(j == 0)
    def _():
        m_ref[...] = jnp.full(m_ref.shape, NEG_INF, F32)
        l_ref[...] = jnp.zeros(l_ref.shape, F32)
        acc_ref[...] = jnp.zeros(acc_ref.shape, F32)

    @pl.when(vld_ref[b, h, j] > 0)
    def _():
        kk = k_ref[...].reshape(rows * kvh, HEAD_DIM).astype(BF)
        vv = v_ref[...].reshape(rows * kvh, HEAD_DIM).astype(BF)
        s = lax.dot_general(qm.astype(BF), kk, NT_DIMS, preferred_element_type=F32)
        col = lax.broadcasted_iota(jnp.int32, (1, rows * kvh), 1)
        valid = jnp.logical_and(col % kvh == h, col // kvh >= r_min)
        s = jnp.where(valid, s, NEG_INF)
        m_old = m_ref[...]
        m_new = jnp.maximum(m_old, jnp.max(s, axis=-1, keepdims=True))
        m_safe = jnp.where(m_new == NEG_INF, 0.0, m_new)
        alpha = jnp.exp(m_old - m_safe)
        p = jnp.exp(s - m_safe)
        l_ref[...] = alpha * l_ref[...] + jnp.sum(p, axis=-1, keepdims=True)
        acc_ref[...] = alpha * acc_ref[...] + jnp.dot(p.astype(BF), vv, preferred_element_type=F32)
        m_ref[...] = m_new

    @pl.when(j == n_steps - 1)
    def _():
        kn = kn_ref[pl.ds(b, 1), :]
        vn = vn_ref[pl.ds(b, 1), :]
        s = jnp.sum(qm * kn, axis=-1, keepdims=True)
        m_old = m_ref[...]
        m_new = jnp.maximum(m_old, s)
        alpha = jnp.exp(m_old - m_new)
        p = jnp.exp(s - m_new)
        l = alpha * l_ref[...] + p
        o_ref[...] = (alpha * acc_ref[...] + p * vn) / l


def _decode_attn(ys, cache, li_fixed, tbl, vld, *, q_ch, kn_ch, vn_ch, nh, rows, kvh, r_min, name):
    nb, _, n_steps = tbl.shape
    page = cache.shape[2]
    bpp = page // rows
    gw = nh * HEAD_DIM
    body = functools.partial(_dec_body, nh=nh, rows=rows, kvh=kvh, n_steps=n_steps, r_min=r_min)

    def kv_spec(kv):
        return pl.BlockSpec((None, None, rows, None, kvh, HEAD_DIM),
                            lambda b, h, j, tb, vl: (li_fixed, tb[b, h, j] // bpp, tb[b, h, j] % bpp, kv, 0, 0))

    nrow = ys.shape[0]
    return pl.pallas_call(
        body,
        grid_spec=pltpu.PrefetchScalarGridSpec(
            num_scalar_prefetch=2,
            grid=(nb, kvh, n_steps),
            in_specs=[pl.BlockSpec((nrow, gw), lambda b, h, j, tb, vl: (0, q_ch // nh + h)),
                      kv_spec(0), kv_spec(1),
                      pl.BlockSpec((nrow, LANE), lambda b, h, j, tb, vl: (0, kn_ch + h)),
                      pl.BlockSpec((nrow, LANE), lambda b, h, j, tb, vl: (0, vn_ch + h))],
            out_specs=pl.BlockSpec((None, None, SUBLANE, HEAD_DIM), lambda b, h, j, tb, vl: (b, h, 0, 0)),
            scratch_shapes=[pltpu.VMEM((SUBLANE, 1), F32), pltpu.VMEM((SUBLANE, 1), F32), pltpu.VMEM((SUBLANE, HEAD_DIM), F32)],
        ),
        out_shape=jax.ShapeDtypeStruct((nb, kvh, SUBLANE, HEAD_DIM), F32),
        compiler_params=_cparams(("parallel", "parallel", "arbitrary")),
        name=name,
    )(tbl, vld, ys, cache, cache, ys, ys)


def _mix_body(oc_ref, os_ref, ow_ref, ob_ref, soc_ref, sos_ref, sow_ref, sob_ref, gate_ref, o_ref, *, n_p_tiles, n_s):
    i = pl.program_id(0)
    w = NSA_HEADS * HEAD_DIM

    def emit(oc, os_, ow, ob, rows):
        g = jax.nn.sigmoid(gate_ref[0:rows, :])
        for h in range(NSA_HEADS):
            sl = slice(h * HEAD_DIM, (h + 1) * HEAD_DIM)
            o = (g[:, 3 * h:3 * h + 1] * oc[:, sl] + g[:, 3 * h + 1:3 * h + 2] * os_[:, sl]
                 + g[:, 3 * h + 2:3 * h + 3] * ow[:, sl])
            o_ref[0:rows, sl] = o.astype(o_ref.dtype)
        o_ref[0:rows, w:] = ob[...].astype(o_ref.dtype)

    @pl.when(i < n_p_tiles)
    def _():
        emit(oc_ref, os_ref, ow_ref, ob_ref, o_ref.shape[0])

    @pl.when(i >= n_p_tiles)
    def _():
        emit(soc_ref, sos_ref, sow_ref, sob_ref, n_s)


def _mix(prompt_parts, sample_parts, y, tm=256):
    n_prompt = prompt_parts[0].shape[0]
    n_s = sample_parts[0].shape[0]
    m = y.shape[0]
    tm = min(tm, n_prompt)
    assert n_prompt % tm == 0 and m == n_prompt + n_s and n_s <= tm
    n_p_tiles = n_prompt // tm
    wa = NSA_HEADS * HEAD_DIM
    wb = MOBA_HEADS * HEAD_DIM
    pmap = lambda i: (jnp.minimum(i, n_p_tiles - 1), 0)
    body = functools.partial(_mix_body, n_p_tiles=n_p_tiles, n_s=n_s)
    return pl.pallas_call(
        body,
        grid=(n_p_tiles + 1,),
        in_specs=[pl.BlockSpec((tm, wa), pmap)] * 3 + [pl.BlockSpec((tm, wb), pmap)]
        + [pl.BlockSpec((n_s, wa), lambda i: (0, 0))] * 3 + [pl.BlockSpec((n_s, wb), lambda i: (0, 0))]
        + [pl.BlockSpec((tm, LANE), lambda i: (i, CH_GATE))],
        out_specs=pl.BlockSpec((tm, wa + wb), lambda i: (i, 0)),
        out_shape=jax.ShapeDtypeStruct((m, wa + wb), BF),
        compiler_params=_cparams(("parallel",)),
        name="attn_mix",
    )(*prompt_parts, *sample_parts, y)


def _convmix_body(up_ref, hp_ref, uc_ref, hc_ref, pw_ref, ps_ref, dw_ref, dwb_ref, lng_ref, lnb_ref, pww_ref,
                  o_ref, zp_ref, zc_ref, cb_ref, *, ts, nt, zero_first, avail0, rc):
    s = pl.program_id(0)
    hp_rows = hp_ref.shape[0]
    hc_rows = hc_ref.shape[0]
    width = up_ref.shape[1]
    if zero_first:
        keep = jnp.where(s % nt == 0, 0.0, 1.0)
        zp_ref[0:hp_rows, :] = hp_ref[...] * keep
        zc_ref[0:hc_rows, :] = hc_ref[...] * keep
        avail = (s % nt) * ts + avail0
    else:
        zp_ref[0:hp_rows, :] = hp_ref[...]
        zc_ref[0:hc_rows, :] = hc_ref[...]
        avail = avail0
    zp_ref[hp_rows:, :] = up_ref[...]
    zc_ref[hc_rows:, :] = uc_ref[...]
    gwidth = width // len(POOL_WINDOWS)
    t_idx = lax.broadcasted_iota(jnp.int32, (ts, 1), 0) + avail + 1
    for gi, w in enumerate(POOL_WINDOWS):
        sl = slice(gi * gwidth, (gi + 1) * gwidth)
        cur = zp_ref[hp_rows:hp_rows + ts, sl]
        acc = cur
        for jj in range(1, w):
            acc = acc + zp_ref[hp_rows - jj:hp_rows - jj + ts, sl]
        cnt = jnp.minimum(t_idx, w).astype(F32)
        yg = acc / cnt - cur
        og = jnp.dot(yg.astype(BF), pw_ref[gi], preferred_element_type=F32) * ps_ref[:, sl]
        o_ref[:, sl] = og.astype(o_ref.dtype)
    base = hc_rows - (CONV_WIDTH - 1)
    for r in range(ts // rc):
        for c in range(width // LANE):
            cs = slice(c * LANE, (c + 1) * LANE)
            acc = jnp.zeros((rc, LANE), F32)
            for jj in range(CONV_WIDTH):
                lo = r * rc + base + jj
                acc = acc + zc_ref[lo:lo + rc, cs] * dw_ref[jj:jj + 1, cs]
            cb_ref[r * rc:(r + 1) * rc, cs] = acc + dwb_ref[:, cs]
    cv = cb_ref[...]
    mu = jnp.mean(cv, axis=-1, keepdims=True)
    xc = cv - mu
    yn = xc * lax.rsqrt(jnp.mean(xc * xc, axis=-1, keepdims=True) + NORM_EPS) * lng_ref[...] + lnb_ref[...]
    act = yn * jax.nn.sigmoid(yn)
    o_ref[:, width:] = jnp.dot(act.astype(BF), pww_ref[...], preferred_element_type=F32).astype(o_ref.dtype)


def _convmix(u_pool, halo_pool, u_conv, halo_conv, prm, *, ts, nt, n_tiles, zero_first, avail0, hp_rows, hc_rows,
             halo_blk):
    pw, ps, dw, dwb, lng, lnb, pww = prm
    width = u_pool.shape[1]
    rc = min(64, ts)
    body = functools.partial(_convmix_body, ts=ts, nt=nt, zero_first=zero_first, avail0=avail0, rc=rc)
    full = lambda a: pl.BlockSpec(a.shape, lambda s: (0,) * a.ndim)
    return pl.pallas_call(
        body,
        grid=(n_tiles,),
        in_specs=[pl.BlockSpec((ts, width), lambda s: (s, 0)),
                  pl.BlockSpec((hp_rows, width), lambda s: (halo_blk(s, hp_rows), 0)),
                  pl.BlockSpec((ts, width), lambda s: (s, 0)),
                  pl.BlockSpec((hc_rows, width), lambda s: (halo_blk(s, hc_rows), 0)),
                  full(pw), full(ps), full(dw), full(dwb), full(lng), full(lnb), full(pww)],
        out_specs=pl.BlockSpec((ts, 2 * width), lambda s: (s, 0)),
        out_shape=jax.ShapeDtypeStruct((n_tiles * ts, 2 * width), BF),
        scratch_shapes=[pltpu.VMEM((hp_rows + ts, width), F32), pltpu.VMEM((hc_rows + ts, width), F32),
                        pltpu.VMEM((ts, width), F32)],
        compiler_params=_cparams(("parallel",)),
        name="conv_mix",
    )(u_pool, halo_pool, u_conv, halo_conv, pw, ps, dw, dwb, lng, lnb, pww)


def _router_body(x_ref, g_ref, rw_ref, r_ref):
    x = x_ref[...]
    xn = x * lax.rsqrt(jnp.mean(x * x, axis=-1, keepdims=True) + NORM_EPS) * g_ref[...]
    logits = _dot3(xn, rw_ref[...])
    lane = lax.broadcasted_iota(jnp.int32, logits.shape, 1)
    l1 = jnp.where(lane < N_EXPERTS, logits, NEG_INF)
    m1 = jnp.max(l1, axis=-1, keepdims=True)
    i1 = jnp.min(jnp.where(l1 == m1, lane, LANE), axis=-1, keepdims=True)
    l2 = jnp.where(lane == i1, NEG_INF, l1)
    m2 = jnp.max(l2, axis=-1, keepdims=True)
    i2 = jnp.min(jnp.where(l2 == m2, lane, LANE), axis=-1, keepdims=True)
    e = jnp.exp(m2 - m1)
    w1 = 1.0 / (1.0 + e)
    w2 = e / (1.0 + e)
    r_ref[...] = jnp.where(lane == 0, i1.astype(F32),
                           jnp.where(lane == 1, i2.astype(F32), jnp.where(lane == 2, w1, jnp.where(lane == 3, w2, 0.0))))


def _router(x, g, rw, tm=256):
    m, d = x.shape
    rw_pad = jnp.pad(rw, ((0, 0), (0, LANE - rw.shape[1])))
    return pl.pallas_call(
        _router_body,
        grid=(pl.cdiv(m, tm),),
        in_specs=[pl.BlockSpec((tm, d), lambda i: (i, 0)), pl.BlockSpec((1, d), lambda i: (0, 0)),
                  pl.BlockSpec((d, LANE), lambda i: (0, 0))],
        out_specs=pl.BlockSpec((tm, LANE), lambda i: (i, 0)),
        out_shape=jax.ShapeDtypeStruct((m, LANE), F32),
        compiler_params=_cparams(("parallel",)),
        name="moe_router",
    )(x, g.reshape(1, d), rw_pad)


def _row_copy(src_hbm, row, dst, drow, sem):
    return pltpu.make_async_copy(src_hbm.at[pl.ds(row, 1), :], dst.at[pl.ds(drow, 1), :], sem)


def _gather_norm_body(src_ref, x_hbm, g_ref, o_ref, buf_ref, sem, *, tm):
    base = pl.program_id(0) * tm

    def issue(r, c):
        _row_copy(x_hbm, src_ref[base + r], buf_ref, r, sem).start()
        return c

    lax.fori_loop(0, tm, issue, 0)

    def wait(r, c):
        _row_copy(x_hbm, 0, buf_ref, r, sem).wait()
        return c

    lax.fori_loop(0, tm, wait, 0)
    x = buf_ref[...]
    ms = jnp.mean(x * x, axis=-1, keepdims=True)
    o_ref[...] = (x * lax.rsqrt(ms + NORM_EPS) * g_ref[...]).astype(o_ref.dtype)


def _gather_norm(x, src, g, tm):
    d = x.shape[1]
    r_tot = src.shape[0]
    body = functools.partial(_gather_norm_body, tm=tm)
    return pl.pallas_call(
        body,
        grid_spec=pltpu.PrefetchScalarGridSpec(
            num_scalar_prefetch=1,
            grid=(r_tot // tm,),
            in_specs=[pl.BlockSpec(memory_space=pl.ANY), pl.BlockSpec((1, d), lambda i, s: (0, 0))],
            out_specs=pl.BlockSpec((tm, d), lambda i, s: (i, 0)),
            scratch_shapes=[pltpu.VMEM((tm, d), F32), pltpu.SemaphoreType.DMA(())],
        ),
        out_shape=jax.ShapeDtypeStruct((r_tot, d), BF),
        compiler_params=_cparams(("arbitrary",)),
        name="moe_gather",
    )(src, x, g.reshape(1, d))


def _combine_body(p1_ref, p2_ref, eo_hbm, x_ref, r_ref, o_ref, b1_ref, b2_ref, sem, *, tm, n_tok):
    base = pl.program_id(0) * tm
    n = jnp.minimum(tm, n_tok - base)

    def issue(r, c):
        _row_copy(eo_hbm, p1_ref[base + r], b1_ref, r, sem).start()
        _row_copy(eo_hbm, p2_ref[base + r], b2_ref, r, sem).start()
        return c

    lax.fori_loop(0, n, issue, 0)

    def wait(r, c):
        _row_copy(eo_hbm, 0, b1_ref, r, sem).wait()
        _row_copy(eo_hbm, 0, b2_ref, r, sem).wait()
        return c

    lax.fori_loop(0, n, wait, 0)
    rr = r_ref[...]
    o_ref[...] = x_ref[...] + rr[:, 2:3] * b1_ref[...] + rr[:, 3:4] * b2_ref[...]


def _combine(x, eo, routing, p1, p2, tm_max=512):
    m, d = x.shape
    tm = max(t for t in range(SUBLANE, tm_max + 1, SUBLANE) if m % t == 0)
    body = functools.partial(_combine_body, tm=tm, n_tok=m)
    return pl.pallas_call(
        body,
        grid_spec=pltpu.PrefetchScalarGridSpec(
            num_scalar_prefetch=2,
            grid=(pl.cdiv(m, tm),),
            in_specs=[pl.BlockSpec(memory_space=pl.ANY), pl.BlockSpec((tm, d), lambda i, a, b: (i, 0)),
                      pl.BlockSpec((tm, LANE), lambda i, a, b: (i, 0))],
            out_specs=pl.BlockSpec((tm, d), lambda i, a, b: (i, 0)),
            scratch_shapes=[pltpu.VMEM((tm, d), F32), pltpu.VMEM((tm, d), F32), pltpu.SemaphoreType.DMA(())],
        ),
        out_shape=jax.ShapeDtypeStruct((m, d), F32),
        compiler_params=_cparams(("arbitrary",)),
        name="moe_combine",
    )(p1, p2, eo, x, routing)


def _moe_plan(e_idx, tm):
    n = e_idx.shape[0]
    n_asg = n * TOP_K
    flat_e = e_idx.reshape(-1)
    onehot = (flat_e[:, None] == jnp.arange(N_EXPERTS, dtype=jnp.int32)[None, :]).astype(jnp.int32)
    cnt = jnp.sum(onehot, axis=0)
    rank = jnp.take_along_axis(jnp.cumsum(onehot, axis=0) - onehot, flat_e[:, None], axis=1)[:, 0]
    cnt_p = ((cnt + tm - 1) // tm) * tm
    ends = jnp.cumsum(cnt_p)
    off = ends - cnt_p
    pos = off[flat_e] + rank
    n_tiles = (n_asg + N_EXPERTS * (tm - 1) + tm - 1) // tm
    r_tot = n_tiles * tm
    src = jnp.zeros((r_tot,), jnp.int32).at[pos].set(jnp.arange(n_asg, dtype=jnp.int32) // TOP_K)
    tile_start = jnp.arange(n_tiles, dtype=jnp.int32) * tm
    tval = (tile_start < ends[-1]).astype(jnp.int32)
    texp = jnp.minimum(jnp.searchsorted(ends, tile_start, side="right"), N_EXPERTS - 1).astype(jnp.int32)
    last = jnp.max(jnp.where(tval > 0, texp, 0))
    texp = jnp.where(tval > 0, texp, last)
    pos2 = pos.reshape(n, TOP_K)
    return src, pos2[:, 0], pos2[:, 1], texp, tval


def _moe(x, g, rw, wg, wu, wd, tm=256):
    routing = _router(x, g, rw)
    e_idx = routing[:, :TOP_K].astype(jnp.int32)
    src, p1, p2, texp, tval = _moe_plan(e_idx, tm)
    xs = _gather_norm(x, src, g, tm)
    hid = _gmm(xs, [(wg, 0), (wu, 0)], wg.shape[2], epi="swiglu", texp=texp, tval=tval, out_dtype=BF, tm=tm,
               name="moe_up")
    eo = _gmm(hid, [(wd, 0)], wd.shape[2], texp=texp, tval=tval, tm=tm, name="moe_down")
    return _combine(x, eo, routing, p1, p2)


def _attn_layer(x, n_prompt, nb_p, t_p, nb_s, li, prm, caches, page_table):
    (attn_norm, w_in, qk_g, cmp_pe, cmp_w1, cmp_w2, moba_g, w_out, ffn_norm, w_gate, w_up, w_down) = prm
    cache_cmp, cache_sel, state_win, cache_moba = caches
    d = x.shape[1]
    past_len = page_table.shape[1] * cache_moba.shape[2]

    n_in = w_in.shape[1]
    g_lo, g_hi = CH_QB * LANE, CH_QB * LANE + 3 * NSA_HEADS
    w_re = jnp.concatenate([w_in[:, :g_lo], w_in[:, g_hi:], w_in[:, g_lo:g_hi],
                            jnp.zeros((d, ATTN_CHUNKS * LANE - n_in), F32)], axis=1)[None]
    ones = jnp.ones((HEAD_DIM,), F32)
    zeros = jnp.zeros((HEAD_DIM,), F32)
    chunk_gain = ([qk_g[0]] * 8 + [ones] * 4 + [qk_g[2]] * 2 + [ones] * 2 + [qk_g[3]] * 2 + [ones] * 2
                  + [moba_g[0]] * 8 + [moba_g[1]] * 8 + [ones] * 9)
    chunk_flag = ([ones] * 8 + [zeros] * 4 + [ones] * 2 + [zeros] * 2 + [ones] * 2 + [zeros] * 2
                  + [ones] * 16 + [zeros] * 9)
    gain = jnp.concatenate(chunk_gain)[None]
    flag = jnp.concatenate(chunk_flag)[None]
    xn = _rmsnorm(x, attn_norm)
    y = _gmm(xn, [(w_re, 0)], ATTN_CHUNKS * LANE, epi="headnorm", gain=gain, flag=flag, tn=5 * LANE, name="attn_in")

    r = NSA_CMP_LEN // NSA_CMP_STRIDE
    wcat = cmp_w1.reshape(2, r, NSA_CMP_STRIDE, HEAD_DIM, NSA_CMP_HIDDEN).transpose(0, 2, 3, 1, 4)
    wcat = wcat.reshape(2, NSA_CMP_STRIDE, HEAD_DIM, r * NSA_CMP_HIDDEN).astype(BF)
    pe_rows = cmp_pe.reshape(2, r, NSA_CMP_STRIDE, HEAD_DIM).transpose(0, 2, 1, 3)
    pe_rows = jnp.pad(pe_rows, ((0, 0), (0, 0), (0, SUBLANE - r), (0, 0))).astype(BF)
    w2 = cmp_w2.astype(BF)
    gk = qk_g[1][None]

    kvc_p = _compress_c2(_compress_c1_prompt(y, wcat, nb_p, t_p), pe_rows, wcat, w2, gk)
    tq = min(256, t_p)
    rows = n_prompt
    oc, sel_a = _cattn(y, 0, 0, kvc_p, nb_p, t_p, tq, t_p, 0, rows)
    sel_b = _moba_gate_prompt(y, nb_p, t_p, tq)
    os_ = _flash(y, sel_a, nb_p, t_p, q_ch=CH_QA, k_ch=CH_KS, v_ch=CH_VS, kvh=NSA_KV, nh=NSA_HPG, blk=NSA_SEL_BLOCK,
                 window=0, out_rows=rows)
    ow = _flash(y, None, nb_p, t_p, q_ch=CH_QA, k_ch=CH_KW, v_ch=CH_VW, kvh=NSA_KV, nh=NSA_HPG, blk=0,
                window=NSA_WINDOW, out_rows=rows)
    ob = _flash(y, sel_b, nb_p, t_p, q_ch=CH_QB, k_ch=CH_KB, v_ch=CH_VB, kvh=MOBA_HEADS, nh=1, blk=MOBA_BLOCK,
                window=0, out_rows=rows)

    n_pad = x.shape[0] - n_prompt
    ys = y[n_prompt:]
    kvc_s = _compress_c2(_compress_c1_sample(cache_cmp, page_table, wcat), pe_rows, wcat, w2, gk)
    q_pad = jnp.zeros((nb_s, SUBLANE, NSA_HEADS * HEAD_DIM), F32).at[:, 0].set(ys[:nb_s, :NSA_HEADS * HEAD_DIM])
    oc_s, sel_s = _cattn(q_pad.reshape(nb_s * SUBLANE, -1), 0, 0, kvc_s, nb_s, SUBLANE, SUBLANE, past_len + 1, past_len,
                         nb_s * SUBLANE)
    oc_s = oc_s.reshape(nb_s, SUBLANE, -1)[:, 0]
    n_sel = (past_len + 1) // NSA_SEL_BLOCK
    k_sel = min(NSA_SEL_COUNT - 1, n_sel)
    mask_a = sel_s[:, :, :n_sel, 0]
    idx_a = jnp.argsort(-mask_a, axis=-1, stable=True)[..., :k_sel].astype(jnp.int32)
    vld_a = (jnp.take_along_axis(mask_a, idx_a, axis=-1) > 0).astype(jnp.int32)
    page = cache_sel.shape[2]
    bpp = page // NSA_SEL_BLOCK
    pt_b = page_table[:, None, :]
    tbl_a = jnp.take_along_axis(jnp.broadcast_to(pt_b, (nb_s, NSA_KV, pt_b.shape[-1])), idx_a // bpp, axis=-1) * bpp + idx_a % bpp
    os_s = _decode_attn(ys, cache_sel, li, tbl_a.astype(jnp.int32), vld_a, q_ch=CH_QA, kn_ch=CH_KS, vn_ch=CH_VS,
                        nh=NSA_HPG, rows=NSA_SEL_BLOCK, kvh=NSA_KV, r_min=0, name="dec_sel")
    win_buf = state_win.shape[2]
    tbl_w = jnp.broadcast_to(jnp.arange(nb_s, dtype=jnp.int32)[:, None, None], (nb_s, NSA_KV, 1))
    ow_s = _decode_attn(ys, state_win, li, tbl_w, jnp.ones_like(tbl_w), q_ch=CH_QA, kn_ch=CH_KW, vn_ch=CH_VW,
                        nh=NSA_HPG, rows=win_buf, kvh=NSA_KV, r_min=max(0, win_buf - (NSA_WINDOW - 1)), name="dec_win")
    means = _moba_means_sample(cache_moba, page_table)
    qb_pad = jnp.zeros((nb_s, SUBLANE, MOBA_HEADS * HEAD_DIM), F32).at[:, 0].set(
        ys[:nb_s, CH_QB * LANE:(CH_QB + MOBA_HEADS) * LANE])
    sel_m = _moba_gate_sample(qb_pad.reshape(nb_s * SUBLANE, -1), means, past_len)
    n_full = means.shape[2]
    k_top = min(MOBA_TOPK, n_full)
    mask_b = sel_m[:, :, :n_full, 0]
    idx_b = jnp.argsort(-mask_b, axis=-1, stable=True)[..., :k_top].astype(jnp.int32)
    vld_b = (jnp.take_along_axis(mask_b, idx_b, axis=-1) > 0).astype(jnp.int32)
    ppb = MOBA_BLOCK // page
    pg_b = (idx_b[..., None] * ppb + jnp.arange(ppb, dtype=jnp.int32)).reshape(nb_s, MOBA_HEADS, k_top * ppb)
    tbl_b = jnp.take_along_axis(jnp.broadcast_to(pt_b, (nb_s, MOBA_HEADS, pt_b.shape[-1])), pg_b, axis=-1)
    vld_b = jnp.repeat(vld_b, ppb, axis=-1)
    ob_s = _decode_attn(ys, cache_moba, li, tbl_b.astype(jnp.int32), vld_b, q_ch=CH_QB, kn_ch=CH_KB, vn_ch=CH_VB,
                        nh=1, rows=page, kvh=MOBA_HEADS, r_min=0, name="dec_moba")

    def rows_s(part, nh):
        return jnp.pad(part[:, :, :nh].reshape(nb_s, -1), ((0, n_pad - nb_s), (0, 0)))

    sample_parts = (jnp.pad(oc_s, ((0, n_pad - nb_s), (0, 0))), rows_s(os_s, NSA_HPG), rows_s(ow_s, NSA_HPG),
                    rows_s(ob_s, 1))
    mixed = _mix((oc, os_, ow, ob), sample_parts, y)
    x = _gmm(mixed, [(w_out[None], 0)], d, resid=x, name="attn_out")
    hid = _gmm(_rmsnorm(x, ffn_norm), [(w_gate[None], 0), (w_up[None], 0)], w_gate.shape[1], epi="swiglu",
               out_dtype=BF, name="ffn_up")
    x = _gmm(hid, [(w_down[None], 0)], d, resid=x, name="ffn_down")

    def rows_of(lo_ch, n_ch):
        return y[:, lo_ch * LANE:(lo_ch + n_ch) * LANE]

    def split(a, kvh):
        ap = a[:n_prompt].reshape(nb_p, t_p, 2, kvh, HEAD_DIM)
        as_ = a[n_prompt:n_prompt + nb_s].reshape(nb_s, 1, 2, kvh, HEAD_DIM)
        return ap, as_

    cmp_p, cmp_s = split(rows_of(CH_KC, 4), NSA_KV)
    sel_p, sel_s_rows = split(rows_of(CH_KS, 4), NSA_KV)
    win_p, win_s = split(rows_of(CH_KW, 4), NSA_KV)
    moba_p, moba_s = split(rows_of(CH_KB, 16), MOBA_HEADS)
    new_win_p = win_p[:, t_p - min(NSA_WINDOW, t_p):]
    win_all = jnp.concatenate([state_win[li], win_s], axis=1)
    keep = min(NSA_WINDOW, past_len + 1)
    new_win_s = win_all[:, win_all.shape[1] - keep:]
    return x, (cmp_p, cmp_s, sel_p, sel_s_rows, new_win_p, new_win_s, moba_p, moba_s)


def _conv_layer(x, n_prompt, nb_p, t_p, nb_s, prm, states):
    (conv_norm, w_in, pool_w, pool_scale, dw, dw_b, ln_g, ln_b, pw, w_out, moe_norm, router_w, wg, wu, wd) = prm
    state_pool, state_conv = states
    d = x.shape[1]
    width = pool_w.shape[0] * pool_w.shape[1]
    xn = _rmsnorm(x, conv_norm)
    w3 = w_in[None]
    tn = min(512, width)
    u_pool = _gmm(xn, [(w3, 0)], width, tn=tn, name="conv_in_pool")
    u_conv = _gmm(xn, [(w3, width), (w3, 2 * width)], width, epi="glu", tn=tn, name="conv_in_glu")
    prm_mix = (pool_w.astype(BF), pool_scale[None], jnp.pad(dw, ((0, 1), (0, 0))), dw_b[None], ln_g[None], ln_b[None],
               pw.astype(BF))
    pool_buf = max(POOL_WINDOWS) - 1
    conv_buf = CONV_WIDTH - 1
    hp_rows, hc_rows = 16, 32
    ts = min(256, t_p)
    nt = t_p // ts
    mixed_p = _convmix(u_pool, u_pool, u_conv, u_conv, prm_mix, ts=ts, nt=nt, n_tiles=nb_p * nt, zero_first=True,
                       avail0=0, hp_rows=hp_rows, hc_rows=hc_rows,
                       halo_blk=lambda s, hr: jnp.maximum(s * (ts // hr) - 1, 0))
    rows = x.shape[0]
    us_pool = jnp.zeros((nb_s, SUBLANE, width), F32).at[:, 0].set(u_pool[n_prompt:n_prompt + nb_s])
    us_conv = jnp.zeros((nb_s, SUBLANE, width), F32).at[:, 0].set(u_conv[n_prompt:n_prompt + nb_s])
    hs_pool = jnp.pad(state_pool, ((0, 0), (hp_rows - pool_buf, 0), (0, 0))).reshape(nb_s * hp_rows, width)
    hs_conv = jnp.pad(state_conv, ((0, 0), (hc_rows - conv_buf, 0), (0, 0))).reshape(nb_s * hc_rows, width)
    mixed_s = _convmix(us_pool.reshape(nb_s * SUBLANE, width), hs_pool, us_conv.reshape(nb_s * SUBLANE, width), hs_conv,
                       prm_mix, ts=SUBLANE, nt=1, n_tiles=nb_s, zero_first=False, avail0=pool_buf, hp_rows=hp_rows,
                       hc_rows=hc_rows, halo_blk=lambda s, hr: s)
    mixed_s = mixed_s.reshape(nb_s, SUBLANE, 2 * width)[:, 0]
    mixed = jnp.concatenate([mixed_p, jnp.pad(mixed_s, ((0, rows - n_prompt - nb_s), (0, 0)))], axis=0)
    x = _gmm(mixed, [(w_out[None], 0)], d, resid=x, name="conv_out")
    x = _moe(x, moe_norm, router_w, wg, wu, wd)

    up = u_pool[:n_prompt].reshape(nb_p, t_p, width)
    uc = u_conv[:n_prompt].reshape(nb_p, t_p, width)
    new_pool_p = up[:, t_p - pool_buf:]
    assert t_p >= conv_buf and t_p >= pool_buf
    new_conv_p = uc[:, t_p - conv_buf:]
    new_pool_s = jnp.concatenate([state_pool, u_pool[n_prompt:n_prompt + nb_s][:, None]], axis=1)[:, 1:]
    new_conv_s = jnp.concatenate([state_conv, u_conv[n_prompt:n_prompt + nb_s][:, None]], axis=1)[:, 1:]
    return x, (new_pool_p, new_pool_s, new_conv_p, new_conv_s)


def kernel(x_prompt, x_sample, cache_nsa_cmp, cache_nsa_sel, state_nsa_win, cache_moba, state_pool, state_conv, page_table, attn_norm, w_attn_in, nsa_qk_norm, nsa_cmp_pe, nsa_cmp_w1, nsa_cmp_w2, moba_qk_norm, w_attn_out, ffn_norm, ffn_w_gate, ffn_w_up, ffn_w_down, conv_norm, w_conv_in, pool_w, pool_scale, conv_dw, conv_dw_b, conv_ln_g, conv_ln_b, conv_pw, w_conv_out, moe_norm, router_w, moe_w_gate, moe_w_up, moe_w_down):
    nb_p, t_p, d = x_prompt.shape
    nb_s = x_sample.shape[0]
    n_prompt = nb_p * t_p
    n_pad = 2 * SUBLANE
    assert x_sample.shape[1] == 1 and nb_s <= n_pad
    x = jnp.concatenate([x_prompt.reshape(n_prompt, d), x_sample.reshape(nb_s, d),
                         jnp.zeros((n_pad - nb_s, d), F32)], axis=0)
    li = 0
    prm_a = (attn_norm[li], w_attn_in[li], nsa_qk_norm[li], nsa_cmp_pe[li], nsa_cmp_w1[li], nsa_cmp_w2[li],
             moba_qk_norm[li], w_attn_out[li], ffn_norm[li], ffn_w_gate[li], ffn_w_up[li], ffn_w_down[li])
    x, attn_new = _attn_layer(x, n_prompt, nb_p, t_p, nb_s, li, prm_a,
                              (cache_nsa_cmp, cache_nsa_sel, state_nsa_win, cache_moba), page_table)
    prm_c = (conv_norm[li], w_conv_in[li], pool_w[li], pool_scale[li], conv_dw[li], conv_dw_b[li], conv_ln_g[li],
             conv_ln_b[li], conv_pw[li], w_conv_out[li], moe_norm[li], router_w[li], moe_w_gate[li], moe_w_up[li],
             moe_w_down[li])
    x, conv_new = _conv_layer(x, n_prompt, nb_p, t_p, nb_s, prm_c, (state_pool[li], state_conv[li]))
    cmp_p, cmp_s, sel_p, sel_s, win_p, win_s, moba_p, moba_s = attn_new
    pool_p, pool_s, conv_p, conv_s = conv_new
    y_p = x[:n_prompt].reshape(nb_p, t_p, d)
    y_s = x[n_prompt:n_prompt + nb_s].reshape(nb_s, 1, d)
    st = lambda a: a[None]
    return (y_p, y_s, st(cmp_p), st(cmp_s), st(sel_p), st(sel_s), st(win_p), st(win_s), st(moba_p), st(moba_s),
            st(pool_p), st(pool_s), st(conv_p), st(conv_s))
```

```python
import functools
import math

import jax
import jax.numpy as jnp
from jax import lax
from jax.experimental import pallas as pl
from jax.experimental.pallas import tpu as pltpu

F32 = jnp.float32
BF = jnp.bfloat16
NEG_INF = float("-inf")

HEAD_DIM = 128
NORM_EPS = 1e-6
NSA_HEADS = 8
NSA_KV = 2
NSA_HPG = NSA_HEADS // NSA_KV
NSA_CMP_LEN = 32
NSA_CMP_STRIDE = 16
NSA_CMP_HIDDEN = 2 * HEAD_DIM
NSA_SEL_BLOCK = 64
NSA_SEL_COUNT = 16
NSA_WINDOW = 512
NSA_SEL_FORCE = 1.0e4
MOBA_HEADS = 8
MOBA_BLOCK = 256
MOBA_TOPK = 3
POOL_WINDOWS = (2, 4, 8, 16)
CONV_WIDTH = 31
N_EXPERTS = 8
TOP_K = 2

LANE = 128
SUBLANE = 8
VMEM_LIMIT = 58 * 1024 * 1024

CH_QA, CH_KC, CH_VC, CH_KS, CH_VS, CH_KW, CH_VW, CH_QB, CH_KB, CH_VB, CH_GATE = 0, 8, 10, 12, 14, 16, 18, 20, 28, 36, 44
ATTN_CHUNKS = 45

NT_DIMS = (((1,), (1,)), ((), ()))


def _cparams(sem):
    return pltpu.CompilerParams(dimension_semantics=sem, vmem_limit_bytes=VMEM_LIMIT)


def _largest_divisor(n, cap):
    return max(k for k in range(1, cap + 1) if n % k == 0)


def _split_bf16(a):
    hi = a.astype(BF)
    lo = (a - hi.astype(F32)).astype(BF)
    return hi, lo


def _dot3(a, b, dims=(((1,), (0,)), ((), ()))):
    ah, al = _split_bf16(a)
    bh, bl = _split_bf16(b)
    d = lambda x, y: lax.dot_general(x, y, dims, preferred_element_type=F32)
    return d(ah, bh) + d(ah, bl) + d(al, bh)


def _masked_softmax(s, mask, axis):
    s = jnp.where(mask, s, NEG_INF)
    m = jnp.max(s, axis=axis, keepdims=True)
    m = jnp.where(m == NEG_INF, 0.0, m)
    p = jnp.exp(s - m)
    d = jnp.sum(p, axis=axis, keepdims=True)
    return p / jnp.where(d > 0, d, 1.0)


def _rmsnorm_body(x_ref, g_ref, o_ref):
    x = x_ref[...]
    ms = jnp.mean(x * x, axis=-1, keepdims=True)
    o_ref[...] = (x * lax.rsqrt(ms + NORM_EPS) * g_ref[...]).astype(o_ref.dtype)


def _rmsnorm(x, g, tm=256):
    m, d = x.shape
    return pl.pallas_call(
        _rmsnorm_body,
        grid=(pl.cdiv(m, tm),),
        in_specs=[pl.BlockSpec((tm, d), lambda i: (i, 0)), pl.BlockSpec((1, d), lambda i: (0, 0))],
        out_specs=pl.BlockSpec((tm, d), lambda i: (i, 0)),
        out_shape=jax.ShapeDtypeStruct((m, d), BF),
        compiler_params=_cparams(("parallel",)),
        name="rmsnorm",
    )(x, g.reshape(1, d))


def _gmm_body(texp_ref, tval_ref, *refs, n_w, epi, has_resid, tn, n_o):
    x_ref = refs[0]
    w_refs = refs[1:1 + n_w]
    pos = 1 + n_w
    if epi == "headnorm":
        gain_ref, flag_ref = refs[pos], refs[pos + 1]
        pos += 2
    if has_resid:
        resid_ref = refs[pos]
        pos += 1
    o_refs = refs[pos:pos + n_o]
    o_ref = o_refs[0]
    wb_refs = refs[pos + n_o:pos + n_o + n_w]

    i = pl.program_id(1)
    changed = jnp.logical_or(i == 0, texp_ref[i] != texp_ref[jnp.maximum(i - 1, 0)])

    @pl.when(changed)
    def _():
        for k in range(n_w):
            wb_refs[k][...] = w_refs[k][...].astype(BF)

    @pl.when(tval_ref[i] == 0)
    def _():
        for o in o_refs:
            o[...] = jnp.zeros(o.shape, o.dtype)

    @pl.when(tval_ref[i] > 0)
    def _():
        x = x_ref[...]
        a = jnp.dot(x, wb_refs[0][...], preferred_element_type=F32)
        if epi == "swiglu":
            b = jnp.dot(x, wb_refs[1][...], preferred_element_type=F32)
            y = a * jax.nn.sigmoid(a) * b
        elif epi == "glu":
            b = jnp.dot(x, wb_refs[1][...], preferred_element_type=F32)
            y = a * jax.nn.sigmoid(b)
        else:
            y = a
        if has_resid:
            y = y + resid_ref[...]
        if epi == "headnorm":
            for c in range(tn // LANE):
                sl = slice(c * LANE, (c + 1) * LANE)
                yc = y[:, sl]
                r = lax.rsqrt(jnp.mean(yc * yc, axis=-1, keepdims=True) + NORM_EPS)
                f = flag_ref[:, sl]
                yn = yc * (f * r + (1.0 - f)) * gain_ref[:, sl]
                for o in o_refs:
                    o[:, sl] = yn.astype(o.dtype)
        else:
            for o in o_refs:
                o[...] = y.astype(o.dtype)


def _gmm(x, ws, n_out, *, epi="none", texp=None, tval=None, resid=None, gain=None, flag=None,
         out_dtype=F32, tm=256, tn=512, name="gmm"):
    m, kdim = x.shape
    tn = min(tn, n_out)
    n_m = pl.cdiv(m, tm)
    n_n = n_out // tn
    assert n_n * tn == n_out
    if texp is None:
        texp = jnp.zeros((n_m,), jnp.int32)
        tval = jnp.ones((n_m,), jnp.int32)
    n_w = len(ws)
    in_specs = [pl.BlockSpec((tm, kdim), lambda j, i, te, tv: (i, 0))]
    args = [x]
    for w, off in ws:
        assert off % tn == 0 and w.shape[1] == kdim
        ob = off // tn
        in_specs.append(pl.BlockSpec((None, kdim, tn), lambda j, i, te, tv, ob=ob: (te[i], 0, j + ob)))
        args.append(w)
    if epi == "headnorm":
        in_specs += [pl.BlockSpec((1, tn), lambda j, i, te, tv: (0, j))] * 2
        args += [gain, flag]
    if resid is not None:
        in_specs.append(pl.BlockSpec((tm, tn), lambda j, i, te, tv: (i, j)))
        args.append(resid)
    dtypes = out_dtype if isinstance(out_dtype, tuple) else (out_dtype,)
    body = functools.partial(_gmm_body, n_w=n_w, epi=epi, has_resid=resid is not None, tn=tn, n_o=len(dtypes))
    outs = pl.pallas_call(
        body,
        grid_spec=pltpu.PrefetchScalarGridSpec(
            num_scalar_prefetch=2,
            grid=(n_n, n_m),
            in_specs=in_specs,
            out_specs=[pl.BlockSpec((tm, tn), lambda j, i, te, tv: (i, j)) for _ in dtypes],
            scratch_shapes=[pltpu.VMEM((kdim, tn), BF) for _ in range(n_w)],
        ),
        out_shape=[jax.ShapeDtypeStruct((m, n_out), dt) for dt in dtypes],
        compiler_params=_cparams(("arbitrary", "arbitrary")),
        name=name,
    )(texp, tval, *args)
    return outs if isinstance(out_dtype, tuple) else outs[0]


def _c1_body(x_ref, w_ref, o_ref):
    nsb = x_ref.shape[0] // NSA_CMP_STRIDE
    acc = jnp.zeros((nsb, 2 * NSA_CMP_HIDDEN), F32)
    for s in range(NSA_CMP_STRIDE):
        xs = x_ref[pl.ds(s, nsb, stride=NSA_CMP_STRIDE), :]
        acc = acc + jnp.dot(xs.astype(BF), w_ref[s], preferred_element_type=F32)
    o_ref[...] = acc


def _compress_c1_prompt(y, wcat, nb, t):
    nsub = t // NSA_CMP_STRIDE
    return pl.pallas_call(
        _c1_body,
        grid=(nb, 2, NSA_KV),
        in_specs=[pl.BlockSpec((t, LANE), lambda b, kv, g: (b, CH_KC + 2 * kv + g)),
                  pl.BlockSpec((None, NSA_CMP_STRIDE, HEAD_DIM, 2 * NSA_CMP_HIDDEN), lambda b, kv, g: (kv, 0, 0, 0))],
        out_specs=pl.BlockSpec((None, None, None, nsub, 2 * NSA_CMP_HIDDEN), lambda b, kv, g: (b, kv, g, 0, 0)),
        out_shape=jax.ShapeDtypeStruct((nb, 2, NSA_KV, nsub, 2 * NSA_CMP_HIDDEN), F32),
        compiler_params=_cparams(("parallel", "parallel", "parallel")),
        name="cmp_c1_prompt",
    )(y, wcat)


C1_PAGES = 8


def _c1_sample_body(pt_ref, *refs, n_in, page):
    x_refs = refs[:n_in]
    w_ref, o_ref, scr = refs[n_in:n_in + 3]
    nsb = page // NSA_CMP_STRIDE
    rows = n_in * nsb
    for kv in range(2):
        acc = jnp.zeros((rows * NSA_KV, 2 * NSA_CMP_HIDDEN), F32)
        for s in range(NSA_CMP_STRIDE):
            parts = [xr[pl.ds(s, nsb, stride=NSA_CMP_STRIDE), kv, :, :].reshape(nsb * NSA_KV, HEAD_DIM) for xr in x_refs]
            xs = parts[0] if n_in == 1 else jnp.concatenate(parts, axis=0)
            acc = acc + jnp.dot(xs.astype(BF), w_ref[kv, s], preferred_element_type=F32)
        for c in range(scr.shape[0]):
            scr[c] = acc[:, c * LANE:(c + 1) * LANE]
        for g in range(NSA_KV):
            for c in range(scr.shape[0]):
                o_ref[kv, g, :, c * LANE:(c + 1) * LANE] = scr[c, pl.ds(g, rows, stride=NSA_KV), :]


def _compress_c1_sample(cache, li, page_table, wcat):
    nb, n_pages = page_table.shape
    page = cache.shape[2]
    npp = math.gcd(C1_PAGES, n_pages)
    nsb = page // NSA_CMP_STRIDE
    nsub = n_pages * nsb
    body = functools.partial(_c1_sample_body, n_in=npp, page=page)
    in_specs = [
        pl.BlockSpec((None, None, page, 2, NSA_KV, HEAD_DIM), lambda b, j, pt, i=i: (li, pt[b, j * npp + i], 0, 0, 0, 0))
        for i in range(npp)
    ]
    in_specs.append(pl.BlockSpec(wcat.shape, lambda b, j, pt: (0, 0, 0, 0)))
    return pl.pallas_call(
        body,
        grid_spec=pltpu.PrefetchScalarGridSpec(
            num_scalar_prefetch=1,
            grid=(nb, n_pages // npp),
            in_specs=in_specs,
            out_specs=pl.BlockSpec((None, 2, NSA_KV, npp * nsb, 2 * NSA_CMP_HIDDEN), lambda b, j, pt: (b, 0, 0, j, 0)),
            scratch_shapes=[pltpu.VMEM((2 * NSA_CMP_HIDDEN // LANE, npp * nsb * NSA_KV, LANE), F32)],
        ),
        out_shape=jax.ShapeDtypeStruct((nb, 2, NSA_KV, nsub, 2 * NSA_CMP_HIDDEN), F32),
        compiler_params=_cparams(("parallel", "parallel")),
        name="cmp_c1_sample",
    )(page_table, *([cache] * npp), wcat)


def _c2_body(p_ref, pe_ref, wcat_ref, w2_ref, g_ref, o_ref, *, nsub, n_c):
    kv = pl.program_id(1)
    pep = jnp.zeros((SUBLANE, 2 * NSA_CMP_HIDDEN), F32)
    for s in range(NSA_CMP_STRIDE):
        pep = pep + jnp.dot(pe_ref[s], wcat_ref[s], preferred_element_type=F32)
    bias = pep[0:1, :NSA_CMP_HIDDEN] + pep[1:2, NSA_CMP_HIDDEN:]
    nxt = pltpu.roll(p_ref[:, NSA_CMP_HIDDEN:], nsub - 1, 0)
    h = p_ref[:, :NSA_CMP_HIDDEN] + nxt + bias
    h = h * jax.nn.sigmoid(h)
    o = jnp.dot(h.astype(BF), w2_ref[...], preferred_element_type=F32)
    r = lax.rsqrt(jnp.mean(o * o, axis=-1, keepdims=True) + NORM_EPS)
    o = jnp.where(kv == 0, o * r * g_ref[...], o)
    row = lax.broadcasted_iota(jnp.int32, (nsub, 1), 0)
    o_ref[...] = jnp.where(row < n_c, o, 0.0)


def _compress_c2(p, pe_rows, wcat, w2, gk):
    nb, _, _, nsub, _ = p.shape
    n_c = nsub - NSA_CMP_LEN // NSA_CMP_STRIDE + 1
    body = functools.partial(_c2_body, nsub=nsub, n_c=n_c)
    return pl.pallas_call(
        body,
        grid=(nb, 2, NSA_KV),
        in_specs=[pl.BlockSpec((None, None, None, nsub, 2 * NSA_CMP_HIDDEN), lambda b, kv, g: (b, kv, g, 0, 0)),
                  pl.BlockSpec((None, NSA_CMP_STRIDE, SUBLANE, HEAD_DIM), lambda b, kv, g: (kv, 0, 0, 0)),
                  pl.BlockSpec((None, NSA_CMP_STRIDE, HEAD_DIM, 2 * NSA_CMP_HIDDEN), lambda b, kv, g: (kv, 0, 0, 0)),
                  pl.BlockSpec((None, NSA_CMP_HIDDEN, HEAD_DIM), lambda b, kv, g: (kv, 0, 0)),
                  pl.BlockSpec((1, HEAD_DIM), lambda b, kv, g: (0, 0))],
        out_specs=pl.BlockSpec((None, None, None, nsub, HEAD_DIM), lambda b, kv, g: (b, kv, g, 0, 0)),
        out_shape=jax.ShapeDtypeStruct((nb, 2, NSA_KV, nsub, HEAD_DIM), F32),
        compiler_params=_cparams(("parallel", "parallel", "parallel")),
        name="cmp_c2",
    )(p, pe_rows, wcat, w2, gk)


def _rank_select(sc_ref, n_iter, k):
    score = sc_ref[...]
    brow = lax.broadcasted_iota(jnp.int32, score.shape, 0)

    def body(j, rank):
        r = sc_ref[pl.ds(j, 1), :]
        beats = jnp.logical_or(r > score, jnp.logical_and(r == score, j < brow))
        return rank + jnp.where(beats, 1.0, 0.0)

    rank = lax.fori_loop(0, n_iter, body, jnp.zeros(score.shape, F32))
    return rank < k


def _cattn_body(q_ref, kc_ref, vc_ref, oc_ref, sel_ref, sc_ref, *, tq, ncp, n_c, ns, nsp, pos0, ksel):
    t0 = pos0 + pl.program_id(2) * tq
    scale = HEAD_DIM ** -0.5
    kc = kc_ref[...].astype(BF)
    vc = vc_ref[...].astype(BF)
    trow = t0 + lax.broadcasted_iota(jnp.int32, (tq, 1), 0)
    ncol = lax.broadcasted_iota(jnp.int32, (1, ncp), 1)
    valid = jnp.logical_and(ncol * NSA_CMP_STRIDE + (NSA_CMP_LEN - 1) <= trow, ncol < n_c)
    tcol = t0 + lax.broadcasted_iota(jnp.int32, (1, tq), 1)
    nrow = lax.broadcasted_iota(jnp.int32, (ncp, 1), 0)
    valid_t = jnp.logical_and(nrow * NSA_CMP_STRIDE + (NSA_CMP_LEN - 1) <= tcol, nrow < n_c)
    psum_t = jnp.zeros((ncp, tq), F32)
    for z in range(NSA_HPG):
        sl = slice(z * HEAD_DIM, (z + 1) * HEAD_DIM)
        q = (q_ref[:, sl] * scale).astype(BF)
        s = lax.dot_general(q, kc, NT_DIMS, preferred_element_type=F32)
        p = _masked_softmax(s, valid, -1)
        oc_ref[:, sl] = jnp.dot(p.astype(BF), vc, preferred_element_type=F32)
        s_t = lax.dot_general(kc, q, NT_DIMS, preferred_element_type=F32)
        psum_t = psum_t + _masked_softmax(s_t, valid_t, 0)
    r = NSA_SEL_BLOCK // NSA_CMP_STRIDE
    brow = lax.broadcasted_iota(jnp.int32, (nsp, 1), 0)
    lo = r * brow - 1
    inside = jnp.logical_and(ncol >= lo, ncol <= lo + r)
    edge = jnp.logical_or(ncol == lo, ncol == lo + r)
    m_t = jnp.where(inside, jnp.where(edge, 0.5, 1.0), 0.0).astype(BF)
    hi = psum_t.astype(BF)
    mid = (psum_t - hi.astype(F32)).astype(BF)
    low = (psum_t - hi.astype(F32) - mid.astype(F32)).astype(BF)
    imp_t = (jnp.dot(m_t, hi, preferred_element_type=F32) + jnp.dot(m_t, mid, preferred_element_type=F32)
             + jnp.dot(m_t, low, preferred_element_type=F32))
    own = tcol // NSA_SEL_BLOCK
    past = jnp.logical_and(brow < own, brow < ns)
    forced = jnp.logical_or(brow == 0, brow == own - 1)
    sc_ref[...] = jnp.where(past, jnp.where(forced, NSA_SEL_FORCE, imp_t), NEG_INF)
    chosen = jnp.logical_and(past, _rank_select(sc_ref, ns, ksel))
    sel = jnp.where(jnp.logical_or(chosen, brow == own), 1.0, 0.0)
    rows = sel_ref.shape[0]
    if rows > nsp:
        sel_ref[...] = jnp.zeros(sel_ref.shape, F32)
    sel_ref[0:nsp, :] = sel


def _cattn(q_arr, q_row_blk0, q_col_blk0, kvc, nb, tlen, tq, length, pos0, out_rows):
    nq = tlen // tq
    ncp = kvc.shape[3]
    n_c = ncp - NSA_CMP_LEN // NSA_CMP_STRIDE + 1
    ns = length // NSA_SEL_BLOCK
    ksel = min(NSA_SEL_COUNT - 1, ns)
    nsp = -(-ns // SUBLANE) * SUBLANE
    sel_rows = max(nsp, LANE)
    body = functools.partial(_cattn_body, tq=tq, ncp=ncp, n_c=n_c, ns=ns, nsp=nsp, pos0=pos0, ksel=ksel)
    gw = NSA_HPG * HEAD_DIM
    return pl.pallas_call(
        body,
        grid=(nb, NSA_KV, nq),
        in_specs=[pl.BlockSpec((tq, gw), lambda b, g, qi: (q_row_blk0 + b * nq + qi, q_col_blk0 + g)),
                  pl.BlockSpec((None, None, None, ncp, HEAD_DIM), lambda b, g, qi: (b, 0, g, 0, 0)),
                  pl.BlockSpec((None, None, None, ncp, HEAD_DIM), lambda b, g, qi: (b, 1, g, 0, 0))],
        out_specs=[pl.BlockSpec((tq, gw), lambda b, g, qi: (b * nq + qi, g)),
                   pl.BlockSpec((None, None, sel_rows, tq), lambda b, g, qi: (b, g, 0, qi))],
        out_shape=[jax.ShapeDtypeStruct((out_rows, NSA_KV * gw), F32),
                   jax.ShapeDtypeStruct((nb, NSA_KV, sel_rows, tlen), F32)],
        scratch_shapes=[pltpu.VMEM((nsp, tq), F32)],
        compiler_params=_cparams(("parallel", "parallel", "parallel")),
        name="nsa_cmp_attn",
    )(q_arr, kvc, kvc)


def _cattn_row_body(q_ref, kc_ref, vc_ref, oc_ref, sel_ref, *, tq, ncp, n_c, ns, nsl, pos0, ksel):
    scale = HEAD_DIM ** -0.5
    kc = kc_ref[...].astype(BF)
    vc = vc_ref[...].astype(BF)
    trow = pos0 + lax.broadcasted_iota(jnp.int32, (tq, 1), 0)
    ncol = lax.broadcasted_iota(jnp.int32, (1, ncp), 1)
    valid = jnp.logical_and(ncol * NSA_CMP_STRIDE + (NSA_CMP_LEN - 1) <= trow, ncol < n_c)
    psum = jnp.zeros((tq, ncp), F32)
    for z in range(NSA_HPG):
        sl = slice(z * HEAD_DIM, (z + 1) * HEAD_DIM)
        q = (q_ref[:, sl] * scale).astype(BF)
        p = _masked_softmax(lax.dot_general(q, kc, NT_DIMS, preferred_element_type=F32), valid, -1)
        oc_ref[:, sl] = jnp.dot(p.astype(BF), vc, preferred_element_type=F32)
        psum = psum + p
    r = NSA_SEL_BLOCK // NSA_CMP_STRIDE
    nrow = lax.broadcasted_iota(jnp.int32, (ncp, 1), 0)
    bcol = lax.broadcasted_iota(jnp.int32, (1, nsl), 1)
    lo = r * bcol - 1
    inside = jnp.logical_and(nrow >= lo, nrow <= lo + r)
    edge = jnp.logical_or(nrow == lo, nrow == lo + r)
    m = jnp.where(inside, jnp.where(edge, 0.5, 1.0), 0.0).astype(BF)
    hi = psum.astype(BF)
    mid = (psum - hi.astype(F32)).astype(BF)
    low = (psum - hi.astype(F32) - mid.astype(F32)).astype(BF)
    imp = (jnp.dot(hi, m, preferred_element_type=F32) + jnp.dot(mid, m, preferred_element_type=F32)
           + jnp.dot(low, m, preferred_element_type=F32))
    own = trow // NSA_SEL_BLOCK
    past = jnp.logical_and(bcol < own, bcol < ns)
    forced = jnp.logical_or(bcol == 0, bcol == own - 1)
    score = jnp.where(past, jnp.where(forced, NSA_SEL_FORCE, imp), NEG_INF)
    rank = jnp.zeros((tq, nsl), F32)
    for j in range(ns):
        cj = score[:, j:j + 1]
        beats = jnp.logical_or(cj > score, jnp.logical_and(cj == score, j < bcol))
        rank = rank + jnp.where(beats, 1.0, 0.0)
    chosen = jnp.logical_and(past, rank < ksel)
    sel_ref[...] = jnp.where(jnp.logical_or(chosen, bcol == own), 1.0, 0.0)


def _cattn_sample(q_pad, kvc, nb, length, pos0):
    tq = SUBLANE
    ncp = kvc.shape[3]
    n_c = ncp - NSA_CMP_LEN // NSA_CMP_STRIDE + 1
    ns = length // NSA_SEL_BLOCK
    ksel = min(NSA_SEL_COUNT - 1, ns)
    nsl = -(-ns // LANE) * LANE
    body = functools.partial(_cattn_row_body, tq=tq, ncp=ncp, n_c=n_c, ns=ns, nsl=nsl, pos0=pos0, ksel=ksel)
    gw = NSA_HPG * HEAD_DIM
    return pl.pallas_call(
        body,
        grid=(nb, NSA_KV),
        in_specs=[pl.BlockSpec((tq, gw), lambda b, g: (b, g)),
                  pl.BlockSpec((None, None, None, ncp, HEAD_DIM), lambda b, g: (b, 0, g, 0, 0)),
                  pl.BlockSpec((None, None, None, ncp, HEAD_DIM), lambda b, g: (b, 1, g, 0, 0))],
        out_specs=[pl.BlockSpec((tq, gw), lambda b, g: (b, g)),
                   pl.BlockSpec((None, None, tq, nsl), lambda b, g: (b, g, 0, 0))],
        out_shape=[jax.ShapeDtypeStruct((nb * tq, NSA_KV * gw), F32),
                   jax.ShapeDtypeStruct((nb, NSA_KV, tq, nsl), F32)],
        compiler_params=_cparams(("parallel", "parallel")),
        name="nsa_cmp_attn_sample",
    )(q_pad, kvc, kvc)


def _gate_body(q_ref, k_ref, sel_ref, mean_ref, sc_ref, *, tq, nb, nbp, pos0, from_means):
    qi = pl.program_id(2)
    if from_means:
        means = k_ref[...]
    else:
        @pl.when(qi == 0)
        def _():
            mean_ref[...] = jnp.zeros(mean_ref.shape, F32)
            for j in range(nb):
                blk = k_ref[j * MOBA_BLOCK:(j + 1) * MOBA_BLOCK, :]
                mean_ref[j:j + 1, :] = jnp.sum(blk, axis=0, keepdims=True) * (1.0 / MOBA_BLOCK)

        means = mean_ref[...]
    g_t = _dot3(means, q_ref[...], NT_DIMS)
    tcol = pos0 + qi * tq + lax.broadcasted_iota(jnp.int32, (1, tq), 1)
    own = tcol // MOBA_BLOCK
    brow = lax.broadcasted_iota(jnp.int32, (nbp, 1), 0)
    past = jnp.logical_and(brow < own, brow < nb)
    sc_ref[...] = jnp.where(past, g_t, NEG_INF)
    chosen = jnp.logical_and(past, _rank_select(sc_ref, nb, min(MOBA_TOPK, nb)))
    sel = jnp.where(jnp.logical_or(chosen, brow == own), 1.0, 0.0)
    if sel_ref.shape[0] > nbp:
        sel_ref[...] = jnp.zeros(sel_ref.shape, F32)
    sel_ref[0:nbp, :] = sel


def _moba_gate_prompt(y, nb, t, tq):
    nq = t // tq
    n_full = t // MOBA_BLOCK
    nbp = -(-n_full // SUBLANE) * SUBLANE
    body = functools.partial(_gate_body, tq=tq, nb=n_full, nbp=nbp, pos0=0, from_means=False)
    return pl.pallas_call(
        body,
        grid=(nb, MOBA_HEADS, nq),
        in_specs=[pl.BlockSpec((tq, LANE), lambda b, h, qi: (b * nq + qi, CH_QB + h)),
                  pl.BlockSpec((t, LANE), lambda b, h, qi: (b, CH_KB + h))],
        out_specs=pl.BlockSpec((None, None, LANE, tq), lambda b, h, qi: (b, h, 0, qi)),
        out_shape=jax.ShapeDtypeStruct((nb, MOBA_HEADS, LANE, t), F32),
        scratch_shapes=[pltpu.VMEM((nbp, HEAD_DIM), F32), pltpu.VMEM((nbp, tq), F32)],
        compiler_params=_cparams(("parallel", "parallel", "arbitrary")),
        name="moba_gate_prompt",
    )(y, y)


def _moba_gate_sample(q_pad, means, pos0):
    nb, _, n_full, _ = means.shape
    tq = SUBLANE
    body = functools.partial(_gate_body, tq=tq, nb=n_full, nbp=n_full, pos0=pos0, from_means=True)
    return pl.pallas_call(
        body,
        grid=(nb, MOBA_HEADS, 1),
        in_specs=[pl.BlockSpec((tq, LANE), lambda b, h, qi: (b, h)),
                  pl.BlockSpec((None, None, n_full, HEAD_DIM), lambda b, h, qi: (b, h, 0, 0))],
        out_specs=pl.BlockSpec((None, None, max(n_full, LANE), tq), lambda b, h, qi: (b, h, 0, 0)),
        out_shape=jax.ShapeDtypeStruct((nb, MOBA_HEADS, max(n_full, LANE), tq), F32),
        scratch_shapes=[pltpu.VMEM((SUBLANE, HEAD_DIM), F32), pltpu.VMEM((n_full, tq), F32)],
        compiler_params=_cparams(("parallel", "parallel", "arbitrary")),
        name="moba_gate_sample",
    )(q_pad, means)


MEANS_BLOCKS = 4


def _means_body(pt_ref, *refs, ppb, bps):
    c_refs = refs[:bps * ppb]
    o_ref = refs[bps * ppb]
    j = pl.program_id(1)
    for q in range(bps):
        acc = jnp.sum(c_refs[q * ppb][...], axis=0)
        for i in range(1, ppb):
            acc = acc + jnp.sum(c_refs[q * ppb + i][...], axis=0)
        m = acc * (1.0 / MOBA_BLOCK)
        for h in range(MOBA_HEADS):
            o_ref[h, pl.ds(j * bps + q, 1), :] = m[h:h + 1, :]


def _moba_means_sample(cache, li, page_table):
    nb, n_pages = page_table.shape
    page = cache.shape[2]
    ppb = MOBA_BLOCK // page
    n_full = n_pages // ppb
    bps = math.gcd(MEANS_BLOCKS, n_full)
    body = functools.partial(_means_body, ppb=ppb, bps=bps)
    in_specs = [
        pl.BlockSpec((None, None, page, None, MOBA_HEADS, HEAD_DIM),
                     lambda b, j, pt, i=i: (li, pt[b, j * bps * ppb + i], 0, 0, 0, 0))
        for i in range(bps * ppb)
    ]
    return pl.pallas_call(
        body,
        grid_spec=pltpu.PrefetchScalarGridSpec(
            num_scalar_prefetch=1,
            grid=(nb, n_full // bps),
            in_specs=in_specs,
            out_specs=pl.BlockSpec((None, MOBA_HEADS, n_full, HEAD_DIM), lambda b, j, pt: (b, 0, 0, 0)),
        ),
        out_shape=jax.ShapeDtypeStruct((nb, MOBA_HEADS, n_full, HEAD_DIM), F32),
        compiler_params=_cparams(("parallel", "arbitrary")),
        name="moba_means_sample",
    )(page_table, *([cache] * (bps * ppb)))


FLASH_UNIT_ROWS = 256
LOG2_E = 1.4426950408889634


def _flash_body(*refs, tq, tk, nh, blk, window, masked, ur):
    q_ref, k_ref, v_ref = refs[:3]
    pos = 3
    if masked:
        sel_ref = refs[pos]
        pos += 1
    o_ref, qs_ref = refs[pos:pos + 2]
    units = [(z, r0) for z in range(nh) for r0 in range(0, tq, ur)]
    stat = refs[pos + 2:pos + 2 + 2 * len(units)]
    m_refs, acc_refs = stat[0::2], stat[1::2]
    t0 = pl.program_id(2) * tq
    scale = HEAD_DIM ** -0.5 * LOG2_E
    for z in range(nh):
        qs_ref[z] = (q_ref[:, z * HEAD_DIM:(z + 1) * HEAD_DIM] * scale).astype(BF)
    for u in range(len(units)):
        m_refs[u][...] = jnp.full(m_refs[u].shape, NEG_INF, F32)
        acc_refs[u][...] = jnp.zeros(acc_refs[u].shape, F32)
    if masked:
        selb = jnp.transpose(sel_ref[...]).astype(BF)
    trow = t0 + lax.broadcasted_iota(jnp.int32, (tq, 1), 0)
    c_hi = (t0 + tq + tk - 1) // tk
    c_lo = jnp.maximum(t0 - window + 1, 0) // tk if window else 0

    def chunk(c, carry):
        ks = pl.multiple_of(c * tk, tk)
        kc = k_ref[pl.ds(ks, tk), :]
        vc = jnp.concatenate([v_ref[pl.ds(ks, tk), :], jnp.ones((tk, LANE), BF)], axis=1)
        pcol = ks + lax.broadcasted_iota(jnp.int32, (1, tk), 1)
        valid = pcol <= trow
        if window:
            valid = jnp.logical_and(valid, pcol > trow - window)
        if masked:
            jrow = lax.broadcasted_iota(jnp.int32, (LANE, 1), 0)
            expand = jnp.where(pcol // blk == jrow, 1.0, 0.0).astype(BF)
            valid = jnp.logical_and(valid, jnp.dot(selb, expand, preferred_element_type=F32) > 0.5)
        for u, (z, r0) in enumerate(units):
            s = lax.dot_general(qs_ref[z, r0:r0 + ur, :], kc, NT_DIMS, preferred_element_type=F32)
            s = jnp.where(valid[r0:r0 + ur], s, NEG_INF)
            m_old = m_refs[u][...]
            m_new = jnp.maximum(m_old, jnp.max(s, axis=-1, keepdims=True))
            m_safe = jnp.where(m_new == NEG_INF, 0.0, m_new)
            alpha = jnp.exp2(m_old - m_safe)
            p = jnp.exp2(s - m_safe)
            acc_refs[u][...] = alpha * acc_refs[u][...] + jnp.dot(p.astype(BF), vc, preferred_element_type=F32)
            m_refs[u][...] = m_new
        return carry

    lax.fori_loop(c_lo, c_hi, chunk, 0)
    for u, (z, r0) in enumerate(units):
        l = acc_refs[u][:, HEAD_DIM:HEAD_DIM + 1]
        o_ref[r0:r0 + ur, z * HEAD_DIM:(z + 1) * HEAD_DIM] = acc_refs[u][:, :HEAD_DIM] / jnp.where(l > 0, l, 1.0)


def _flash(y, y16, sel, nb, t, *, q_ch, k_ch, v_ch, kvh, nh, blk, window, out_rows, tq=512, tk=512):
    tq = min(tq, t)
    tk = min(tk, t)
    nq = t // tq
    masked = sel is not None
    gw = nh * HEAD_DIM
    assert q_ch % nh == 0
    in_specs = [pl.BlockSpec((tq, gw), lambda b, h, qi: (b * nq + qi, q_ch // nh + h)),
                pl.BlockSpec((t, LANE), lambda b, h, qi: (b, k_ch + h)),
                pl.BlockSpec((t, LANE), lambda b, h, qi: (b, v_ch + h))]
    args = [y, y16, y16]
    if masked:
        in_specs.append(pl.BlockSpec((None, None, LANE, tq), lambda b, h, qi: (b, h, 0, qi)))
        args.append(sel)
    ur = min(FLASH_UNIT_ROWS, tq)
    n_units = nh * (tq // ur)
    body = functools.partial(_flash_body, tq=tq, tk=tk, nh=nh, blk=blk, window=window, masked=masked, ur=ur)
    unit_scratch = [pltpu.VMEM((ur, 1), F32), pltpu.VMEM((ur, 2 * HEAD_DIM), F32)]
    return pl.pallas_call(
        body,
        grid=(nb, kvh, nq),
        in_specs=in_specs,
        out_specs=pl.BlockSpec((tq, gw), lambda b, h, qi: (b * nq + qi, h)),
        out_shape=jax.ShapeDtypeStruct((out_rows, kvh * gw), F32),
        scratch_shapes=[pltpu.VMEM((nh, tq, HEAD_DIM), BF)] + unit_scratch * n_units,
        compiler_params=_cparams(("parallel", "parallel", "parallel")),
        name="flash_" + ("win" if window else "blk%d" % blk),
    )(*args)


def _dec_body(tbl_ref, vld_ref, q_ref, *refs, nh, rows, kvh, n_steps, nblk, r_min):
    k_refs = refs[:nblk]
    v_refs = refs[nblk:2 * nblk]
    kn_ref, vn_ref, o_ref, m_ref, l_ref, acc_ref = refs[2 * nblk:2 * nblk + 6]
    b = pl.program_id(0)
    h = pl.program_id(1)
    j = pl.program_id(2)
    scale = HEAD_DIM ** -0.5
    qrow = q_ref[pl.ds(b, 1), :] * scale
    zrow = lax.broadcasted_iota(jnp.int32, (SUBLANE, 1), 0)
    qm = jnp.zeros((SUBLANE, HEAD_DIM), F32)
    for z in range(nh):
        qm = jnp.where(zrow == z, qrow[:, z * HEAD_DIM:(z + 1) * HEAD_DIM], qm)

    @pl.when(j == 0)
    def _():
        m_ref[...] = jnp.full(m_ref.shape, NEG_INF, F32)
        l_ref[...] = jnp.zeros(l_ref.shape, F32)
        acc_ref[...] = jnp.zeros(acc_ref.shape, F32)

    col = lax.broadcasted_iota(jnp.int32, (1, rows * kvh), 1)
    valid = jnp.logical_and(col % kvh == h, col // kvh >= r_min)
    for i in range(nblk):
        @pl.when(vld_ref[b, h, j * nblk + i] > 0)
        def _(i=i):
            kk = k_refs[i][...].reshape(rows * kvh, HEAD_DIM).astype(BF)
            vv = v_refs[i][...].reshape(rows * kvh, HEAD_DIM).astype(BF)
            s = lax.dot_general(qm.astype(BF), kk, NT_DIMS, preferred_element_type=F32)
            s = jnp.where(valid, s, NEG_INF)
            m_old = m_ref[...]
            m_new = jnp.maximum(m_old, jnp.max(s, axis=-1, keepdims=True))
            m_safe = jnp.where(m_new == NEG_INF, 0.0, m_new)
            alpha = jnp.exp(m_old - m_safe)
            p = jnp.exp(s - m_safe)
            l_ref[...] = alpha * l_ref[...] + jnp.sum(p, axis=-1, keepdims=True)
            acc_ref[...] = alpha * acc_ref[...] + jnp.dot(p.astype(BF), vv, preferred_element_type=F32)
            m_ref[...] = m_new

    @pl.when(j == n_steps - 1)
    def _():
        kn = kn_ref[pl.ds(b, 1), :]
        vn = vn_ref[pl.ds(b, 1), :]
        s = jnp.sum(qm * kn, axis=-1, keepdims=True)
        m_old = m_ref[...]
        m_new = jnp.maximum(m_old, s)
        alpha = jnp.exp(m_old - m_new)
        p = jnp.exp(s - m_new)
        l = alpha * l_ref[...] + p
        o_ref[...] = (alpha * acc_ref[...] + p * vn) / l


def _decode_attn(ys, cache, li_fixed, tbl, vld, *, q_ch, kn_ch, vn_ch, nh, rows, kvh, r_min, nblk, name):
    nb, _, n_list = tbl.shape
    assert n_list % nblk == 0
    n_steps = n_list // nblk
    page = cache.shape[2]
    bpp = page // rows
    gw = nh * HEAD_DIM
    body = functools.partial(_dec_body, nh=nh, rows=rows, kvh=kvh, n_steps=n_steps, nblk=nblk, r_min=r_min)

    def kv_spec(kv, i):
        def imap(b, h, j, tb, vl):
            t = tb[b, h, j * nblk + i]
            return (li_fixed, t // bpp, t % bpp, kv, 0, 0)

        return pl.BlockSpec((None, None, rows, None, kvh, HEAD_DIM), imap)

    nrow = ys.shape[0]
    return pl.pallas_call(
        body,
        grid_spec=pltpu.PrefetchScalarGridSpec(
            num_scalar_prefetch=2,
            grid=(nb, kvh, n_steps),
            in_specs=[pl.BlockSpec((nrow, gw), lambda b, h, j, tb, vl: (0, q_ch // nh + h))]
            + [kv_spec(0, i) for i in range(nblk)] + [kv_spec(1, i) for i in range(nblk)]
            + [pl.BlockSpec((nrow, LANE), lambda b, h, j, tb, vl: (0, kn_ch + h)),
               pl.BlockSpec((nrow, LANE), lambda b, h, j, tb, vl: (0, vn_ch + h))],
            out_specs=pl.BlockSpec((None, None, SUBLANE, HEAD_DIM), lambda b, h, j, tb, vl: (b, h, 0, 0)),
            scratch_shapes=[pltpu.VMEM((SUBLANE, 1), F32), pltpu.VMEM((SUBLANE, 1), F32), pltpu.VMEM((SUBLANE, HEAD_DIM), F32)],
        ),
        out_shape=jax.ShapeDtypeStruct((nb, kvh, SUBLANE, HEAD_DIM), F32),
        compiler_params=_cparams(("parallel", "parallel", "arbitrary")),
        name=name,
    )(tbl, vld, ys, *([cache] * (2 * nblk)), ys, ys)


def _mix_body(oc_ref, os_ref, ow_ref, ob_ref, soc_ref, sos_ref, sow_ref, sob_ref, gate_ref, o_ref, *, n_p_tiles, n_s):
    i = pl.program_id(0)
    w = NSA_HEADS * HEAD_DIM

    def emit(oc, os_, ow, ob, rows):
        g = jax.nn.sigmoid(gate_ref[0:rows, :])
        for h in range(NSA_HEADS):
            sl = slice(h * HEAD_DIM, (h + 1) * HEAD_DIM)
            o = (g[:, 3 * h:3 * h + 1] * oc[:, sl] + g[:, 3 * h + 1:3 * h + 2] * os_[:, sl]
                 + g[:, 3 * h + 2:3 * h + 3] * ow[:, sl])
            o_ref[0:rows, sl] = o.astype(o_ref.dtype)
        o_ref[0:rows, w:] = ob[...].astype(o_ref.dtype)

    @pl.when(i < n_p_tiles)
    def _():
        emit(oc_ref, os_ref, ow_ref, ob_ref, o_ref.shape[0])

    @pl.when(i >= n_p_tiles)
    def _():
        emit(soc_ref, sos_ref, sow_ref, sob_ref, n_s)


def _mix(prompt_parts, sample_parts, y, tm=256):
    n_prompt = prompt_parts[0].shape[0]
    n_s = sample_parts[0].shape[0]
    m = y.shape[0]
    tm = min(tm, n_prompt)
    assert n_prompt % tm == 0 and m == n_prompt + n_s and n_s <= tm
    n_p_tiles = n_prompt // tm
    wa = NSA_HEADS * HEAD_DIM
    wb = MOBA_HEADS * HEAD_DIM
    pmap = lambda i: (jnp.minimum(i, n_p_tiles - 1), 0)
    body = functools.partial(_mix_body, n_p_tiles=n_p_tiles, n_s=n_s)
    return pl.pallas_call(
        body,
        grid=(n_p_tiles + 1,),
        in_specs=[pl.BlockSpec((tm, wa), pmap)] * 3 + [pl.BlockSpec((tm, wb), pmap)]
        + [pl.BlockSpec((n_s, wa), lambda i: (0, 0))] * 3 + [pl.BlockSpec((n_s, wb), lambda i: (0, 0))]
        + [pl.BlockSpec((tm, LANE), lambda i: (i, CH_GATE))],
        out_specs=pl.BlockSpec((tm, wa + wb), lambda i: (i, 0)),
        out_shape=jax.ShapeDtypeStruct((m, wa + wb), BF),
        compiler_params=_cparams(("parallel",)),
        name="attn_mix",
    )(*prompt_parts, *sample_parts, y)


def _convmix_body(up_ref, hp_ref, uc_ref, hc_ref, pw_ref, ps_ref, dw_ref, dwb_ref, lng_ref, lnb_ref, pww_ref,
                  o_ref, zp_ref, zc_ref, cb_ref, *, ts, nt, zero_first, avail0, rc):
    s = pl.program_id(0)
    hp_rows = hp_ref.shape[0]
    hc_rows = hc_ref.shape[0]
    width = up_ref.shape[1]
    if zero_first:
        keep = jnp.where(s % nt == 0, 0.0, 1.0)
        zp_ref[0:hp_rows, :] = hp_ref[...] * keep
        zc_ref[0:hc_rows, :] = hc_ref[...] * keep
        avail = (s % nt) * ts + avail0
    else:
        zp_ref[0:hp_rows, :] = hp_ref[...]
        zc_ref[0:hc_rows, :] = hc_ref[...]
        avail = avail0
    zp_ref[hp_rows:, :] = up_ref[...]
    zc_ref[hc_rows:, :] = uc_ref[...]
    gwidth = width // len(POOL_WINDOWS)
    t_idx = lax.broadcasted_iota(jnp.int32, (ts, 1), 0) + avail + 1
    for gi, w in enumerate(POOL_WINDOWS):
        sl = slice(gi * gwidth, (gi + 1) * gwidth)
        cur = zp_ref[hp_rows:hp_rows + ts, sl]
        acc = cur
        for jj in range(1, w):
            acc = acc + zp_ref[hp_rows - jj:hp_rows - jj + ts, sl]
        cnt = jnp.minimum(t_idx, w).astype(F32)
        yg = acc / cnt - cur
        og = jnp.dot(yg.astype(BF), pw_ref[gi], preferred_element_type=F32) * ps_ref[:, sl]
        o_ref[:, sl] = og.astype(o_ref.dtype)
    base = hc_rows - (CONV_WIDTH - 1)
    for r in range(ts // rc):
        for c in range(width // LANE):
            cs = slice(c * LANE, (c + 1) * LANE)
            acc = jnp.zeros((rc, LANE), F32)
            for jj in range(CONV_WIDTH):
                lo = r * rc + base + jj
                acc = acc + zc_ref[lo:lo + rc, cs] * dw_ref[jj:jj + 1, cs]
            cb_ref[r * rc:(r + 1) * rc, cs] = acc + dwb_ref[:, cs]
    cv = cb_ref[...]
    mu = jnp.mean(cv, axis=-1, keepdims=True)
    xc = cv - mu
    yn = xc * lax.rsqrt(jnp.mean(xc * xc, axis=-1, keepdims=True) + NORM_EPS) * lng_ref[...] + lnb_ref[...]
    act = yn * jax.nn.sigmoid(yn)
    o_ref[:, width:] = jnp.dot(act.astype(BF), pww_ref[...], preferred_element_type=F32).astype(o_ref.dtype)


def _convmix(u_pool, halo_pool, u_conv, halo_conv, prm, *, ts, nt, n_tiles, zero_first, avail0, hp_rows, hc_rows,
             halo_blk):
    pw, ps, dw, dwb, lng, lnb, pww = prm
    width = u_pool.shape[1]
    rc = min(64, ts)
    body = functools.partial(_convmix_body, ts=ts, nt=nt, zero_first=zero_first, avail0=avail0, rc=rc)
    full = lambda a: pl.BlockSpec(a.shape, lambda s: (0,) * a.ndim)
    return pl.pallas_call(
        body,
        grid=(n_tiles,),
        in_specs=[pl.BlockSpec((ts, width), lambda s: (s, 0)),
                  pl.BlockSpec((hp_rows, width), lambda s: (halo_blk(s, hp_rows), 0)),
                  pl.BlockSpec((ts, width), lambda s: (s, 0)),
                  pl.BlockSpec((hc_rows, width), lambda s: (halo_blk(s, hc_rows), 0)),
                  full(pw), full(ps), full(dw), full(dwb), full(lng), full(lnb), full(pww)],
        out_specs=pl.BlockSpec((ts, 2 * width), lambda s: (s, 0)),
        out_shape=jax.ShapeDtypeStruct((n_tiles * ts, 2 * width), BF),
        scratch_shapes=[pltpu.VMEM((hp_rows + ts, width), F32), pltpu.VMEM((hc_rows + ts, width), F32),
                        pltpu.VMEM((ts, width), F32)],
        compiler_params=_cparams(("parallel",)),
        name="conv_mix",
    )(u_pool, halo_pool, u_conv, halo_conv, pw, ps, dw, dwb, lng, lnb, pww)


def _router_body(x_ref, g_ref, rw_ref, r_ref):
    x = x_ref[...]
    xn = x * lax.rsqrt(jnp.mean(x * x, axis=-1, keepdims=True) + NORM_EPS) * g_ref[...]
    logits = _dot3(xn, rw_ref[...])
    lane = lax.broadcasted_iota(jnp.int32, logits.shape, 1)
    l1 = jnp.where(lane < N_EXPERTS, logits, NEG_INF)
    m1 = jnp.max(l1, axis=-1, keepdims=True)
    i1 = jnp.min(jnp.where(l1 == m1, lane, LANE), axis=-1, keepdims=True)
    l2 = jnp.where(lane == i1, NEG_INF, l1)
    m2 = jnp.max(l2, axis=-1, keepdims=True)
    i2 = jnp.min(jnp.where(l2 == m2, lane, LANE), axis=-1, keepdims=True)
    e = jnp.exp(m2 - m1)
    w1 = 1.0 / (1.0 + e)
    w2 = e / (1.0 + e)
    r_ref[...] = jnp.where(lane == 0, i1.astype(F32),
                           jnp.where(lane == 1, i2.astype(F32), jnp.where(lane == 2, w1, jnp.where(lane == 3, w2, 0.0))))


def _router(x, g, rw, tm=256):
    m, d = x.shape
    rw_pad = jnp.pad(rw, ((0, 0), (0, LANE - rw.shape[1])))
    return pl.pallas_call(
        _router_body,
        grid=(pl.cdiv(m, tm),),
        in_specs=[pl.BlockSpec((tm, d), lambda i: (i, 0)), pl.BlockSpec((1, d), lambda i: (0, 0)),
                  pl.BlockSpec((d, LANE), lambda i: (0, 0))],
        out_specs=pl.BlockSpec((tm, LANE), lambda i: (i, 0)),
        out_shape=jax.ShapeDtypeStruct((m, LANE), F32),
        compiler_params=_cparams(("parallel",)),
        name="moe_router",
    )(x, g.reshape(1, d), rw_pad)


def _row_copy(src_hbm, row, dst, drow, sem):
    return pltpu.make_async_copy(src_hbm.at[pl.ds(row, 1), :], dst.at[pl.ds(drow, 1), :], sem)


def _gather_norm_body(src_ref, x_hbm, g_ref, o_ref, buf_ref, sem, *, tm):
    base = pl.program_id(0) * tm

    def issue(r, c):
        _row_copy(x_hbm, src_ref[base + r], buf_ref, r, sem).start()
        return c

    lax.fori_loop(0, tm, issue, 0)

    def wait(r, c):
        _row_copy(x_hbm, 0, buf_ref, r, sem).wait()
        return c

    lax.fori_loop(0, tm, wait, 0)
    x = buf_ref[...]
    ms = jnp.mean(x * x, axis=-1, keepdims=True)
    o_ref[...] = (x * lax.rsqrt(ms + NORM_EPS) * g_ref[...]).astype(o_ref.dtype)


def _gather_norm(x, src, g, tm):
    d = x.shape[1]
    r_tot = src.shape[0]
    body = functools.partial(_gather_norm_body, tm=tm)
    return pl.pallas_call(
        body,
        grid_spec=pltpu.PrefetchScalarGridSpec(
            num_scalar_prefetch=1,
            grid=(r_tot // tm,),
            in_specs=[pl.BlockSpec(memory_space=pl.ANY), pl.BlockSpec((1, d), lambda i, s: (0, 0))],
            out_specs=pl.BlockSpec((tm, d), lambda i, s: (i, 0)),
            scratch_shapes=[pltpu.VMEM((tm, d), F32), pltpu.SemaphoreType.DMA(())],
        ),
        out_shape=jax.ShapeDtypeStruct((r_tot, d), BF),
        compiler_params=_cparams(("arbitrary",)),
        name="moe_gather",
    )(src, x, g.reshape(1, d))


def _combine_body(p1_ref, p2_ref, eo_hbm, x_ref, r_ref, o_ref, b1_ref, b2_ref, sem, *, tm, n_tok):
    base = pl.program_id(0) * tm
    n = jnp.minimum(tm, n_tok - base)

    def issue(r, c):
        _row_copy(eo_hbm, p1_ref[base + r], b1_ref, r, sem).start()
        _row_copy(eo_hbm, p2_ref[base + r], b2_ref, r, sem).start()
        return c

    lax.fori_loop(0, n, issue, 0)

    def wait(r, c):
        _row_copy(eo_hbm, 0, b1_ref, r, sem).wait()
        _row_copy(eo_hbm, 0, b2_ref, r, sem).wait()
        return c

    lax.fori_loop(0, n, wait, 0)
    rr = r_ref[...]
    o_ref[...] = x_ref[...] + rr[:, 2:3] * b1_ref[...] + rr[:, 3:4] * b2_ref[...]


def _combine(x, eo, routing, p1, p2, tm_max=512):
    m, d = x.shape
    tm = max(t for t in range(SUBLANE, tm_max + 1, SUBLANE) if m % t == 0)
    body = functools.partial(_combine_body, tm=tm, n_tok=m)
    return pl.pallas_call(
        body,
        grid_spec=pltpu.PrefetchScalarGridSpec(
            num_scalar_prefetch=2,
            grid=(pl.cdiv(m, tm),),
            in_specs=[pl.BlockSpec(memory_space=pl.ANY), pl.BlockSpec((tm, d), lambda i, a, b: (i, 0)),
                      pl.BlockSpec((tm, LANE), lambda i, a, b: (i, 0))],
            out_specs=pl.BlockSpec((tm, d), lambda i, a, b: (i, 0)),
            scratch_shapes=[pltpu.VMEM((tm, d), F32), pltpu.VMEM((tm, d), F32), pltpu.SemaphoreType.DMA(())],
        ),
        out_shape=jax.ShapeDtypeStruct((m, d), F32),
        compiler_params=_cparams(("arbitrary",)),
        name="moe_combine",
    )(p1, p2, eo, x, routing)


def _moe_plan(e_idx, tm):
    n = e_idx.shape[0]
    n_asg = n * TOP_K
    flat_e = e_idx.reshape(-1)
    onehot = (flat_e[:, None] == jnp.arange(N_EXPERTS, dtype=jnp.int32)[None, :]).astype(jnp.int32)
    cnt = jnp.sum(onehot, axis=0)
    rank = jnp.take_along_axis(jnp.cumsum(onehot, axis=0) - onehot, flat_e[:, None], axis=1)[:, 0]
    cnt_p = ((cnt + tm - 1) // tm) * tm
    ends = jnp.cumsum(cnt_p)
    off = ends - cnt_p
    pos = off[flat_e] + rank
    n_tiles = (n_asg + N_EXPERTS * (tm - 1) + tm - 1) // tm
    r_tot = n_tiles * tm
    src = jnp.zeros((r_tot,), jnp.int32).at[pos].set(jnp.arange(n_asg, dtype=jnp.int32) // TOP_K)
    tile_start = jnp.arange(n_tiles, dtype=jnp.int32) * tm
    tval = (tile_start < ends[-1]).astype(jnp.int32)
    texp = jnp.minimum(jnp.searchsorted(ends, tile_start, side="right"), N_EXPERTS - 1).astype(jnp.int32)
    last = jnp.max(jnp.where(tval > 0, texp, 0))
    texp = jnp.where(tval > 0, texp, last)
    pos2 = pos.reshape(n, TOP_K)
    return src, pos2[:, 0], pos2[:, 1], texp, tval


def _moe(x, g, rw, wg, wu, wd, tm=256):
    routing = _router(x, g, rw)
    e_idx = routing[:, :TOP_K].astype(jnp.int32)
    src, p1, p2, texp, tval = _moe_plan(e_idx, tm)
    xs = _gather_norm(x, src, g, tm)
    hid = _gmm(xs, [(wg, 0), (wu, 0)], wg.shape[2], epi="swiglu", texp=texp, tval=tval, out_dtype=BF, tm=tm, tn=1024,
               name="moe_up")
    eo = _gmm(hid, [(wd, 0)], wd.shape[2], texp=texp, tval=tval, tm=tm, name="moe_down")
    return _combine(x, eo, routing, p1, p2)


def _attn_layer(x, n_prompt, nb_p, t_p, nb_s, li, prm, caches, page_table):
    (attn_norm, w_in, qk_g, cmp_pe, cmp_w1, cmp_w2, moba_g, w_out, ffn_norm, w_gate, w_up, w_down) = prm
    cache_cmp, cache_sel, state_win, cache_moba = caches
    d = x.shape[1]
    past_len = page_table.shape[1] * cache_moba.shape[2]

    n_in = w_in.shape[1]
    g_lo, g_hi = CH_QB * LANE, CH_QB * LANE + 3 * NSA_HEADS
    w_re = jnp.concatenate([w_in[:, :g_lo], w_in[:, g_hi:], w_in[:, g_lo:g_hi],
                            jnp.zeros((d, ATTN_CHUNKS * LANE - n_in), F32)], axis=1)[None]
    ones = jnp.ones((HEAD_DIM,), F32)
    zeros = jnp.zeros((HEAD_DIM,), F32)
    chunk_gain = ([qk_g[0]] * 8 + [ones] * 4 + [qk_g[2]] * 2 + [ones] * 2 + [qk_g[3]] * 2 + [ones] * 2
                  + [moba_g[0]] * 8 + [moba_g[1]] * 8 + [ones] * 9)
    chunk_flag = ([ones] * 8 + [zeros] * 4 + [ones] * 2 + [zeros] * 2 + [ones] * 2 + [zeros] * 2
                  + [ones] * 16 + [zeros] * 9)
    gain = jnp.concatenate(chunk_gain)[None]
    flag = jnp.concatenate(chunk_flag)[None]
    xn = _rmsnorm(x, attn_norm)
    y, y16 = _gmm(xn, [(w_re, 0)], ATTN_CHUNKS * LANE, epi="headnorm", gain=gain, flag=flag, out_dtype=(F32, BF),
                  tm=512, tn=9 * LANE, name="attn_in")

    r = NSA_CMP_LEN // NSA_CMP_STRIDE
    wcat = cmp_w1.reshape(2, r, NSA_CMP_STRIDE, HEAD_DIM, NSA_CMP_HIDDEN).transpose(0, 2, 3, 1, 4)
    wcat = wcat.reshape(2, NSA_CMP_STRIDE, HEAD_DIM, r * NSA_CMP_HIDDEN).astype(BF)
    pe_rows = cmp_pe.reshape(2, r, NSA_CMP_STRIDE, HEAD_DIM).transpose(0, 2, 1, 3)
    pe_rows = jnp.pad(pe_rows, ((0, 0), (0, 0), (0, SUBLANE - r), (0, 0))).astype(BF)
    w2 = cmp_w2.astype(BF)
    gk = qk_g[1][None]

    kvc_p = _compress_c2(_compress_c1_prompt(y, wcat, nb_p, t_p), pe_rows, wcat, w2, gk)
    tq = min(256, t_p)
    rows = n_prompt
    oc, sel_a = _cattn(y, 0, 0, kvc_p, nb_p, t_p, tq, t_p, 0, rows)
    sel_b = _moba_gate_prompt(y, nb_p, t_p, tq)
    os_ = _flash(y, y16, sel_a, nb_p, t_p, q_ch=CH_QA, k_ch=CH_KS, v_ch=CH_VS, kvh=NSA_KV, nh=NSA_HPG,
                 blk=NSA_SEL_BLOCK, window=0, out_rows=rows)
    ow = _flash(y, y16, None, nb_p, t_p, q_ch=CH_QA, k_ch=CH_KW, v_ch=CH_VW, kvh=NSA_KV, nh=NSA_HPG, blk=0,
                window=NSA_WINDOW, out_rows=rows)
    ob = _flash(y, y16, sel_b, nb_p, t_p, q_ch=CH_QB, k_ch=CH_KB, v_ch=CH_VB, kvh=MOBA_HEADS, nh=1, blk=MOBA_BLOCK,
                window=0, out_rows=rows)

    n_pad = x.shape[0] - n_prompt
    ys = y[n_prompt:]
    kvc_s = _compress_c2(_compress_c1_sample(cache_cmp, li, page_table, wcat), pe_rows, wcat, w2, gk)
    q_pad = jnp.zeros((nb_s, SUBLANE, NSA_HEADS * HEAD_DIM), F32).at[:, 0].set(ys[:nb_s, :NSA_HEADS * HEAD_DIM])
    oc_s, sel_s = _cattn_sample(q_pad.reshape(nb_s * SUBLANE, -1), kvc_s, nb_s, past_len + 1, past_len)
    oc_s = oc_s.reshape(nb_s, SUBLANE, -1)[:, 0]
    n_sel = (past_len + 1) // NSA_SEL_BLOCK
    k_sel = min(NSA_SEL_COUNT - 1, n_sel)
    mask_a = sel_s[:, :, 0, :n_sel]
    idx_a = jnp.argsort(-mask_a, axis=-1, stable=True)[..., :k_sel].astype(jnp.int32)
    vld_a = (jnp.take_along_axis(mask_a, idx_a, axis=-1) > 0).astype(jnp.int32)
    page = cache_sel.shape[2]
    bpp = page // NSA_SEL_BLOCK
    pt_b = page_table[:, None, :]
    tbl_a = jnp.take_along_axis(jnp.broadcast_to(pt_b, (nb_s, NSA_KV, pt_b.shape[-1])), idx_a // bpp, axis=-1) * bpp + idx_a % bpp
    os_s = _decode_attn(ys, cache_sel, li, tbl_a.astype(jnp.int32), vld_a, q_ch=CH_QA, kn_ch=CH_KS, vn_ch=CH_VS,
                        nh=NSA_HPG, rows=NSA_SEL_BLOCK, kvh=NSA_KV, r_min=0, nblk=_largest_divisor(k_sel, 5),
                        name="dec_sel")
    win_buf = state_win.shape[2]
    tbl_w = jnp.broadcast_to(jnp.arange(nb_s, dtype=jnp.int32)[:, None, None], (nb_s, NSA_KV, 1))
    ow_s = _decode_attn(ys, state_win, li, tbl_w, jnp.ones_like(tbl_w), q_ch=CH_QA, kn_ch=CH_KW, vn_ch=CH_VW,
                        nh=NSA_HPG, rows=win_buf, kvh=NSA_KV, r_min=max(0, win_buf - (NSA_WINDOW - 1)), nblk=1,
                        name="dec_win")
    means = _moba_means_sample(cache_moba, li, page_table)
    qb_pad = jnp.zeros((nb_s, SUBLANE, MOBA_HEADS * HEAD_DIM), F32).at[:, 0].set(
        ys[:nb_s, CH_QB * LANE:(CH_QB + MOBA_HEADS) * LANE])
    sel_m = _moba_gate_sample(qb_pad.reshape(nb_s * SUBLANE, -1), means, past_len)
    n_full = means.shape[2]
    k_top = min(MOBA_TOPK, n_full)
    mask_b = sel_m[:, :, :n_full, 0]
    idx_b = jnp.argsort(-mask_b, axis=-1, stable=True)[..., :k_top].astype(jnp.int32)
    vld_b = (jnp.take_along_axis(mask_b, idx_b, axis=-1) > 0).astype(jnp.int32)
    ppb = MOBA_BLOCK // page
    pg_b = (idx_b[..., None] * ppb + jnp.arange(ppb, dtype=jnp.int32)).reshape(nb_s, MOBA_HEADS, k_top * ppb)
    tbl_b = jnp.take_along_axis(jnp.broadcast_to(pt_b, (nb_s, MOBA_HEADS, pt_b.shape[-1])), pg_b, axis=-1)
    vld_b = jnp.repeat(vld_b, ppb, axis=-1)
    ob_s = _decode_attn(ys, cache_moba, li, tbl_b.astype(jnp.int32), vld_b, q_ch=CH_QB, kn_ch=CH_KB, vn_ch=CH_VB,
                        nh=1, rows=page, kvh=MOBA_HEADS, r_min=0, nblk=_largest_divisor(k_top * ppb, 3),
                        name="dec_moba")

    def rows_s(part, nh):
        return jnp.pad(part[:, :, :nh].reshape(nb_s, -1), ((0, n_pad - nb_s), (0, 0)))

    sample_parts = (jnp.pad(oc_s, ((0, n_pad - nb_s), (0, 0))), rows_s(os_s, NSA_HPG), rows_s(ow_s, NSA_HPG),
                    rows_s(ob_s, 1))
    mixed = _mix((oc, os_, ow, ob), sample_parts, y)
    x = _gmm(mixed, [(w_out[None], 0)], d, resid=x, tm=512, tn=1024, name="attn_out")
    hid = _gmm(_rmsnorm(x, ffn_norm), [(w_gate[None], 0), (w_up[None], 0)], w_gate.shape[1], epi="swiglu",
               out_dtype=BF, tm=512, tn=512, name="ffn_up")
    x = _gmm(hid, [(w_down[None], 0)], d, resid=x, tm=512, tn=512, name="ffn_down")

    def rows_of(lo_ch, n_ch):
        return y[:, lo_ch * LANE:(lo_ch + n_ch) * LANE]

    def split(a, kvh):
        ap = a[:n_prompt].reshape(nb_p, t_p, 2, kvh, HEAD_DIM)
        as_ = a[n_prompt:n_prompt + nb_s].reshape(nb_s, 1, 2, kvh, HEAD_DIM)
        return ap, as_

    cmp_p, cmp_s = split(rows_of(CH_KC, 4), NSA_KV)
    sel_p, sel_s_rows = split(rows_of(CH_KS, 4), NSA_KV)
    win_p, win_s = split(rows_of(CH_KW, 4), NSA_KV)
    moba_p, moba_s = split(rows_of(CH_KB, 16), MOBA_HEADS)
    new_win_p = win_p[:, t_p - min(NSA_WINDOW, t_p):]
    win_all = jnp.concatenate([state_win[li], win_s], axis=1)
    keep = min(NSA_WINDOW, past_len + 1)
    new_win_s = win_all[:, win_all.shape[1] - keep:]
    return x, (cmp_p, cmp_s, sel_p, sel_s_rows, new_win_p, new_win_s, moba_p, moba_s)


def _conv_layer(x, n_prompt, nb_p, t_p, nb_s, prm, states):
    (conv_norm, w_in, pool_w, pool_scale, dw, dw_b, ln_g, ln_b, pw, w_out, moe_norm, router_w, wg, wu, wd) = prm
    state_pool, state_conv = states
    d = x.shape[1]
    width = pool_w.shape[0] * pool_w.shape[1]
    xn = _rmsnorm(x, conv_norm)
    w3 = w_in[None]
    u_pool = _gmm(xn, [(w3, 0)], width, tm=512, tn=width, name="conv_in_pool")
    u_conv = _gmm(xn, [(w3, width), (w3, 2 * width)], width, epi="glu", tm=512, tn=512, name="conv_in_glu")
    prm_mix = (pool_w.astype(BF), pool_scale[None], jnp.pad(dw, ((0, 1), (0, 0))), dw_b[None], ln_g[None], ln_b[None],
               pw.astype(BF))
    pool_buf = max(POOL_WINDOWS) - 1
    conv_buf = CONV_WIDTH - 1
    hp_rows, hc_rows = 16, 32
    ts = min(256, t_p)
    nt = t_p // ts
    mixed_p = _convmix(u_pool, u_pool, u_conv, u_conv, prm_mix, ts=ts, nt=nt, n_tiles=nb_p * nt, zero_first=True,
                       avail0=0, hp_rows=hp_rows, hc_rows=hc_rows,
                       halo_blk=lambda s, hr: jnp.maximum(s * (ts // hr) - 1, 0))
    rows = x.shape[0]
    us_pool = jnp.zeros((nb_s, SUBLANE, width), F32).at[:, 0].set(u_pool[n_prompt:n_prompt + nb_s])
    us_conv = jnp.zeros((nb_s, SUBLANE, width), F32).at[:, 0].set(u_conv[n_prompt:n_prompt + nb_s])
    hs_pool = jnp.pad(state_pool, ((0, 0), (hp_rows - pool_buf, 0), (0, 0))).reshape(nb_s * hp_rows, width)
    hs_conv = jnp.pad(state_conv, ((0, 0), (hc_rows - conv_buf, 0), (0, 0))).reshape(nb_s * hc_rows, width)
    mixed_s = _convmix(us_pool.reshape(nb_s * SUBLANE, width), hs_pool, us_conv.reshape(nb_s * SUBLANE, width), hs_conv,
                       prm_mix, ts=SUBLANE, nt=1, n_tiles=nb_s, zero_first=False, avail0=pool_buf, hp_rows=hp_rows,
                       hc_rows=hc_rows, halo_blk=lambda s, hr: s)
    mixed_s = mixed_s.reshape(nb_s, SUBLANE, 2 * width)[:, 0]
    mixed = jnp.concatenate([mixed_p, jnp.pad(mixed_s, ((0, rows - n_prompt - nb_s), (0, 0)))], axis=0)
    x = _gmm(mixed, [(w_out[None], 0)], d, resid=x, tm=512, tn=1024, name="conv_out")
    x = _moe(x, moe_norm, router_w, wg, wu, wd)

    up = u_pool[:n_prompt].reshape(nb_p, t_p, width)
    uc = u_conv[:n_prompt].reshape(nb_p, t_p, width)
    new_pool_p = up[:, t_p - pool_buf:]
    assert t_p >= conv_buf and t_p >= pool_buf
    new_conv_p = uc[:, t_p - conv_buf:]
    new_pool_s = jnp.concatenate([state_pool, u_pool[n_prompt:n_prompt + nb_s][:, None]], axis=1)[:, 1:]
    new_conv_s = jnp.concatenate([state_conv, u_conv[n_prompt:n_prompt + nb_s][:, None]], axis=1)[:, 1:]
    return x, (new_pool_p, new_pool_s, new_conv_p, new_conv_s)


def kernel(x_prompt, x_sample, cache_nsa_cmp, cache_nsa_sel, state_nsa_win, cache_moba, state_pool, state_conv, page_table, attn_norm, w_attn_in, nsa_qk_norm, nsa_cmp_pe, nsa_cmp_w1, nsa_cmp_w2, moba_qk_norm, w_attn_out, ffn_norm, ffn_w_gate, ffn_w_up, ffn_w_down, conv_norm, w_conv_in, pool_w, pool_scale, conv_dw, conv_dw_b, conv_ln_g, conv_ln_b, conv_pw, w_conv_out, moe_norm, router_w, moe_w_gate, moe_w_up, moe_w_down):
    nb_p, t_p, d = x_prompt.shape
    nb_s = x_sample.shape[0]
    n_prompt = nb_p * t_p
    n_pad = 2 * SUBLANE
    assert x_sample.shape[1] == 1 and nb_s <= n_pad
    x = jnp.concatenate([x_prompt.reshape(n_prompt, d), x_sample.reshape(nb_s, d),
                         jnp.zeros((n_pad - nb_s, d), F32)], axis=0)
    li = 0
    prm_a = (attn_norm[li], w_attn_in[li], nsa_qk_norm[li], nsa_cmp_pe[li], nsa_cmp_w1[li], nsa_cmp_w2[li],
             moba_qk_norm[li], w_attn_out[li], ffn_norm[li], ffn_w_gate[li], ffn_w_up[li], ffn_w_down[li])
    x, attn_new = _attn_layer(x, n_prompt, nb_p, t_p, nb_s, li, prm_a,
                              (cache_nsa_cmp, cache_nsa_sel, state_nsa_win, cache_moba), page_table)
    prm_c = (conv_norm[li], w_conv_in[li], pool_w[li], pool_scale[li], conv_dw[li], conv_dw_b[li], conv_ln_g[li],
             conv_ln_b[li], conv_pw[li], w_conv_out[li], moe_norm[li], router_w[li], moe_w_gate[li], moe_w_up[li],
             moe_w_down[li])
    x, conv_new = _conv_layer(x, n_prompt, nb_p, t_p, nb_s, prm_c, (state_pool[li], state_conv[li]))
    cmp_p, cmp_s, sel_p, sel_s, win_p, win_s, moba_p, moba_s = attn_new
    pool_p, pool_s, conv_p, conv_s = conv_new
    y_p = x[:n_prompt].reshape(nb_p, t_p, d)
    y_s = x[n_prompt:n_prompt + nb_s].reshape(nb_s, 1, d)
    st = lambda a: a[None]
    return (y_p, y_s, st(cmp_p), st(cmp_s), st(sel_p), st(sel_s), st(win_p), st(win_s), st(moba_p), st(moba_s),
            st(pool_p), st(pool_s), st(conv_p), st(conv_s))
```

```python
import functools
import math

import jax
import jax.numpy as jnp
from jax import lax
from jax.experimental import pallas as pl
from jax.experimental.pallas import tpu as pltpu

F32 = jnp.float32
BF = jnp.bfloat16
NEG_INF = float("-inf")

HEAD_DIM = 128
NORM_EPS = 1e-6
NSA_HEADS = 8
NSA_KV = 2
NSA_HPG = NSA_HEADS // NSA_KV
NSA_CMP_LEN = 32
NSA_CMP_STRIDE = 16
NSA_CMP_HIDDEN = 2 * HEAD_DIM
NSA_SEL_BLOCK = 64
NSA_SEL_COUNT = 16
NSA_WINDOW = 512
NSA_SEL_FORCE = 1.0e4
MOBA_HEADS = 8
MOBA_BLOCK = 256
MOBA_TOPK = 3
POOL_WINDOWS = (2, 4, 8, 16)
CONV_WIDTH = 31
N_EXPERTS = 8
TOP_K = 2

LANE = 128
SUBLANE = 8
VMEM_LIMIT = 58 * 1024 * 1024

CH_QA, CH_KC, CH_VC, CH_KS, CH_VS, CH_KW, CH_VW, CH_QB, CH_KB, CH_VB, CH_GATE = 0, 8, 10, 12, 14, 16, 18, 20, 28, 36, 44
ATTN_CHUNKS = 45

NT_DIMS = (((1,), (1,)), ((), ()))


def _cparams(sem):
    return pltpu.CompilerParams(dimension_semantics=sem, vmem_limit_bytes=VMEM_LIMIT)


def _largest_divisor(n, cap):
    return max(k for k in range(1, cap + 1) if n % k == 0)


def _split_bf16(a):
    hi = a.astype(BF)
    lo = (a - hi.astype(F32)).astype(BF)
    return hi, lo


def _dot3(a, b, dims=(((1,), (0,)), ((), ()))):
    ah, al = _split_bf16(a)
    bh, bl = _split_bf16(b)
    d = lambda x, y: lax.dot_general(x, y, dims, preferred_element_type=F32)
    return d(ah, bh) + d(ah, bl) + d(al, bh)


def _masked_softmax(s, mask, axis):
    s = jnp.where(mask, s, NEG_INF)
    m = jnp.max(s, axis=axis, keepdims=True)
    m = jnp.where(m == NEG_INF, 0.0, m)
    p = jnp.exp(s - m)
    d = jnp.sum(p, axis=axis, keepdims=True)
    return p / jnp.where(d > 0, d, 1.0)


def _rmsnorm_body(x_ref, g_ref, o_ref):
    x = x_ref[...]
    ms = jnp.mean(x * x, axis=-1, keepdims=True)
    o_ref[...] = (x * lax.rsqrt(ms + NORM_EPS) * g_ref[...]).astype(o_ref.dtype)


def _rmsnorm(x, g, tm=256):
    m, d = x.shape
    return pl.pallas_call(
        _rmsnorm_body,
        grid=(pl.cdiv(m, tm),),
        in_specs=[pl.BlockSpec((tm, d), lambda i: (i, 0)), pl.BlockSpec((1, d), lambda i: (0, 0))],
        out_specs=pl.BlockSpec((tm, d), lambda i: (i, 0)),
        out_shape=jax.ShapeDtypeStruct((m, d), BF),
        compiler_params=_cparams(("parallel",)),
        name="rmsnorm",
    )(x, g.reshape(1, d))


def _gmm_body(texp_ref, tval_ref, *refs, n_w, epi, has_resid, tn, n_o):
    x_ref = refs[0]
    w_refs = refs[1:1 + n_w]
    pos = 1 + n_w
    if epi == "headnorm":
        gain_ref, flag_ref = refs[pos], refs[pos + 1]
        pos += 2
    if has_resid:
        resid_ref = refs[pos]
        pos += 1
    o_refs = refs[pos:pos + n_o]
    o_ref = o_refs[0]
    wb_refs = refs[pos + n_o:pos + n_o + n_w]

    i = pl.program_id(1)
    changed = jnp.logical_or(i == 0, texp_ref[i] != texp_ref[jnp.maximum(i - 1, 0)])

    @pl.when(changed)
    def _():
        for k in range(n_w):
            wb_refs[k][...] = w_refs[k][...].astype(BF)

    @pl.when(tval_ref[i] == 0)
    def _():
        for o in o_refs:
            o[...] = jnp.zeros(o.shape, o.dtype)

    @pl.when(tval_ref[i] > 0)
    def _():
        x = x_ref[...]
        a = jnp.dot(x, wb_refs[0][...], preferred_element_type=F32)
        if epi == "swiglu":
            b = jnp.dot(x, wb_refs[1][...], preferred_element_type=F32)
            y = a * jax.nn.sigmoid(a) * b
        elif epi == "glu":
            b = jnp.dot(x, wb_refs[1][...], preferred_element_type=F32)
            y = a * jax.nn.sigmoid(b)
        else:
            y = a
        if has_resid:
            y = y + resid_ref[...]
        if epi == "headnorm":
            for c in range(tn // LANE):
                sl = slice(c * LANE, (c + 1) * LANE)
                yc = y[:, sl]
                r = lax.rsqrt(jnp.mean(yc * yc, axis=-1, keepdims=True) + NORM_EPS)
                f = flag_ref[:, sl]
                yn = yc * (f * r + (1.0 - f)) * gain_ref[:, sl]
                for o in o_refs:
                    o[:, sl] = yn.astype(o.dtype)
        else:
            for o in o_refs:
                o[...] = y.astype(o.dtype)


def _gmm(x, ws, n_out, *, epi="none", texp=None, tval=None, resid=None, gain=None, flag=None,
         out_dtype=F32, tm=256, tn=512, name="gmm"):
    m, kdim = x.shape
    tn = min(tn, n_out)
    n_m = pl.cdiv(m, tm)
    n_n = n_out // tn
    assert n_n * tn == n_out
    if texp is None:
        texp = jnp.zeros((n_m,), jnp.int32)
        tval = jnp.ones((n_m,), jnp.int32)
    n_w = len(ws)
    in_specs = [pl.BlockSpec((tm, kdim), lambda j, i, te, tv: (i, 0))]
    args = [x]
    for w, off in ws:
        assert off % tn == 0 and w.shape[1] == kdim
        ob = off // tn
        in_specs.append(pl.BlockSpec((None, kdim, tn), lambda j, i, te, tv, ob=ob: (te[i], 0, j + ob)))
        args.append(w)
    if epi == "headnorm":
        in_specs += [pl.BlockSpec((1, tn), lambda j, i, te, tv: (0, j))] * 2
        args += [gain, flag]
    if resid is not None:
        in_specs.append(pl.BlockSpec((tm, tn), lambda j, i, te, tv: (i, j)))
        args.append(resid)
    dtypes = out_dtype if isinstance(out_dtype, tuple) else (out_dtype,)
    body = functools.partial(_gmm_body, n_w=n_w, epi=epi, has_resid=resid is not None, tn=tn, n_o=len(dtypes))
    outs = pl.pallas_call(
        body,
        grid_spec=pltpu.PrefetchScalarGridSpec(
            num_scalar_prefetch=2,
            grid=(n_n, n_m),
            in_specs=in_specs,
            out_specs=[pl.BlockSpec((tm, tn), lambda j, i, te, tv: (i, j)) for _ in dtypes],
            scratch_shapes=[pltpu.VMEM((kdim, tn), BF) for _ in range(n_w)],
        ),
        out_shape=[jax.ShapeDtypeStruct((m, n_out), dt) for dt in dtypes],
        compiler_params=_cparams(("arbitrary", "arbitrary")),
        name=name,
    )(texp, tval, *args)
    return outs if isinstance(out_dtype, tuple) else outs[0]


def _c1_body(x_ref, w_ref, o_ref):
    nsb = x_ref.shape[0] // NSA_CMP_STRIDE
    acc = jnp.zeros((nsb, 2 * NSA_CMP_HIDDEN), F32)
    for s in range(NSA_CMP_STRIDE):
        xs = x_ref[pl.ds(s, nsb, stride=NSA_CMP_STRIDE), :]
        acc = acc + jnp.dot(xs.astype(BF), w_ref[s], preferred_element_type=F32)
    o_ref[...] = acc


def _compress_c1_prompt(y, wcat, nb, t):
    nsub = t // NSA_CMP_STRIDE
    return pl.pallas_call(
        _c1_body,
        grid=(nb, 2, NSA_KV),
        in_specs=[pl.BlockSpec((t, LANE), lambda b, kv, g: (b, CH_KC + 2 * kv + g)),
                  pl.BlockSpec((None, NSA_CMP_STRIDE, HEAD_DIM, 2 * NSA_CMP_HIDDEN), lambda b, kv, g: (kv, 0, 0, 0))],
        out_specs=pl.BlockSpec((None, None, None, nsub, 2 * NSA_CMP_HIDDEN), lambda b, kv, g: (b, kv, g, 0, 0)),
        out_shape=jax.ShapeDtypeStruct((nb, 2, NSA_KV, nsub, 2 * NSA_CMP_HIDDEN), F32),
        compiler_params=_cparams(("parallel", "parallel", "parallel")),
        name="cmp_c1_prompt",
    )(y, wcat)


C1_PAGES = 16


def _c1_sample_body(pt_ref, *refs, n_in, page):
    x_refs = refs[:n_in]
    w_ref, o_ref, scr = refs[n_in:n_in + 3]
    nsb = page // NSA_CMP_STRIDE
    rows = n_in * nsb
    for kv in range(2):
        acc = jnp.zeros((rows * NSA_KV, 2 * NSA_CMP_HIDDEN), F32)

        def rows_at(s):
            parts = [xr[pl.ds(s, nsb, stride=NSA_CMP_STRIDE), kv, :, :].reshape(nsb * NSA_KV, HEAD_DIM) for xr in x_refs]
            return (parts[0] if n_in == 1 else jnp.concatenate(parts, axis=0)).astype(BF)

        for s in range(0, NSA_CMP_STRIDE, 2):
            xs = jnp.concatenate([rows_at(s), rows_at(s + 1)], axis=1)
            w = w_ref[kv, s:s + 2].reshape(2 * HEAD_DIM, 2 * NSA_CMP_HIDDEN)
            acc = acc + jnp.dot(xs, w, preferred_element_type=F32)
        for c in range(scr.shape[0]):
            scr[c] = acc[:, c * LANE:(c + 1) * LANE]
        for g in range(NSA_KV):
            for c in range(scr.shape[0]):
                o_ref[kv, g, :, c * LANE:(c + 1) * LANE] = scr[c, pl.ds(g, rows, stride=NSA_KV), :]


def _compress_c1_sample(cache, li, page_table, wcat):
    nb, n_pages = page_table.shape
    page = cache.shape[2]
    npp = math.gcd(C1_PAGES, n_pages)
    nsb = page // NSA_CMP_STRIDE
    nsub = n_pages * nsb
    body = functools.partial(_c1_sample_body, n_in=npp, page=page)
    in_specs = [
        pl.BlockSpec((None, None, page, 2, NSA_KV, HEAD_DIM), lambda b, j, pt, i=i: (li, pt[b, j * npp + i], 0, 0, 0, 0))
        for i in range(npp)
    ]
    in_specs.append(pl.BlockSpec(wcat.shape, lambda b, j, pt: (0, 0, 0, 0)))
    return pl.pallas_call(
        body,
        grid_spec=pltpu.PrefetchScalarGridSpec(
            num_scalar_prefetch=1,
            grid=(nb, n_pages // npp),
            in_specs=in_specs,
            out_specs=pl.BlockSpec((None, 2, NSA_KV, npp * nsb, 2 * NSA_CMP_HIDDEN), lambda b, j, pt: (b, 0, 0, j, 0)),
            scratch_shapes=[pltpu.VMEM((2 * NSA_CMP_HIDDEN // LANE, npp * nsb * NSA_KV, LANE), F32)],
        ),
        out_shape=jax.ShapeDtypeStruct((nb, 2, NSA_KV, nsub, 2 * NSA_CMP_HIDDEN), F32),
        compiler_params=_cparams(("parallel", "parallel")),
        name="cmp_c1_sample",
    )(page_table, *([cache] * npp), wcat)


def _c2_body(p_ref, pe_ref, wcat_ref, w2_ref, g_ref, o_ref, *, nsub, n_c):
    kv = pl.program_id(1)
    pep = jnp.zeros((SUBLANE, 2 * NSA_CMP_HIDDEN), F32)
    for s in range(NSA_CMP_STRIDE):
        pep = pep + jnp.dot(pe_ref[s], wcat_ref[s], preferred_element_type=F32)
    bias = pep[0:1, :NSA_CMP_HIDDEN] + pep[1:2, NSA_CMP_HIDDEN:]
    nxt = pltpu.roll(p_ref[:, NSA_CMP_HIDDEN:], nsub - 1, 0)
    h = p_ref[:, :NSA_CMP_HIDDEN] + nxt + bias
    h = h * jax.nn.sigmoid(h)
    o = jnp.dot(h.astype(BF), w2_ref[...], preferred_element_type=F32)
    r = lax.rsqrt(jnp.mean(o * o, axis=-1, keepdims=True) + NORM_EPS)
    o = jnp.where(kv == 0, o * r * g_ref[...], o)
    row = lax.broadcasted_iota(jnp.int32, (nsub, 1), 0)
    o_ref[...] = jnp.where(row < n_c, o, 0.0)


def _compress_c2(p, pe_rows, wcat, w2, gk):
    nb, _, _, nsub, _ = p.shape
    n_c = nsub - NSA_CMP_LEN // NSA_CMP_STRIDE + 1
    body = functools.partial(_c2_body, nsub=nsub, n_c=n_c)
    return pl.pallas_call(
        body,
        grid=(nb, 2, NSA_KV),
        in_specs=[pl.BlockSpec((None, None, None, nsub, 2 * NSA_CMP_HIDDEN), lambda b, kv, g: (b, kv, g, 0, 0)),
                  pl.BlockSpec((None, NSA_CMP_STRIDE, SUBLANE, HEAD_DIM), lambda b, kv, g: (kv, 0, 0, 0)),
                  pl.BlockSpec((None, NSA_CMP_STRIDE, HEAD_DIM, 2 * NSA_CMP_HIDDEN), lambda b, kv, g: (kv, 0, 0, 0)),
                  pl.BlockSpec((None, NSA_CMP_HIDDEN, HEAD_DIM), lambda b, kv, g: (kv, 0, 0)),
                  pl.BlockSpec((1, HEAD_DIM), lambda b, kv, g: (0, 0))],
        out_specs=pl.BlockSpec((None, None, None, nsub, HEAD_DIM), lambda b, kv, g: (b, kv, g, 0, 0)),
        out_shape=jax.ShapeDtypeStruct((nb, 2, NSA_KV, nsub, HEAD_DIM), F32),
        compiler_params=_cparams(("parallel", "parallel", "parallel")),
        name="cmp_c2",
    )(p, pe_rows, wcat, w2, gk)


def _rank_select(sc_ref, n_iter, k):
    score = sc_ref[...]
    brow = lax.broadcasted_iota(jnp.int32, score.shape, 0)

    def body(j, rank):
        r = sc_ref[pl.ds(j, 1), :]
        beats = jnp.logical_or(r > score, jnp.logical_and(r == score, j < brow))
        return rank + jnp.where(beats, 1.0, 0.0)

    rank = lax.fori_loop(0, n_iter, body, jnp.zeros(score.shape, F32))
    return rank < k


def _cattn_body(q_ref, kc_ref, vc_ref, oc_ref, sel_ref, sc_ref, *, tq, ncp, n_c, ns, nsp, pos0, ksel):
    t0 = pos0 + pl.program_id(2) * tq
    scale = HEAD_DIM ** -0.5
    kc = kc_ref[...].astype(BF)
    vc = vc_ref[...].astype(BF)
    trow = t0 + lax.broadcasted_iota(jnp.int32, (tq, 1), 0)
    ncol = lax.broadcasted_iota(jnp.int32, (1, ncp), 1)
    valid = jnp.logical_and(ncol * NSA_CMP_STRIDE + (NSA_CMP_LEN - 1) <= trow, ncol < n_c)
    tcol = t0 + lax.broadcasted_iota(jnp.int32, (1, tq), 1)
    nrow = lax.broadcasted_iota(jnp.int32, (ncp, 1), 0)
    valid_t = jnp.logical_and(nrow * NSA_CMP_STRIDE + (NSA_CMP_LEN - 1) <= tcol, nrow < n_c)
    psum_t = jnp.zeros((ncp, tq), F32)
    for z in range(NSA_HPG):
        sl = slice(z * HEAD_DIM, (z + 1) * HEAD_DIM)
        q = (q_ref[:, sl] * scale).astype(BF)
        s = lax.dot_general(q, kc, NT_DIMS, preferred_element_type=F32)
        p = _masked_softmax(s, valid, -1)
        oc_ref[:, sl] = jnp.dot(p.astype(BF), vc, preferred_element_type=F32)
        s_t = lax.dot_general(kc, q, NT_DIMS, preferred_element_type=F32)
        psum_t = psum_t + _masked_softmax(s_t, valid_t, 0)
    r = NSA_SEL_BLOCK // NSA_CMP_STRIDE
    brow = lax.broadcasted_iota(jnp.int32, (nsp, 1), 0)
    lo = r * brow - 1
    inside = jnp.logical_and(ncol >= lo, ncol <= lo + r)
    edge = jnp.logical_or(ncol == lo, ncol == lo + r)
    m_t = jnp.where(inside, jnp.where(edge, 0.5, 1.0), 0.0).astype(BF)
    hi = psum_t.astype(BF)
    mid = (psum_t - hi.astype(F32)).astype(BF)
    low = (psum_t - hi.astype(F32) - mid.astype(F32)).astype(BF)
    imp_t = (jnp.dot(m_t, hi, preferred_element_type=F32) + jnp.dot(m_t, mid, preferred_element_type=F32)
             + jnp.dot(m_t, low, preferred_element_type=F32))
    own = tcol // NSA_SEL_BLOCK
    past = jnp.logical_and(brow < own, brow < ns)
    forced = jnp.logical_or(brow == 0, brow == own - 1)
    sc_ref[...] = jnp.where(past, jnp.where(forced, NSA_SEL_FORCE, imp_t), NEG_INF)
    chosen = jnp.logical_and(past, _rank_select(sc_ref, ns, ksel))
    sel = jnp.where(jnp.logical_or(chosen, brow == own), 1.0, 0.0)
    rows = sel_ref.shape[0]
    if rows > nsp:
        sel_ref[...] = jnp.zeros(sel_ref.shape, F32)
    sel_ref[0:nsp, :] = sel


def _cattn(q_arr, q_row_blk0, q_col_blk0, kvc, nb, tlen, tq, length, pos0, out_rows):
    nq = tlen // tq
    ncp = kvc.shape[3]
    n_c = ncp - NSA_CMP_LEN // NSA_CMP_STRIDE + 1
    ns = length // NSA_SEL_BLOCK
    ksel = min(NSA_SEL_COUNT - 1, ns)
    nsp = -(-ns // SUBLANE) * SUBLANE
    sel_rows = max(nsp, LANE)
    body = functools.partial(_cattn_body, tq=tq, ncp=ncp, n_c=n_c, ns=ns, nsp=nsp, pos0=pos0, ksel=ksel)
    gw = NSA_HPG * HEAD_DIM
    return pl.pallas_call(
        body,
        grid=(nb, NSA_KV, nq),
        in_specs=[pl.BlockSpec((tq, gw), lambda b, g, qi: (q_row_blk0 + b * nq + qi, q_col_blk0 + g)),
                  pl.BlockSpec((None, None, None, ncp, HEAD_DIM), lambda b, g, qi: (b, 0, g, 0, 0)),
                  pl.BlockSpec((None, None, None, ncp, HEAD_DIM), lambda b, g, qi: (b, 1, g, 0, 0))],
        out_specs=[pl.BlockSpec((tq, gw), lambda b, g, qi: (b * nq + qi, g)),
                   pl.BlockSpec((None, None, sel_rows, tq), lambda b, g, qi: (b, g, 0, qi))],
        out_shape=[jax.ShapeDtypeStruct((out_rows, NSA_KV * gw), F32),
                   jax.ShapeDtypeStruct((nb, NSA_KV, sel_rows, tlen), F32)],
        scratch_shapes=[pltpu.VMEM((nsp, tq), F32)],
        compiler_params=_cparams(("parallel", "parallel", "parallel")),
        name="nsa_cmp_attn",
    )(q_arr, kvc, kvc)


def _cattn_row_body(q_ref, kc_ref, vc_ref, oc_ref, sel_ref, *, tq, ncp, n_c, ns, nsl, pos0, ksel):
    scale = HEAD_DIM ** -0.5
    kc = kc_ref[...].astype(BF)
    vc = vc_ref[...].astype(BF)
    trow = pos0 + lax.broadcasted_iota(jnp.int32, (tq, 1), 0)
    ncol = lax.broadcasted_iota(jnp.int32, (1, ncp), 1)
    valid = jnp.logical_and(ncol * NSA_CMP_STRIDE + (NSA_CMP_LEN - 1) <= trow, ncol < n_c)
    psum = jnp.zeros((tq, ncp), F32)
    for z in range(NSA_HPG):
        sl = slice(z * HEAD_DIM, (z + 1) * HEAD_DIM)
        q = (q_ref[:, sl] * scale).astype(BF)
        p = _masked_softmax(lax.dot_general(q, kc, NT_DIMS, preferred_element_type=F32), valid, -1)
        oc_ref[:, sl] = jnp.dot(p.astype(BF), vc, preferred_element_type=F32)
        psum = psum + p
    r = NSA_SEL_BLOCK // NSA_CMP_STRIDE
    nrow = lax.broadcasted_iota(jnp.int32, (ncp, 1), 0)
    bcol = lax.broadcasted_iota(jnp.int32, (1, nsl), 1)
    lo = r * bcol - 1
    inside = jnp.logical_and(nrow >= lo, nrow <= lo + r)
    edge = jnp.logical_or(nrow == lo, nrow == lo + r)
    m = jnp.where(inside, jnp.where(edge, 0.5, 1.0), 0.0).astype(BF)
    hi = psum.astype(BF)
    mid = (psum - hi.astype(F32)).astype(BF)
    low = (psum - hi.astype(F32) - mid.astype(F32)).astype(BF)
    imp = (jnp.dot(hi, m, preferred_element_type=F32) + jnp.dot(mid, m, preferred_element_type=F32)
           + jnp.dot(low, m, preferred_element_type=F32))
    own = trow // NSA_SEL_BLOCK
    past = jnp.logical_and(bcol < own, bcol < ns)
    forced = jnp.logical_or(bcol == 0, bcol == own - 1)
    score = jnp.where(past, jnp.where(forced, NSA_SEL_FORCE, imp), NEG_INF)
    rank = jnp.zeros((tq, nsl), F32)
    for j in range(ns):
        cj = score[:, j:j + 1]
        beats = jnp.logical_or(cj > score, jnp.logical_and(cj == score, j < bcol))
        rank = rank + jnp.where(beats, 1.0, 0.0)
    chosen = jnp.logical_and(past, rank < ksel)
    sel_ref[...] = jnp.where(jnp.logical_or(chosen, bcol == own), 1.0, 0.0)


def _cattn_sample(q_pad, kvc, nb, length, pos0):
    tq = SUBLANE
    ncp = kvc.shape[3]
    n_c = ncp - NSA_CMP_LEN // NSA_CMP_STRIDE + 1
    ns = length // NSA_SEL_BLOCK
    ksel = min(NSA_SEL_COUNT - 1, ns)
    nsl = -(-ns // LANE) * LANE
    body = functools.partial(_cattn_row_body, tq=tq, ncp=ncp, n_c=n_c, ns=ns, nsl=nsl, pos0=pos0, ksel=ksel)
    gw = NSA_HPG * HEAD_DIM
    return pl.pallas_call(
        body,
        grid=(nb, NSA_KV),
        in_specs=[pl.BlockSpec((tq, gw), lambda b, g: (b, g)),
                  pl.BlockSpec((None, None, None, ncp, HEAD_DIM), lambda b, g: (b, 0, g, 0, 0)),
                  pl.BlockSpec((None, None, None, ncp, HEAD_DIM), lambda b, g: (b, 1, g, 0, 0))],
        out_specs=[pl.BlockSpec((tq, gw), lambda b, g: (b, g)),
                   pl.BlockSpec((None, None, tq, nsl), lambda b, g: (b, g, 0, 0))],
        out_shape=[jax.ShapeDtypeStruct((nb * tq, NSA_KV * gw), F32),
                   jax.ShapeDtypeStruct((nb, NSA_KV, tq, nsl), F32)],
        compiler_params=_cparams(("parallel", "parallel")),
        name="nsa_cmp_attn_sample",
    )(q_pad, kvc, kvc)


def _gate_body(q_ref, k_ref, sel_ref, mean_ref, sc_ref, *, tq, nb, nbp, pos0, from_means):
    qi = pl.program_id(2)
    if from_means:
        means = k_ref[...]
    else:
        @pl.when(qi == 0)
        def _():
            mean_ref[...] = jnp.zeros(mean_ref.shape, F32)
            for j in range(nb):
                blk = k_ref[j * MOBA_BLOCK:(j + 1) * MOBA_BLOCK, :]
                mean_ref[j:j + 1, :] = jnp.sum(blk, axis=0, keepdims=True) * (1.0 / MOBA_BLOCK)

        means = mean_ref[...]
    g_t = _dot3(means, q_ref[...], NT_DIMS)
    tcol = pos0 + qi * tq + lax.broadcasted_iota(jnp.int32, (1, tq), 1)
    own = tcol // MOBA_BLOCK
    brow = lax.broadcasted_iota(jnp.int32, (nbp, 1), 0)
    past = jnp.logical_and(brow < own, brow < nb)
    sc_ref[...] = jnp.where(past, g_t, NEG_INF)
    chosen = jnp.logical_and(past, _rank_select(sc_ref, nb, min(MOBA_TOPK, nb)))
    sel = jnp.where(jnp.logical_or(chosen, brow == own), 1.0, 0.0)
    if sel_ref.shape[0] > nbp:
        sel_ref[...] = jnp.zeros(sel_ref.shape, F32)
    sel_ref[0:nbp, :] = sel


def _moba_gate_prompt(y, nb, t, tq):
    nq = t // tq
    n_full = t // MOBA_BLOCK
    nbp = -(-n_full // SUBLANE) * SUBLANE
    body = functools.partial(_gate_body, tq=tq, nb=n_full, nbp=nbp, pos0=0, from_means=False)
    return pl.pallas_call(
        body,
        grid=(nb, MOBA_HEADS, nq),
        in_specs=[pl.BlockSpec((tq, LANE), lambda b, h, qi: (b * nq + qi, CH_QB + h)),
                  pl.BlockSpec((t, LANE), lambda b, h, qi: (b, CH_KB + h))],
        out_specs=pl.BlockSpec((None, None, LANE, tq), lambda b, h, qi: (b, h, 0, qi)),
        out_shape=jax.ShapeDtypeStruct((nb, MOBA_HEADS, LANE, t), F32),
        scratch_shapes=[pltpu.VMEM((nbp, HEAD_DIM), F32), pltpu.VMEM((nbp, tq), F32)],
        compiler_params=_cparams(("parallel", "parallel", "arbitrary")),
        name="moba_gate_prompt",
    )(y, y)


def _moba_gate_sample(q_pad, means, pos0):
    nb, _, n_full, _ = means.shape
    tq = SUBLANE
    body = functools.partial(_gate_body, tq=tq, nb=n_full, nbp=n_full, pos0=pos0, from_means=True)
    return pl.pallas_call(
        body,
        grid=(nb, MOBA_HEADS, 1),
        in_specs=[pl.BlockSpec((tq, LANE), lambda b, h, qi: (b, h)),
                  pl.BlockSpec((None, None, n_full, HEAD_DIM), lambda b, h, qi: (b, h, 0, 0))],
        out_specs=pl.BlockSpec((None, None, max(n_full, LANE), tq), lambda b, h, qi: (b, h, 0, 0)),
        out_shape=jax.ShapeDtypeStruct((nb, MOBA_HEADS, max(n_full, LANE), tq), F32),
        scratch_shapes=[pltpu.VMEM((SUBLANE, HEAD_DIM), F32), pltpu.VMEM((n_full, tq), F32)],
        compiler_params=_cparams(("parallel", "parallel", "arbitrary")),
        name="moba_gate_sample",
    )(q_pad, means)


MEANS_BLOCKS = 4


def _means_body(pt_ref, *refs, ppb, bps):
    c_refs = refs[:bps * ppb]
    o_ref = refs[bps * ppb]
    j = pl.program_id(1)
    for q in range(bps):
        acc = jnp.sum(c_refs[q * ppb][...], axis=0)
        for i in range(1, ppb):
            acc = acc + jnp.sum(c_refs[q * ppb + i][...], axis=0)
        m = acc * (1.0 / MOBA_BLOCK)
        for h in range(MOBA_HEADS):
            o_ref[h, pl.ds(j * bps + q, 1), :] = m[h:h + 1, :]


def _moba_means_sample(cache, li, page_table):
    nb, n_pages = page_table.shape
    page = cache.shape[2]
    ppb = MOBA_BLOCK // page
    n_full = n_pages // ppb
    bps = math.gcd(MEANS_BLOCKS, n_full)
    body = functools.partial(_means_body, ppb=ppb, bps=bps)
    in_specs = [
        pl.BlockSpec((None, None, page, None, MOBA_HEADS, HEAD_DIM),
                     lambda b, j, pt, i=i: (li, pt[b, j * bps * ppb + i], 0, 0, 0, 0))
        for i in range(bps * ppb)
    ]
    return pl.pallas_call(
        body,
        grid_spec=pltpu.PrefetchScalarGridSpec(
            num_scalar_prefetch=1,
            grid=(nb, n_full // bps),
            in_specs=in_specs,
            out_specs=pl.BlockSpec((None, MOBA_HEADS, n_full, HEAD_DIM), lambda b, j, pt: (b, 0, 0, 0)),
        ),
        out_shape=jax.ShapeDtypeStruct((nb, MOBA_HEADS, n_full, HEAD_DIM), F32),
        compiler_params=_cparams(("parallel", "arbitrary")),
        name="moba_means_sample",
    )(page_table, *([cache] * (bps * ppb)))


FLASH_UNIT_ROWS = 256
LOG2_E = 1.4426950408889634
MASK_BIG = 2.0 ** 60


def _flash_body(*refs, tq, tk, nh, blk, window, masked, ur):
    q_ref, k_ref, v_ref = refs[:3]
    pos = 3
    if masked:
        sel_ref = refs[pos]
        pos += 1
    o_ref, qs_ref = refs[pos:pos + 2]
    units = [(z, r0) for z in range(nh) for r0 in range(0, tq, ur)]
    stat = refs[pos + 2:pos + 2 + 2 * len(units)]
    m_refs, acc_refs = stat[0::2], stat[1::2]
    qi = pl.program_id(2)
    t0 = qi * tq
    scale = HEAD_DIM ** -0.5 * LOG2_E
    if masked:
        bias = ((jnp.transpose(sel_ref[...]) - 1.0) * MASK_BIG).astype(BF)
    for z in range(nh):
        qz = (q_ref[:, z * HEAD_DIM:(z + 1) * HEAD_DIM] * scale).astype(BF)
        qs_ref[z] = jnp.concatenate([qz, bias], axis=1) if masked else qz
    for u in range(len(units)):
        m_refs[u][...] = jnp.full(m_refs[u].shape, NEG_INF, F32)
        acc_refs[u][...] = jnp.zeros(acc_refs[u].shape, F32)
    trow = t0 + lax.broadcasted_iota(jnp.int32, (tq, 1), 0)

    def process(c, edge):
        ks = pl.multiple_of(c * tk, tk)
        kc = k_ref[pl.ds(ks, tk), :]
        if masked:
            prow = ks + lax.broadcasted_iota(jnp.int32, (tk, 1), 0)
            jcol = lax.broadcasted_iota(jnp.int32, (1, LANE), 1)
            kc = jnp.concatenate([kc, jnp.where(prow // blk == jcol, 1.0, 0.0).astype(BF)], axis=1)
        vc = jnp.concatenate([v_ref[pl.ds(ks, tk), :], jnp.ones((tk, LANE), BF)], axis=1)
        if edge:
            pcol = ks + lax.broadcasted_iota(jnp.int32, (1, tk), 1)
            valid = pcol <= trow
            if window:
                valid = jnp.logical_and(valid, pcol > trow - window)
        for u, (z, r0) in enumerate(units):
            s = lax.dot_general(qs_ref[z, r0:r0 + ur, :], kc, NT_DIMS, preferred_element_type=F32)
            m_old = m_refs[u][...]
            if edge:
                s = jnp.where(valid[r0:r0 + ur], s, NEG_INF)
                m_new = jnp.maximum(m_old, jnp.max(s, axis=-1, keepdims=True))
                m_use = jnp.where(m_new == NEG_INF, 0.0, m_new)
            else:
                m_new = jnp.maximum(m_old, jnp.max(s, axis=-1, keepdims=True))
                m_use = m_new
            alpha = jnp.exp2(m_old - m_use)
            p = jnp.exp2(s - m_use)
            acc_refs[u][...] = alpha * acc_refs[u][...] + jnp.dot(p.astype(BF), vc, preferred_element_type=F32)
            m_refs[u][...] = m_new

    if masked:
        lax.fori_loop(0, qi, lambda c, carry: (process(c, False), carry)[1], 0)
        process(qi, True)
    else:
        c_hi = (t0 + tq + tk - 1) // tk
        c_lo = jnp.maximum(t0 - window + 1, 0) // tk
        lax.fori_loop(c_lo, c_hi, lambda c, carry: (process(c, True), carry)[1], 0)
    for u, (z, r0) in enumerate(units):
        l = acc_refs[u][:, HEAD_DIM:HEAD_DIM + 1]
        o_ref[r0:r0 + ur, z * HEAD_DIM:(z + 1) * HEAD_DIM] = acc_refs[u][:, :HEAD_DIM] / jnp.where(l > 0, l, 1.0)


def _flash(y, y16, sel, nb, t, *, q_ch, k_ch, v_ch, kvh, nh, blk, window, out_rows, tq=512, tk=512):
    tq = min(tq, t)
    tk = min(tk, t)
    nq = t // tq
    masked = sel is not None
    gw = nh * HEAD_DIM
    assert q_ch % nh == 0 and (tq == tk or not masked)
    in_specs = [pl.BlockSpec((tq, gw), lambda b, h, qi: (b * nq + qi, q_ch // nh + h)),
                pl.BlockSpec((t, LANE), lambda b, h, qi: (b, k_ch + h)),
                pl.BlockSpec((t, LANE), lambda b, h, qi: (b, v_ch + h))]
    args = [y, y16, y16]
    if masked:
        in_specs.append(pl.BlockSpec((None, None, LANE, tq), lambda b, h, qi: (b, h, 0, qi)))
        args.append(sel)
    ur = min(FLASH_UNIT_ROWS, tq)
    n_units = nh * (tq // ur)
    body = functools.partial(_flash_body, tq=tq, tk=tk, nh=nh, blk=blk, window=window, masked=masked, ur=ur)
    unit_scratch = [pltpu.VMEM((ur, 1), F32), pltpu.VMEM((ur, 2 * HEAD_DIM), F32)]
    return pl.pallas_call(
        body,
        grid=(nb, kvh, nq),
        in_specs=in_specs,
        out_specs=pl.BlockSpec((tq, gw), lambda b, h, qi: (b * nq + qi, h)),
        out_shape=jax.ShapeDtypeStruct((out_rows, kvh * gw), F32),
        scratch_shapes=[pltpu.VMEM((nh, tq, (2 if masked else 1) * HEAD_DIM), BF)] + unit_scratch * n_units,
        compiler_params=_cparams(("parallel", "parallel", "parallel")),
        name="flash_" + ("win" if window else "blk%d" % blk),
    )(*args)


def _dec_body(tbl_ref, vld_ref, q_ref, *refs, nh, rows, kvh, n_steps, nblk, r_min):
    k_refs = refs[:nblk]
    v_refs = refs[nblk:2 * nblk]
    kn_ref, vn_ref, o_ref, m_ref, l_ref, acc_ref = refs[2 * nblk:2 * nblk + 6]
    b = pl.program_id(0)
    h = pl.program_id(1)
    j = pl.program_id(2)
    scale = HEAD_DIM ** -0.5
    qrow = q_ref[pl.ds(b, 1), :] * scale
    zrow = lax.broadcasted_iota(jnp.int32, (SUBLANE, 1), 0)
    qm = jnp.zeros((SUBLANE, HEAD_DIM), F32)
    for z in range(nh):
        qm = jnp.where(zrow == z, qrow[:, z * HEAD_DIM:(z + 1) * HEAD_DIM], qm)

    @pl.when(j == 0)
    def _():
        m_ref[...] = jnp.full(m_ref.shape, NEG_INF, F32)
        l_ref[...] = jnp.zeros(l_ref.shape, F32)
        acc_ref[...] = jnp.zeros(acc_ref.shape, F32)

    col = lax.broadcasted_iota(jnp.int32, (1, rows * kvh), 1)
    valid = jnp.logical_and(col % kvh == h, col // kvh >= r_min)
    for i in range(nblk):
        @pl.when(vld_ref[b, h, j * nblk + i] > 0)
        def _(i=i):
            kk = k_refs[i][...].reshape(rows * kvh, HEAD_DIM).astype(BF)
            vv = v_refs[i][...].reshape(rows * kvh, HEAD_DIM).astype(BF)
            s = lax.dot_general(qm.astype(BF), kk, NT_DIMS, preferred_element_type=F32)
            s = jnp.where(valid, s, NEG_INF)
            m_old = m_ref[...]
            m_new = jnp.maximum(m_old, jnp.max(s, axis=-1, keepdims=True))
            m_safe = jnp.where(m_new == NEG_INF, 0.0, m_new)
            alpha = jnp.exp(m_old - m_safe)
            p = jnp.exp(s - m_safe)
            l_ref[...] = alpha * l_ref[...] + jnp.sum(p, axis=-1, keepdims=True)
            acc_ref[...] = alpha * acc_ref[...] + jnp.dot(p.astype(BF), vv, preferred_element_type=F32)
            m_ref[...] = m_new

    @pl.when(j == n_steps - 1)
    def _():
        kn = kn_ref[pl.ds(b, 1), :]
        vn = vn_ref[pl.ds(b, 1), :]
        s = jnp.sum(qm * kn, axis=-1, keepdims=True)
        m_old = m_ref[...]
        m_new = jnp.maximum(m_old, s)
        alpha = jnp.exp(m_old - m_new)
        p = jnp.exp(s - m_new)
        l = alpha * l_ref[...] + p
        o_ref[...] = (alpha * acc_ref[...] + p * vn) / l


def _decode_attn(ys, cache, li_fixed, tbl, vld, *, q_ch, kn_ch, vn_ch, nh, rows, kvh, r_min, nblk, name):
    nb, _, n_list = tbl.shape
    assert n_list % nblk == 0
    n_steps = n_list // nblk
    page = cache.shape[2]
    bpp = page // rows
    gw = nh * HEAD_DIM
    body = functools.partial(_dec_body, nh=nh, rows=rows, kvh=kvh, n_steps=n_steps, nblk=nblk, r_min=r_min)

    def kv_spec(kv, i):
        def imap(b, h, j, tb, vl):
            t = tb[b, h, j * nblk + i]
            return (li_fixed, t // bpp, t % bpp, kv, 0, 0)

        return pl.BlockSpec((None, None, rows, None, kvh, HEAD_DIM), imap)

    nrow = ys.shape[0]
    return pl.pallas_call(
        body,
        grid_spec=pltpu.PrefetchScalarGridSpec(
            num_scalar_prefetch=2,
            grid=(nb, kvh, n_steps),
            in_specs=[pl.BlockSpec((nrow, gw), lambda b, h, j, tb, vl: (0, q_ch // nh + h))]
            + [kv_spec(0, i) for i in range(nblk)] + [kv_spec(1, i) for i in range(nblk)]
            + [pl.BlockSpec((nrow, LANE), lambda b, h, j, tb, vl: (0, kn_ch + h)),
               pl.BlockSpec((nrow, LANE), lambda b, h, j, tb, vl: (0, vn_ch + h))],
            out_specs=pl.BlockSpec((None, None, SUBLANE, HEAD_DIM), lambda b, h, j, tb, vl: (b, h, 0, 0)),
            scratch_shapes=[pltpu.VMEM((SUBLANE, 1), F32), pltpu.VMEM((SUBLANE, 1), F32), pltpu.VMEM((SUBLANE, HEAD_DIM), F32)],
        ),
        out_shape=jax.ShapeDtypeStruct((nb, kvh, SUBLANE, HEAD_DIM), F32),
        compiler_params=_cparams(("parallel", "parallel", "arbitrary")),
        name=name,
    )(tbl, vld, ys, *([cache] * (2 * nblk)), ys, ys)


def _mix_body(oc_ref, os_ref, ow_ref, ob_ref, soc_ref, sos_ref, sow_ref, sob_ref, gate_ref, o_ref, *, n_p_tiles, n_s):
    i = pl.program_id(0)
    w = NSA_HEADS * HEAD_DIM

    def emit(oc, os_, ow, ob, rows):
        g = jax.nn.sigmoid(gate_ref[0:rows, :])
        for h in range(NSA_HEADS):
            sl = slice(h * HEAD_DIM, (h + 1) * HEAD_DIM)
            o = (g[:, 3 * h:3 * h + 1] * oc[:, sl] + g[:, 3 * h + 1:3 * h + 2] * os_[:, sl]
                 + g[:, 3 * h + 2:3 * h + 3] * ow[:, sl])
            o_ref[0:rows, sl] = o.astype(o_ref.dtype)
        o_ref[0:rows, w:] = ob[...].astype(o_ref.dtype)

    @pl.when(i < n_p_tiles)
    def _():
        emit(oc_ref, os_ref, ow_ref, ob_ref, o_ref.shape[0])

    @pl.when(i >= n_p_tiles)
    def _():
        emit(soc_ref, sos_ref, sow_ref, sob_ref, n_s)


def _mix(prompt_parts, sample_parts, y, tm=256):
    n_prompt = prompt_parts[0].shape[0]
    n_s = sample_parts[0].shape[0]
    m = y.shape[0]
    tm = min(tm, n_prompt)
    assert n_prompt % tm == 0 and m == n_prompt + n_s and n_s <= tm
    n_p_tiles = n_prompt // tm
    wa = NSA_HEADS * HEAD_DIM
    wb = MOBA_HEADS * HEAD_DIM
    pmap = lambda i: (jnp.minimum(i, n_p_tiles - 1), 0)
    body = functools.partial(_mix_body, n_p_tiles=n_p_tiles, n_s=n_s)
    return pl.pallas_call(
        body,
        grid=(n_p_tiles + 1,),
        in_specs=[pl.BlockSpec((tm, wa), pmap)] * 3 + [pl.BlockSpec((tm, wb), pmap)]
        + [pl.BlockSpec((n_s, wa), lambda i: (0, 0))] * 3 + [pl.BlockSpec((n_s, wb), lambda i: (0, 0))]
        + [pl.BlockSpec((tm, LANE), lambda i: (i, CH_GATE))],
        out_specs=pl.BlockSpec((tm, wa + wb), lambda i: (i, 0)),
        out_shape=jax.ShapeDtypeStruct((m, wa + wb), BF),
        compiler_params=_cparams(("parallel",)),
        name="attn_mix",
    )(*prompt_parts, *sample_parts, y)


def _convmix_body(up_ref, hp_ref, uc_ref, hc_ref, pw_ref, ps_ref, dw_ref, dwb_ref, lng_ref, lnb_ref, pww_ref,
                  o_ref, zp_ref, zc_ref, cb_ref, *, ts, nt, zero_first, avail0, rc):
    s = pl.program_id(0)
    hp_rows = hp_ref.shape[0]
    hc_rows = hc_ref.shape[0]
    width = up_ref.shape[1]
    if zero_first:
        keep = jnp.where(s % nt == 0, 0.0, 1.0)
        zp_ref[0:hp_rows, :] = hp_ref[...] * keep
        zc_ref[0:hc_rows, :] = hc_ref[...] * keep
        avail = (s % nt) * ts + avail0
    else:
        zp_ref[0:hp_rows, :] = hp_ref[...]
        zc_ref[0:hc_rows, :] = hc_ref[...]
        avail = avail0
    zp_ref[hp_rows:, :] = up_ref[...]
    zc_ref[hc_rows:, :] = uc_ref[...]
    gwidth = width // len(POOL_WINDOWS)
    t_idx = lax.broadcasted_iota(jnp.int32, (ts, 1), 0) + avail + 1
    for gi, w in enumerate(POOL_WINDOWS):
        sl = slice(gi * gwidth, (gi + 1) * gwidth)
        cur = zp_ref[hp_rows:hp_rows + ts, sl]
        acc = cur
        for jj in range(1, w):
            acc = acc + zp_ref[hp_rows - jj:hp_rows - jj + ts, sl]
        cnt = jnp.minimum(t_idx, w).astype(F32)
        yg = acc / cnt - cur
        og = jnp.dot(yg.astype(BF), pw_ref[gi], preferred_element_type=F32) * ps_ref[:, sl]
        o_ref[:, sl] = og.astype(o_ref.dtype)
    base = hc_rows - (CONV_WIDTH - 1)
    for r in range(ts // rc):
        for c in range(width // LANE):
            cs = slice(c * LANE, (c + 1) * LANE)
            acc = jnp.zeros((rc, LANE), F32)
            for jj in range(CONV_WIDTH):
                lo = r * rc + base + jj
                acc = acc + zc_ref[lo:lo + rc, cs] * dw_ref[jj:jj + 1, cs]
            cb_ref[r * rc:(r + 1) * rc, cs] = acc + dwb_ref[:, cs]
    cv = cb_ref[...]
    mu = jnp.mean(cv, axis=-1, keepdims=True)
    xc = cv - mu
    yn = xc * lax.rsqrt(jnp.mean(xc * xc, axis=-1, keepdims=True) + NORM_EPS) * lng_ref[...] + lnb_ref[...]
    act = yn * jax.nn.sigmoid(yn)
    o_ref[:, width:] = jnp.dot(act.astype(BF), pww_ref[...], preferred_element_type=F32).astype(o_ref.dtype)


def _convmix(u_pool, halo_pool, u_conv, halo_conv, prm, *, ts, nt, n_tiles, zero_first, avail0, hp_rows, hc_rows,
             halo_blk):
    pw, ps, dw, dwb, lng, lnb, pww = prm
    width = u_pool.shape[1]
    rc = min(64, ts)
    body = functools.partial(_convmix_body, ts=ts, nt=nt, zero_first=zero_first, avail0=avail0, rc=rc)
    full = lambda a: pl.BlockSpec(a.shape, lambda s: (0,) * a.ndim)
    return pl.pallas_call(
        body,
        grid=(n_tiles,),
        in_specs=[pl.BlockSpec((ts, width), lambda s: (s, 0)),
                  pl.BlockSpec((hp_rows, width), lambda s: (halo_blk(s, hp_rows), 0)),
                  pl.BlockSpec((ts, width), lambda s: (s, 0)),
                  pl.BlockSpec((hc_rows, width), lambda s: (halo_blk(s, hc_rows), 0)),
                  full(pw), full(ps), full(dw), full(dwb), full(lng), full(lnb), full(pww)],
        out_specs=pl.BlockSpec((ts, 2 * width), lambda s: (s, 0)),
        out_shape=jax.ShapeDtypeStruct((n_tiles * ts, 2 * width), BF),
        scratch_shapes=[pltpu.VMEM((hp_rows + ts, width), F32), pltpu.VMEM((hc_rows + ts, width), F32),
                        pltpu.VMEM((ts, width), F32)],
        compiler_params=_cparams(("parallel",)),
        name="conv_mix",
    )(u_pool, halo_pool, u_conv, halo_conv, pw, ps, dw, dwb, lng, lnb, pww)


def _router_body(x_ref, g_ref, rw_ref, r_ref):
    x = x_ref[...]
    xn = x * lax.rsqrt(jnp.mean(x * x, axis=-1, keepdims=True) + NORM_EPS) * g_ref[...]
    logits = _dot3(xn, rw_ref[...])
    lane = lax.broadcasted_iota(jnp.int32, logits.shape, 1)
    l1 = jnp.where(lane < N_EXPERTS, logits, NEG_INF)
    m1 = jnp.max(l1, axis=-1, keepdims=True)
    i1 = jnp.min(jnp.where(l1 == m1, lane, LANE), axis=-1, keepdims=True)
    l2 = jnp.where(lane == i1, NEG_INF, l1)
    m2 = jnp.max(l2, axis=-1, keepdims=True)
    i2 = jnp.min(jnp.where(l2 == m2, lane, LANE), axis=-1, keepdims=True)
    e = jnp.exp(m2 - m1)
    w1 = 1.0 / (1.0 + e)
    w2 = e / (1.0 + e)
    r_ref[...] = jnp.where(lane == 0, i1.astype(F32),
                           jnp.where(lane == 1, i2.astype(F32), jnp.where(lane == 2, w1, jnp.where(lane == 3, w2, 0.0))))


def _router(x, g, rw, tm=256):
    m, d = x.shape
    rw_pad = jnp.pad(rw, ((0, 0), (0, LANE - rw.shape[1])))
    return pl.pallas_call(
        _router_body,
        grid=(pl.cdiv(m, tm),),
        in_specs=[pl.BlockSpec((tm, d), lambda i: (i, 0)), pl.BlockSpec((1, d), lambda i: (0, 0)),
                  pl.BlockSpec((d, LANE), lambda i: (0, 0))],
        out_specs=pl.BlockSpec((tm, LANE), lambda i: (i, 0)),
        out_shape=jax.ShapeDtypeStruct((m, LANE), F32),
        compiler_params=_cparams(("parallel",)),
        name="moe_router",
    )(x, g.reshape(1, d), rw_pad)


def _row_copy(src_hbm, row, dst, drow, sem):
    return pltpu.make_async_copy(src_hbm.at[pl.ds(row, 1), :], dst.at[pl.ds(drow, 1), :], sem)


def _gather_norm_body(src_ref, x_hbm, g_ref, o_ref, buf_ref, sem, *, tm):
    base = pl.program_id(0) * tm

    def issue(r, c):
        _row_copy(x_hbm, src_ref[base + r], buf_ref, r, sem).start()
        return c

    lax.fori_loop(0, tm, issue, 0)

    def wait(r, c):
        _row_copy(x_hbm, 0, buf_ref, r, sem).wait()
        return c

    lax.fori_loop(0, tm, wait, 0)
    x = buf_ref[...]
    ms = jnp.mean(x * x, axis=-1, keepdims=True)
    o_ref[...] = (x * lax.rsqrt(ms + NORM_EPS) * g_ref[...]).astype(o_ref.dtype)


def _gather_norm(x, src, g, tm):
    d = x.shape[1]
    r_tot = src.shape[0]
    body = functools.partial(_gather_norm_body, tm=tm)
    return pl.pallas_call(
        body,
        grid_spec=pltpu.PrefetchScalarGridSpec(
            num_scalar_prefetch=1,
            grid=(r_tot // tm,),
            in_specs=[pl.BlockSpec(memory_space=pl.ANY), pl.BlockSpec((1, d), lambda i, s: (0, 0))],
            out_specs=pl.BlockSpec((tm, d), lambda i, s: (i, 0)),
            scratch_shapes=[pltpu.VMEM((tm, d), F32), pltpu.SemaphoreType.DMA(())],
        ),
        out_shape=jax.ShapeDtypeStruct((r_tot, d), BF),
        compiler_params=_cparams(("arbitrary",)),
        name="moe_gather",
    )(src, x, g.reshape(1, d))


def _combine_body(p1_ref, p2_ref, eo_hbm, x_ref, r_ref, o_ref, b1_ref, b2_ref, sem, *, tm, n_tok):
    base = pl.program_id(0) * tm
    n = jnp.minimum(tm, n_tok - base)

    def issue(r, c):
        _row_copy(eo_hbm, p1_ref[base + r], b1_ref, r, sem).start()
        _row_copy(eo_hbm, p2_ref[base + r], b2_ref, r, sem).start()
        return c

    lax.fori_loop(0, n, issue, 0)

    def wait(r, c):
        _row_copy(eo_hbm, 0, b1_ref, r, sem).wait()
        _row_copy(eo_hbm, 0, b2_ref, r, sem).wait()
        return c

    lax.fori_loop(0, n, wait, 0)
    rr = r_ref[...]
    o_ref[...] = x_ref[...] + rr[:, 2:3] * b1_ref[...] + rr[:, 3:4] * b2_ref[...]


def _combine(x, eo, routing, p1, p2, tm_max=512):
    m, d = x.shape
    tm = max(t for t in range(SUBLANE, tm_max + 1, SUBLANE) if m % t == 0)
    body = functools.partial(_combine_body, tm=tm, n_tok=m)
    return pl.pallas_call(
        body,
        grid_spec=pltpu.PrefetchScalarGridSpec(
            num_scalar_prefetch=2,
            grid=(pl.cdiv(m, tm),),
            in_specs=[pl.BlockSpec(memory_space=pl.ANY), pl.BlockSpec((tm, d), lambda i, a, b: (i, 0)),
                      pl.BlockSpec((tm, LANE), lambda i, a, b: (i, 0))],
            out_specs=pl.BlockSpec((tm, d), lambda i, a, b: (i, 0)),
            scratch_shapes=[pltpu.VMEM((tm, d), F32), pltpu.VMEM((tm, d), F32), pltpu.SemaphoreType.DMA(())],
        ),
        out_shape=jax.ShapeDtypeStruct((m, d), F32),
        compiler_params=_cparams(("arbitrary",)),
        name="moe_combine",
    )(p1, p2, eo, x, routing)


def _moe_plan(e_idx, tm):
    n = e_idx.shape[0]
    n_asg = n * TOP_K
    flat_e = e_idx.reshape(-1)
    onehot = (flat_e[:, None] == jnp.arange(N_EXPERTS, dtype=jnp.int32)[None, :]).astype(jnp.int32)
    cnt = jnp.sum(onehot, axis=0)
    rank = jnp.take_along_axis(jnp.cumsum(onehot, axis=0) - onehot, flat_e[:, None], axis=1)[:, 0]
    cnt_p = ((cnt + tm - 1) // tm) * tm
    ends = jnp.cumsum(cnt_p)
    off = ends - cnt_p
    pos = off[flat_e] + rank
    n_tiles = (n_asg + N_EXPERTS * (tm - 1) + tm - 1) // tm
    r_tot = n_tiles * tm
    src = jnp.zeros((r_tot,), jnp.int32).at[pos].set(jnp.arange(n_asg, dtype=jnp.int32) // TOP_K)
    tile_start = jnp.arange(n_tiles, dtype=jnp.int32) * tm
    tval = (tile_start < ends[-1]).astype(jnp.int32)
    texp = jnp.minimum(jnp.searchsorted(ends, tile_start, side="right"), N_EXPERTS - 1).astype(jnp.int32)
    last = jnp.max(jnp.where(tval > 0, texp, 0))
    texp = jnp.where(tval > 0, texp, last)
    pos2 = pos.reshape(n, TOP_K)
    return src, pos2[:, 0], pos2[:, 1], texp, tval


def _moe(x, g, rw, wg, wu, wd, tm=256):
    routing = _router(x, g, rw)
    e_idx = routing[:, :TOP_K].astype(jnp.int32)
    src, p1, p2, texp, tval = _moe_plan(e_idx, tm)
    xs = _gather_norm(x, src, g, tm)
    hid = _gmm(xs, [(wg, 0), (wu, 0)], wg.shape[2], epi="swiglu", texp=texp, tval=tval, out_dtype=BF, tm=tm, tn=1024,
               name="moe_up")
    eo = _gmm(hid, [(wd, 0)], wd.shape[2], texp=texp, tval=tval, tm=tm, name="moe_down")
    return _combine(x, eo, routing, p1, p2)


def _attn_layer(x, n_prompt, nb_p, t_p, nb_s, li, prm, caches, page_table):
    (attn_norm, w_in, qk_g, cmp_pe, cmp_w1, cmp_w2, moba_g, w_out, ffn_norm, w_gate, w_up, w_down) = prm
    cache_cmp, cache_sel, state_win, cache_moba = caches
    d = x.shape[1]
    past_len = page_table.shape[1] * cache_moba.shape[2]

    n_in = w_in.shape[1]
    g_lo, g_hi = CH_QB * LANE, CH_QB * LANE + 3 * NSA_HEADS
    w_re = jnp.concatenate([w_in[:, :g_lo], w_in[:, g_hi:], w_in[:, g_lo:g_hi],
                            jnp.zeros((d, ATTN_CHUNKS * LANE - n_in), F32)], axis=1)[None]
    ones = jnp.ones((HEAD_DIM,), F32)
    zeros = jnp.zeros((HEAD_DIM,), F32)
    chunk_gain = ([qk_g[0]] * 8 + [ones] * 4 + [qk_g[2]] * 2 + [ones] * 2 + [qk_g[3]] * 2 + [ones] * 2
                  + [moba_g[0]] * 8 + [moba_g[1]] * 8 + [ones] * 9)
    chunk_flag = ([ones] * 8 + [zeros] * 4 + [ones] * 2 + [zeros] * 2 + [ones] * 2 + [zeros] * 2
                  + [ones] * 16 + [zeros] * 9)
    gain = jnp.concatenate(chunk_gain)[None]
    flag = jnp.concatenate(chunk_flag)[None]
    xn = _rmsnorm(x, attn_norm)
    y, y16 = _gmm(xn, [(w_re, 0)], ATTN_CHUNKS * LANE, epi="headnorm", gain=gain, flag=flag, out_dtype=(F32, BF),
                  tm=512, tn=9 * LANE, name="attn_in")

    r = NSA_CMP_LEN // NSA_CMP_STRIDE
    wcat = cmp_w1.reshape(2, r, NSA_CMP_STRIDE, HEAD_DIM, NSA_CMP_HIDDEN).transpose(0, 2, 3, 1, 4)
    wcat = wcat.reshape(2, NSA_CMP_STRIDE, HEAD_DIM, r * NSA_CMP_HIDDEN).astype(BF)
    pe_rows = cmp_pe.reshape(2, r, NSA_CMP_STRIDE, HEAD_DIM).transpose(0, 2, 1, 3)
    pe_rows = jnp.pad(pe_rows, ((0, 0), (0, 0), (0, SUBLANE - r), (0, 0))).astype(BF)
    w2 = cmp_w2.astype(BF)
    gk = qk_g[1][None]

    kvc_p = _compress_c2(_compress_c1_prompt(y, wcat, nb_p, t_p), pe_rows, wcat, w2, gk)
    rows = n_prompt
    oc, sel_a = _cattn(y, 0, 0, kvc_p, nb_p, t_p, min(512, t_p), t_p, 0, rows)
    sel_b = _moba_gate_prompt(y, nb_p, t_p, min(1024, t_p))
    os_ = _flash(y, y16, sel_a, nb_p, t_p, q_ch=CH_QA, k_ch=CH_KS, v_ch=CH_VS, kvh=NSA_KV, nh=NSA_HPG,
                 blk=NSA_SEL_BLOCK, window=0, out_rows=rows)
    ow = _flash(y, y16, None, nb_p, t_p, q_ch=CH_QA, k_ch=CH_KW, v_ch=CH_VW, kvh=NSA_KV, nh=NSA_HPG, blk=0,
                window=NSA_WINDOW, out_rows=rows)
    ob = _flash(y, y16, sel_b, nb_p, t_p, q_ch=CH_QB, k_ch=CH_KB, v_ch=CH_VB, kvh=MOBA_HEADS, nh=1, blk=MOBA_BLOCK,
                window=0, out_rows=rows)

    n_pad = x.shape[0] - n_prompt
    ys = y[n_prompt:]
    kvc_s = _compress_c2(_compress_c1_sample(cache_cmp, li, page_table, wcat), pe_rows, wcat, w2, gk)
    q_pad = jnp.zeros((nb_s, SUBLANE, NSA_HEADS * HEAD_DIM), F32).at[:, 0].set(ys[:nb_s, :NSA_HEADS * HEAD_DIM])
    oc_s, sel_s = _cattn_sample(q_pad.reshape(nb_s * SUBLANE, -1), kvc_s, nb_s, past_len + 1, past_len)
    oc_s = oc_s.reshape(nb_s, SUBLANE, -1)[:, 0]
    n_sel = (past_len + 1) // NSA_SEL_BLOCK
    k_sel = min(NSA_SEL_COUNT - 1, n_sel)
    mask_a = sel_s[:, :, 0, :n_sel]
    idx_a = jnp.argsort(-mask_a, axis=-1, stable=True)[..., :k_sel].astype(jnp.int32)
    vld_a = (jnp.take_along_axis(mask_a, idx_a, axis=-1) > 0).astype(jnp.int32)
    page = cache_sel.shape[2]
    bpp = page // NSA_SEL_BLOCK
    pt_b = page_table[:, None, :]
    tbl_a = jnp.take_along_axis(jnp.broadcast_to(pt_b, (nb_s, NSA_KV, pt_b.shape[-1])), idx_a // bpp, axis=-1) * bpp + idx_a % bpp
    os_s = _decode_attn(ys, cache_sel, li, tbl_a.astype(jnp.int32), vld_a, q_ch=CH_QA, kn_ch=CH_KS, vn_ch=CH_VS,
                        nh=NSA_HPG, rows=NSA_SEL_BLOCK, kvh=NSA_KV, r_min=0, nblk=_largest_divisor(k_sel, 15),
                        name="dec_sel")
    win_buf = state_win.shape[2]
    tbl_w = jnp.broadcast_to(jnp.arange(nb_s, dtype=jnp.int32)[:, None, None], (nb_s, NSA_KV, 1))
    ow_s = _decode_attn(ys, state_win, li, tbl_w, jnp.ones_like(tbl_w), q_ch=CH_QA, kn_ch=CH_KW, vn_ch=CH_VW,
                        nh=NSA_HPG, rows=win_buf, kvh=NSA_KV, r_min=max(0, win_buf - (NSA_WINDOW - 1)), nblk=1,
                        name="dec_win")
    means = _moba_means_sample(cache_moba, li, page_table)
    qb_pad = jnp.zeros((nb_s, SUBLANE, MOBA_HEADS * HEAD_DIM), F32).at[:, 0].set(
        ys[:nb_s, CH_QB * LANE:(CH_QB + MOBA_HEADS) * LANE])
    sel_m = _moba_gate_sample(qb_pad.reshape(nb_s * SUBLANE, -1), means, past_len)
    n_full = means.shape[2]
    k_top = min(MOBA_TOPK, n_full)
    mask_b = sel_m[:, :, :n_full, 0]
    idx_b = jnp.argsort(-mask_b, axis=-1, stable=True)[..., :k_top].astype(jnp.int32)
    vld_b = (jnp.take_along_axis(mask_b, idx_b, axis=-1) > 0).astype(jnp.int32)
    ppb = MOBA_BLOCK // page
    pg_b = (idx_b[..., None] * ppb + jnp.arange(ppb, dtype=jnp.int32)).reshape(nb_s, MOBA_HEADS, k_top * ppb)
    tbl_b = jnp.take_along_axis(jnp.broadcast_to(pt_b, (nb_s, MOBA_HEADS, pt_b.shape[-1])), pg_b, axis=-1)
    vld_b = jnp.repeat(vld_b, ppb, axis=-1)
    ob_s = _decode_attn(ys, cache_moba, li, tbl_b.astype(jnp.int32), vld_b, q_ch=CH_QB, kn_ch=CH_KB, vn_ch=CH_VB,
                        nh=1, rows=page, kvh=MOBA_HEADS, r_min=0, nblk=_largest_divisor(k_top * ppb, 6),
                        name="dec_moba")

    def rows_s(part, nh):
        return jnp.pad(part[:, :, :nh].reshape(nb_s, -1), ((0, n_pad - nb_s), (0, 0)))

    sample_parts = (jnp.pad(oc_s, ((0, n_pad - nb_s), (0, 0))), rows_s(os_s, NSA_HPG), rows_s(ow_s, NSA_HPG),
                    rows_s(ob_s, 1))
    mixed = _mix((oc, os_, ow, ob), sample_parts, y)
    x = _gmm(mixed, [(w_out[None], 0)], d, resid=x, tm=512, tn=1024, name="attn_out")
    hid = _gmm(_rmsnorm(x, ffn_norm), [(w_gate[None], 0), (w_up[None], 0)], w_gate.shape[1], epi="swiglu",
               out_dtype=BF, tm=512, tn=512, name="ffn_up")
    x = _gmm(hid, [(w_down[None], 0)], d, resid=x, tm=512, tn=512, name="ffn_down")

    def rows_of(lo_ch, n_ch):
        return y[:, lo_ch * LANE:(lo_ch + n_ch) * LANE]

    def split(a, kvh):
        ap = a[:n_prompt].reshape(nb_p, t_p, 2, kvh, HEAD_DIM)
        as_ = a[n_prompt:n_prompt + nb_s].reshape(nb_s, 1, 2, kvh, HEAD_DIM)
        return ap, as_

    cmp_p, cmp_s = split(rows_of(CH_KC, 4), NSA_KV)
    sel_p, sel_s_rows = split(rows_of(CH_KS, 4), NSA_KV)
    win_p, win_s = split(rows_of(CH_KW, 4), NSA_KV)
    moba_p, moba_s = split(rows_of(CH_KB, 16), MOBA_HEADS)
    new_win_p = win_p[:, t_p - min(NSA_WINDOW, t_p):]
    win_all = jnp.concatenate([state_win[li], win_s], axis=1)
    keep = min(NSA_WINDOW, past_len + 1)
    new_win_s = win_all[:, win_all.shape[1] - keep:]
    return x, (cmp_p, cmp_s, sel_p, sel_s_rows, new_win_p, new_win_s, moba_p, moba_s)


def _conv_layer(x, n_prompt, nb_p, t_p, nb_s, prm, states):
    (conv_norm, w_in, pool_w, pool_scale, dw, dw_b, ln_g, ln_b, pw, w_out, moe_norm, router_w, wg, wu, wd) = prm
    state_pool, state_conv = states
    d = x.shape[1]
    width = pool_w.shape[0] * pool_w.shape[1]
    xn = _rmsnorm(x, conv_norm)
    w3 = w_in[None]
    u_pool = _gmm(xn, [(w3, 0)], width, tm=512, tn=width, name="conv_in_pool")
    u_conv = _gmm(xn, [(w3, width), (w3, 2 * width)], width, epi="glu", tm=512, tn=512, name="conv_in_glu")
    prm_mix = (pool_w.astype(BF), pool_scale[None], jnp.pad(dw, ((0, 1), (0, 0))), dw_b[None], ln_g[None], ln_b[None],
               pw.astype(BF))
    pool_buf = max(POOL_WINDOWS) - 1
    conv_buf = CONV_WIDTH - 1
    hp_rows, hc_rows = 16, 32
    ts = min(256, t_p)
    nt = t_p // ts
    mixed_p = _convmix(u_pool, u_pool, u_conv, u_conv, prm_mix, ts=ts, nt=nt, n_tiles=nb_p * nt, zero_first=True,
                       avail0=0, hp_rows=hp_rows, hc_rows=hc_rows,
                       halo_blk=lambda s, hr: jnp.maximum(s * (ts // hr) - 1, 0))
    rows = x.shape[0]
    us_pool = jnp.zeros((nb_s, SUBLANE, width), F32).at[:, 0].set(u_pool[n_prompt:n_prompt + nb_s])
    us_conv = jnp.zeros((nb_s, SUBLANE, width), F32).at[:, 0].set(u_conv[n_prompt:n_prompt + nb_s])
    hs_pool = jnp.pad(state_pool, ((0, 0), (hp_rows - pool_buf, 0), (0, 0))).reshape(nb_s * hp_rows, width)
    hs_conv = jnp.pad(state_conv, ((0, 0), (hc_rows - conv_buf, 0), (0, 0))).reshape(nb_s * hc_rows, width)
    mixed_s = _convmix(us_pool.reshape(nb_s * SUBLANE, width), hs_pool, us_conv.reshape(nb_s * SUBLANE, width), hs_conv,
                       prm_mix, ts=SUBLANE, nt=1, n_tiles=nb_s, zero_first=False, avail0=pool_buf, hp_rows=hp_rows,
                       hc_rows=hc_rows, halo_blk=lambda s, hr: s)
    mixed_s = mixed_s.reshape(nb_s, SUBLANE, 2 * width)[:, 0]
    mixed = jnp.concatenate([mixed_p, jnp.pad(mixed_s, ((0, rows - n_prompt - nb_s), (0, 0)))], axis=0)
    x = _gmm(mixed, [(w_out[None], 0)], d, resid=x, tm=512, tn=1024, name="conv_out")
    x = _moe(x, moe_norm, router_w, wg, wu, wd)

    up = u_pool[:n_prompt].reshape(nb_p, t_p, width)
    uc = u_conv[:n_prompt].reshape(nb_p, t_p, width)
    new_pool_p = up[:, t_p - pool_buf:]
    assert t_p >= conv_buf and t_p >= pool_buf
    new_conv_p = uc[:, t_p - conv_buf:]
    new_pool_s = jnp.concatenate([state_pool, u_pool[n_prompt:n_prompt + nb_s][:, None]], axis=1)[:, 1:]
    new_conv_s = jnp.concatenate([state_conv, u_conv[n_prompt:n_prompt + nb_s][:, None]], axis=1)[:, 1:]
    return x, (new_pool_p, new_pool_s, new_conv_p, new_conv_s)


def kernel(x_prompt, x_sample, cache_nsa_cmp, cache_nsa_sel, state_nsa_win, cache_moba, state_pool, state_conv, page_table, attn_norm, w_attn_in, nsa_qk_norm, nsa_cmp_pe, nsa_cmp_w1, nsa_cmp_w2, moba_qk_norm, w_attn_out, ffn_norm, ffn_w_gate, ffn_w_up, ffn_w_down, conv_norm, w_conv_in, pool_w, pool_scale, conv_dw, conv_dw_b, conv_ln_g, conv_ln_b, conv_pw, w_conv_out, moe_norm, router_w, moe_w_gate, moe_w_up, moe_w_down):
    nb_p, t_p, d = x_prompt.shape
    nb_s = x_sample.shape[0]
    n_prompt = nb_p * t_p
    n_pad = 2 * SUBLANE
    assert x_sample.shape[1] == 1 and nb_s <= n_pad
    x = jnp.concatenate([x_prompt.reshape(n_prompt, d), x_sample.reshape(nb_s, d),
                         jnp.zeros((n_pad - nb_s, d), F32)], axis=0)
    li = 0
    prm_a = (attn_norm[li], w_attn_in[li], nsa_qk_norm[li], nsa_cmp_pe[li], nsa_cmp_w1[li], nsa_cmp_w2[li],
             moba_qk_norm[li], w_attn_out[li], ffn_norm[li], ffn_w_gate[li], ffn_w_up[li], ffn_w_down[li])
    x, attn_new = _attn_layer(x, n_prompt, nb_p, t_p, nb_s, li, prm_a,
                              (cache_nsa_cmp, cache_nsa_sel, state_nsa_win, cache_moba), page_table)
    prm_c = (conv_norm[li], w_conv_in[li], pool_w[li], pool_scale[li], conv_dw[li], conv_dw_b[li], conv_ln_g[li],
             conv_ln_b[li], conv_pw[li], w_conv_out[li], moe_norm[li], router_w[li], moe_w_gate[li], moe_w_up[li],
             moe_w_down[li])
    x, conv_new = _conv_layer(x, n_prompt, nb_p, t_p, nb_s, prm_c, (state_pool[li], state_conv[li]))
    cmp_p, cmp_s, sel_p, sel_s, win_p, win_s, moba_p, moba_s = attn_new
    pool_p, pool_s, conv_p, conv_s = conv_new
    y_p = x[:n_prompt].reshape(nb_p, t_p, d)
    y_s = x[n_prompt:n_prompt + nb_s].reshape(nb_s, 1, d)
    st = lambda a: a[None]
    return (y_p, y_s, st(cmp_p), st(cmp_s), st(sel_p), st(sel_s), st(win_p), st(win_s), st(moba_p), st(moba_s),
            st(pool_p), st(pool_s), st(conv_p), st(conv_s))
```

```python
import functools
import math

import jax
import jax.numpy as jnp
from jax import lax
from jax.experimental import pallas as pl
from jax.experimental.pallas import tpu as pltpu

F32 = jnp.float32
BF = jnp.bfloat16
NEG_INF = float("-inf")

HEAD_DIM = 128
NORM_EPS = 1e-6
NSA_HEADS = 8
NSA_KV = 2
NSA_HPG = NSA_HEADS // NSA_KV
NSA_CMP_LEN = 32
NSA_CMP_STRIDE = 16
NSA_CMP_HIDDEN = 2 * HEAD_DIM
NSA_SEL_BLOCK = 64
NSA_SEL_COUNT = 16
NSA_WINDOW = 512
NSA_SEL_FORCE = 1.0e4
MOBA_HEADS = 8
MOBA_BLOCK = 256
MOBA_TOPK = 3
POOL_WINDOWS = (2, 4, 8, 16)
CONV_WIDTH = 31
N_EXPERTS = 8
TOP_K = 2

LANE = 128
SUBLANE = 8
VMEM_LIMIT = 58 * 1024 * 1024

CH_QA, CH_KC, CH_VC, CH_KS, CH_VS, CH_KW, CH_VW, CH_QB, CH_KB, CH_VB, CH_GATE = 0, 8, 10, 12, 14, 16, 18, 20, 28, 36, 44
ATTN_CHUNKS = 45

NT_DIMS = (((1,), (1,)), ((), ()))


def _cparams(sem):
    return pltpu.CompilerParams(dimension_semantics=sem, vmem_limit_bytes=VMEM_LIMIT)


def _split_bf16(a):
    hi = a.astype(BF)
    lo = (a - hi.astype(F32)).astype(BF)
    return hi, lo


def _dot3(a, b, dims=(((1,), (0,)), ((), ()))):
    ah, al = _split_bf16(a)
    bh, bl = _split_bf16(b)
    d = lambda x, y: lax.dot_general(x, y, dims, preferred_element_type=F32)
    return d(ah, bh) + d(ah, bl) + d(al, bh)


def _masked_softmax(s, mask, axis):
    s = jnp.where(mask, s, NEG_INF)
    m = jnp.max(s, axis=axis, keepdims=True)
    m = jnp.where(m == NEG_INF, 0.0, m)
    p = jnp.exp(s - m)
    d = jnp.sum(p, axis=axis, keepdims=True)
    return p / jnp.where(d > 0, d, 1.0)


def _rmsnorm_body(x_ref, g_ref, o_ref):
    x = x_ref[...]
    ms = jnp.mean(x * x, axis=-1, keepdims=True)
    o_ref[...] = (x * lax.rsqrt(ms + NORM_EPS) * g_ref[...]).astype(o_ref.dtype)


def _rmsnorm(x, g, tm=256):
    m, d = x.shape
    return pl.pallas_call(
        _rmsnorm_body,
        grid=(pl.cdiv(m, tm),),
        in_specs=[pl.BlockSpec((tm, d), lambda i: (i, 0)), pl.BlockSpec((1, d), lambda i: (0, 0))],
        out_specs=pl.BlockSpec((tm, d), lambda i: (i, 0)),
        out_shape=jax.ShapeDtypeStruct((m, d), BF),
        compiler_params=_cparams(("parallel",)),
        name="rmsnorm",
    )(x, g.reshape(1, d))


def _gmm_body(texp_ref, tval_ref, *refs, n_w, epi, has_resid, tn, n_o):
    x_ref = refs[0]
    w_refs = refs[1:1 + n_w]
    pos = 1 + n_w
    if epi == "headnorm":
        gain_ref, flag_ref = refs[pos], refs[pos + 1]
        pos += 2
    if has_resid:
        resid_ref = refs[pos]
        pos += 1
    o_refs = refs[pos:pos + n_o]
    o_ref = o_refs[0]
    wb_refs = refs[pos + n_o:pos + n_o + n_w]

    i = pl.program_id(1)
    changed = jnp.logical_or(i == 0, texp_ref[i] != texp_ref[jnp.maximum(i - 1, 0)])

    @pl.when(changed)
    def _():
        for k in range(n_w):
            wb_refs[k][...] = w_refs[k][...].astype(BF)

    @pl.when(tval_ref[i] == 0)
    def _():
        for o in o_refs:
            o[...] = jnp.zeros(o.shape, o.dtype)

    @pl.when(tval_ref[i] > 0)
    def _():
        x = x_ref[...]
        a = jnp.dot(x, wb_refs[0][...], preferred_element_type=F32)
        if epi == "swiglu":
            b = jnp.dot(x, wb_refs[1][...], preferred_element_type=F32)
            y = a * jax.nn.sigmoid(a) * b
        elif epi == "glu":
            b = jnp.dot(x, wb_refs[1][...], preferred_element_type=F32)
            y = a * jax.nn.sigmoid(b)
        else:
            y = a
        if has_resid:
            y = y + resid_ref[...]
        if epi == "headnorm":
            for c in range(tn // LANE):
                sl = slice(c * LANE, (c + 1) * LANE)
                yc = y[:, sl]
                r = lax.rsqrt(jnp.mean(yc * yc, axis=-1, keepdims=True) + NORM_EPS)
                f = flag_ref[:, sl]
                yn = yc * (f * r + (1.0 - f)) * gain_ref[:, sl]
                for o in o_refs:
                    o[:, sl] = yn.astype(o.dtype)
        else:
            for o in o_refs:
                o[...] = y.astype(o.dtype)


def _gmm(x, ws, n_out, *, epi="none", texp=None, tval=None, resid=None, gain=None, flag=None,
         out_dtype=F32, tm=256, tn=512, name="gmm"):
    m, kdim = x.shape
    tn = min(tn, n_out)
    n_m = pl.cdiv(m, tm)
    n_n = n_out // tn
    assert n_n * tn == n_out
    if texp is None:
        texp = jnp.zeros((n_m,), jnp.int32)
        tval = jnp.ones((n_m,), jnp.int32)
    n_w = len(ws)
    in_specs = [pl.BlockSpec((tm, kdim), lambda j, i, te, tv: (i, 0))]
    args = [x]
    for w, off in ws:
        assert off % tn == 0 and w.shape[1] == kdim
        ob = off // tn
        in_specs.append(pl.BlockSpec((None, kdim, tn), lambda j, i, te, tv, ob=ob: (te[i], 0, j + ob)))
        args.append(w)
    if epi == "headnorm":
        in_specs += [pl.BlockSpec((1, tn), lambda j, i, te, tv: (0, j))] * 2
        args += [gain, flag]
    if resid is not None:
        in_specs.append(pl.BlockSpec((tm, tn), lambda j, i, te, tv: (i, j)))
        args.append(resid)
    dtypes = out_dtype if isinstance(out_dtype, tuple) else (out_dtype,)
    body = functools.partial(_gmm_body, n_w=n_w, epi=epi, has_resid=resid is not None, tn=tn, n_o=len(dtypes))
    outs = pl.pallas_call(
        body,
        grid_spec=pltpu.PrefetchScalarGridSpec(
            num_scalar_prefetch=2,
            grid=(n_n, n_m),
            in_specs=in_specs,
            out_specs=[pl.BlockSpec((tm, tn), lambda j, i, te, tv: (i, j)) for _ in dtypes],
            scratch_shapes=[pltpu.VMEM((kdim, tn), BF) for _ in range(n_w)],
        ),
        out_shape=[jax.ShapeDtypeStruct((m, n_out), dt) for dt in dtypes],
        compiler_params=_cparams(("arbitrary", "arbitrary")),
        name=name,
    )(texp, tval, *args)
    return outs if isinstance(out_dtype, tuple) else outs[0]


def _c1_body(x_ref, w_ref, o_ref):
    nsb = x_ref.shape[0] // NSA_CMP_STRIDE
    acc = jnp.zeros((nsb, 2 * NSA_CMP_HIDDEN), F32)
    for s in range(NSA_CMP_STRIDE):
        xs = x_ref[pl.ds(s, nsb, stride=NSA_CMP_STRIDE), :]
        acc = acc + jnp.dot(xs.astype(BF), w_ref[s], preferred_element_type=F32)
    o_ref[...] = acc


def _compress_c1_prompt(y, wcat, nb, t):
    nsub = t // NSA_CMP_STRIDE
    return pl.pallas_call(
        _c1_body,
        grid=(nb, 2, NSA_KV),
        in_specs=[pl.BlockSpec((t, LANE), lambda b, kv, g: (b, CH_KC + 2 * kv + g)),
                  pl.BlockSpec((None, NSA_CMP_STRIDE, HEAD_DIM, 2 * NSA_CMP_HIDDEN), lambda b, kv, g: (kv, 0, 0, 0))],
        out_specs=pl.BlockSpec((None, None, None, nsub, 2 * NSA_CMP_HIDDEN), lambda b, kv, g: (b, kv, g, 0, 0)),
        out_shape=jax.ShapeDtypeStruct((nb, 2, NSA_KV, nsub, 2 * NSA_CMP_HIDDEN), F32),
        compiler_params=_cparams(("parallel", "parallel", "parallel")),
        name="cmp_c1_prompt",
    )(y, wcat)


C1_PAGES = 16


def _c1_sample_body(pt_ref, *refs, n_in, page):
    x_refs = refs[:n_in]
    w_ref, o_ref, scr = refs[n_in:n_in + 3]
    nsb = page // NSA_CMP_STRIDE
    rows = n_in * nsb
    for kv in range(2):
        acc = jnp.zeros((rows * NSA_KV, 2 * NSA_CMP_HIDDEN), F32)

        def rows_at(s):
            parts = [xr[pl.ds(s, nsb, stride=NSA_CMP_STRIDE), kv, :, :].reshape(nsb * NSA_KV, HEAD_DIM) for xr in x_refs]
            return (parts[0] if n_in == 1 else jnp.concatenate(parts, axis=0)).astype(BF)

        for s in range(0, NSA_CMP_STRIDE, 2):
            xs = jnp.concatenate([rows_at(s), rows_at(s + 1)], axis=1)
            w = w_ref[kv, s:s + 2].reshape(2 * HEAD_DIM, 2 * NSA_CMP_HIDDEN)
            acc = acc + jnp.dot(xs, w, preferred_element_type=F32)
        for c in range(scr.shape[0]):
            scr[c] = acc[:, c * LANE:(c + 1) * LANE]
        for g in range(NSA_KV):
            for c in range(scr.shape[0]):
                o_ref[kv, g, :, c * LANE:(c + 1) * LANE] = scr[c, pl.ds(g, rows, stride=NSA_KV), :]


def _compress_c1_sample(cache, li, page_table, wcat):
    nb, n_pages = page_table.shape
    page = cache.shape[2]
    npp = math.gcd(C1_PAGES, n_pages)
    nsb = page // NSA_CMP_STRIDE
    nsub = n_pages * nsb
    body = functools.partial(_c1_sample_body, n_in=npp, page=page)
    in_specs = [
        pl.BlockSpec((None, None, page, 2, NSA_KV, HEAD_DIM), lambda b, j, pt, i=i: (li, pt[b, j * npp + i], 0, 0, 0, 0))
        for i in range(npp)
    ]
    in_specs.append(pl.BlockSpec(wcat.shape, lambda b, j, pt: (0, 0, 0, 0)))
    return pl.pallas_call(
        body,
        grid_spec=pltpu.PrefetchScalarGridSpec(
            num_scalar_prefetch=1,
            grid=(nb, n_pages // npp),
            in_specs=in_specs,
            out_specs=pl.BlockSpec((None, 2, NSA_KV, npp * nsb, 2 * NSA_CMP_HIDDEN), lambda b, j, pt: (b, 0, 0, j, 0)),
            scratch_shapes=[pltpu.VMEM((2 * NSA_CMP_HIDDEN // LANE, npp * nsb * NSA_KV, LANE), F32)],
        ),
        out_shape=jax.ShapeDtypeStruct((nb, 2, NSA_KV, nsub, 2 * NSA_CMP_HIDDEN), F32),
        compiler_params=_cparams(("parallel", "parallel")),
        name="cmp_c1_sample",
    )(page_table, *([cache] * npp), wcat)


def _c2_body(p_ref, pe_ref, wcat_ref, w2_ref, g_ref, o_ref, *, nsub, n_c):
    kv = pl.program_id(1)
    pep = jnp.zeros((SUBLANE, 2 * NSA_CMP_HIDDEN), F32)
    for s in range(NSA_CMP_STRIDE):
        pep = pep + jnp.dot(pe_ref[s], wcat_ref[s], preferred_element_type=F32)
    bias = pep[0:1, :NSA_CMP_HIDDEN] + pep[1:2, NSA_CMP_HIDDEN:]
    nxt = pltpu.roll(p_ref[:, NSA_CMP_HIDDEN:], nsub - 1, 0)
    h = p_ref[:, :NSA_CMP_HIDDEN] + nxt + bias
    h = h * jax.nn.sigmoid(h)
    o = jnp.dot(h.astype(BF), w2_ref[...], preferred_element_type=F32)
    r = lax.rsqrt(jnp.mean(o * o, axis=-1, keepdims=True) + NORM_EPS)
    o = jnp.where(kv == 0, o * r * g_ref[...], o)
    row = lax.broadcasted_iota(jnp.int32, (nsub, 1), 0)
    o_ref[...] = jnp.where(row < n_c, o, 0.0)


def _compress_c2(p, pe_rows, wcat, w2, gk):
    nb, _, _, nsub, _ = p.shape
    n_c = nsub - NSA_CMP_LEN // NSA_CMP_STRIDE + 1
    body = functools.partial(_c2_body, nsub=nsub, n_c=n_c)
    return pl.pallas_call(
        body,
        grid=(nb, 2, NSA_KV),
        in_specs=[pl.BlockSpec((None, None, None, nsub, 2 * NSA_CMP_HIDDEN), lambda b, kv, g: (b, kv, g, 0, 0)),
                  pl.BlockSpec((None, NSA_CMP_STRIDE, SUBLANE, HEAD_DIM), lambda b, kv, g: (kv, 0, 0, 0)),
                  pl.BlockSpec((None, NSA_CMP_STRIDE, HEAD_DIM, 2 * NSA_CMP_HIDDEN), lambda b, kv, g: (kv, 0, 0, 0)),
                  pl.BlockSpec((None, NSA_CMP_HIDDEN, HEAD_DIM), lambda b, kv, g: (kv, 0, 0)),
                  pl.BlockSpec((1, HEAD_DIM), lambda b, kv, g: (0, 0))],
        out_specs=pl.BlockSpec((None, None, None, nsub, HEAD_DIM), lambda b, kv, g: (b, kv, g, 0, 0)),
        out_shape=jax.ShapeDtypeStruct((nb, 2, NSA_KV, nsub, HEAD_DIM), F32),
        compiler_params=_cparams(("parallel", "parallel", "parallel")),
        name="cmp_c2",
    )(p, pe_rows, wcat, w2, gk)


def _rank_select(sc_ref, n_iter, k):
    score = sc_ref[...]
    brow = lax.broadcasted_iota(jnp.int32, score.shape, 0)

    def body(j, rank):
        r = sc_ref[pl.ds(j, 1), :]
        beats = jnp.logical_or(r > score, jnp.logical_and(r == score, j < brow))
        return rank + jnp.where(beats, 1.0, 0.0)

    rank = lax.fori_loop(0, n_iter, body, jnp.zeros(score.shape, F32))
    return rank < k


def _cattn_body(q_ref, kc_ref, vc_ref, oc_ref, sel_ref, sc_ref, *, tq, ncp, n_c, ns, nsp, pos0, ksel):
    t0 = pos0 + pl.program_id(2) * tq
    scale = HEAD_DIM ** -0.5
    kc = kc_ref[...].astype(BF)
    vc = vc_ref[...].astype(BF)
    trow = t0 + lax.broadcasted_iota(jnp.int32, (tq, 1), 0)
    ncol = lax.broadcasted_iota(jnp.int32, (1, ncp), 1)
    valid = jnp.logical_and(ncol * NSA_CMP_STRIDE + (NSA_CMP_LEN - 1) <= trow, ncol < n_c)
    tcol = t0 + lax.broadcasted_iota(jnp.int32, (1, tq), 1)
    nrow = lax.broadcasted_iota(jnp.int32, (ncp, 1), 0)
    valid_t = jnp.logical_and(nrow * NSA_CMP_STRIDE + (NSA_CMP_LEN - 1) <= tcol, nrow < n_c)
    psum_t = jnp.zeros((ncp, tq), F32)
    for z in range(NSA_HPG):
        sl = slice(z * HEAD_DIM, (z + 1) * HEAD_DIM)
        q = (q_ref[:, sl] * scale).astype(BF)
        s = lax.dot_general(q, kc, NT_DIMS, preferred_element_type=F32)
        p = _masked_softmax(s, valid, -1)
        oc_ref[:, sl] = jnp.dot(p.astype(BF), vc, preferred_element_type=F32)
        s_t = lax.dot_general(kc, q, NT_DIMS, preferred_element_type=F32)
        psum_t = psum_t + _masked_softmax(s_t, valid_t, 0)
    r = NSA_SEL_BLOCK // NSA_CMP_STRIDE
    brow = lax.broadcasted_iota(jnp.int32, (nsp, 1), 0)
    lo = r * brow - 1
    inside = jnp.logical_and(ncol >= lo, ncol <= lo + r)
    edge = jnp.logical_or(ncol == lo, ncol == lo + r)
    m_t = jnp.where(inside, jnp.where(edge, 0.5, 1.0), 0.0).astype(BF)
    hi = psum_t.astype(BF)
    mid = (psum_t - hi.astype(F32)).astype(BF)
    low = (psum_t - hi.astype(F32) - mid.astype(F32)).astype(BF)
    imp_t = (jnp.dot(m_t, hi, preferred_element_type=F32) + jnp.dot(m_t, mid, preferred_element_type=F32)
             + jnp.dot(m_t, low, preferred_element_type=F32))
    own = tcol // NSA_SEL_BLOCK
    past = jnp.logical_and(brow < own, brow < ns)
    forced = jnp.logical_or(brow == 0, brow == own - 1)
    sc_ref[...] = jnp.where(past, jnp.where(forced, NSA_SEL_FORCE, imp_t), NEG_INF)
    chosen = jnp.logical_and(past, _rank_select(sc_ref, ns, ksel))
    sel = jnp.where(jnp.logical_or(chosen, brow == own), 1.0, 0.0)
    rows = sel_ref.shape[0]
    if rows > nsp:
        sel_ref[...] = jnp.zeros(sel_ref.shape, F32)
    sel_ref[0:nsp, :] = sel


def _cattn(q_arr, q_row_blk0, q_col_blk0, kvc, nb, tlen, tq, length, pos0, out_rows):
    nq = tlen // tq
    ncp = kvc.shape[3]
    n_c = ncp - NSA_CMP_LEN // NSA_CMP_STRIDE + 1
    ns = length // NSA_SEL_BLOCK
    ksel = min(NSA_SEL_COUNT - 1, ns)
    nsp = -(-ns // SUBLANE) * SUBLANE
    sel_rows = max(nsp, LANE)
    body = functools.partial(_cattn_body, tq=tq, ncp=ncp, n_c=n_c, ns=ns, nsp=nsp, pos0=pos0, ksel=ksel)
    gw = NSA_HPG * HEAD_DIM
    return pl.pallas_call(
        body,
        grid=(nb, NSA_KV, nq),
        in_specs=[pl.BlockSpec((tq, gw), lambda b, g, qi: (q_row_blk0 + b * nq + qi, q_col_blk0 + g)),
                  pl.BlockSpec((None, None, None, ncp, HEAD_DIM), lambda b, g, qi: (b, 0, g, 0, 0)),
                  pl.BlockSpec((None, None, None, ncp, HEAD_DIM), lambda b, g, qi: (b, 1, g, 0, 0))],
        out_specs=[pl.BlockSpec((tq, gw), lambda b, g, qi: (b * nq + qi, g)),
                   pl.BlockSpec((None, None, sel_rows, tq), lambda b, g, qi: (b, g, 0, qi))],
        out_shape=[jax.ShapeDtypeStruct((out_rows, NSA_KV * gw), F32),
                   jax.ShapeDtypeStruct((nb, NSA_KV, sel_rows, tlen), F32)],
        scratch_shapes=[pltpu.VMEM((nsp, tq), F32)],
        compiler_params=_cparams(("parallel", "parallel", "parallel")),
        name="nsa_cmp_attn",
    )(q_arr, kvc, kvc)


def _cattn_row_body(q_ref, kc_ref, vc_ref, oc_ref, sel_ref, *, tq, ncp, n_c, ns, nsl, pos0, ksel):
    scale = HEAD_DIM ** -0.5
    kc = kc_ref[...].astype(BF)
    vc = vc_ref[...].astype(BF)
    trow = pos0 + lax.broadcasted_iota(jnp.int32, (tq, 1), 0)
    ncol = lax.broadcasted_iota(jnp.int32, (1, ncp), 1)
    valid = jnp.logical_and(ncol * NSA_CMP_STRIDE + (NSA_CMP_LEN - 1) <= trow, ncol < n_c)
    psum = jnp.zeros((tq, ncp), F32)
    for z in range(NSA_HPG):
        sl = slice(z * HEAD_DIM, (z + 1) * HEAD_DIM)
        q = (q_ref[:, sl] * scale).astype(BF)
        p = _masked_softmax(lax.dot_general(q, kc, NT_DIMS, preferred_element_type=F32), valid, -1)
        oc_ref[:, sl] = jnp.dot(p.astype(BF), vc, preferred_element_type=F32)
        psum = psum + p
    r = NSA_SEL_BLOCK // NSA_CMP_STRIDE
    nrow = lax.broadcasted_iota(jnp.int32, (ncp, 1), 0)
    bcol = lax.broadcasted_iota(jnp.int32, (1, nsl), 1)
    lo = r * bcol - 1
    inside = jnp.logical_and(nrow >= lo, nrow <= lo + r)
    edge = jnp.logical_or(nrow == lo, nrow == lo + r)
    m = jnp.where(inside, jnp.where(edge, 0.5, 1.0), 0.0).astype(BF)
    hi = psum.astype(BF)
    mid = (psum - hi.astype(F32)).astype(BF)
    low = (psum - hi.astype(F32) - mid.astype(F32)).astype(BF)
    imp = (jnp.dot(hi, m, preferred_element_type=F32) + jnp.dot(mid, m, preferred_element_type=F32)
           + jnp.dot(low, m, preferred_element_type=F32))
    own = trow // NSA_SEL_BLOCK
    past = jnp.logical_and(bcol < own, bcol < ns)
    forced = jnp.logical_or(bcol == 0, bcol == own - 1)
    score = jnp.where(past, jnp.where(forced, NSA_SEL_FORCE, imp), NEG_INF)
    rank = jnp.zeros((tq, nsl), F32)
    for j in range(ns):
        cj = score[:, j:j + 1]
        beats = jnp.logical_or(cj > score, jnp.logical_and(cj == score, j < bcol))
        rank = rank + jnp.where(beats, 1.0, 0.0)
    chosen = jnp.logical_and(past, rank < ksel)
    sel_ref[...] = jnp.where(jnp.logical_or(chosen, bcol == own), 1.0, 0.0)


def _cattn_sample(q_pad, kvc, nb, length, pos0):
    tq = SUBLANE
    ncp = kvc.shape[3]
    n_c = ncp - NSA_CMP_LEN // NSA_CMP_STRIDE + 1
    ns = length // NSA_SEL_BLOCK
    ksel = min(NSA_SEL_COUNT - 1, ns)
    nsl = -(-ns // LANE) * LANE
    body = functools.partial(_cattn_row_body, tq=tq, ncp=ncp, n_c=n_c, ns=ns, nsl=nsl, pos0=pos0, ksel=ksel)
    gw = NSA_HPG * HEAD_DIM
    return pl.pallas_call(
        body,
        grid=(nb, NSA_KV),
        in_specs=[pl.BlockSpec((tq, gw), lambda b, g: (b, g)),
                  pl.BlockSpec((None, None, None, ncp, HEAD_DIM), lambda b, g: (b, 0, g, 0, 0)),
                  pl.BlockSpec((None, None, None, ncp, HEAD_DIM), lambda b, g: (b, 1, g, 0, 0))],
        out_specs=[pl.BlockSpec((tq, gw), lambda b, g: (b, g)),
                   pl.BlockSpec((None, None, tq, nsl), lambda b, g: (b, g, 0, 0))],
        out_shape=[jax.ShapeDtypeStruct((nb * tq, NSA_KV * gw), F32),
                   jax.ShapeDtypeStruct((nb, NSA_KV, tq, nsl), F32)],
        compiler_params=_cparams(("parallel", "parallel")),
        name="nsa_cmp_attn_sample",
    )(q_pad, kvc, kvc)


def _gate_body(q_ref, k_ref, sel_ref, mean_ref, sc_ref, *, tq, nb, nbp, pos0, from_means):
    qi = pl.program_id(2)
    if from_means:
        means = k_ref[...]
    else:
        @pl.when(qi == 0)
        def _():
            mean_ref[...] = jnp.zeros(mean_ref.shape, F32)
            for j in range(nb):
                blk = k_ref[j * MOBA_BLOCK:(j + 1) * MOBA_BLOCK, :]
                mean_ref[j:j + 1, :] = jnp.sum(blk, axis=0, keepdims=True) * (1.0 / MOBA_BLOCK)

        means = mean_ref[...]
    g_t = _dot3(means, q_ref[...], NT_DIMS)
    tcol = pos0 + qi * tq + lax.broadcasted_iota(jnp.int32, (1, tq), 1)
    own = tcol // MOBA_BLOCK
    brow = lax.broadcasted_iota(jnp.int32, (nbp, 1), 0)
    past = jnp.logical_and(brow < own, brow < nb)
    sc_ref[...] = jnp.where(past, g_t, NEG_INF)
    chosen = jnp.logical_and(past, _rank_select(sc_ref, nb, min(MOBA_TOPK, nb)))
    sel = jnp.where(jnp.logical_or(chosen, brow == own), 1.0, 0.0)
    if sel_ref.shape[0] > nbp:
        sel_ref[...] = jnp.zeros(sel_ref.shape, F32)
    sel_ref[0:nbp, :] = sel


def _moba_gate_prompt(y, nb, t, tq):
    nq = t // tq
    n_full = t // MOBA_BLOCK
    nbp = -(-n_full // SUBLANE) * SUBLANE
    body = functools.partial(_gate_body, tq=tq, nb=n_full, nbp=nbp, pos0=0, from_means=False)
    return pl.pallas_call(
        body,
        grid=(nb, MOBA_HEADS, nq),
        in_specs=[pl.BlockSpec((tq, LANE), lambda b, h, qi: (b * nq + qi, CH_QB + h)),
                  pl.BlockSpec((t, LANE), lambda b, h, qi: (b, CH_KB + h))],
        out_specs=pl.BlockSpec((None, None, LANE, tq), lambda b, h, qi: (b, h, 0, qi)),
        out_shape=jax.ShapeDtypeStruct((nb, MOBA_HEADS, LANE, t), F32),
        scratch_shapes=[pltpu.VMEM((nbp, HEAD_DIM), F32), pltpu.VMEM((nbp, tq), F32)],
        compiler_params=_cparams(("parallel", "parallel", "arbitrary")),
        name="moba_gate_prompt",
    )(y, y)


def _moba_gate_sample(q_pad, means, pos0):
    nb, _, n_full, _ = means.shape
    tq = SUBLANE
    body = functools.partial(_gate_body, tq=tq, nb=n_full, nbp=n_full, pos0=pos0, from_means=True)
    return pl.pallas_call(
        body,
        grid=(nb, MOBA_HEADS, 1),
        in_specs=[pl.BlockSpec((tq, LANE), lambda b, h, qi: (b, h)),
                  pl.BlockSpec((None, None, n_full, HEAD_DIM), lambda b, h, qi: (b, h, 0, 0))],
        out_specs=pl.BlockSpec((None, None, max(n_full, LANE), tq), lambda b, h, qi: (b, h, 0, 0)),
        out_shape=jax.ShapeDtypeStruct((nb, MOBA_HEADS, max(n_full, LANE), tq), F32),
        scratch_shapes=[pltpu.VMEM((SUBLANE, HEAD_DIM), F32), pltpu.VMEM((n_full, tq), F32)],
        compiler_params=_cparams(("parallel", "parallel", "arbitrary")),
        name="moba_gate_sample",
    )(q_pad, means)


MEANS_BLOCKS = 4


def _means_body(pt_ref, *refs, ppb, bps):
    c_refs = refs[:bps * ppb]
    o_ref = refs[bps * ppb]
    j = pl.program_id(1)
    for q in range(bps):
        acc = jnp.sum(c_refs[q * ppb][...], axis=0)
        for i in range(1, ppb):
            acc = acc + jnp.sum(c_refs[q * ppb + i][...], axis=0)
        m = acc * (1.0 / MOBA_BLOCK)
        for h in range(MOBA_HEADS):
            o_ref[h, pl.ds(j * bps + q, 1), :] = m[h:h + 1, :]


def _moba_means_sample(cache, li, page_table):
    nb, n_pages = page_table.shape
    page = cache.shape[2]
    ppb = MOBA_BLOCK // page
    n_full = n_pages // ppb
    bps = math.gcd(MEANS_BLOCKS, n_full)
    body = functools.partial(_means_body, ppb=ppb, bps=bps)
    in_specs = [
        pl.BlockSpec((None, None, page, None, MOBA_HEADS, HEAD_DIM),
                     lambda b, j, pt, i=i: (li, pt[b, j * bps * ppb + i], 0, 0, 0, 0))
        for i in range(bps * ppb)
    ]
    return pl.pallas_call(
        body,
        grid_spec=pltpu.PrefetchScalarGridSpec(
            num_scalar_prefetch=1,
            grid=(nb, n_full // bps),
            in_specs=in_specs,
            out_specs=pl.BlockSpec((None, MOBA_HEADS, n_full, HEAD_DIM), lambda b, j, pt: (b, 0, 0, 0)),
        ),
        out_shape=jax.ShapeDtypeStruct((nb, MOBA_HEADS, n_full, HEAD_DIM), F32),
        compiler_params=_cparams(("parallel", "arbitrary")),
        name="moba_means_sample",
    )(page_table, *([cache] * (bps * ppb)))


FLASH_UNIT_ROWS = 256
LOG2_E = 1.4426950408889634
MASK_BIG = 2.0 ** 60


def _flash_body(*refs, tq, tk, nh, blk, window, masked, ur):
    q_ref, k_ref, v_ref = refs[:3]
    pos = 3
    if masked:
        sel_ref = refs[pos]
        pos += 1
    o_ref, qs_ref, kt_ref = refs[pos:pos + 3]
    pos += 1
    units = [(z, r0) for z in range(nh) for r0 in range(0, tq, ur)]
    stat = refs[pos + 2:pos + 2 + 2 * len(units)]
    m_refs, acc_refs = stat[0::2], stat[1::2]
    qi = pl.program_id(2)
    t0 = qi * tq
    scale = HEAD_DIM ** -0.5 * LOG2_E
    if masked:
        bias = ((jnp.transpose(sel_ref[...]) - 1.0) * MASK_BIG).astype(BF)
    for z in range(nh):
        qz = (q_ref[:, z * HEAD_DIM:(z + 1) * HEAD_DIM] * scale).astype(BF)
        qs_ref[z] = jnp.concatenate([qz, bias], axis=1) if masked else qz
    for u in range(len(units)):
        m_refs[u][...] = jnp.full(m_refs[u].shape, NEG_INF, F32)
        acc_refs[u][...] = jnp.zeros(acc_refs[u].shape, F32)
    trow = t0 + lax.broadcasted_iota(jnp.int32, (tq, 1), 0)

    @pl.when(qi == 0)
    def _():
        for c in range(kt_ref.shape[0]):
            kc = k_ref[c * tk:(c + 1) * tk, :]
            if masked:
                prow = c * tk + lax.broadcasted_iota(jnp.int32, (tk, 1), 0)
                jcol = lax.broadcasted_iota(jnp.int32, (1, LANE), 1)
                kc = jnp.concatenate([kc, jnp.where(prow // blk == jcol, 1.0, 0.0).astype(BF)], axis=1)
            kt_ref[c] = kc.T

    def process(c, edge):
        ks = pl.multiple_of(c * tk, tk)
        kt = kt_ref[c]
        vc = jnp.concatenate([v_ref[pl.ds(ks, tk), :], jnp.ones((tk, LANE), BF)], axis=1)
        if edge:
            pcol = ks + lax.broadcasted_iota(jnp.int32, (1, tk), 1)
            valid = pcol <= trow
            if window:
                valid = jnp.logical_and(valid, pcol > trow - window)
        for u, (z, r0) in enumerate(units):
            s = jnp.dot(qs_ref[z, r0:r0 + ur, :], kt, preferred_element_type=F32)
            m_old = m_refs[u][...]
            if edge:
                s = jnp.where(valid[r0:r0 + ur], s, NEG_INF)
                m_new = jnp.maximum(m_old, jnp.max(s, axis=-1, keepdims=True))
                m_use = jnp.where(m_new == NEG_INF, 0.0, m_new)
            else:
                m_new = jnp.maximum(m_old, jnp.max(s, axis=-1, keepdims=True))
                m_use = m_new
            alpha = jnp.exp2(m_old - m_use)
            p = jnp.exp2(s - m_use)
            acc_refs[u][...] = alpha * acc_refs[u][...] + jnp.dot(p.astype(BF), vc, preferred_element_type=F32)
            m_refs[u][...] = m_new

    if masked:
        def pair(c2, carry):
            process(2 * c2, False)
            process(2 * c2 + 1, False)
            return carry

        lax.fori_loop(0, qi // 2, pair, 0)

        @pl.when(qi % 2 == 1)
        def _():
            process(qi - 1, False)

        process(qi, True)
    else:
        c_hi = (t0 + tq + tk - 1) // tk
        c_lo = jnp.maximum(t0 - window + 1, 0) // tk
        lax.fori_loop(c_lo, c_hi, lambda c, carry: (process(c, True), carry)[1], 0)
    for u, (z, r0) in enumerate(units):
        l = acc_refs[u][:, HEAD_DIM:HEAD_DIM + 1]
        o_ref[r0:r0 + ur, z * HEAD_DIM:(z + 1) * HEAD_DIM] = acc_refs[u][:, :HEAD_DIM] / jnp.where(l > 0, l, 1.0)


def _flash(y, y16, sel, nb, t, *, q_ch, k_ch, v_ch, kvh, nh, blk, window, out_rows, tq=512, tk=512):
    tq = min(tq, t)
    tk = min(tk, t)
    nq = t // tq
    masked = sel is not None
    gw = nh * HEAD_DIM
    assert q_ch % nh == 0 and (tq == tk or not masked)
    in_specs = [pl.BlockSpec((tq, gw), lambda b, h, qi: (b * nq + qi, q_ch // nh + h)),
                pl.BlockSpec((t, LANE), lambda b, h, qi: (b, k_ch + h)),
                pl.BlockSpec((t, LANE), lambda b, h, qi: (b, v_ch + h))]
    args = [y, y16, y16]
    if masked:
        in_specs.append(pl.BlockSpec((None, None, LANE, tq), lambda b, h, qi: (b, h, 0, qi)))
        args.append(sel)
    ur = min(FLASH_UNIT_ROWS, tq)
    n_units = nh * (tq // ur)
    kd = (2 if masked else 1) * HEAD_DIM
    body = functools.partial(_flash_body, tq=tq, tk=tk, nh=nh, blk=blk, window=window, masked=masked, ur=ur)
    unit_scratch = [pltpu.VMEM((ur, 1), F32), pltpu.VMEM((ur, 2 * HEAD_DIM), F32)]
    return pl.pallas_call(
        body,
        grid=(nb, kvh, nq),
        in_specs=in_specs,
        out_specs=pl.BlockSpec((tq, gw), lambda b, h, qi: (b * nq + qi, h)),
        out_shape=jax.ShapeDtypeStruct((out_rows, kvh * gw), F32),
        scratch_shapes=[pltpu.VMEM((nh, tq, kd), BF), pltpu.VMEM((t // tk, kd, tk), BF)] + unit_scratch * n_units,
        compiler_params=_cparams(("parallel", "parallel", "arbitrary")),
        name="flash_" + ("win" if window else "blk%d" % blk),
    )(*args)


def _dec_body(tbl_ref, vld_ref, q_ref, *refs, nh, rows, kvh, nblk, r_min):
    k_refs = refs[:nblk]
    v_refs = refs[nblk:2 * nblk]
    kn_ref, vn_ref, o_ref = refs[2 * nblk:2 * nblk + 3]
    b = pl.program_id(0)
    h = pl.program_id(1)
    scale = HEAD_DIM ** -0.5
    qrow = q_ref[pl.ds(b, 1), :] * scale
    zrow = lax.broadcasted_iota(jnp.int32, (SUBLANE, 1), 0)
    qm = jnp.zeros((SUBLANE, HEAD_DIM), F32)
    for z in range(nh):
        qm = jnp.where(zrow == z, qrow[:, z * HEAD_DIM:(z + 1) * HEAD_DIM], qm)
    qb = qm.astype(BF)
    col = lax.broadcasted_iota(jnp.int32, (1, rows * kvh), 1)
    valid = jnp.logical_and(col % kvh == h, col // kvh >= r_min)
    kn = kn_ref[pl.ds(b, 1), :]
    vn = vn_ref[pl.ds(b, 1), :]
    s_new = jnp.sum(qm * kn, axis=-1, keepdims=True)
    scores = []
    m = s_new
    for i in range(nblk):
        kk = k_refs[i][...].reshape(rows * kvh, HEAD_DIM).astype(BF)
        s = lax.dot_general(qb, kk, NT_DIMS, preferred_element_type=F32)
        s = jnp.where(jnp.logical_and(valid, vld_ref[b, h, i] > 0), s, NEG_INF)
        scores.append(s)
        m = jnp.maximum(m, jnp.max(s, axis=-1, keepdims=True))
    p_new = jnp.exp(s_new - m)
    l = p_new
    acc = p_new * vn
    for i in range(nblk):
        p = jnp.exp(scores[i] - m)
        l = l + jnp.sum(p, axis=-1, keepdims=True)
        vv = v_refs[i][...].reshape(rows * kvh, HEAD_DIM).astype(BF)
        acc = acc + jnp.dot(p.astype(BF), vv, preferred_element_type=F32)
    o_ref[...] = acc / l


def _decode_attn(ys, cache, li_fixed, tbl, vld, *, q_ch, kn_ch, vn_ch, nh, rows, kvh, r_min, name):
    nb, _, nblk = tbl.shape
    page = cache.shape[2]
    bpp = page // rows
    gw = nh * HEAD_DIM
    body = functools.partial(_dec_body, nh=nh, rows=rows, kvh=kvh, nblk=nblk, r_min=r_min)

    def kv_spec(kv, i):
        def imap(b, h, tb, vl):
            t = tb[b, h, i]
            return (li_fixed, t // bpp, t % bpp, kv, 0, 0)

        return pl.BlockSpec((None, None, rows, None, kvh, HEAD_DIM), imap)

    nrow = ys.shape[0]
    return pl.pallas_call(
        body,
        grid_spec=pltpu.PrefetchScalarGridSpec(
            num_scalar_prefetch=2,
            grid=(nb, kvh),
            in_specs=[pl.BlockSpec((nrow, gw), lambda b, h, tb, vl: (0, q_ch // nh + h))]
            + [kv_spec(0, i) for i in range(nblk)] + [kv_spec(1, i) for i in range(nblk)]
            + [pl.BlockSpec((nrow, LANE), lambda b, h, tb, vl: (0, kn_ch + h)),
               pl.BlockSpec((nrow, LANE), lambda b, h, tb, vl: (0, vn_ch + h))],
            out_specs=pl.BlockSpec((None, None, SUBLANE, HEAD_DIM), lambda b, h, tb, vl: (b, h, 0, 0)),
        ),
        out_shape=jax.ShapeDtypeStruct((nb, kvh, SUBLANE, HEAD_DIM), F32),
        compiler_params=_cparams(("parallel", "parallel")),
        name=name,
    )(tbl, vld, ys, *([cache] * (2 * nblk)), ys, ys)


def _mix_body(oc_ref, os_ref, ow_ref, ob_ref, soc_ref, sos_ref, sow_ref, sob_ref, gate_ref, o_ref, *, n_p_tiles, n_s):
    i = pl.program_id(0)
    w = NSA_HEADS * HEAD_DIM

    def emit(oc, os_, ow, ob, rows):
        g = jax.nn.sigmoid(gate_ref[0:rows, :])
        for h in range(NSA_HEADS):
            sl = slice(h * HEAD_DIM, (h + 1) * HEAD_DIM)
            o = (g[:, 3 * h:3 * h + 1] * oc[:, sl] + g[:, 3 * h + 1:3 * h + 2] * os_[:, sl]
                 + g[:, 3 * h + 2:3 * h + 3] * ow[:, sl])
            o_ref[0:rows, sl] = o.astype(o_ref.dtype)
        o_ref[0:rows, w:] = ob[...].astype(o_ref.dtype)

    @pl.when(i < n_p_tiles)
    def _():
        emit(oc_ref, os_ref, ow_ref, ob_ref, o_ref.shape[0])

    @pl.when(i >= n_p_tiles)
    def _():
        emit(soc_ref, sos_ref, sow_ref, sob_ref, n_s)


def _mix(prompt_parts, sample_parts, y, tm=256):
    n_prompt = prompt_parts[0].shape[0]
    n_s = sample_parts[0].shape[0]
    m = y.shape[0]
    tm = min(tm, n_prompt)
    assert n_prompt % tm == 0 and m == n_prompt + n_s and n_s <= tm
    n_p_tiles = n_prompt // tm
    wa = NSA_HEADS * HEAD_DIM
    wb = MOBA_HEADS * HEAD_DIM
    pmap = lambda i: (jnp.minimum(i, n_p_tiles - 1), 0)
    body = functools.partial(_mix_body, n_p_tiles=n_p_tiles, n_s=n_s)
    return pl.pallas_call(
        body,
        grid=(n_p_tiles + 1,),
        in_specs=[pl.BlockSpec((tm, wa), pmap)] * 3 + [pl.BlockSpec((tm, wb), pmap)]
        + [pl.BlockSpec((n_s, wa), lambda i: (0, 0))] * 3 + [pl.BlockSpec((n_s, wb), lambda i: (0, 0))]
        + [pl.BlockSpec((tm, LANE), lambda i: (i, CH_GATE))],
        out_specs=pl.BlockSpec((tm, wa + wb), lambda i: (i, 0)),
        out_shape=jax.ShapeDtypeStruct((m, wa + wb), BF),
        compiler_params=_cparams(("parallel",)),
        name="attn_mix",
    )(*prompt_parts, *sample_parts, y)


def _convmix_body(up_ref, hp_ref, uc_ref, hc_ref, pw_ref, ps_ref, dw_ref, dwb_ref, lng_ref, lnb_ref, pww_ref,
                  o_ref, zp_ref, zc_ref, cb_ref, *, ts, nt, zero_first, avail0, rc):
    s = pl.program_id(0)
    hp_rows = hp_ref.shape[0]
    hc_rows = hc_ref.shape[0]
    width = up_ref.shape[1]
    if zero_first:
        keep = jnp.where(s % nt == 0, 0.0, 1.0)
        zp_ref[0:hp_rows, :] = hp_ref[...] * keep
        zc_ref[0:hc_rows, :] = hc_ref[...] * keep
        avail = (s % nt) * ts + avail0
    else:
        zp_ref[0:hp_rows, :] = hp_ref[...]
        zc_ref[0:hc_rows, :] = hc_ref[...]
        avail = avail0
    zp_ref[hp_rows:, :] = up_ref[...]
    zc_ref[hc_rows:, :] = uc_ref[...]
    gwidth = width // len(POOL_WINDOWS)
    t_idx = lax.broadcasted_iota(jnp.int32, (ts, 1), 0) + avail + 1
    for gi, w in enumerate(POOL_WINDOWS):
        sl = slice(gi * gwidth, (gi + 1) * gwidth)
        cur = zp_ref[hp_rows:hp_rows + ts, sl]
        acc = cur
        for jj in range(1, w):
            acc = acc + zp_ref[hp_rows - jj:hp_rows - jj + ts, sl]
        cnt = jnp.minimum(t_idx, w).astype(F32)
        yg = acc / cnt - cur
        og = jnp.dot(yg.astype(BF), pw_ref[gi], preferred_element_type=F32) * ps_ref[:, sl]
        o_ref[:, sl] = og.astype(o_ref.dtype)
    base = hc_rows - (CONV_WIDTH - 1)
    for r in range(ts // rc):
        for c in range(width // LANE):
            cs = slice(c * LANE, (c + 1) * LANE)
            acc = jnp.zeros((rc, LANE), F32)
            for jj in range(CONV_WIDTH):
                lo = r * rc + base + jj
                acc = acc + zc_ref[lo:lo + rc, cs] * dw_ref[jj:jj + 1, cs]
            cb_ref[r * rc:(r + 1) * rc, cs] = acc + dwb_ref[:, cs]
    cv = cb_ref[...]
    mu = jnp.mean(cv, axis=-1, keepdims=True)
    xc = cv - mu
    yn = xc * lax.rsqrt(jnp.mean(xc * xc, axis=-1, keepdims=True) + NORM_EPS) * lng_ref[...] + lnb_ref[...]
    act = yn * jax.nn.sigmoid(yn)
    o_ref[:, width:] = jnp.dot(act.astype(BF), pww_ref[...], preferred_element_type=F32).astype(o_ref.dtype)


def _convmix(u_pool, halo_pool, u_conv, halo_conv, prm, *, ts, nt, n_tiles, zero_first, avail0, hp_rows, hc_rows,
             halo_blk):
    pw, ps, dw, dwb, lng, lnb, pww = prm
    width = u_pool.shape[1]
    rc = min(64, ts)
    body = functools.partial(_convmix_body, ts=ts, nt=nt, zero_first=zero_first, avail0=avail0, rc=rc)
    full = lambda a: pl.BlockSpec(a.shape, lambda s: (0,) * a.ndim)
    return pl.pallas_call(
        body,
        grid=(n_tiles,),
        in_specs=[pl.BlockSpec((ts, width), lambda s: (s, 0)),
                  pl.BlockSpec((hp_rows, width), lambda s: (halo_blk(s, hp_rows), 0)),
                  pl.BlockSpec((ts, width), lambda s: (s, 0)),
                  pl.BlockSpec((hc_rows, width), lambda s: (halo_blk(s, hc_rows), 0)),
                  full(pw), full(ps), full(dw), full(dwb), full(lng), full(lnb), full(pww)],
        out_specs=pl.BlockSpec((ts, 2 * width), lambda s: (s, 0)),
        out_shape=jax.ShapeDtypeStruct((n_tiles * ts, 2 * width), BF),
        scratch_shapes=[pltpu.VMEM((hp_rows + ts, width), F32), pltpu.VMEM((hc_rows + ts, width), F32),
                        pltpu.VMEM((ts, width), F32)],
        compiler_params=_cparams(("parallel",)),
        name="conv_mix",
    )(u_pool, halo_pool, u_conv, halo_conv, pw, ps, dw, dwb, lng, lnb, pww)


def _router_body(x_ref, g_ref, rw_ref, r_ref, xn_ref):
    x = x_ref[...]
    xn = x * lax.rsqrt(jnp.mean(x * x, axis=-1, keepdims=True) + NORM_EPS) * g_ref[...]
    xn_ref[...] = xn.astype(xn_ref.dtype)
    logits = _dot3(xn, rw_ref[...])
    lane = lax.broadcasted_iota(jnp.int32, logits.shape, 1)
    l1 = jnp.where(lane < N_EXPERTS, logits, NEG_INF)
    m1 = jnp.max(l1, axis=-1, keepdims=True)
    i1 = jnp.min(jnp.where(l1 == m1, lane, LANE), axis=-1, keepdims=True)
    l2 = jnp.where(lane == i1, NEG_INF, l1)
    m2 = jnp.max(l2, axis=-1, keepdims=True)
    i2 = jnp.min(jnp.where(l2 == m2, lane, LANE), axis=-1, keepdims=True)
    e = jnp.exp(m2 - m1)
    w1 = 1.0 / (1.0 + e)
    w2 = e / (1.0 + e)
    r_ref[...] = jnp.where(lane == 0, i1.astype(F32),
                           jnp.where(lane == 1, i2.astype(F32), jnp.where(lane == 2, w1, jnp.where(lane == 3, w2, 0.0))))


def _router(x, g, rw, tm=256):
    m, d = x.shape
    rw_pad = jnp.pad(rw, ((0, 0), (0, LANE - rw.shape[1])))
    return pl.pallas_call(
        _router_body,
        grid=(pl.cdiv(m, tm),),
        in_specs=[pl.BlockSpec((tm, d), lambda i: (i, 0)), pl.BlockSpec((1, d), lambda i: (0, 0)),
                  pl.BlockSpec((d, LANE), lambda i: (0, 0))],
        out_specs=[pl.BlockSpec((tm, LANE), lambda i: (i, 0)), pl.BlockSpec((tm, d), lambda i: (i, 0))],
        out_shape=[jax.ShapeDtypeStruct((m, LANE), F32), jax.ShapeDtypeStruct((m, d), BF)],
        compiler_params=_cparams(("parallel",)),
        name="moe_router",
    )(x, g.reshape(1, d), rw_pad)


def _row_copy(src_hbm, row, dst, drow, sem):
    return pltpu.make_async_copy(src_hbm.at[pl.ds(row, 1), :], dst.at[pl.ds(drow, 1), :], sem)


def _gather_rows_body(c0_ref, nc_ref, src_ref, xn_hbm, o_ref, buf_ref, acc_ref, sem, *, tc):
    i = pl.program_id(0)
    c0 = c0_ref[i]
    n = nc_ref[i]

    def chunk_copy(c, slot):
        return pltpu.make_async_copy(xn_hbm.at[pl.ds(c * tc, tc), :], buf_ref.at[slot], sem.at[slot])

    acc_ref[...] = jnp.zeros(acc_ref.shape, F32)

    @pl.when(n > 0)
    def _():
        chunk_copy(c0, 0).start()

    src = src_ref[...]

    def body(k, carry):
        slot = k % 2
        chunk_copy(c0 + k, slot).wait()

        @pl.when(k + 1 < n)
        def _():
            chunk_copy(c0 + k + 1, 1 - slot).start()

        tok = (c0 + k) * tc + lax.broadcasted_iota(jnp.int32, (1, tc), 1)
        onehot = jnp.where(src == tok, 1.0, 0.0).astype(buf_ref.dtype)
        acc_ref[...] += jnp.dot(onehot, buf_ref[slot], preferred_element_type=F32)
        return carry

    lax.fori_loop(0, n, body, 0)
    o_ref[...] = acc_ref[...].astype(o_ref.dtype)


def _gather_rows(xn, src, c0, nc, tm, tc):
    n_tok, d = xn.shape
    r_tot = src.shape[0]
    body = functools.partial(_gather_rows_body, tc=tc)
    return pl.pallas_call(
        body,
        grid_spec=pltpu.PrefetchScalarGridSpec(
            num_scalar_prefetch=2,
            grid=(r_tot // tm,),
            in_specs=[pl.BlockSpec((tm, 1), lambda i, a, b: (i, 0)), pl.BlockSpec(memory_space=pl.ANY)],
            out_specs=pl.BlockSpec((tm, d), lambda i, a, b: (i, 0)),
            scratch_shapes=[pltpu.VMEM((2, tc, d), xn.dtype), pltpu.VMEM((tm, d), F32), pltpu.SemaphoreType.DMA((2,))],
        ),
        out_shape=jax.ShapeDtypeStruct((r_tot, d), xn.dtype),
        compiler_params=_cparams(("arbitrary",)),
        name="moe_gather",
    )(c0, nc, src.reshape(r_tot, 1), xn)


def _combine_body(p1_ref, p2_ref, eo_hbm, x_ref, r_ref, o_ref, b1_ref, b2_ref, sem, *, tm, n_tok):
    base = pl.program_id(0) * tm
    n = jnp.minimum(tm, n_tok - base)

    def issue(r, c):
        _row_copy(eo_hbm, p1_ref[base + r], b1_ref, r, sem).start()
        _row_copy(eo_hbm, p2_ref[base + r], b2_ref, r, sem).start()
        return c

    lax.fori_loop(0, n, issue, 0)

    def wait(r, c):
        _row_copy(eo_hbm, 0, b1_ref, r, sem).wait()
        _row_copy(eo_hbm, 0, b2_ref, r, sem).wait()
        return c

    lax.fori_loop(0, n, wait, 0)
    rr = r_ref[...]
    o_ref[...] = x_ref[...] + rr[:, 2:3] * b1_ref[...] + rr[:, 3:4] * b2_ref[...]


def _combine(x, eo, routing, p1, p2, tm_max=512):
    m, d = x.shape
    tm = max(t for t in range(SUBLANE, tm_max + 1, SUBLANE) if m % t == 0)
    body = functools.partial(_combine_body, tm=tm, n_tok=m)
    return pl.pallas_call(
        body,
        grid_spec=pltpu.PrefetchScalarGridSpec(
            num_scalar_prefetch=2,
            grid=(pl.cdiv(m, tm),),
            in_specs=[pl.BlockSpec(memory_space=pl.ANY), pl.BlockSpec((tm, d), lambda i, a, b: (i, 0)),
                      pl.BlockSpec((tm, LANE), lambda i, a, b: (i, 0))],
            out_specs=pl.BlockSpec((tm, d), lambda i, a, b: (i, 0)),
            scratch_shapes=[pltpu.VMEM((tm, d), F32), pltpu.VMEM((tm, d), F32), pltpu.SemaphoreType.DMA(())],
        ),
        out_shape=jax.ShapeDtypeStruct((m, d), F32),
        compiler_params=_cparams(("arbitrary",)),
        name="moe_combine",
    )(p1, p2, eo, x, routing)


def _moe_plan(e_idx, tm, tc):
    n = e_idx.shape[0]
    n_asg = n * TOP_K
    flat_e = e_idx.reshape(-1)
    onehot = (flat_e[:, None] == jnp.arange(N_EXPERTS, dtype=jnp.int32)[None, :]).astype(jnp.int32)
    cnt = jnp.sum(onehot, axis=0)
    rank = jnp.take_along_axis(jnp.cumsum(onehot, axis=0) - onehot, flat_e[:, None], axis=1)[:, 0]
    cnt_p = ((cnt + tm - 1) // tm) * tm
    ends = jnp.cumsum(cnt_p)
    off = ends - cnt_p
    pos = off[flat_e] + rank
    n_tiles = (n_asg + N_EXPERTS * (tm - 1) + tm - 1) // tm
    r_tot = n_tiles * tm
    src = jnp.full((r_tot,), -1, jnp.int32).at[pos].set(jnp.arange(n_asg, dtype=jnp.int32) // TOP_K)
    src_t = src.reshape(n_tiles, tm)
    lo = jnp.min(jnp.where(src_t >= 0, src_t, n), axis=1)
    hi = jnp.max(src_t, axis=1)
    c0 = jnp.where(hi >= 0, lo // tc, 0).astype(jnp.int32)
    nc = jnp.where(hi >= 0, hi // tc + 1 - lo // tc, 0).astype(jnp.int32)
    tile_start = jnp.arange(n_tiles, dtype=jnp.int32) * tm
    tval = (tile_start < ends[-1]).astype(jnp.int32)
    texp = jnp.minimum(jnp.searchsorted(ends, tile_start, side="right"), N_EXPERTS - 1).astype(jnp.int32)
    last = jnp.max(jnp.where(tval > 0, texp, 0))
    texp = jnp.where(tval > 0, texp, last)
    pos2 = pos.reshape(n, TOP_K)
    return src, c0, nc, pos2[:, 0], pos2[:, 1], texp, tval


def _moe(x, g, rw, wg, wu, wd, tm=256):
    n = x.shape[0]
    tc = max(t for t in range(2 * SUBLANE, 513, 2 * SUBLANE) if n % t == 0)
    routing, xn = _router(x, g, rw)
    e_idx = routing[:, :TOP_K].astype(jnp.int32)
    src, c0, nc, p1, p2, texp, tval = _moe_plan(e_idx, tm, tc)
    xs = _gather_rows(xn, src, c0, nc, tm, tc)
    hid = _gmm(xs, [(wg, 0), (wu, 0)], wg.shape[2], epi="swiglu", texp=texp, tval=tval, out_dtype=BF, tm=tm, tn=1024,
               name="moe_up")
    eo = _gmm(hid, [(wd, 0)], wd.shape[2], texp=texp, tval=tval, tm=tm, name="moe_down")
    return _combine(x, eo, routing, p1, p2)


def _attn_layer(x, n_prompt, nb_p, t_p, nb_s, li, prm, caches, page_table):
    (attn_norm, w_in, qk_g, cmp_pe, cmp_w1, cmp_w2, moba_g, w_out, ffn_norm, w_gate, w_up, w_down) = prm
    cache_cmp, cache_sel, state_win, cache_moba = caches
    d = x.shape[1]
    past_len = page_table.shape[1] * cache_moba.shape[2]

    n_in = w_in.shape[1]
    g_lo, g_hi = CH_QB * LANE, CH_QB * LANE + 3 * NSA_HEADS
    w_re = jnp.concatenate([w_in[:, :g_lo], w_in[:, g_hi:], w_in[:, g_lo:g_hi],
                            jnp.zeros((d, ATTN_CHUNKS * LANE - n_in), F32)], axis=1)[None]
    ones = jnp.ones((HEAD_DIM,), F32)
    zeros = jnp.zeros((HEAD_DIM,), F32)
    chunk_gain = ([qk_g[0]] * 8 + [ones] * 4 + [qk_g[2]] * 2 + [ones] * 2 + [qk_g[3]] * 2 + [ones] * 2
                  + [moba_g[0]] * 8 + [moba_g[1]] * 8 + [ones] * 9)
    chunk_flag = ([ones] * 8 + [zeros] * 4 + [ones] * 2 + [zeros] * 2 + [ones] * 2 + [zeros] * 2
                  + [ones] * 16 + [zeros] * 9)
    gain = jnp.concatenate(chunk_gain)[None]
    flag = jnp.concatenate(chunk_flag)[None]
    xn = _rmsnorm(x, attn_norm)
    y, y16 = _gmm(xn, [(w_re, 0)], ATTN_CHUNKS * LANE, epi="headnorm", gain=gain, flag=flag, out_dtype=(F32, BF),
                  tm=512, tn=9 * LANE, name="attn_in")

    r = NSA_CMP_LEN // NSA_CMP_STRIDE
    wcat = cmp_w1.reshape(2, r, NSA_CMP_STRIDE, HEAD_DIM, NSA_CMP_HIDDEN).transpose(0, 2, 3, 1, 4)
    wcat = wcat.reshape(2, NSA_CMP_STRIDE, HEAD_DIM, r * NSA_CMP_HIDDEN).astype(BF)
    pe_rows = cmp_pe.reshape(2, r, NSA_CMP_STRIDE, HEAD_DIM).transpose(0, 2, 1, 3)
    pe_rows = jnp.pad(pe_rows, ((0, 0), (0, 0), (0, SUBLANE - r), (0, 0))).astype(BF)
    w2 = cmp_w2.astype(BF)
    gk = qk_g[1][None]

    kvc_p = _compress_c2(_compress_c1_prompt(y, wcat, nb_p, t_p), pe_rows, wcat, w2, gk)
    rows = n_prompt
    oc, sel_a = _cattn(y, 0, 0, kvc_p, nb_p, t_p, min(512, t_p), t_p, 0, rows)
    sel_b = _moba_gate_prompt(y, nb_p, t_p, min(1024, t_p))
    os_ = _flash(y, y16, sel_a, nb_p, t_p, q_ch=CH_QA, k_ch=CH_KS, v_ch=CH_VS, kvh=NSA_KV, nh=NSA_HPG,
                 blk=NSA_SEL_BLOCK, window=0, out_rows=rows)
    ow = _flash(y, y16, None, nb_p, t_p, q_ch=CH_QA, k_ch=CH_KW, v_ch=CH_VW, kvh=NSA_KV, nh=NSA_HPG, blk=0,
                window=NSA_WINDOW, out_rows=rows)
    ob = _flash(y, y16, sel_b, nb_p, t_p, q_ch=CH_QB, k_ch=CH_KB, v_ch=CH_VB, kvh=MOBA_HEADS, nh=1, blk=MOBA_BLOCK,
                window=0, out_rows=rows)

    n_pad = x.shape[0] - n_prompt
    ys = y[n_prompt:]
    kvc_s = _compress_c2(_compress_c1_sample(cache_cmp, li, page_table, wcat), pe_rows, wcat, w2, gk)
    q_pad = jnp.zeros((nb_s, SUBLANE, NSA_HEADS * HEAD_DIM), F32).at[:, 0].set(ys[:nb_s, :NSA_HEADS * HEAD_DIM])
    oc_s, sel_s = _cattn_sample(q_pad.reshape(nb_s * SUBLANE, -1), kvc_s, nb_s, past_len + 1, past_len)
    oc_s = oc_s.reshape(nb_s, SUBLANE, -1)[:, 0]
    n_sel = (past_len + 1) // NSA_SEL_BLOCK
    k_sel = min(NSA_SEL_COUNT - 1, n_sel)
    mask_a = sel_s[:, :, 0, :n_sel]
    idx_a = jnp.argsort(-mask_a, axis=-1, stable=True)[..., :k_sel].astype(jnp.int32)
    vld_a = (jnp.take_along_axis(mask_a, idx_a, axis=-1) > 0).astype(jnp.int32)
    page = cache_sel.shape[2]
    bpp = page // NSA_SEL_BLOCK
    pt_b = page_table[:, None, :]
    tbl_a = jnp.take_along_axis(jnp.broadcast_to(pt_b, (nb_s, NSA_KV, pt_b.shape[-1])), idx_a // bpp, axis=-1) * bpp + idx_a % bpp
    os_s = _decode_attn(ys, cache_sel, li, tbl_a.astype(jnp.int32), vld_a, q_ch=CH_QA, kn_ch=CH_KS, vn_ch=CH_VS,
                        nh=NSA_HPG, rows=NSA_SEL_BLOCK, kvh=NSA_KV, r_min=0,
                        name="dec_sel")
    win_buf = state_win.shape[2]
    tbl_w = jnp.broadcast_to(jnp.arange(nb_s, dtype=jnp.int32)[:, None, None], (nb_s, NSA_KV, 1))
    ow_s = _decode_attn(ys, state_win, li, tbl_w, jnp.ones_like(tbl_w), q_ch=CH_QA, kn_ch=CH_KW, vn_ch=CH_VW,
                        nh=NSA_HPG, rows=win_buf, kvh=NSA_KV, r_min=max(0, win_buf - (NSA_WINDOW - 1)),
                        name="dec_win")
    means = _moba_means_sample(cache_moba, li, page_table)
    qb_pad = jnp.zeros((nb_s, SUBLANE, MOBA_HEADS * HEAD_DIM), F32).at[:, 0].set(
        ys[:nb_s, CH_QB * LANE:(CH_QB + MOBA_HEADS) * LANE])
    sel_m = _moba_gate_sample(qb_pad.reshape(nb_s * SUBLANE, -1), means, past_len)
    n_full = means.shape[2]
    k_top = min(MOBA_TOPK, n_full)
    mask_b = sel_m[:, :, :n_full, 0]
    idx_b = jnp.argsort(-mask_b, axis=-1, stable=True)[..., :k_top].astype(jnp.int32)
    vld_b = (jnp.take_along_axis(mask_b, idx_b, axis=-1) > 0).astype(jnp.int32)
    ppb = MOBA_BLOCK // page
    pg_b = (idx_b[..., None] * ppb + jnp.arange(ppb, dtype=jnp.int32)).reshape(nb_s, MOBA_HEADS, k_top * ppb)
    tbl_b = jnp.take_along_axis(jnp.broadcast_to(pt_b, (nb_s, MOBA_HEADS, pt_b.shape[-1])), pg_b, axis=-1)
    vld_b = jnp.repeat(vld_b, ppb, axis=-1)
    ob_s = _decode_attn(ys, cache_moba, li, tbl_b.astype(jnp.int32), vld_b, q_ch=CH_QB, kn_ch=CH_KB, vn_ch=CH_VB,
                        nh=1, rows=page, kvh=MOBA_HEADS, r_min=0,
                        name="dec_moba")

    def rows_s(part, nh):
        return jnp.pad(part[:, :, :nh].reshape(nb_s, -1), ((0, n_pad - nb_s), (0, 0)))

    sample_parts = (jnp.pad(oc_s, ((0, n_pad - nb_s), (0, 0))), rows_s(os_s, NSA_HPG), rows_s(ow_s, NSA_HPG),
                    rows_s(ob_s, 1))
    mixed = _mix((oc, os_, ow, ob), sample_parts, y)
    x = _gmm(mixed, [(w_out[None], 0)], d, resid=x, tm=512, tn=1024, name="attn_out")
    hid = _gmm(_rmsnorm(x, ffn_norm), [(w_gate[None], 0), (w_up[None], 0)], w_gate.shape[1], epi="swiglu",
               out_dtype=BF, tm=512, tn=512, name="ffn_up")
    x = _gmm(hid, [(w_down[None], 0)], d, resid=x, tm=512, tn=512, name="ffn_down")

    def rows_of(lo_ch, n_ch):
        return y[:, lo_ch * LANE:(lo_ch + n_ch) * LANE]

    def split(a, kvh):
        ap = a[:n_prompt].reshape(nb_p, t_p, 2, kvh, HEAD_DIM)
        as_ = a[n_prompt:n_prompt + nb_s].reshape(nb_s, 1, 2, kvh, HEAD_DIM)
        return ap, as_

    cmp_p, cmp_s = split(rows_of(CH_KC, 4), NSA_KV)
    sel_p, sel_s_rows = split(rows_of(CH_KS, 4), NSA_KV)
    win_p, win_s = split(rows_of(CH_KW, 4), NSA_KV)
    moba_p, moba_s = split(rows_of(CH_KB, 16), MOBA_HEADS)
    new_win_p = win_p[:, t_p - min(NSA_WINDOW, t_p):]
    win_all = jnp.concatenate([state_win[li], win_s], axis=1)
    keep = min(NSA_WINDOW, past_len + 1)
    new_win_s = win_all[:, win_all.shape[1] - keep:]
    return x, (cmp_p, cmp_s, sel_p, sel_s_rows, new_win_p, new_win_s, moba_p, moba_s)


def _conv_layer(x, n_prompt, nb_p, t_p, nb_s, prm, states):
    (conv_norm, w_in, pool_w, pool_scale, dw, dw_b, ln_g, ln_b, pw, w_out, moe_norm, router_w, wg, wu, wd) = prm
    state_pool, state_conv = states
    d = x.shape[1]
    width = pool_w.shape[0] * pool_w.shape[1]
    xn = _rmsnorm(x, conv_norm)
    w3 = w_in[None]
    u_pool = _gmm(xn, [(w3, 0)], width, tm=512, tn=width, name="conv_in_pool")
    u_conv = _gmm(xn, [(w3, width), (w3, 2 * width)], width, epi="glu", tm=512, tn=512, name="conv_in_glu")
    prm_mix = (pool_w.astype(BF), pool_scale[None], jnp.pad(dw, ((0, 1), (0, 0))), dw_b[None], ln_g[None], ln_b[None],
               pw.astype(BF))
    pool_buf = max(POOL_WINDOWS) - 1
    conv_buf = CONV_WIDTH - 1
    hp_rows, hc_rows = 16, 32
    ts = min(256, t_p)
    nt = t_p // ts
    mixed_p = _convmix(u_pool, u_pool, u_conv, u_conv, prm_mix, ts=ts, nt=nt, n_tiles=nb_p * nt, zero_first=True,
                       avail0=0, hp_rows=hp_rows, hc_rows=hc_rows,
                       halo_blk=lambda s, hr: jnp.maximum(s * (ts // hr) - 1, 0))
    rows = x.shape[0]
    us_pool = jnp.zeros((nb_s, SUBLANE, width), F32).at[:, 0].set(u_pool[n_prompt:n_prompt + nb_s])
    us_conv = jnp.zeros((nb_s, SUBLANE, width), F32).at[:, 0].set(u_conv[n_prompt:n_prompt + nb_s])
    hs_pool = jnp.pad(state_pool, ((0, 0), (hp_rows - pool_buf, 0), (0, 0))).reshape(nb_s * hp_rows, width)
    hs_conv = jnp.pad(state_conv, ((0, 0), (hc_rows - conv_buf, 0), (0, 0))).reshape(nb_s * hc_rows, width)
    mixed_s = _convmix(us_pool.reshape(nb_s * SUBLANE, width), hs_pool, us_conv.reshape(nb_s * SUBLANE, width), hs_conv,
                       prm_mix, ts=SUBLANE, nt=1, n_tiles=nb_s, zero_first=False, avail0=pool_buf, hp_rows=hp_rows,
                       hc_rows=hc_rows, halo_blk=lambda s, hr: s)
    mixed_s = mixed_s.reshape(nb_s, SUBLANE, 2 * width)[:, 0]
    mixed = jnp.concatenate([mixed_p, jnp.pad(mixed_s, ((0, rows - n_prompt - nb_s), (0, 0)))], axis=0)
    x = _gmm(mixed, [(w_out[None], 0)], d, resid=x, tm=512, tn=1024, name="conv_out")
    x = _moe(x, moe_norm, router_w, wg, wu, wd)

    up = u_pool[:n_prompt].reshape(nb_p, t_p, width)
    uc = u_conv[:n_prompt].reshape(nb_p, t_p, width)
    new_pool_p = up[:, t_p - pool_buf:]
    assert t_p >= conv_buf and t_p >= pool_buf
    new_conv_p = uc[:, t_p - conv_buf:]
    new_pool_s = jnp.concatenate([state_pool, u_pool[n_prompt:n_prompt + nb_s][:, None]], axis=1)[:, 1:]
    new_conv_s = jnp.concatenate([state_conv, u_conv[n_prompt:n_prompt + nb_s][:, None]], axis=1)[:, 1:]
    return x, (new_pool_p, new_pool_s, new_conv_p, new_conv_s)


def kernel(x_prompt, x_sample, cache_nsa_cmp, cache_nsa_sel, state_nsa_win, cache_moba, state_pool, state_conv, page_table, attn_norm, w_attn_in, nsa_qk_norm, nsa_cmp_pe, nsa_cmp_w1, nsa_cmp_w2, moba_qk_norm, w_attn_out, ffn_norm, ffn_w_gate, ffn_w_up, ffn_w_down, conv_norm, w_conv_in, pool_w, pool_scale, conv_dw, conv_dw_b, conv_ln_g, conv_ln_b, conv_pw, w_conv_out, moe_norm, router_w, moe_w_gate, moe_w_up, moe_w_down):
    nb_p, t_p, d = x_prompt.shape
    nb_s = x_sample.shape[0]
    n_prompt = nb_p * t_p
    n_pad = 2 * SUBLANE
    assert x_sample.shape[1] == 1 and nb_s <= n_pad
    x = jnp.concatenate([x_prompt.reshape(n_prompt, d), x_sample.reshape(nb_s, d),
                         jnp.zeros((n_pad - nb_s, d), F32)], axis=0)
    li = 0
    prm_a = (attn_norm[li], w_attn_in[li], nsa_qk_norm[li], nsa_cmp_pe[li], nsa_cmp_w1[li], nsa_cmp_w2[li],
             moba_qk_norm[li], w_attn_out[li], ffn_norm[li], ffn_w_gate[li], ffn_w_up[li], ffn_w_down[li])
    x, attn_new = _attn_layer(x, n_prompt, nb_p, t_p, nb_s, li, prm_a,
                              (cache_nsa_cmp, cache_nsa_sel, state_nsa_win, cache_moba), page_table)
    prm_c = (conv_norm[li], w_conv_in[li], pool_w[li], pool_scale[li], conv_dw[li], conv_dw_b[li], conv_ln_g[li],
             conv_ln_b[li], conv_pw[li], w_conv_out[li], moe_norm[li], router_w[li], moe_w_gate[li], moe_w_up[li],
             moe_w_down[li])
    x, conv_new = _conv_layer(x, n_prompt, nb_p, t_p, nb_s, prm_c, (state_pool[li], state_conv[li]))
    cmp_p, cmp_s, sel_p, sel_s, win_p, win_s, moba_p, moba_s = attn_new
    pool_p, pool_s, conv_p, conv_s = conv_new
    y_p = x[:n_prompt].reshape(nb_p, t_p, d)
    y_s = x[n_prompt:n_prompt + nb_s].reshape(nb_s, 1, d)
    st = lambda a: a[None]
    return (y_p, y_s, st(cmp_p), st(cmp_s), st(sel_p), st(sel_s), st(win_p), st(win_s), st(moba_p), st(moba_s),
            st(pool_p), st(pool_s), st(conv_p), st(conv_s))
```

```python
import functools
import math

import jax
import jax.numpy as jnp
from jax import lax
from jax.experimental import pallas as pl
from jax.experimental.pallas import tpu as pltpu

F32 = jnp.float32
BF = jnp.bfloat16
NEG_INF = float("-inf")

HEAD_DIM = 128
NORM_EPS = 1e-6
NSA_HEADS = 8
NSA_KV = 2
NSA_HPG = NSA_HEADS // NSA_KV
NSA_CMP_LEN = 32
NSA_CMP_STRIDE = 16
NSA_CMP_HIDDEN = 2 * HEAD_DIM
NSA_SEL_BLOCK = 64
NSA_SEL_COUNT = 16
NSA_WINDOW = 512
NSA_SEL_FORCE = 1.0e4
MOBA_HEADS = 8
MOBA_BLOCK = 256
MOBA_TOPK = 3
POOL_WINDOWS = (2, 4, 8, 16)
CONV_WIDTH = 31
N_EXPERTS = 8
TOP_K = 2

LANE = 128
SUBLANE = 8
VMEM_LIMIT = 58 * 1024 * 1024
STREAM_TM = 512

CH_QA, CH_KC, CH_VC, CH_KS, CH_VS, CH_KW, CH_VW, CH_QB, CH_KB, CH_VB, CH_GATE = 0, 8, 10, 12, 14, 16, 18, 20, 28, 36, 44
ATTN_CHUNKS = 45

NT_DIMS = (((1,), (1,)), ((), ()))


def _cparams(sem):
    return pltpu.CompilerParams(dimension_semantics=sem, vmem_limit_bytes=VMEM_LIMIT)


def _split_bf16(a):
    hi = a.astype(BF)
    lo = (a - hi.astype(F32)).astype(BF)
    return hi, lo


def _dot3(a, b, dims=(((1,), (0,)), ((), ()))):
    ah, al = _split_bf16(a)
    bh, bl = _split_bf16(b)
    d = lambda x, y: lax.dot_general(x, y, dims, preferred_element_type=F32)
    return d(ah, bh) + d(ah, bl) + d(al, bh)


def _masked_softmax(s, mask, axis):
    s = jnp.where(mask, s, NEG_INF)
    m = jnp.max(s, axis=axis, keepdims=True)
    m = jnp.where(m == NEG_INF, 0.0, m)
    p = jnp.exp(s - m)
    d = jnp.sum(p, axis=axis, keepdims=True)
    return p / jnp.where(d > 0, d, 1.0)


def _split_stream(a, tm):
    if not isinstance(a, tuple):
        return [a], [lambda i: i], None, a.shape[0]
    main, tail, m = a
    n_full = main.shape[0] // tm
    assert n_full * tm == main.shape[0] and tail.shape[0] == tm and n_full * tm < m <= (n_full + 1) * tm
    return [main, tail], [lambda i: jnp.minimum(i, n_full - 1), lambda i: 0], n_full, m


def _pick_tile(refs, i, n_full):
    return refs[0][...] if n_full is None else jnp.where(i < n_full, refs[0][...], refs[1][...])


def _rmsnorm_body(*refs, n_full):
    x = _pick_tile(refs[:-2], pl.program_id(0), n_full)
    g_ref, o_ref = refs[-2:]
    ms = jnp.mean(x * x, axis=-1, keepdims=True)
    o_ref[...] = (x * lax.rsqrt(ms + NORM_EPS) * g_ref[...]).astype(o_ref.dtype)


def _rmsnorm(x, g, tm=256):
    arrs, maps, n_full, m = _split_stream(x, tm)
    d = arrs[0].shape[1]
    return pl.pallas_call(
        functools.partial(_rmsnorm_body, n_full=n_full),
        grid=(pl.cdiv(m, tm),),
        in_specs=[pl.BlockSpec((tm, d), lambda i, f=f: (f(i), 0)) for f in maps] + [pl.BlockSpec((1, d), lambda i: (0, 0))],
        out_specs=pl.BlockSpec((tm, d), lambda i: (i, 0)),
        out_shape=jax.ShapeDtypeStruct((m, d), BF),
        compiler_params=_cparams(("parallel",)),
        name="rmsnorm",
    )(*arrs, g.reshape(1, d))


def _gmm_body(texp_ref, tval_ref, *refs, n_x, x_full, n_w, cast_w, epi, n_r, r_full, tn, n_o):
    x_refs = refs[:n_x]
    w_refs = refs[n_x:n_x + n_w]
    pos = n_x + n_w
    if epi == "headnorm":
        gain_ref, flag_ref = refs[pos], refs[pos + 1]
        pos += 2
    resid_refs = refs[pos:pos + n_r]
    pos += n_r
    o_refs = refs[pos:pos + n_o]
    wb_refs = refs[pos + n_o:pos + n_o + n_w] if cast_w else w_refs

    i = pl.program_id(1)
    if cast_w:
        changed = jnp.logical_or(i == 0, texp_ref[i] != texp_ref[jnp.maximum(i - 1, 0)])

        @pl.when(changed)
        def _():
            for k in range(n_w):
                wb_refs[k][...] = w_refs[k][...].astype(BF)

    @pl.when(tval_ref[i] == 0)
    def _():
        for o in o_refs:
            o[...] = jnp.zeros(o.shape, o.dtype)

    @pl.when(tval_ref[i] > 0)
    def _():
        x = _pick_tile(x_refs, i, x_full)
        a = jnp.dot(x, wb_refs[0][...], preferred_element_type=F32)
        if epi == "swiglu":
            b = jnp.dot(x, wb_refs[1][...], preferred_element_type=F32)
            y = a * jax.nn.sigmoid(a) * b
        elif epi == "glu":
            b = jnp.dot(x, wb_refs[1][...], preferred_element_type=F32)
            y = a * jax.nn.sigmoid(b)
        else:
            y = a
        if n_r:
            y = y + _pick_tile(resid_refs, i, r_full)
        if epi == "headnorm":
            for c in range(tn // LANE):
                sl = slice(c * LANE, (c + 1) * LANE)
                yc = y[:, sl]
                r = lax.rsqrt(jnp.mean(yc * yc, axis=-1, keepdims=True) + NORM_EPS)
                f = flag_ref[:, sl]
                yn = yc * (f * r + (1.0 - f)) * gain_ref[:, sl]
                for o in o_refs:
                    o[:, sl] = yn.astype(o.dtype)
        else:
            for o in o_refs:
                o[...] = y.astype(o.dtype)


def _gmm(x, ws, n_out, *, epi="none", texp=None, tval=None, resid=None, gain=None, flag=None,
         out_dtype=F32, tm=256, tn=512, name="gmm"):
    x_arrs, x_maps, x_full, m = _split_stream(x, tm)
    kdim = x_arrs[0].shape[1]
    tn = min(tn, n_out)
    n_m = pl.cdiv(m, tm)
    n_n = n_out // tn
    assert n_n * tn == n_out
    if texp is None:
        texp = jnp.zeros((n_m,), jnp.int32)
        tval = jnp.ones((n_m,), jnp.int32)
    n_w = len(ws)
    cast_w = ws[0][0].dtype != BF
    in_specs = [pl.BlockSpec((tm, kdim), lambda j, i, te, tv, f=f: (f(i), 0)) for f in x_maps]
    args = list(x_arrs)
    for w, off in ws:
        assert off % tn == 0 and w.shape[1] == kdim and (w.dtype != BF) == cast_w
        ob = off // tn
        in_specs.append(pl.BlockSpec((None, kdim, tn), lambda j, i, te, tv, ob=ob: (te[i], 0, j + ob)))
        args.append(w)
    if epi == "headnorm":
        in_specs += [pl.BlockSpec((1, tn), lambda j, i, te, tv: (0, j))] * 2
        args += [gain, flag]
    n_r, r_full = 0, None
    if resid is not None:
        r_arrs, r_maps, r_full, r_m = _split_stream(resid, tm)
        assert r_m == m
        n_r = len(r_arrs)
        in_specs += [pl.BlockSpec((tm, tn), lambda j, i, te, tv, f=f: (f(i), j)) for f in r_maps]
        args += r_arrs
    dtypes = out_dtype if isinstance(out_dtype, tuple) else (out_dtype,)
    body = functools.partial(_gmm_body, n_x=len(x_arrs), x_full=x_full, n_w=n_w, cast_w=cast_w, epi=epi, n_r=n_r,
                             r_full=r_full, tn=tn, n_o=len(dtypes))
    outs = pl.pallas_call(
        body,
        grid_spec=pltpu.PrefetchScalarGridSpec(
            num_scalar_prefetch=2,
            grid=(n_n, n_m),
            in_specs=in_specs,
            out_specs=[pl.BlockSpec((tm, tn), lambda j, i, te, tv: (i, j)) for _ in dtypes],
            scratch_shapes=[pltpu.VMEM((kdim, tn), BF) for _ in range(n_w if cast_w else 0)],
        ),
        out_shape=[jax.ShapeDtypeStruct((m, n_out), dt) for dt in dtypes],
        compiler_params=_cparams(("arbitrary", "arbitrary")),
        name=name,
    )(texp, tval, *args)
    return outs if isinstance(out_dtype, tuple) else outs[0]


def _c1_body(x_ref, w_ref, o_ref):
    nsb = x_ref.shape[0] // NSA_CMP_STRIDE
    acc = jnp.zeros((nsb, 2 * NSA_CMP_HIDDEN), F32)
    for s in range(NSA_CMP_STRIDE):
        xs = x_ref[pl.ds(s, nsb, stride=NSA_CMP_STRIDE), :]
        acc = acc + jnp.dot(xs.astype(BF), w_ref[s], preferred_element_type=F32)
    o_ref[...] = acc


def _compress_c1_prompt(y, wcat, nb, t):
    nsub = t // NSA_CMP_STRIDE
    return pl.pallas_call(
        _c1_body,
        grid=(nb, 2, NSA_KV),
        in_specs=[pl.BlockSpec((t, LANE), lambda b, kv, g: (b, CH_KC + 2 * kv + g)),
                  pl.BlockSpec((None, NSA_CMP_STRIDE, HEAD_DIM, 2 * NSA_CMP_HIDDEN), lambda b, kv, g: (kv, 0, 0, 0))],
        out_specs=pl.BlockSpec((None, None, None, nsub, 2 * NSA_CMP_HIDDEN), lambda b, kv, g: (b, kv, g, 0, 0)),
        out_shape=jax.ShapeDtypeStruct((nb, 2, NSA_KV, nsub, 2 * NSA_CMP_HIDDEN), F32),
        compiler_params=_cparams(("parallel", "parallel", "parallel")),
        name="cmp_c1_prompt",
    )(y, wcat)


C1_PAGES = 16


def _c1_sample_body(pt_ref, *refs, n_in, page):
    x_refs = refs[:n_in]
    w_ref, o_ref, scr = refs[n_in:n_in + 3]
    nsb = page // NSA_CMP_STRIDE
    rows = n_in * nsb
    for kv in range(2):
        acc = jnp.zeros((rows * NSA_KV, 2 * NSA_CMP_HIDDEN), F32)

        def rows_at(s):
            parts = [xr[pl.ds(s, nsb, stride=NSA_CMP_STRIDE), kv, :, :].reshape(nsb * NSA_KV, HEAD_DIM) for xr in x_refs]
            return (parts[0] if n_in == 1 else jnp.concatenate(parts, axis=0)).astype(BF)

        for s in range(0, NSA_CMP_STRIDE, 2):
            xs = jnp.concatenate([rows_at(s), rows_at(s + 1)], axis=1)
            w = w_ref[kv, s:s + 2].reshape(2 * HEAD_DIM, 2 * NSA_CMP_HIDDEN)
            acc = acc + jnp.dot(xs, w, preferred_element_type=F32)
        for c in range(scr.shape[0]):
            scr[c] = acc[:, c * LANE:(c + 1) * LANE]
        for g in range(NSA_KV):
            for c in range(scr.shape[0]):
                o_ref[kv, g, :, c * LANE:(c + 1) * LANE] = scr[c, pl.ds(g, rows, stride=NSA_KV), :]


def _compress_c1_sample(cache, li, page_table, wcat):
    nb, n_pages = page_table.shape
    page = cache.shape[2]
    npp = math.gcd(C1_PAGES, n_pages)
    nsb = page // NSA_CMP_STRIDE
    nsub = n_pages * nsb
    body = functools.partial(_c1_sample_body, n_in=npp, page=page)
    in_specs = [
        pl.BlockSpec((None, None, page, 2, NSA_KV, HEAD_DIM), lambda b, j, pt, i=i: (li, pt[b, j * npp + i], 0, 0, 0, 0))
        for i in range(npp)
    ]
    in_specs.append(pl.BlockSpec(wcat.shape, lambda b, j, pt: (0, 0, 0, 0)))
    return pl.pallas_call(
        body,
        grid_spec=pltpu.PrefetchScalarGridSpec(
            num_scalar_prefetch=1,
            grid=(nb, n_pages // npp),
            in_specs=in_specs,
            out_specs=pl.BlockSpec((None, 2, NSA_KV, npp * nsb, 2 * NSA_CMP_HIDDEN), lambda b, j, pt: (b, 0, 0, j, 0)),
            scratch_shapes=[pltpu.VMEM((2 * NSA_CMP_HIDDEN // LANE, npp * nsb * NSA_KV, LANE), F32)],
        ),
        out_shape=jax.ShapeDtypeStruct((nb, 2, NSA_KV, nsub, 2 * NSA_CMP_HIDDEN), F32),
        compiler_params=_cparams(("parallel", "parallel")),
        name="cmp_c1_sample",
    )(page_table, *([cache] * npp), wcat)


def _c2_body(p_ref, pe_ref, wcat_ref, w2_ref, g_ref, o_ref, *, nsub, n_c):
    kv = pl.program_id(1)
    pep = jnp.zeros((SUBLANE, 2 * NSA_CMP_HIDDEN), F32)
    for s in range(NSA_CMP_STRIDE):
        pep = pep + jnp.dot(pe_ref[s], wcat_ref[s], preferred_element_type=F32)
    bias = pep[0:1, :NSA_CMP_HIDDEN] + pep[1:2, NSA_CMP_HIDDEN:]
    nxt = pltpu.roll(p_ref[:, NSA_CMP_HIDDEN:], nsub - 1, 0)
    h = p_ref[:, :NSA_CMP_HIDDEN] + nxt + bias
    h = h * jax.nn.sigmoid(h)
    o = jnp.dot(h.astype(BF), w2_ref[...], preferred_element_type=F32)
    r = lax.rsqrt(jnp.mean(o * o, axis=-1, keepdims=True) + NORM_EPS)
    o = jnp.where(kv == 0, o * r * g_ref[...], o)
    row = lax.broadcasted_iota(jnp.int32, (nsub, 1), 0)
    o_ref[...] = jnp.where(row < n_c, o, 0.0)


def _compress_c2(p, pe_rows, wcat, w2, gk):
    nb, _, _, nsub, _ = p.shape
    n_c = nsub - NSA_CMP_LEN // NSA_CMP_STRIDE + 1
    body = functools.partial(_c2_body, nsub=nsub, n_c=n_c)
    return pl.pallas_call(
        body,
        grid=(nb, 2, NSA_KV),
        in_specs=[pl.BlockSpec((None, None, None, nsub, 2 * NSA_CMP_HIDDEN), lambda b, kv, g: (b, kv, g, 0, 0)),
                  pl.BlockSpec((None, NSA_CMP_STRIDE, SUBLANE, HEAD_DIM), lambda b, kv, g: (kv, 0, 0, 0)),
                  pl.BlockSpec((None, NSA_CMP_STRIDE, HEAD_DIM, 2 * NSA_CMP_HIDDEN), lambda b, kv, g: (kv, 0, 0, 0)),
                  pl.BlockSpec((None, NSA_CMP_HIDDEN, HEAD_DIM), lambda b, kv, g: (kv, 0, 0)),
                  pl.BlockSpec((1, HEAD_DIM), lambda b, kv, g: (0, 0))],
        out_specs=pl.BlockSpec((None, None, None, nsub, HEAD_DIM), lambda b, kv, g: (b, kv, g, 0, 0)),
        out_shape=jax.ShapeDtypeStruct((nb, 2, NSA_KV, nsub, HEAD_DIM), F32),
        compiler_params=_cparams(("parallel", "parallel", "parallel")),
        name="cmp_c2",
    )(p, pe_rows, wcat, w2, gk)


def _rank_select(sc_ref, n_iter, k):
    score = sc_ref[...]
    brow = lax.broadcasted_iota(jnp.int32, score.shape, 0)

    def body(j, rank):
        r = sc_ref[pl.ds(j, 1), :]
        beats = jnp.logical_or(r > score, jnp.logical_and(r == score, j < brow))
        return rank + jnp.where(beats, 1.0, 0.0)

    rank = lax.fori_loop(0, n_iter, body, jnp.zeros(score.shape, F32))
    return rank < k


def _cattn_body(q_ref, kc_ref, vc_ref, oc_ref, sel_ref, sc_ref, *, tq, ncp, n_c, ns, nsp, pos0, ksel):
    t0 = pos0 + pl.program_id(2) * tq
    scale = HEAD_DIM ** -0.5
    kc = kc_ref[...].astype(BF)
    vc = vc_ref[...].astype(BF)
    trow = t0 + lax.broadcasted_iota(jnp.int32, (tq, 1), 0)
    ncol = lax.broadcasted_iota(jnp.int32, (1, ncp), 1)
    valid = jnp.logical_and(ncol * NSA_CMP_STRIDE + (NSA_CMP_LEN - 1) <= trow, ncol < n_c)
    tcol = t0 + lax.broadcasted_iota(jnp.int32, (1, tq), 1)
    nrow = lax.broadcasted_iota(jnp.int32, (ncp, 1), 0)
    valid_t = jnp.logical_and(nrow * NSA_CMP_STRIDE + (NSA_CMP_LEN - 1) <= tcol, nrow < n_c)
    psum_t = jnp.zeros((ncp, tq), F32)
    for z in range(NSA_HPG):
        sl = slice(z * HEAD_DIM, (z + 1) * HEAD_DIM)
        q = (q_ref[:, sl] * scale).astype(BF)
        s = lax.dot_general(q, kc, NT_DIMS, preferred_element_type=F32)
        p = _masked_softmax(s, valid, -1)
        oc_ref[:, sl] = jnp.dot(p.astype(BF), vc, preferred_element_type=F32)
        s_t = lax.dot_general(kc, q, NT_DIMS, preferred_element_type=F32)
        psum_t = psum_t + _masked_softmax(s_t, valid_t, 0)
    r = NSA_SEL_BLOCK // NSA_CMP_STRIDE
    brow = lax.broadcasted_iota(jnp.int32, (nsp, 1), 0)
    lo = r * brow - 1
    inside = jnp.logical_and(ncol >= lo, ncol <= lo + r)
    edge = jnp.logical_or(ncol == lo, ncol == lo + r)
    m_t = jnp.where(inside, jnp.where(edge, 0.5, 1.0), 0.0).astype(BF)
    hi = psum_t.astype(BF)
    mid = (psum_t - hi.astype(F32)).astype(BF)
    low = (psum_t - hi.astype(F32) - mid.astype(F32)).astype(BF)
    imp_t = (jnp.dot(m_t, hi, preferred_element_type=F32) + jnp.dot(m_t, mid, preferred_element_type=F32)
             + jnp.dot(m_t, low, preferred_element_type=F32))
    own = tcol // NSA_SEL_BLOCK
    past = jnp.logical_and(brow < own, brow < ns)
    forced = jnp.logical_or(brow == 0, brow == own - 1)
    sc_ref[...] = jnp.where(past, jnp.where(forced, NSA_SEL_FORCE, imp_t), NEG_INF)
    chosen = jnp.logical_and(past, _rank_select(sc_ref, ns, ksel))
    sel = jnp.where(jnp.logical_or(chosen, brow == own), 1.0, 0.0)
    rows = sel_ref.shape[0]
    if rows > nsp:
        sel_ref[...] = jnp.zeros(sel_ref.shape, F32)
    sel_ref[0:nsp, :] = sel


def _cattn(q_arr, q_row_blk0, q_col_blk0, kvc, nb, tlen, tq, length, pos0, out_rows):
    nq = tlen // tq
    ncp = kvc.shape[3]
    n_c = ncp - NSA_CMP_LEN // NSA_CMP_STRIDE + 1
    ns = length // NSA_SEL_BLOCK
    ksel = min(NSA_SEL_COUNT - 1, ns)
    nsp = -(-ns // SUBLANE) * SUBLANE
    sel_rows = max(nsp, LANE)
    body = functools.partial(_cattn_body, tq=tq, ncp=ncp, n_c=n_c, ns=ns, nsp=nsp, pos0=pos0, ksel=ksel)
    gw = NSA_HPG * HEAD_DIM
    return pl.pallas_call(
        body,
        grid=(nb, NSA_KV, nq),
        in_specs=[pl.BlockSpec((tq, gw), lambda b, g, qi: (q_row_blk0 + b * nq + qi, q_col_blk0 + g)),
                  pl.BlockSpec((None, None, None, ncp, HEAD_DIM), lambda b, g, qi: (b, 0, g, 0, 0)),
                  pl.BlockSpec((None, None, None, ncp, HEAD_DIM), lambda b, g, qi: (b, 1, g, 0, 0))],
        out_specs=[pl.BlockSpec((tq, gw), lambda b, g, qi: (b * nq + qi, g)),
                   pl.BlockSpec((None, None, sel_rows, tq), lambda b, g, qi: (b, g, 0, qi))],
        out_shape=[jax.ShapeDtypeStruct((out_rows, NSA_KV * gw), F32),
                   jax.ShapeDtypeStruct((nb, NSA_KV, sel_rows, tlen), F32)],
        scratch_shapes=[pltpu.VMEM((nsp, tq), F32)],
        compiler_params=_cparams(("parallel", "parallel", "parallel")),
        name="nsa_cmp_attn",
    )(q_arr, kvc, kvc)


def _cattn_row_body(q_ref, kc_ref, vc_ref, oc_ref, sel_ref, *, tq, ncp, n_c, ns, nsl, pos0, ksel):
    scale = HEAD_DIM ** -0.5
    kc = kc_ref[...].astype(BF)
    vc = vc_ref[...].astype(BF)
    trow = pos0 + lax.broadcasted_iota(jnp.int32, (tq, 1), 0)
    ncol = lax.broadcasted_iota(jnp.int32, (1, ncp), 1)
    valid = jnp.logical_and(ncol * NSA_CMP_STRIDE + (NSA_CMP_LEN - 1) <= trow, ncol < n_c)
    psum = jnp.zeros((tq, ncp), F32)
    for z in range(NSA_HPG):
        sl = slice(z * HEAD_DIM, (z + 1) * HEAD_DIM)
        q = (q_ref[:, sl] * scale).astype(BF)
        p = _masked_softmax(lax.dot_general(q, kc, NT_DIMS, preferred_element_type=F32), valid, -1)
        oc_ref[:, sl] = jnp.dot(p.astype(BF), vc, preferred_element_type=F32)
        psum = psum + p
    r = NSA_SEL_BLOCK // NSA_CMP_STRIDE
    nrow = lax.broadcasted_iota(jnp.int32, (ncp, 1), 0)
    bcol = lax.broadcasted_iota(jnp.int32, (1, nsl), 1)
    lo = r * bcol - 1
    inside = jnp.logical_and(nrow >= lo, nrow <= lo + r)
    edge = jnp.logical_or(nrow == lo, nrow == lo + r)
    m = jnp.where(inside, jnp.where(edge, 0.5, 1.0), 0.0).astype(BF)
    hi = psum.astype(BF)
    mid = (psum - hi.astype(F32)).astype(BF)
    low = (psum - hi.astype(F32) - mid.astype(F32)).astype(BF)
    imp = (jnp.dot(hi, m, preferred_element_type=F32) + jnp.dot(mid, m, preferred_element_type=F32)
           + jnp.dot(low, m, preferred_element_type=F32))
    own = trow // NSA_SEL_BLOCK
    past = jnp.logical_and(bcol < own, bcol < ns)
    forced = jnp.logical_or(bcol == 0, bcol == own - 1)
    score = jnp.where(past, jnp.where(forced, NSA_SEL_FORCE, imp), NEG_INF)
    rank = jnp.zeros((tq, nsl), F32)
    for j in range(ns):
        cj = score[:, j:j + 1]
        beats = jnp.logical_or(cj > score, jnp.logical_and(cj == score, j < bcol))
        rank = rank + jnp.where(beats, 1.0, 0.0)
    chosen = jnp.logical_and(past, rank < ksel)
    sel_ref[...] = jnp.where(jnp.logical_or(chosen, bcol == own), 1.0, 0.0)


def _cattn_sample(q_pad, kvc, nb, length, pos0):
    tq = SUBLANE
    ncp = kvc.shape[3]
    n_c = ncp - NSA_CMP_LEN // NSA_CMP_STRIDE + 1
    ns = length // NSA_SEL_BLOCK
    ksel = min(NSA_SEL_COUNT - 1, ns)
    nsl = -(-ns // LANE) * LANE
    body = functools.partial(_cattn_row_body, tq=tq, ncp=ncp, n_c=n_c, ns=ns, nsl=nsl, pos0=pos0, ksel=ksel)
    gw = NSA_HPG * HEAD_DIM
    return pl.pallas_call(
        body,
        grid=(nb, NSA_KV),
        in_specs=[pl.BlockSpec((tq, gw), lambda b, g: (b, g)),
                  pl.BlockSpec((None, None, None, ncp, HEAD_DIM), lambda b, g: (b, 0, g, 0, 0)),
                  pl.BlockSpec((None, None, None, ncp, HEAD_DIM), lambda b, g: (b, 1, g, 0, 0))],
        out_specs=[pl.BlockSpec((tq, gw), lambda b, g: (b, g)),
                   pl.BlockSpec((None, None, tq, nsl), lambda b, g: (b, g, 0, 0))],
        out_shape=[jax.ShapeDtypeStruct((nb * tq, NSA_KV * gw), F32),
                   jax.ShapeDtypeStruct((nb, NSA_KV, tq, nsl), F32)],
        compiler_params=_cparams(("parallel", "parallel")),
        name="nsa_cmp_attn_sample",
    )(q_pad, kvc, kvc)


def _gate_body(q_ref, k_ref, sel_ref, mean_ref, sc_ref, *, tq, nb, nbp, pos0, from_means):
    qi = pl.program_id(2)
    if from_means:
        means = k_ref[...]
    else:
        @pl.when(qi == 0)
        def _():
            mean_ref[...] = jnp.zeros(mean_ref.shape, F32)
            for j in range(nb):
                blk = k_ref[j * MOBA_BLOCK:(j + 1) * MOBA_BLOCK, :]
                mean_ref[j:j + 1, :] = jnp.sum(blk, axis=0, keepdims=True) * (1.0 / MOBA_BLOCK)

        means = mean_ref[...]
    g_t = _dot3(means, q_ref[...], NT_DIMS)
    tcol = pos0 + qi * tq + lax.broadcasted_iota(jnp.int32, (1, tq), 1)
    own = tcol // MOBA_BLOCK
    brow = lax.broadcasted_iota(jnp.int32, (nbp, 1), 0)
    past = jnp.logical_and(brow < own, brow < nb)
    sc_ref[...] = jnp.where(past, g_t, NEG_INF)
    chosen = jnp.logical_and(past, _rank_select(sc_ref, nb, min(MOBA_TOPK, nb)))
    sel = jnp.where(jnp.logical_or(chosen, brow == own), 1.0, 0.0)
    if sel_ref.shape[0] > nbp:
        sel_ref[...] = jnp.zeros(sel_ref.shape, F32)
    sel_ref[0:nbp, :] = sel


def _moba_gate_prompt(y, nb, t, tq):
    nq = t // tq
    n_full = t // MOBA_BLOCK
    nbp = -(-n_full // SUBLANE) * SUBLANE
    body = functools.partial(_gate_body, tq=tq, nb=n_full, nbp=nbp, pos0=0, from_means=False)
    return pl.pallas_call(
        body,
        grid=(nb, MOBA_HEADS, nq),
        in_specs=[pl.BlockSpec((tq, LANE), lambda b, h, qi: (b * nq + qi, CH_QB + h)),
                  pl.BlockSpec((t, LANE), lambda b, h, qi: (b, CH_KB + h))],
        out_specs=pl.BlockSpec((None, None, LANE, tq), lambda b, h, qi: (b, h, 0, qi)),
        out_shape=jax.ShapeDtypeStruct((nb, MOBA_HEADS, LANE, t), F32),
        scratch_shapes=[pltpu.VMEM((nbp, HEAD_DIM), F32), pltpu.VMEM((nbp, tq), F32)],
        compiler_params=_cparams(("parallel", "parallel", "arbitrary")),
        name="moba_gate_prompt",
    )(y, y)


def _moba_gate_sample(q_pad, means, pos0):
    nb, _, n_full, _ = means.shape
    tq = SUBLANE
    body = functools.partial(_gate_body, tq=tq, nb=n_full, nbp=n_full, pos0=pos0, from_means=True)
    return pl.pallas_call(
        body,
        grid=(nb, MOBA_HEADS, 1),
        in_specs=[pl.BlockSpec((tq, LANE), lambda b, h, qi: (b, h)),
                  pl.BlockSpec((None, None, n_full, HEAD_DIM), lambda b, h, qi: (b, h, 0, 0))],
        out_specs=pl.BlockSpec((None, None, max(n_full, LANE), tq), lambda b, h, qi: (b, h, 0, 0)),
        out_shape=jax.ShapeDtypeStruct((nb, MOBA_HEADS, max(n_full, LANE), tq), F32),
        scratch_shapes=[pltpu.VMEM((SUBLANE, HEAD_DIM), F32), pltpu.VMEM((n_full, tq), F32)],
        compiler_params=_cparams(("parallel", "parallel", "arbitrary")),
        name="moba_gate_sample",
    )(q_pad, means)


MEANS_BLOCKS = 4


def _means_body(pt_ref, *refs, ppb, bps):
    c_refs = refs[:bps * ppb]
    o_ref = refs[bps * ppb]
    j = pl.program_id(1)
    for q in range(bps):
        acc = jnp.sum(c_refs[q * ppb][...], axis=0)
        for i in range(1, ppb):
            acc = acc + jnp.sum(c_refs[q * ppb + i][...], axis=0)
        m = acc * (1.0 / MOBA_BLOCK)
        for h in range(MOBA_HEADS):
            o_ref[h, pl.ds(j * bps + q, 1), :] = m[h:h + 1, :]


def _moba_means_sample(cache, li, page_table):
    nb, n_pages = page_table.shape
    page = cache.shape[2]
    ppb = MOBA_BLOCK // page
    n_full = n_pages // ppb
    bps = math.gcd(MEANS_BLOCKS, n_full)
    body = functools.partial(_means_body, ppb=ppb, bps=bps)
    in_specs = [
        pl.BlockSpec((None, None, page, None, MOBA_HEADS, HEAD_DIM),
                     lambda b, j, pt, i=i: (li, pt[b, j * bps * ppb + i], 0, 0, 0, 0))
        for i in range(bps * ppb)
    ]
    return pl.pallas_call(
        body,
        grid_spec=pltpu.PrefetchScalarGridSpec(
            num_scalar_prefetch=1,
            grid=(nb, n_full // bps),
            in_specs=in_specs,
            out_specs=pl.BlockSpec((None, MOBA_HEADS, n_full, HEAD_DIM), lambda b, j, pt: (b, 0, 0, 0)),
        ),
        out_shape=jax.ShapeDtypeStruct((nb, MOBA_HEADS, n_full, HEAD_DIM), F32),
        compiler_params=_cparams(("parallel", "arbitrary")),
        name="moba_means_sample",
    )(page_table, *([cache] * (bps * ppb)))


FLASH_UNIT_ROWS = 256
LOG2_E = 1.4426950408889634
MASK_BIG = 2.0 ** 60


def _flash_body(*refs, tq, tk, nh, blk, window, masked, ur):
    q_ref, k_ref, v_ref = refs[:3]
    pos = 3
    if masked:
        sel_ref = refs[pos]
        pos += 1
    o_ref, qs_ref, kt_ref = refs[pos:pos + 3]
    pos += 1
    units = [(z, r0) for z in range(nh) for r0 in range(0, tq, ur)]
    stat = refs[pos + 2:pos + 2 + 2 * len(units)]
    m_refs, acc_refs = stat[0::2], stat[1::2]
    qi = pl.program_id(2)
    t0 = qi * tq
    scale = HEAD_DIM ** -0.5 * LOG2_E
    if masked:
        bias = ((jnp.transpose(sel_ref[...]) - 1.0) * MASK_BIG).astype(BF)
    for z in range(nh):
        qz = (q_ref[:, z * HEAD_DIM:(z + 1) * HEAD_DIM] * scale).astype(BF)
        qs_ref[z] = jnp.concatenate([qz, bias], axis=1) if masked else qz
    for u in range(len(units)):
        m_refs[u][...] = jnp.full(m_refs[u].shape, NEG_INF, F32)
        acc_refs[u][...] = jnp.zeros(acc_refs[u].shape, F32)
    trow = t0 + lax.broadcasted_iota(jnp.int32, (tq, 1), 0)

    @pl.when(qi == 0)
    def _():
        for c in range(kt_ref.shape[0]):
            kc = k_ref[c * tk:(c + 1) * tk, :]
            if masked:
                prow = c * tk + lax.broadcasted_iota(jnp.int32, (tk, 1), 0)
                jcol = lax.broadcasted_iota(jnp.int32, (1, LANE), 1)
                kc = jnp.concatenate([kc, jnp.where(prow // blk == jcol, 1.0, 0.0).astype(BF)], axis=1)
            kt_ref[c] = kc.T

    def process(c, edge):
        ks = pl.multiple_of(c * tk, tk)
        kt = kt_ref[c]
        vc = jnp.concatenate([v_ref[pl.ds(ks, tk), :], jnp.ones((tk, LANE), BF)], axis=1)
        if edge:
            pcol = ks + lax.broadcasted_iota(jnp.int32, (1, tk), 1)
            valid = pcol <= trow
            if window:
                valid = jnp.logical_and(valid, pcol > trow - window)
        for u, (z, r0) in enumerate(units):
            s = jnp.dot(qs_ref[z, r0:r0 + ur, :], kt, preferred_element_type=F32)
            m_old = m_refs[u][...]
            if edge:
                s = jnp.where(valid[r0:r0 + ur], s, NEG_INF)
                m_new = jnp.maximum(m_old, jnp.max(s, axis=-1, keepdims=True))
                m_use = jnp.where(m_new == NEG_INF, 0.0, m_new)
            else:
                m_new = jnp.maximum(m_old, jnp.max(s, axis=-1, keepdims=True))
                m_use = m_new
            alpha = jnp.exp2(m_old - m_use)
            p = jnp.exp2(s - m_use)
            acc_refs[u][...] = alpha * acc_refs[u][...] + jnp.dot(p.astype(BF), vc, preferred_element_type=F32)
            m_refs[u][...] = m_new

    if masked:
        def pair(c2, carry):
            process(2 * c2, False)
            process(2 * c2 + 1, False)
            return carry

        lax.fori_loop(0, qi // 2, pair, 0)

        @pl.when(qi % 2 == 1)
        def _():
            process(qi - 1, False)

        process(qi, True)
    else:
        c_hi = (t0 + tq + tk - 1) // tk
        c_lo = jnp.maximum(t0 - window + 1, 0) // tk
        lax.fori_loop(c_lo, c_hi, lambda c, carry: (process(c, True), carry)[1], 0)
    for u, (z, r0) in enumerate(units):
        l = acc_refs[u][:, HEAD_DIM:HEAD_DIM + 1]
        o_ref[r0:r0 + ur, z * HEAD_DIM:(z + 1) * HEAD_DIM] = acc_refs[u][:, :HEAD_DIM] / jnp.where(l > 0, l, 1.0)


def _flash(y, y16, sel, nb, t, *, q_ch, k_ch, v_ch, kvh, nh, blk, window, out_rows, tq=512, tk=512):
    tq = min(tq, t)
    tk = min(tk, t)
    nq = t // tq
    masked = sel is not None
    gw = nh * HEAD_DIM
    assert q_ch % nh == 0 and (tq == tk or not masked)
    in_specs = [pl.BlockSpec((tq, gw), lambda b, h, qi: (b * nq + qi, q_ch // nh + h)),
                pl.BlockSpec((t, LANE), lambda b, h, qi: (b, k_ch + h)),
                pl.BlockSpec((t, LANE), lambda b, h, qi: (b, v_ch + h))]
    args = [y, y16, y16]
    if masked:
        in_specs.append(pl.BlockSpec((None, None, LANE, tq), lambda b, h, qi: (b, h, 0, qi)))
        args.append(sel)
    ur = min(FLASH_UNIT_ROWS, tq)
    n_units = nh * (tq // ur)
    kd = (2 if masked else 1) * HEAD_DIM
    body = functools.partial(_flash_body, tq=tq, tk=tk, nh=nh, blk=blk, window=window, masked=masked, ur=ur)
    unit_scratch = [pltpu.VMEM((ur, 1), F32), pltpu.VMEM((ur, 2 * HEAD_DIM), F32)]
    return pl.pallas_call(
        body,
        grid=(nb, kvh, nq),
        in_specs=in_specs,
        out_specs=pl.BlockSpec((tq, gw), lambda b, h, qi: (b * nq + qi, h)),
        out_shape=jax.ShapeDtypeStruct((out_rows, kvh * gw), F32),
        scratch_shapes=[pltpu.VMEM((nh, tq, kd), BF), pltpu.VMEM((t // tk, kd, tk), BF)] + unit_scratch * n_units,
        compiler_params=_cparams(("parallel", "parallel", "arbitrary")),
        name="flash_" + ("win" if window else "blk%d" % blk),
    )(*args)


def _dec_body(tbl_ref, vld_ref, q_ref, *refs, nh, rows, kvh, nblk, r_min):
    k_refs = refs[:nblk]
    v_refs = refs[nblk:2 * nblk]
    kn_ref, vn_ref, o_ref = refs[2 * nblk:2 * nblk + 3]
    b = pl.program_id(0)
    h = pl.program_id(1)
    scale = HEAD_DIM ** -0.5
    qrow = q_ref[pl.ds(b, 1), :] * scale
    zrow = lax.broadcasted_iota(jnp.int32, (SUBLANE, 1), 0)
    qm = jnp.zeros((SUBLANE, HEAD_DIM), F32)
    for z in range(nh):
        qm = jnp.where(zrow == z, qrow[:, z * HEAD_DIM:(z + 1) * HEAD_DIM], qm)
    qb = qm.astype(BF)
    col = lax.broadcasted_iota(jnp.int32, (1, rows * kvh), 1)
    valid = jnp.logical_and(col % kvh == h, col // kvh >= r_min)
    kn = kn_ref[pl.ds(b, 1), :]
    vn = vn_ref[pl.ds(b, 1), :]
    s_new = jnp.sum(qm * kn, axis=-1, keepdims=True)
    scores = []
    m = s_new
    for i in range(nblk):
        kk = k_refs[i][...].reshape(rows * kvh, HEAD_DIM).astype(BF)
        s = lax.dot_general(qb, kk, NT_DIMS, preferred_element_type=F32)
        s = jnp.where(jnp.logical_and(valid, vld_ref[b, h, i] > 0), s, NEG_INF)
        scores.append(s)
        m = jnp.maximum(m, jnp.max(s, axis=-1, keepdims=True))
    p_new = jnp.exp(s_new - m)
    l = p_new
    acc = p_new * vn
    for i in range(nblk):
        p = jnp.exp(scores[i] - m)
        l = l + jnp.sum(p, axis=-1, keepdims=True)
        vv = v_refs[i][...].reshape(rows * kvh, HEAD_DIM).astype(BF)
        acc = acc + jnp.dot(p.astype(BF), vv, preferred_element_type=F32)
    o_ref[...] = acc / l


def _decode_attn(ys, cache, li_fixed, tbl, vld, *, q_ch, kn_ch, vn_ch, nh, rows, kvh, r_min, name):
    nb, _, nblk = tbl.shape
    page = cache.shape[2]
    bpp = page // rows
    gw = nh * HEAD_DIM
    body = functools.partial(_dec_body, nh=nh, rows=rows, kvh=kvh, nblk=nblk, r_min=r_min)

    def kv_spec(kv, i):
        def imap(b, h, tb, vl):
            t = tb[b, h, i]
            return (li_fixed, t // bpp, t % bpp, kv, 0, 0)

        return pl.BlockSpec((None, None, rows, None, kvh, HEAD_DIM), imap)

    nrow = ys.shape[0]
    return pl.pallas_call(
        body,
        grid_spec=pltpu.PrefetchScalarGridSpec(
            num_scalar_prefetch=2,
            grid=(nb, kvh),
            in_specs=[pl.BlockSpec((nrow, gw), lambda b, h, tb, vl: (0, q_ch // nh + h))]
            + [kv_spec(0, i) for i in range(nblk)] + [kv_spec(1, i) for i in range(nblk)]
            + [pl.BlockSpec((nrow, LANE), lambda b, h, tb, vl: (0, kn_ch + h)),
               pl.BlockSpec((nrow, LANE), lambda b, h, tb, vl: (0, vn_ch + h))],
            out_specs=pl.BlockSpec((None, None, SUBLANE, HEAD_DIM), lambda b, h, tb, vl: (b, h, 0, 0)),
        ),
        out_shape=jax.ShapeDtypeStruct((nb, kvh, SUBLANE, HEAD_DIM), F32),
        compiler_params=_cparams(("parallel", "parallel")),
        name=name,
    )(tbl, vld, ys, *([cache] * (2 * nblk)), ys, ys)


def _mix_body(oc_ref, os_ref, ow_ref, ob_ref, soc_ref, sos_ref, sow_ref, sob_ref, gate_ref, o_ref, *, n_p_tiles, n_s):
    i = pl.program_id(0)
    w = NSA_HEADS * HEAD_DIM

    def emit(oc, os_, ow, ob, rows):
        g = jax.nn.sigmoid(gate_ref[0:rows, :])
        for h in range(NSA_HEADS):
            sl = slice(h * HEAD_DIM, (h + 1) * HEAD_DIM)
            o = (g[:, 3 * h:3 * h + 1] * oc[:, sl] + g[:, 3 * h + 1:3 * h + 2] * os_[:, sl]
                 + g[:, 3 * h + 2:3 * h + 3] * ow[:, sl])
            o_ref[0:rows, sl] = o.astype(o_ref.dtype)
        o_ref[0:rows, w:] = ob[...].astype(o_ref.dtype)

    @pl.when(i < n_p_tiles)
    def _():
        emit(oc_ref, os_ref, ow_ref, ob_ref, o_ref.shape[0])

    @pl.when(i >= n_p_tiles)
    def _():
        emit(soc_ref, sos_ref, sow_ref, sob_ref, n_s)


def _mix(prompt_parts, sample_parts, y, tm=256):
    n_prompt = prompt_parts[0].shape[0]
    n_s = sample_parts[0].shape[0]
    m = y.shape[0]
    tm = min(tm, n_prompt)
    assert n_prompt % tm == 0 and m == n_prompt + n_s and n_s <= tm
    n_p_tiles = n_prompt // tm
    wa = NSA_HEADS * HEAD_DIM
    wb = MOBA_HEADS * HEAD_DIM
    pmap = lambda i: (jnp.minimum(i, n_p_tiles - 1), 0)
    body = functools.partial(_mix_body, n_p_tiles=n_p_tiles, n_s=n_s)
    return pl.pallas_call(
        body,
        grid=(n_p_tiles + 1,),
        in_specs=[pl.BlockSpec((tm, wa), pmap)] * 3 + [pl.BlockSpec((tm, wb), pmap)]
        + [pl.BlockSpec((n_s, wa), lambda i: (0, 0))] * 3 + [pl.BlockSpec((n_s, wb), lambda i: (0, 0))]
        + [pl.BlockSpec((tm, LANE), lambda i: (i, CH_GATE))],
        out_specs=pl.BlockSpec((tm, wa + wb), lambda i: (i, 0)),
        out_shape=jax.ShapeDtypeStruct((m, wa + wb), BF),
        compiler_params=_cparams(("parallel",)),
        name="attn_mix",
    )(*prompt_parts, *sample_parts, y)


def _convmix_body(up_ref, hp_ref, uc_ref, hc_ref, pw_ref, ps_ref, dw_ref, dwb_ref, lng_ref, lnb_ref, pww_ref,
                  o_ref, zp_ref, zc_ref, cb_ref, *, ts, nt, zero_first, avail0, rc):
    s = pl.program_id(0)
    hp_rows = hp_ref.shape[0]
    hc_rows = hc_ref.shape[0]
    width = up_ref.shape[1]
    if zero_first:
        keep = jnp.where(s % nt == 0, 0.0, 1.0)
        zp_ref[0:hp_rows, :] = hp_ref[...] * keep
        zc_ref[0:hc_rows, :] = hc_ref[...] * keep
        avail = (s % nt) * ts + avail0
    else:
        zp_ref[0:hp_rows, :] = hp_ref[...]
        zc_ref[0:hc_rows, :] = hc_ref[...]
        avail = avail0
    zp_ref[hp_rows:, :] = up_ref[...]
    zc_ref[hc_rows:, :] = uc_ref[...]
    gwidth = width // len(POOL_WINDOWS)
    t_idx = lax.broadcasted_iota(jnp.int32, (ts, 1), 0) + avail + 1
    for gi, w in enumerate(POOL_WINDOWS):
        sl = slice(gi * gwidth, (gi + 1) * gwidth)
        cur = zp_ref[hp_rows:hp_rows + ts, sl]
        acc = cur
        for jj in range(1, w):
            acc = acc + zp_ref[hp_rows - jj:hp_rows - jj + ts, sl]
        cnt = jnp.minimum(t_idx, w).astype(F32)
        yg = acc / cnt - cur
        og = jnp.dot(yg.astype(BF), pw_ref[gi], preferred_element_type=F32) * ps_ref[:, sl]
        o_ref[:, sl] = og.astype(o_ref.dtype)
    base = hc_rows - (CONV_WIDTH - 1)
    for r in range(ts // rc):
        for c in range(width // LANE):
            cs = slice(c * LANE, (c + 1) * LANE)
            acc = jnp.zeros((rc, LANE), F32)
            for jj in range(CONV_WIDTH):
                lo = r * rc + base + jj
                acc = acc + zc_ref[lo:lo + rc, cs] * dw_ref[jj:jj + 1, cs]
            cb_ref[r * rc:(r + 1) * rc, cs] = acc + dwb_ref[:, cs]
    cv = cb_ref[...]
    mu = jnp.mean(cv, axis=-1, keepdims=True)
    xc = cv - mu
    yn = xc * lax.rsqrt(jnp.mean(xc * xc, axis=-1, keepdims=True) + NORM_EPS) * lng_ref[...] + lnb_ref[...]
    act = yn * jax.nn.sigmoid(yn)
    o_ref[:, width:] = jnp.dot(act.astype(BF), pww_ref[...], preferred_element_type=F32).astype(o_ref.dtype)


def _convmix(u_pool, halo_pool, u_conv, halo_conv, prm, *, ts, nt, n_tiles, zero_first, avail0, hp_rows, hc_rows,
             halo_blk):
    pw, ps, dw, dwb, lng, lnb, pww = prm
    width = u_pool.shape[1]
    rc = min(64, ts)
    body = functools.partial(_convmix_body, ts=ts, nt=nt, zero_first=zero_first, avail0=avail0, rc=rc)
    full = lambda a: pl.BlockSpec(a.shape, lambda s: (0,) * a.ndim)
    return pl.pallas_call(
        body,
        grid=(n_tiles,),
        in_specs=[pl.BlockSpec((ts, width), lambda s: (s, 0)),
                  pl.BlockSpec((hp_rows, width), lambda s: (halo_blk(s, hp_rows), 0)),
                  pl.BlockSpec((ts, width), lambda s: (s, 0)),
                  pl.BlockSpec((hc_rows, width), lambda s: (halo_blk(s, hc_rows), 0)),
                  full(pw), full(ps), full(dw), full(dwb), full(lng), full(lnb), full(pww)],
        out_specs=pl.BlockSpec((ts, 2 * width), lambda s: (s, 0)),
        out_shape=jax.ShapeDtypeStruct((n_tiles * ts, 2 * width), BF),
        scratch_shapes=[pltpu.VMEM((hp_rows + ts, width), F32), pltpu.VMEM((hc_rows + ts, width), F32),
                        pltpu.VMEM((ts, width), F32)],
        compiler_params=_cparams(("parallel",)),
        name="conv_mix",
    )(u_pool, halo_pool, u_conv, halo_conv, pw, ps, dw, dwb, lng, lnb, pww)


def _router_body(x_ref, g_ref, rw_ref, r_ref, xn_ref):
    x = x_ref[...]
    xn = x * lax.rsqrt(jnp.mean(x * x, axis=-1, keepdims=True) + NORM_EPS) * g_ref[...]
    xn_ref[...] = xn.astype(xn_ref.dtype)
    logits = _dot3(xn, rw_ref[...])
    lane = lax.broadcasted_iota(jnp.int32, logits.shape, 1)
    l1 = jnp.where(lane < N_EXPERTS, logits, NEG_INF)
    m1 = jnp.max(l1, axis=-1, keepdims=True)
    i1 = jnp.min(jnp.where(l1 == m1, lane, LANE), axis=-1, keepdims=True)
    l2 = jnp.where(lane == i1, NEG_INF, l1)
    m2 = jnp.max(l2, axis=-1, keepdims=True)
    i2 = jnp.min(jnp.where(l2 == m2, lane, LANE), axis=-1, keepdims=True)
    e = jnp.exp(m2 - m1)
    w1 = 1.0 / (1.0 + e)
    w2 = e / (1.0 + e)
    r_ref[...] = jnp.where(lane == 0, i1.astype(F32),
                           jnp.where(lane == 1, i2.astype(F32), jnp.where(lane == 2, w1, jnp.where(lane == 3, w2, 0.0))))


def _router(x, g, rw, tm=256):
    m, d = x.shape
    rw_pad = jnp.pad(rw, ((0, 0), (0, LANE - rw.shape[1])))
    return pl.pallas_call(
        _router_body,
        grid=(pl.cdiv(m, tm),),
        in_specs=[pl.BlockSpec((tm, d), lambda i: (i, 0)), pl.BlockSpec((1, d), lambda i: (0, 0)),
                  pl.BlockSpec((d, LANE), lambda i: (0, 0))],
        out_specs=[pl.BlockSpec((tm, LANE), lambda i: (i, 0)), pl.BlockSpec((tm, d), lambda i: (i, 0))],
        out_shape=[jax.ShapeDtypeStruct((m, LANE), F32), jax.ShapeDtypeStruct((m, d), BF)],
        compiler_params=_cparams(("parallel",)),
        name="moe_router",
    )(x, g.reshape(1, d), rw_pad)


def _row_copy(src_hbm, row, dst, drow, sem):
    return pltpu.make_async_copy(src_hbm.at[pl.ds(row, 1), :], dst.at[pl.ds(drow, 1), :], sem)


def _gather_rows_body(c0_ref, nc_ref, src_ref, xn_hbm, o_ref, buf_ref, acc_ref, sem, *, tc):
    i = pl.program_id(0)
    c0 = c0_ref[i]
    n = nc_ref[i]

    def chunk_copy(c, slot):
        return pltpu.make_async_copy(xn_hbm.at[pl.ds(c * tc, tc), :], buf_ref.at[slot], sem.at[slot])

    acc_ref[...] = jnp.zeros(acc_ref.shape, F32)

    @pl.when(n > 0)
    def _():
        chunk_copy(c0, 0).start()

    src = src_ref[...]

    def body(k, carry):
        slot = k % 2
        chunk_copy(c0 + k, slot).wait()

        @pl.when(k + 1 < n)
        def _():
            chunk_copy(c0 + k + 1, 1 - slot).start()

        tok = (c0 + k) * tc + lax.broadcasted_iota(jnp.int32, (1, tc), 1)
        onehot = jnp.where(src == tok, 1.0, 0.0).astype(buf_ref.dtype)
        acc_ref[...] += jnp.dot(onehot, buf_ref[slot], preferred_element_type=F32)
        return carry

    lax.fori_loop(0, n, body, 0)
    o_ref[...] = acc_ref[...].astype(o_ref.dtype)


def _gather_rows(xn, src, c0, nc, tm, tc):
    n_tok, d = xn.shape
    r_tot = src.shape[0]
    body = functools.partial(_gather_rows_body, tc=tc)
    return pl.pallas_call(
        body,
        grid_spec=pltpu.PrefetchScalarGridSpec(
            num_scalar_prefetch=2,
            grid=(r_tot // tm,),
            in_specs=[pl.BlockSpec((tm, 1), lambda i, a, b: (i, 0)), pl.BlockSpec(memory_space=pl.ANY)],
            out_specs=pl.BlockSpec((tm, d), lambda i, a, b: (i, 0)),
            scratch_shapes=[pltpu.VMEM((2, tc, d), xn.dtype), pltpu.VMEM((tm, d), F32), pltpu.SemaphoreType.DMA((2,))],
        ),
        out_shape=jax.ShapeDtypeStruct((r_tot, d), xn.dtype),
        compiler_params=_cparams(("arbitrary",)),
        name="moe_gather",
    )(c0, nc, src.reshape(r_tot, 1), xn)


def _combine_body(p1_ref, p2_ref, eo_hbm, x_ref, r_ref, o_ref, b1_ref, b2_ref, sem, *, tm, n_tok):
    base = pl.program_id(0) * tm
    n = jnp.minimum(tm, n_tok - base)

    def issue(r, c):
        _row_copy(eo_hbm, p1_ref[base + r], b1_ref, r, sem).start()
        _row_copy(eo_hbm, p2_ref[base + r], b2_ref, r, sem).start()
        return c

    lax.fori_loop(0, n, issue, 0)

    def wait(r, c):
        _row_copy(eo_hbm, 0, b1_ref, r, sem).wait()
        _row_copy(eo_hbm, 0, b2_ref, r, sem).wait()
        return c

    lax.fori_loop(0, n, wait, 0)
    rr = r_ref[...]
    o_ref[...] = x_ref[...] + rr[:, 2:3] * b1_ref[...] + rr[:, 3:4] * b2_ref[...]


def _combine(x, eo, routing, p1, p2, tm_max=512):
    m, d = x.shape
    tm = max(t for t in range(SUBLANE, tm_max + 1, SUBLANE) if m % t == 0)
    body = functools.partial(_combine_body, tm=tm, n_tok=m)
    return pl.pallas_call(
        body,
        grid_spec=pltpu.PrefetchScalarGridSpec(
            num_scalar_prefetch=2,
            grid=(pl.cdiv(m, tm),),
            in_specs=[pl.BlockSpec(memory_space=pl.ANY), pl.BlockSpec((tm, d), lambda i, a, b: (i, 0)),
                      pl.BlockSpec((tm, LANE), lambda i, a, b: (i, 0))],
            out_specs=pl.BlockSpec((tm, d), lambda i, a, b: (i, 0)),
            scratch_shapes=[pltpu.VMEM((tm, d), F32), pltpu.VMEM((tm, d), F32), pltpu.SemaphoreType.DMA(())],
        ),
        out_shape=jax.ShapeDtypeStruct((m, d), F32),
        compiler_params=_cparams(("arbitrary",)),
        name="moe_combine",
    )(p1, p2, eo, x, routing)


def _moe_plan(e_idx, tm, tc):
    n = e_idx.shape[0]
    n_asg = n * TOP_K
    flat_e = e_idx.reshape(-1)
    onehot = (flat_e[:, None] == jnp.arange(N_EXPERTS, dtype=jnp.int32)[None, :]).astype(jnp.int32)
    cnt = jnp.sum(onehot, axis=0)
    rank = jnp.take_along_axis(jnp.cumsum(onehot, axis=0) - onehot, flat_e[:, None], axis=1)[:, 0]
    cnt_p = ((cnt + tm - 1) // tm) * tm
    ends = jnp.cumsum(cnt_p)
    off = ends - cnt_p
    pos = off[flat_e] + rank
    n_tiles = (n_asg + N_EXPERTS * (tm - 1) + tm - 1) // tm
    r_tot = n_tiles * tm
    src = jnp.full((r_tot,), -1, jnp.int32).at[pos].set(jnp.arange(n_asg, dtype=jnp.int32) // TOP_K)
    src_t = src.reshape(n_tiles, tm)
    lo = jnp.min(jnp.where(src_t >= 0, src_t, n), axis=1)
    hi = jnp.max(src_t, axis=1)
    c0 = jnp.where(hi >= 0, lo // tc, 0).astype(jnp.int32)
    nc = jnp.where(hi >= 0, hi // tc + 1 - lo // tc, 0).astype(jnp.int32)
    tile_start = jnp.arange(n_tiles, dtype=jnp.int32) * tm
    tval = (tile_start < ends[-1]).astype(jnp.int32)
    texp = jnp.minimum(jnp.sum((tile_start[:, None] >= ends[None, :]).astype(jnp.int32), axis=1), N_EXPERTS - 1)
    last = jnp.max(jnp.where(tval > 0, texp, 0))
    texp = jnp.where(tval > 0, texp, last)
    pos2 = pos.reshape(n, TOP_K)
    return src, c0, nc, pos2[:, 0], pos2[:, 1], texp, tval


def _moe(x, g, rw, wg, wu, wd, tm=256):
    n = x.shape[0]
    tc = max(t for t in range(2 * SUBLANE, 513, 2 * SUBLANE) if n % t == 0)
    routing, xn = _router(x, g, rw)
    e_idx = routing[:, :TOP_K].astype(jnp.int32)
    src, c0, nc, p1, p2, texp, tval = _moe_plan(e_idx, tm, tc)
    xs = _gather_rows(xn, src, c0, nc, tm, tc)
    hid = _gmm(xs, [(wg, 0), (wu, 0)], wg.shape[2], epi="swiglu", texp=texp, tval=tval, out_dtype=BF, tm=tm, tn=1024,
               name="moe_up")
    eo = _gmm(hid, [(wd, 0)], wd.shape[2], texp=texp, tval=tval, tm=tm, name="moe_down")
    return _combine(x, eo, routing, p1, p2)


def _seq_tails(a, nb, t, k):
    return jnp.stack([a[(b + 1) * t - k:(b + 1) * t] for b in range(nb)])


def _attn_layer(x, n_prompt, nb_p, t_p, nb_s, li, prm, caches, page_table):
    (attn_norm, w_in, qk_g, cmp_pe, cmp_w1, cmp_w2, moba_g, w_out, ffn_norm, w_gate, w_up, w_down) = prm
    cache_cmp, cache_sel, state_win, cache_moba = caches
    d = x[0].shape[1]
    past_len = page_table.shape[1] * cache_moba.shape[2]

    n_in = w_in.shape[1]
    g_lo, g_hi = CH_QB * LANE, CH_QB * LANE + 3 * NSA_HEADS
    w_re = jnp.concatenate([w_in[:, :g_lo], w_in[:, g_hi:], w_in[:, g_lo:g_hi],
                            jnp.zeros((d, ATTN_CHUNKS * LANE - n_in), F32)], axis=1).astype(BF)[None]
    ones = jnp.ones((HEAD_DIM,), F32)
    zeros = jnp.zeros((HEAD_DIM,), F32)
    chunk_gain = ([qk_g[0]] * 8 + [ones] * 4 + [qk_g[2]] * 2 + [ones] * 2 + [qk_g[3]] * 2 + [ones] * 2
                  + [moba_g[0]] * 8 + [moba_g[1]] * 8 + [ones] * 9)
    chunk_flag = ([ones] * 8 + [zeros] * 4 + [ones] * 2 + [zeros] * 2 + [ones] * 2 + [zeros] * 2
                  + [ones] * 16 + [zeros] * 9)
    gain = jnp.concatenate(chunk_gain)[None]
    flag = jnp.concatenate(chunk_flag)[None]
    xn = _rmsnorm(x, attn_norm, tm=STREAM_TM)
    y, y16 = _gmm(xn, [(w_re, 0)], ATTN_CHUNKS * LANE, epi="headnorm", gain=gain, flag=flag, out_dtype=(F32, BF),
                  tm=STREAM_TM, tn=9 * LANE, name="attn_in")

    r = NSA_CMP_LEN // NSA_CMP_STRIDE
    wcat = cmp_w1.reshape(2, r, NSA_CMP_STRIDE, HEAD_DIM, NSA_CMP_HIDDEN).transpose(0, 2, 3, 1, 4)
    wcat = wcat.reshape(2, NSA_CMP_STRIDE, HEAD_DIM, r * NSA_CMP_HIDDEN).astype(BF)
    pe_rows = cmp_pe.reshape(2, r, NSA_CMP_STRIDE, HEAD_DIM).transpose(0, 2, 1, 3)
    pe_rows = jnp.pad(pe_rows, ((0, 0), (0, 0), (0, SUBLANE - r), (0, 0))).astype(BF)
    w2 = cmp_w2.astype(BF)
    gk = qk_g[1][None]

    kvc_p = _compress_c2(_compress_c1_prompt(y, wcat, nb_p, t_p), pe_rows, wcat, w2, gk)
    rows = n_prompt
    oc, sel_a = _cattn(y, 0, 0, kvc_p, nb_p, t_p, min(512, t_p), t_p, 0, rows)
    sel_b = _moba_gate_prompt(y, nb_p, t_p, min(1024, t_p))
    os_ = _flash(y, y16, sel_a, nb_p, t_p, q_ch=CH_QA, k_ch=CH_KS, v_ch=CH_VS, kvh=NSA_KV, nh=NSA_HPG,
                 blk=NSA_SEL_BLOCK, window=0, out_rows=rows)
    ow = _flash(y, y16, None, nb_p, t_p, q_ch=CH_QA, k_ch=CH_KW, v_ch=CH_VW, kvh=NSA_KV, nh=NSA_HPG, blk=0,
                window=NSA_WINDOW, out_rows=rows)
    ob = _flash(y, y16, sel_b, nb_p, t_p, q_ch=CH_QB, k_ch=CH_KB, v_ch=CH_VB, kvh=MOBA_HEADS, nh=1, blk=MOBA_BLOCK,
                window=0, out_rows=rows)

    n_pad = x[2] - n_prompt
    ys = y[n_prompt:]
    kvc_s = _compress_c2(_compress_c1_sample(cache_cmp, li, page_table, wcat), pe_rows, wcat, w2, gk)
    q_pad = jnp.zeros((nb_s, SUBLANE, NSA_HEADS * HEAD_DIM), F32).at[:, 0].set(ys[:nb_s, :NSA_HEADS * HEAD_DIM])
    oc_s, sel_s = _cattn_sample(q_pad.reshape(nb_s * SUBLANE, -1), kvc_s, nb_s, past_len + 1, past_len)
    oc_s = oc_s.reshape(nb_s, SUBLANE, -1)[:, 0]
    n_sel = (past_len + 1) // NSA_SEL_BLOCK
    k_sel = min(NSA_SEL_COUNT - 1, n_sel)
    mask_a = sel_s[:, :, 0, :n_sel]
    idx_a = jnp.argsort(-mask_a, axis=-1, stable=True)[..., :k_sel].astype(jnp.int32)
    vld_a = (jnp.take_along_axis(mask_a, idx_a, axis=-1) > 0).astype(jnp.int32)
    page = cache_sel.shape[2]
    bpp = page // NSA_SEL_BLOCK
    pt_b = page_table[:, None, :]
    tbl_a = jnp.take_along_axis(jnp.broadcast_to(pt_b, (nb_s, NSA_KV, pt_b.shape[-1])), idx_a // bpp, axis=-1) * bpp + idx_a % bpp
    os_s = _decode_attn(ys, cache_sel, li, tbl_a.astype(jnp.int32), vld_a, q_ch=CH_QA, kn_ch=CH_KS, vn_ch=CH_VS,
                        nh=NSA_HPG, rows=NSA_SEL_BLOCK, kvh=NSA_KV, r_min=0,
                        name="dec_sel")
    win_buf = state_win.shape[2]
    tbl_w = jnp.broadcast_to(jnp.arange(nb_s, dtype=jnp.int32)[:, None, None], (nb_s, NSA_KV, 1))
    ow_s = _decode_attn(ys, state_win, li, tbl_w, jnp.ones_like(tbl_w), q_ch=CH_QA, kn_ch=CH_KW, vn_ch=CH_VW,
                        nh=NSA_HPG, rows=win_buf, kvh=NSA_KV, r_min=max(0, win_buf - (NSA_WINDOW - 1)),
                        name="dec_win")
    means = _moba_means_sample(cache_moba, li, page_table)
    qb_pad = jnp.zeros((nb_s, SUBLANE, MOBA_HEADS * HEAD_DIM), F32).at[:, 0].set(
        ys[:nb_s, CH_QB * LANE:(CH_QB + MOBA_HEADS) * LANE])
    sel_m = _moba_gate_sample(qb_pad.reshape(nb_s * SUBLANE, -1), means, past_len)
    n_full = means.shape[2]
    k_top = min(MOBA_TOPK, n_full)
    mask_b = sel_m[:, :, :n_full, 0]
    idx_b = jnp.argsort(-mask_b, axis=-1, stable=True)[..., :k_top].astype(jnp.int32)
    vld_b = (jnp.take_along_axis(mask_b, idx_b, axis=-1) > 0).astype(jnp.int32)
    ppb = MOBA_BLOCK // page
    pg_b = (idx_b[..., None] * ppb + jnp.arange(ppb, dtype=jnp.int32)).reshape(nb_s, MOBA_HEADS, k_top * ppb)
    tbl_b = jnp.take_along_axis(jnp.broadcast_to(pt_b, (nb_s, MOBA_HEADS, pt_b.shape[-1])), pg_b, axis=-1)
    vld_b = jnp.repeat(vld_b, ppb, axis=-1)
    ob_s = _decode_attn(ys, cache_moba, li, tbl_b.astype(jnp.int32), vld_b, q_ch=CH_QB, kn_ch=CH_KB, vn_ch=CH_VB,
                        nh=1, rows=page, kvh=MOBA_HEADS, r_min=0,
                        name="dec_moba")

    def rows_s(part, nh):
        return jnp.pad(part[:, :, :nh].reshape(nb_s, -1), ((0, n_pad - nb_s), (0, 0)))

    sample_parts = (jnp.pad(oc_s, ((0, n_pad - nb_s), (0, 0))), rows_s(os_s, NSA_HPG), rows_s(ow_s, NSA_HPG),
                    rows_s(ob_s, 1))
    mixed = _mix((oc, os_, ow, ob), sample_parts, y)
    x = _gmm(mixed, [(w_out[None], 0)], d, resid=x, tm=STREAM_TM, tn=1024, name="attn_out")
    hid = _gmm(_rmsnorm(x, ffn_norm), [(w_gate[None], 0), (w_up[None], 0)], w_gate.shape[1], epi="swiglu",
               out_dtype=BF, tm=STREAM_TM, tn=512, name="ffn_up")
    x = _gmm(hid, [(w_down[None], 0)], d, resid=x, tm=STREAM_TM, tn=512, name="ffn_down")

    def rows_of(lo_ch, n_ch):
        return y[:, lo_ch * LANE:(lo_ch + n_ch) * LANE]

    def split(a, kvh):
        ap = a[:n_prompt].reshape(nb_p, t_p, 2, kvh, HEAD_DIM)
        as_ = a[n_prompt:n_prompt + nb_s].reshape(nb_s, 1, 2, kvh, HEAD_DIM)
        return ap, as_

    cmp_p, cmp_s = split(rows_of(CH_KC, 4), NSA_KV)
    sel_p, sel_s_rows = split(rows_of(CH_KS, 4), NSA_KV)
    win_rows = rows_of(CH_KW, 4)
    win_s = win_rows[n_prompt:n_prompt + nb_s].reshape(nb_s, 1, 2, NSA_KV, HEAD_DIM)
    moba_p, moba_s = split(rows_of(CH_KB, 16), MOBA_HEADS)
    keep_p = min(NSA_WINDOW, t_p)
    new_win_p = _seq_tails(win_rows, nb_p, t_p, keep_p).reshape(nb_p, keep_p, 2, NSA_KV, HEAD_DIM)
    win_all = jnp.concatenate([state_win[li], win_s], axis=1)
    keep = min(NSA_WINDOW, past_len + 1)
    new_win_s = win_all[:, win_all.shape[1] - keep:]
    return x, (cmp_p, cmp_s, sel_p, sel_s_rows, new_win_p, new_win_s, moba_p, moba_s)


def _conv_layer(x, n_prompt, nb_p, t_p, nb_s, prm, states):
    (conv_norm, w_in, pool_w, pool_scale, dw, dw_b, ln_g, ln_b, pw, w_out, moe_norm, router_w, wg, wu, wd) = prm
    state_pool, state_conv = states
    d = x.shape[1]
    width = pool_w.shape[0] * pool_w.shape[1]
    xn = _rmsnorm(x, conv_norm)
    w3 = w_in[None]
    u_pool = _gmm(xn, [(w3, 0)], width, tm=512, tn=width, name="conv_in_pool")
    u_conv = _gmm(xn, [(w3, width), (w3, 2 * width)], width, epi="glu", tm=512, tn=512, name="conv_in_glu")
    prm_mix = (pool_w.astype(BF), pool_scale[None], jnp.pad(dw, ((0, 1), (0, 0))), dw_b[None], ln_g[None], ln_b[None],
               pw.astype(BF))
    pool_buf = max(POOL_WINDOWS) - 1
    conv_buf = CONV_WIDTH - 1
    hp_rows, hc_rows = 16, 32
    ts = min(256, t_p)
    nt = t_p // ts
    mixed_p = _convmix(u_pool, u_pool, u_conv, u_conv, prm_mix, ts=ts, nt=nt, n_tiles=nb_p * nt, zero_first=True,
                       avail0=0, hp_rows=hp_rows, hc_rows=hc_rows,
                       halo_blk=lambda s, hr: jnp.maximum(s * (ts // hr) - 1, 0))
    rows = x.shape[0]
    us_pool = jnp.zeros((nb_s, SUBLANE, width), F32).at[:, 0].set(u_pool[n_prompt:n_prompt + nb_s])
    us_conv = jnp.zeros((nb_s, SUBLANE, width), F32).at[:, 0].set(u_conv[n_prompt:n_prompt + nb_s])
    hs_pool = jnp.pad(state_pool, ((0, 0), (hp_rows - pool_buf, 0), (0, 0))).reshape(nb_s * hp_rows, width)
    hs_conv = jnp.pad(state_conv, ((0, 0), (hc_rows - conv_buf, 0), (0, 0))).reshape(nb_s * hc_rows, width)
    mixed_s = _convmix(us_pool.reshape(nb_s * SUBLANE, width), hs_pool, us_conv.reshape(nb_s * SUBLANE, width), hs_conv,
                       prm_mix, ts=SUBLANE, nt=1, n_tiles=nb_s, zero_first=False, avail0=pool_buf, hp_rows=hp_rows,
                       hc_rows=hc_rows, halo_blk=lambda s, hr: s)
    mixed_s = mixed_s.reshape(nb_s, SUBLANE, 2 * width)[:, 0]
    mixed = (mixed_p, jnp.pad(mixed_s, ((0, STREAM_TM - nb_s), (0, 0))), rows)
    x = _gmm(mixed, [(w_out[None], 0)], d, resid=x, tm=STREAM_TM, tn=1024, name="conv_out")
    x = _moe(x, moe_norm, router_w, wg, wu, wd)

    assert t_p >= conv_buf and t_p >= pool_buf
    new_pool_p = _seq_tails(u_pool, nb_p, t_p, pool_buf)
    new_conv_p = _seq_tails(u_conv, nb_p, t_p, conv_buf)
    new_pool_s = jnp.concatenate([state_pool, u_pool[n_prompt:n_prompt + nb_s][:, None]], axis=1)[:, 1:]
    new_conv_s = jnp.concatenate([state_conv, u_conv[n_prompt:n_prompt + nb_s][:, None]], axis=1)[:, 1:]
    return x, (new_pool_p, new_pool_s, new_conv_p, new_conv_s)


def kernel(x_prompt, x_sample, cache_nsa_cmp, cache_nsa_sel, state_nsa_win, cache_moba, state_pool, state_conv, page_table, attn_norm, w_attn_in, nsa_qk_norm, nsa_cmp_pe, nsa_cmp_w1, nsa_cmp_w2, moba_qk_norm, w_attn_out, ffn_norm, ffn_w_gate, ffn_w_up, ffn_w_down, conv_norm, w_conv_in, pool_w, pool_scale, conv_dw, conv_dw_b, conv_ln_g, conv_ln_b, conv_pw, w_conv_out, moe_norm, router_w, moe_w_gate, moe_w_up, moe_w_down):
    nb_p, t_p, d = x_prompt.shape
    nb_s = x_sample.shape[0]
    n_prompt = nb_p * t_p
    n_pad = 2 * SUBLANE
    assert x_sample.shape[1] == 1 and nb_s <= n_pad
    x = (x_prompt.reshape(n_prompt, d), jnp.pad(x_sample.reshape(nb_s, d), ((0, STREAM_TM - nb_s), (0, 0))),
         n_prompt + n_pad)
    li = 0
    prm_a = (attn_norm[li], w_attn_in[li], nsa_qk_norm[li], nsa_cmp_pe[li], nsa_cmp_w1[li], nsa_cmp_w2[li],
             moba_qk_norm[li], w_attn_out[li], ffn_norm[li], ffn_w_gate[li], ffn_w_up[li], ffn_w_down[li])
    x, attn_new = _attn_layer(x, n_prompt, nb_p, t_p, nb_s, li, prm_a,
                              (cache_nsa_cmp, cache_nsa_sel, state_nsa_win, cache_moba), page_table)
    prm_c = (conv_norm[li], w_conv_in[li], pool_w[li], pool_scale[li], conv_dw[li], conv_dw_b[li], conv_ln_g[li],
             conv_ln_b[li], conv_pw[li], w_conv_out[li], moe_norm[li], router_w[li], moe_w_gate[li], moe_w_up[li],
             moe_w_down[li])
    x, conv_new = _conv_layer(x, n_prompt, nb_p, t_p, nb_s, prm_c, (state_pool[li], state_conv[li]))
    cmp_p, cmp_s, sel_p, sel_s, win_p, win_s, moba_p, moba_s = attn_new
    pool_p, pool_s, conv_p, conv_s = conv_new
    y_p = x[:n_prompt].reshape(nb_p, t_p, d)
    y_s = x[n_prompt:n_prompt + nb_s].reshape(nb_s, 1, d)
    st = lambda a: a[None]
    return (y_p, y_s, st(cmp_p), st(cmp_s), st(sel_p), st(sel_s), st(win_p), st(win_s), st(moba_p), st(moba_s),
            st(pool_p), st(pool_s), st(conv_p), st(conv_s))
```

```python
import functools
import math

import jax
import jax.numpy as jnp
from jax import lax
from jax.experimental import pallas as pl
from jax.experimental.pallas import tpu as pltpu

F32 = jnp.float32
BF = jnp.bfloat16
NEG_INF = float("-inf")

HEAD_DIM = 128
NORM_EPS = 1e-6
NSA_HEADS = 8
NSA_KV = 2
NSA_HPG = NSA_HEADS // NSA_KV
NSA_CMP_LEN = 32
NSA_CMP_STRIDE = 16
NSA_CMP_HIDDEN = 2 * HEAD_DIM
NSA_SEL_BLOCK = 64
NSA_SEL_COUNT = 16
NSA_WINDOW = 512
NSA_SEL_FORCE = 1.0e4
MOBA_HEADS = 8
MOBA_BLOCK = 256
MOBA_TOPK = 3
POOL_WINDOWS = (2, 4, 8, 16)
CONV_WIDTH = 31
N_EXPERTS = 8
TOP_K = 2

LANE = 128
SUBLANE = 8
VMEM_LIMIT = 58 * 1024 * 1024
STREAM_TM = 512

CH_QA, CH_KC, CH_VC, CH_KS, CH_VS, CH_KW, CH_VW, CH_QB, CH_KB, CH_VB, CH_GATE = 0, 8, 10, 12, 14, 16, 18, 20, 28, 36, 44
ATTN_CHUNKS = 45

NT_DIMS = (((1,), (1,)), ((), ()))


def _cparams(sem):
    return pltpu.CompilerParams(dimension_semantics=sem, vmem_limit_bytes=VMEM_LIMIT)


def _split_bf16(a):
    hi = a.astype(BF)
    lo = (a - hi.astype(F32)).astype(BF)
    return hi, lo


def _dot3(a, b, dims=(((1,), (0,)), ((), ()))):
    ah, al = _split_bf16(a)
    bh, bl = _split_bf16(b)
    d = lambda x, y: lax.dot_general(x, y, dims, preferred_element_type=F32)
    return d(ah, bh) + d(ah, bl) + d(al, bh)


def _masked_softmax(s, mask, axis):
    s = jnp.where(mask, s, NEG_INF)
    m = jnp.max(s, axis=axis, keepdims=True)
    m = jnp.where(m == NEG_INF, 0.0, m)
    p = jnp.exp(s - m)
    d = jnp.sum(p, axis=axis, keepdims=True)
    return p / jnp.where(d > 0, d, 1.0)


def _split_stream(a, tm):
    if not isinstance(a, tuple):
        return [a], [lambda i: i], None, a.shape[0]
    main, tail, m = a
    n_full = main.shape[0] // tm
    assert n_full * tm == main.shape[0] and tail.shape[0] == tm and n_full * tm < m <= (n_full + 1) * tm
    return [main, tail], [lambda i: jnp.minimum(i, n_full - 1), lambda i: 0], n_full, m


def _pick_tile(refs, i, n_full):
    return refs[0][...] if n_full is None else jnp.where(i < n_full, refs[0][...], refs[1][...])


def _rmsnorm_body(*refs, n_full):
    x = _pick_tile(refs[:-2], pl.program_id(0), n_full)
    g_ref, o_ref = refs[-2:]
    ms = jnp.mean(x * x, axis=-1, keepdims=True)
    o_ref[...] = (x * lax.rsqrt(ms + NORM_EPS) * g_ref[...]).astype(o_ref.dtype)


def _rmsnorm(x, g, tm=256):
    arrs, maps, n_full, m = _split_stream(x, tm)
    d = arrs[0].shape[1]
    return pl.pallas_call(
        functools.partial(_rmsnorm_body, n_full=n_full),
        grid=(pl.cdiv(m, tm),),
        in_specs=[pl.BlockSpec((tm, d), lambda i, f=f: (f(i), 0)) for f in maps] + [pl.BlockSpec((1, d), lambda i: (0, 0))],
        out_specs=pl.BlockSpec((tm, d), lambda i: (i, 0)),
        out_shape=jax.ShapeDtypeStruct((m, d), BF),
        compiler_params=_cparams(("parallel",)),
        name="rmsnorm",
    )(*arrs, g.reshape(1, d))


def _gmm_body(texp_ref, tval_ref, *refs, n_x, x_full, n_w, cast_w, epi, n_r, r_full, tn, n_o):
    x_refs = refs[:n_x]
    w_refs = refs[n_x:n_x + n_w]
    pos = n_x + n_w
    if epi == "headnorm":
        gain_ref, flag_ref = refs[pos], refs[pos + 1]
        pos += 2
    resid_refs = refs[pos:pos + n_r]
    pos += n_r
    o_refs = refs[pos:pos + n_o]
    wb_refs = refs[pos + n_o:pos + n_o + n_w] if cast_w else w_refs

    i = pl.program_id(1)
    if cast_w:
        changed = jnp.logical_or(i == 0, texp_ref[i] != texp_ref[jnp.maximum(i - 1, 0)])

        @pl.when(changed)
        def _():
            for k in range(n_w):
                wb_refs[k][...] = w_refs[k][...].astype(BF)

    @pl.when(tval_ref[i] == 0)
    def _():
        for o in o_refs:
            o[...] = jnp.zeros(o.shape, o.dtype)

    @pl.when(tval_ref[i] > 0)
    def _():
        x = _pick_tile(x_refs, i, x_full)
        a = jnp.dot(x, wb_refs[0][...], preferred_element_type=F32)
        if epi == "swiglu":
            b = jnp.dot(x, wb_refs[1][...], preferred_element_type=F32)
            y = a * jax.nn.sigmoid(a) * b
        elif epi == "glu":
            b = jnp.dot(x, wb_refs[1][...], preferred_element_type=F32)
            y = a * jax.nn.sigmoid(b)
        else:
            y = a
        if n_r:
            y = y + _pick_tile(resid_refs, i, r_full)
        if epi == "headnorm":
            for c in range(tn // LANE):
                sl = slice(c * LANE, (c + 1) * LANE)
                yc = y[:, sl]
                r = lax.rsqrt(jnp.mean(yc * yc, axis=-1, keepdims=True) + NORM_EPS)
                f = flag_ref[:, sl]
                yn = yc * (f * r + (1.0 - f)) * gain_ref[:, sl]
                for o in o_refs:
                    o[:, sl] = yn.astype(o.dtype)
        else:
            for o in o_refs:
                o[...] = y.astype(o.dtype)


def _gmm(x, ws, n_out, *, epi="none", texp=None, tval=None, resid=None, gain=None, flag=None,
         out_dtype=F32, tm=256, tn=512, name="gmm"):
    x_arrs, x_maps, x_full, m = _split_stream(x, tm)
    kdim = x_arrs[0].shape[1]
    tn = min(tn, n_out)
    n_m = pl.cdiv(m, tm)
    n_n = n_out // tn
    assert n_n * tn == n_out
    if texp is None:
        texp = jnp.zeros((n_m,), jnp.int32)
        tval = jnp.ones((n_m,), jnp.int32)
    n_w = len(ws)
    cast_w = ws[0][0].dtype != BF
    in_specs = [pl.BlockSpec((tm, kdim), lambda j, i, te, tv, f=f: (f(i), 0)) for f in x_maps]
    args = list(x_arrs)
    for w, off in ws:
        assert off % tn == 0 and w.shape[1] == kdim and (w.dtype != BF) == cast_w
        ob = off // tn
        in_specs.append(pl.BlockSpec((None, kdim, tn), lambda j, i, te, tv, ob=ob: (te[i], 0, j + ob)))
        args.append(w)
    if epi == "headnorm":
        in_specs += [pl.BlockSpec((1, tn), lambda j, i, te, tv: (0, j))] * 2
        args += [gain, flag]
    n_r, r_full = 0, None
    if resid is not None:
        r_arrs, r_maps, r_full, r_m = _split_stream(resid, tm)
        assert r_m == m
        n_r = len(r_arrs)
        in_specs += [pl.BlockSpec((tm, tn), lambda j, i, te, tv, f=f: (f(i), j)) for f in r_maps]
        args += r_arrs
    dtypes = out_dtype if isinstance(out_dtype, tuple) else (out_dtype,)
    body = functools.partial(_gmm_body, n_x=len(x_arrs), x_full=x_full, n_w=n_w, cast_w=cast_w, epi=epi, n_r=n_r,
                             r_full=r_full, tn=tn, n_o=len(dtypes))
    outs = pl.pallas_call(
        body,
        grid_spec=pltpu.PrefetchScalarGridSpec(
            num_scalar_prefetch=2,
            grid=(n_n, n_m),
            in_specs=in_specs,
            out_specs=[pl.BlockSpec((tm, tn), lambda j, i, te, tv: (i, j)) for _ in dtypes],
            scratch_shapes=[pltpu.VMEM((kdim, tn), BF) for _ in range(n_w if cast_w else 0)],
        ),
        out_shape=[jax.ShapeDtypeStruct((m, n_out), dt) for dt in dtypes],
        compiler_params=_cparams(("arbitrary", "arbitrary")),
        name=name,
    )(texp, tval, *args)
    return outs if isinstance(out_dtype, tuple) else outs[0]


def _c1_body(x_ref, w_ref, o_ref):
    nsb = x_ref.shape[0] // NSA_CMP_STRIDE
    acc = jnp.zeros((nsb, 2 * NSA_CMP_HIDDEN), F32)
    for s in range(NSA_CMP_STRIDE):
        xs = x_ref[pl.ds(s, nsb, stride=NSA_CMP_STRIDE), :]
        acc = acc + jnp.dot(xs.astype(BF), w_ref[s], preferred_element_type=F32)
    o_ref[...] = acc


def _compress_c1_prompt(y, wcat, nb, t):
    nsub = t // NSA_CMP_STRIDE
    return pl.pallas_call(
        _c1_body,
        grid=(nb, 2, NSA_KV),
        in_specs=[pl.BlockSpec((t, LANE), lambda b, kv, g: (b, CH_KC + 2 * kv + g)),
                  pl.BlockSpec((None, NSA_CMP_STRIDE, HEAD_DIM, 2 * NSA_CMP_HIDDEN), lambda b, kv, g: (kv, 0, 0, 0))],
        out_specs=pl.BlockSpec((None, None, None, nsub, 2 * NSA_CMP_HIDDEN), lambda b, kv, g: (b, kv, g, 0, 0)),
        out_shape=jax.ShapeDtypeStruct((nb, 2, NSA_KV, nsub, 2 * NSA_CMP_HIDDEN), F32),
        compiler_params=_cparams(("parallel", "parallel", "parallel")),
        name="cmp_c1_prompt",
    )(y, wcat)


C1_PAGES = 16


def _c1_sample_body(pt_ref, *refs, n_in, page):
    x_refs = refs[:n_in]
    w_ref, o_ref, scr = refs[n_in:n_in + 3]
    nsb = page // NSA_CMP_STRIDE
    rows = n_in * nsb
    for kv in range(2):
        acc = jnp.zeros((rows * NSA_KV, 2 * NSA_CMP_HIDDEN), F32)

        def rows_at(s):
            parts = [xr[pl.ds(s, nsb, stride=NSA_CMP_STRIDE), kv, :, :].reshape(nsb * NSA_KV, HEAD_DIM) for xr in x_refs]
            return (parts[0] if n_in == 1 else jnp.concatenate(parts, axis=0)).astype(BF)

        for s in range(0, NSA_CMP_STRIDE, 2):
            xs = jnp.concatenate([rows_at(s), rows_at(s + 1)], axis=1)
            w = w_ref[kv, s:s + 2].reshape(2 * HEAD_DIM, 2 * NSA_CMP_HIDDEN)
            acc = acc + jnp.dot(xs, w, preferred_element_type=F32)
        for c in range(scr.shape[0]):
            scr[c] = acc[:, c * LANE:(c + 1) * LANE]
        for g in range(NSA_KV):
            for c in range(scr.shape[0]):
                o_ref[kv, g, :, c * LANE:(c + 1) * LANE] = scr[c, pl.ds(g, rows, stride=NSA_KV), :]


def _compress_c1_sample(cache, li, page_table, wcat):
    nb, n_pages = page_table.shape
    page = cache.shape[2]
    npp = math.gcd(C1_PAGES, n_pages)
    nsb = page // NSA_CMP_STRIDE
    nsub = n_pages * nsb
    body = functools.partial(_c1_sample_body, n_in=npp, page=page)
    in_specs = [
        pl.BlockSpec((None, None, page, 2, NSA_KV, HEAD_DIM), lambda b, j, pt, i=i: (li, pt[b, j * npp + i], 0, 0, 0, 0))
        for i in range(npp)
    ]
    in_specs.append(pl.BlockSpec(wcat.shape, lambda b, j, pt: (0, 0, 0, 0)))
    return pl.pallas_call(
        body,
        grid_spec=pltpu.PrefetchScalarGridSpec(
            num_scalar_prefetch=1,
            grid=(nb, n_pages // npp),
            in_specs=in_specs,
            out_specs=pl.BlockSpec((None, 2, NSA_KV, npp * nsb, 2 * NSA_CMP_HIDDEN), lambda b, j, pt: (b, 0, 0, j, 0)),
            scratch_shapes=[pltpu.VMEM((2 * NSA_CMP_HIDDEN // LANE, npp * nsb * NSA_KV, LANE), F32)],
        ),
        out_shape=jax.ShapeDtypeStruct((nb, 2, NSA_KV, nsub, 2 * NSA_CMP_HIDDEN), F32),
        compiler_params=_cparams(("parallel", "parallel")),
        name="cmp_c1_sample",
    )(page_table, *([cache] * npp), wcat)


def _c2_body(p_ref, pe_ref, wcat_ref, w2_ref, g_ref, o_ref, *, nsub, n_c):
    kv = pl.program_id(1)
    pep = jnp.zeros((SUBLANE, 2 * NSA_CMP_HIDDEN), F32)
    for s in range(NSA_CMP_STRIDE):
        pep = pep + jnp.dot(pe_ref[s], wcat_ref[s], preferred_element_type=F32)
    bias = pep[0:1, :NSA_CMP_HIDDEN] + pep[1:2, NSA_CMP_HIDDEN:]
    nxt = pltpu.roll(p_ref[:, NSA_CMP_HIDDEN:], nsub - 1, 0)
    h = p_ref[:, :NSA_CMP_HIDDEN] + nxt + bias
    h = h * jax.nn.sigmoid(h)
    o = jnp.dot(h.astype(BF), w2_ref[...], preferred_element_type=F32)
    r = lax.rsqrt(jnp.mean(o * o, axis=-1, keepdims=True) + NORM_EPS)
    o = jnp.where(kv == 0, o * r * g_ref[...], o)
    row = lax.broadcasted_iota(jnp.int32, (nsub, 1), 0)
    o_ref[...] = jnp.where(row < n_c, o, 0.0)


def _compress_c2(p, pe_rows, wcat, w2, gk):
    nb, _, _, nsub, _ = p.shape
    n_c = nsub - NSA_CMP_LEN // NSA_CMP_STRIDE + 1
    body = functools.partial(_c2_body, nsub=nsub, n_c=n_c)
    return pl.pallas_call(
        body,
        grid=(nb, 2, NSA_KV),
        in_specs=[pl.BlockSpec((None, None, None, nsub, 2 * NSA_CMP_HIDDEN), lambda b, kv, g: (b, kv, g, 0, 0)),
                  pl.BlockSpec((None, NSA_CMP_STRIDE, SUBLANE, HEAD_DIM), lambda b, kv, g: (kv, 0, 0, 0)),
                  pl.BlockSpec((None, NSA_CMP_STRIDE, HEAD_DIM, 2 * NSA_CMP_HIDDEN), lambda b, kv, g: (kv, 0, 0, 0)),
                  pl.BlockSpec((None, NSA_CMP_HIDDEN, HEAD_DIM), lambda b, kv, g: (kv, 0, 0)),
                  pl.BlockSpec((1, HEAD_DIM), lambda b, kv, g: (0, 0))],
        out_specs=pl.BlockSpec((None, None, None, nsub, HEAD_DIM), lambda b, kv, g: (b, kv, g, 0, 0)),
        out_shape=jax.ShapeDtypeStruct((nb, 2, NSA_KV, nsub, HEAD_DIM), F32),
        compiler_params=_cparams(("parallel", "parallel", "parallel")),
        name="cmp_c2",
    )(p, pe_rows, wcat, w2, gk)


def _rank_select(sc_ref, n_iter, k):
    score = sc_ref[...]
    brow = lax.broadcasted_iota(jnp.int32, score.shape, 0)

    def body(j, rank):
        r = sc_ref[pl.ds(j, 1), :]
        beats = jnp.logical_or(r > score, jnp.logical_and(r == score, j < brow))
        return rank + jnp.where(beats, 1.0, 0.0)

    rank = lax.fori_loop(0, n_iter, body, jnp.zeros(score.shape, F32))
    return rank < k


def _cattn_body(q_ref, kc_ref, vc_ref, oc_ref, sel_ref, sc_ref, *, tq, ncp, n_c, ns, nsp, pos0, ksel):
    t0 = pos0 + pl.program_id(2) * tq
    scale = HEAD_DIM ** -0.5
    kc = kc_ref[...].astype(BF)
    vc = vc_ref[...].astype(BF)
    trow = t0 + lax.broadcasted_iota(jnp.int32, (tq, 1), 0)
    ncol = lax.broadcasted_iota(jnp.int32, (1, ncp), 1)
    valid = jnp.logical_and(ncol * NSA_CMP_STRIDE + (NSA_CMP_LEN - 1) <= trow, ncol < n_c)
    tcol = t0 + lax.broadcasted_iota(jnp.int32, (1, tq), 1)
    nrow = lax.broadcasted_iota(jnp.int32, (ncp, 1), 0)
    valid_t = jnp.logical_and(nrow * NSA_CMP_STRIDE + (NSA_CMP_LEN - 1) <= tcol, nrow < n_c)
    psum_t = jnp.zeros((ncp, tq), F32)
    for z in range(NSA_HPG):
        sl = slice(z * HEAD_DIM, (z + 1) * HEAD_DIM)
        q = (q_ref[:, sl] * scale).astype(BF)
        s = lax.dot_general(q, kc, NT_DIMS, preferred_element_type=F32)
        p = _masked_softmax(s, valid, -1)
        oc_ref[:, sl] = jnp.dot(p.astype(BF), vc, preferred_element_type=F32)
        s_t = lax.dot_general(kc, q, NT_DIMS, preferred_element_type=F32)
        psum_t = psum_t + _masked_softmax(s_t, valid_t, 0)
    r = NSA_SEL_BLOCK // NSA_CMP_STRIDE
    brow = lax.broadcasted_iota(jnp.int32, (nsp, 1), 0)
    lo = r * brow - 1
    inside = jnp.logical_and(ncol >= lo, ncol <= lo + r)
    edge = jnp.logical_or(ncol == lo, ncol == lo + r)
    m_t = jnp.where(inside, jnp.where(edge, 0.5, 1.0), 0.0).astype(BF)
    hi = psum_t.astype(BF)
    mid = (psum_t - hi.astype(F32)).astype(BF)
    low = (psum_t - hi.astype(F32) - mid.astype(F32)).astype(BF)
    imp_t = (jnp.dot(m_t, hi, preferred_element_type=F32) + jnp.dot(m_t, mid, preferred_element_type=F32)
             + jnp.dot(m_t, low, preferred_element_type=F32))
    own = tcol // NSA_SEL_BLOCK
    past = jnp.logical_and(brow < own, brow < ns)
    forced = jnp.logical_or(brow == 0, brow == own - 1)
    sc_ref[...] = jnp.where(past, jnp.where(forced, NSA_SEL_FORCE, imp_t), NEG_INF)
    chosen = jnp.logical_and(past, _rank_select(sc_ref, ns, ksel))
    sel = jnp.where(jnp.logical_or(chosen, brow == own), 1.0, 0.0)
    rows = sel_ref.shape[0]
    if rows > nsp:
        sel_ref[...] = jnp.zeros(sel_ref.shape, F32)
    sel_ref[0:nsp, :] = sel


def _cattn(q_arr, q_row_blk0, q_col_blk0, kvc, nb, tlen, tq, length, pos0, out_rows):
    nq = tlen // tq
    ncp = kvc.shape[3]
    n_c = ncp - NSA_CMP_LEN // NSA_CMP_STRIDE + 1
    ns = length // NSA_SEL_BLOCK
    ksel = min(NSA_SEL_COUNT - 1, ns)
    nsp = -(-ns // SUBLANE) * SUBLANE
    sel_rows = max(nsp, LANE)
    body = functools.partial(_cattn_body, tq=tq, ncp=ncp, n_c=n_c, ns=ns, nsp=nsp, pos0=pos0, ksel=ksel)
    gw = NSA_HPG * HEAD_DIM
    return pl.pallas_call(
        body,
        grid=(nb, NSA_KV, nq),
        in_specs=[pl.BlockSpec((tq, gw), lambda b, g, qi: (q_row_blk0 + b * nq + qi, q_col_blk0 + g)),
                  pl.BlockSpec((None, None, None, ncp, HEAD_DIM), lambda b, g, qi: (b, 0, g, 0, 0)),
                  pl.BlockSpec((None, None, None, ncp, HEAD_DIM), lambda b, g, qi: (b, 1, g, 0, 0))],
        out_specs=[pl.BlockSpec((tq, gw), lambda b, g, qi: (b * nq + qi, g)),
                   pl.BlockSpec((None, None, sel_rows, tq), lambda b, g, qi: (b, g, 0, qi))],
        out_shape=[jax.ShapeDtypeStruct((out_rows, NSA_KV * gw), F32),
                   jax.ShapeDtypeStruct((nb, NSA_KV, sel_rows, tlen), F32)],
        scratch_shapes=[pltpu.VMEM((nsp, tq), F32)],
        compiler_params=_cparams(("parallel", "parallel", "parallel")),
        name="nsa_cmp_attn",
    )(q_arr, kvc, kvc)


def _cattn_row_body(q_ref, kc_ref, vc_ref, oc_ref, sel_ref, *, tq, ncp, n_c, ns, nsl, pos0, ksel):
    scale = HEAD_DIM ** -0.5
    kc = kc_ref[...].astype(BF)
    vc = vc_ref[...].astype(BF)
    trow = pos0 + lax.broadcasted_iota(jnp.int32, (tq, 1), 0)
    ncol = lax.broadcasted_iota(jnp.int32, (1, ncp), 1)
    valid = jnp.logical_and(ncol * NSA_CMP_STRIDE + (NSA_CMP_LEN - 1) <= trow, ncol < n_c)
    psum = jnp.zeros((tq, ncp), F32)
    for z in range(NSA_HPG):
        sl = slice(z * HEAD_DIM, (z + 1) * HEAD_DIM)
        q = (q_ref[:, sl] * scale).astype(BF)
        p = _masked_softmax(lax.dot_general(q, kc, NT_DIMS, preferred_element_type=F32), valid, -1)
        oc_ref[:, sl] = jnp.dot(p.astype(BF), vc, preferred_element_type=F32)
        psum = psum + p
    r = NSA_SEL_BLOCK // NSA_CMP_STRIDE
    nrow = lax.broadcasted_iota(jnp.int32, (ncp, 1), 0)
    bcol = lax.broadcasted_iota(jnp.int32, (1, nsl), 1)
    lo = r * bcol - 1
    inside = jnp.logical_and(nrow >= lo, nrow <= lo + r)
    edge = jnp.logical_or(nrow == lo, nrow == lo + r)
    m = jnp.where(inside, jnp.where(edge, 0.5, 1.0), 0.0).astype(BF)
    hi = psum.astype(BF)
    mid = (psum - hi.astype(F32)).astype(BF)
    low = (psum - hi.astype(F32) - mid.astype(F32)).astype(BF)
    imp = (jnp.dot(hi, m, preferred_element_type=F32) + jnp.dot(mid, m, preferred_element_type=F32)
           + jnp.dot(low, m, preferred_element_type=F32))
    own = trow // NSA_SEL_BLOCK
    past = jnp.logical_and(bcol < own, bcol < ns)
    forced = jnp.logical_or(bcol == 0, bcol == own - 1)
    score = jnp.where(past, jnp.where(forced, NSA_SEL_FORCE, imp), NEG_INF)
    rank = jnp.zeros((tq, nsl), F32)
    for j in range(ns):
        cj = score[:, j:j + 1]
        beats = jnp.logical_or(cj > score, jnp.logical_and(cj == score, j < bcol))
        rank = rank + jnp.where(beats, 1.0, 0.0)
    chosen = jnp.logical_and(past, rank < ksel)
    sel_ref[...] = jnp.where(jnp.logical_or(chosen, bcol == own), 1.0, 0.0)


def _cattn_sample(q_pad, kvc, nb, length, pos0):
    tq = SUBLANE
    ncp = kvc.shape[3]
    n_c = ncp - NSA_CMP_LEN // NSA_CMP_STRIDE + 1
    ns = length // NSA_SEL_BLOCK
    ksel = min(NSA_SEL_COUNT - 1, ns)
    nsl = -(-ns // LANE) * LANE
    body = functools.partial(_cattn_row_body, tq=tq, ncp=ncp, n_c=n_c, ns=ns, nsl=nsl, pos0=pos0, ksel=ksel)
    gw = NSA_HPG * HEAD_DIM
    return pl.pallas_call(
        body,
        grid=(nb, NSA_KV),
        in_specs=[pl.BlockSpec((tq, gw), lambda b, g: (b, g)),
                  pl.BlockSpec((None, None, None, ncp, HEAD_DIM), lambda b, g: (b, 0, g, 0, 0)),
                  pl.BlockSpec((None, None, None, ncp, HEAD_DIM), lambda b, g: (b, 1, g, 0, 0))],
        out_specs=[pl.BlockSpec((tq, gw), lambda b, g: (b, g)),
                   pl.BlockSpec((None, None, tq, nsl), lambda b, g: (b, g, 0, 0))],
        out_shape=[jax.ShapeDtypeStruct((nb * tq, NSA_KV * gw), F32),
                   jax.ShapeDtypeStruct((nb, NSA_KV, tq, nsl), F32)],
        compiler_params=_cparams(("parallel", "parallel")),
        name="nsa_cmp_attn_sample",
    )(q_pad, kvc, kvc)


def _gate_body(q_ref, k_ref, sel_ref, mean_ref, sc_ref, *, tq, nb, nbp, pos0, from_means):
    qi = pl.program_id(2)
    if from_means:
        means = k_ref[...]
    else:
        @pl.when(qi == 0)
        def _():
            mean_ref[...] = jnp.zeros(mean_ref.shape, F32)
            for j in range(nb):
                blk = k_ref[j * MOBA_BLOCK:(j + 1) * MOBA_BLOCK, :]
                mean_ref[j:j + 1, :] = jnp.sum(blk, axis=0, keepdims=True) * (1.0 / MOBA_BLOCK)

        means = mean_ref[...]
    g_t = _dot3(means, q_ref[...], NT_DIMS)
    tcol = pos0 + qi * tq + lax.broadcasted_iota(jnp.int32, (1, tq), 1)
    own = tcol // MOBA_BLOCK
    brow = lax.broadcasted_iota(jnp.int32, (nbp, 1), 0)
    past = jnp.logical_and(brow < own, brow < nb)
    sc_ref[...] = jnp.where(past, g_t, NEG_INF)
    chosen = jnp.logical_and(past, _rank_select(sc_ref, nb, min(MOBA_TOPK, nb)))
    sel = jnp.where(jnp.logical_or(chosen, brow == own), 1.0, 0.0)
    if sel_ref.shape[0] > nbp:
        sel_ref[...] = jnp.zeros(sel_ref.shape, F32)
    sel_ref[0:nbp, :] = sel


def _moba_gate_prompt(y, nb, t, tq):
    nq = t // tq
    n_full = t // MOBA_BLOCK
    nbp = -(-n_full // SUBLANE) * SUBLANE
    body = functools.partial(_gate_body, tq=tq, nb=n_full, nbp=nbp, pos0=0, from_means=False)
    return pl.pallas_call(
        body,
        grid=(nb, MOBA_HEADS, nq),
        in_specs=[pl.BlockSpec((tq, LANE), lambda b, h, qi: (b * nq + qi, CH_QB + h)),
                  pl.BlockSpec((t, LANE), lambda b, h, qi: (b, CH_KB + h))],
        out_specs=pl.BlockSpec((None, None, LANE, tq), lambda b, h, qi: (b, h, 0, qi)),
        out_shape=jax.ShapeDtypeStruct((nb, MOBA_HEADS, LANE, t), F32),
        scratch_shapes=[pltpu.VMEM((nbp, HEAD_DIM), F32), pltpu.VMEM((nbp, tq), F32)],
        compiler_params=_cparams(("parallel", "parallel", "arbitrary")),
        name="moba_gate_prompt",
    )(y, y)


def _moba_gate_sample(q_pad, means, pos0):
    nb, _, n_full, _ = means.shape
    tq = SUBLANE
    body = functools.partial(_gate_body, tq=tq, nb=n_full, nbp=n_full, pos0=pos0, from_means=True)
    return pl.pallas_call(
        body,
        grid=(nb, MOBA_HEADS, 1),
        in_specs=[pl.BlockSpec((tq, LANE), lambda b, h, qi: (b, h)),
                  pl.BlockSpec((None, None, n_full, HEAD_DIM), lambda b, h, qi: (b, h, 0, 0))],
        out_specs=pl.BlockSpec((None, None, max(n_full, LANE), tq), lambda b, h, qi: (b, h, 0, 0)),
        out_shape=jax.ShapeDtypeStruct((nb, MOBA_HEADS, max(n_full, LANE), tq), F32),
        scratch_shapes=[pltpu.VMEM((SUBLANE, HEAD_DIM), F32), pltpu.VMEM((n_full, tq), F32)],
        compiler_params=_cparams(("parallel", "parallel", "arbitrary")),
        name="moba_gate_sample",
    )(q_pad, means)


MEANS_BLOCKS = 4


def _means_body(pt_ref, *refs, ppb, bps):
    c_refs = refs[:bps * ppb]
    o_ref = refs[bps * ppb]
    j = pl.program_id(1)
    for q in range(bps):
        acc = jnp.sum(c_refs[q * ppb][...], axis=0)
        for i in range(1, ppb):
            acc = acc + jnp.sum(c_refs[q * ppb + i][...], axis=0)
        m = acc * (1.0 / MOBA_BLOCK)
        for h in range(MOBA_HEADS):
            o_ref[h, pl.ds(j * bps + q, 1), :] = m[h:h + 1, :]


def _moba_means_sample(cache, li, page_table):
    nb, n_pages = page_table.shape
    page = cache.shape[2]
    ppb = MOBA_BLOCK // page
    n_full = n_pages // ppb
    bps = math.gcd(MEANS_BLOCKS, n_full)
    body = functools.partial(_means_body, ppb=ppb, bps=bps)
    in_specs = [
        pl.BlockSpec((None, None, page, None, MOBA_HEADS, HEAD_DIM),
                     lambda b, j, pt, i=i: (li, pt[b, j * bps * ppb + i], 0, 0, 0, 0))
        for i in range(bps * ppb)
    ]
    return pl.pallas_call(
        body,
        grid_spec=pltpu.PrefetchScalarGridSpec(
            num_scalar_prefetch=1,
            grid=(nb, n_full // bps),
            in_specs=in_specs,
            out_specs=pl.BlockSpec((None, MOBA_HEADS, n_full, HEAD_DIM), lambda b, j, pt: (b, 0, 0, 0)),
        ),
        out_shape=jax.ShapeDtypeStruct((nb, MOBA_HEADS, n_full, HEAD_DIM), F32),
        compiler_params=_cparams(("parallel", "arbitrary")),
        name="moba_means_sample",
    )(page_table, *([cache] * (bps * ppb)))


FLASH_UNIT_ROWS = 256
LOG2_E = 1.4426950408889634
MASK_BIG = 2.0 ** 60


def _flash_body(*refs, tq, tk, nh, blk, window, masked, ur):
    q_ref, k_ref, v_ref = refs[:3]
    pos = 3
    if masked:
        sel_ref = refs[pos]
        pos += 1
    o_ref, qs_ref, kt_ref = refs[pos:pos + 3]
    pos += 1
    units = [(z, r0) for z in range(nh) for r0 in range(0, tq, ur)]
    stat = refs[pos + 2:pos + 2 + 2 * len(units)]
    m_refs, acc_refs = stat[0::2], stat[1::2]
    qi = pl.program_id(2)
    t0 = qi * tq
    scale = HEAD_DIM ** -0.5 * LOG2_E
    if masked:
        bias = ((jnp.transpose(sel_ref[...]) - 1.0) * MASK_BIG).astype(BF)
    for z in range(nh):
        qz = (q_ref[:, z * HEAD_DIM:(z + 1) * HEAD_DIM] * scale).astype(BF)
        qs_ref[z] = jnp.concatenate([qz, bias], axis=1) if masked else qz
    for u in range(len(units)):
        m_refs[u][...] = jnp.full(m_refs[u].shape, NEG_INF, F32)
        acc_refs[u][...] = jnp.zeros(acc_refs[u].shape, F32)
    trow = t0 + lax.broadcasted_iota(jnp.int32, (tq, 1), 0)

    @pl.when(qi == 0)
    def _():
        for c in range(kt_ref.shape[0]):
            kc = k_ref[c * tk:(c + 1) * tk, :]
            if masked:
                prow = c * tk + lax.broadcasted_iota(jnp.int32, (tk, 1), 0)
                jcol = lax.broadcasted_iota(jnp.int32, (1, LANE), 1)
                kc = jnp.concatenate([kc, jnp.where(prow // blk == jcol, 1.0, 0.0).astype(BF)], axis=1)
            kt_ref[c] = kc.T

    def process(c, edge):
        ks = pl.multiple_of(c * tk, tk)
        kt = kt_ref[c]
        vc = jnp.concatenate([v_ref[pl.ds(ks, tk), :], jnp.ones((tk, LANE), BF)], axis=1)
        if edge:
            pcol = ks + lax.broadcasted_iota(jnp.int32, (1, tk), 1)
            valid = pcol <= trow
            if window:
                valid = jnp.logical_and(valid, pcol > trow - window)
        for u, (z, r0) in enumerate(units):
            s = jnp.dot(qs_ref[z, r0:r0 + ur, :], kt, preferred_element_type=F32)
            m_old = m_refs[u][...]
            if edge:
                s = jnp.where(valid[r0:r0 + ur], s, NEG_INF)
                m_new = jnp.maximum(m_old, jnp.max(s, axis=-1, keepdims=True))
                m_use = jnp.where(m_new == NEG_INF, 0.0, m_new)
            else:
                m_new = jnp.maximum(m_old, jnp.max(s, axis=-1, keepdims=True))
                m_use = m_new
            alpha = jnp.exp2(m_old - m_use)
            p = jnp.exp2(s - m_use)
            acc_refs[u][...] = alpha * acc_refs[u][...] + jnp.dot(p.astype(BF), vc, preferred_element_type=F32)
            m_refs[u][...] = m_new

    if masked:
        def pair(c2, carry):
            process(2 * c2, False)
            process(2 * c2 + 1, False)
            return carry

        lax.fori_loop(0, qi // 2, pair, 0)

        @pl.when(qi % 2 == 1)
        def _():
            process(qi - 1, False)

        process(qi, True)
    else:
        c_hi = (t0 + tq + tk - 1) // tk
        c_lo = jnp.maximum(t0 - window + 1, 0) // tk
        lax.fori_loop(c_lo, c_hi, lambda c, carry: (process(c, True), carry)[1], 0)
    for u, (z, r0) in enumerate(units):
        l = acc_refs[u][:, HEAD_DIM:HEAD_DIM + 1]
        o_ref[r0:r0 + ur, z * HEAD_DIM:(z + 1) * HEAD_DIM] = acc_refs[u][:, :HEAD_DIM] / jnp.where(l > 0, l, 1.0)


def _flash(y, y16, sel, nb, t, *, q_ch, k_ch, v_ch, kvh, nh, blk, window, out_rows, tq=512, tk=512):
    tq = min(tq, t)
    tk = min(tk, t)
    nq = t // tq
    masked = sel is not None
    gw = nh * HEAD_DIM
    assert q_ch % nh == 0 and (tq == tk or not masked)
    in_specs = [pl.BlockSpec((tq, gw), lambda b, h, qi: (b * nq + qi, q_ch // nh + h)),
                pl.BlockSpec((t, LANE), lambda b, h, qi: (b, k_ch + h)),
                pl.BlockSpec((t, LANE), lambda b, h, qi: (b, v_ch + h))]
    args = [y, y16, y16]
    if masked:
        in_specs.append(pl.BlockSpec((None, None, LANE, tq), lambda b, h, qi: (b, h, 0, qi)))
        args.append(sel)
    ur = min(FLASH_UNIT_ROWS, tq)
    n_units = nh * (tq // ur)
    kd = (2 if masked else 1) * HEAD_DIM
    body = functools.partial(_flash_body, tq=tq, tk=tk, nh=nh, blk=blk, window=window, masked=masked, ur=ur)
    unit_scratch = [pltpu.VMEM((ur, 1), F32), pltpu.VMEM((ur, 2 * HEAD_DIM), F32)]
    return pl.pallas_call(
        body,
        grid=(nb, kvh, nq),
        in_specs=in_specs,
        out_specs=pl.BlockSpec((tq, gw), lambda b, h, qi: (b * nq + qi, h)),
        out_shape=jax.ShapeDtypeStruct((out_rows, kvh * gw), F32),
        scratch_shapes=[pltpu.VMEM((nh, tq, kd), BF), pltpu.VMEM((t // tk, kd, tk), BF)] + unit_scratch * n_units,
        compiler_params=_cparams(("parallel", "parallel", "arbitrary")),
        name="flash_" + ("win" if window else "blk%d" % blk),
    )(*args)


def _dec_body(tbl_ref, vld_ref, q_ref, *refs, nh, rows, kvh, nblk, r_min):
    k_refs = refs[:nblk]
    v_refs = refs[nblk:2 * nblk]
    kn_ref, vn_ref, o_ref = refs[2 * nblk:2 * nblk + 3]
    b = pl.program_id(0)
    h = pl.program_id(1)
    scale = HEAD_DIM ** -0.5
    qrow = q_ref[pl.ds(b, 1), :] * scale
    zrow = lax.broadcasted_iota(jnp.int32, (SUBLANE, 1), 0)
    qm = jnp.zeros((SUBLANE, HEAD_DIM), F32)
    for z in range(nh):
        qm = jnp.where(zrow == z, qrow[:, z * HEAD_DIM:(z + 1) * HEAD_DIM], qm)
    qb = qm.astype(BF)
    col = lax.broadcasted_iota(jnp.int32, (1, rows * kvh), 1)
    valid = jnp.logical_and(col % kvh == h, col // kvh >= r_min)
    kn = kn_ref[pl.ds(b, 1), :]
    vn = vn_ref[pl.ds(b, 1), :]
    s_new = jnp.sum(qm * kn, axis=-1, keepdims=True)
    scores = []
    m = s_new
    for i in range(nblk):
        kk = k_refs[i][...].reshape(rows * kvh, HEAD_DIM).astype(BF)
        s = lax.dot_general(qb, kk, NT_DIMS, preferred_element_type=F32)
        s = jnp.where(jnp.logical_and(valid, vld_ref[b, h, i] > 0), s, NEG_INF)
        scores.append(s)
        m = jnp.maximum(m, jnp.max(s, axis=-1, keepdims=True))
    p_new = jnp.exp(s_new - m)
    l = p_new
    acc = p_new * vn
    for i in range(nblk):
        p = jnp.exp(scores[i] - m)
        l = l + jnp.sum(p, axis=-1, keepdims=True)
        vv = v_refs[i][...].reshape(rows * kvh, HEAD_DIM).astype(BF)
        acc = acc + jnp.dot(p.astype(BF), vv, preferred_element_type=F32)
    o_ref[...] = acc / l


def _decode_attn(ys, cache, li_fixed, tbl, vld, *, q_ch, kn_ch, vn_ch, nh, rows, kvh, r_min, name):
    nb, _, nblk = tbl.shape
    page = cache.shape[2]
    bpp = page // rows
    gw = nh * HEAD_DIM
    body = functools.partial(_dec_body, nh=nh, rows=rows, kvh=kvh, nblk=nblk, r_min=r_min)

    def kv_spec(kv, i):
        def imap(b, h, tb, vl):
            t = tb[b, h, i]
            return (li_fixed, t // bpp, t % bpp, kv, 0, 0)

        return pl.BlockSpec((None, None, rows, None, kvh, HEAD_DIM), imap)

    nrow = ys.shape[0]
    return pl.pallas_call(
        body,
        grid_spec=pltpu.PrefetchScalarGridSpec(
            num_scalar_prefetch=2,
            grid=(nb, kvh),
            in_specs=[pl.BlockSpec((nrow, gw), lambda b, h, tb, vl: (0, q_ch // nh + h))]
            + [kv_spec(0, i) for i in range(nblk)] + [kv_spec(1, i) for i in range(nblk)]
            + [pl.BlockSpec((nrow, LANE), lambda b, h, tb, vl: (0, kn_ch + h)),
               pl.BlockSpec((nrow, LANE), lambda b, h, tb, vl: (0, vn_ch + h))],
            out_specs=pl.BlockSpec((None, None, SUBLANE, HEAD_DIM), lambda b, h, tb, vl: (b, h, 0, 0)),
        ),
        out_shape=jax.ShapeDtypeStruct((nb, kvh, SUBLANE, HEAD_DIM), F32),
        compiler_params=_cparams(("parallel", "parallel")),
        name=name,
    )(tbl, vld, ys, *([cache] * (2 * nblk)), ys, ys)


def _mix_body(oc_ref, os_ref, ow_ref, ob_ref, soc_ref, sos_ref, sow_ref, sob_ref, gate_ref, o_ref, *, n_p_tiles, n_s):
    i = pl.program_id(0)
    w = NSA_HEADS * HEAD_DIM

    def emit(oc, os_, ow, ob, rows):
        g = jax.nn.sigmoid(gate_ref[0:rows, :])
        for h in range(NSA_HEADS):
            sl = slice(h * HEAD_DIM, (h + 1) * HEAD_DIM)
            o = (g[:, 3 * h:3 * h + 1] * oc[:, sl] + g[:, 3 * h + 1:3 * h + 2] * os_[:, sl]
                 + g[:, 3 * h + 2:3 * h + 3] * ow[:, sl])
            o_ref[0:rows, sl] = o.astype(o_ref.dtype)
        o_ref[0:rows, w:] = ob[...].astype(o_ref.dtype)

    @pl.when(i < n_p_tiles)
    def _():
        emit(oc_ref, os_ref, ow_ref, ob_ref, o_ref.shape[0])

    @pl.when(i >= n_p_tiles)
    def _():
        emit(soc_ref, sos_ref, sow_ref, sob_ref, n_s)


def _mix(prompt_parts, sample_parts, y, tm=256):
    n_prompt = prompt_parts[0].shape[0]
    n_s = sample_parts[0].shape[0]
    m = y.shape[0]
    tm = min(tm, n_prompt)
    assert n_prompt % tm == 0 and m == n_prompt + n_s and n_s <= tm
    n_p_tiles = n_prompt // tm
    wa = NSA_HEADS * HEAD_DIM
    wb = MOBA_HEADS * HEAD_DIM
    pmap = lambda i: (jnp.minimum(i, n_p_tiles - 1), 0)
    body = functools.partial(_mix_body, n_p_tiles=n_p_tiles, n_s=n_s)
    return pl.pallas_call(
        body,
        grid=(n_p_tiles + 1,),
        in_specs=[pl.BlockSpec((tm, wa), pmap)] * 3 + [pl.BlockSpec((tm, wb), pmap)]
        + [pl.BlockSpec((n_s, wa), lambda i: (0, 0))] * 3 + [pl.BlockSpec((n_s, wb), lambda i: (0, 0))]
        + [pl.BlockSpec((tm, LANE), lambda i: (i, CH_GATE))],
        out_specs=pl.BlockSpec((tm, wa + wb), lambda i: (i, 0)),
        out_shape=jax.ShapeDtypeStruct((m, wa + wb), BF),
        compiler_params=_cparams(("parallel",)),
        name="attn_mix",
    )(*prompt_parts, *sample_parts, y)


def _convmix_body(up_ref, hp_ref, uc_ref, hc_ref, pw_ref, ps_ref, dw_ref, dwb_ref, lng_ref, lnb_ref, pww_ref,
                  o_ref, zp_ref, zs_ref, cb_ref, *, ts, nt, zero_first, avail0, rc):
    s = pl.program_id(0)
    hp_rows = hp_ref.shape[0]
    hc_rows = hc_ref.shape[0]
    width = up_ref.shape[1]
    if zero_first:
        keep = jnp.where(s % nt == 0, 0.0, 1.0)
        zp_ref[0:hp_rows, :] = hp_ref[...] * keep
        zs_ref[0, 0:hc_rows, :] = hc_ref[...] * keep
        avail = (s % nt) * ts + avail0
    else:
        zp_ref[0:hp_rows, :] = hp_ref[...]
        zs_ref[0, 0:hc_rows, :] = hc_ref[...]
        avail = avail0
    zp_ref[hp_rows:, :] = up_ref[...]
    zs_ref[0, hc_rows:, :] = uc_ref[...]
    span = hc_rows + ts - SUBLANE
    for k in range(1, SUBLANE):
        zs_ref[k, 0:span, :] = zs_ref[0, k:k + span, :]
    gwidth = width // len(POOL_WINDOWS)
    t_idx = lax.broadcasted_iota(jnp.int32, (ts, 1), 0) + avail + 1
    for gi, w in enumerate(POOL_WINDOWS):
        sl = slice(gi * gwidth, (gi + 1) * gwidth)
        cur = zp_ref[hp_rows:hp_rows + ts, sl]
        acc = cur
        for jj in range(1, w):
            acc = acc + zp_ref[hp_rows - jj:hp_rows - jj + ts, sl]
        cnt = jnp.minimum(t_idx, w).astype(F32)
        yg = acc / cnt - cur
        og = jnp.dot(yg.astype(BF), pw_ref[gi], preferred_element_type=F32) * ps_ref[:, sl]
        o_ref[:, sl] = og.astype(o_ref.dtype)
    base = hc_rows - (CONV_WIDTH - 1)
    for r in range(ts // rc):
        for c in range(width // LANE):
            cs = slice(c * LANE, (c + 1) * LANE)
            acc = jnp.zeros((rc, LANE), F32)
            for jj in range(CONV_WIDTH):
                k = (base + jj) % SUBLANE
                lo = r * rc + base + jj - k
                acc = acc + zs_ref[k, lo:lo + rc, cs] * dw_ref[jj:jj + 1, cs]
            cb_ref[r * rc:(r + 1) * rc, cs] = acc + dwb_ref[:, cs]
    cv = cb_ref[...]
    mu = jnp.mean(cv, axis=-1, keepdims=True)
    xc = cv - mu
    yn = xc * lax.rsqrt(jnp.mean(xc * xc, axis=-1, keepdims=True) + NORM_EPS) * lng_ref[...] + lnb_ref[...]
    act = yn * jax.nn.sigmoid(yn)
    o_ref[:, width:] = jnp.dot(act.astype(BF), pww_ref[...], preferred_element_type=F32).astype(o_ref.dtype)


def _convmix(u_pool, halo_pool, u_conv, halo_conv, prm, *, ts, nt, n_tiles, zero_first, avail0, hp_rows, hc_rows,
             halo_blk):
    pw, ps, dw, dwb, lng, lnb, pww = prm
    width = u_pool.shape[1]
    rc = min(64, ts)
    body = functools.partial(_convmix_body, ts=ts, nt=nt, zero_first=zero_first, avail0=avail0, rc=rc)
    full = lambda a: pl.BlockSpec(a.shape, lambda s: (0,) * a.ndim)
    return pl.pallas_call(
        body,
        grid=(n_tiles,),
        in_specs=[pl.BlockSpec((ts, width), lambda s: (s, 0)),
                  pl.BlockSpec((hp_rows, width), lambda s: (halo_blk(s, hp_rows), 0)),
                  pl.BlockSpec((ts, width), lambda s: (s, 0)),
                  pl.BlockSpec((hc_rows, width), lambda s: (halo_blk(s, hc_rows), 0)),
                  full(pw), full(ps), full(dw), full(dwb), full(lng), full(lnb), full(pww)],
        out_specs=pl.BlockSpec((ts, 2 * width), lambda s: (s, 0)),
        out_shape=jax.ShapeDtypeStruct((n_tiles * ts, 2 * width), BF),
        scratch_shapes=[pltpu.VMEM((hp_rows + ts, width), F32), pltpu.VMEM((SUBLANE, hc_rows + ts, width), F32),
                        pltpu.VMEM((ts, width), F32)],
        compiler_params=_cparams(("parallel",)),
        name="conv_mix",
    )(u_pool, halo_pool, u_conv, halo_conv, pw, ps, dw, dwb, lng, lnb, pww)


def _router_body(x_ref, g_ref, rw_ref, r_ref, xn_ref):
    x = x_ref[...]
    xn = x * lax.rsqrt(jnp.mean(x * x, axis=-1, keepdims=True) + NORM_EPS) * g_ref[...]
    xn_ref[...] = xn.astype(xn_ref.dtype)
    logits = _dot3(xn, rw_ref[...])
    lane = lax.broadcasted_iota(jnp.int32, logits.shape, 1)
    l1 = jnp.where(lane < N_EXPERTS, logits, NEG_INF)
    m1 = jnp.max(l1, axis=-1, keepdims=True)
    i1 = jnp.min(jnp.where(l1 == m1, lane, LANE), axis=-1, keepdims=True)
    l2 = jnp.where(lane == i1, NEG_INF, l1)
    m2 = jnp.max(l2, axis=-1, keepdims=True)
    i2 = jnp.min(jnp.where(l2 == m2, lane, LANE), axis=-1, keepdims=True)
    e = jnp.exp(m2 - m1)
    w1 = 1.0 / (1.0 + e)
    w2 = e / (1.0 + e)
    r_ref[...] = jnp.where(lane == 0, i1.astype(F32),
                           jnp.where(lane == 1, i2.astype(F32), jnp.where(lane == 2, w1, jnp.where(lane == 3, w2, 0.0))))


def _router(x, g, rw, tm=256):
    m, d = x.shape
    rw_pad = jnp.pad(rw, ((0, 0), (0, LANE - rw.shape[1])))
    return pl.pallas_call(
        _router_body,
        grid=(pl.cdiv(m, tm),),
        in_specs=[pl.BlockSpec((tm, d), lambda i: (i, 0)), pl.BlockSpec((1, d), lambda i: (0, 0)),
                  pl.BlockSpec((d, LANE), lambda i: (0, 0))],
        out_specs=[pl.BlockSpec((tm, LANE), lambda i: (i, 0)), pl.BlockSpec((tm, d), lambda i: (i, 0))],
        out_shape=[jax.ShapeDtypeStruct((m, LANE), F32), jax.ShapeDtypeStruct((m, d), BF)],
        compiler_params=_cparams(("parallel",)),
        name="moe_router",
    )(x, g.reshape(1, d), rw_pad)


def _row_copy(src_hbm, row, dst, drow, sem):
    return pltpu.make_async_copy(src_hbm.at[pl.ds(row, 1), :], dst.at[pl.ds(drow, 1), :], sem)


def _gather_rows_body(c0_ref, nc_ref, src_ref, xn_hbm, o_ref, buf_ref, acc_ref, first_ref, sem, *, tc):
    i = pl.program_id(0)
    c0 = c0_ref[i]
    n = nc_ref[i]
    i_next = jnp.minimum(i + 1, pl.num_programs(0) - 1)
    n_next = jnp.where(i + 1 < pl.num_programs(0), nc_ref[i_next], 0)

    def chunk_copy(c, slot):
        return pltpu.make_async_copy(xn_hbm.at[pl.ds(c * tc, tc), :], buf_ref.at[slot], sem.at[slot])

    acc_ref[...] = jnp.zeros(acc_ref.shape, F32)

    @pl.when(i == 0)
    def _():
        first_ref[0] = 0

        @pl.when(n > 0)
        def _():
            chunk_copy(c0, 0).start()

    first = first_ref[0]
    src = src_ref[...]

    def body(k, carry):
        slot = (first + k) % 2
        chunk_copy(c0 + k, slot).wait()

        @pl.when(k + 1 < n)
        def _():
            chunk_copy(c0 + k + 1, 1 - slot).start()

        @pl.when(jnp.logical_and(k + 1 == n, n_next > 0))
        def _():
            chunk_copy(c0_ref[i_next], 1 - slot).start()
            first_ref[0] = 1 - slot

        tok = (c0 + k) * tc + lax.broadcasted_iota(jnp.int32, (1, tc), 1)
        onehot = jnp.where(src == tok, 1.0, 0.0).astype(buf_ref.dtype)
        acc_ref[...] += jnp.dot(onehot, buf_ref[slot], preferred_element_type=F32)
        return carry

    lax.fori_loop(0, n, body, 0)
    o_ref[...] = acc_ref[...].astype(o_ref.dtype)


def _gather_rows(xn, src, c0, nc, tm, tc):
    n_tok, d = xn.shape
    r_tot = src.shape[0]
    body = functools.partial(_gather_rows_body, tc=tc)
    return pl.pallas_call(
        body,
        grid_spec=pltpu.PrefetchScalarGridSpec(
            num_scalar_prefetch=2,
            grid=(r_tot // tm,),
            in_specs=[pl.BlockSpec((tm, 1), lambda i, a, b: (i, 0)), pl.BlockSpec(memory_space=pl.ANY)],
            out_specs=pl.BlockSpec((tm, d), lambda i, a, b: (i, 0)),
            scratch_shapes=[pltpu.VMEM((2, tc, d), xn.dtype), pltpu.VMEM((tm, d), F32), pltpu.SMEM((1,), jnp.int32),
                            pltpu.SemaphoreType.DMA((2,))],
        ),
        out_shape=jax.ShapeDtypeStruct((r_tot, d), xn.dtype),
        compiler_params=_cparams(("arbitrary",)),
        name="moe_gather",
    )(c0, nc, src.reshape(r_tot, 1), xn)


def _combine_body(p1_ref, p2_ref, eo_hbm, x_ref, r_ref, o_ref, ot_ref, b1_ref, b2_ref, sem, *, tm, n_main, n_tail):
    i = pl.program_id(0)
    base = i * tm
    n = jnp.where(i < n_main // tm, tm, n_tail)

    def issue(r, c):
        _row_copy(eo_hbm, p1_ref[base + r], b1_ref, r, sem).start()
        _row_copy(eo_hbm, p2_ref[base + r], b2_ref, r, sem).start()
        return c

    lax.fori_loop(0, n, issue, 0)

    def wait(r, c):
        _row_copy(eo_hbm, 0, b1_ref, r, sem).wait()
        _row_copy(eo_hbm, 0, b2_ref, r, sem).wait()
        return c

    lax.fori_loop(0, n, wait, 0)

    def rows(r):
        rr = r_ref[0:r, :]
        return x_ref[0:r, :] + rr[:, 2:3] * b1_ref[0:r, :] + rr[:, 3:4] * b2_ref[0:r, :]

    @pl.when(i < n_main // tm)
    def _():
        o_ref[...] = rows(tm)

    @pl.when(i == n_main // tm)
    def _():
        ot_ref[...] = rows(n_tail)


def _combine(x, eo, routing, p1, p2, n_main, tm=256):
    m, d = x.shape
    n_tail = m - n_main
    assert n_main % tm == 0 and 0 < n_tail <= tm and n_tail % SUBLANE == 0
    n_full = n_main // tm
    body = functools.partial(_combine_body, tm=tm, n_main=n_main, n_tail=n_tail)
    return pl.pallas_call(
        body,
        grid_spec=pltpu.PrefetchScalarGridSpec(
            num_scalar_prefetch=2,
            grid=(n_full + 1,),
            in_specs=[pl.BlockSpec(memory_space=pl.ANY), pl.BlockSpec((tm, d), lambda i, a, b: (i, 0)),
                      pl.BlockSpec((tm, LANE), lambda i, a, b: (i, 0))],
            out_specs=[pl.BlockSpec((tm, d), lambda i, a, b: (jnp.minimum(i, n_full - 1), 0)),
                       pl.BlockSpec((n_tail, d), lambda i, a, b: (0, 0))],
            scratch_shapes=[pltpu.VMEM((tm, d), F32), pltpu.VMEM((tm, d), F32), pltpu.SemaphoreType.DMA(())],
        ),
        out_shape=[jax.ShapeDtypeStruct((n_main, d), F32), jax.ShapeDtypeStruct((n_tail, d), F32)],
        compiler_params=_cparams(("arbitrary",)),
        name="moe_combine",
    )(p1, p2, eo, x, routing)


def _moe_plan(e_idx, tm, tc):
    n = e_idx.shape[0]
    n_asg = n * TOP_K
    flat_e = e_idx.reshape(-1)
    onehot = (flat_e[:, None] == jnp.arange(N_EXPERTS, dtype=jnp.int32)[None, :]).astype(jnp.int32)
    cnt = jnp.sum(onehot, axis=0)
    rank = jnp.take_along_axis(jnp.cumsum(onehot, axis=0) - onehot, flat_e[:, None], axis=1)[:, 0]
    cnt_p = ((cnt + tm - 1) // tm) * tm
    ends = jnp.cumsum(cnt_p)
    off = ends - cnt_p
    pos = off[flat_e] + rank
    n_tiles = (n_asg + N_EXPERTS * (tm - 1) + tm - 1) // tm
    r_tot = n_tiles * tm
    src = jnp.full((r_tot,), -1, jnp.int32).at[pos].set(jnp.arange(n_asg, dtype=jnp.int32) // TOP_K)
    src_t = src.reshape(n_tiles, tm)
    lo = jnp.min(jnp.where(src_t >= 0, src_t, n), axis=1)
    hi = jnp.max(src_t, axis=1)
    c0 = jnp.where(hi >= 0, lo // tc, 0).astype(jnp.int32)
    nc = jnp.where(hi >= 0, hi // tc + 1 - lo // tc, 0).astype(jnp.int32)
    tile_start = jnp.arange(n_tiles, dtype=jnp.int32) * tm
    tval = (tile_start < ends[-1]).astype(jnp.int32)
    texp = jnp.minimum(jnp.sum((tile_start[:, None] >= ends[None, :]).astype(jnp.int32), axis=1), N_EXPERTS - 1)
    last = jnp.max(jnp.where(tval > 0, texp, 0))
    texp = jnp.where(tval > 0, texp, last)
    pos2 = pos.reshape(n, TOP_K)
    return src, c0, nc, pos2[:, 0], pos2[:, 1], texp, tval


def _moe(x, g, rw, wg, wu, wd, n_main, tm=256):
    n = x.shape[0]
    tc = max(t for t in range(2 * SUBLANE, 513, 2 * SUBLANE) if n % t == 0)
    routing, xn = _router(x, g, rw)
    e_idx = routing[:, :TOP_K].astype(jnp.int32)
    src, c0, nc, p1, p2, texp, tval = _moe_plan(e_idx, tm, tc)
    xs = _gather_rows(xn, src, c0, nc, tm, tc)
    hid = _gmm(xs, [(wg, 0), (wu, 0)], wg.shape[2], epi="swiglu", texp=texp, tval=tval, out_dtype=BF, tm=tm, tn=1024,
               name="moe_up")
    eo = _gmm(hid, [(wd, 0)], wd.shape[2], texp=texp, tval=tval, tm=tm, name="moe_down")
    return _combine(x, eo, routing, p1, p2, n_main)


def _seq_tails(a, nb, t, k):
    return jnp.stack([a[(b + 1) * t - k:(b + 1) * t] for b in range(nb)])


def _attn_layer(x, n_prompt, nb_p, t_p, nb_s, li, prm, caches, page_table):
    (attn_norm, w_in, qk_g, cmp_pe, cmp_w1, cmp_w2, moba_g, w_out, ffn_norm, w_gate, w_up, w_down) = prm
    cache_cmp, cache_sel, state_win, cache_moba = caches
    d = x[0].shape[1]
    past_len = page_table.shape[1] * cache_moba.shape[2]

    n_in = w_in.shape[1]
    g_lo, g_hi = CH_QB * LANE, CH_QB * LANE + 3 * NSA_HEADS
    w_re = jnp.concatenate([w_in[:, :g_lo], w_in[:, g_hi:], w_in[:, g_lo:g_hi],
                            jnp.zeros((d, ATTN_CHUNKS * LANE - n_in), F32)], axis=1)[None]
    ones = jnp.ones((HEAD_DIM,), F32)
    zeros = jnp.zeros((HEAD_DIM,), F32)
    chunk_gain = ([qk_g[0]] * 8 + [ones] * 4 + [qk_g[2]] * 2 + [ones] * 2 + [qk_g[3]] * 2 + [ones] * 2
                  + [moba_g[0]] * 8 + [moba_g[1]] * 8 + [ones] * 9)
    chunk_flag = ([ones] * 8 + [zeros] * 4 + [ones] * 2 + [zeros] * 2 + [ones] * 2 + [zeros] * 2
                  + [ones] * 16 + [zeros] * 9)
    gain = jnp.concatenate(chunk_gain)[None]
    flag = jnp.concatenate(chunk_flag)[None]
    xn = _rmsnorm(x, attn_norm, tm=STREAM_TM)
    y, y16 = _gmm(xn, [(w_re, 0)], ATTN_CHUNKS * LANE, epi="headnorm", gain=gain, flag=flag, out_dtype=(F32, BF),
                  tm=STREAM_TM, tn=9 * LANE, name="attn_in")

    r = NSA_CMP_LEN // NSA_CMP_STRIDE
    wcat = cmp_w1.reshape(2, r, NSA_CMP_STRIDE, HEAD_DIM, NSA_CMP_HIDDEN).transpose(0, 2, 3, 1, 4)
    wcat = wcat.reshape(2, NSA_CMP_STRIDE, HEAD_DIM, r * NSA_CMP_HIDDEN).astype(BF)
    pe_rows = cmp_pe.reshape(2, r, NSA_CMP_STRIDE, HEAD_DIM).transpose(0, 2, 1, 3)
    pe_rows = jnp.pad(pe_rows, ((0, 0), (0, 0), (0, SUBLANE - r), (0, 0))).astype(BF)
    w2 = cmp_w2.astype(BF)
    gk = qk_g[1][None]

    kvc_p = _compress_c2(_compress_c1_prompt(y, wcat, nb_p, t_p), pe_rows, wcat, w2, gk)
    rows = n_prompt
    oc, sel_a = _cattn(y, 0, 0, kvc_p, nb_p, t_p, min(512, t_p), t_p, 0, rows)
    sel_b = _moba_gate_prompt(y, nb_p, t_p, min(1024, t_p))
    os_ = _flash(y, y16, sel_a, nb_p, t_p, q_ch=CH_QA, k_ch=CH_KS, v_ch=CH_VS, kvh=NSA_KV, nh=NSA_HPG,
                 blk=NSA_SEL_BLOCK, window=0, out_rows=rows)
    ow = _flash(y, y16, None, nb_p, t_p, q_ch=CH_QA, k_ch=CH_KW, v_ch=CH_VW, kvh=NSA_KV, nh=NSA_HPG, blk=0,
                window=NSA_WINDOW, out_rows=rows)
    ob = _flash(y, y16, sel_b, nb_p, t_p, q_ch=CH_QB, k_ch=CH_KB, v_ch=CH_VB, kvh=MOBA_HEADS, nh=1, blk=MOBA_BLOCK,
                window=0, out_rows=rows)

    n_pad = x[2] - n_prompt
    ys = y[n_prompt:]
    kvc_s = _compress_c2(_compress_c1_sample(cache_cmp, li, page_table, wcat), pe_rows, wcat, w2, gk)
    q_pad = jnp.zeros((nb_s, SUBLANE, NSA_HEADS * HEAD_DIM), F32).at[:, 0].set(ys[:nb_s, :NSA_HEADS * HEAD_DIM])
    oc_s, sel_s = _cattn_sample(q_pad.reshape(nb_s * SUBLANE, -1), kvc_s, nb_s, past_len + 1, past_len)
    oc_s = oc_s.reshape(nb_s, SUBLANE, -1)[:, 0]
    n_sel = (past_len + 1) // NSA_SEL_BLOCK
    k_sel = min(NSA_SEL_COUNT - 1, n_sel)
    mask_a = sel_s[:, :, 0, :n_sel]
    idx_a = jnp.argsort(-mask_a, axis=-1, stable=True)[..., :k_sel].astype(jnp.int32)
    vld_a = (jnp.take_along_axis(mask_a, idx_a, axis=-1) > 0).astype(jnp.int32)
    page = cache_sel.shape[2]
    bpp = page // NSA_SEL_BLOCK
    pt_b = page_table[:, None, :]
    tbl_a = jnp.take_along_axis(jnp.broadcast_to(pt_b, (nb_s, NSA_KV, pt_b.shape[-1])), idx_a // bpp, axis=-1) * bpp + idx_a % bpp
    os_s = _decode_attn(ys, cache_sel, li, tbl_a.astype(jnp.int32), vld_a, q_ch=CH_QA, kn_ch=CH_KS, vn_ch=CH_VS,
                        nh=NSA_HPG, rows=NSA_SEL_BLOCK, kvh=NSA_KV, r_min=0,
                        name="dec_sel")
    win_buf = state_win.shape[2]
    tbl_w = jnp.broadcast_to(jnp.arange(nb_s, dtype=jnp.int32)[:, None, None], (nb_s, NSA_KV, 1))
    ow_s = _decode_attn(ys, state_win, li, tbl_w, jnp.ones_like(tbl_w), q_ch=CH_QA, kn_ch=CH_KW, vn_ch=CH_VW,
                        nh=NSA_HPG, rows=win_buf, kvh=NSA_KV, r_min=max(0, win_buf - (NSA_WINDOW - 1)),
                        name="dec_win")
    means = _moba_means_sample(cache_moba, li, page_table)
    qb_pad = jnp.zeros((nb_s, SUBLANE, MOBA_HEADS * HEAD_DIM), F32).at[:, 0].set(
        ys[:nb_s, CH_QB * LANE:(CH_QB + MOBA_HEADS) * LANE])
    sel_m = _moba_gate_sample(qb_pad.reshape(nb_s * SUBLANE, -1), means, past_len)
    n_full = means.shape[2]
    k_top = min(MOBA_TOPK, n_full)
    mask_b = sel_m[:, :, :n_full, 0]
    idx_b = jnp.argsort(-mask_b, axis=-1, stable=True)[..., :k_top].astype(jnp.int32)
    vld_b = (jnp.take_along_axis(mask_b, idx_b, axis=-1) > 0).astype(jnp.int32)
    ppb = MOBA_BLOCK // page
    pg_b = (idx_b[..., None] * ppb + jnp.arange(ppb, dtype=jnp.int32)).reshape(nb_s, MOBA_HEADS, k_top * ppb)
    tbl_b = jnp.take_along_axis(jnp.broadcast_to(pt_b, (nb_s, MOBA_HEADS, pt_b.shape[-1])), pg_b, axis=-1)
    vld_b = jnp.repeat(vld_b, ppb, axis=-1)
    ob_s = _decode_attn(ys, cache_moba, li, tbl_b.astype(jnp.int32), vld_b, q_ch=CH_QB, kn_ch=CH_KB, vn_ch=CH_VB,
                        nh=1, rows=page, kvh=MOBA_HEADS, r_min=0,
                        name="dec_moba")

    def rows_s(part, nh):
        return jnp.pad(part[:, :, :nh].reshape(nb_s, -1), ((0, n_pad - nb_s), (0, 0)))

    sample_parts = (jnp.pad(oc_s, ((0, n_pad - nb_s), (0, 0))), rows_s(os_s, NSA_HPG), rows_s(ow_s, NSA_HPG),
                    rows_s(ob_s, 1))
    mixed = _mix((oc, os_, ow, ob), sample_parts, y)
    x = _gmm(mixed, [(w_out[None], 0)], d, resid=x, tm=STREAM_TM, tn=1024, name="attn_out")
    hid = _gmm(_rmsnorm(x, ffn_norm), [(w_gate[None], 0), (w_up[None], 0)], w_gate.shape[1], epi="swiglu",
               out_dtype=BF, tm=STREAM_TM, tn=512, name="ffn_up")
    x = _gmm(hid, [(w_down[None], 0)], d, resid=x, tm=STREAM_TM, tn=512, name="ffn_down")

    def rows_of(lo_ch, n_ch):
        return y[:, lo_ch * LANE:(lo_ch + n_ch) * LANE]

    def split(a, kvh):
        ap = a[:n_prompt].reshape(nb_p, t_p, 2, kvh, HEAD_DIM)
        as_ = a[n_prompt:n_prompt + nb_s].reshape(nb_s, 1, 2, kvh, HEAD_DIM)
        return ap, as_

    cmp_p, cmp_s = split(rows_of(CH_KC, 4), NSA_KV)
    sel_p, sel_s_rows = split(rows_of(CH_KS, 4), NSA_KV)
    win_rows = rows_of(CH_KW, 4)
    win_s = win_rows[n_prompt:n_prompt + nb_s].reshape(nb_s, 1, 2, NSA_KV, HEAD_DIM)
    moba_p, moba_s = split(rows_of(CH_KB, 16), MOBA_HEADS)
    keep_p = min(NSA_WINDOW, t_p)
    new_win_p = _seq_tails(win_rows, nb_p, t_p, keep_p).reshape(nb_p, keep_p, 2, NSA_KV, HEAD_DIM)
    win_all = jnp.concatenate([state_win[li], win_s], axis=1)
    keep = min(NSA_WINDOW, past_len + 1)
    new_win_s = win_all[:, win_all.shape[1] - keep:]
    return x, (cmp_p, cmp_s, sel_p, sel_s_rows, new_win_p, new_win_s, moba_p, moba_s)


def _conv_layer(x, n_prompt, nb_p, t_p, nb_s, prm, states):
    (conv_norm, w_in, pool_w, pool_scale, dw, dw_b, ln_g, ln_b, pw, w_out, moe_norm, router_w, wg, wu, wd) = prm
    state_pool, state_conv = states
    d = x.shape[1]
    width = pool_w.shape[0] * pool_w.shape[1]
    xn = _rmsnorm(x, conv_norm)
    w3 = w_in[None]
    u_pool = _gmm(xn, [(w3, 0)], width, tm=512, tn=width, name="conv_in_pool")
    u_conv = _gmm(xn, [(w3, width), (w3, 2 * width)], width, epi="glu", tm=512, tn=512, name="conv_in_glu")
    prm_mix = (pool_w.astype(BF), pool_scale[None], jnp.pad(dw, ((0, 1), (0, 0))), dw_b[None], ln_g[None], ln_b[None],
               pw.astype(BF))
    pool_buf = max(POOL_WINDOWS) - 1
    conv_buf = CONV_WIDTH - 1
    hp_rows, hc_rows = 16, 32
    ts = min(256, t_p)
    nt = t_p // ts
    mixed_p = _convmix(u_pool, u_pool, u_conv, u_conv, prm_mix, ts=ts, nt=nt, n_tiles=nb_p * nt, zero_first=True,
                       avail0=0, hp_rows=hp_rows, hc_rows=hc_rows,
                       halo_blk=lambda s, hr: jnp.maximum(s * (ts // hr) - 1, 0))
    rows = x.shape[0]
    us_pool = jnp.zeros((nb_s, SUBLANE, width), F32).at[:, 0].set(u_pool[n_prompt:n_prompt + nb_s])
    us_conv = jnp.zeros((nb_s, SUBLANE, width), F32).at[:, 0].set(u_conv[n_prompt:n_prompt + nb_s])
    hs_pool = jnp.pad(state_pool, ((0, 0), (hp_rows - pool_buf, 0), (0, 0))).reshape(nb_s * hp_rows, width)
    hs_conv = jnp.pad(state_conv, ((0, 0), (hc_rows - conv_buf, 0), (0, 0))).reshape(nb_s * hc_rows, width)
    mixed_s = _convmix(us_pool.reshape(nb_s * SUBLANE, width), hs_pool, us_conv.reshape(nb_s * SUBLANE, width), hs_conv,
                       prm_mix, ts=SUBLANE, nt=1, n_tiles=nb_s, zero_first=False, avail0=pool_buf, hp_rows=hp_rows,
                       hc_rows=hc_rows, halo_blk=lambda s, hr: s)
    mixed_s = mixed_s.reshape(nb_s, SUBLANE, 2 * width)[:, 0]
    mixed = (mixed_p, jnp.pad(mixed_s, ((0, STREAM_TM - nb_s), (0, 0))), rows)
    x = _gmm(mixed, [(w_out[None], 0)], d, resid=x, tm=STREAM_TM, tn=1024, name="conv_out")
    x = _moe(x, moe_norm, router_w, wg, wu, wd, n_prompt)

    assert t_p >= conv_buf and t_p >= pool_buf
    new_pool_p = _seq_tails(u_pool, nb_p, t_p, pool_buf)
    new_conv_p = _seq_tails(u_conv, nb_p, t_p, conv_buf)
    new_pool_s = jnp.concatenate([state_pool, u_pool[n_prompt:n_prompt + nb_s][:, None]], axis=1)[:, 1:]
    new_conv_s = jnp.concatenate([state_conv, u_conv[n_prompt:n_prompt + nb_s][:, None]], axis=1)[:, 1:]
    return x, (new_pool_p, new_pool_s, new_conv_p, new_conv_s)


def kernel(x_prompt, x_sample, cache_nsa_cmp, cache_nsa_sel, state_nsa_win, cache_moba, state_pool, state_conv, page_table, attn_norm, w_attn_in, nsa_qk_norm, nsa_cmp_pe, nsa_cmp_w1, nsa_cmp_w2, moba_qk_norm, w_attn_out, ffn_norm, ffn_w_gate, ffn_w_up, ffn_w_down, conv_norm, w_conv_in, pool_w, pool_scale, conv_dw, conv_dw_b, conv_ln_g, conv_ln_b, conv_pw, w_conv_out, moe_norm, router_w, moe_w_gate, moe_w_up, moe_w_down):
    nb_p, t_p, d = x_prompt.shape
    nb_s = x_sample.shape[0]
    n_prompt = nb_p * t_p
    n_pad = 2 * SUBLANE
    assert x_sample.shape[1] == 1 and nb_s <= n_pad
    x = (x_prompt.reshape(n_prompt, d), jnp.pad(x_sample.reshape(nb_s, d), ((0, STREAM_TM - nb_s), (0, 0))),
         n_prompt + n_pad)
    li = 0
    prm_a = (attn_norm[li], w_attn_in[li], nsa_qk_norm[li], nsa_cmp_pe[li], nsa_cmp_w1[li], nsa_cmp_w2[li],
             moba_qk_norm[li], w_attn_out[li], ffn_norm[li], ffn_w_gate[li], ffn_w_up[li], ffn_w_down[li])
    x, attn_new = _attn_layer(x, n_prompt, nb_p, t_p, nb_s, li, prm_a,
                              (cache_nsa_cmp, cache_nsa_sel, state_nsa_win, cache_moba), page_table)
    prm_c = (conv_norm[li], w_conv_in[li], pool_w[li], pool_scale[li], conv_dw[li], conv_dw_b[li], conv_ln_g[li],
             conv_ln_b[li], conv_pw[li], w_conv_out[li], moe_norm[li], router_w[li], moe_w_gate[li], moe_w_up[li],
             moe_w_down[li])
    x, conv_new = _conv_layer(x, n_prompt, nb_p, t_p, nb_s, prm_c, (state_pool[li], state_conv[li]))
    cmp_p, cmp_s, sel_p, sel_s, win_p, win_s, moba_p, moba_s = attn_new
    pool_p, pool_s, conv_p, conv_s = conv_new
    y_p = x[0].reshape(nb_p, t_p, d)
    y_s = x[1][:nb_s].reshape(nb_s, 1, d)
    st = lambda a: a[None]
    return (y_p, y_s, st(cmp_p), st(cmp_s), st(sel_p), st(sel_s), st(win_p), st(win_s), st(moba_p), st(moba_s),
            st(pool_p), st(pool_s), st(conv_p), st(conv_s))
```

```python
import functools
import math

import jax
import jax.numpy as jnp
from jax import lax
from jax.experimental import pallas as pl
from jax.experimental.pallas import tpu as pltpu

F32 = jnp.float32
BF = jnp.bfloat16
NEG_INF = float("-inf")

HEAD_DIM = 128
NORM_EPS = 1e-6
NSA_HEADS = 8
NSA_KV = 2
NSA_HPG = NSA_HEADS // NSA_KV
NSA_CMP_LEN = 32
NSA_CMP_STRIDE = 16
NSA_CMP_HIDDEN = 2 * HEAD_DIM
NSA_SEL_BLOCK = 64
NSA_SEL_COUNT = 16
NSA_WINDOW = 512
NSA_SEL_FORCE = 1.0e4
MOBA_HEADS = 8
MOBA_BLOCK = 256
MOBA_TOPK = 3
POOL_WINDOWS = (2, 4, 8, 16)
CONV_WIDTH = 31
N_EXPERTS = 8
TOP_K = 2

LANE = 128
SUBLANE = 8
VMEM_LIMIT = 58 * 1024 * 1024
STREAM_TM = 512

CH_QA, CH_KC, CH_VC, CH_KS, CH_VS, CH_KW, CH_VW = 0, 8, 10, 12, 14, 16, 18
NSA_CHUNKS = 20
CH_QB, CH_KB, CH_VB = 0, 8, 16
MOBA_CHUNKS = 24
CH_GATE = 0

NT_DIMS = (((1,), (1,)), ((), ()))


def _cparams(sem):
    return pltpu.CompilerParams(dimension_semantics=sem, vmem_limit_bytes=VMEM_LIMIT)


def _split_bf16(a):
    hi = a.astype(BF)
    lo = (a - hi.astype(F32)).astype(BF)
    return hi, lo


def _dot3(a, b, dims=(((1,), (0,)), ((), ()))):
    ah, al = _split_bf16(a)
    bh, bl = _split_bf16(b)
    d = lambda x, y: lax.dot_general(x, y, dims, preferred_element_type=F32)
    return d(ah, bh) + d(ah, bl) + d(al, bh)


def _masked_softmax(s, mask, axis):
    s = jnp.where(mask, s, NEG_INF)
    m = jnp.max(s, axis=axis, keepdims=True)
    m = jnp.where(m == NEG_INF, 0.0, m)
    p = jnp.exp(s - m)
    d = jnp.sum(p, axis=axis, keepdims=True)
    return p / jnp.where(d > 0, d, 1.0)


def _split_stream(a, tm):
    if not isinstance(a, tuple):
        return [a], [lambda i: i], None, a.shape[0]
    main, tail, m = a
    n_full = main.shape[0] // tm
    assert n_full * tm == main.shape[0] and tail.shape[0] == tm and n_full * tm < m <= (n_full + 1) * tm
    return [main, tail], [lambda i: jnp.minimum(i, n_full - 1), lambda i: 0], n_full, m


def _pick_tile(refs, i, n_full):
    return refs[0][...] if n_full is None else jnp.where(i < n_full, refs[0][...], refs[1][...])


def _rmsnorm_body(*refs, n_full):
    x = _pick_tile(refs[:-2], pl.program_id(0), n_full)
    g_ref, o_ref = refs[-2:]
    ms = jnp.mean(x * x, axis=-1, keepdims=True)
    o_ref[...] = (x * lax.rsqrt(ms + NORM_EPS) * g_ref[...]).astype(o_ref.dtype)


def _rmsnorm(x, g, tm=256):
    arrs, maps, n_full, m = _split_stream(x, tm)
    d = arrs[0].shape[1]
    return pl.pallas_call(
        functools.partial(_rmsnorm_body, n_full=n_full),
        grid=(pl.cdiv(m, tm),),
        in_specs=[pl.BlockSpec((tm, d), lambda i, f=f: (f(i), 0)) for f in maps] + [pl.BlockSpec((1, d), lambda i: (0, 0))],
        out_specs=pl.BlockSpec((tm, d), lambda i: (i, 0)),
        out_shape=jax.ShapeDtypeStruct((m, d), BF),
        compiler_params=_cparams(("parallel",)),
        name="rmsnorm",
    )(*arrs, g.reshape(1, d))


def _gmm_body(texp_ref, tval_ref, *refs, n_x, x_full, n_w, cast_w, epi, n_r, r_full, tn, n_o):
    x_refs = refs[:n_x]
    w_refs = refs[n_x:n_x + n_w]
    pos = n_x + n_w
    if epi == "headnorm":
        gain_ref, flag_ref = refs[pos], refs[pos + 1]
        pos += 2
    resid_refs = refs[pos:pos + n_r]
    pos += n_r
    o_refs = refs[pos:pos + n_o]
    wb_refs = refs[pos + n_o:pos + n_o + n_w] if cast_w else w_refs

    i = pl.program_id(1)
    if cast_w:
        changed = jnp.logical_or(i == 0, texp_ref[i] != texp_ref[jnp.maximum(i - 1, 0)])

        @pl.when(changed)
        def _():
            for k in range(n_w):
                wb_refs[k][...] = w_refs[k][...].astype(BF)

    @pl.when(tval_ref[i] == 0)
    def _():
        for o in o_refs:
            o[...] = jnp.zeros(o.shape, o.dtype)

    @pl.when(tval_ref[i] > 0)
    def _():
        x = _pick_tile(x_refs, i, x_full)
        a = jnp.dot(x, wb_refs[0][...], preferred_element_type=F32)
        if epi == "swiglu":
            b = jnp.dot(x, wb_refs[1][...], preferred_element_type=F32)
            y = a * jax.nn.sigmoid(a) * b
        elif epi == "glu":
            b = jnp.dot(x, wb_refs[1][...], preferred_element_type=F32)
            y = a * jax.nn.sigmoid(b)
        else:
            y = a
        if n_r:
            y = y + _pick_tile(resid_refs, i, r_full)
        if epi == "headnorm":
            for c in range(tn // LANE):
                sl = slice(c * LANE, (c + 1) * LANE)
                yc = y[:, sl]
                r = lax.rsqrt(jnp.mean(yc * yc, axis=-1, keepdims=True) + NORM_EPS)
                f = flag_ref[:, sl]
                yn = yc * (f * r + (1.0 - f)) * gain_ref[:, sl]
                for o in o_refs:
                    o[:, sl] = yn.astype(o.dtype)
        else:
            for o in o_refs:
                o[...] = y.astype(o.dtype)


def _gmm(x, ws, n_out, *, epi="none", texp=None, tval=None, resid=None, gain=None, flag=None,
         out_dtype=F32, tm=256, tn=512, name="gmm"):
    x_arrs, x_maps, x_full, m = _split_stream(x, tm)
    kdim = x_arrs[0].shape[1]
    tn = min(tn, n_out)
    n_m = pl.cdiv(m, tm)
    n_n = n_out // tn
    assert n_n * tn == n_out
    if texp is None:
        texp = jnp.zeros((n_m,), jnp.int32)
        tval = jnp.ones((n_m,), jnp.int32)
    n_w = len(ws)
    cast_w = ws[0][0].dtype != BF
    in_specs = [pl.BlockSpec((tm, kdim), lambda j, i, te, tv, f=f: (f(i), 0)) for f in x_maps]
    args = list(x_arrs)
    for w, off in ws:
        assert off % tn == 0 and w.shape[1] == kdim and (w.dtype != BF) == cast_w
        ob = off // tn
        in_specs.append(pl.BlockSpec((None, kdim, tn), lambda j, i, te, tv, ob=ob: (te[i], 0, j + ob)))
        args.append(w)
    if epi == "headnorm":
        in_specs += [pl.BlockSpec((1, tn), lambda j, i, te, tv: (0, j))] * 2
        args += [gain, flag]
    n_r, r_full = 0, None
    if resid is not None:
        r_arrs, r_maps, r_full, r_m = _split_stream(resid, tm)
        assert r_m == m
        n_r = len(r_arrs)
        in_specs += [pl.BlockSpec((tm, tn), lambda j, i, te, tv, f=f: (f(i), j)) for f in r_maps]
        args += r_arrs
    dtypes = out_dtype if isinstance(out_dtype, tuple) else (out_dtype,)
    body = functools.partial(_gmm_body, n_x=len(x_arrs), x_full=x_full, n_w=n_w, cast_w=cast_w, epi=epi, n_r=n_r,
                             r_full=r_full, tn=tn, n_o=len(dtypes))
    outs = pl.pallas_call(
        body,
        grid_spec=pltpu.PrefetchScalarGridSpec(
            num_scalar_prefetch=2,
            grid=(n_n, n_m),
            in_specs=in_specs,
            out_specs=[pl.BlockSpec((tm, tn), lambda j, i, te, tv: (i, j)) for _ in dtypes],
            scratch_shapes=[pltpu.VMEM((kdim, tn), BF) for _ in range(n_w if cast_w else 0)],
        ),
        out_shape=[jax.ShapeDtypeStruct((m, n_out), dt) for dt in dtypes],
        compiler_params=_cparams(("arbitrary", "arbitrary")),
        name=name,
    )(texp, tval, *args)
    return outs if isinstance(out_dtype, tuple) else outs[0]


def _c1_body(x_ref, w_ref, o_ref):
    nsb = x_ref.shape[0] // NSA_CMP_STRIDE
    acc = jnp.zeros((nsb, 2 * NSA_CMP_HIDDEN), F32)
    for s in range(NSA_CMP_STRIDE):
        xs = x_ref[pl.ds(s, nsb, stride=NSA_CMP_STRIDE), :]
        acc = acc + jnp.dot(xs.astype(BF), w_ref[s], preferred_element_type=F32)
    o_ref[...] = acc


def _compress_c1_prompt(y, wcat, nb, t):
    nsub = t // NSA_CMP_STRIDE
    return pl.pallas_call(
        _c1_body,
        grid=(nb, 2, NSA_KV),
        in_specs=[pl.BlockSpec((t, LANE), lambda b, kv, g: (b, CH_KC + 2 * kv + g)),
                  pl.BlockSpec((None, NSA_CMP_STRIDE, HEAD_DIM, 2 * NSA_CMP_HIDDEN), lambda b, kv, g: (kv, 0, 0, 0))],
        out_specs=pl.BlockSpec((None, None, None, nsub, 2 * NSA_CMP_HIDDEN), lambda b, kv, g: (b, kv, g, 0, 0)),
        out_shape=jax.ShapeDtypeStruct((nb, 2, NSA_KV, nsub, 2 * NSA_CMP_HIDDEN), F32),
        compiler_params=_cparams(("parallel", "parallel", "parallel")),
        name="cmp_c1_prompt",
    )(y, wcat)


C1_PAGES = 16


def _c1_sample_body(pt_ref, *refs, n_in, page):
    x_refs = refs[:n_in]
    w_ref, o_ref, scr = refs[n_in:n_in + 3]
    nsb = page // NSA_CMP_STRIDE
    rows = n_in * nsb
    for kv in range(2):
        acc = jnp.zeros((rows * NSA_KV, 2 * NSA_CMP_HIDDEN), F32)

        def rows_at(s):
            parts = [xr[pl.ds(s, nsb, stride=NSA_CMP_STRIDE), kv, :, :].reshape(nsb * NSA_KV, HEAD_DIM) for xr in x_refs]
            return (parts[0] if n_in == 1 else jnp.concatenate(parts, axis=0)).astype(BF)

        for s in range(0, NSA_CMP_STRIDE, 2):
            xs = jnp.concatenate([rows_at(s), rows_at(s + 1)], axis=1)
            w = w_ref[kv, s:s + 2].reshape(2 * HEAD_DIM, 2 * NSA_CMP_HIDDEN)
            acc = acc + jnp.dot(xs, w, preferred_element_type=F32)
        for c in range(scr.shape[0]):
            scr[c] = acc[:, c * LANE:(c + 1) * LANE]
        for g in range(NSA_KV):
            for c in range(scr.shape[0]):
                o_ref[kv, g, :, c * LANE:(c + 1) * LANE] = scr[c, pl.ds(g, rows, stride=NSA_KV), :]


def _compress_c1_sample(cache, li, page_table, wcat):
    nb, n_pages = page_table.shape
    page = cache.shape[2]
    npp = math.gcd(C1_PAGES, n_pages)
    nsb = page // NSA_CMP_STRIDE
    nsub = n_pages * nsb
    body = functools.partial(_c1_sample_body, n_in=npp, page=page)
    in_specs = [
        pl.BlockSpec((None, None, page, 2, NSA_KV, HEAD_DIM), lambda b, j, pt, i=i: (li, pt[b, j * npp + i], 0, 0, 0, 0))
        for i in range(npp)
    ]
    in_specs.append(pl.BlockSpec(wcat.shape, lambda b, j, pt: (0, 0, 0, 0)))
    return pl.pallas_call(
        body,
        grid_spec=pltpu.PrefetchScalarGridSpec(
            num_scalar_prefetch=1,
            grid=(nb, n_pages // npp),
            in_specs=in_specs,
            out_specs=pl.BlockSpec((None, 2, NSA_KV, npp * nsb, 2 * NSA_CMP_HIDDEN), lambda b, j, pt: (b, 0, 0, j, 0)),
            scratch_shapes=[pltpu.VMEM((2 * NSA_CMP_HIDDEN // LANE, npp * nsb * NSA_KV, LANE), F32)],
        ),
        out_shape=jax.ShapeDtypeStruct((nb, 2, NSA_KV, nsub, 2 * NSA_CMP_HIDDEN), F32),
        compiler_params=_cparams(("parallel", "parallel")),
        name="cmp_c1_sample",
    )(page_table, *([cache] * npp), wcat)


def _c2_body(p_ref, pe_ref, wcat_ref, w2_ref, g_ref, o_ref, *, nsub, n_c):
    kv = pl.program_id(1)
    pep = jnp.zeros((SUBLANE, 2 * NSA_CMP_HIDDEN), F32)
    for s in range(NSA_CMP_STRIDE):
        pep = pep + jnp.dot(pe_ref[s], wcat_ref[s], preferred_element_type=F32)
    bias = pep[0:1, :NSA_CMP_HIDDEN] + pep[1:2, NSA_CMP_HIDDEN:]
    nxt = pltpu.roll(p_ref[:, NSA_CMP_HIDDEN:], nsub - 1, 0)
    h = p_ref[:, :NSA_CMP_HIDDEN] + nxt + bias
    h = h * jax.nn.sigmoid(h)
    o = jnp.dot(h.astype(BF), w2_ref[...], preferred_element_type=F32)
    r = lax.rsqrt(jnp.mean(o * o, axis=-1, keepdims=True) + NORM_EPS)
    o = jnp.where(kv == 0, o * r * g_ref[...], o)
    row = lax.broadcasted_iota(jnp.int32, (nsub, 1), 0)
    o_ref[...] = jnp.where(row < n_c, o, 0.0)


def _compress_c2(p, pe_rows, wcat, w2, gk):
    nb, _, _, nsub, _ = p.shape
    n_c = nsub - NSA_CMP_LEN // NSA_CMP_STRIDE + 1
    body = functools.partial(_c2_body, nsub=nsub, n_c=n_c)
    return pl.pallas_call(
        body,
        grid=(nb, 2, NSA_KV),
        in_specs=[pl.BlockSpec((None, None, None, nsub, 2 * NSA_CMP_HIDDEN), lambda b, kv, g: (b, kv, g, 0, 0)),
                  pl.BlockSpec((None, NSA_CMP_STRIDE, SUBLANE, HEAD_DIM), lambda b, kv, g: (kv, 0, 0, 0)),
                  pl.BlockSpec((None, NSA_CMP_STRIDE, HEAD_DIM, 2 * NSA_CMP_HIDDEN), lambda b, kv, g: (kv, 0, 0, 0)),
                  pl.BlockSpec((None, NSA_CMP_HIDDEN, HEAD_DIM), lambda b, kv, g: (kv, 0, 0)),
                  pl.BlockSpec((1, HEAD_DIM), lambda b, kv, g: (0, 0))],
        out_specs=pl.BlockSpec((None, None, None, nsub, HEAD_DIM), lambda b, kv, g: (b, kv, g, 0, 0)),
        out_shape=jax.ShapeDtypeStruct((nb, 2, NSA_KV, nsub, HEAD_DIM), F32),
        compiler_params=_cparams(("parallel", "parallel", "parallel")),
        name="cmp_c2",
    )(p, pe_rows, wcat, w2, gk)


def _rank_select(sc_ref, n_iter, k):
    score = sc_ref[...]
    brow = lax.broadcasted_iota(jnp.int32, score.shape, 0)

    def body(j, rank):
        r = sc_ref[pl.ds(j, 1), :]
        beats = jnp.logical_or(r > score, jnp.logical_and(r == score, j < brow))
        return rank + jnp.where(beats, 1.0, 0.0)

    rank = lax.fori_loop(0, n_iter, body, jnp.zeros(score.shape, F32))
    return rank < k


def _cattn_body(q_ref, kc_ref, vc_ref, oc_ref, sel_ref, sc_ref, *, tq, ncp, n_c, ns, nsp, pos0, ksel):
    t0 = pos0 + pl.program_id(2) * tq
    scale = HEAD_DIM ** -0.5
    kc = kc_ref[...].astype(BF)
    vc = vc_ref[...].astype(BF)
    trow = t0 + lax.broadcasted_iota(jnp.int32, (tq, 1), 0)
    ncol = lax.broadcasted_iota(jnp.int32, (1, ncp), 1)
    valid = jnp.logical_and(ncol * NSA_CMP_STRIDE + (NSA_CMP_LEN - 1) <= trow, ncol < n_c)
    tcol = t0 + lax.broadcasted_iota(jnp.int32, (1, tq), 1)
    nrow = lax.broadcasted_iota(jnp.int32, (ncp, 1), 0)
    valid_t = jnp.logical_and(nrow * NSA_CMP_STRIDE + (NSA_CMP_LEN - 1) <= tcol, nrow < n_c)
    psum_t = jnp.zeros((ncp, tq), F32)
    for z in range(NSA_HPG):
        sl = slice(z * HEAD_DIM, (z + 1) * HEAD_DIM)
        q = (q_ref[:, sl] * scale).astype(BF)
        s = lax.dot_general(q, kc, NT_DIMS, preferred_element_type=F32)
        p = _masked_softmax(s, valid, -1)
        oc_ref[:, sl] = jnp.dot(p.astype(BF), vc, preferred_element_type=F32)
        s_t = lax.dot_general(kc, q, NT_DIMS, preferred_element_type=F32)
        psum_t = psum_t + _masked_softmax(s_t, valid_t, 0)
    r = NSA_SEL_BLOCK // NSA_CMP_STRIDE
    brow = lax.broadcasted_iota(jnp.int32, (nsp, 1), 0)
    lo = r * brow - 1
    inside = jnp.logical_and(ncol >= lo, ncol <= lo + r)
    edge = jnp.logical_or(ncol == lo, ncol == lo + r)
    m_t = jnp.where(inside, jnp.where(edge, 0.5, 1.0), 0.0).astype(BF)
    hi = psum_t.astype(BF)
    mid = (psum_t - hi.astype(F32)).astype(BF)
    low = (psum_t - hi.astype(F32) - mid.astype(F32)).astype(BF)
    imp_t = (jnp.dot(m_t, hi, preferred_element_type=F32) + jnp.dot(m_t, mid, preferred_element_type=F32)
             + jnp.dot(m_t, low, preferred_element_type=F32))
    own = tcol // NSA_SEL_BLOCK
    past = jnp.logical_and(brow < own, brow < ns)
    forced = jnp.logical_or(brow == 0, brow == own - 1)
    sc_ref[...] = jnp.where(past, jnp.where(forced, NSA_SEL_FORCE, imp_t), NEG_INF)
    chosen = jnp.logical_and(past, _rank_select(sc_ref, ns, ksel))
    sel = jnp.where(jnp.logical_or(chosen, brow == own), 1.0, 0.0)
    rows = sel_ref.shape[0]
    if rows > nsp:
        sel_ref[...] = jnp.zeros(sel_ref.shape, F32)
    sel_ref[0:nsp, :] = sel


def _cattn(q_arr, q_row_blk0, q_col_blk0, kvc, nb, tlen, tq, length, pos0, out_rows):
    nq = tlen // tq
    ncp = kvc.shape[3]
    n_c = ncp - NSA_CMP_LEN // NSA_CMP_STRIDE + 1
    ns = length // NSA_SEL_BLOCK
    ksel = min(NSA_SEL_COUNT - 1, ns)
    nsp = -(-ns // SUBLANE) * SUBLANE
    sel_rows = max(nsp, LANE)
    body = functools.partial(_cattn_body, tq=tq, ncp=ncp, n_c=n_c, ns=ns, nsp=nsp, pos0=pos0, ksel=ksel)
    gw = NSA_HPG * HEAD_DIM
    return pl.pallas_call(
        body,
        grid=(nb, NSA_KV, nq),
        in_specs=[pl.BlockSpec((tq, gw), lambda b, g, qi: (q_row_blk0 + b * nq + qi, q_col_blk0 + g)),
                  pl.BlockSpec((None, None, None, ncp, HEAD_DIM), lambda b, g, qi: (b, 0, g, 0, 0)),
                  pl.BlockSpec((None, None, None, ncp, HEAD_DIM), lambda b, g, qi: (b, 1, g, 0, 0))],
        out_specs=[pl.BlockSpec((tq, gw), lambda b, g, qi: (b * nq + qi, g)),
                   pl.BlockSpec((None, None, sel_rows, tq), lambda b, g, qi: (b, g, 0, qi))],
        out_shape=[jax.ShapeDtypeStruct((out_rows, NSA_KV * gw), F32),
                   jax.ShapeDtypeStruct((nb, NSA_KV, sel_rows, tlen), F32)],
        scratch_shapes=[pltpu.VMEM((nsp, tq), F32)],
        compiler_params=_cparams(("parallel", "parallel", "parallel")),
        name="nsa_cmp_attn",
    )(q_arr, kvc, kvc)


def _cattn_row_body(q_ref, kc_ref, vc_ref, oc_ref, sel_ref, *, tq, ncp, n_c, ns, nsl, pos0, ksel):
    scale = HEAD_DIM ** -0.5
    kc = kc_ref[...].astype(BF)
    vc = vc_ref[...].astype(BF)
    trow = pos0 + lax.broadcasted_iota(jnp.int32, (tq, 1), 0)
    ncol = lax.broadcasted_iota(jnp.int32, (1, ncp), 1)
    valid = jnp.logical_and(ncol * NSA_CMP_STRIDE + (NSA_CMP_LEN - 1) <= trow, ncol < n_c)
    psum = jnp.zeros((tq, ncp), F32)
    for z in range(NSA_HPG):
        sl = slice(z * HEAD_DIM, (z + 1) * HEAD_DIM)
        q = (q_ref[:, sl] * scale).astype(BF)
        p = _masked_softmax(lax.dot_general(q, kc, NT_DIMS, preferred_element_type=F32), valid, -1)
        oc_ref[:, sl] = jnp.dot(p.astype(BF), vc, preferred_element_type=F32)
        psum = psum + p
    r = NSA_SEL_BLOCK // NSA_CMP_STRIDE
    nrow = lax.broadcasted_iota(jnp.int32, (ncp, 1), 0)
    bcol = lax.broadcasted_iota(jnp.int32, (1, nsl), 1)
    lo = r * bcol - 1
    inside = jnp.logical_and(nrow >= lo, nrow <= lo + r)
    edge = jnp.logical_or(nrow == lo, nrow == lo + r)
    m = jnp.where(inside, jnp.where(edge, 0.5, 1.0), 0.0).astype(BF)
    hi = psum.astype(BF)
    mid = (psum - hi.astype(F32)).astype(BF)
    low = (psum - hi.astype(F32) - mid.astype(F32)).astype(BF)
    imp = (jnp.dot(hi, m, preferred_element_type=F32) + jnp.dot(mid, m, preferred_element_type=F32)
           + jnp.dot(low, m, preferred_element_type=F32))
    own = trow // NSA_SEL_BLOCK
    past = jnp.logical_and(bcol < own, bcol < ns)
    forced = jnp.logical_or(bcol == 0, bcol == own - 1)
    score = jnp.where(past, jnp.where(forced, NSA_SEL_FORCE, imp), NEG_INF)
    rank = jnp.zeros((tq, nsl), F32)
    for j in range(ns):
        cj = score[:, j:j + 1]
        beats = jnp.logical_or(cj > score, jnp.logical_and(cj == score, j < bcol))
        rank = rank + jnp.where(beats, 1.0, 0.0)
    chosen = jnp.logical_and(past, rank < ksel)
    sel_ref[...] = jnp.where(jnp.logical_or(chosen, bcol == own), 1.0, 0.0)


def _cattn_sample(q_pad, kvc, nb, length, pos0):
    tq = SUBLANE
    ncp = kvc.shape[3]
    n_c = ncp - NSA_CMP_LEN // NSA_CMP_STRIDE + 1
    ns = length // NSA_SEL_BLOCK
    ksel = min(NSA_SEL_COUNT - 1, ns)
    nsl = -(-ns // LANE) * LANE
    body = functools.partial(_cattn_row_body, tq=tq, ncp=ncp, n_c=n_c, ns=ns, nsl=nsl, pos0=pos0, ksel=ksel)
    gw = NSA_HPG * HEAD_DIM
    return pl.pallas_call(
        body,
        grid=(nb, NSA_KV),
        in_specs=[pl.BlockSpec((tq, gw), lambda b, g: (b, g)),
                  pl.BlockSpec((None, None, None, ncp, HEAD_DIM), lambda b, g: (b, 0, g, 0, 0)),
                  pl.BlockSpec((None, None, None, ncp, HEAD_DIM), lambda b, g: (b, 1, g, 0, 0))],
        out_specs=[pl.BlockSpec((tq, gw), lambda b, g: (b, g)),
                   pl.BlockSpec((None, None, tq, nsl), lambda b, g: (b, g, 0, 0))],
        out_shape=[jax.ShapeDtypeStruct((nb * tq, NSA_KV * gw), F32),
                   jax.ShapeDtypeStruct((nb, NSA_KV, tq, nsl), F32)],
        compiler_params=_cparams(("parallel", "parallel")),
        name="nsa_cmp_attn_sample",
    )(q_pad, kvc, kvc)


def _gate_body(q_ref, k_ref, sel_ref, mean_ref, sc_ref, *, tq, nb, nbp, pos0):
    qi = pl.program_id(2)

    @pl.when(qi == 0)
    def _():
        mean_ref[...] = jnp.zeros(mean_ref.shape, F32)
        for j in range(nb):
            blk = k_ref[j * MOBA_BLOCK:(j + 1) * MOBA_BLOCK, :]
            mean_ref[j:j + 1, :] = jnp.sum(blk, axis=0, keepdims=True) * (1.0 / MOBA_BLOCK)

    g_t = _dot3(mean_ref[...], q_ref[...], NT_DIMS)
    tcol = pos0 + qi * tq + lax.broadcasted_iota(jnp.int32, (1, tq), 1)
    own = tcol // MOBA_BLOCK
    brow = lax.broadcasted_iota(jnp.int32, (nbp, 1), 0)
    past = jnp.logical_and(brow < own, brow < nb)
    sc_ref[...] = jnp.where(past, g_t, NEG_INF)
    chosen = jnp.logical_and(past, _rank_select(sc_ref, nb, min(MOBA_TOPK, nb)))
    sel = jnp.where(jnp.logical_or(chosen, brow == own), 1.0, 0.0)
    if sel_ref.shape[0] > nbp:
        sel_ref[...] = jnp.zeros(sel_ref.shape, F32)
    sel_ref[0:nbp, :] = sel


def _moba_gate_prompt(y, nb, t, tq):
    nq = t // tq
    n_full = t // MOBA_BLOCK
    nbp = -(-n_full // SUBLANE) * SUBLANE
    body = functools.partial(_gate_body, tq=tq, nb=n_full, nbp=nbp, pos0=0)
    return pl.pallas_call(
        body,
        grid=(nb, MOBA_HEADS, nq),
        in_specs=[pl.BlockSpec((tq, LANE), lambda b, h, qi: (b * nq + qi, CH_QB + h)),
                  pl.BlockSpec((t, LANE), lambda b, h, qi: (b, CH_KB + h))],
        out_specs=pl.BlockSpec((None, None, LANE, tq), lambda b, h, qi: (b, h, 0, qi)),
        out_shape=jax.ShapeDtypeStruct((nb, MOBA_HEADS, LANE, t), F32),
        scratch_shapes=[pltpu.VMEM((nbp, HEAD_DIM), F32), pltpu.VMEM((nbp, tq), F32)],
        compiler_params=_cparams(("parallel", "parallel", "arbitrary")),
        name="moba_gate_prompt",
    )(y, y)


def _gate_sample_body(q_ref, m_ref, sel_ref, sc_ref, *, nb, pos0):
    tq = q_ref.shape[0]
    own = (pos0 + lax.broadcasted_iota(jnp.int32, (1, tq), 1)) // MOBA_BLOCK
    brow = lax.broadcasted_iota(jnp.int32, (nb, 1), 0)
    past = brow < own
    if sel_ref.shape[1] > nb:
        sel_ref[...] = jnp.zeros(sel_ref.shape, F32)
    for h in range(MOBA_HEADS):
        g_t = _dot3(m_ref[h], q_ref[:, h * HEAD_DIM:(h + 1) * HEAD_DIM], NT_DIMS)
        sc_ref[...] = jnp.where(past, g_t, NEG_INF)
        chosen = jnp.logical_and(past, _rank_select(sc_ref, nb, min(MOBA_TOPK, nb)))
        sel_ref[h, 0:nb, :] = jnp.where(jnp.logical_or(chosen, brow == own), 1.0, 0.0)


def _moba_gate_sample(q_pad, means, pos0):
    nb, _, n_full, _ = means.shape
    assert n_full % SUBLANE == 0
    tq = SUBLANE
    rows = max(n_full, LANE)
    body = functools.partial(_gate_sample_body, nb=n_full, pos0=pos0)
    return pl.pallas_call(
        body,
        grid=(nb,),
        in_specs=[pl.BlockSpec((tq, MOBA_HEADS * HEAD_DIM), lambda b: (b, 0)),
                  pl.BlockSpec((None, MOBA_HEADS, n_full, HEAD_DIM), lambda b: (b, 0, 0, 0))],
        out_specs=pl.BlockSpec((None, MOBA_HEADS, rows, tq), lambda b: (b, 0, 0, 0)),
        out_shape=jax.ShapeDtypeStruct((nb, MOBA_HEADS, rows, tq), F32),
        scratch_shapes=[pltpu.VMEM((n_full, tq), F32)],
        compiler_params=_cparams(("parallel",)),
        name="moba_gate_sample",
    )(q_pad, means)


MEANS_BLOCKS = 4


def _means_body(pt_ref, *refs, ppb, bps):
    c_refs = refs[:bps * ppb]
    o_ref = refs[bps * ppb]
    j = pl.program_id(1)
    for q in range(bps):
        acc = jnp.sum(c_refs[q * ppb][...], axis=0)
        for i in range(1, ppb):
            acc = acc + jnp.sum(c_refs[q * ppb + i][...], axis=0)
        m = acc * (1.0 / MOBA_BLOCK)
        for h in range(MOBA_HEADS):
            o_ref[h, pl.ds(j * bps + q, 1), :] = m[h:h + 1, :]


def _moba_means_sample(cache, li, page_table):
    nb, n_pages = page_table.shape
    page = cache.shape[2]
    ppb = MOBA_BLOCK // page
    n_full = n_pages // ppb
    bps = math.gcd(MEANS_BLOCKS, n_full)
    body = functools.partial(_means_body, ppb=ppb, bps=bps)
    in_specs = [
        pl.BlockSpec((None, None, page, None, MOBA_HEADS, HEAD_DIM),
                     lambda b, j, pt, i=i: (li, pt[b, j * bps * ppb + i], 0, 0, 0, 0))
        for i in range(bps * ppb)
    ]
    return pl.pallas_call(
        body,
        grid_spec=pltpu.PrefetchScalarGridSpec(
            num_scalar_prefetch=1,
            grid=(nb, n_full // bps),
            in_specs=in_specs,
            out_specs=pl.BlockSpec((None, MOBA_HEADS, n_full, HEAD_DIM), lambda b, j, pt: (b, 0, 0, 0)),
        ),
        out_shape=jax.ShapeDtypeStruct((nb, MOBA_HEADS, n_full, HEAD_DIM), F32),
        compiler_params=_cparams(("parallel", "arbitrary")),
        name="moba_means_sample",
    )(page_table, *([cache] * (bps * ppb)))


FLASH_UNIT_ROWS = 256
LOG2_E = 1.4426950408889634
MASK_BIG = 2.0 ** 60


def _flash_body(*refs, tq, tk, nh, blk, window, masked, ur):
    q_ref, k_ref, v_ref = refs[:3]
    pos = 3
    if masked:
        sel_ref = refs[pos]
        pos += 1
    o_ref, qs_ref, kt_ref = refs[pos:pos + 3]
    pos += 1
    units = [(z, r0) for z in range(nh) for r0 in range(0, tq, ur)]
    stat = refs[pos + 2:pos + 2 + 2 * len(units)]
    m_refs, acc_refs = stat[0::2], stat[1::2]
    qi = pl.program_id(2)
    t0 = qi * tq
    scale = HEAD_DIM ** -0.5 * LOG2_E
    if masked:
        bias = ((jnp.transpose(sel_ref[...]) - 1.0) * MASK_BIG).astype(BF)
    for z in range(nh):
        qz = (q_ref[:, z * HEAD_DIM:(z + 1) * HEAD_DIM] * scale).astype(BF)
        qs_ref[z] = jnp.concatenate([qz, bias], axis=1) if masked else qz
    for u in range(len(units)):
        m_refs[u][...] = jnp.full(m_refs[u].shape, NEG_INF, F32)
        acc_refs[u][...] = jnp.zeros(acc_refs[u].shape, F32)
    trow = t0 + lax.broadcasted_iota(jnp.int32, (tq, 1), 0)

    @pl.when(qi == 0)
    def _():
        for c in range(kt_ref.shape[0]):
            kc = k_ref[c * tk:(c + 1) * tk, :]
            if masked:
                prow = c * tk + lax.broadcasted_iota(jnp.int32, (tk, 1), 0)
                jcol = lax.broadcasted_iota(jnp.int32, (1, LANE), 1)
                kc = jnp.concatenate([kc, jnp.where(prow // blk == jcol, 1.0, 0.0).astype(BF)], axis=1)
            kt_ref[c] = kc.T

    def process(c, edge):
        ks = pl.multiple_of(c * tk, tk)
        kt = kt_ref[c]
        vc = jnp.concatenate([v_ref[pl.ds(ks, tk), :], jnp.ones((tk, LANE), BF)], axis=1)
        if edge:
            pcol = ks + lax.broadcasted_iota(jnp.int32, (1, tk), 1)
            valid = pcol <= trow
            if window:
                valid = jnp.logical_and(valid, pcol > trow - window)
        for u, (z, r0) in enumerate(units):
            s = jnp.dot(qs_ref[z, r0:r0 + ur, :], kt, preferred_element_type=F32)
            m_old = m_refs[u][...]
            if edge:
                s = jnp.where(valid[r0:r0 + ur], s, NEG_INF)
                m_new = jnp.maximum(m_old, jnp.max(s, axis=-1, keepdims=True))
                m_use = jnp.where(m_new == NEG_INF, 0.0, m_new)
            else:
                m_new = jnp.maximum(m_old, jnp.max(s, axis=-1, keepdims=True))
                m_use = m_new
            alpha = jnp.exp2(m_old - m_use)
            p = jnp.exp2(s - m_use)
            acc_refs[u][...] = alpha * acc_refs[u][...] + jnp.dot(p.astype(BF), vc, preferred_element_type=F32)
            m_refs[u][...] = m_new

    if masked:
        def pair(c2, carry):
            process(2 * c2, False)
            process(2 * c2 + 1, False)
            return carry

        lax.fori_loop(0, qi // 2, pair, 0)

        @pl.when(qi % 2 == 1)
        def _():
            process(qi - 1, False)

        process(qi, True)
    else:
        c_hi = (t0 + tq + tk - 1) // tk
        c_lo = jnp.maximum(t0 - window + 1, 0) // tk
        lax.fori_loop(c_lo, c_hi, lambda c, carry: (process(c, True), carry)[1], 0)
    for u, (z, r0) in enumerate(units):
        l = acc_refs[u][:, HEAD_DIM:HEAD_DIM + 1]
        o_ref[r0:r0 + ur, z * HEAD_DIM:(z + 1) * HEAD_DIM] = acc_refs[u][:, :HEAD_DIM] / jnp.where(l > 0, l, 1.0)


def _flash(y, y16, sel, nb, t, *, q_ch, k_ch, v_ch, kvh, nh, blk, window, out_rows, tq=512, tk=512):
    tq = min(tq, t)
    tk = min(tk, t)
    nq = t // tq
    masked = sel is not None
    gw = nh * HEAD_DIM
    assert q_ch % nh == 0 and (tq == tk or not masked)
    in_specs = [pl.BlockSpec((tq, gw), lambda b, h, qi: (b * nq + qi, q_ch // nh + h)),
                pl.BlockSpec((t, LANE), lambda b, h, qi: (b, k_ch + h)),
                pl.BlockSpec((t, LANE), lambda b, h, qi: (b, v_ch + h))]
    args = [y, y16, y16]
    if masked:
        in_specs.append(pl.BlockSpec((None, None, LANE, tq), lambda b, h, qi: (b, h, 0, qi)))
        args.append(sel)
    ur = min(FLASH_UNIT_ROWS, tq)
    n_units = nh * (tq // ur)
    kd = (2 if masked else 1) * HEAD_DIM
    body = functools.partial(_flash_body, tq=tq, tk=tk, nh=nh, blk=blk, window=window, masked=masked, ur=ur)
    unit_scratch = [pltpu.VMEM((ur, 1), F32), pltpu.VMEM((ur, 2 * HEAD_DIM), F32)]
    return pl.pallas_call(
        body,
        grid=(nb, kvh, nq),
        in_specs=in_specs,
        out_specs=pl.BlockSpec((tq, gw), lambda b, h, qi: (b * nq + qi, h)),
        out_shape=jax.ShapeDtypeStruct((out_rows, kvh * gw), F32),
        scratch_shapes=[pltpu.VMEM((nh, tq, kd), BF), pltpu.VMEM((t // tk, kd, tk), BF)] + unit_scratch * n_units,
        compiler_params=_cparams(("parallel", "parallel", "arbitrary")),
        name="flash_" + ("win" if window else "blk%d" % blk),
    )(*args)


def _dec_body(tbl_ref, vld_ref, q_ref, *refs, nh, rows, kvh, nblk, r_min):
    k_refs = refs[:nblk]
    v_refs = refs[nblk:2 * nblk]
    kn_ref, vn_ref, o_ref = refs[2 * nblk:2 * nblk + 3]
    b = pl.program_id(0)
    h = pl.program_id(1)
    scale = HEAD_DIM ** -0.5
    qrow = q_ref[pl.ds(b, 1), :] * scale
    zrow = lax.broadcasted_iota(jnp.int32, (SUBLANE, 1), 0)
    qm = jnp.zeros((SUBLANE, HEAD_DIM), F32)
    for z in range(nh):
        qm = jnp.where(zrow == z, qrow[:, z * HEAD_DIM:(z + 1) * HEAD_DIM], qm)
    qb = qm.astype(BF)
    col = lax.broadcasted_iota(jnp.int32, (1, rows * kvh), 1)
    valid = jnp.logical_and(col % kvh == h, col // kvh >= r_min)
    kn = kn_ref[pl.ds(b, 1), :]
    vn = vn_ref[pl.ds(b, 1), :]
    s_new = jnp.sum(qm * kn, axis=-1, keepdims=True)
    scores = []
    m = s_new
    for i in range(nblk):
        kk = k_refs[i][...].reshape(rows * kvh, HEAD_DIM).astype(BF)
        s = lax.dot_general(qb, kk, NT_DIMS, preferred_element_type=F32)
        s = jnp.where(jnp.logical_and(valid, vld_ref[b, h, i] > 0), s, NEG_INF)
        scores.append(s)
        m = jnp.maximum(m, jnp.max(s, axis=-1, keepdims=True))
    p_new = jnp.exp(s_new - m)
    l = p_new
    acc = p_new * vn
    for i in range(nblk):
        p = jnp.exp(scores[i] - m)
        l = l + jnp.sum(p, axis=-1, keepdims=True)
        vv = v_refs[i][...].reshape(rows * kvh, HEAD_DIM).astype(BF)
        acc = acc + jnp.dot(p.astype(BF), vv, preferred_element_type=F32)
    o_ref[...] = acc / l


def _decode_attn(ys, cache, li_fixed, tbl, vld, *, q_ch, kn_ch, vn_ch, nh, rows, kvh, r_min, name):
    nb, _, nblk = tbl.shape
    page = cache.shape[2]
    bpp = page // rows
    gw = nh * HEAD_DIM
    body = functools.partial(_dec_body, nh=nh, rows=rows, kvh=kvh, nblk=nblk, r_min=r_min)

    def kv_spec(kv, i):
        def imap(b, h, tb, vl):
            t = tb[b, h, i]
            return (li_fixed, t // bpp, t % bpp, kv, 0, 0)

        return pl.BlockSpec((None, None, rows, None, kvh, HEAD_DIM), imap)

    nrow = ys.shape[0]
    return pl.pallas_call(
        body,
        grid_spec=pltpu.PrefetchScalarGridSpec(
            num_scalar_prefetch=2,
            grid=(nb, kvh),
            in_specs=[pl.BlockSpec((nrow, gw), lambda b, h, tb, vl: (0, q_ch // nh + h))]
            + [kv_spec(0, i) for i in range(nblk)] + [kv_spec(1, i) for i in range(nblk)]
            + [pl.BlockSpec((nrow, LANE), lambda b, h, tb, vl: (0, kn_ch + h)),
               pl.BlockSpec((nrow, LANE), lambda b, h, tb, vl: (0, vn_ch + h))],
            out_specs=pl.BlockSpec((None, None, SUBLANE, HEAD_DIM), lambda b, h, tb, vl: (b, h, 0, 0)),
        ),
        out_shape=jax.ShapeDtypeStruct((nb, kvh, SUBLANE, HEAD_DIM), F32),
        compiler_params=_cparams(("parallel", "parallel")),
        name=name,
    )(tbl, vld, ys, *([cache] * (2 * nblk)), ys, ys)


def _mix_body(oc_ref, os_ref, ow_ref, ob_ref, soc_ref, sos_ref, sow_ref, sob_ref, gate_ref, o_ref, *, n_p_tiles, n_s):
    i = pl.program_id(0)
    w = NSA_HEADS * HEAD_DIM

    def emit(oc, os_, ow, ob, rows):
        g = jax.nn.sigmoid(gate_ref[0:rows, :])
        for h in range(NSA_HEADS):
            sl = slice(h * HEAD_DIM, (h + 1) * HEAD_DIM)
            o = (g[:, 3 * h:3 * h + 1] * oc[:, sl] + g[:, 3 * h + 1:3 * h + 2] * os_[:, sl]
                 + g[:, 3 * h + 2:3 * h + 3] * ow[:, sl])
            o_ref[0:rows, sl] = o.astype(o_ref.dtype)
        o_ref[0:rows, w:] = ob[...].astype(o_ref.dtype)

    @pl.when(i < n_p_tiles)
    def _():
        emit(oc_ref, os_ref, ow_ref, ob_ref, o_ref.shape[0])

    @pl.when(i >= n_p_tiles)
    def _():
        emit(soc_ref, sos_ref, sow_ref, sob_ref, n_s)


def _mix(prompt_parts, sample_parts, y, tm=256):
    n_prompt = prompt_parts[0].shape[0]
    n_s = sample_parts[0].shape[0]
    m = y.shape[0]
    tm = min(tm, n_prompt)
    assert n_prompt % tm == 0 and m == n_prompt + n_s and n_s <= tm
    n_p_tiles = n_prompt // tm
    wa = NSA_HEADS * HEAD_DIM
    wb = MOBA_HEADS * HEAD_DIM
    pmap = lambda i: (jnp.minimum(i, n_p_tiles - 1), 0)
    body = functools.partial(_mix_body, n_p_tiles=n_p_tiles, n_s=n_s)
    return pl.pallas_call(
        body,
        grid=(n_p_tiles + 1,),
        in_specs=[pl.BlockSpec((tm, wa), pmap)] * 3 + [pl.BlockSpec((tm, wb), pmap)]
        + [pl.BlockSpec((n_s, wa), lambda i: (0, 0))] * 3 + [pl.BlockSpec((n_s, wb), lambda i: (0, 0))]
        + [pl.BlockSpec((tm, LANE), lambda i: (i, CH_GATE))],
        out_specs=pl.BlockSpec((tm, wa + wb), lambda i: (i, 0)),
        out_shape=jax.ShapeDtypeStruct((m, wa + wb), BF),
        compiler_params=_cparams(("parallel",)),
        name="attn_mix",
    )(*prompt_parts, *sample_parts, y)


def _convmix_body(up_ref, hp_ref, uc_ref, hc_ref, pw_ref, ps_ref, dw_ref, dwb_ref, lng_ref, lnb_ref, pww_ref,
                  o_ref, zp_ref, zs_ref, cb_ref, *, ts, nt, zero_first, avail0, rc):
    s = pl.program_id(0)
    hp_rows = hp_ref.shape[0]
    hc_rows = hc_ref.shape[0]
    width = up_ref.shape[1]
    if zero_first:
        keep = jnp.where(s % nt == 0, 0.0, 1.0)
        zp_ref[0:hp_rows, :] = hp_ref[...] * keep
        zs_ref[0, 0:hc_rows, :] = hc_ref[...] * keep
        avail = (s % nt) * ts + avail0
    else:
        zp_ref[0:hp_rows, :] = hp_ref[...]
        zs_ref[0, 0:hc_rows, :] = hc_ref[...]
        avail = avail0
    zp_ref[hp_rows:, :] = up_ref[...]
    zs_ref[0, hc_rows:, :] = uc_ref[...]
    span = hc_rows + ts - SUBLANE
    for k in range(1, SUBLANE):
        zs_ref[k, 0:span, :] = zs_ref[0, k:k + span, :]
    gwidth = width // len(POOL_WINDOWS)
    t_idx = lax.broadcasted_iota(jnp.int32, (ts, 1), 0) + avail + 1
    for gi, w in enumerate(POOL_WINDOWS):
        sl = slice(gi * gwidth, (gi + 1) * gwidth)
        cur = zp_ref[hp_rows:hp_rows + ts, sl]
        acc = cur
        for jj in range(1, w):
            acc = acc + zp_ref[hp_rows - jj:hp_rows - jj + ts, sl]
        cnt = jnp.minimum(t_idx, w).astype(F32)
        yg = acc / cnt - cur
        og = jnp.dot(yg.astype(BF), pw_ref[gi], preferred_element_type=F32) * ps_ref[:, sl]
        o_ref[:, sl] = og.astype(o_ref.dtype)
    base = hc_rows - (CONV_WIDTH - 1)
    for r in range(ts // rc):
        for c in range(width // LANE):
            cs = slice(c * LANE, (c + 1) * LANE)
            acc = jnp.zeros((rc, LANE), F32)
            for jj in range(CONV_WIDTH):
                k = (base + jj) % SUBLANE
                lo = r * rc + base + jj - k
                acc = acc + zs_ref[k, lo:lo + rc, cs] * dw_ref[jj:jj + 1, cs]
            cb_ref[r * rc:(r + 1) * rc, cs] = acc + dwb_ref[:, cs]
    cv = cb_ref[...]
    mu = jnp.mean(cv, axis=-1, keepdims=True)
    xc = cv - mu
    yn = xc * lax.rsqrt(jnp.mean(xc * xc, axis=-1, keepdims=True) + NORM_EPS) * lng_ref[...] + lnb_ref[...]
    act = yn * jax.nn.sigmoid(yn)
    o_ref[:, width:] = jnp.dot(act.astype(BF), pww_ref[...], preferred_element_type=F32).astype(o_ref.dtype)


def _convmix(u_pool, halo_pool, u_conv, halo_conv, prm, *, ts, nt, n_tiles, zero_first, avail0, hp_rows, hc_rows,
             halo_blk):
    pw, ps, dw, dwb, lng, lnb, pww = prm
    width = u_pool.shape[1]
    rc = min(64, ts)
    body = functools.partial(_convmix_body, ts=ts, nt=nt, zero_first=zero_first, avail0=avail0, rc=rc)
    full = lambda a: pl.BlockSpec(a.shape, lambda s: (0,) * a.ndim)
    return pl.pallas_call(
        body,
        grid=(n_tiles,),
        in_specs=[pl.BlockSpec((ts, width), lambda s: (s, 0)),
                  pl.BlockSpec((hp_rows, width), lambda s: (halo_blk(s, hp_rows), 0)),
                  pl.BlockSpec((ts, width), lambda s: (s, 0)),
                  pl.BlockSpec((hc_rows, width), lambda s: (halo_blk(s, hc_rows), 0)),
                  full(pw), full(ps), full(dw), full(dwb), full(lng), full(lnb), full(pww)],
        out_specs=pl.BlockSpec((ts, 2 * width), lambda s: (s, 0)),
        out_shape=jax.ShapeDtypeStruct((n_tiles * ts, 2 * width), BF),
        scratch_shapes=[pltpu.VMEM((hp_rows + ts, width), F32), pltpu.VMEM((SUBLANE, hc_rows + ts, width), F32),
                        pltpu.VMEM((ts, width), F32)],
        compiler_params=_cparams(("parallel",)),
        name="conv_mix",
    )(u_pool, halo_pool, u_conv, halo_conv, pw, ps, dw, dwb, lng, lnb, pww)


def _router_body(x_ref, g_ref, rw_ref, r_ref, xn_ref):
    x = x_ref[...]
    xn = x * lax.rsqrt(jnp.mean(x * x, axis=-1, keepdims=True) + NORM_EPS) * g_ref[...]
    xn_ref[...] = xn.astype(xn_ref.dtype)
    logits = _dot3(xn, rw_ref[...])
    lane = lax.broadcasted_iota(jnp.int32, logits.shape, 1)
    l1 = jnp.where(lane < N_EXPERTS, logits, NEG_INF)
    m1 = jnp.max(l1, axis=-1, keepdims=True)
    i1 = jnp.min(jnp.where(l1 == m1, lane, LANE), axis=-1, keepdims=True)
    l2 = jnp.where(lane == i1, NEG_INF, l1)
    m2 = jnp.max(l2, axis=-1, keepdims=True)
    i2 = jnp.min(jnp.where(l2 == m2, lane, LANE), axis=-1, keepdims=True)
    e = jnp.exp(m2 - m1)
    w1 = 1.0 / (1.0 + e)
    w2 = e / (1.0 + e)
    r_ref[...] = jnp.where(lane == 0, i1.astype(F32),
                           jnp.where(lane == 1, i2.astype(F32), jnp.where(lane == 2, w1, jnp.where(lane == 3, w2, 0.0))))


def _router(x, g, rw, tm=256):
    m, d = x.shape
    rw_pad = jnp.pad(rw, ((0, 0), (0, LANE - rw.shape[1])))
    return pl.pallas_call(
        _router_body,
        grid=(pl.cdiv(m, tm),),
        in_specs=[pl.BlockSpec((tm, d), lambda i: (i, 0)), pl.BlockSpec((1, d), lambda i: (0, 0)),
                  pl.BlockSpec((d, LANE), lambda i: (0, 0))],
        out_specs=[pl.BlockSpec((tm, LANE), lambda i: (i, 0)), pl.BlockSpec((tm, d), lambda i: (i, 0))],
        out_shape=[jax.ShapeDtypeStruct((m, LANE), F32), jax.ShapeDtypeStruct((m, d), BF)],
        compiler_params=_cparams(("parallel",)),
        name="moe_router",
    )(x, g.reshape(1, d), rw_pad)


def _row_copy(src_hbm, row, dst, drow, sem):
    return pltpu.make_async_copy(src_hbm.at[pl.ds(row, 1), :], dst.at[pl.ds(drow, 1), :], sem)


def _gather_rows_body(c0_ref, nc_ref, src_ref, xn_hbm, o_ref, buf_ref, acc_ref, first_ref, sem, *, tc):
    i = pl.program_id(0)
    c0 = c0_ref[i]
    n = nc_ref[i]
    i_next = jnp.minimum(i + 1, pl.num_programs(0) - 1)
    n_next = jnp.where(i + 1 < pl.num_programs(0), nc_ref[i_next], 0)

    def chunk_copy(c, slot):
        return pltpu.make_async_copy(xn_hbm.at[pl.ds(c * tc, tc), :], buf_ref.at[slot], sem.at[slot])

    acc_ref[...] = jnp.zeros(acc_ref.shape, F32)

    @pl.when(i == 0)
    def _():
        first_ref[0] = 0

        @pl.when(n > 0)
        def _():
            chunk_copy(c0, 0).start()

    first = first_ref[0]
    src = src_ref[...]

    def body(k, carry):
        slot = (first + k) % 2
        chunk_copy(c0 + k, slot).wait()

        @pl.when(k + 1 < n)
        def _():
            chunk_copy(c0 + k + 1, 1 - slot).start()

        @pl.when(jnp.logical_and(k + 1 == n, n_next > 0))
        def _():
            chunk_copy(c0_ref[i_next], 1 - slot).start()
            first_ref[0] = 1 - slot

        tok = (c0 + k) * tc + lax.broadcasted_iota(jnp.int32, (1, tc), 1)
        onehot = jnp.where(src == tok, 1.0, 0.0).astype(buf_ref.dtype)
        acc_ref[...] += jnp.dot(onehot, buf_ref[slot], preferred_element_type=F32)
        return carry

    lax.fori_loop(0, n, body, 0)
    o_ref[...] = acc_ref[...].astype(o_ref.dtype)


def _gather_rows(xn, src, c0, nc, tm, tc):
    n_tok, d = xn.shape
    r_tot = src.shape[0]
    body = functools.partial(_gather_rows_body, tc=tc)
    return pl.pallas_call(
        body,
        grid_spec=pltpu.PrefetchScalarGridSpec(
            num_scalar_prefetch=2,
            grid=(r_tot // tm,),
            in_specs=[pl.BlockSpec((tm, 1), lambda i, a, b: (i, 0)), pl.BlockSpec(memory_space=pl.ANY)],
            out_specs=pl.BlockSpec((tm, d), lambda i, a, b: (i, 0)),
            scratch_shapes=[pltpu.VMEM((2, tc, d), xn.dtype), pltpu.VMEM((tm, d), F32), pltpu.SMEM((1,), jnp.int32),
                            pltpu.SemaphoreType.DMA((2,))],
        ),
        out_shape=jax.ShapeDtypeStruct((r_tot, d), xn.dtype),
        compiler_params=_cparams(("arbitrary",)),
        name="moe_gather",
    )(c0, nc, src.reshape(r_tot, 1), xn)


def _combine_body(p1_ref, p2_ref, eo_hbm, x_ref, r_ref, o_ref, ot_ref, b1_ref, b2_ref, sem, *, tm, n_main, n_tail):
    i = pl.program_id(0)
    base = i * tm
    n = jnp.where(i < n_main // tm, tm, n_tail)

    def issue(r, c):
        _row_copy(eo_hbm, p1_ref[base + r], b1_ref, r, sem).start()
        _row_copy(eo_hbm, p2_ref[base + r], b2_ref, r, sem).start()
        return c

    lax.fori_loop(0, n, issue, 0)

    def wait(r, c):
        _row_copy(eo_hbm, 0, b1_ref, r, sem).wait()
        _row_copy(eo_hbm, 0, b2_ref, r, sem).wait()
        return c

    lax.fori_loop(0, n, wait, 0)

    def rows(r):
        rr = r_ref[0:r, :]
        return x_ref[0:r, :] + rr[:, 2:3] * b1_ref[0:r, :] + rr[:, 3:4] * b2_ref[0:r, :]

    @pl.when(i < n_main // tm)
    def _():
        o_ref[...] = rows(tm)

    @pl.when(i == n_main // tm)
    def _():
        ot_ref[...] = rows(n_tail)


def _combine(x, eo, routing, p1, p2, n_main, tm=256):
    m, d = x.shape
    n_tail = m - n_main
    assert n_main % tm == 0 and 0 < n_tail <= tm and n_tail % SUBLANE == 0
    n_full = n_main // tm
    body = functools.partial(_combine_body, tm=tm, n_main=n_main, n_tail=n_tail)
    return pl.pallas_call(
        body,
        grid_spec=pltpu.PrefetchScalarGridSpec(
            num_scalar_prefetch=2,
            grid=(n_full + 1,),
            in_specs=[pl.BlockSpec(memory_space=pl.ANY), pl.BlockSpec((tm, d), lambda i, a, b: (i, 0)),
                      pl.BlockSpec((tm, LANE), lambda i, a, b: (i, 0))],
            out_specs=[pl.BlockSpec((tm, d), lambda i, a, b: (jnp.minimum(i, n_full - 1), 0)),
                       pl.BlockSpec((n_tail, d), lambda i, a, b: (0, 0))],
            scratch_shapes=[pltpu.VMEM((tm, d), F32), pltpu.VMEM((tm, d), F32), pltpu.SemaphoreType.DMA(())],
        ),
        out_shape=[jax.ShapeDtypeStruct((n_main, d), F32), jax.ShapeDtypeStruct((n_tail, d), F32)],
        compiler_params=_cparams(("arbitrary",)),
        name="moe_combine",
    )(p1, p2, eo, x, routing)


def _moe_plan(e_idx, tm, tc):
    n = e_idx.shape[0]
    n_asg = n * TOP_K
    flat_e = e_idx.reshape(-1)
    onehot = (flat_e[:, None] == jnp.arange(N_EXPERTS, dtype=jnp.int32)[None, :]).astype(jnp.int32)
    cnt = jnp.sum(onehot, axis=0)
    rank = jnp.take_along_axis(jnp.cumsum(onehot, axis=0) - onehot, flat_e[:, None], axis=1)[:, 0]
    cnt_p = ((cnt + tm - 1) // tm) * tm
    ends = jnp.cumsum(cnt_p)
    off = ends - cnt_p
    pos = off[flat_e] + rank
    n_tiles = (n_asg + N_EXPERTS * (tm - 1) + tm - 1) // tm
    r_tot = n_tiles * tm
    src = jnp.full((r_tot,), -1, jnp.int32).at[pos].set(jnp.arange(n_asg, dtype=jnp.int32) // TOP_K)
    src_t = src.reshape(n_tiles, tm)
    lo = jnp.min(jnp.where(src_t >= 0, src_t, n), axis=1)
    hi = jnp.max(src_t, axis=1)
    c0 = jnp.where(hi >= 0, lo // tc, 0).astype(jnp.int32)
    nc = jnp.where(hi >= 0, hi // tc + 1 - lo // tc, 0).astype(jnp.int32)
    tile_start = jnp.arange(n_tiles, dtype=jnp.int32) * tm
    tval = (tile_start < ends[-1]).astype(jnp.int32)
    texp = jnp.minimum(jnp.sum((tile_start[:, None] >= ends[None, :]).astype(jnp.int32), axis=1), N_EXPERTS - 1)
    last = jnp.max(jnp.where(tval > 0, texp, 0))
    texp = jnp.where(tval > 0, texp, last)
    pos2 = pos.reshape(n, TOP_K)
    return src, c0, nc, pos2[:, 0], pos2[:, 1], texp, tval


def _moe(x, g, rw, wg, wu, wd, n_main, tm=256):
    n = x.shape[0]
    tc = max(t for t in range(2 * SUBLANE, 513, 2 * SUBLANE) if n % t == 0)
    routing, xn = _router(x, g, rw)
    e_idx = routing[:, :TOP_K].astype(jnp.int32)
    src, c0, nc, p1, p2, texp, tval = _moe_plan(e_idx, tm, tc)
    xs = _gather_rows(xn, src, c0, nc, tm, tc)
    hid = _gmm(xs, [(wg, 0), (wu, 0)], wg.shape[2], epi="swiglu", texp=texp, tval=tval, out_dtype=BF, tm=tm, tn=1024,
               name="moe_up")
    eo = _gmm(hid, [(wd, 0)], wd.shape[2], texp=texp, tval=tval, tm=tm, name="moe_down")
    return _combine(x, eo, routing, p1, p2, n_main)


def _seq_tails(a, nb, t, k):
    return jnp.stack([a[(b + 1) * t - k:(b + 1) * t] for b in range(nb)])


def _attn_layer(x, n_prompt, nb_p, t_p, nb_s, li, prm, caches, page_table):
    (attn_norm, w_in, qk_g, cmp_pe, cmp_w1, cmp_w2, moba_g, w_out, ffn_norm, w_gate, w_up, w_down) = prm
    cache_cmp, cache_sel, state_win, cache_moba = caches
    d = x[0].shape[1]
    past_len = page_table.shape[1] * cache_moba.shape[2]

    g_lo = NSA_CHUNKS * LANE
    g_hi = g_lo + 3 * NSA_HEADS
    assert w_in.shape[1] == g_hi + MOBA_CHUNKS * LANE
    w_moba = w_in[:, g_hi:][None]
    w_gate_cols = jnp.pad(w_in[:, g_lo:g_hi], ((0, 0), (0, LANE - 3 * NSA_HEADS)))[None]
    ones = jnp.ones((HEAD_DIM,), F32)
    zeros = jnp.zeros((HEAD_DIM,), F32)
    gain_a = jnp.concatenate([qk_g[0]] * 8 + [ones] * 4 + [qk_g[2]] * 2 + [ones] * 2 + [qk_g[3]] * 2 + [ones] * 2)[None]
    flag_a = jnp.concatenate([ones] * 8 + [zeros] * 4 + [ones] * 2 + [zeros] * 2 + [ones] * 2 + [zeros] * 2)[None]
    gain_b = jnp.concatenate([moba_g[0]] * 8 + [moba_g[1]] * 8 + [ones] * 8)[None]
    flag_b = jnp.concatenate([ones] * 16 + [zeros] * 8)[None]
    xn = _rmsnorm(x, attn_norm, tm=STREAM_TM)
    ya, ya16 = _gmm(xn, [(w_in[None], 0)], NSA_CHUNKS * LANE, epi="headnorm", gain=gain_a, flag=flag_a,
                    out_dtype=(F32, BF), tm=STREAM_TM, tn=10 * LANE, name="attn_in_nsa")
    yb, yb16 = _gmm(xn, [(w_moba, 0)], MOBA_CHUNKS * LANE, epi="headnorm", gain=gain_b, flag=flag_b,
                    out_dtype=(F32, BF), tm=STREAM_TM, tn=8 * LANE, name="attn_in_moba")
    yg = _gmm(xn, [(w_gate_cols, 0)], LANE, tm=STREAM_TM, tn=LANE, name="attn_in_gate")

    r = NSA_CMP_LEN // NSA_CMP_STRIDE
    wcat = cmp_w1.reshape(2, r, NSA_CMP_STRIDE, HEAD_DIM, NSA_CMP_HIDDEN).transpose(0, 2, 3, 1, 4)
    wcat = wcat.reshape(2, NSA_CMP_STRIDE, HEAD_DIM, r * NSA_CMP_HIDDEN).astype(BF)
    pe_rows = cmp_pe.reshape(2, r, NSA_CMP_STRIDE, HEAD_DIM).transpose(0, 2, 1, 3)
    pe_rows = jnp.pad(pe_rows, ((0, 0), (0, 0), (0, SUBLANE - r), (0, 0))).astype(BF)
    w2 = cmp_w2.astype(BF)
    gk = qk_g[1][None]

    kvc_p = _compress_c2(_compress_c1_prompt(ya, wcat, nb_p, t_p), pe_rows, wcat, w2, gk)
    rows = n_prompt
    oc, sel_a = _cattn(ya, 0, 0, kvc_p, nb_p, t_p, min(512, t_p), t_p, 0, rows)
    sel_b = _moba_gate_prompt(yb, nb_p, t_p, min(1024, t_p))
    os_ = _flash(ya, ya16, sel_a, nb_p, t_p, q_ch=CH_QA, k_ch=CH_KS, v_ch=CH_VS, kvh=NSA_KV, nh=NSA_HPG,
                 blk=NSA_SEL_BLOCK, window=0, out_rows=rows)
    ow = _flash(ya, ya16, None, nb_p, t_p, q_ch=CH_QA, k_ch=CH_KW, v_ch=CH_VW, kvh=NSA_KV, nh=NSA_HPG, blk=0,
                window=NSA_WINDOW, out_rows=rows)
    ob = _flash(yb, yb16, sel_b, nb_p, t_p, q_ch=CH_QB, k_ch=CH_KB, v_ch=CH_VB, kvh=MOBA_HEADS, nh=1,
                blk=MOBA_BLOCK, window=0, out_rows=rows)

    n_pad = x[2] - n_prompt
    ys = ya[n_prompt:]
    ysb = yb[n_prompt:]
    kvc_s = _compress_c2(_compress_c1_sample(cache_cmp, li, page_table, wcat), pe_rows, wcat, w2, gk)
    q_pad = jnp.zeros((nb_s, SUBLANE, NSA_HEADS * HEAD_DIM), F32).at[:, 0].set(ys[:nb_s, :NSA_HEADS * HEAD_DIM])
    oc_s, sel_s = _cattn_sample(q_pad.reshape(nb_s * SUBLANE, -1), kvc_s, nb_s, past_len + 1, past_len)
    oc_s = oc_s.reshape(nb_s, SUBLANE, -1)[:, 0]
    n_sel = (past_len + 1) // NSA_SEL_BLOCK
    k_sel = min(NSA_SEL_COUNT - 1, n_sel)
    mask_a = sel_s[:, :, 0, :n_sel]
    idx_a = jnp.argsort(-mask_a, axis=-1, stable=True)[..., :k_sel].astype(jnp.int32)
    vld_a = (jnp.take_along_axis(mask_a, idx_a, axis=-1) > 0).astype(jnp.int32)
    page = cache_sel.shape[2]
    bpp = page // NSA_SEL_BLOCK
    pt_b = page_table[:, None, :]
    tbl_a = jnp.take_along_axis(jnp.broadcast_to(pt_b, (nb_s, NSA_KV, pt_b.shape[-1])), idx_a // bpp, axis=-1) * bpp + idx_a % bpp
    os_s = _decode_attn(ys, cache_sel, li, tbl_a.astype(jnp.int32), vld_a, q_ch=CH_QA, kn_ch=CH_KS, vn_ch=CH_VS,
                        nh=NSA_HPG, rows=NSA_SEL_BLOCK, kvh=NSA_KV, r_min=0,
                        name="dec_sel")
    win_buf = state_win.shape[2]
    tbl_w = jnp.broadcast_to(jnp.arange(nb_s, dtype=jnp.int32)[:, None, None], (nb_s, NSA_KV, 1))
    ow_s = _decode_attn(ys, state_win, li, tbl_w, jnp.ones_like(tbl_w), q_ch=CH_QA, kn_ch=CH_KW, vn_ch=CH_VW,
                        nh=NSA_HPG, rows=win_buf, kvh=NSA_KV, r_min=max(0, win_buf - (NSA_WINDOW - 1)),
                        name="dec_win")
    means = _moba_means_sample(cache_moba, li, page_table)
    qb_pad = jnp.zeros((nb_s, SUBLANE, MOBA_HEADS * HEAD_DIM), F32).at[:, 0].set(
        ysb[:nb_s, CH_QB * LANE:(CH_QB + MOBA_HEADS) * LANE])
    sel_m = _moba_gate_sample(qb_pad.reshape(nb_s * SUBLANE, -1), means, past_len)
    n_full = means.shape[2]
    k_top = min(MOBA_TOPK, n_full)
    mask_b = sel_m[:, :, :n_full, 0]
    idx_b = jnp.argsort(-mask_b, axis=-1, stable=True)[..., :k_top].astype(jnp.int32)
    vld_b = (jnp.take_along_axis(mask_b, idx_b, axis=-1) > 0).astype(jnp.int32)
    ppb = MOBA_BLOCK // page
    pg_b = (idx_b[..., None] * ppb + jnp.arange(ppb, dtype=jnp.int32)).reshape(nb_s, MOBA_HEADS, k_top * ppb)
    tbl_b = jnp.take_along_axis(jnp.broadcast_to(pt_b, (nb_s, MOBA_HEADS, pt_b.shape[-1])), pg_b, axis=-1)
    vld_b = jnp.repeat(vld_b, ppb, axis=-1)
    ob_s = _decode_attn(ysb, cache_moba, li, tbl_b.astype(jnp.int32), vld_b, q_ch=CH_QB, kn_ch=CH_KB, vn_ch=CH_VB,
                        nh=1, rows=page, kvh=MOBA_HEADS, r_min=0,
                        name="dec_moba")

    def rows_s(part, nh):
        return jnp.pad(part[:, :, :nh].reshape(nb_s, -1), ((0, n_pad - nb_s), (0, 0)))

    sample_parts = (jnp.pad(oc_s, ((0, n_pad - nb_s), (0, 0))), rows_s(os_s, NSA_HPG), rows_s(ow_s, NSA_HPG),
                    rows_s(ob_s, 1))
    mixed = _mix((oc, os_, ow, ob), sample_parts, yg)
    x = _gmm(mixed, [(w_out[None], 0)], d, resid=x, tm=STREAM_TM, tn=1024, name="attn_out")
    hid = _gmm(_rmsnorm(x, ffn_norm), [(w_gate[None], 0), (w_up[None], 0)], w_gate.shape[1], epi="swiglu",
               out_dtype=BF, tm=STREAM_TM, tn=512, name="ffn_up")
    x = _gmm(hid, [(w_down[None], 0)], d, resid=x, tm=STREAM_TM, tn=512, name="ffn_down")

    def rows_of(lo_ch, n_ch, y=ya):
        return y[:, lo_ch * LANE:(lo_ch + n_ch) * LANE]

    def split(a, kvh):
        ap = a[:n_prompt].reshape(nb_p, t_p, 2, kvh, HEAD_DIM)
        as_ = a[n_prompt:n_prompt + nb_s].reshape(nb_s, 1, 2, kvh, HEAD_DIM)
        return ap, as_

    cmp_p, cmp_s = split(rows_of(CH_KC, 4), NSA_KV)
    sel_p, sel_s_rows = split(rows_of(CH_KS, 4), NSA_KV)
    win_rows = rows_of(CH_KW, 4)
    win_s = win_rows[n_prompt:n_prompt + nb_s].reshape(nb_s, 1, 2, NSA_KV, HEAD_DIM)
    moba_p, moba_s = split(rows_of(CH_KB, 16, yb), MOBA_HEADS)
    keep_p = min(NSA_WINDOW, t_p)
    new_win_p = _seq_tails(win_rows, nb_p, t_p, keep_p).reshape(nb_p, keep_p, 2, NSA_KV, HEAD_DIM)
    win_all = jnp.concatenate([state_win[li], win_s], axis=1)
    keep = min(NSA_WINDOW, past_len + 1)
    new_win_s = win_all[:, win_all.shape[1] - keep:]
    return x, (cmp_p, cmp_s, sel_p, sel_s_rows, new_win_p, new_win_s, moba_p, moba_s)


def _conv_layer(x, n_prompt, nb_p, t_p, nb_s, prm, states):
    (conv_norm, w_in, pool_w, pool_scale, dw, dw_b, ln_g, ln_b, pw, w_out, moe_norm, router_w, wg, wu, wd) = prm
    state_pool, state_conv = states
    d = x.shape[1]
    width = pool_w.shape[0] * pool_w.shape[1]
    xn = _rmsnorm(x, conv_norm)
    w3 = w_in[None]
    u_pool = _gmm(xn, [(w3, 0)], width, tm=512, tn=width, name="conv_in_pool")
    u_conv = _gmm(xn, [(w3, width), (w3, 2 * width)], width, epi="glu", tm=512, tn=512, name="conv_in_glu")
    prm_mix = (pool_w.astype(BF), pool_scale[None], jnp.pad(dw, ((0, 1), (0, 0))), dw_b[None], ln_g[None], ln_b[None],
               pw.astype(BF))
    pool_buf = max(POOL_WINDOWS) - 1
    conv_buf = CONV_WIDTH - 1
    hp_rows, hc_rows = 16, 32
    ts = min(256, t_p)
    nt = t_p // ts
    mixed_p = _convmix(u_pool, u_pool, u_conv, u_conv, prm_mix, ts=ts, nt=nt, n_tiles=nb_p * nt, zero_first=True,
                       avail0=0, hp_rows=hp_rows, hc_rows=hc_rows,
                       halo_blk=lambda s, hr: jnp.maximum(s * (ts // hr) - 1, 0))
    rows = x.shape[0]
    us_pool = jnp.zeros((nb_s, SUBLANE, width), F32).at[:, 0].set(u_pool[n_prompt:n_prompt + nb_s])
    us_conv = jnp.zeros((nb_s, SUBLANE, width), F32).at[:, 0].set(u_conv[n_prompt:n_prompt + nb_s])
    hs_pool = jnp.pad(state_pool, ((0, 0), (hp_rows - pool_buf, 0), (0, 0))).reshape(nb_s * hp_rows, width)
    hs_conv = jnp.pad(state_conv, ((0, 0), (hc_rows - conv_buf, 0), (0, 0))).reshape(nb_s * hc_rows, width)
    mixed_s = _convmix(us_pool.reshape(nb_s * SUBLANE, width), hs_pool, us_conv.reshape(nb_s * SUBLANE, width), hs_conv,
                       prm_mix, ts=SUBLANE, nt=1, n_tiles=nb_s, zero_first=False, avail0=pool_buf, hp_rows=hp_rows,
                       hc_rows=hc_rows, halo_blk=lambda s, hr: s)
    mixed_s = mixed_s.reshape(nb_s, SUBLANE, 2 * width)[:, 0]
    mixed = (mixed_p, jnp.pad(mixed_s, ((0, STREAM_TM - nb_s), (0, 0))), rows)
    x = _gmm(mixed, [(w_out[None], 0)], d, resid=x, tm=STREAM_TM, tn=1024, name="conv_out")
    x = _moe(x, moe_norm, router_w, wg, wu, wd, n_prompt)

    assert t_p >= conv_buf and t_p >= pool_buf
    new_pool_p = _seq_tails(u_pool, nb_p, t_p, pool_buf)
    new_conv_p = _seq_tails(u_conv, nb_p, t_p, conv_buf)
    new_pool_s = jnp.concatenate([state_pool, u_pool[n_prompt:n_prompt + nb_s][:, None]], axis=1)[:, 1:]
    new_conv_s = jnp.concatenate([state_conv, u_conv[n_prompt:n_prompt + nb_s][:, None]], axis=1)[:, 1:]
    return x, (new_pool_p, new_pool_s, new_conv_p, new_conv_s)


def kernel(x_prompt, x_sample, cache_nsa_cmp, cache_nsa_sel, state_nsa_win, cache_moba, state_pool, state_conv, page_table, attn_norm, w_attn_in, nsa_qk_norm, nsa_cmp_pe, nsa_cmp_w1, nsa_cmp_w2, moba_qk_norm, w_attn_out, ffn_norm, ffn_w_gate, ffn_w_up, ffn_w_down, conv_norm, w_conv_in, pool_w, pool_scale, conv_dw, conv_dw_b, conv_ln_g, conv_ln_b, conv_pw, w_conv_out, moe_norm, router_w, moe_w_gate, moe_w_up, moe_w_down):
    nb_p, t_p, d = x_prompt.shape
    nb_s = x_sample.shape[0]
    n_prompt = nb_p * t_p
    n_pad = 2 * SUBLANE
    assert x_sample.shape[1] == 1 and nb_s <= n_pad
    x = (x_prompt.reshape(n_prompt, d), jnp.pad(x_sample.reshape(nb_s, d), ((0, STREAM_TM - nb_s), (0, 0))),
         n_prompt + n_pad)
    li = 0
    prm_a = (attn_norm[li], w_attn_in[li], nsa_qk_norm[li], nsa_cmp_pe[li], nsa_cmp_w1[li], nsa_cmp_w2[li],
             moba_qk_norm[li], w_attn_out[li], ffn_norm[li], ffn_w_gate[li], ffn_w_up[li], ffn_w_down[li])
    x, attn_new = _attn_layer(x, n_prompt, nb_p, t_p, nb_s, li, prm_a,
                              (cache_nsa_cmp, cache_nsa_sel, state_nsa_win, cache_moba), page_table)
    prm_c = (conv_norm[li], w_conv_in[li], pool_w[li], pool_scale[li], conv_dw[li], conv_dw_b[li], conv_ln_g[li],
             conv_ln_b[li], conv_pw[li], w_conv_out[li], moe_norm[li], router_w[li], moe_w_gate[li], moe_w_up[li],
             moe_w_down[li])
    x, conv_new = _conv_layer(x, n_prompt, nb_p, t_p, nb_s, prm_c, (state_pool[li], state_conv[li]))
    cmp_p, cmp_s, sel_p, sel_s, win_p, win_s, moba_p, moba_s = attn_new
    pool_p, pool_s, conv_p, conv_s = conv_new
    y_p = x[0].reshape(nb_p, t_p, d)
    y_s = x[1][:nb_s].reshape(nb_s, 1, d)
    st = lambda a: a[None]
    return (y_p, y_s, st(cmp_p), st(cmp_s), st(sel_p), st(sel_s), st(win_p), st(win_s), st(moba_p), st(moba_s),
            st(pool_p), st(pool_s), st(conv_p), st(conv_s))
```

```python
import functools
import math

import jax
import jax.numpy as jnp
from jax import lax
from jax.experimental import pallas as pl
from jax.experimental.pallas import tpu as pltpu

F32 = jnp.float32
BF = jnp.bfloat16
NEG_INF = float("-inf")

HEAD_DIM = 128
NORM_EPS = 1e-6
NSA_HEADS = 8
NSA_KV = 2
NSA_HPG = NSA_HEADS // NSA_KV
NSA_CMP_LEN = 32
NSA_CMP_STRIDE = 16
NSA_CMP_HIDDEN = 2 * HEAD_DIM
NSA_SEL_BLOCK = 64
NSA_SEL_COUNT = 16
NSA_WINDOW = 512
NSA_SEL_FORCE = 1.0e4
MOBA_HEADS = 8
MOBA_BLOCK = 256
MOBA_TOPK = 3
POOL_WINDOWS = (2, 4, 8, 16)
CONV_WIDTH = 31
N_EXPERTS = 8
TOP_K = 2

LANE = 128
SUBLANE = 8
VMEM_LIMIT = 58 * 1024 * 1024
STREAM_TM = 512

CH_QA, CH_KC, CH_VC, CH_KS, CH_VS, CH_KW, CH_VW = 0, 8, 10, 12, 14, 16, 18
NSA_CHUNKS = 20
CH_QB, CH_KB, CH_VB = 0, 8, 16
MOBA_CHUNKS = 24
CH_GATE = 0

NT_DIMS = (((1,), (1,)), ((), ()))


def _cparams(sem):
    return pltpu.CompilerParams(dimension_semantics=sem, vmem_limit_bytes=VMEM_LIMIT)


def _split_bf16(a):
    hi = a.astype(BF)
    lo = (a - hi.astype(F32)).astype(BF)
    return hi, lo


def _dot3(a, b, dims=(((1,), (0,)), ((), ()))):
    ah, al = _split_bf16(a)
    bh, bl = _split_bf16(b)
    d = lambda x, y: lax.dot_general(x, y, dims, preferred_element_type=F32)
    return d(ah, bh) + d(ah, bl) + d(al, bh)


def _masked_softmax(s, mask, axis):
    s = jnp.where(mask, s, NEG_INF)
    m = jnp.max(s, axis=axis, keepdims=True)
    m = jnp.where(m == NEG_INF, 0.0, m)
    p = jnp.exp(s - m)
    d = jnp.sum(p, axis=axis, keepdims=True)
    return p / jnp.where(d > 0, d, 1.0)


def _split_stream(a, tm):
    if not isinstance(a, tuple):
        return [a], [lambda i: i], None, a.shape[0]
    main, tail, m = a
    n_full = main.shape[0] // tm
    assert n_full * tm == main.shape[0] and tail.shape[0] == tm and n_full * tm < m <= (n_full + 1) * tm
    return [main, tail], [lambda i: jnp.minimum(i, n_full - 1), lambda i: 0], n_full, m


def _pick_tile(refs, i, n_full):
    return refs[0][...] if n_full is None else jnp.where(i < n_full, refs[0][...], refs[1][...])


def _rmsnorm_body(*refs, n_full):
    x = _pick_tile(refs[:-2], pl.program_id(0), n_full)
    g_ref, o_ref = refs[-2:]
    ms = jnp.mean(x * x, axis=-1, keepdims=True)
    o_ref[...] = (x * lax.rsqrt(ms + NORM_EPS) * g_ref[...]).astype(o_ref.dtype)


def _rmsnorm(x, g, tm=256):
    arrs, maps, n_full, m = _split_stream(x, tm)
    d = arrs[0].shape[1]
    return pl.pallas_call(
        functools.partial(_rmsnorm_body, n_full=n_full),
        grid=(pl.cdiv(m, tm),),
        in_specs=[pl.BlockSpec((tm, d), lambda i, f=f: (f(i), 0)) for f in maps] + [pl.BlockSpec((1, d), lambda i: (0, 0))],
        out_specs=pl.BlockSpec((tm, d), lambda i: (i, 0)),
        out_shape=jax.ShapeDtypeStruct((m, d), BF),
        compiler_params=_cparams(("parallel",)),
        name="rmsnorm",
    )(*arrs, g.reshape(1, d))


def _gmm_body(texp_ref, tval_ref, *refs, n_x, x_full, n_w, cast_w, w_t, epi, n_r, r_full, tn, n_o):
    mm = (lambda x, w: lax.dot_general(x, w, NT_DIMS, preferred_element_type=F32)) if w_t else (
        lambda x, w: jnp.dot(x, w, preferred_element_type=F32))
    x_refs = refs[:n_x]
    w_refs = refs[n_x:n_x + n_w]
    pos = n_x + n_w
    if epi == "headnorm":
        gain_ref, flag_ref = refs[pos], refs[pos + 1]
        pos += 2
    resid_refs = refs[pos:pos + n_r]
    pos += n_r
    o_refs = refs[pos:pos + n_o]
    wb_refs = refs[pos + n_o:pos + n_o + n_w] if cast_w else w_refs

    i = pl.program_id(1)
    if cast_w:
        changed = jnp.logical_or(i == 0, texp_ref[i] != texp_ref[jnp.maximum(i - 1, 0)])

        @pl.when(changed)
        def _():
            for k in range(n_w):
                wb_refs[k][...] = w_refs[k][...].astype(BF)

    @pl.when(tval_ref[i] == 0)
    def _():
        for o in o_refs:
            o[...] = jnp.zeros(o.shape, o.dtype)

    @pl.when(tval_ref[i] > 0)
    def _():
        x = _pick_tile(x_refs, i, x_full)
        a = mm(x, wb_refs[0][...])
        if epi == "swiglu":
            b = mm(x, wb_refs[1][...])
            y = a * jax.nn.sigmoid(a) * b
        elif epi == "glu":
            b = mm(x, wb_refs[1][...])
            y = a * jax.nn.sigmoid(b)
        else:
            y = a
        if n_r:
            y = y + _pick_tile(resid_refs, i, r_full)
        if epi == "headnorm":
            for c in range(tn // LANE):
                sl = slice(c * LANE, (c + 1) * LANE)
                yc = y[:, sl]
                r = lax.rsqrt(jnp.mean(yc * yc, axis=-1, keepdims=True) + NORM_EPS)
                f = flag_ref[:, sl]
                yn = yc * (f * r + (1.0 - f)) * gain_ref[:, sl]
                for o in o_refs:
                    o[:, sl] = yn.astype(o.dtype)
        else:
            for o in o_refs:
                o[...] = y.astype(o.dtype)


def _gmm(x, ws, n_out, *, epi="none", texp=None, tval=None, resid=None, gain=None, flag=None,
         out_dtype=F32, tm=256, tn=512, w_t=False, name="gmm"):
    x_arrs, x_maps, x_full, m = _split_stream(x, tm)
    kdim = x_arrs[0].shape[1]
    tn = min(tn, n_out)
    n_m = pl.cdiv(m, tm)
    n_n = n_out // tn
    assert n_n * tn == n_out
    if texp is None:
        texp = jnp.zeros((n_m,), jnp.int32)
        tval = jnp.ones((n_m,), jnp.int32)
    n_w = len(ws)
    cast_w = ws[0][0].dtype != BF
    in_specs = [pl.BlockSpec((tm, kdim), lambda j, i, te, tv, f=f: (f(i), 0)) for f in x_maps]
    args = list(x_arrs)
    for w, off in ws:
        assert off % tn == 0 and w.shape[2 if w_t else 1] == kdim and (w.dtype != BF) == cast_w
        ob = off // tn
        if w_t:
            in_specs.append(pl.BlockSpec((None, tn, kdim), lambda j, i, te, tv, ob=ob: (te[i], j + ob, 0)))
        else:
            in_specs.append(pl.BlockSpec((None, kdim, tn), lambda j, i, te, tv, ob=ob: (te[i], 0, j + ob)))
        args.append(w)
    if epi == "headnorm":
        in_specs += [pl.BlockSpec((1, tn), lambda j, i, te, tv: (0, j))] * 2
        args += [gain, flag]
    n_r, r_full = 0, None
    if resid is not None:
        r_arrs, r_maps, r_full, r_m = _split_stream(resid, tm)
        assert r_m == m
        n_r = len(r_arrs)
        in_specs += [pl.BlockSpec((tm, tn), lambda j, i, te, tv, f=f: (f(i), j)) for f in r_maps]
        args += r_arrs
    dtypes = out_dtype if isinstance(out_dtype, tuple) else (out_dtype,)
    body = functools.partial(_gmm_body, n_x=len(x_arrs), x_full=x_full, n_w=n_w, cast_w=cast_w, w_t=w_t, epi=epi,
                             n_r=n_r, r_full=r_full, tn=tn, n_o=len(dtypes))
    outs = pl.pallas_call(
        body,
        grid_spec=pltpu.PrefetchScalarGridSpec(
            num_scalar_prefetch=2,
            grid=(n_n, n_m),
            in_specs=in_specs,
            out_specs=[pl.BlockSpec((tm, tn), lambda j, i, te, tv: (i, j)) for _ in dtypes],
            scratch_shapes=[pltpu.VMEM((tn, kdim) if w_t else (kdim, tn), BF) for _ in range(n_w if cast_w else 0)],
        ),
        out_shape=[jax.ShapeDtypeStruct((m, n_out), dt) for dt in dtypes],
        compiler_params=_cparams(("arbitrary", "arbitrary")),
        name=name,
    )(texp, tval, *args)
    return outs if isinstance(out_dtype, tuple) else outs[0]


def _c1_body(x_ref, w_ref, o_ref):
    nsb = x_ref.shape[0] // NSA_CMP_STRIDE
    acc = jnp.zeros((nsb, 2 * NSA_CMP_HIDDEN), F32)
    for s in range(NSA_CMP_STRIDE):
        xs = x_ref[pl.ds(s, nsb, stride=NSA_CMP_STRIDE), :]
        acc = acc + jnp.dot(xs.astype(BF), w_ref[s], preferred_element_type=F32)
    o_ref[...] = acc


def _compress_c1_prompt(y, wcat, nb, t):
    nsub = t // NSA_CMP_STRIDE
    return pl.pallas_call(
        _c1_body,
        grid=(nb, 2, NSA_KV),
        in_specs=[pl.BlockSpec((t, LANE), lambda b, kv, g: (b, CH_KC + 2 * kv + g)),
                  pl.BlockSpec((None, NSA_CMP_STRIDE, HEAD_DIM, 2 * NSA_CMP_HIDDEN), lambda b, kv, g: (kv, 0, 0, 0))],
        out_specs=pl.BlockSpec((None, None, None, nsub, 2 * NSA_CMP_HIDDEN), lambda b, kv, g: (b, kv, g, 0, 0)),
        out_shape=jax.ShapeDtypeStruct((nb, 2, NSA_KV, nsub, 2 * NSA_CMP_HIDDEN), F32),
        compiler_params=_cparams(("parallel", "parallel", "parallel")),
        name="cmp_c1_prompt",
    )(y, wcat)


C1_PAGES = 16


def _c1_sample_body(pt_ref, *refs, n_in, page):
    x_refs = refs[:n_in]
    w_ref, o_ref, scr = refs[n_in:n_in + 3]
    nsb = page // NSA_CMP_STRIDE
    rows = n_in * nsb
    for kv in range(2):
        acc = jnp.zeros((rows * NSA_KV, 2 * NSA_CMP_HIDDEN), F32)

        def rows_at(s):
            parts = [xr[pl.ds(s, nsb, stride=NSA_CMP_STRIDE), kv, :, :].reshape(nsb * NSA_KV, HEAD_DIM) for xr in x_refs]
            return (parts[0] if n_in == 1 else jnp.concatenate(parts, axis=0)).astype(BF)

        for s in range(0, NSA_CMP_STRIDE, 2):
            xs = jnp.concatenate([rows_at(s), rows_at(s + 1)], axis=1)
            w = w_ref[kv, s:s + 2].reshape(2 * HEAD_DIM, 2 * NSA_CMP_HIDDEN)
            acc = acc + jnp.dot(xs, w, preferred_element_type=F32)
        for c in range(scr.shape[0]):
            scr[c] = acc[:, c * LANE:(c + 1) * LANE]
        for g in range(NSA_KV):
            for c in range(scr.shape[0]):
                o_ref[kv, g, :, c * LANE:(c + 1) * LANE] = scr[c, pl.ds(g, rows, stride=NSA_KV), :]


def _compress_c1_sample(cache, li, page_table, wcat):
    nb, n_pages = page_table.shape
    page = cache.shape[2]
    npp = math.gcd(C1_PAGES, n_pages)
    nsb = page // NSA_CMP_STRIDE
    nsub = n_pages * nsb
    body = functools.partial(_c1_sample_body, n_in=npp, page=page)
    in_specs = [
        pl.BlockSpec((None, None, page, 2, NSA_KV, HEAD_DIM), lambda b, j, pt, i=i: (li, pt[b, j * npp + i], 0, 0, 0, 0))
        for i in range(npp)
    ]
    in_specs.append(pl.BlockSpec(wcat.shape, lambda b, j, pt: (0, 0, 0, 0)))
    return pl.pallas_call(
        body,
        grid_spec=pltpu.PrefetchScalarGridSpec(
            num_scalar_prefetch=1,
            grid=(nb, n_pages // npp),
            in_specs=in_specs,
            out_specs=pl.BlockSpec((None, 2, NSA_KV, npp * nsb, 2 * NSA_CMP_HIDDEN), lambda b, j, pt: (b, 0, 0, j, 0)),
            scratch_shapes=[pltpu.VMEM((2 * NSA_CMP_HIDDEN // LANE, npp * nsb * NSA_KV, LANE), F32)],
        ),
        out_shape=jax.ShapeDtypeStruct((nb, 2, NSA_KV, nsub, 2 * NSA_CMP_HIDDEN), F32),
        compiler_params=_cparams(("parallel", "parallel")),
        name="cmp_c1_sample",
    )(page_table, *([cache] * npp), wcat)


def _c2_body(p_ref, pe_ref, wcat_ref, w2_ref, g_ref, o_ref, *, nsub, n_c):
    kv = pl.program_id(1)
    pep = jnp.zeros((SUBLANE, 2 * NSA_CMP_HIDDEN), F32)
    for s in range(NSA_CMP_STRIDE):
        pep = pep + jnp.dot(pe_ref[s], wcat_ref[s], preferred_element_type=F32)
    bias = pep[0:1, :NSA_CMP_HIDDEN] + pep[1:2, NSA_CMP_HIDDEN:]
    nxt = pltpu.roll(p_ref[:, NSA_CMP_HIDDEN:], nsub - 1, 0)
    h = p_ref[:, :NSA_CMP_HIDDEN] + nxt + bias
    h = h * jax.nn.sigmoid(h)
    o = jnp.dot(h.astype(BF), w2_ref[...], preferred_element_type=F32)
    r = lax.rsqrt(jnp.mean(o * o, axis=-1, keepdims=True) + NORM_EPS)
    o = jnp.where(kv == 0, o * r * g_ref[...], o)
    row = lax.broadcasted_iota(jnp.int32, (nsub, 1), 0)
    o_ref[...] = jnp.where(row < n_c, o, 0.0)


def _compress_c2(p, pe_rows, wcat, w2, gk):
    nb, _, _, nsub, _ = p.shape
    n_c = nsub - NSA_CMP_LEN // NSA_CMP_STRIDE + 1
    body = functools.partial(_c2_body, nsub=nsub, n_c=n_c)
    return pl.pallas_call(
        body,
        grid=(nb, 2, NSA_KV),
        in_specs=[pl.BlockSpec((None, None, None, nsub, 2 * NSA_CMP_HIDDEN), lambda b, kv, g: (b, kv, g, 0, 0)),
                  pl.BlockSpec((None, NSA_CMP_STRIDE, SUBLANE, HEAD_DIM), lambda b, kv, g: (kv, 0, 0, 0)),
                  pl.BlockSpec((None, NSA_CMP_STRIDE, HEAD_DIM, 2 * NSA_CMP_HIDDEN), lambda b, kv, g: (kv, 0, 0, 0)),
                  pl.BlockSpec((None, NSA_CMP_HIDDEN, HEAD_DIM), lambda b, kv, g: (kv, 0, 0)),
                  pl.BlockSpec((1, HEAD_DIM), lambda b, kv, g: (0, 0))],
        out_specs=pl.BlockSpec((None, None, None, nsub, HEAD_DIM), lambda b, kv, g: (b, kv, g, 0, 0)),
        out_shape=jax.ShapeDtypeStruct((nb, 2, NSA_KV, nsub, HEAD_DIM), F32),
        compiler_params=_cparams(("parallel", "parallel", "parallel")),
        name="cmp_c2",
    )(p, pe_rows, wcat, w2, gk)


def _rank_select(sc_ref, n_iter, k):
    score = sc_ref[...]
    brow = lax.broadcasted_iota(jnp.int32, score.shape, 0)

    def body(j, rank):
        r = sc_ref[pl.ds(j, 1), :]
        beats = jnp.logical_or(r > score, jnp.logical_and(r == score, j < brow))
        return rank + jnp.where(beats, 1.0, 0.0)

    rank = lax.fori_loop(0, n_iter, body, jnp.zeros(score.shape, F32))
    return rank < k


def _cattn_body(q_ref, kc_ref, vc_ref, oc_ref, sel_ref, sc_ref, *, tq, ncp, n_c, ns, nsp, pos0, ksel):
    t0 = pos0 + pl.program_id(2) * tq
    scale = HEAD_DIM ** -0.5
    kc = kc_ref[...].astype(BF)
    vc = vc_ref[...].astype(BF)
    trow = t0 + lax.broadcasted_iota(jnp.int32, (tq, 1), 0)
    ncol = lax.broadcasted_iota(jnp.int32, (1, ncp), 1)
    valid = jnp.logical_and(ncol * NSA_CMP_STRIDE + (NSA_CMP_LEN - 1) <= trow, ncol < n_c)
    tcol = t0 + lax.broadcasted_iota(jnp.int32, (1, tq), 1)
    nrow = lax.broadcasted_iota(jnp.int32, (ncp, 1), 0)
    valid_t = jnp.logical_and(nrow * NSA_CMP_STRIDE + (NSA_CMP_LEN - 1) <= tcol, nrow < n_c)
    psum_t = jnp.zeros((ncp, tq), F32)
    for z in range(NSA_HPG):
        sl = slice(z * HEAD_DIM, (z + 1) * HEAD_DIM)
        q = (q_ref[:, sl] * scale).astype(BF)
        s = lax.dot_general(q, kc, NT_DIMS, preferred_element_type=F32)
        p = _masked_softmax(s, valid, -1)
        oc_ref[:, sl] = jnp.dot(p.astype(BF), vc, preferred_element_type=F32)
        s_t = lax.dot_general(kc, q, NT_DIMS, preferred_element_type=F32)
        psum_t = psum_t + _masked_softmax(s_t, valid_t, 0)
    r = NSA_SEL_BLOCK // NSA_CMP_STRIDE
    brow = lax.broadcasted_iota(jnp.int32, (nsp, 1), 0)
    lo = r * brow - 1
    inside = jnp.logical_and(ncol >= lo, ncol <= lo + r)
    edge = jnp.logical_or(ncol == lo, ncol == lo + r)
    m_t = jnp.where(inside, jnp.where(edge, 0.5, 1.0), 0.0).astype(BF)
    hi = psum_t.astype(BF)
    mid = (psum_t - hi.astype(F32)).astype(BF)
    low = (psum_t - hi.astype(F32) - mid.astype(F32)).astype(BF)
    imp_t = (jnp.dot(m_t, hi, preferred_element_type=F32) + jnp.dot(m_t, mid, preferred_element_type=F32)
             + jnp.dot(m_t, low, preferred_element_type=F32))
    own = tcol // NSA_SEL_BLOCK
    past = jnp.logical_and(brow < own, brow < ns)
    forced = jnp.logical_or(brow == 0, brow == own - 1)
    sc_ref[...] = jnp.where(past, jnp.where(forced, NSA_SEL_FORCE, imp_t), NEG_INF)
    chosen = jnp.logical_and(past, _rank_select(sc_ref, ns, ksel))
    sel = jnp.where(jnp.logical_or(chosen, brow == own), 1.0, 0.0)
    rows = sel_ref.shape[0]
    if rows > nsp:
        sel_ref[...] = jnp.zeros(sel_ref.shape, F32)
    sel_ref[0:nsp, :] = sel


def _cattn(q_arr, q_row_blk0, q_col_blk0, kvc, nb, tlen, tq, length, pos0, out_rows):
    nq = tlen // tq
    ncp = kvc.shape[3]
    n_c = ncp - NSA_CMP_LEN // NSA_CMP_STRIDE + 1
    ns = length // NSA_SEL_BLOCK
    ksel = min(NSA_SEL_COUNT - 1, ns)
    nsp = -(-ns // SUBLANE) * SUBLANE
    sel_rows = max(nsp, LANE)
    body = functools.partial(_cattn_body, tq=tq, ncp=ncp, n_c=n_c, ns=ns, nsp=nsp, pos0=pos0, ksel=ksel)
    gw = NSA_HPG * HEAD_DIM
    return pl.pallas_call(
        body,
        grid=(nb, NSA_KV, nq),
        in_specs=[pl.BlockSpec((tq, gw), lambda b, g, qi: (q_row_blk0 + b * nq + qi, q_col_blk0 + g)),
                  pl.BlockSpec((None, None, None, ncp, HEAD_DIM), lambda b, g, qi: (b, 0, g, 0, 0)),
                  pl.BlockSpec((None, None, None, ncp, HEAD_DIM), lambda b, g, qi: (b, 1, g, 0, 0))],
        out_specs=[pl.BlockSpec((tq, gw), lambda b, g, qi: (b * nq + qi, g)),
                   pl.BlockSpec((None, None, sel_rows, tq), lambda b, g, qi: (b, g, 0, qi))],
        out_shape=[jax.ShapeDtypeStruct((out_rows, NSA_KV * gw), F32),
                   jax.ShapeDtypeStruct((nb, NSA_KV, sel_rows, tlen), F32)],
        scratch_shapes=[pltpu.VMEM((nsp, tq), F32)],
        compiler_params=_cparams(("parallel", "parallel", "parallel")),
        name="nsa_cmp_attn",
    )(q_arr, kvc, kvc)


def _cattn_row_body(q_ref, kc_ref, vc_ref, oc_ref, sel_ref, *, tq, ncp, n_c, ns, nsl, pos0, ksel):
    scale = HEAD_DIM ** -0.5
    kc = kc_ref[...].astype(BF)
    vc = vc_ref[...].astype(BF)
    trow = pos0 + lax.broadcasted_iota(jnp.int32, (tq, 1), 0)
    ncol = lax.broadcasted_iota(jnp.int32, (1, ncp), 1)
    valid = jnp.logical_and(ncol * NSA_CMP_STRIDE + (NSA_CMP_LEN - 1) <= trow, ncol < n_c)
    psum = jnp.zeros((tq, ncp), F32)
    for z in range(NSA_HPG):
        sl = slice(z * HEAD_DIM, (z + 1) * HEAD_DIM)
        q = (q_ref[:, sl] * scale).astype(BF)
        p = _masked_softmax(lax.dot_general(q, kc, NT_DIMS, preferred_element_type=F32), valid, -1)
        oc_ref[:, sl] = jnp.dot(p.astype(BF), vc, preferred_element_type=F32)
        psum = psum + p
    r = NSA_SEL_BLOCK // NSA_CMP_STRIDE
    nrow = lax.broadcasted_iota(jnp.int32, (ncp, 1), 0)
    bcol = lax.broadcasted_iota(jnp.int32, (1, nsl), 1)
    lo = r * bcol - 1
    inside = jnp.logical_and(nrow >= lo, nrow <= lo + r)
    edge = jnp.logical_or(nrow == lo, nrow == lo + r)
    m = jnp.where(inside, jnp.where(edge, 0.5, 1.0), 0.0).astype(BF)
    hi = psum.astype(BF)
    mid = (psum - hi.astype(F32)).astype(BF)
    low = (psum - hi.astype(F32) - mid.astype(F32)).astype(BF)
    imp = (jnp.dot(hi, m, preferred_element_type=F32) + jnp.dot(mid, m, preferred_element_type=F32)
           + jnp.dot(low, m, preferred_element_type=F32))
    own = trow // NSA_SEL_BLOCK
    past = jnp.logical_and(bcol < own, bcol < ns)
    forced = jnp.logical_or(bcol == 0, bcol == own - 1)
    score = jnp.where(past, jnp.where(forced, NSA_SEL_FORCE, imp), NEG_INF)
    rank = jnp.zeros((tq, nsl), F32)
    for j in range(ns):
        cj = score[:, j:j + 1]
        beats = jnp.logical_or(cj > score, jnp.logical_and(cj == score, j < bcol))
        rank = rank + jnp.where(beats, 1.0, 0.0)
    chosen = jnp.logical_and(past, rank < ksel)
    sel_ref[...] = jnp.where(jnp.logical_or(chosen, bcol == own), 1.0, 0.0)


def _cattn_sample(q_pad, kvc, nb, length, pos0):
    tq = SUBLANE
    ncp = kvc.shape[3]
    n_c = ncp - NSA_CMP_LEN // NSA_CMP_STRIDE + 1
    ns = length // NSA_SEL_BLOCK
    ksel = min(NSA_SEL_COUNT - 1, ns)
    nsl = -(-ns // LANE) * LANE
    body = functools.partial(_cattn_row_body, tq=tq, ncp=ncp, n_c=n_c, ns=ns, nsl=nsl, pos0=pos0, ksel=ksel)
    gw = NSA_HPG * HEAD_DIM
    return pl.pallas_call(
        body,
        grid=(nb, NSA_KV),
        in_specs=[pl.BlockSpec((tq, gw), lambda b, g: (b, g)),
                  pl.BlockSpec((None, None, None, ncp, HEAD_DIM), lambda b, g: (b, 0, g, 0, 0)),
                  pl.BlockSpec((None, None, None, ncp, HEAD_DIM), lambda b, g: (b, 1, g, 0, 0))],
        out_specs=[pl.BlockSpec((tq, gw), lambda b, g: (b, g)),
                   pl.BlockSpec((None, None, tq, nsl), lambda b, g: (b, g, 0, 0))],
        out_shape=[jax.ShapeDtypeStruct((nb * tq, NSA_KV * gw), F32),
                   jax.ShapeDtypeStruct((nb, NSA_KV, tq, nsl), F32)],
        compiler_params=_cparams(("parallel", "parallel")),
        name="nsa_cmp_attn_sample",
    )(q_pad, kvc, kvc)


def _gate_body(q_ref, k_ref, sel_ref, mean_ref, sc_ref, *, tq, nb, nbp, pos0):
    qi = pl.program_id(2)

    @pl.when(qi == 0)
    def _():
        mean_ref[...] = jnp.zeros(mean_ref.shape, F32)
        for j in range(nb):
            blk = k_ref[j * MOBA_BLOCK:(j + 1) * MOBA_BLOCK, :]
            mean_ref[j:j + 1, :] = jnp.sum(blk, axis=0, keepdims=True) * (1.0 / MOBA_BLOCK)

    g_t = _dot3(mean_ref[...], q_ref[...], NT_DIMS)
    tcol = pos0 + qi * tq + lax.broadcasted_iota(jnp.int32, (1, tq), 1)
    own = tcol // MOBA_BLOCK
    brow = lax.broadcasted_iota(jnp.int32, (nbp, 1), 0)
    past = jnp.logical_and(brow < own, brow < nb)
    sc_ref[...] = jnp.where(past, g_t, NEG_INF)
    chosen = jnp.logical_and(past, _rank_select(sc_ref, nb, min(MOBA_TOPK, nb)))
    sel = jnp.where(jnp.logical_or(chosen, brow == own), 1.0, 0.0)
    if sel_ref.shape[0] > nbp:
        sel_ref[...] = jnp.zeros(sel_ref.shape, F32)
    sel_ref[0:nbp, :] = sel


def _moba_gate_prompt(y, nb, t, tq):
    nq = t // tq
    n_full = t // MOBA_BLOCK
    nbp = -(-n_full // SUBLANE) * SUBLANE
    body = functools.partial(_gate_body, tq=tq, nb=n_full, nbp=nbp, pos0=0)
    return pl.pallas_call(
        body,
        grid=(nb, MOBA_HEADS, nq),
        in_specs=[pl.BlockSpec((tq, LANE), lambda b, h, qi: (b * nq + qi, CH_QB + h)),
                  pl.BlockSpec((t, LANE), lambda b, h, qi: (b, CH_KB + h))],
        out_specs=pl.BlockSpec((None, None, LANE, tq), lambda b, h, qi: (b, h, 0, qi)),
        out_shape=jax.ShapeDtypeStruct((nb, MOBA_HEADS, LANE, t), F32),
        scratch_shapes=[pltpu.VMEM((nbp, HEAD_DIM), F32), pltpu.VMEM((nbp, tq), F32)],
        compiler_params=_cparams(("parallel", "parallel", "arbitrary")),
        name="moba_gate_prompt",
    )(y, y)


def _gate_sample_body(q_ref, m_ref, sel_ref, sc_ref, *, nb, pos0):
    tq = q_ref.shape[0]
    own = (pos0 + lax.broadcasted_iota(jnp.int32, (1, tq), 1)) // MOBA_BLOCK
    brow = lax.broadcasted_iota(jnp.int32, (nb, 1), 0)
    past = brow < own
    if sel_ref.shape[1] > nb:
        sel_ref[...] = jnp.zeros(sel_ref.shape, F32)
    for h in range(MOBA_HEADS):
        g_t = _dot3(m_ref[h], q_ref[:, h * HEAD_DIM:(h + 1) * HEAD_DIM], NT_DIMS)
        sc_ref[...] = jnp.where(past, g_t, NEG_INF)
        chosen = jnp.logical_and(past, _rank_select(sc_ref, nb, min(MOBA_TOPK, nb)))
        sel_ref[h, 0:nb, :] = jnp.where(jnp.logical_or(chosen, brow == own), 1.0, 0.0)


def _moba_gate_sample(q_pad, means, pos0):
    nb, _, n_full, _ = means.shape
    assert n_full % SUBLANE == 0
    tq = SUBLANE
    rows = max(n_full, LANE)
    body = functools.partial(_gate_sample_body, nb=n_full, pos0=pos0)
    return pl.pallas_call(
        body,
        grid=(nb,),
        in_specs=[pl.BlockSpec((tq, MOBA_HEADS * HEAD_DIM), lambda b: (b, 0)),
                  pl.BlockSpec((None, MOBA_HEADS, n_full, HEAD_DIM), lambda b: (b, 0, 0, 0))],
        out_specs=pl.BlockSpec((None, MOBA_HEADS, rows, tq), lambda b: (b, 0, 0, 0)),
        out_shape=jax.ShapeDtypeStruct((nb, MOBA_HEADS, rows, tq), F32),
        scratch_shapes=[pltpu.VMEM((n_full, tq), F32)],
        compiler_params=_cparams(("parallel",)),
        name="moba_gate_sample",
    )(q_pad, means)


MEANS_BLOCKS = 4


def _means_body(pt_ref, *refs, ppb, bps):
    c_refs = refs[:bps * ppb]
    o_ref = refs[bps * ppb]
    j = pl.program_id(1)
    for q in range(bps):
        acc = jnp.sum(c_refs[q * ppb][...], axis=0)
        for i in range(1, ppb):
            acc = acc + jnp.sum(c_refs[q * ppb + i][...], axis=0)
        m = acc * (1.0 / MOBA_BLOCK)
        for h in range(MOBA_HEADS):
            o_ref[h, pl.ds(j * bps + q, 1), :] = m[h:h + 1, :]


def _moba_means_sample(cache, li, page_table):
    nb, n_pages = page_table.shape
    page = cache.shape[2]
    ppb = MOBA_BLOCK // page
    n_full = n_pages // ppb
    bps = math.gcd(MEANS_BLOCKS, n_full)
    body = functools.partial(_means_body, ppb=ppb, bps=bps)
    in_specs = [
        pl.BlockSpec((None, None, page, None, MOBA_HEADS, HEAD_DIM),
                     lambda b, j, pt, i=i: (li, pt[b, j * bps * ppb + i], 0, 0, 0, 0))
        for i in range(bps * ppb)
    ]
    return pl.pallas_call(
        body,
        grid_spec=pltpu.PrefetchScalarGridSpec(
            num_scalar_prefetch=1,
            grid=(nb, n_full // bps),
            in_specs=in_specs,
            out_specs=pl.BlockSpec((None, MOBA_HEADS, n_full, HEAD_DIM), lambda b, j, pt: (b, 0, 0, 0)),
        ),
        out_shape=jax.ShapeDtypeStruct((nb, MOBA_HEADS, n_full, HEAD_DIM), F32),
        compiler_params=_cparams(("parallel", "arbitrary")),
        name="moba_means_sample",
    )(page_table, *([cache] * (bps * ppb)))


FLASH_UNIT_ROWS = 256
LOG2_E = 1.4426950408889634
MASK_BIG = 2.0 ** 60


def _flash_body(*refs, tq, tk, nh, blk, window, masked, ur):
    q_ref, k_ref, v_ref = refs[:3]
    pos = 3
    if masked:
        sel_ref = refs[pos]
        pos += 1
    o_ref, qs_ref, kt_ref = refs[pos:pos + 3]
    pos += 1
    units = [(z, r0) for z in range(nh) for r0 in range(0, tq, ur)]
    stat = refs[pos + 2:pos + 2 + 2 * len(units)]
    m_refs, acc_refs = stat[0::2], stat[1::2]
    qi = pl.program_id(2)
    t0 = qi * tq
    scale = HEAD_DIM ** -0.5 * LOG2_E
    if masked:
        bias = ((jnp.transpose(sel_ref[...]) - 1.0) * MASK_BIG).astype(BF)
    for z in range(nh):
        qz = (q_ref[:, z * HEAD_DIM:(z + 1) * HEAD_DIM] * scale).astype(BF)
        qs_ref[z] = jnp.concatenate([qz, bias], axis=1) if masked else qz
    for u in range(len(units)):
        m_refs[u][...] = jnp.full(m_refs[u].shape, NEG_INF, F32)
        acc_refs[u][...] = jnp.zeros(acc_refs[u].shape, F32)
    trow = t0 + lax.broadcasted_iota(jnp.int32, (tq, 1), 0)

    @pl.when(qi == 0)
    def _():
        for c in range(kt_ref.shape[0]):
            kc = k_ref[c * tk:(c + 1) * tk, :]
            if masked:
                prow = c * tk + lax.broadcasted_iota(jnp.int32, (tk, 1), 0)
                jcol = lax.broadcasted_iota(jnp.int32, (1, LANE), 1)
                kc = jnp.concatenate([kc, jnp.where(prow // blk == jcol, 1.0, 0.0).astype(BF)], axis=1)
            kt_ref[c] = kc.T

    def process(c, edge):
        ks = pl.multiple_of(c * tk, tk)
        kt = kt_ref[c]
        vc = jnp.concatenate([v_ref[pl.ds(ks, tk), :], jnp.ones((tk, LANE), BF)], axis=1)
        if edge:
            pcol = ks + lax.broadcasted_iota(jnp.int32, (1, tk), 1)
            valid = pcol <= trow
            if window:
                valid = jnp.logical_and(valid, pcol > trow - window)
        for u, (z, r0) in enumerate(units):
            s = jnp.dot(qs_ref[z, r0:r0 + ur, :], kt, preferred_element_type=F32)
            m_old = m_refs[u][...]
            if edge:
                s = jnp.where(valid[r0:r0 + ur], s, NEG_INF)
                m_new = jnp.maximum(m_old, jnp.max(s, axis=-1, keepdims=True))
                m_use = jnp.where(m_new == NEG_INF, 0.0, m_new)
            else:
                m_new = jnp.maximum(m_old, jnp.max(s, axis=-1, keepdims=True))
                m_use = m_new
            alpha = jnp.exp2(m_old - m_use)
            p = jnp.exp2(s - m_use)
            acc_refs[u][...] = alpha * acc_refs[u][...] + jnp.dot(p.astype(BF), vc, preferred_element_type=F32)
            m_refs[u][...] = m_new

    if masked:
        def pair(c2, carry):
            process(2 * c2, False)
            process(2 * c2 + 1, False)
            return carry

        lax.fori_loop(0, qi // 2, pair, 0)

        @pl.when(qi % 2 == 1)
        def _():
            process(qi - 1, False)

        process(qi, True)
    else:
        c_hi = (t0 + tq + tk - 1) // tk
        c_lo = jnp.maximum(t0 - window + 1, 0) // tk
        lax.fori_loop(c_lo, c_hi, lambda c, carry: (process(c, True), carry)[1], 0)
    for u, (z, r0) in enumerate(units):
        l = acc_refs[u][:, HEAD_DIM:HEAD_DIM + 1]
        o_ref[r0:r0 + ur, z * HEAD_DIM:(z + 1) * HEAD_DIM] = acc_refs[u][:, :HEAD_DIM] / jnp.where(l > 0, l, 1.0)


def _flash(y, y16, sel, nb, t, *, q_ch, k_ch, v_ch, kvh, nh, blk, window, out_rows, tq=512, tk=512):
    tq = min(tq, t)
    tk = min(tk, t)
    nq = t // tq
    masked = sel is not None
    gw = nh * HEAD_DIM
    assert q_ch % nh == 0 and (tq == tk or not masked)
    in_specs = [pl.BlockSpec((tq, gw), lambda b, h, qi: (b * nq + qi, q_ch // nh + h)),
                pl.BlockSpec((t, LANE), lambda b, h, qi: (b, k_ch + h)),
                pl.BlockSpec((t, LANE), lambda b, h, qi: (b, v_ch + h))]
    args = [y, y16, y16]
    if masked:
        in_specs.append(pl.BlockSpec((None, None, LANE, tq), lambda b, h, qi: (b, h, 0, qi)))
        args.append(sel)
    ur = min(FLASH_UNIT_ROWS, tq)
    n_units = nh * (tq // ur)
    kd = (2 if masked else 1) * HEAD_DIM
    body = functools.partial(_flash_body, tq=tq, tk=tk, nh=nh, blk=blk, window=window, masked=masked, ur=ur)
    unit_scratch = [pltpu.VMEM((ur, 1), F32), pltpu.VMEM((ur, 2 * HEAD_DIM), F32)]
    return pl.pallas_call(
        body,
        grid=(nb, kvh, nq),
        in_specs=in_specs,
        out_specs=pl.BlockSpec((tq, gw), lambda b, h, qi: (b * nq + qi, h)),
        out_shape=jax.ShapeDtypeStruct((out_rows, kvh * gw), F32),
        scratch_shapes=[pltpu.VMEM((nh, tq, kd), BF), pltpu.VMEM((t // tk, kd, tk), BF)] + unit_scratch * n_units,
        compiler_params=_cparams(("parallel", "parallel", "arbitrary")),
        name="flash_" + ("win" if window else "blk%d" % blk),
    )(*args)


def _dec_body(tbl_ref, vld_ref, q_ref, *refs, nh, rows, kvh, nblk, r_min):
    k_refs = refs[:nblk]
    v_refs = refs[nblk:2 * nblk]
    kn_ref, vn_ref, o_ref = refs[2 * nblk:2 * nblk + 3]
    b = pl.program_id(0)
    h = pl.program_id(1)
    scale = HEAD_DIM ** -0.5
    qrow = q_ref[pl.ds(b, 1), :] * scale
    zrow = lax.broadcasted_iota(jnp.int32, (SUBLANE, 1), 0)
    qm = jnp.zeros((SUBLANE, HEAD_DIM), F32)
    for z in range(nh):
        qm = jnp.where(zrow == z, qrow[:, z * HEAD_DIM:(z + 1) * HEAD_DIM], qm)
    qb = qm.astype(BF)
    col = lax.broadcasted_iota(jnp.int32, (1, rows * kvh), 1)
    valid = jnp.logical_and(col % kvh == h, col // kvh >= r_min)
    kn = kn_ref[pl.ds(b, 1), :]
    vn = vn_ref[pl.ds(b, 1), :]
    s_new = jnp.sum(qm * kn, axis=-1, keepdims=True)
    scores = []
    m = s_new
    for i in range(nblk):
        kk = k_refs[i][...].reshape(rows * kvh, HEAD_DIM).astype(BF)
        s = lax.dot_general(qb, kk, NT_DIMS, preferred_element_type=F32)
        s = jnp.where(jnp.logical_and(valid, vld_ref[b, h, i] > 0), s, NEG_INF)
        scores.append(s)
        m = jnp.maximum(m, jnp.max(s, axis=-1, keepdims=True))
    p_new = jnp.exp(s_new - m)
    l = p_new
    acc = p_new * vn
    for i in range(nblk):
        p = jnp.exp(scores[i] - m)
        l = l + jnp.sum(p, axis=-1, keepdims=True)
        vv = v_refs[i][...].reshape(rows * kvh, HEAD_DIM).astype(BF)
        acc = acc + jnp.dot(p.astype(BF), vv, preferred_element_type=F32)
    o_ref[...] = acc / l


def _decode_attn(ys, cache, li_fixed, tbl, vld, *, q_ch, kn_ch, vn_ch, nh, rows, kvh, r_min, name):
    nb, _, nblk = tbl.shape
    page = cache.shape[2]
    bpp = page // rows
    gw = nh * HEAD_DIM
    body = functools.partial(_dec_body, nh=nh, rows=rows, kvh=kvh, nblk=nblk, r_min=r_min)

    def kv_spec(kv, i):
        def imap(b, h, tb, vl):
            t = tb[b, h, i]
            return (li_fixed, t // bpp, t % bpp, kv, 0, 0)

        return pl.BlockSpec((None, None, rows, None, kvh, HEAD_DIM), imap)

    nrow = ys.shape[0]
    return pl.pallas_call(
        body,
        grid_spec=pltpu.PrefetchScalarGridSpec(
            num_scalar_prefetch=2,
            grid=(nb, kvh),
            in_specs=[pl.BlockSpec((nrow, gw), lambda b, h, tb, vl: (0, q_ch // nh + h))]
            + [kv_spec(0, i) for i in range(nblk)] + [kv_spec(1, i) for i in range(nblk)]
            + [pl.BlockSpec((nrow, LANE), lambda b, h, tb, vl: (0, kn_ch + h)),
               pl.BlockSpec((nrow, LANE), lambda b, h, tb, vl: (0, vn_ch + h))],
            out_specs=pl.BlockSpec((None, None, SUBLANE, HEAD_DIM), lambda b, h, tb, vl: (b, h, 0, 0)),
        ),
        out_shape=jax.ShapeDtypeStruct((nb, kvh, SUBLANE, HEAD_DIM), F32),
        compiler_params=_cparams(("parallel", "parallel")),
        name=name,
    )(tbl, vld, ys, *([cache] * (2 * nblk)), ys, ys)


def _mix_body(oc_ref, os_ref, ow_ref, ob_ref, soc_ref, sos_ref, sow_ref, sob_ref, gate_ref, o_ref, *, n_p_tiles, n_s):
    i = pl.program_id(0)
    w = NSA_HEADS * HEAD_DIM

    def emit(oc, os_, ow, ob, rows):
        g = jax.nn.sigmoid(gate_ref[0:rows, :])
        for h in range(NSA_HEADS):
            sl = slice(h * HEAD_DIM, (h + 1) * HEAD_DIM)
            o = (g[:, 3 * h:3 * h + 1] * oc[:, sl] + g[:, 3 * h + 1:3 * h + 2] * os_[:, sl]
                 + g[:, 3 * h + 2:3 * h + 3] * ow[:, sl])
            o_ref[0:rows, sl] = o.astype(o_ref.dtype)
        o_ref[0:rows, w:] = ob[...].astype(o_ref.dtype)

    @pl.when(i < n_p_tiles)
    def _():
        emit(oc_ref, os_ref, ow_ref, ob_ref, o_ref.shape[0])

    @pl.when(i >= n_p_tiles)
    def _():
        emit(soc_ref, sos_ref, sow_ref, sob_ref, n_s)


def _mix(prompt_parts, sample_parts, y, tm=256):
    n_prompt = prompt_parts[0].shape[0]
    n_s = sample_parts[0].shape[0]
    m = y.shape[0]
    tm = min(tm, n_prompt)
    assert n_prompt % tm == 0 and m == n_prompt + n_s and n_s <= tm
    n_p_tiles = n_prompt // tm
    wa = NSA_HEADS * HEAD_DIM
    wb = MOBA_HEADS * HEAD_DIM
    pmap = lambda i: (jnp.minimum(i, n_p_tiles - 1), 0)
    body = functools.partial(_mix_body, n_p_tiles=n_p_tiles, n_s=n_s)
    return pl.pallas_call(
        body,
        grid=(n_p_tiles + 1,),
        in_specs=[pl.BlockSpec((tm, wa), pmap)] * 3 + [pl.BlockSpec((tm, wb), pmap)]
        + [pl.BlockSpec((n_s, wa), lambda i: (0, 0))] * 3 + [pl.BlockSpec((n_s, wb), lambda i: (0, 0))]
        + [pl.BlockSpec((tm, LANE), lambda i: (i, CH_GATE))],
        out_specs=pl.BlockSpec((tm, wa + wb), lambda i: (i, 0)),
        out_shape=jax.ShapeDtypeStruct((m, wa + wb), BF),
        compiler_params=_cparams(("parallel",)),
        name="attn_mix",
    )(*prompt_parts, *sample_parts, y)


def _convmix_body(up_ref, hp_ref, uc_ref, hc_ref, pw_ref, ps_ref, dw_ref, dwb_ref, lng_ref, lnb_ref, pww_ref,
                  o_ref, zp_ref, zs_ref, cb_ref, *, ts, nt, zero_first, avail0, rc):
    s = pl.program_id(0)
    hp_rows = hp_ref.shape[0]
    hc_rows = hc_ref.shape[0]
    width = up_ref.shape[1]
    if zero_first:
        keep = jnp.where(s % nt == 0, 0.0, 1.0)
        zp_ref[0:hp_rows, :] = hp_ref[...] * keep
        zs_ref[0, 0:hc_rows, :] = hc_ref[...] * keep
        avail = (s % nt) * ts + avail0
    else:
        zp_ref[0:hp_rows, :] = hp_ref[...]
        zs_ref[0, 0:hc_rows, :] = hc_ref[...]
        avail = avail0
    zp_ref[hp_rows:, :] = up_ref[...]
    zs_ref[0, hc_rows:, :] = uc_ref[...]
    span = hc_rows + ts - SUBLANE
    for k in range(1, SUBLANE):
        zs_ref[k, 0:span, :] = zs_ref[0, k:k + span, :]
    gwidth = width // len(POOL_WINDOWS)
    t_idx = lax.broadcasted_iota(jnp.int32, (ts, 1), 0) + avail + 1
    for gi, w in enumerate(POOL_WINDOWS):
        sl = slice(gi * gwidth, (gi + 1) * gwidth)
        cur = zp_ref[hp_rows:hp_rows + ts, sl]
        acc = cur
        for jj in range(1, w):
            acc = acc + zp_ref[hp_rows - jj:hp_rows - jj + ts, sl]
        cnt = jnp.minimum(t_idx, w).astype(F32)
        yg = acc / cnt - cur
        og = jnp.dot(yg.astype(BF), pw_ref[gi], preferred_element_type=F32) * ps_ref[:, sl]
        o_ref[:, sl] = og.astype(o_ref.dtype)
    base = hc_rows - (CONV_WIDTH - 1)
    for r in range(ts // rc):
        for c in range(width // LANE):
            cs = slice(c * LANE, (c + 1) * LANE)
            acc = jnp.zeros((rc, LANE), F32)
            for jj in range(CONV_WIDTH):
                k = (base + jj) % SUBLANE
                lo = r * rc + base + jj - k
                acc = acc + zs_ref[k, lo:lo + rc, cs] * dw_ref[jj:jj + 1, cs]
            cb_ref[r * rc:(r + 1) * rc, cs] = acc + dwb_ref[:, cs]
    cv = cb_ref[...]
    mu = jnp.mean(cv, axis=-1, keepdims=True)
    xc = cv - mu
    yn = xc * lax.rsqrt(jnp.mean(xc * xc, axis=-1, keepdims=True) + NORM_EPS) * lng_ref[...] + lnb_ref[...]
    act = yn * jax.nn.sigmoid(yn)
    o_ref[:, width:] = jnp.dot(act.astype(BF), pww_ref[...], preferred_element_type=F32).astype(o_ref.dtype)


def _convmix(u_pool, halo_pool, u_conv, halo_conv, prm, *, ts, nt, n_tiles, zero_first, avail0, hp_rows, hc_rows,
             halo_blk):
    pw, ps, dw, dwb, lng, lnb, pww = prm
    width = u_pool.shape[1]
    rc = min(64, ts)
    body = functools.partial(_convmix_body, ts=ts, nt=nt, zero_first=zero_first, avail0=avail0, rc=rc)
    full = lambda a: pl.BlockSpec(a.shape, lambda s: (0,) * a.ndim)
    return pl.pallas_call(
        body,
        grid=(n_tiles,),
        in_specs=[pl.BlockSpec((ts, width), lambda s: (s, 0)),
                  pl.BlockSpec((hp_rows, width), lambda s: (halo_blk(s, hp_rows), 0)),
                  pl.BlockSpec((ts, width), lambda s: (s, 0)),
                  pl.BlockSpec((hc_rows, width), lambda s: (halo_blk(s, hc_rows), 0)),
                  full(pw), full(ps), full(dw), full(dwb), full(lng), full(lnb), full(pww)],
        out_specs=pl.BlockSpec((ts, 2 * width), lambda s: (s, 0)),
        out_shape=jax.ShapeDtypeStruct((n_tiles * ts, 2 * width), BF),
        scratch_shapes=[pltpu.VMEM((hp_rows + ts, width), F32), pltpu.VMEM((SUBLANE, hc_rows + ts, width), F32),
                        pltpu.VMEM((ts, width), F32)],
        compiler_params=_cparams(("parallel",)),
        name="conv_mix",
    )(u_pool, halo_pool, u_conv, halo_conv, pw, ps, dw, dwb, lng, lnb, pww)


def _router_body(x_ref, g_ref, rw_ref, r_ref, xn_ref):
    x = x_ref[...]
    xn = x * lax.rsqrt(jnp.mean(x * x, axis=-1, keepdims=True) + NORM_EPS) * g_ref[...]
    xn_ref[...] = xn.astype(xn_ref.dtype)
    logits = _dot3(xn, rw_ref[...])
    lane = lax.broadcasted_iota(jnp.int32, logits.shape, 1)
    l1 = jnp.where(lane < N_EXPERTS, logits, NEG_INF)
    m1 = jnp.max(l1, axis=-1, keepdims=True)
    i1 = jnp.min(jnp.where(l1 == m1, lane, LANE), axis=-1, keepdims=True)
    l2 = jnp.where(lane == i1, NEG_INF, l1)
    m2 = jnp.max(l2, axis=-1, keepdims=True)
    i2 = jnp.min(jnp.where(l2 == m2, lane, LANE), axis=-1, keepdims=True)
    e = jnp.exp(m2 - m1)
    w1 = 1.0 / (1.0 + e)
    w2 = e / (1.0 + e)
    r_ref[...] = jnp.where(lane == 0, i1.astype(F32),
                           jnp.where(lane == 1, i2.astype(F32), jnp.where(lane == 2, w1, jnp.where(lane == 3, w2, 0.0))))


def _router(x, g, rw, tm=256):
    m, d = x.shape
    rw_pad = jnp.pad(rw, ((0, 0), (0, LANE - rw.shape[1])))
    return pl.pallas_call(
        _router_body,
        grid=(pl.cdiv(m, tm),),
        in_specs=[pl.BlockSpec((tm, d), lambda i: (i, 0)), pl.BlockSpec((1, d), lambda i: (0, 0)),
                  pl.BlockSpec((d, LANE), lambda i: (0, 0))],
        out_specs=[pl.BlockSpec((tm, LANE), lambda i: (i, 0)), pl.BlockSpec((tm, d), lambda i: (i, 0))],
        out_shape=[jax.ShapeDtypeStruct((m, LANE), F32), jax.ShapeDtypeStruct((m, d), BF)],
        compiler_params=_cparams(("parallel",)),
        name="moe_router",
    )(x, g.reshape(1, d), rw_pad)


def _row_copy(src_hbm, row, dst, drow, sem):
    return pltpu.make_async_copy(src_hbm.at[pl.ds(row, 1), :], dst.at[pl.ds(drow, 1), :], sem)


def _gather_rows_body(c0_ref, nc_ref, src_ref, xn_hbm, o_ref, buf_ref, acc_ref, first_ref, sem, *, tc):
    i = pl.program_id(0)
    c0 = c0_ref[i]
    n = nc_ref[i]
    i_next = jnp.minimum(i + 1, pl.num_programs(0) - 1)
    n_next = jnp.where(i + 1 < pl.num_programs(0), nc_ref[i_next], 0)

    def chunk_copy(c, slot):
        return pltpu.make_async_copy(xn_hbm.at[pl.ds(c * tc, tc), :], buf_ref.at[slot], sem.at[slot])

    acc_ref[...] = jnp.zeros(acc_ref.shape, F32)

    @pl.when(i == 0)
    def _():
        first_ref[0] = 0

        @pl.when(n > 0)
        def _():
            chunk_copy(c0, 0).start()

    first = first_ref[0]
    src = src_ref[...]

    def body(k, carry):
        slot = (first + k) % 2
        chunk_copy(c0 + k, slot).wait()

        @pl.when(k + 1 < n)
        def _():
            chunk_copy(c0 + k + 1, 1 - slot).start()

        @pl.when(jnp.logical_and(k + 1 == n, n_next > 0))
        def _():
            chunk_copy(c0_ref[i_next], 1 - slot).start()
            first_ref[0] = 1 - slot

        tok = (c0 + k) * tc + lax.broadcasted_iota(jnp.int32, (1, tc), 1)
        onehot = jnp.where(src == tok, 1.0, 0.0).astype(buf_ref.dtype)
        acc_ref[...] += jnp.dot(onehot, buf_ref[slot], preferred_element_type=F32)
        return carry

    lax.fori_loop(0, n, body, 0)
    o_ref[...] = acc_ref[...].astype(o_ref.dtype)


def _gather_rows(xn, src, c0, nc, tm, tc):
    n_tok, d = xn.shape
    r_tot = src.shape[0]
    body = functools.partial(_gather_rows_body, tc=tc)
    return pl.pallas_call(
        body,
        grid_spec=pltpu.PrefetchScalarGridSpec(
            num_scalar_prefetch=2,
            grid=(r_tot // tm,),
            in_specs=[pl.BlockSpec((tm, 1), lambda i, a, b: (i, 0)), pl.BlockSpec(memory_space=pl.ANY)],
            out_specs=pl.BlockSpec((tm, d), lambda i, a, b: (i, 0)),
            scratch_shapes=[pltpu.VMEM((2, tc, d), xn.dtype), pltpu.VMEM((tm, d), F32), pltpu.SMEM((1,), jnp.int32),
                            pltpu.SemaphoreType.DMA((2,))],
        ),
        out_shape=jax.ShapeDtypeStruct((r_tot, d), xn.dtype),
        compiler_params=_cparams(("arbitrary",)),
        name="moe_gather",
    )(c0, nc, src.reshape(r_tot, 1), xn)


def _combine_body(p1_ref, p2_ref, eo_hbm, x_ref, r_ref, o_ref, ot_ref, b1_ref, b2_ref, sem, *, tm, n_main, n_tail):
    i = pl.program_id(0)
    base = i * tm
    n = jnp.where(i < n_main // tm, tm, n_tail)

    def issue(r, c):
        _row_copy(eo_hbm, p1_ref[base + r], b1_ref, r, sem).start()
        _row_copy(eo_hbm, p2_ref[base + r], b2_ref, r, sem).start()
        return c

    lax.fori_loop(0, n, issue, 0)

    def wait(r, c):
        _row_copy(eo_hbm, 0, b1_ref, r, sem).wait()
        _row_copy(eo_hbm, 0, b2_ref, r, sem).wait()
        return c

    lax.fori_loop(0, n, wait, 0)

    def rows(r):
        rr = r_ref[0:r, :]
        return x_ref[0:r, :] + rr[:, 2:3] * b1_ref[0:r, :] + rr[:, 3:4] * b2_ref[0:r, :]

    @pl.when(i < n_main // tm)
    def _():
        o_ref[...] = rows(tm)

    @pl.when(i == n_main // tm)
    def _():
        ot_ref[...] = rows(n_tail)


def _combine(x, eo, routing, p1, p2, n_main, tm=256):
    m, d = x.shape
    n_tail = m - n_main
    assert n_main % tm == 0 and 0 < n_tail <= tm and n_tail % SUBLANE == 0
    n_full = n_main // tm
    body = functools.partial(_combine_body, tm=tm, n_main=n_main, n_tail=n_tail)
    return pl.pallas_call(
        body,
        grid_spec=pltpu.PrefetchScalarGridSpec(
            num_scalar_prefetch=2,
            grid=(n_full + 1,),
            in_specs=[pl.BlockSpec(memory_space=pl.ANY), pl.BlockSpec((tm, d), lambda i, a, b: (i, 0)),
                      pl.BlockSpec((tm, LANE), lambda i, a, b: (i, 0))],
            out_specs=[pl.BlockSpec((tm, d), lambda i, a, b: (jnp.minimum(i, n_full - 1), 0)),
                       pl.BlockSpec((n_tail, d), lambda i, a, b: (0, 0))],
            scratch_shapes=[pltpu.VMEM((tm, d), F32), pltpu.VMEM((tm, d), F32), pltpu.SemaphoreType.DMA(())],
        ),
        out_shape=[jax.ShapeDtypeStruct((n_main, d), F32), jax.ShapeDtypeStruct((n_tail, d), F32)],
        compiler_params=_cparams(("arbitrary",)),
        name="moe_combine",
    )(p1, p2, eo, x, routing)


def _moe_plan(e_idx, tm, tc):
    n = e_idx.shape[0]
    n_asg = n * TOP_K
    flat_e = e_idx.reshape(-1)
    onehot = (flat_e[:, None] == jnp.arange(N_EXPERTS, dtype=jnp.int32)[None, :]).astype(jnp.int32)
    cnt = jnp.sum(onehot, axis=0)
    rank = jnp.take_along_axis(jnp.cumsum(onehot, axis=0) - onehot, flat_e[:, None], axis=1)[:, 0]
    cnt_p = ((cnt + tm - 1) // tm) * tm
    ends = jnp.cumsum(cnt_p)
    off = ends - cnt_p
    pos = off[flat_e] + rank
    n_tiles = (n_asg + N_EXPERTS * (tm - 1) + tm - 1) // tm
    r_tot = n_tiles * tm
    src = jnp.full((r_tot,), -1, jnp.int32).at[pos].set(jnp.arange(n_asg, dtype=jnp.int32) // TOP_K)
    src_t = src.reshape(n_tiles, tm)
    lo = jnp.min(jnp.where(src_t >= 0, src_t, n), axis=1)
    hi = jnp.max(src_t, axis=1)
    c0 = jnp.where(hi >= 0, lo // tc, 0).astype(jnp.int32)
    nc = jnp.where(hi >= 0, hi // tc + 1 - lo // tc, 0).astype(jnp.int32)
    tile_start = jnp.arange(n_tiles, dtype=jnp.int32) * tm
    tval = (tile_start < ends[-1]).astype(jnp.int32)
    texp = jnp.minimum(jnp.sum((tile_start[:, None] >= ends[None, :]).astype(jnp.int32), axis=1), N_EXPERTS - 1)
    last = jnp.max(jnp.where(tval > 0, texp, 0))
    texp = jnp.where(tval > 0, texp, last)
    pos2 = pos.reshape(n, TOP_K)
    return src, c0, nc, pos2[:, 0], pos2[:, 1], texp, tval


def _moe(x, g, rw, wg, wu, wd, n_main, tm=256):
    n = x.shape[0]
    tc = max(t for t in range(2 * SUBLANE, 513, 2 * SUBLANE) if n % t == 0)
    routing, xn = _router(x, g, rw)
    e_idx = routing[:, :TOP_K].astype(jnp.int32)
    src, c0, nc, p1, p2, texp, tval = _moe_plan(e_idx, tm, tc)
    xs = _gather_rows(xn, src, c0, nc, tm, tc)
    hid = _gmm(xs, [(wg, 0), (wu, 0)], wg.shape[2], epi="swiglu", texp=texp, tval=tval, out_dtype=BF, tm=tm, tn=1024,
               name="moe_up")
    eo = _gmm(hid, [(wd, 0)], wd.shape[2], texp=texp, tval=tval, tm=tm, name="moe_down")
    return _combine(x, eo, routing, p1, p2, n_main)


def _seq_tails(a, nb, t, k):
    return jnp.stack([a[(b + 1) * t - k:(b + 1) * t] for b in range(nb)])


def _attn_layer(x, n_prompt, nb_p, t_p, nb_s, li, prm, caches, page_table):
    (attn_norm, w_in, qk_g, cmp_pe, cmp_w1, cmp_w2, moba_g, w_out, ffn_norm, w_gate, w_up, w_down) = prm
    cache_cmp, cache_sel, state_win, cache_moba = caches
    d = x[0].shape[1]
    past_len = page_table.shape[1] * cache_moba.shape[2]

    g_lo = NSA_CHUNKS * LANE
    g_hi = g_lo + 3 * NSA_HEADS
    assert w_in.shape[1] == g_hi + MOBA_CHUNKS * LANE
    w_t = w_in.T
    w_moba = w_t[g_hi:][None]
    w_gate_cols = jnp.pad(w_t[g_lo:g_hi], ((0, LANE - 3 * NSA_HEADS), (0, 0)))[None]
    ones = jnp.ones((HEAD_DIM,), F32)
    zeros = jnp.zeros((HEAD_DIM,), F32)
    gain_a = jnp.concatenate([qk_g[0]] * 8 + [ones] * 4 + [qk_g[2]] * 2 + [ones] * 2 + [qk_g[3]] * 2 + [ones] * 2)[None]
    flag_a = jnp.concatenate([ones] * 8 + [zeros] * 4 + [ones] * 2 + [zeros] * 2 + [ones] * 2 + [zeros] * 2)[None]
    gain_b = jnp.concatenate([moba_g[0]] * 8 + [moba_g[1]] * 8 + [ones] * 8)[None]
    flag_b = jnp.concatenate([ones] * 16 + [zeros] * 8)[None]
    xn = _rmsnorm(x, attn_norm, tm=STREAM_TM)
    ya, ya16 = _gmm(xn, [(w_t[None], 0)], NSA_CHUNKS * LANE, epi="headnorm", gain=gain_a, flag=flag_a,
                    out_dtype=(F32, BF), tm=STREAM_TM, tn=10 * LANE, w_t=True, name="attn_in_nsa")
    yb, yb16 = _gmm(xn, [(w_moba, 0)], MOBA_CHUNKS * LANE, epi="headnorm", gain=gain_b, flag=flag_b,
                    out_dtype=(F32, BF), tm=STREAM_TM, tn=8 * LANE, w_t=True, name="attn_in_moba")
    yg = _gmm(xn, [(w_gate_cols, 0)], LANE, tm=STREAM_TM, tn=LANE, w_t=True, name="attn_in_gate")

    r = NSA_CMP_LEN // NSA_CMP_STRIDE
    wcat = cmp_w1.reshape(2, r, NSA_CMP_STRIDE, HEAD_DIM, NSA_CMP_HIDDEN).transpose(0, 2, 3, 1, 4)
    wcat = wcat.reshape(2, NSA_CMP_STRIDE, HEAD_DIM, r * NSA_CMP_HIDDEN).astype(BF)
    pe_rows = cmp_pe.reshape(2, r, NSA_CMP_STRIDE, HEAD_DIM).transpose(0, 2, 1, 3)
    pe_rows = jnp.pad(pe_rows, ((0, 0), (0, 0), (0, SUBLANE - r), (0, 0))).astype(BF)
    w2 = cmp_w2.astype(BF)
    gk = qk_g[1][None]

    kvc_p = _compress_c2(_compress_c1_prompt(ya, wcat, nb_p, t_p), pe_rows, wcat, w2, gk)
    rows = n_prompt
    oc, sel_a = _cattn(ya, 0, 0, kvc_p, nb_p, t_p, min(512, t_p), t_p, 0, rows)
    sel_b = _moba_gate_prompt(yb, nb_p, t_p, min(1024, t_p))
    os_ = _flash(ya, ya16, sel_a, nb_p, t_p, q_ch=CH_QA, k_ch=CH_KS, v_ch=CH_VS, kvh=NSA_KV, nh=NSA_HPG,
                 blk=NSA_SEL_BLOCK, window=0, out_rows=rows)
    ow = _flash(ya, ya16, None, nb_p, t_p, q_ch=CH_QA, k_ch=CH_KW, v_ch=CH_VW, kvh=NSA_KV, nh=NSA_HPG, blk=0,
                window=NSA_WINDOW, out_rows=rows)
    ob = _flash(yb, yb16, sel_b, nb_p, t_p, q_ch=CH_QB, k_ch=CH_KB, v_ch=CH_VB, kvh=MOBA_HEADS, nh=1,
                blk=MOBA_BLOCK, window=0, out_rows=rows)

    n_pad = x[2] - n_prompt
    ys = ya[n_prompt:]
    ysb = yb[n_prompt:]
    kvc_s = _compress_c2(_compress_c1_sample(cache_cmp, li, page_table, wcat), pe_rows, wcat, w2, gk)
    q_pad = jnp.zeros((nb_s, SUBLANE, NSA_HEADS * HEAD_DIM), F32).at[:, 0].set(ys[:nb_s, :NSA_HEADS * HEAD_DIM])
    oc_s, sel_s = _cattn_sample(q_pad.reshape(nb_s * SUBLANE, -1), kvc_s, nb_s, past_len + 1, past_len)
    oc_s = oc_s.reshape(nb_s, SUBLANE, -1)[:, 0]
    n_sel = (past_len + 1) // NSA_SEL_BLOCK
    k_sel = min(NSA_SEL_COUNT - 1, n_sel)
    mask_a = sel_s[:, :, 0, :n_sel]
    idx_a = jnp.argsort(-mask_a, axis=-1, stable=True)[..., :k_sel].astype(jnp.int32)
    vld_a = (jnp.take_along_axis(mask_a, idx_a, axis=-1) > 0).astype(jnp.int32)
    page = cache_sel.shape[2]
    bpp = page // NSA_SEL_BLOCK
    pt_b = page_table[:, None, :]
    tbl_a = jnp.take_along_axis(jnp.broadcast_to(pt_b, (nb_s, NSA_KV, pt_b.shape[-1])), idx_a // bpp, axis=-1) * bpp + idx_a % bpp
    os_s = _decode_attn(ys, cache_sel, li, tbl_a.astype(jnp.int32), vld_a, q_ch=CH_QA, kn_ch=CH_KS, vn_ch=CH_VS,
                        nh=NSA_HPG, rows=NSA_SEL_BLOCK, kvh=NSA_KV, r_min=0,
                        name="dec_sel")
    win_buf = state_win.shape[2]
    tbl_w = jnp.broadcast_to(jnp.arange(nb_s, dtype=jnp.int32)[:, None, None], (nb_s, NSA_KV, 1))
    ow_s = _decode_attn(ys, state_win, li, tbl_w, jnp.ones_like(tbl_w), q_ch=CH_QA, kn_ch=CH_KW, vn_ch=CH_VW,
                        nh=NSA_HPG, rows=win_buf, kvh=NSA_KV, r_min=max(0, win_buf - (NSA_WINDOW - 1)),
                        name="dec_win")
    means = _moba_means_sample(cache_moba, li, page_table)
    qb_pad = jnp.zeros((nb_s, SUBLANE, MOBA_HEADS * HEAD_DIM), F32).at[:, 0].set(
        ysb[:nb_s, CH_QB * LANE:(CH_QB + MOBA_HEADS) * LANE])
    sel_m = _moba_gate_sample(qb_pad.reshape(nb_s * SUBLANE, -1), means, past_len)
    n_full = means.shape[2]
    k_top = min(MOBA_TOPK, n_full)
    mask_b = sel_m[:, :, :n_full, 0]
    idx_b = jnp.argsort(-mask_b, axis=-1, stable=True)[..., :k_top].astype(jnp.int32)
    vld_b = (jnp.take_along_axis(mask_b, idx_b, axis=-1) > 0).astype(jnp.int32)
    ppb = MOBA_BLOCK // page
    pg_b = (idx_b[..., None] * ppb + jnp.arange(ppb, dtype=jnp.int32)).reshape(nb_s, MOBA_HEADS, k_top * ppb)
    tbl_b = jnp.take_along_axis(jnp.broadcast_to(pt_b, (nb_s, MOBA_HEADS, pt_b.shape[-1])), pg_b, axis=-1)
    vld_b = jnp.repeat(vld_b, ppb, axis=-1)
    ob_s = _decode_attn(ysb, cache_moba, li, tbl_b.astype(jnp.int32), vld_b, q_ch=CH_QB, kn_ch=CH_KB, vn_ch=CH_VB,
                        nh=1, rows=page, kvh=MOBA_HEADS, r_min=0,
                        name="dec_moba")

    def rows_s(part, nh):
        return jnp.pad(part[:, :, :nh].reshape(nb_s, -1), ((0, n_pad - nb_s), (0, 0)))

    sample_parts = (jnp.pad(oc_s, ((0, n_pad - nb_s), (0, 0))), rows_s(os_s, NSA_HPG), rows_s(ow_s, NSA_HPG),
                    rows_s(ob_s, 1))
    mixed = _mix((oc, os_, ow, ob), sample_parts, yg)
    x = _gmm(mixed, [(w_out[None], 0)], d, resid=x, tm=STREAM_TM, tn=1024, name="attn_out")
    hid = _gmm(_rmsnorm(x, ffn_norm), [(w_gate[None], 0), (w_up[None], 0)], w_gate.shape[1], epi="swiglu",
               out_dtype=BF, tm=STREAM_TM, tn=512, name="ffn_up")
    x = _gmm(hid, [(w_down[None], 0)], d, resid=x, tm=STREAM_TM, tn=512, name="ffn_down")

    def rows_of(lo_ch, n_ch, y=ya):
        return y[:, lo_ch * LANE:(lo_ch + n_ch) * LANE]

    def split(a, kvh):
        ap = a[:n_prompt].reshape(nb_p, t_p, 2, kvh, HEAD_DIM)
        as_ = a[n_prompt:n_prompt + nb_s].reshape(nb_s, 1, 2, kvh, HEAD_DIM)
        return ap, as_

    cmp_p, cmp_s = split(rows_of(CH_KC, 4), NSA_KV)
    sel_p, sel_s_rows = split(rows_of(CH_KS, 4), NSA_KV)
    win_rows = rows_of(CH_KW, 4)
    win_s = win_rows[n_prompt:n_prompt + nb_s].reshape(nb_s, 1, 2, NSA_KV, HEAD_DIM)
    moba_p, moba_s = split(rows_of(CH_KB, 16, yb), MOBA_HEADS)
    keep_p = min(NSA_WINDOW, t_p)
    new_win_p = _seq_tails(win_rows, nb_p, t_p, keep_p).reshape(nb_p, keep_p, 2, NSA_KV, HEAD_DIM)
    win_all = jnp.concatenate([state_win[li], win_s], axis=1)
    keep = min(NSA_WINDOW, past_len + 1)
    new_win_s = win_all[:, win_all.shape[1] - keep:]
    return x, (cmp_p, cmp_s, sel_p, sel_s_rows, new_win_p, new_win_s, moba_p, moba_s)


def _conv_layer(x, n_prompt, nb_p, t_p, nb_s, prm, states):
    (conv_norm, w_in, pool_w, pool_scale, dw, dw_b, ln_g, ln_b, pw, w_out, moe_norm, router_w, wg, wu, wd) = prm
    state_pool, state_conv = states
    d = x.shape[1]
    width = pool_w.shape[0] * pool_w.shape[1]
    xn = _rmsnorm(x, conv_norm)
    w3 = w_in[None]
    u_pool = _gmm(xn, [(w3, 0)], width, tm=512, tn=width, name="conv_in_pool")
    u_conv = _gmm(xn, [(w3, width), (w3, 2 * width)], width, epi="glu", tm=512, tn=512, name="conv_in_glu")
    prm_mix = (pool_w.astype(BF), pool_scale[None], jnp.pad(dw, ((0, 1), (0, 0))), dw_b[None], ln_g[None], ln_b[None],
               pw.astype(BF))
    pool_buf = max(POOL_WINDOWS) - 1
    conv_buf = CONV_WIDTH - 1
    hp_rows, hc_rows = 16, 32
    ts = min(256, t_p)
    nt = t_p // ts
    mixed_p = _convmix(u_pool, u_pool, u_conv, u_conv, prm_mix, ts=ts, nt=nt, n_tiles=nb_p * nt, zero_first=True,
                       avail0=0, hp_rows=hp_rows, hc_rows=hc_rows,
                       halo_blk=lambda s, hr: jnp.maximum(s * (ts // hr) - 1, 0))
    rows = x.shape[0]
    us_pool = jnp.zeros((nb_s, SUBLANE, width), F32).at[:, 0].set(u_pool[n_prompt:n_prompt + nb_s])
    us_conv = jnp.zeros((nb_s, SUBLANE, width), F32).at[:, 0].set(u_conv[n_prompt:n_prompt + nb_s])
    hs_pool = jnp.pad(state_pool, ((0, 0), (hp_rows - pool_buf, 0), (0, 0))).reshape(nb_s * hp_rows, width)
    hs_conv = jnp.pad(state_conv, ((0, 0), (hc_rows - conv_buf, 0), (0, 0))).reshape(nb_s * hc_rows, width)
    mixed_s = _convmix(us_pool.reshape(nb_s * SUBLANE, width), hs_pool, us_conv.reshape(nb_s * SUBLANE, width), hs_conv,
                       prm_mix, ts=SUBLANE, nt=1, n_tiles=nb_s, zero_first=False, avail0=pool_buf, hp_rows=hp_rows,
                       hc_rows=hc_rows, halo_blk=lambda s, hr: s)
    mixed_s = mixed_s.reshape(nb_s, SUBLANE, 2 * width)[:, 0]
    mixed = (mixed_p, jnp.pad(mixed_s, ((0, STREAM_TM - nb_s), (0, 0))), rows)
    x = _gmm(mixed, [(w_out[None], 0)], d, resid=x, tm=STREAM_TM, tn=1024, name="conv_out")
    x = _moe(x, moe_norm, router_w, wg, wu, wd, n_prompt)

    assert t_p >= conv_buf and t_p >= pool_buf
    new_pool_p = _seq_tails(u_pool, nb_p, t_p, pool_buf)
    new_conv_p = _seq_tails(u_conv, nb_p, t_p, conv_buf)
    new_pool_s = jnp.concatenate([state_pool, u_pool[n_prompt:n_prompt + nb_s][:, None]], axis=1)[:, 1:]
    new_conv_s = jnp.concatenate([state_conv, u_conv[n_prompt:n_prompt + nb_s][:, None]], axis=1)[:, 1:]
    return x, (new_pool_p, new_pool_s, new_conv_p, new_conv_s)


def kernel(x_prompt, x_sample, cache_nsa_cmp, cache_nsa_sel, state_nsa_win, cache_moba, state_pool, state_conv, page_table, attn_norm, w_attn_in, nsa_qk_norm, nsa_cmp_pe, nsa_cmp_w1, nsa_cmp_w2, moba_qk_norm, w_attn_out, ffn_norm, ffn_w_gate, ffn_w_up, ffn_w_down, conv_norm, w_conv_in, pool_w, pool_scale, conv_dw, conv_dw_b, conv_ln_g, conv_ln_b, conv_pw, w_conv_out, moe_norm, router_w, moe_w_gate, moe_w_up, moe_w_down):
    nb_p, t_p, d = x_prompt.shape
    nb_s = x_sample.shape[0]
    n_prompt = nb_p * t_p
    n_pad = 2 * SUBLANE
    assert x_sample.shape[1] == 1 and nb_s <= n_pad
    x = (x_prompt.reshape(n_prompt, d), jnp.pad(x_sample.reshape(nb_s, d), ((0, STREAM_TM - nb_s), (0, 0))),
         n_prompt + n_pad)
    li = 0
    prm_a = (attn_norm[li], w_attn_in[li], nsa_qk_norm[li], nsa_cmp_pe[li], nsa_cmp_w1[li], nsa_cmp_w2[li],
             moba_qk_norm[li], w_attn_out[li], ffn_norm[li], ffn_w_gate[li], ffn_w_up[li], ffn_w_down[li])
    x, attn_new = _attn_layer(x, n_prompt, nb_p, t_p, nb_s, li, prm_a,
                              (cache_nsa_cmp, cache_nsa_sel, state_nsa_win, cache_moba), page_table)
    prm_c = (conv_norm[li], w_conv_in[li], pool_w[li], pool_scale[li], conv_dw[li], conv_dw_b[li], conv_ln_g[li],
             conv_ln_b[li], conv_pw[li], w_conv_out[li], moe_norm[li], router_w[li], moe_w_gate[li], moe_w_up[li],
             moe_w_down[li])
    x, conv_new = _conv_layer(x, n_prompt, nb_p, t_p, nb_s, prm_c, (state_pool[li], state_conv[li]))
    cmp_p, cmp_s, sel_p, sel_s, win_p, win_s, moba_p, moba_s = attn_new
    pool_p, pool_s, conv_p, conv_s = conv_new
    y_p = x[0].reshape(nb_p, t_p, d)
    y_s = x[1][:nb_s].reshape(nb_s, 1, d)
    st = lambda a: a[None]
    return (y_p, y_s, st(cmp_p), st(cmp_s), st(sel_p), st(sel_s), st(win_p), st(win_s), st(moba_p), st(moba_s),
            st(pool_p), st(pool_s), st(conv_p), st(conv_s))
```

```python
import functools
import math

import jax
import jax.numpy as jnp
from jax import lax
from jax.experimental import pallas as pl
from jax.experimental.pallas import tpu as pltpu

F32 = jnp.float32
BF = jnp.bfloat16
NEG_INF = float("-inf")

HEAD_DIM = 128
NORM_EPS = 1e-6
NSA_HEADS = 8
NSA_KV = 2
NSA_HPG = NSA_HEADS // NSA_KV
NSA_CMP_LEN = 32
NSA_CMP_STRIDE = 16
NSA_CMP_HIDDEN = 2 * HEAD_DIM
NSA_SEL_BLOCK = 64
NSA_SEL_COUNT = 16
NSA_WINDOW = 512
NSA_SEL_FORCE = 1.0e4
MOBA_HEADS = 8
MOBA_BLOCK = 256
MOBA_TOPK = 3
POOL_WINDOWS = (2, 4, 8, 16)
CONV_WIDTH = 31
N_EXPERTS = 8
TOP_K = 2

LANE = 128
SUBLANE = 8
VMEM_LIMIT = 58 * 1024 * 1024
STREAM_TM = 512

CH_QA, CH_KC, CH_VC, CH_KS, CH_VS, CH_KW, CH_VW = 0, 8, 10, 12, 14, 16, 18
NSA_CHUNKS = 20
CH_QB, CH_KB, CH_VB = 0, 8, 16
MOBA_CHUNKS = 24
CH_GATE = 0

NT_DIMS = (((1,), (1,)), ((), ()))


def _cparams(sem):
    return pltpu.CompilerParams(dimension_semantics=sem, vmem_limit_bytes=VMEM_LIMIT)


def _split_bf16(a):
    hi = a.astype(BF)
    lo = (a - hi.astype(F32)).astype(BF)
    return hi, lo


def _dot3(a, b, dims=(((1,), (0,)), ((), ()))):
    ah, al = _split_bf16(a)
    bh, bl = _split_bf16(b)
    d = lambda x, y: lax.dot_general(x, y, dims, preferred_element_type=F32)
    return d(ah, bh) + d(ah, bl) + d(al, bh)


def _masked_softmax(s, mask, axis):
    s = jnp.where(mask, s, NEG_INF)
    m = jnp.max(s, axis=axis, keepdims=True)
    m = jnp.where(m == NEG_INF, 0.0, m)
    p = jnp.exp(s - m)
    d = jnp.sum(p, axis=axis, keepdims=True)
    return p / jnp.where(d > 0, d, 1.0)


def _split_stream(a, tm):
    if not isinstance(a, tuple):
        return [a], [lambda i: i], None, a.shape[0]
    main, tail, m = a
    n_full = main.shape[0] // tm
    assert n_full * tm == main.shape[0] and tail.shape[0] == tm and n_full * tm < m <= (n_full + 1) * tm
    return [main, tail], [lambda i: jnp.minimum(i, n_full - 1), lambda i: 0], n_full, m


def _pick_tile(refs, i, n_full):
    return refs[0][...] if n_full is None else jnp.where(i < n_full, refs[0][...], refs[1][...])


def _rmsnorm_body(*refs, n_full):
    x = _pick_tile(refs[:-2], pl.program_id(0), n_full)
    g_ref, o_ref = refs[-2:]
    ms = jnp.mean(x * x, axis=-1, keepdims=True)
    o_ref[...] = (x * lax.rsqrt(ms + NORM_EPS) * g_ref[...]).astype(o_ref.dtype)


def _rmsnorm(x, g, tm=256):
    arrs, maps, n_full, m = _split_stream(x, tm)
    d = arrs[0].shape[1]
    return pl.pallas_call(
        functools.partial(_rmsnorm_body, n_full=n_full),
        grid=(pl.cdiv(m, tm),),
        in_specs=[pl.BlockSpec((tm, d), lambda i, f=f: (f(i), 0)) for f in maps] + [pl.BlockSpec((1, d), lambda i: (0, 0))],
        out_specs=pl.BlockSpec((tm, d), lambda i: (i, 0)),
        out_shape=jax.ShapeDtypeStruct((m, d), BF),
        compiler_params=_cparams(("parallel",)),
        name="rmsnorm",
    )(*arrs, g.reshape(1, d))


def _gmm_body(texp_ref, tval_ref, *refs, n_x, x_full, n_w, cast_w, w_t, epi, n_r, r_full, tn, n_o):
    mm = (lambda x, w: lax.dot_general(x, w, NT_DIMS, preferred_element_type=F32)) if w_t else (
        lambda x, w: jnp.dot(x, w, preferred_element_type=F32))
    x_refs = refs[:n_x]
    w_refs = refs[n_x:n_x + n_w]
    pos = n_x + n_w
    if epi == "headnorm":
        gain_ref, flag_ref = refs[pos], refs[pos + 1]
        pos += 2
    resid_refs = refs[pos:pos + n_r]
    pos += n_r
    o_refs = refs[pos:pos + n_o]
    wb_refs = refs[pos + n_o:pos + n_o + n_w] if cast_w else w_refs

    i = pl.program_id(1)
    if cast_w:
        changed = jnp.logical_or(i == 0, texp_ref[i] != texp_ref[jnp.maximum(i - 1, 0)])

        @pl.when(changed)
        def _():
            for k in range(n_w):
                wb_refs[k][...] = w_refs[k][...].astype(BF)

    @pl.when(tval_ref[i] == 0)
    def _():
        for o in o_refs:
            o[...] = jnp.zeros(o.shape, o.dtype)

    @pl.when(tval_ref[i] > 0)
    def _():
        x = _pick_tile(x_refs, i, x_full)
        a = mm(x, wb_refs[0][...])
        if epi == "swiglu":
            b = mm(x, wb_refs[1][...])
            y = a * jax.nn.sigmoid(a) * b
        elif epi == "glu":
            b = mm(x, wb_refs[1][...])
            y = a * jax.nn.sigmoid(b)
        else:
            y = a
        if n_r:
            y = y + _pick_tile(resid_refs, i, r_full)
        if epi == "headnorm":
            for c in range(tn // LANE):
                sl = slice(c * LANE, (c + 1) * LANE)
                yc = y[:, sl]
                r = lax.rsqrt(jnp.mean(yc * yc, axis=-1, keepdims=True) + NORM_EPS)
                f = flag_ref[:, sl]
                yn = yc * (f * r + (1.0 - f)) * gain_ref[:, sl]
                for o in o_refs:
                    o[:, sl] = yn.astype(o.dtype)
        else:
            for o in o_refs:
                o[...] = y.astype(o.dtype)


def _gmm(x, ws, n_out, *, epi="none", texp=None, tval=None, resid=None, gain=None, flag=None,
         out_dtype=F32, tm=256, tn=512, w_t=False, name="gmm"):
    x_arrs, x_maps, x_full, m = _split_stream(x, tm)
    kdim = x_arrs[0].shape[1]
    tn = min(tn, n_out)
    n_m = pl.cdiv(m, tm)
    n_n = n_out // tn
    assert n_n * tn == n_out
    if texp is None:
        texp = jnp.zeros((n_m,), jnp.int32)
        tval = jnp.ones((n_m,), jnp.int32)
    n_w = len(ws)
    cast_w = ws[0][0].dtype != BF
    in_specs = [pl.BlockSpec((tm, kdim), lambda j, i, te, tv, f=f: (f(i), 0)) for f in x_maps]
    args = list(x_arrs)
    for w, off in ws:
        assert off % tn == 0 and w.shape[2 if w_t else 1] == kdim and (w.dtype != BF) == cast_w
        ob = off // tn
        if w_t:
            in_specs.append(pl.BlockSpec((None, tn, kdim), lambda j, i, te, tv, ob=ob: (te[i], j + ob, 0)))
        else:
            in_specs.append(pl.BlockSpec((None, kdim, tn), lambda j, i, te, tv, ob=ob: (te[i], 0, j + ob)))
        args.append(w)
    if epi == "headnorm":
        in_specs += [pl.BlockSpec((1, tn), lambda j, i, te, tv: (0, j))] * 2
        args += [gain, flag]
    n_r, r_full = 0, None
    if resid is not None:
        r_arrs, r_maps, r_full, r_m = _split_stream(resid, tm)
        assert r_m == m
        n_r = len(r_arrs)
        in_specs += [pl.BlockSpec((tm, tn), lambda j, i, te, tv, f=f: (f(i), j)) for f in r_maps]
        args += r_arrs
    dtypes = out_dtype if isinstance(out_dtype, tuple) else (out_dtype,)
    body = functools.partial(_gmm_body, n_x=len(x_arrs), x_full=x_full, n_w=n_w, cast_w=cast_w, w_t=w_t, epi=epi,
                             n_r=n_r, r_full=r_full, tn=tn, n_o=len(dtypes))
    outs = pl.pallas_call(
        body,
        grid_spec=pltpu.PrefetchScalarGridSpec(
            num_scalar_prefetch=2,
            grid=(n_n, n_m),
            in_specs=in_specs,
            out_specs=[pl.BlockSpec((tm, tn), lambda j, i, te, tv: (i, j)) for _ in dtypes],
            scratch_shapes=[pltpu.VMEM((tn, kdim) if w_t else (kdim, tn), BF) for _ in range(n_w if cast_w else 0)],
        ),
        out_shape=[jax.ShapeDtypeStruct((m, n_out), dt) for dt in dtypes],
        compiler_params=_cparams(("arbitrary", "arbitrary")),
        name=name,
    )(texp, tval, *args)
    return outs if isinstance(out_dtype, tuple) else outs[0]


def _c1_body(x_ref, w_ref, o_ref):
    nsb = x_ref.shape[0] // NSA_CMP_STRIDE
    acc = jnp.zeros((nsb, 2 * NSA_CMP_HIDDEN), F32)
    for s in range(NSA_CMP_STRIDE):
        xs = x_ref[pl.ds(s, nsb, stride=NSA_CMP_STRIDE), :]
        acc = acc + jnp.dot(xs.astype(BF), w_ref[s], preferred_element_type=F32)
    o_ref[...] = acc


def _compress_c1_prompt(y, wcat, nb, t):
    nsub = t // NSA_CMP_STRIDE
    return pl.pallas_call(
        _c1_body,
        grid=(nb, 2, NSA_KV),
        in_specs=[pl.BlockSpec((t, LANE), lambda b, kv, g: (b, CH_KC + 2 * kv + g)),
                  pl.BlockSpec((None, NSA_CMP_STRIDE, HEAD_DIM, 2 * NSA_CMP_HIDDEN), lambda b, kv, g: (kv, 0, 0, 0))],
        out_specs=pl.BlockSpec((None, None, None, nsub, 2 * NSA_CMP_HIDDEN), lambda b, kv, g: (b, kv, g, 0, 0)),
        out_shape=jax.ShapeDtypeStruct((nb, 2, NSA_KV, nsub, 2 * NSA_CMP_HIDDEN), F32),
        compiler_params=_cparams(("parallel", "parallel", "parallel")),
        name="cmp_c1_prompt",
    )(y, wcat)


C1_PAGES = 16


def _c1_sample_body(pt_ref, *refs, n_in, page):
    x_refs = refs[:n_in]
    w_ref, o_ref, scr = refs[n_in:n_in + 3]
    nsb = page // NSA_CMP_STRIDE
    rows = n_in * nsb
    for kv in range(2):
        acc = jnp.zeros((rows * NSA_KV, 2 * NSA_CMP_HIDDEN), F32)

        def rows_at(s):
            parts = [xr[pl.ds(s, nsb, stride=NSA_CMP_STRIDE), kv, :, :].reshape(nsb * NSA_KV, HEAD_DIM) for xr in x_refs]
            return (parts[0] if n_in == 1 else jnp.concatenate(parts, axis=0)).astype(BF)

        for s in range(0, NSA_CMP_STRIDE, 2):
            xs = jnp.concatenate([rows_at(s), rows_at(s + 1)], axis=1)
            w = w_ref[kv, s:s + 2].reshape(2 * HEAD_DIM, 2 * NSA_CMP_HIDDEN)
            acc = acc + jnp.dot(xs, w, preferred_element_type=F32)
        for c in range(scr.shape[0]):
            scr[c] = acc[:, c * LANE:(c + 1) * LANE]
        for g in range(NSA_KV):
            for c in range(scr.shape[0]):
                o_ref[kv, g, :, c * LANE:(c + 1) * LANE] = scr[c, pl.ds(g, rows, stride=NSA_KV), :]


def _compress_c1_sample(cache, li, page_table, wcat):
    nb, n_pages = page_table.shape
    page = cache.shape[2]
    npp = math.gcd(C1_PAGES, n_pages)
    nsb = page // NSA_CMP_STRIDE
    nsub = n_pages * nsb
    body = functools.partial(_c1_sample_body, n_in=npp, page=page)
    in_specs = [
        pl.BlockSpec((None, None, page, 2, NSA_KV, HEAD_DIM), lambda b, j, pt, i=i: (li, pt[b, j * npp + i], 0, 0, 0, 0))
        for i in range(npp)
    ]
    in_specs.append(pl.BlockSpec(wcat.shape, lambda b, j, pt: (0, 0, 0, 0)))
    return pl.pallas_call(
        body,
        grid_spec=pltpu.PrefetchScalarGridSpec(
            num_scalar_prefetch=1,
            grid=(nb, n_pages // npp),
            in_specs=in_specs,
            out_specs=pl.BlockSpec((None, 2, NSA_KV, npp * nsb, 2 * NSA_CMP_HIDDEN), lambda b, j, pt: (b, 0, 0, j, 0)),
            scratch_shapes=[pltpu.VMEM((2 * NSA_CMP_HIDDEN // LANE, npp * nsb * NSA_KV, LANE), F32)],
        ),
        out_shape=jax.ShapeDtypeStruct((nb, 2, NSA_KV, nsub, 2 * NSA_CMP_HIDDEN), F32),
        compiler_params=_cparams(("parallel", "parallel")),
        name="cmp_c1_sample",
    )(page_table, *([cache] * npp), wcat)


def _c2_body(p_ref, pe_ref, wcat_ref, w2_ref, g_ref, o_ref, *, nsub, n_c):
    kv = pl.program_id(1)
    pep = jnp.zeros((SUBLANE, 2 * NSA_CMP_HIDDEN), F32)
    for s in range(NSA_CMP_STRIDE):
        pep = pep + jnp.dot(pe_ref[s], wcat_ref[s], preferred_element_type=F32)
    bias = pep[0:1, :NSA_CMP_HIDDEN] + pep[1:2, NSA_CMP_HIDDEN:]
    nxt = pltpu.roll(p_ref[:, NSA_CMP_HIDDEN:], nsub - 1, 0)
    h = p_ref[:, :NSA_CMP_HIDDEN] + nxt + bias
    h = h * jax.nn.sigmoid(h)
    o = jnp.dot(h.astype(BF), w2_ref[...], preferred_element_type=F32)
    r = lax.rsqrt(jnp.mean(o * o, axis=-1, keepdims=True) + NORM_EPS)
    o = jnp.where(kv == 0, o * r * g_ref[...], o)
    row = lax.broadcasted_iota(jnp.int32, (nsub, 1), 0)
    o_ref[...] = jnp.where(row < n_c, o, 0.0)


def _compress_c2(p, pe_rows, wcat, w2, gk):
    nb, _, _, nsub, _ = p.shape
    n_c = nsub - NSA_CMP_LEN // NSA_CMP_STRIDE + 1
    body = functools.partial(_c2_body, nsub=nsub, n_c=n_c)
    return pl.pallas_call(
        body,
        grid=(nb, 2, NSA_KV),
        in_specs=[pl.BlockSpec((None, None, None, nsub, 2 * NSA_CMP_HIDDEN), lambda b, kv, g: (b, kv, g, 0, 0)),
                  pl.BlockSpec((None, NSA_CMP_STRIDE, SUBLANE, HEAD_DIM), lambda b, kv, g: (kv, 0, 0, 0)),
                  pl.BlockSpec((None, NSA_CMP_STRIDE, HEAD_DIM, 2 * NSA_CMP_HIDDEN), lambda b, kv, g: (kv, 0, 0, 0)),
                  pl.BlockSpec((None, NSA_CMP_HIDDEN, HEAD_DIM), lambda b, kv, g: (kv, 0, 0)),
                  pl.BlockSpec((1, HEAD_DIM), lambda b, kv, g: (0, 0))],
        out_specs=pl.BlockSpec((None, None, None, nsub, HEAD_DIM), lambda b, kv, g: (b, kv, g, 0, 0)),
        out_shape=jax.ShapeDtypeStruct((nb, 2, NSA_KV, nsub, HEAD_DIM), F32),
        compiler_params=_cparams(("parallel", "parallel", "parallel")),
        name="cmp_c2",
    )(p, pe_rows, wcat, w2, gk)


def _rank_select(sc_ref, n_iter, k):
    score = sc_ref[...]
    brow = lax.broadcasted_iota(jnp.int32, score.shape, 0)

    def body(j, rank):
        r = sc_ref[pl.ds(j, 1), :]
        beats = jnp.logical_or(r > score, jnp.logical_and(r == score, j < brow))
        return rank + jnp.where(beats, 1.0, 0.0)

    rank = lax.fori_loop(0, n_iter, body, jnp.zeros(score.shape, F32))
    return rank < k


def _cattn_body(q_ref, kc_ref, vc_ref, oc_ref, sel_ref, sc_ref, *, tq, ncp, n_c, ns, nsp, pos0, ksel):
    t0 = pos0 + pl.program_id(2) * tq
    scale = HEAD_DIM ** -0.5
    kc = kc_ref[...].astype(BF)
    vc = vc_ref[...].astype(BF)
    trow = t0 + lax.broadcasted_iota(jnp.int32, (tq, 1), 0)
    ncol = lax.broadcasted_iota(jnp.int32, (1, ncp), 1)
    valid = jnp.logical_and(ncol * NSA_CMP_STRIDE + (NSA_CMP_LEN - 1) <= trow, ncol < n_c)
    tcol = t0 + lax.broadcasted_iota(jnp.int32, (1, tq), 1)
    nrow = lax.broadcasted_iota(jnp.int32, (ncp, 1), 0)
    valid_t = jnp.logical_and(nrow * NSA_CMP_STRIDE + (NSA_CMP_LEN - 1) <= tcol, nrow < n_c)
    psum_t = jnp.zeros((ncp, tq), F32)
    for z in range(NSA_HPG):
        sl = slice(z * HEAD_DIM, (z + 1) * HEAD_DIM)
        q = (q_ref[:, sl] * scale).astype(BF)
        s = lax.dot_general(q, kc, NT_DIMS, preferred_element_type=F32)
        p = _masked_softmax(s, valid, -1)
        oc_ref[:, sl] = jnp.dot(p.astype(BF), vc, preferred_element_type=F32)
        s_t = lax.dot_general(kc, q, NT_DIMS, preferred_element_type=F32)
        psum_t = psum_t + _masked_softmax(s_t, valid_t, 0)
    r = NSA_SEL_BLOCK // NSA_CMP_STRIDE
    brow = lax.broadcasted_iota(jnp.int32, (nsp, 1), 0)
    lo = r * brow - 1
    inside = jnp.logical_and(ncol >= lo, ncol <= lo + r)
    edge = jnp.logical_or(ncol == lo, ncol == lo + r)
    m_t = jnp.where(inside, jnp.where(edge, 0.5, 1.0), 0.0).astype(BF)
    hi = psum_t.astype(BF)
    mid = (psum_t - hi.astype(F32)).astype(BF)
    low = (psum_t - hi.astype(F32) - mid.astype(F32)).astype(BF)
    imp_t = (jnp.dot(m_t, hi, preferred_element_type=F32) + jnp.dot(m_t, mid, preferred_element_type=F32)
             + jnp.dot(m_t, low, preferred_element_type=F32))
    own = tcol // NSA_SEL_BLOCK
    past = jnp.logical_and(brow < own, brow < ns)
    forced = jnp.logical_or(brow == 0, brow == own - 1)
    sc_ref[...] = jnp.where(past, jnp.where(forced, NSA_SEL_FORCE, imp_t), NEG_INF)
    chosen = jnp.logical_and(past, _rank_select(sc_ref, ns, ksel))
    sel = jnp.where(jnp.logical_or(chosen, brow == own), 1.0, 0.0)
    rows = sel_ref.shape[0]
    if rows > nsp:
        sel_ref[...] = jnp.zeros(sel_ref.shape, F32)
    sel_ref[0:nsp, :] = sel


def _cattn(q_arr, q_row_blk0, q_col_blk0, kvc, nb, tlen, tq, length, pos0, out_rows):
    nq = tlen // tq
    ncp = kvc.shape[3]
    n_c = ncp - NSA_CMP_LEN // NSA_CMP_STRIDE + 1
    ns = length // NSA_SEL_BLOCK
    ksel = min(NSA_SEL_COUNT - 1, ns)
    nsp = -(-ns // SUBLANE) * SUBLANE
    sel_rows = max(nsp, LANE)
    body = functools.partial(_cattn_body, tq=tq, ncp=ncp, n_c=n_c, ns=ns, nsp=nsp, pos0=pos0, ksel=ksel)
    gw = NSA_HPG * HEAD_DIM
    return pl.pallas_call(
        body,
        grid=(nb, NSA_KV, nq),
        in_specs=[pl.BlockSpec((tq, gw), lambda b, g, qi: (q_row_blk0 + b * nq + qi, q_col_blk0 + g)),
                  pl.BlockSpec((None, None, None, ncp, HEAD_DIM), lambda b, g, qi: (b, 0, g, 0, 0)),
                  pl.BlockSpec((None, None, None, ncp, HEAD_DIM), lambda b, g, qi: (b, 1, g, 0, 0))],
        out_specs=[pl.BlockSpec((tq, gw), lambda b, g, qi: (b * nq + qi, g)),
                   pl.BlockSpec((None, None, sel_rows, tq), lambda b, g, qi: (b, g, 0, qi))],
        out_shape=[jax.ShapeDtypeStruct((out_rows, NSA_KV * gw), F32),
                   jax.ShapeDtypeStruct((nb, NSA_KV, sel_rows, tlen), F32)],
        scratch_shapes=[pltpu.VMEM((nsp, tq), F32)],
        compiler_params=_cparams(("parallel", "parallel", "parallel")),
        name="nsa_cmp_attn",
    )(q_arr, kvc, kvc)


def _cattn_row_body(q_ref, kc_ref, vc_ref, oc_ref, sel_ref, *, tq, ncp, n_c, ns, nsl, pos0, ksel):
    scale = HEAD_DIM ** -0.5
    kc = kc_ref[...].astype(BF)
    vc = vc_ref[...].astype(BF)
    trow = pos0 + lax.broadcasted_iota(jnp.int32, (tq, 1), 0)
    ncol = lax.broadcasted_iota(jnp.int32, (1, ncp), 1)
    valid = jnp.logical_and(ncol * NSA_CMP_STRIDE + (NSA_CMP_LEN - 1) <= trow, ncol < n_c)
    psum = jnp.zeros((tq, ncp), F32)
    for z in range(NSA_HPG):
        sl = slice(z * HEAD_DIM, (z + 1) * HEAD_DIM)
        q = (q_ref[:, sl] * scale).astype(BF)
        p = _masked_softmax(lax.dot_general(q, kc, NT_DIMS, preferred_element_type=F32), valid, -1)
        oc_ref[:, sl] = jnp.dot(p.astype(BF), vc, preferred_element_type=F32)
        psum = psum + p
    r = NSA_SEL_BLOCK // NSA_CMP_STRIDE
    nrow = lax.broadcasted_iota(jnp.int32, (ncp, 1), 0)
    bcol = lax.broadcasted_iota(jnp.int32, (1, nsl), 1)
    lo = r * bcol - 1
    inside = jnp.logical_and(nrow >= lo, nrow <= lo + r)
    edge = jnp.logical_or(nrow == lo, nrow == lo + r)
    m = jnp.where(inside, jnp.where(edge, 0.5, 1.0), 0.0).astype(BF)
    hi = psum.astype(BF)
    mid = (psum - hi.astype(F32)).astype(BF)
    low = (psum - hi.astype(F32) - mid.astype(F32)).astype(BF)
    imp = (jnp.dot(hi, m, preferred_element_type=F32) + jnp.dot(mid, m, preferred_element_type=F32)
           + jnp.dot(low, m, preferred_element_type=F32))
    own = trow // NSA_SEL_BLOCK
    past = jnp.logical_and(bcol < own, bcol < ns)
    forced = jnp.logical_or(bcol == 0, bcol == own - 1)
    score = jnp.where(past, jnp.where(forced, NSA_SEL_FORCE, imp), NEG_INF)
    rank = jnp.zeros((tq, nsl), F32)
    for j in range(ns):
        cj = score[:, j:j + 1]
        beats = jnp.logical_or(cj > score, jnp.logical_and(cj == score, j < bcol))
        rank = rank + jnp.where(beats, 1.0, 0.0)
    chosen = jnp.logical_and(past, rank < ksel)
    sel_ref[...] = jnp.where(jnp.logical_or(chosen, bcol == own), 1.0, 0.0)


def _cattn_sample(q_pad, kvc, nb, length, pos0):
    tq = SUBLANE
    ncp = kvc.shape[3]
    n_c = ncp - NSA_CMP_LEN // NSA_CMP_STRIDE + 1
    ns = length // NSA_SEL_BLOCK
    ksel = min(NSA_SEL_COUNT - 1, ns)
    nsl = -(-ns // LANE) * LANE
    body = functools.partial(_cattn_row_body, tq=tq, ncp=ncp, n_c=n_c, ns=ns, nsl=nsl, pos0=pos0, ksel=ksel)
    gw = NSA_HPG * HEAD_DIM
    return pl.pallas_call(
        body,
        grid=(nb, NSA_KV),
        in_specs=[pl.BlockSpec((tq, gw), lambda b, g: (b, g)),
                  pl.BlockSpec((None, None, None, ncp, HEAD_DIM), lambda b, g: (b, 0, g, 0, 0)),
                  pl.BlockSpec((None, None, None, ncp, HEAD_DIM), lambda b, g: (b, 1, g, 0, 0))],
        out_specs=[pl.BlockSpec((tq, gw), lambda b, g: (b, g)),
                   pl.BlockSpec((None, None, tq, nsl), lambda b, g: (b, g, 0, 0))],
        out_shape=[jax.ShapeDtypeStruct((nb * tq, NSA_KV * gw), F32),
                   jax.ShapeDtypeStruct((nb, NSA_KV, tq, nsl), F32)],
        compiler_params=_cparams(("parallel", "parallel")),
        name="nsa_cmp_attn_sample",
    )(q_pad, kvc, kvc)


def _gate_body(q_ref, k_ref, sel_ref, mean_ref, sc_ref, *, tq, nb, nbp, pos0):
    qi = pl.program_id(2)

    @pl.when(qi == 0)
    def _():
        mean_ref[...] = jnp.zeros(mean_ref.shape, F32)
        for j in range(nb):
            blk = k_ref[j * MOBA_BLOCK:(j + 1) * MOBA_BLOCK, :]
            mean_ref[j:j + 1, :] = jnp.sum(blk, axis=0, keepdims=True) * (1.0 / MOBA_BLOCK)

    g_t = _dot3(mean_ref[...], q_ref[...], NT_DIMS)
    tcol = pos0 + qi * tq + lax.broadcasted_iota(jnp.int32, (1, tq), 1)
    own = tcol // MOBA_BLOCK
    brow = lax.broadcasted_iota(jnp.int32, (nbp, 1), 0)
    past = jnp.logical_and(brow < own, brow < nb)
    sc_ref[...] = jnp.where(past, g_t, NEG_INF)
    chosen = jnp.logical_and(past, _rank_select(sc_ref, nb, min(MOBA_TOPK, nb)))
    sel = jnp.where(jnp.logical_or(chosen, brow == own), 1.0, 0.0)
    if sel_ref.shape[0] > nbp:
        sel_ref[...] = jnp.zeros(sel_ref.shape, F32)
    sel_ref[0:nbp, :] = sel


def _moba_gate_prompt(y, nb, t, tq):
    nq = t // tq
    n_full = t // MOBA_BLOCK
    nbp = -(-n_full // SUBLANE) * SUBLANE
    body = functools.partial(_gate_body, tq=tq, nb=n_full, nbp=nbp, pos0=0)
    return pl.pallas_call(
        body,
        grid=(nb, MOBA_HEADS, nq),
        in_specs=[pl.BlockSpec((tq, LANE), lambda b, h, qi: (b * nq + qi, CH_QB + h)),
                  pl.BlockSpec((t, LANE), lambda b, h, qi: (b, CH_KB + h))],
        out_specs=pl.BlockSpec((None, None, LANE, tq), lambda b, h, qi: (b, h, 0, qi)),
        out_shape=jax.ShapeDtypeStruct((nb, MOBA_HEADS, LANE, t), F32),
        scratch_shapes=[pltpu.VMEM((nbp, HEAD_DIM), F32), pltpu.VMEM((nbp, tq), F32)],
        compiler_params=_cparams(("parallel", "parallel", "arbitrary")),
        name="moba_gate_prompt",
    )(y, y)


def _gate_sample_body(q_ref, m_ref, sel_ref, *, nb, pos0):
    tq = q_ref.shape[0]
    own = (pos0 + lax.broadcasted_iota(jnp.int32, (tq, 1), 0)) // MOBA_BLOCK
    bcol = lax.broadcasted_iota(jnp.int32, (1, nb), 1)
    past = bcol < own
    for h in range(MOBA_HEADS):
        g = _dot3(q_ref[:, h * HEAD_DIM:(h + 1) * HEAD_DIM], m_ref[h], NT_DIMS)
        score = jnp.where(past, g, NEG_INF)
        rank = jnp.zeros((tq, nb), F32)
        for j in range(nb):
            cj = score[:, j:j + 1]
            beats = jnp.logical_or(cj > score, jnp.logical_and(cj == score, j < bcol))
            rank = rank + jnp.where(beats, 1.0, 0.0)
        chosen = jnp.logical_and(past, rank < min(MOBA_TOPK, nb))
        sel_ref[h] = jnp.where(jnp.logical_or(chosen, bcol == own), 1.0, 0.0)


def _moba_gate_sample(q_pad, means, pos0):
    nb, _, n_full, _ = means.shape
    tq = SUBLANE
    body = functools.partial(_gate_sample_body, nb=n_full, pos0=pos0)
    return pl.pallas_call(
        body,
        grid=(nb,),
        in_specs=[pl.BlockSpec((tq, MOBA_HEADS * HEAD_DIM), lambda b: (b, 0)),
                  pl.BlockSpec((None, MOBA_HEADS, n_full, HEAD_DIM), lambda b: (b, 0, 0, 0))],
        out_specs=pl.BlockSpec((None, MOBA_HEADS, tq, n_full), lambda b: (b, 0, 0, 0)),
        out_shape=jax.ShapeDtypeStruct((nb, MOBA_HEADS, tq, n_full), F32),
        compiler_params=_cparams(("parallel",)),
        name="moba_gate_sample",
    )(q_pad, means)


MEANS_BLOCKS = 4


def _means_body(pt_ref, *refs, ppb, bps):
    c_refs = refs[:bps * ppb]
    o_ref = refs[bps * ppb]
    j = pl.program_id(1)
    for q in range(bps):
        acc = jnp.sum(c_refs[q * ppb][...], axis=0)
        for i in range(1, ppb):
            acc = acc + jnp.sum(c_refs[q * ppb + i][...], axis=0)
        m = acc * (1.0 / MOBA_BLOCK)
        for h in range(MOBA_HEADS):
            o_ref[h, pl.ds(j * bps + q, 1), :] = m[h:h + 1, :]


def _moba_means_sample(cache, li, page_table):
    nb, n_pages = page_table.shape
    page = cache.shape[2]
    ppb = MOBA_BLOCK // page
    n_full = n_pages // ppb
    bps = math.gcd(MEANS_BLOCKS, n_full)
    body = functools.partial(_means_body, ppb=ppb, bps=bps)
    in_specs = [
        pl.BlockSpec((None, None, page, None, MOBA_HEADS, HEAD_DIM),
                     lambda b, j, pt, i=i: (li, pt[b, j * bps * ppb + i], 0, 0, 0, 0))
        for i in range(bps * ppb)
    ]
    return pl.pallas_call(
        body,
        grid_spec=pltpu.PrefetchScalarGridSpec(
            num_scalar_prefetch=1,
            grid=(nb, n_full // bps),
            in_specs=in_specs,
            out_specs=pl.BlockSpec((None, MOBA_HEADS, n_full, HEAD_DIM), lambda b, j, pt: (b, 0, 0, 0)),
        ),
        out_shape=jax.ShapeDtypeStruct((nb, MOBA_HEADS, n_full, HEAD_DIM), F32),
        compiler_params=_cparams(("parallel", "arbitrary")),
        name="moba_means_sample",
    )(page_table, *([cache] * (bps * ppb)))


FLASH_UNIT_ROWS = 256
LOG2_E = 1.4426950408889634
MASK_BIG = 2.0 ** 60


def _flash_body(*refs, tq, tk, nh, blk, window, masked, ur):
    q_ref, k_ref, v_ref = refs[:3]
    pos = 3
    if masked:
        sel_ref = refs[pos]
        pos += 1
    o_ref, qs_ref, kt_ref = refs[pos:pos + 3]
    pos += 1
    units = [(z, r0) for z in range(nh) for r0 in range(0, tq, ur)]
    stat = refs[pos + 2:pos + 2 + 2 * len(units)]
    m_refs, acc_refs = stat[0::2], stat[1::2]
    qi = pl.program_id(2)
    t0 = qi * tq
    scale = HEAD_DIM ** -0.5 * LOG2_E
    if masked:
        bias = ((jnp.transpose(sel_ref[...]) - 1.0) * MASK_BIG).astype(BF)
    for z in range(nh):
        qz = (q_ref[:, z * HEAD_DIM:(z + 1) * HEAD_DIM] * scale).astype(BF)
        qs_ref[z] = jnp.concatenate([qz, bias], axis=1) if masked else qz
    for u in range(len(units)):
        m_refs[u][...] = jnp.full(m_refs[u].shape, NEG_INF, F32)
        acc_refs[u][...] = jnp.zeros(acc_refs[u].shape, F32)
    trow = t0 + lax.broadcasted_iota(jnp.int32, (tq, 1), 0)

    @pl.when(qi == 0)
    def _():
        for c in range(kt_ref.shape[0]):
            kc = k_ref[c * tk:(c + 1) * tk, :]
            if masked:
                prow = c * tk + lax.broadcasted_iota(jnp.int32, (tk, 1), 0)
                jcol = lax.broadcasted_iota(jnp.int32, (1, LANE), 1)
                kc = jnp.concatenate([kc, jnp.where(prow // blk == jcol, 1.0, 0.0).astype(BF)], axis=1)
            kt_ref[c] = kc.T

    def process(c, edge):
        ks = pl.multiple_of(c * tk, tk)
        kt = kt_ref[c]
        vc = jnp.concatenate([v_ref[pl.ds(ks, tk), :], jnp.ones((tk, LANE), BF)], axis=1)
        if edge:
            pcol = ks + lax.broadcasted_iota(jnp.int32, (1, tk), 1)
            valid = pcol <= trow
            if window:
                valid = jnp.logical_and(valid, pcol > trow - window)
        for u, (z, r0) in enumerate(units):
            s = jnp.dot(qs_ref[z, r0:r0 + ur, :], kt, preferred_element_type=F32)
            m_old = m_refs[u][...]
            if edge:
                s = jnp.where(valid[r0:r0 + ur], s, NEG_INF)
                m_new = jnp.maximum(m_old, jnp.max(s, axis=-1, keepdims=True))
                m_use = jnp.where(m_new == NEG_INF, 0.0, m_new)
            else:
                m_new = jnp.maximum(m_old, jnp.max(s, axis=-1, keepdims=True))
                m_use = m_new
            alpha = jnp.exp2(m_old - m_use)
            p = jnp.exp2(s - m_use)
            acc_refs[u][...] = alpha * acc_refs[u][...] + jnp.dot(p.astype(BF), vc, preferred_element_type=F32)
            m_refs[u][...] = m_new

    if masked:
        def pair(c2, carry):
            process(2 * c2, False)
            process(2 * c2 + 1, False)
            return carry

        lax.fori_loop(0, qi // 2, pair, 0)

        @pl.when(qi % 2 == 1)
        def _():
            process(qi - 1, False)

        process(qi, True)
    else:
        c_hi = (t0 + tq + tk - 1) // tk
        c_lo = jnp.maximum(t0 - window + 1, 0) // tk
        lax.fori_loop(c_lo, c_hi, lambda c, carry: (process(c, True), carry)[1], 0)
    for u, (z, r0) in enumerate(units):
        l = acc_refs[u][:, HEAD_DIM:HEAD_DIM + 1]
        o_ref[r0:r0 + ur, z * HEAD_DIM:(z + 1) * HEAD_DIM] = acc_refs[u][:, :HEAD_DIM] / jnp.where(l > 0, l, 1.0)


def _flash(y, y16, sel, nb, t, *, q_ch, k_ch, v_ch, kvh, nh, blk, window, out_rows, tq=512, tk=512):
    tq = min(tq, t)
    tk = min(tk, t)
    nq = t // tq
    masked = sel is not None
    gw = nh * HEAD_DIM
    assert q_ch % nh == 0 and (tq == tk or not masked)
    in_specs = [pl.BlockSpec((tq, gw), lambda b, h, qi: (b * nq + qi, q_ch // nh + h)),
                pl.BlockSpec((t, LANE), lambda b, h, qi: (b, k_ch + h)),
                pl.BlockSpec((t, LANE), lambda b, h, qi: (b, v_ch + h))]
    args = [y, y16, y16]
    if masked:
        in_specs.append(pl.BlockSpec((None, None, LANE, tq), lambda b, h, qi: (b, h, 0, qi)))
        args.append(sel)
    ur = min(FLASH_UNIT_ROWS, tq)
    n_units = nh * (tq // ur)
    kd = (2 if masked else 1) * HEAD_DIM
    body = functools.partial(_flash_body, tq=tq, tk=tk, nh=nh, blk=blk, window=window, masked=masked, ur=ur)
    unit_scratch = [pltpu.VMEM((ur, 1), F32), pltpu.VMEM((ur, 2 * HEAD_DIM), F32)]
    return pl.pallas_call(
        body,
        grid=(nb, kvh, nq),
        in_specs=in_specs,
        out_specs=pl.BlockSpec((tq, gw), lambda b, h, qi: (b * nq + qi, h)),
        out_shape=jax.ShapeDtypeStruct((out_rows, kvh * gw), F32),
        scratch_shapes=[pltpu.VMEM((nh, tq, kd), BF), pltpu.VMEM((t // tk, kd, tk), BF)] + unit_scratch * n_units,
        compiler_params=_cparams(("parallel", "parallel", "arbitrary")),
        name="flash_" + ("win" if window else "blk%d" % blk),
    )(*args)


def _dec_body(tbl_ref, vld_ref, q_ref, *refs, nh, rows, kvh, nblk, r_min):
    k_refs = refs[:nblk]
    v_refs = refs[nblk:2 * nblk]
    kn_ref, vn_ref, o_ref = refs[2 * nblk:2 * nblk + 3]
    b = pl.program_id(0)
    h = pl.program_id(1)
    scale = HEAD_DIM ** -0.5
    qrow = q_ref[pl.ds(b, 1), :] * scale
    zrow = lax.broadcasted_iota(jnp.int32, (SUBLANE, 1), 0)
    qm = jnp.zeros((SUBLANE, HEAD_DIM), F32)
    for z in range(nh):
        qm = jnp.where(zrow == z, qrow[:, z * HEAD_DIM:(z + 1) * HEAD_DIM], qm)
    qb = qm.astype(BF)
    col = lax.broadcasted_iota(jnp.int32, (1, rows * kvh), 1)
    valid = jnp.logical_and(col % kvh == h, col // kvh >= r_min)
    kn = kn_ref[pl.ds(b, 1), :]
    vn = vn_ref[pl.ds(b, 1), :]
    s_new = jnp.sum(qm * kn, axis=-1, keepdims=True)
    scores = []
    m = s_new
    for i in range(nblk):
        kk = k_refs[i][...].reshape(rows * kvh, HEAD_DIM).astype(BF)
        s = lax.dot_general(qb, kk, NT_DIMS, preferred_element_type=F32)
        s = jnp.where(jnp.logical_and(valid, vld_ref[b, h, i] > 0), s, NEG_INF)
        scores.append(s)
        m = jnp.maximum(m, jnp.max(s, axis=-1, keepdims=True))
    p_new = jnp.exp(s_new - m)
    l = p_new
    acc = p_new * vn
    for i in range(nblk):
        p = jnp.exp(scores[i] - m)
        l = l + jnp.sum(p, axis=-1, keepdims=True)
        vv = v_refs[i][...].reshape(rows * kvh, HEAD_DIM).astype(BF)
        acc = acc + jnp.dot(p.astype(BF), vv, preferred_element_type=F32)
    o_ref[...] = acc / l


def _decode_attn(ys, cache, li_fixed, tbl, vld, *, q_ch, kn_ch, vn_ch, nh, rows, kvh, r_min, name):
    nb, _, nblk = tbl.shape
    page = cache.shape[2]
    bpp = page // rows
    gw = nh * HEAD_DIM
    body = functools.partial(_dec_body, nh=nh, rows=rows, kvh=kvh, nblk=nblk, r_min=r_min)

    def kv_spec(kv, i):
        def imap(b, h, tb, vl):
            t = tb[b, h, i]
            return (li_fixed, t // bpp, t % bpp, kv, 0, 0)

        return pl.BlockSpec((None, None, rows, None, kvh, HEAD_DIM), imap)

    nrow = ys.shape[0]
    return pl.pallas_call(
        body,
        grid_spec=pltpu.PrefetchScalarGridSpec(
            num_scalar_prefetch=2,
            grid=(nb, kvh),
            in_specs=[pl.BlockSpec((nrow, gw), lambda b, h, tb, vl: (0, q_ch // nh + h))]
            + [kv_spec(0, i) for i in range(nblk)] + [kv_spec(1, i) for i in range(nblk)]
            + [pl.BlockSpec((nrow, LANE), lambda b, h, tb, vl: (0, kn_ch + h)),
               pl.BlockSpec((nrow, LANE), lambda b, h, tb, vl: (0, vn_ch + h))],
            out_specs=pl.BlockSpec((None, None, SUBLANE, HEAD_DIM), lambda b, h, tb, vl: (b, h, 0, 0)),
        ),
        out_shape=jax.ShapeDtypeStruct((nb, kvh, SUBLANE, HEAD_DIM), F32),
        compiler_params=_cparams(("parallel", "parallel")),
        name=name,
    )(tbl, vld, ys, *([cache] * (2 * nblk)), ys, ys)


def _mix_body(oc_ref, os_ref, ow_ref, ob_ref, soc_ref, sos_ref, sow_ref, sob_ref, gate_ref, o_ref, *, n_p_tiles, n_s):
    i = pl.program_id(0)
    w = NSA_HEADS * HEAD_DIM

    def emit(oc, os_, ow, ob, rows):
        g = jax.nn.sigmoid(gate_ref[0:rows, :])
        for h in range(NSA_HEADS):
            sl = slice(h * HEAD_DIM, (h + 1) * HEAD_DIM)
            o = (g[:, 3 * h:3 * h + 1] * oc[:, sl] + g[:, 3 * h + 1:3 * h + 2] * os_[:, sl]
                 + g[:, 3 * h + 2:3 * h + 3] * ow[:, sl])
            o_ref[0:rows, sl] = o.astype(o_ref.dtype)
        o_ref[0:rows, w:] = ob[...].astype(o_ref.dtype)

    @pl.when(i < n_p_tiles)
    def _():
        emit(oc_ref, os_ref, ow_ref, ob_ref, o_ref.shape[0])

    @pl.when(i >= n_p_tiles)
    def _():
        emit(soc_ref, sos_ref, sow_ref, sob_ref, n_s)


def _mix(prompt_parts, sample_parts, y, tm=256):
    n_prompt = prompt_parts[0].shape[0]
    n_s = sample_parts[0].shape[0]
    m = y.shape[0]
    tm = min(tm, n_prompt)
    assert n_prompt % tm == 0 and m == n_prompt + n_s and n_s <= tm
    n_p_tiles = n_prompt // tm
    wa = NSA_HEADS * HEAD_DIM
    wb = MOBA_HEADS * HEAD_DIM
    pmap = lambda i: (jnp.minimum(i, n_p_tiles - 1), 0)
    body = functools.partial(_mix_body, n_p_tiles=n_p_tiles, n_s=n_s)
    return pl.pallas_call(
        body,
        grid=(n_p_tiles + 1,),
        in_specs=[pl.BlockSpec((tm, wa), pmap)] * 3 + [pl.BlockSpec((tm, wb), pmap)]
        + [pl.BlockSpec((n_s, wa), lambda i: (0, 0))] * 3 + [pl.BlockSpec((n_s, wb), lambda i: (0, 0))]
        + [pl.BlockSpec((tm, LANE), lambda i: (i, CH_GATE))],
        out_specs=pl.BlockSpec((tm, wa + wb), lambda i: (i, 0)),
        out_shape=jax.ShapeDtypeStruct((m, wa + wb), BF),
        compiler_params=_cparams(("parallel",)),
        name="attn_mix",
    )(*prompt_parts, *sample_parts, y)


def _convmix_body(up_ref, hp_ref, uc_ref, hc_ref, pw_ref, ps_ref, dw_ref, dwb_ref, lng_ref, lnb_ref, pww_ref,
                  o_ref, zp_ref, zs_ref, cb_ref, *, ts, nt, zero_first, avail0, rc):
    s = pl.program_id(0)
    hp_rows = hp_ref.shape[0]
    hc_rows = hc_ref.shape[0]
    width = up_ref.shape[1]
    if zero_first:
        keep = jnp.where(s % nt == 0, 0.0, 1.0)
        zp_ref[0:hp_rows, :] = hp_ref[...] * keep
        zs_ref[0, 0:hc_rows, :] = hc_ref[...] * keep
        avail = (s % nt) * ts + avail0
    else:
        zp_ref[0:hp_rows, :] = hp_ref[...]
        zs_ref[0, 0:hc_rows, :] = hc_ref[...]
        avail = avail0
    zp_ref[hp_rows:, :] = up_ref[...]
    zs_ref[0, hc_rows:, :] = uc_ref[...]
    span = hc_rows + ts - SUBLANE
    for k in range(1, SUBLANE):
        zs_ref[k, 0:span, :] = zs_ref[0, k:k + span, :]
    gwidth = width // len(POOL_WINDOWS)
    t_idx = lax.broadcasted_iota(jnp.int32, (ts, 1), 0) + avail + 1
    for gi, w in enumerate(POOL_WINDOWS):
        sl = slice(gi * gwidth, (gi + 1) * gwidth)
        cur = zp_ref[hp_rows:hp_rows + ts, sl]
        acc = cur
        for jj in range(1, w):
            acc = acc + zp_ref[hp_rows - jj:hp_rows - jj + ts, sl]
        cnt = jnp.minimum(t_idx, w).astype(F32)
        yg = acc / cnt - cur
        og = jnp.dot(yg.astype(BF), pw_ref[gi], preferred_element_type=F32) * ps_ref[:, sl]
        o_ref[:, sl] = og.astype(o_ref.dtype)
    base = hc_rows - (CONV_WIDTH - 1)
    for r in range(ts // rc):
        for c in range(width // LANE):
            cs = slice(c * LANE, (c + 1) * LANE)
            acc = jnp.zeros((rc, LANE), F32)
            for jj in range(CONV_WIDTH):
                k = (base + jj) % SUBLANE
                lo = r * rc + base + jj - k
                acc = acc + zs_ref[k, lo:lo + rc, cs] * dw_ref[jj:jj + 1, cs]
            cb_ref[r * rc:(r + 1) * rc, cs] = acc + dwb_ref[:, cs]
    cv = cb_ref[...]
    mu = jnp.mean(cv, axis=-1, keepdims=True)
    xc = cv - mu
    yn = xc * lax.rsqrt(jnp.mean(xc * xc, axis=-1, keepdims=True) + NORM_EPS) * lng_ref[...] + lnb_ref[...]
    act = yn * jax.nn.sigmoid(yn)
    o_ref[:, width:] = jnp.dot(act.astype(BF), pww_ref[...], preferred_element_type=F32).astype(o_ref.dtype)


def _convmix(u_pool, halo_pool, u_conv, halo_conv, prm, *, ts, nt, n_tiles, zero_first, avail0, hp_rows, hc_rows,
             halo_blk):
    pw, ps, dw, dwb, lng, lnb, pww = prm
    width = u_pool.shape[1]
    rc = min(64, ts)
    body = functools.partial(_convmix_body, ts=ts, nt=nt, zero_first=zero_first, avail0=avail0, rc=rc)
    full = lambda a: pl.BlockSpec(a.shape, lambda s: (0,) * a.ndim)
    return pl.pallas_call(
        body,
        grid=(n_tiles,),
        in_specs=[pl.BlockSpec((ts, width), lambda s: (s, 0)),
                  pl.BlockSpec((hp_rows, width), lambda s: (halo_blk(s, hp_rows), 0)),
                  pl.BlockSpec((ts, width), lambda s: (s, 0)),
                  pl.BlockSpec((hc_rows, width), lambda s: (halo_blk(s, hc_rows), 0)),
                  full(pw), full(ps), full(dw), full(dwb), full(lng), full(lnb), full(pww)],
        out_specs=pl.BlockSpec((ts, 2 * width), lambda s: (s, 0)),
        out_shape=jax.ShapeDtypeStruct((n_tiles * ts, 2 * width), BF),
        scratch_shapes=[pltpu.VMEM((hp_rows + ts, width), F32), pltpu.VMEM((SUBLANE, hc_rows + ts, width), F32),
                        pltpu.VMEM((ts, width), F32)],
        compiler_params=_cparams(("parallel",)),
        name="conv_mix",
    )(u_pool, halo_pool, u_conv, halo_conv, pw, ps, dw, dwb, lng, lnb, pww)


def _router_body(x_ref, g_ref, rw_ref, r_ref, xn_ref):
    x = x_ref[...]
    xn = x * lax.rsqrt(jnp.mean(x * x, axis=-1, keepdims=True) + NORM_EPS) * g_ref[...]
    xn_ref[...] = xn.astype(xn_ref.dtype)
    logits = _dot3(xn, rw_ref[...])
    lane = lax.broadcasted_iota(jnp.int32, logits.shape, 1)
    l1 = jnp.where(lane < N_EXPERTS, logits, NEG_INF)
    m1 = jnp.max(l1, axis=-1, keepdims=True)
    i1 = jnp.min(jnp.where(l1 == m1, lane, LANE), axis=-1, keepdims=True)
    l2 = jnp.where(lane == i1, NEG_INF, l1)
    m2 = jnp.max(l2, axis=-1, keepdims=True)
    i2 = jnp.min(jnp.where(l2 == m2, lane, LANE), axis=-1, keepdims=True)
    e = jnp.exp(m2 - m1)
    w1 = 1.0 / (1.0 + e)
    w2 = e / (1.0 + e)
    r_ref[...] = jnp.where(lane == 0, i1.astype(F32),
                           jnp.where(lane == 1, i2.astype(F32), jnp.where(lane == 2, w1, jnp.where(lane == 3, w2, 0.0))))


def _router(x, g, rw, tm=256):
    m, d = x.shape
    rw_pad = jnp.pad(rw, ((0, 0), (0, LANE - rw.shape[1])))
    return pl.pallas_call(
        _router_body,
        grid=(pl.cdiv(m, tm),),
        in_specs=[pl.BlockSpec((tm, d), lambda i: (i, 0)), pl.BlockSpec((1, d), lambda i: (0, 0)),
                  pl.BlockSpec((d, LANE), lambda i: (0, 0))],
        out_specs=[pl.BlockSpec((tm, LANE), lambda i: (i, 0)), pl.BlockSpec((tm, d), lambda i: (i, 0))],
        out_shape=[jax.ShapeDtypeStruct((m, LANE), F32), jax.ShapeDtypeStruct((m, d), BF)],
        compiler_params=_cparams(("parallel",)),
        name="moe_router",
    )(x, g.reshape(1, d), rw_pad)


def _row_copy(src_hbm, row, dst, drow, sem):
    return pltpu.make_async_copy(src_hbm.at[pl.ds(row, 1), :], dst.at[pl.ds(drow, 1), :], sem)


def _gather_rows_body(c0_ref, nc_ref, src_ref, xn_hbm, o_ref, buf_ref, acc_ref, first_ref, sem, *, tc):
    i = pl.program_id(0)
    c0 = c0_ref[i]
    n = nc_ref[i]
    i_next = jnp.minimum(i + 1, pl.num_programs(0) - 1)
    n_next = jnp.where(i + 1 < pl.num_programs(0), nc_ref[i_next], 0)

    def chunk_copy(c, slot):
        return pltpu.make_async_copy(xn_hbm.at[pl.ds(c * tc, tc), :], buf_ref.at[slot], sem.at[slot])

    acc_ref[...] = jnp.zeros(acc_ref.shape, F32)

    @pl.when(i == 0)
    def _():
        first_ref[0] = 0

        @pl.when(n > 0)
        def _():
            chunk_copy(c0, 0).start()

    first = first_ref[0]
    src = src_ref[...]

    def body(k, carry):
        slot = (first + k) % 2
        chunk_copy(c0 + k, slot).wait()

        @pl.when(k + 1 < n)
        def _():
            chunk_copy(c0 + k + 1, 1 - slot).start()

        @pl.when(jnp.logical_and(k + 1 == n, n_next > 0))
        def _():
            chunk_copy(c0_ref[i_next], 1 - slot).start()
            first_ref[0] = 1 - slot

        tok = (c0 + k) * tc + lax.broadcasted_iota(jnp.int32, (1, tc), 1)
        onehot = jnp.where(src == tok, 1.0, 0.0).astype(buf_ref.dtype)
        acc_ref[...] += jnp.dot(onehot, buf_ref[slot], preferred_element_type=F32)
        return carry

    lax.fori_loop(0, n, body, 0)
    o_ref[...] = acc_ref[...].astype(o_ref.dtype)


def _gather_rows(xn, src, c0, nc, tm, tc):
    n_tok, d = xn.shape
    r_tot = src.shape[0]
    body = functools.partial(_gather_rows_body, tc=tc)
    return pl.pallas_call(
        body,
        grid_spec=pltpu.PrefetchScalarGridSpec(
            num_scalar_prefetch=2,
            grid=(r_tot // tm,),
            in_specs=[pl.BlockSpec((tm, 1), lambda i, a, b: (i, 0)), pl.BlockSpec(memory_space=pl.ANY)],
            out_specs=pl.BlockSpec((tm, d), lambda i, a, b: (i, 0)),
            scratch_shapes=[pltpu.VMEM((2, tc, d), xn.dtype), pltpu.VMEM((tm, d), F32), pltpu.SMEM((1,), jnp.int32),
                            pltpu.SemaphoreType.DMA((2,))],
        ),
        out_shape=jax.ShapeDtypeStruct((r_tot, d), xn.dtype),
        compiler_params=_cparams(("arbitrary",)),
        name="moe_gather",
    )(c0, nc, src.reshape(r_tot, 1), xn)


def _combine_body(p1_ref, p2_ref, eo_hbm, x_ref, r_ref, o_ref, ot_ref, b1_ref, b2_ref, sem, *, tm, n_main, n_tail):
    i = pl.program_id(0)
    base = i * tm
    n = jnp.where(i < n_main // tm, tm, n_tail)

    def issue(rq, c):
        for u in range(4):
            r = rq * 4 + u
            _row_copy(eo_hbm, p1_ref[base + r], b1_ref, r, sem).start()
            _row_copy(eo_hbm, p2_ref[base + r], b2_ref, r, sem).start()
        return c

    lax.fori_loop(0, n // 4, issue, 0)

    def wait(r, c):
        _row_copy(eo_hbm, 0, b1_ref, r, sem).wait()
        _row_copy(eo_hbm, 0, b2_ref, r, sem).wait()
        return c

    lax.fori_loop(0, n, wait, 0)

    def rows(r):
        rr = r_ref[0:r, :]
        return x_ref[0:r, :] + rr[:, 2:3] * b1_ref[0:r, :] + rr[:, 3:4] * b2_ref[0:r, :]

    @pl.when(i < n_main // tm)
    def _():
        o_ref[...] = rows(tm)

    @pl.when(i == n_main // tm)
    def _():
        ot_ref[...] = rows(n_tail)


def _combine(x, eo, routing, p1, p2, n_main, tm=256):
    m, d = x.shape
    n_tail = m - n_main
    assert n_main % tm == 0 and 0 < n_tail <= tm and n_tail % SUBLANE == 0
    n_full = n_main // tm
    body = functools.partial(_combine_body, tm=tm, n_main=n_main, n_tail=n_tail)
    return pl.pallas_call(
        body,
        grid_spec=pltpu.PrefetchScalarGridSpec(
            num_scalar_prefetch=2,
            grid=(n_full + 1,),
            in_specs=[pl.BlockSpec(memory_space=pl.ANY), pl.BlockSpec((tm, d), lambda i, a, b: (i, 0)),
                      pl.BlockSpec((tm, LANE), lambda i, a, b: (i, 0))],
            out_specs=[pl.BlockSpec((tm, d), lambda i, a, b: (jnp.minimum(i, n_full - 1), 0)),
                       pl.BlockSpec((n_tail, d), lambda i, a, b: (0, 0))],
            scratch_shapes=[pltpu.VMEM((tm, d), F32), pltpu.VMEM((tm, d), F32), pltpu.SemaphoreType.DMA(())],
        ),
        out_shape=[jax.ShapeDtypeStruct((n_main, d), F32), jax.ShapeDtypeStruct((n_tail, d), F32)],
        compiler_params=_cparams(("arbitrary",)),
        name="moe_combine",
    )(p1, p2, eo, x, routing)


def _moe_plan(e_idx, tm, tc):
    n = e_idx.shape[0]
    n_asg = n * TOP_K
    flat_e = e_idx.reshape(-1)
    onehot = (flat_e[:, None] == jnp.arange(N_EXPERTS, dtype=jnp.int32)[None, :]).astype(jnp.int32)
    cnt = jnp.sum(onehot, axis=0)
    rank = jnp.take_along_axis(jnp.cumsum(onehot, axis=0) - onehot, flat_e[:, None], axis=1)[:, 0]
    cnt_p = ((cnt + tm - 1) // tm) * tm
    ends = jnp.cumsum(cnt_p)
    off = ends - cnt_p
    pos = off[flat_e] + rank
    n_tiles = (n_asg + N_EXPERTS * (tm - 1) + tm - 1) // tm
    r_tot = n_tiles * tm
    src = jnp.full((r_tot,), -1, jnp.int32).at[pos].set(jnp.arange(n_asg, dtype=jnp.int32) // TOP_K)
    src_t = src.reshape(n_tiles, tm)
    lo = jnp.min(jnp.where(src_t >= 0, src_t, n), axis=1)
    hi = jnp.max(src_t, axis=1)
    c0 = jnp.where(hi >= 0, lo // tc, 0).astype(jnp.int32)
    nc = jnp.where(hi >= 0, hi // tc + 1 - lo // tc, 0).astype(jnp.int32)
    tile_start = jnp.arange(n_tiles, dtype=jnp.int32) * tm
    tval = (tile_start < ends[-1]).astype(jnp.int32)
    texp = jnp.minimum(jnp.sum((tile_start[:, None] >= ends[None, :]).astype(jnp.int32), axis=1), N_EXPERTS - 1)
    last = jnp.max(jnp.where(tval > 0, texp, 0))
    texp = jnp.where(tval > 0, texp, last)
    pos2 = pos.reshape(n, TOP_K)
    return src, c0, nc, pos2[:, 0], pos2[:, 1], texp, tval


def _moe(x, g, rw, wg, wu, wd, n_main, tm=256):
    n = x.shape[0]
    tc = max(t for t in range(2 * SUBLANE, 513, 2 * SUBLANE) if n % t == 0)
    routing, xn = _router(x, g, rw)
    e_idx = routing[:, :TOP_K].astype(jnp.int32)
    src, c0, nc, p1, p2, texp, tval = _moe_plan(e_idx, tm, tc)
    xs = _gather_rows(xn, src, c0, nc, tm, tc)
    hid = _gmm(xs, [(wg, 0), (wu, 0)], wg.shape[2], epi="swiglu", texp=texp, tval=tval, out_dtype=BF, tm=tm, tn=1024,
               name="moe_up")
    eo = _gmm(hid, [(wd, 0)], wd.shape[2], texp=texp, tval=tval, tm=tm, name="moe_down")
    return _combine(x, eo, routing, p1, p2, n_main)


def _seq_tails(a, nb, t, k):
    return jnp.stack([a[(b + 1) * t - k:(b + 1) * t] for b in range(nb)])


def _attn_layer(x, n_prompt, nb_p, t_p, nb_s, li, prm, caches, page_table):
    (attn_norm, w_in, qk_g, cmp_pe, cmp_w1, cmp_w2, moba_g, w_out, ffn_norm, w_gate, w_up, w_down) = prm
    cache_cmp, cache_sel, state_win, cache_moba = caches
    d = x[0].shape[1]
    past_len = page_table.shape[1] * cache_moba.shape[2]

    g_lo = NSA_CHUNKS * LANE
    g_hi = g_lo + 3 * NSA_HEADS
    assert w_in.shape[1] == g_hi + MOBA_CHUNKS * LANE
    w_t = w_in.T
    w_moba = w_t[g_hi:][None]
    w_gate_cols = jnp.pad(w_t[g_lo:g_hi], ((0, LANE - 3 * NSA_HEADS), (0, 0)))[None]
    ones = jnp.ones((HEAD_DIM,), F32)
    zeros = jnp.zeros((HEAD_DIM,), F32)
    gain_a = jnp.concatenate([qk_g[0]] * 8 + [ones] * 4 + [qk_g[2]] * 2 + [ones] * 2 + [qk_g[3]] * 2 + [ones] * 2)[None]
    flag_a = jnp.concatenate([ones] * 8 + [zeros] * 4 + [ones] * 2 + [zeros] * 2 + [ones] * 2 + [zeros] * 2)[None]
    gain_b = jnp.concatenate([moba_g[0]] * 8 + [moba_g[1]] * 8 + [ones] * 8)[None]
    flag_b = jnp.concatenate([ones] * 16 + [zeros] * 8)[None]
    xn = _rmsnorm(x, attn_norm, tm=STREAM_TM)
    ya, ya16 = _gmm(xn, [(w_t[None], 0)], NSA_CHUNKS * LANE, epi="headnorm", gain=gain_a, flag=flag_a,
                    out_dtype=(F32, BF), tm=STREAM_TM, tn=10 * LANE, w_t=True, name="attn_in_nsa")
    yb, yb16 = _gmm(xn, [(w_moba, 0)], MOBA_CHUNKS * LANE, epi="headnorm", gain=gain_b, flag=flag_b,
                    out_dtype=(F32, BF), tm=STREAM_TM, tn=8 * LANE, w_t=True, name="attn_in_moba")
    yg = _gmm(xn, [(w_gate_cols, 0)], LANE, tm=STREAM_TM, tn=LANE, w_t=True, name="attn_in_gate")

    r = NSA_CMP_LEN // NSA_CMP_STRIDE
    wcat = cmp_w1.reshape(2, r, NSA_CMP_STRIDE, HEAD_DIM, NSA_CMP_HIDDEN).transpose(0, 2, 3, 1, 4)
    wcat = wcat.reshape(2, NSA_CMP_STRIDE, HEAD_DIM, r * NSA_CMP_HIDDEN).astype(BF)
    pe_rows = cmp_pe.reshape(2, r, NSA_CMP_STRIDE, HEAD_DIM).transpose(0, 2, 1, 3)
    pe_rows = jnp.pad(pe_rows, ((0, 0), (0, 0), (0, SUBLANE - r), (0, 0))).astype(BF)
    w2 = cmp_w2.astype(BF)
    gk = qk_g[1][None]

    kvc_p = _compress_c2(_compress_c1_prompt(ya, wcat, nb_p, t_p), pe_rows, wcat, w2, gk)
    rows = n_prompt
    oc, sel_a = _cattn(ya, 0, 0, kvc_p, nb_p, t_p, min(512, t_p), t_p, 0, rows)
    sel_b = _moba_gate_prompt(yb, nb_p, t_p, min(1024, t_p))
    os_ = _flash(ya, ya16, sel_a, nb_p, t_p, q_ch=CH_QA, k_ch=CH_KS, v_ch=CH_VS, kvh=NSA_KV, nh=NSA_HPG,
                 blk=NSA_SEL_BLOCK, window=0, out_rows=rows)
    ow = _flash(ya, ya16, None, nb_p, t_p, q_ch=CH_QA, k_ch=CH_KW, v_ch=CH_VW, kvh=NSA_KV, nh=NSA_HPG, blk=0,
                window=NSA_WINDOW, out_rows=rows)
    ob = _flash(yb, yb16, sel_b, nb_p, t_p, q_ch=CH_QB, k_ch=CH_KB, v_ch=CH_VB, kvh=MOBA_HEADS, nh=1,
                blk=MOBA_BLOCK, window=0, out_rows=rows)

    n_pad = x[2] - n_prompt
    ys = ya[n_prompt:]
    ysb = yb[n_prompt:]
    kvc_s = _compress_c2(_compress_c1_sample(cache_cmp, li, page_table, wcat), pe_rows, wcat, w2, gk)
    q_pad = jnp.zeros((nb_s, SUBLANE, NSA_HEADS * HEAD_DIM), F32).at[:, 0].set(ys[:nb_s, :NSA_HEADS * HEAD_DIM])
    oc_s, sel_s = _cattn_sample(q_pad.reshape(nb_s * SUBLANE, -1), kvc_s, nb_s, past_len + 1, past_len)
    oc_s = oc_s.reshape(nb_s, SUBLANE, -1)[:, 0]
    n_sel = (past_len + 1) // NSA_SEL_BLOCK
    k_sel = min(NSA_SEL_COUNT - 1, n_sel)
    mask_a = sel_s[:, :, 0, :n_sel]
    idx_a = jnp.argsort(-mask_a, axis=-1, stable=True)[..., :k_sel].astype(jnp.int32)
    vld_a = (jnp.take_along_axis(mask_a, idx_a, axis=-1) > 0).astype(jnp.int32)
    page = cache_sel.shape[2]
    bpp = page // NSA_SEL_BLOCK
    pt_b = page_table[:, None, :]
    tbl_a = jnp.take_along_axis(jnp.broadcast_to(pt_b, (nb_s, NSA_KV, pt_b.shape[-1])), idx_a // bpp, axis=-1) * bpp + idx_a % bpp
    os_s = _decode_attn(ys, cache_sel, li, tbl_a.astype(jnp.int32), vld_a, q_ch=CH_QA, kn_ch=CH_KS, vn_ch=CH_VS,
                        nh=NSA_HPG, rows=NSA_SEL_BLOCK, kvh=NSA_KV, r_min=0,
                        name="dec_sel")
    win_buf = state_win.shape[2]
    tbl_w = jnp.broadcast_to(jnp.arange(nb_s, dtype=jnp.int32)[:, None, None], (nb_s, NSA_KV, 1))
    ow_s = _decode_attn(ys, state_win, li, tbl_w, jnp.ones_like(tbl_w), q_ch=CH_QA, kn_ch=CH_KW, vn_ch=CH_VW,
                        nh=NSA_HPG, rows=win_buf, kvh=NSA_KV, r_min=max(0, win_buf - (NSA_WINDOW - 1)),
                        name="dec_win")
    means = _moba_means_sample(cache_moba, li, page_table)
    qb_pad = jnp.zeros((nb_s, SUBLANE, MOBA_HEADS * HEAD_DIM), F32).at[:, 0].set(
        ysb[:nb_s, CH_QB * LANE:(CH_QB + MOBA_HEADS) * LANE])
    sel_m = _moba_gate_sample(qb_pad.reshape(nb_s * SUBLANE, -1), means, past_len)
    n_full = means.shape[2]
    k_top = min(MOBA_TOPK, n_full)
    mask_b = sel_m[:, :, 0, :]
    idx_b = jnp.argsort(-mask_b, axis=-1, stable=True)[..., :k_top].astype(jnp.int32)
    vld_b = (jnp.take_along_axis(mask_b, idx_b, axis=-1) > 0).astype(jnp.int32)
    ppb = MOBA_BLOCK // page
    pg_b = (idx_b[..., None] * ppb + jnp.arange(ppb, dtype=jnp.int32)).reshape(nb_s, MOBA_HEADS, k_top * ppb)
    tbl_b = jnp.take_along_axis(jnp.broadcast_to(pt_b, (nb_s, MOBA_HEADS, pt_b.shape[-1])), pg_b, axis=-1)
    vld_b = jnp.repeat(vld_b, ppb, axis=-1)
    ob_s = _decode_attn(ysb, cache_moba, li, tbl_b.astype(jnp.int32), vld_b, q_ch=CH_QB, kn_ch=CH_KB, vn_ch=CH_VB,
                        nh=1, rows=page, kvh=MOBA_HEADS, r_min=0,
                        name="dec_moba")

    def rows_s(part, nh):
        return jnp.pad(part[:, :, :nh].reshape(nb_s, -1), ((0, n_pad - nb_s), (0, 0)))

    sample_parts = (jnp.pad(oc_s, ((0, n_pad - nb_s), (0, 0))), rows_s(os_s, NSA_HPG), rows_s(ow_s, NSA_HPG),
                    rows_s(ob_s, 1))
    mixed = _mix((oc, os_, ow, ob), sample_parts, yg)
    x = _gmm(mixed, [(w_out[None], 0)], d, resid=x, tm=STREAM_TM, tn=1024, name="attn_out")
    hid = _gmm(_rmsnorm(x, ffn_norm), [(w_gate[None], 0), (w_up[None], 0)], w_gate.shape[1], epi="swiglu",
               out_dtype=BF, tm=STREAM_TM, tn=512, name="ffn_up")
    x = _gmm(hid, [(w_down[None], 0)], d, resid=x, tm=STREAM_TM, tn=512, name="ffn_down")

    def rows_of(lo_ch, n_ch, y=ya):
        return y[:, lo_ch * LANE:(lo_ch + n_ch) * LANE]

    def split(a, kvh):
        ap = a[:n_prompt].reshape(nb_p, t_p, 2, kvh, HEAD_DIM)
        as_ = a[n_prompt:n_prompt + nb_s].reshape(nb_s, 1, 2, kvh, HEAD_DIM)
        return ap, as_

    cmp_p, cmp_s = split(rows_of(CH_KC, 4), NSA_KV)
    sel_p, sel_s_rows = split(rows_of(CH_KS, 4), NSA_KV)
    win_rows = rows_of(CH_KW, 4)
    win_s = win_rows[n_prompt:n_prompt + nb_s].reshape(nb_s, 1, 2, NSA_KV, HEAD_DIM)
    moba_p, moba_s = split(rows_of(CH_KB, 16, yb), MOBA_HEADS)
    keep_p = min(NSA_WINDOW, t_p)
    new_win_p = _seq_tails(win_rows, nb_p, t_p, keep_p).reshape(nb_p, keep_p, 2, NSA_KV, HEAD_DIM)
    win_all = jnp.concatenate([state_win[li], win_s], axis=1)
    keep = min(NSA_WINDOW, past_len + 1)
    new_win_s = win_all[:, win_all.shape[1] - keep:]
    return x, (cmp_p, cmp_s, sel_p, sel_s_rows, new_win_p, new_win_s, moba_p, moba_s)


def _conv_layer(x, n_prompt, nb_p, t_p, nb_s, prm, states):
    (conv_norm, w_in, pool_w, pool_scale, dw, dw_b, ln_g, ln_b, pw, w_out, moe_norm, router_w, wg, wu, wd) = prm
    state_pool, state_conv = states
    d = x.shape[1]
    width = pool_w.shape[0] * pool_w.shape[1]
    xn = _rmsnorm(x, conv_norm)
    w3 = w_in[None]
    u_pool = _gmm(xn, [(w3, 0)], width, tm=512, tn=width, name="conv_in_pool")
    u_conv = _gmm(xn, [(w3, width), (w3, 2 * width)], width, epi="glu", tm=512, tn=512, name="conv_in_glu")
    prm_mix = (pool_w.astype(BF), pool_scale[None], jnp.pad(dw, ((0, 1), (0, 0))), dw_b[None], ln_g[None], ln_b[None],
               pw.astype(BF))
    pool_buf = max(POOL_WINDOWS) - 1
    conv_buf = CONV_WIDTH - 1
    hp_rows, hc_rows = 16, 32
    ts = min(256, t_p)
    nt = t_p // ts
    mixed_p = _convmix(u_pool, u_pool, u_conv, u_conv, prm_mix, ts=ts, nt=nt, n_tiles=nb_p * nt, zero_first=True,
                       avail0=0, hp_rows=hp_rows, hc_rows=hc_rows,
                       halo_blk=lambda s, hr: jnp.maximum(s * (ts // hr) - 1, 0))
    rows = x.shape[0]
    us_pool = jnp.zeros((nb_s, SUBLANE, width), F32).at[:, 0].set(u_pool[n_prompt:n_prompt + nb_s])
    us_conv = jnp.zeros((nb_s, SUBLANE, width), F32).at[:, 0].set(u_conv[n_prompt:n_prompt + nb_s])
    hs_pool = jnp.pad(state_pool, ((0, 0), (hp_rows - pool_buf, 0), (0, 0))).reshape(nb_s * hp_rows, width)
    hs_conv = jnp.pad(state_conv, ((0, 0), (hc_rows - conv_buf, 0), (0, 0))).reshape(nb_s * hc_rows, width)
    mixed_s = _convmix(us_pool.reshape(nb_s * SUBLANE, width), hs_pool, us_conv.reshape(nb_s * SUBLANE, width), hs_conv,
                       prm_mix, ts=SUBLANE, nt=1, n_tiles=nb_s, zero_first=False, avail0=pool_buf, hp_rows=hp_rows,
                       hc_rows=hc_rows, halo_blk=lambda s, hr: s)
    mixed_s = mixed_s.reshape(nb_s, SUBLANE, 2 * width)[:, 0]
    mixed = (mixed_p, jnp.pad(mixed_s, ((0, STREAM_TM - nb_s), (0, 0))), rows)
    x = _gmm(mixed, [(w_out[None], 0)], d, resid=x, tm=STREAM_TM, tn=1024, name="conv_out")
    x = _moe(x, moe_norm, router_w, wg, wu, wd, n_prompt)

    assert t_p >= conv_buf and t_p >= pool_buf
    new_pool_p = _seq_tails(u_pool, nb_p, t_p, pool_buf)
    new_conv_p = _seq_tails(u_conv, nb_p, t_p, conv_buf)
    new_pool_s = jnp.concatenate([state_pool, u_pool[n_prompt:n_prompt + nb_s][:, None]], axis=1)[:, 1:]
    new_conv_s = jnp.concatenate([state_conv, u_conv[n_prompt:n_prompt + nb_s][:, None]], axis=1)[:, 1:]
    return x, (new_pool_p, new_pool_s, new_conv_p, new_conv_s)


def kernel(x_prompt, x_sample, cache_nsa_cmp, cache_nsa_sel, state_nsa_win, cache_moba, state_pool, state_conv, page_table, attn_norm, w_attn_in, nsa_qk_norm, nsa_cmp_pe, nsa_cmp_w1, nsa_cmp_w2, moba_qk_norm, w_attn_out, ffn_norm, ffn_w_gate, ffn_w_up, ffn_w_down, conv_norm, w_conv_in, pool_w, pool_scale, conv_dw, conv_dw_b, conv_ln_g, conv_ln_b, conv_pw, w_conv_out, moe_norm, router_w, moe_w_gate, moe_w_up, moe_w_down):
    nb_p, t_p, d = x_prompt.shape
    nb_s = x_sample.shape[0]
    n_prompt = nb_p * t_p
    n_pad = 2 * SUBLANE
    assert x_sample.shape[1] == 1 and nb_s <= n_pad
    x = (x_prompt.reshape(n_prompt, d), jnp.pad(x_sample.reshape(nb_s, d), ((0, STREAM_TM - nb_s), (0, 0))),
         n_prompt + n_pad)
    li = 0
    prm_a = (attn_norm[li], w_attn_in[li], nsa_qk_norm[li], nsa_cmp_pe[li], nsa_cmp_w1[li], nsa_cmp_w2[li],
             moba_qk_norm[li], w_attn_out[li], ffn_norm[li], ffn_w_gate[li], ffn_w_up[li], ffn_w_down[li])
    x, attn_new = _attn_layer(x, n_prompt, nb_p, t_p, nb_s, li, prm_a,
                              (cache_nsa_cmp, cache_nsa_sel, state_nsa_win, cache_moba), page_table)
    prm_c = (conv_norm[li], w_conv_in[li], pool_w[li], pool_scale[li], conv_dw[li], conv_dw_b[li], conv_ln_g[li],
             conv_ln_b[li], conv_pw[li], w_conv_out[li], moe_norm[li], router_w[li], moe_w_gate[li], moe_w_up[li],
             moe_w_down[li])
    x, conv_new = _conv_layer(x, n_prompt, nb_p, t_p, nb_s, prm_c, (state_pool[li], state_conv[li]))
    cmp_p, cmp_s, sel_p, sel_s, win_p, win_s, moba_p, moba_s = attn_new
    pool_p, pool_s, conv_p, conv_s = conv_new
    y_p = x[0].reshape(nb_p, t_p, d)
    y_s = x[1][:nb_s].reshape(nb_s, 1, d)
    st = lambda a: a[None]
    return (y_p, y_s, st(cmp_p), st(cmp_s), st(sel_p), st(sel_s), st(win_p), st(win_s), st(moba_p), st(moba_s),
            st(pool_p), st(pool_s), st(conv_p), st(conv_s))
```

```python
import functools
import math

import jax
import jax.numpy as jnp
from jax import lax
from jax.experimental import pallas as pl
from jax.experimental.pallas import tpu as pltpu

F32 = jnp.float32
BF = jnp.bfloat16
NEG_INF = float("-inf")

HEAD_DIM = 128
NORM_EPS = 1e-6
NSA_HEADS = 8
NSA_KV = 2
NSA_HPG = NSA_HEADS // NSA_KV
NSA_CMP_LEN = 32
NSA_CMP_STRIDE = 16
NSA_CMP_HIDDEN = 2 * HEAD_DIM
NSA_SEL_BLOCK = 64
NSA_SEL_COUNT = 16
NSA_WINDOW = 512
NSA_SEL_FORCE = 1.0e4
MOBA_HEADS = 8
MOBA_BLOCK = 256
MOBA_TOPK = 3
POOL_WINDOWS = (2, 4, 8, 16)
CONV_WIDTH = 31
N_EXPERTS = 8
TOP_K = 2

LANE = 128
SUBLANE = 8
VMEM_LIMIT = 58 * 1024 * 1024
STREAM_TM = 512

CH_QA, CH_KC, CH_VC, CH_KS, CH_VS, CH_KW, CH_VW = 0, 8, 10, 12, 14, 16, 18
NSA_CHUNKS = 20
CH_QB, CH_KB, CH_VB = 0, 8, 16
MOBA_CHUNKS = 24
CH_GATE = 0

NT_DIMS = (((1,), (1,)), ((), ()))


def _cparams(sem):
    return pltpu.CompilerParams(dimension_semantics=sem, vmem_limit_bytes=VMEM_LIMIT)


def _split_bf16(a):
    hi = a.astype(BF)
    lo = (a - hi.astype(F32)).astype(BF)
    return hi, lo


def _dot3(a, b, dims=(((1,), (0,)), ((), ()))):
    ah, al = _split_bf16(a)
    bh, bl = _split_bf16(b)
    d = lambda x, y: lax.dot_general(x, y, dims, preferred_element_type=F32)
    return d(ah, bh) + d(ah, bl) + d(al, bh)


def _masked_softmax(s, mask, axis):
    s = jnp.where(mask, s, NEG_INF)
    m = jnp.max(s, axis=axis, keepdims=True)
    m = jnp.where(m == NEG_INF, 0.0, m)
    p = jnp.exp(s - m)
    d = jnp.sum(p, axis=axis, keepdims=True)
    return p / jnp.where(d > 0, d, 1.0)


def _split_stream(a, tm):
    if not isinstance(a, tuple):
        return [a], [lambda i: i], None, a.shape[0]
    main, tail, m = a
    n_full = main.shape[0] // tm
    assert n_full * tm == main.shape[0] and tail.shape[0] == tm and n_full * tm < m <= (n_full + 1) * tm
    return [main, tail], [lambda i: jnp.minimum(i, n_full - 1), lambda i: 0], n_full, m


def _pick_tile(refs, i, n_full):
    return refs[0][...] if n_full is None else jnp.where(i < n_full, refs[0][...], refs[1][...])


def _rmsnorm_body(*refs, n_full):
    x = _pick_tile(refs[:-2], pl.program_id(0), n_full)
    g_ref, o_ref = refs[-2:]
    ms = jnp.mean(x * x, axis=-1, keepdims=True)
    o_ref[...] = (x * lax.rsqrt(ms + NORM_EPS) * g_ref[...]).astype(o_ref.dtype)


def _rmsnorm(x, g, tm=256):
    arrs, maps, n_full, m = _split_stream(x, tm)
    d = arrs[0].shape[1]
    return pl.pallas_call(
        functools.partial(_rmsnorm_body, n_full=n_full),
        grid=(pl.cdiv(m, tm),),
        in_specs=[pl.BlockSpec((tm, d), lambda i, f=f: (f(i), 0)) for f in maps] + [pl.BlockSpec((1, d), lambda i: (0, 0))],
        out_specs=pl.BlockSpec((tm, d), lambda i: (i, 0)),
        out_shape=jax.ShapeDtypeStruct((m, d), BF),
        compiler_params=_cparams(("parallel",)),
        name="rmsnorm",
    )(*arrs, g.reshape(1, d))


def _gmm_body(texp_ref, tval_ref, *refs, n_x, x_full, n_w, cast_w, w_t, epi, n_r, r_full, tn, n_o):
    mm = (lambda x, w: lax.dot_general(x, w, NT_DIMS, preferred_element_type=F32)) if w_t else (
        lambda x, w: jnp.dot(x, w, preferred_element_type=F32))
    x_refs = refs[:n_x]
    w_refs = refs[n_x:n_x + n_w]
    pos = n_x + n_w
    if epi == "headnorm":
        gain_ref, flag_ref = refs[pos], refs[pos + 1]
        pos += 2
    resid_refs = refs[pos:pos + n_r]
    pos += n_r
    o_refs = refs[pos:pos + n_o]
    wb_refs = refs[pos + n_o:pos + n_o + n_w] if cast_w else w_refs

    i = pl.program_id(1)
    if cast_w:
        changed = jnp.logical_or(i == 0, texp_ref[i] != texp_ref[jnp.maximum(i - 1, 0)])

        @pl.when(changed)
        def _():
            for k in range(n_w):
                wb_refs[k][...] = w_refs[k][...].astype(BF)

    @pl.when(tval_ref[i] == 0)
    def _():
        for o in o_refs:
            o[...] = jnp.zeros(o.shape, o.dtype)

    @pl.when(tval_ref[i] > 0)
    def _():
        x = _pick_tile(x_refs, i, x_full)
        a = mm(x, wb_refs[0][...])
        if epi == "swiglu":
            b = mm(x, wb_refs[1][...])
            y = a * jax.nn.sigmoid(a) * b
        elif epi == "glu":
            b = mm(x, wb_refs[1][...])
            y = a * jax.nn.sigmoid(b)
        else:
            y = a
        if n_r:
            y = y + _pick_tile(resid_refs, i, r_full)
        if epi == "headnorm":
            for c in range(tn // LANE):
                sl = slice(c * LANE, (c + 1) * LANE)
                yc = y[:, sl]
                r = lax.rsqrt(jnp.mean(yc * yc, axis=-1, keepdims=True) + NORM_EPS)
                f = flag_ref[:, sl]
                yn = yc * (f * r + (1.0 - f)) * gain_ref[:, sl]
                for o in o_refs:
                    o[:, sl] = yn.astype(o.dtype)
        else:
            for o in o_refs:
                o[...] = y.astype(o.dtype)


def _gmm(x, ws, n_out, *, epi="none", texp=None, tval=None, resid=None, gain=None, flag=None,
         out_dtype=F32, tm=256, tn=512, w_t=False, name="gmm"):
    x_arrs, x_maps, x_full, m = _split_stream(x, tm)
    kdim = x_arrs[0].shape[1]
    tn = min(tn, n_out)
    n_m = pl.cdiv(m, tm)
    n_n = n_out // tn
    assert n_n * tn == n_out
    if texp is None:
        texp = jnp.zeros((n_m,), jnp.int32)
        tval = jnp.ones((n_m,), jnp.int32)
    n_w = len(ws)
    cast_w = ws[0][0].dtype != BF
    in_specs = [pl.BlockSpec((tm, kdim), lambda j, i, te, tv, f=f: (f(i), 0)) for f in x_maps]
    args = list(x_arrs)
    for w, off in ws:
        assert off % tn == 0 and w.shape[2 if w_t else 1] == kdim and (w.dtype != BF) == cast_w
        ob = off // tn
        if w_t:
            in_specs.append(pl.BlockSpec((None, tn, kdim), lambda j, i, te, tv, ob=ob: (te[i], j + ob, 0)))
        else:
            in_specs.append(pl.BlockSpec((None, kdim, tn), lambda j, i, te, tv, ob=ob: (te[i], 0, j + ob)))
        args.append(w)
    if epi == "headnorm":
        in_specs += [pl.BlockSpec((1, tn), lambda j, i, te, tv: (0, j))] * 2
        args += [gain, flag]
    n_r, r_full = 0, None
    if resid is not None:
        r_arrs, r_maps, r_full, r_m = _split_stream(resid, tm)
        assert r_m == m
        n_r = len(r_arrs)
        in_specs += [pl.BlockSpec((tm, tn), lambda j, i, te, tv, f=f: (f(i), j)) for f in r_maps]
        args += r_arrs
    dtypes = out_dtype if isinstance(out_dtype, tuple) else (out_dtype,)
    body = functools.partial(_gmm_body, n_x=len(x_arrs), x_full=x_full, n_w=n_w, cast_w=cast_w, w_t=w_t, epi=epi,
                             n_r=n_r, r_full=r_full, tn=tn, n_o=len(dtypes))
    outs = pl.pallas_call(
        body,
        grid_spec=pltpu.PrefetchScalarGridSpec(
            num_scalar_prefetch=2,
            grid=(n_n, n_m),
            in_specs=in_specs,
            out_specs=[pl.BlockSpec((tm, tn), lambda j, i, te, tv: (i, j)) for _ in dtypes],
            scratch_shapes=[pltpu.VMEM((tn, kdim) if w_t else (kdim, tn), BF) for _ in range(n_w if cast_w else 0)],
        ),
        out_shape=[jax.ShapeDtypeStruct((m, n_out), dt) for dt in dtypes],
        compiler_params=_cparams(("arbitrary", "arbitrary")),
        name=name,
    )(texp, tval, *args)
    return outs if isinstance(out_dtype, tuple) else outs[0]


def _c1_body(x_ref, w_ref, o_ref):
    nsb = x_ref.shape[0] // NSA_CMP_STRIDE
    acc = jnp.zeros((nsb, 2 * NSA_CMP_HIDDEN), F32)
    for s in range(NSA_CMP_STRIDE):
        xs = x_ref[pl.ds(s, nsb, stride=NSA_CMP_STRIDE), :]
        acc = acc + jnp.dot(xs.astype(BF), w_ref[s], preferred_element_type=F32)
    o_ref[...] = acc


def _compress_c1_prompt(y, wcat, nb, t):
    nsub = t // NSA_CMP_STRIDE
    return pl.pallas_call(
        _c1_body,
        grid=(nb, 2, NSA_KV),
        in_specs=[pl.BlockSpec((t, LANE), lambda b, kv, g: (b, CH_KC + 2 * kv + g)),
                  pl.BlockSpec((None, NSA_CMP_STRIDE, HEAD_DIM, 2 * NSA_CMP_HIDDEN), lambda b, kv, g: (kv, 0, 0, 0))],
        out_specs=pl.BlockSpec((None, None, None, nsub, 2 * NSA_CMP_HIDDEN), lambda b, kv, g: (b, kv, g, 0, 0)),
        out_shape=jax.ShapeDtypeStruct((nb, 2, NSA_KV, nsub, 2 * NSA_CMP_HIDDEN), F32),
        compiler_params=_cparams(("parallel", "parallel", "parallel")),
        name="cmp_c1_prompt",
    )(y, wcat)


C1_PAGES = 16


def _c1_sample_body(pt_ref, *refs, n_in, page):
    x_refs = refs[:n_in]
    w_ref, o_ref, scr = refs[n_in:n_in + 3]
    nsb = page // NSA_CMP_STRIDE
    rows = n_in * nsb
    for kv in range(2):
        acc = jnp.zeros((rows * NSA_KV, 2 * NSA_CMP_HIDDEN), F32)

        def rows_at(s):
            parts = [xr[pl.ds(s, nsb, stride=NSA_CMP_STRIDE), kv, :, :].reshape(nsb * NSA_KV, HEAD_DIM) for xr in x_refs]
            return (parts[0] if n_in == 1 else jnp.concatenate(parts, axis=0)).astype(BF)

        for s in range(0, NSA_CMP_STRIDE, 2):
            xs = jnp.concatenate([rows_at(s), rows_at(s + 1)], axis=1)
            w = w_ref[kv, s:s + 2].reshape(2 * HEAD_DIM, 2 * NSA_CMP_HIDDEN)
            acc = acc + jnp.dot(xs, w, preferred_element_type=F32)
        for c in range(scr.shape[0]):
            scr[c] = acc[:, c * LANE:(c + 1) * LANE]
        for g in range(NSA_KV):
            for c in range(scr.shape[0]):
                o_ref[kv, g, :, c * LANE:(c + 1) * LANE] = scr[c, pl.ds(g, rows, stride=NSA_KV), :]


def _compress_c1_sample(cache, li, page_table, wcat):
    nb, n_pages = page_table.shape
    page = cache.shape[2]
    npp = math.gcd(C1_PAGES, n_pages)
    nsb = page // NSA_CMP_STRIDE
    nsub = n_pages * nsb
    body = functools.partial(_c1_sample_body, n_in=npp, page=page)
    in_specs = [
        pl.BlockSpec((None, None, page, 2, NSA_KV, HEAD_DIM), lambda b, j, pt, i=i: (li, pt[b, j * npp + i], 0, 0, 0, 0))
        for i in range(npp)
    ]
    in_specs.append(pl.BlockSpec(wcat.shape, lambda b, j, pt: (0, 0, 0, 0)))
    return pl.pallas_call(
        body,
        grid_spec=pltpu.PrefetchScalarGridSpec(
            num_scalar_prefetch=1,
            grid=(nb, n_pages // npp),
            in_specs=in_specs,
            out_specs=pl.BlockSpec((None, 2, NSA_KV, npp * nsb, 2 * NSA_CMP_HIDDEN), lambda b, j, pt: (b, 0, 0, j, 0)),
            scratch_shapes=[pltpu.VMEM((2 * NSA_CMP_HIDDEN // LANE, npp * nsb * NSA_KV, LANE), F32)],
        ),
        out_shape=jax.ShapeDtypeStruct((nb, 2, NSA_KV, nsub, 2 * NSA_CMP_HIDDEN), F32),
        compiler_params=_cparams(("parallel", "parallel")),
        name="cmp_c1_sample",
    )(page_table, *([cache] * npp), wcat)


def _c2_body(p_ref, pe_ref, wcat_ref, w2_ref, g_ref, o_ref, *, nsub, n_c):
    kv = pl.program_id(1)
    pep = jnp.zeros((SUBLANE, 2 * NSA_CMP_HIDDEN), F32)
    for s in range(NSA_CMP_STRIDE):
        pep = pep + jnp.dot(pe_ref[s], wcat_ref[s], preferred_element_type=F32)
    bias = pep[0:1, :NSA_CMP_HIDDEN] + pep[1:2, NSA_CMP_HIDDEN:]
    nxt = pltpu.roll(p_ref[:, NSA_CMP_HIDDEN:], nsub - 1, 0)
    h = p_ref[:, :NSA_CMP_HIDDEN] + nxt + bias
    h = h * jax.nn.sigmoid(h)
    o = jnp.dot(h.astype(BF), w2_ref[...], preferred_element_type=F32)
    r = lax.rsqrt(jnp.mean(o * o, axis=-1, keepdims=True) + NORM_EPS)
    o = jnp.where(kv == 0, o * r * g_ref[...], o)
    row = lax.broadcasted_iota(jnp.int32, (nsub, 1), 0)
    o_ref[...] = jnp.where(row < n_c, o, 0.0)


def _compress_c2(p, pe_rows, wcat, w2, gk):
    nb, _, _, nsub, _ = p.shape
    n_c = nsub - NSA_CMP_LEN // NSA_CMP_STRIDE + 1
    body = functools.partial(_c2_body, nsub=nsub, n_c=n_c)
    return pl.pallas_call(
        body,
        grid=(nb, 2, NSA_KV),
        in_specs=[pl.BlockSpec((None, None, None, nsub, 2 * NSA_CMP_HIDDEN), lambda b, kv, g: (b, kv, g, 0, 0)),
                  pl.BlockSpec((None, NSA_CMP_STRIDE, SUBLANE, HEAD_DIM), lambda b, kv, g: (kv, 0, 0, 0)),
                  pl.BlockSpec((None, NSA_CMP_STRIDE, HEAD_DIM, 2 * NSA_CMP_HIDDEN), lambda b, kv, g: (kv, 0, 0, 0)),
                  pl.BlockSpec((None, NSA_CMP_HIDDEN, HEAD_DIM), lambda b, kv, g: (kv, 0, 0)),
                  pl.BlockSpec((1, HEAD_DIM), lambda b, kv, g: (0, 0))],
        out_specs=pl.BlockSpec((None, None, None, nsub, HEAD_DIM), lambda b, kv, g: (b, kv, g, 0, 0)),
        out_shape=jax.ShapeDtypeStruct((nb, 2, NSA_KV, nsub, HEAD_DIM), F32),
        compiler_params=_cparams(("parallel", "parallel", "parallel")),
        name="cmp_c2",
    )(p, pe_rows, wcat, w2, gk)


def _rank_select(sc_ref, n_iter, k):
    score = sc_ref[...]
    brow = lax.broadcasted_iota(jnp.int32, score.shape, 0)

    def body(j, rank):
        r = sc_ref[pl.ds(j, 1), :]
        beats = jnp.logical_or(r > score, jnp.logical_and(r == score, j < brow))
        return rank + jnp.where(beats, 1.0, 0.0)

    rank = lax.fori_loop(0, n_iter, body, jnp.zeros(score.shape, F32))
    return rank < k


def _cattn_body(q_ref, kc_ref, vc_ref, oc_ref, sel_ref, sc_ref, *, tq, ncp, n_c, ns, nsp, pos0, ksel):
    t0 = pos0 + pl.program_id(2) * tq
    scale = HEAD_DIM ** -0.5
    kc = kc_ref[...].astype(BF)
    vc = vc_ref[...].astype(BF)
    trow = t0 + lax.broadcasted_iota(jnp.int32, (tq, 1), 0)
    ncol = lax.broadcasted_iota(jnp.int32, (1, ncp), 1)
    valid = jnp.logical_and(ncol * NSA_CMP_STRIDE + (NSA_CMP_LEN - 1) <= trow, ncol < n_c)
    tcol = t0 + lax.broadcasted_iota(jnp.int32, (1, tq), 1)
    nrow = lax.broadcasted_iota(jnp.int32, (ncp, 1), 0)
    valid_t = jnp.logical_and(nrow * NSA_CMP_STRIDE + (NSA_CMP_LEN - 1) <= tcol, nrow < n_c)
    psum_t = jnp.zeros((ncp, tq), F32)
    for z in range(NSA_HPG):
        sl = slice(z * HEAD_DIM, (z + 1) * HEAD_DIM)
        q = (q_ref[:, sl] * scale).astype(BF)
        s = lax.dot_general(q, kc, NT_DIMS, preferred_element_type=F32)
        p = _masked_softmax(s, valid, -1)
        oc_ref[:, sl] = jnp.dot(p.astype(BF), vc, preferred_element_type=F32)
        s_t = lax.dot_general(kc, q, NT_DIMS, preferred_element_type=F32)
        psum_t = psum_t + _masked_softmax(s_t, valid_t, 0)
    r = NSA_SEL_BLOCK // NSA_CMP_STRIDE
    brow = lax.broadcasted_iota(jnp.int32, (nsp, 1), 0)
    lo = r * brow - 1
    inside = jnp.logical_and(ncol >= lo, ncol <= lo + r)
    edge = jnp.logical_or(ncol == lo, ncol == lo + r)
    m_t = jnp.where(inside, jnp.where(edge, 0.5, 1.0), 0.0).astype(BF)
    hi = psum_t.astype(BF)
    mid = (psum_t - hi.astype(F32)).astype(BF)
    low = (psum_t - hi.astype(F32) - mid.astype(F32)).astype(BF)
    imp_t = (jnp.dot(m_t, hi, preferred_element_type=F32) + jnp.dot(m_t, mid, preferred_element_type=F32)
             + jnp.dot(m_t, low, preferred_element_type=F32))
    own = tcol // NSA_SEL_BLOCK
    past = jnp.logical_and(brow < own, brow < ns)
    forced = jnp.logical_or(brow == 0, brow == own - 1)
    sc_ref[...] = jnp.where(past, jnp.where(forced, NSA_SEL_FORCE, imp_t), NEG_INF)
    n_live = jnp.minimum(ns, (t0 + tq - 1) // NSA_SEL_BLOCK)
    chosen = jnp.logical_and(past, _rank_select(sc_ref, n_live, ksel))
    sel = jnp.where(jnp.logical_or(chosen, brow == own), 1.0, 0.0)
    rows = sel_ref.shape[0]
    if rows > nsp:
        sel_ref[...] = jnp.zeros(sel_ref.shape, F32)
    sel_ref[0:nsp, :] = sel


def _cattn(q_arr, q_row_blk0, q_col_blk0, kvc, nb, tlen, tq, length, pos0, out_rows):
    nq = tlen // tq
    ncp = kvc.shape[3]
    n_c = ncp - NSA_CMP_LEN // NSA_CMP_STRIDE + 1
    ns = length // NSA_SEL_BLOCK
    ksel = min(NSA_SEL_COUNT - 1, ns)
    nsp = -(-ns // SUBLANE) * SUBLANE
    sel_rows = max(nsp, LANE)
    body = functools.partial(_cattn_body, tq=tq, ncp=ncp, n_c=n_c, ns=ns, nsp=nsp, pos0=pos0, ksel=ksel)
    gw = NSA_HPG * HEAD_DIM
    return pl.pallas_call(
        body,
        grid=(nb, NSA_KV, nq),
        in_specs=[pl.BlockSpec((tq, gw), lambda b, g, qi: (q_row_blk0 + b * nq + qi, q_col_blk0 + g)),
                  pl.BlockSpec((None, None, None, ncp, HEAD_DIM), lambda b, g, qi: (b, 0, g, 0, 0)),
                  pl.BlockSpec((None, None, None, ncp, HEAD_DIM), lambda b, g, qi: (b, 1, g, 0, 0))],
        out_specs=[pl.BlockSpec((tq, gw), lambda b, g, qi: (b * nq + qi, g)),
                   pl.BlockSpec((None, None, sel_rows, tq), lambda b, g, qi: (b, g, 0, qi))],
        out_shape=[jax.ShapeDtypeStruct((out_rows, NSA_KV * gw), F32),
                   jax.ShapeDtypeStruct((nb, NSA_KV, sel_rows, tlen), F32)],
        scratch_shapes=[pltpu.VMEM((nsp, tq), F32)],
        compiler_params=_cparams(("parallel", "parallel", "parallel")),
        name="nsa_cmp_attn",
    )(q_arr, kvc, kvc)


def _cattn_row_body(q_ref, kc_ref, vc_ref, oc_ref, sel_ref, *, tq, ncp, n_c, ns, nsl, pos0, ksel):
    scale = HEAD_DIM ** -0.5
    kc = kc_ref[...].astype(BF)
    vc = vc_ref[...].astype(BF)
    trow = pos0 + lax.broadcasted_iota(jnp.int32, (tq, 1), 0)
    ncol = lax.broadcasted_iota(jnp.int32, (1, ncp), 1)
    valid = jnp.logical_and(ncol * NSA_CMP_STRIDE + (NSA_CMP_LEN - 1) <= trow, ncol < n_c)
    psum = jnp.zeros((tq, ncp), F32)
    for z in range(NSA_HPG):
        sl = slice(z * HEAD_DIM, (z + 1) * HEAD_DIM)
        q = (q_ref[:, sl] * scale).astype(BF)
        p = _masked_softmax(lax.dot_general(q, kc, NT_DIMS, preferred_element_type=F32), valid, -1)
        oc_ref[:, sl] = jnp.dot(p.astype(BF), vc, preferred_element_type=F32)
        psum = psum + p
    r = NSA_SEL_BLOCK // NSA_CMP_STRIDE
    nrow = lax.broadcasted_iota(jnp.int32, (ncp, 1), 0)
    bcol = lax.broadcasted_iota(jnp.int32, (1, nsl), 1)
    lo = r * bcol - 1
    inside = jnp.logical_and(nrow >= lo, nrow <= lo + r)
    edge = jnp.logical_or(nrow == lo, nrow == lo + r)
    m = jnp.where(inside, jnp.where(edge, 0.5, 1.0), 0.0).astype(BF)
    hi = psum.astype(BF)
    mid = (psum - hi.astype(F32)).astype(BF)
    low = (psum - hi.astype(F32) - mid.astype(F32)).astype(BF)
    imp = (jnp.dot(hi, m, preferred_element_type=F32) + jnp.dot(mid, m, preferred_element_type=F32)
           + jnp.dot(low, m, preferred_element_type=F32))
    own = trow // NSA_SEL_BLOCK
    past = jnp.logical_and(bcol < own, bcol < ns)
    forced = jnp.logical_or(bcol == 0, bcol == own - 1)
    score = jnp.where(past, jnp.where(forced, NSA_SEL_FORCE, imp), NEG_INF)
    rank = jnp.zeros((tq, nsl), F32)
    for j in range(ns):
        cj = score[:, j:j + 1]
        beats = jnp.logical_or(cj > score, jnp.logical_and(cj == score, j < bcol))
        rank = rank + jnp.where(beats, 1.0, 0.0)
    chosen = jnp.logical_and(past, rank < ksel)
    sel_ref[...] = jnp.where(jnp.logical_or(chosen, bcol == own), 1.0, 0.0)


def _cattn_sample(q_pad, kvc, nb, length, pos0):
    tq = SUBLANE
    ncp = kvc.shape[3]
    n_c = ncp - NSA_CMP_LEN // NSA_CMP_STRIDE + 1
    ns = length // NSA_SEL_BLOCK
    ksel = min(NSA_SEL_COUNT - 1, ns)
    nsl = -(-ns // LANE) * LANE
    body = functools.partial(_cattn_row_body, tq=tq, ncp=ncp, n_c=n_c, ns=ns, nsl=nsl, pos0=pos0, ksel=ksel)
    gw = NSA_HPG * HEAD_DIM
    return pl.pallas_call(
        body,
        grid=(nb, NSA_KV),
        in_specs=[pl.BlockSpec((tq, gw), lambda b, g: (b, g)),
                  pl.BlockSpec((None, None, None, ncp, HEAD_DIM), lambda b, g: (b, 0, g, 0, 0)),
                  pl.BlockSpec((None, None, None, ncp, HEAD_DIM), lambda b, g: (b, 1, g, 0, 0))],
        out_specs=[pl.BlockSpec((tq, gw), lambda b, g: (b, g)),
                   pl.BlockSpec((None, None, tq, nsl), lambda b, g: (b, g, 0, 0))],
        out_shape=[jax.ShapeDtypeStruct((nb * tq, NSA_KV * gw), F32),
                   jax.ShapeDtypeStruct((nb, NSA_KV, tq, nsl), F32)],
        compiler_params=_cparams(("parallel", "parallel")),
        name="nsa_cmp_attn_sample",
    )(q_pad, kvc, kvc)


def _gate_body(q_ref, k_ref, sel_ref, mean_ref, sc_ref, *, tq, nb, nbp, pos0):
    qi = pl.program_id(2)

    @pl.when(qi == 0)
    def _():
        mean_ref[...] = jnp.zeros(mean_ref.shape, F32)
        for j in range(nb):
            blk = k_ref[j * MOBA_BLOCK:(j + 1) * MOBA_BLOCK, :]
            mean_ref[j:j + 1, :] = jnp.sum(blk, axis=0, keepdims=True) * (1.0 / MOBA_BLOCK)

    g_t = _dot3(mean_ref[...], q_ref[...], NT_DIMS)
    tcol = pos0 + qi * tq + lax.broadcasted_iota(jnp.int32, (1, tq), 1)
    own = tcol // MOBA_BLOCK
    brow = lax.broadcasted_iota(jnp.int32, (nbp, 1), 0)
    past = jnp.logical_and(brow < own, brow < nb)
    sc_ref[...] = jnp.where(past, g_t, NEG_INF)
    chosen = jnp.logical_and(past, _rank_select(sc_ref, nb, min(MOBA_TOPK, nb)))
    sel = jnp.where(jnp.logical_or(chosen, brow == own), 1.0, 0.0)
    if sel_ref.shape[0] > nbp:
        sel_ref[...] = jnp.zeros(sel_ref.shape, F32)
    sel_ref[0:nbp, :] = sel


def _moba_gate_prompt(y, nb, t, tq):
    nq = t // tq
    n_full = t // MOBA_BLOCK
    nbp = -(-n_full // SUBLANE) * SUBLANE
    body = functools.partial(_gate_body, tq=tq, nb=n_full, nbp=nbp, pos0=0)
    return pl.pallas_call(
        body,
        grid=(nb, MOBA_HEADS, nq),
        in_specs=[pl.BlockSpec((tq, LANE), lambda b, h, qi: (b * nq + qi, CH_QB + h)),
                  pl.BlockSpec((t, LANE), lambda b, h, qi: (b, CH_KB + h))],
        out_specs=pl.BlockSpec((None, None, LANE, tq), lambda b, h, qi: (b, h, 0, qi)),
        out_shape=jax.ShapeDtypeStruct((nb, MOBA_HEADS, LANE, t), F32),
        scratch_shapes=[pltpu.VMEM((nbp, HEAD_DIM), F32), pltpu.VMEM((nbp, tq), F32)],
        compiler_params=_cparams(("parallel", "parallel", "arbitrary")),
        name="moba_gate_prompt",
    )(y, y)


def _gate_sample_body(q_ref, m_ref, sel_ref, *, nb, pos0):
    tq = q_ref.shape[0]
    own = (pos0 + lax.broadcasted_iota(jnp.int32, (tq, 1), 0)) // MOBA_BLOCK
    bcol = lax.broadcasted_iota(jnp.int32, (1, nb), 1)
    past = bcol < own
    for h in range(MOBA_HEADS):
        g = _dot3(q_ref[:, h * HEAD_DIM:(h + 1) * HEAD_DIM], m_ref[h], NT_DIMS)
        score = jnp.where(past, g, NEG_INF)
        rank = jnp.zeros((tq, nb), F32)
        for j in range(nb):
            cj = score[:, j:j + 1]
            beats = jnp.logical_or(cj > score, jnp.logical_and(cj == score, j < bcol))
            rank = rank + jnp.where(beats, 1.0, 0.0)
        chosen = jnp.logical_and(past, rank < min(MOBA_TOPK, nb))
        sel_ref[h] = jnp.where(jnp.logical_or(chosen, bcol == own), 1.0, 0.0)


def _moba_gate_sample(q_pad, means, pos0):
    nb, _, n_full, _ = means.shape
    tq = SUBLANE
    body = functools.partial(_gate_sample_body, nb=n_full, pos0=pos0)
    return pl.pallas_call(
        body,
        grid=(nb,),
        in_specs=[pl.BlockSpec((tq, MOBA_HEADS * HEAD_DIM), lambda b: (b, 0)),
                  pl.BlockSpec((None, MOBA_HEADS, n_full, HEAD_DIM), lambda b: (b, 0, 0, 0))],
        out_specs=pl.BlockSpec((None, MOBA_HEADS, tq, n_full), lambda b: (b, 0, 0, 0)),
        out_shape=jax.ShapeDtypeStruct((nb, MOBA_HEADS, tq, n_full), F32),
        compiler_params=_cparams(("parallel",)),
        name="moba_gate_sample",
    )(q_pad, means)


MEANS_BLOCKS = 4


def _means_body(pt_ref, *refs, ppb, bps):
    c_refs = refs[:bps * ppb]
    o_ref = refs[bps * ppb]
    j = pl.program_id(1)
    for q in range(bps):
        acc = jnp.sum(c_refs[q * ppb][...], axis=0)
        for i in range(1, ppb):
            acc = acc + jnp.sum(c_refs[q * ppb + i][...], axis=0)
        m = acc * (1.0 / MOBA_BLOCK)
        for h in range(MOBA_HEADS):
            o_ref[h, pl.ds(j * bps + q, 1), :] = m[h:h + 1, :]


def _moba_means_sample(cache, li, page_table):
    nb, n_pages = page_table.shape
    page = cache.shape[2]
    ppb = MOBA_BLOCK // page
    n_full = n_pages // ppb
    bps = math.gcd(MEANS_BLOCKS, n_full)
    body = functools.partial(_means_body, ppb=ppb, bps=bps)
    in_specs = [
        pl.BlockSpec((None, None, page, None, MOBA_HEADS, HEAD_DIM),
                     lambda b, j, pt, i=i: (li, pt[b, j * bps * ppb + i], 0, 0, 0, 0))
        for i in range(bps * ppb)
    ]
    return pl.pallas_call(
        body,
        grid_spec=pltpu.PrefetchScalarGridSpec(
            num_scalar_prefetch=1,
            grid=(nb, n_full // bps),
            in_specs=in_specs,
            out_specs=pl.BlockSpec((None, MOBA_HEADS, n_full, HEAD_DIM), lambda b, j, pt: (b, 0, 0, 0)),
        ),
        out_shape=jax.ShapeDtypeStruct((nb, MOBA_HEADS, n_full, HEAD_DIM), F32),
        compiler_params=_cparams(("parallel", "arbitrary")),
        name="moba_means_sample",
    )(page_table, *([cache] * (bps * ppb)))


FLASH_UNIT_ROWS = 256
LOG2_E = 1.4426950408889634
MASK_BIG = 2.0 ** 60


def _flash_body(*refs, tq, tk, nh, blk, window, masked, ur):
    q_ref, k_ref, v_ref = refs[:3]
    pos = 3
    if masked:
        sel_ref = refs[pos]
        pos += 1
    o_ref, qs_ref, kt_ref = refs[pos:pos + 3]
    pos += 1
    units = [(z, r0) for z in range(nh) for r0 in range(0, tq, ur)]
    stat = refs[pos + 2:pos + 2 + 2 * len(units)]
    m_refs, acc_refs = stat[0::2], stat[1::2]
    qi = pl.program_id(2)
    t0 = qi * tq
    scale = HEAD_DIM ** -0.5 * LOG2_E
    if masked:
        bias = ((jnp.transpose(sel_ref[...]) - 1.0) * MASK_BIG).astype(BF)
    for z in range(nh):
        qz = (q_ref[:, z * HEAD_DIM:(z + 1) * HEAD_DIM] * scale).astype(BF)
        qs_ref[z] = jnp.concatenate([qz, bias], axis=1) if masked else qz
    for u in range(len(units)):
        m_refs[u][...] = jnp.full(m_refs[u].shape, NEG_INF, F32)
        acc_refs[u][...] = jnp.zeros(acc_refs[u].shape, F32)
    trow = t0 + lax.broadcasted_iota(jnp.int32, (tq, 1), 0)

    @pl.when(qi == 0)
    def _():
        for c in range(kt_ref.shape[0]):
            kc = k_ref[c * tk:(c + 1) * tk, :]
            if masked:
                prow = c * tk + lax.broadcasted_iota(jnp.int32, (tk, 1), 0)
                jcol = lax.broadcasted_iota(jnp.int32, (1, LANE), 1)
                kc = jnp.concatenate([kc, jnp.where(prow // blk == jcol, 1.0, 0.0).astype(BF)], axis=1)
            kt_ref[c] = kc.T

    def process(c, edge):
        ks = pl.multiple_of(c * tk, tk)
        kt = kt_ref[c]
        vc = jnp.concatenate([v_ref[pl.ds(ks, tk), :], jnp.ones((tk, LANE), BF)], axis=1)
        if edge:
            pcol = ks + lax.broadcasted_iota(jnp.int32, (1, tk), 1)
            valid = pcol <= trow
            if window:
                valid = jnp.logical_and(valid, pcol > trow - window)
        for u, (z, r0) in enumerate(units):
            s = jnp.dot(qs_ref[z, r0:r0 + ur, :], kt, preferred_element_type=F32)
            m_old = m_refs[u][...]
            if edge:
                s = jnp.where(valid[r0:r0 + ur], s, NEG_INF)
                m_new = jnp.maximum(m_old, jnp.max(s, axis=-1, keepdims=True))
                m_use = jnp.where(m_new == NEG_INF, 0.0, m_new)
            else:
                m_new = jnp.maximum(m_old, jnp.max(s, axis=-1, keepdims=True))
                m_use = m_new
            alpha = jnp.exp2(m_old - m_use)
            p = jnp.exp2(s - m_use)
            acc_refs[u][...] = alpha * acc_refs[u][...] + jnp.dot(p.astype(BF), vc, preferred_element_type=F32)
            m_refs[u][...] = m_new

    if masked:
        def pair(c2, carry):
            process(2 * c2, False)
            process(2 * c2 + 1, False)
            return carry

        lax.fori_loop(0, qi // 2, pair, 0)

        @pl.when(qi % 2 == 1)
        def _():
            process(qi - 1, False)

        process(qi, True)
    else:
        c_hi = (t0 + tq + tk - 1) // tk
        c_lo = jnp.maximum(t0 - window + 1, 0) // tk
        lax.fori_loop(c_lo, c_hi, lambda c, carry: (process(c, True), carry)[1], 0)
    for u, (z, r0) in enumerate(units):
        l = acc_refs[u][:, HEAD_DIM:HEAD_DIM + 1]
        o_ref[r0:r0 + ur, z * HEAD_DIM:(z + 1) * HEAD_DIM] = acc_refs[u][:, :HEAD_DIM] / jnp.where(l > 0, l, 1.0)


def _flash(y, y16, sel, nb, t, *, q_ch, k_ch, v_ch, kvh, nh, blk, window, out_rows, tq=512, tk=512):
    tq = min(tq, t)
    tk = min(tk, t)
    nq = t // tq
    masked = sel is not None
    gw = nh * HEAD_DIM
    assert q_ch % nh == 0 and (tq == tk or not masked)
    in_specs = [pl.BlockSpec((tq, gw), lambda b, h, qi: (b * nq + qi, q_ch // nh + h)),
                pl.BlockSpec((t, LANE), lambda b, h, qi: (b, k_ch + h)),
                pl.BlockSpec((t, LANE), lambda b, h, qi: (b, v_ch + h))]
    args = [y, y16, y16]
    if masked:
        in_specs.append(pl.BlockSpec((None, None, LANE, tq), lambda b, h, qi: (b, h, 0, qi)))
        args.append(sel)
    ur = min(FLASH_UNIT_ROWS, tq)
    n_units = nh * (tq // ur)
    kd = (2 if masked else 1) * HEAD_DIM
    body = functools.partial(_flash_body, tq=tq, tk=tk, nh=nh, blk=blk, window=window, masked=masked, ur=ur)
    unit_scratch = [pltpu.VMEM((ur, 1), F32), pltpu.VMEM((ur, 2 * HEAD_DIM), F32)]
    return pl.pallas_call(
        body,
        grid=(nb, kvh, nq),
        in_specs=in_specs,
        out_specs=pl.BlockSpec((tq, gw), lambda b, h, qi: (b * nq + qi, h)),
        out_shape=jax.ShapeDtypeStruct((out_rows, kvh * gw), F32),
        scratch_shapes=[pltpu.VMEM((nh, tq, kd), BF), pltpu.VMEM((t // tk, kd, tk), BF)] + unit_scratch * n_units,
        compiler_params=_cparams(("parallel", "parallel", "arbitrary")),
        name="flash_" + ("win" if window else "blk%d" % blk),
    )(*args)


def _dec_body(tbl_ref, vld_ref, q_ref, *refs, nh, rows, kvh, nblk, r_min):
    k_refs = refs[:nblk]
    v_refs = refs[nblk:2 * nblk]
    kn_ref, vn_ref, o_ref = refs[2 * nblk:2 * nblk + 3]
    b = pl.program_id(0)
    h = pl.program_id(1)
    scale = HEAD_DIM ** -0.5
    qrow = q_ref[pl.ds(b, 1), :] * scale
    zrow = lax.broadcasted_iota(jnp.int32, (SUBLANE, 1), 0)
    qm = jnp.zeros((SUBLANE, HEAD_DIM), F32)
    for z in range(nh):
        qm = jnp.where(zrow == z, qrow[:, z * HEAD_DIM:(z + 1) * HEAD_DIM], qm)
    qb = qm.astype(BF)
    col = lax.broadcasted_iota(jnp.int32, (1, rows * kvh), 1)
    valid = jnp.logical_and(col % kvh == h, col // kvh >= r_min)
    kn = kn_ref[pl.ds(b, 1), :]
    vn = vn_ref[pl.ds(b, 1), :]
    s_new = jnp.sum(qm * kn, axis=-1, keepdims=True)
    scores = []
    m = s_new
    for i in range(nblk):
        kk = k_refs[i][...].reshape(rows * kvh, HEAD_DIM).astype(BF)
        s = lax.dot_general(qb, kk, NT_DIMS, preferred_element_type=F32)
        s = jnp.where(jnp.logical_and(valid, vld_ref[b, h, i] > 0), s, NEG_INF)
        scores.append(s)
        m = jnp.maximum(m, jnp.max(s, axis=-1, keepdims=True))
    p_new = jnp.exp(s_new - m)
    l = p_new
    acc = p_new * vn
    for i in range(nblk):
        p = jnp.exp(scores[i] - m)
        l = l + jnp.sum(p, axis=-1, keepdims=True)
        vv = v_refs[i][...].reshape(rows * kvh, HEAD_DIM).astype(BF)
        acc = acc + jnp.dot(p.astype(BF), vv, preferred_element_type=F32)
    o_ref[...] = acc / l


def _decode_attn(ys, cache, li_fixed, tbl, vld, *, q_ch, kn_ch, vn_ch, nh, rows, kvh, r_min, name):
    nb, _, nblk = tbl.shape
    page = cache.shape[2]
    bpp = page // rows
    gw = nh * HEAD_DIM
    body = functools.partial(_dec_body, nh=nh, rows=rows, kvh=kvh, nblk=nblk, r_min=r_min)

    def kv_spec(kv, i):
        def imap(b, h, tb, vl):
            t = tb[b, h, i]
            return (li_fixed, t // bpp, t % bpp, kv, 0, 0)

        return pl.BlockSpec((None, None, rows, None, kvh, HEAD_DIM), imap)

    nrow = ys.shape[0]
    return pl.pallas_call(
        body,
        grid_spec=pltpu.PrefetchScalarGridSpec(
            num_scalar_prefetch=2,
            grid=(nb, kvh),
            in_specs=[pl.BlockSpec((nrow, gw), lambda b, h, tb, vl: (0, q_ch // nh + h))]
            + [kv_spec(0, i) for i in range(nblk)] + [kv_spec(1, i) for i in range(nblk)]
            + [pl.BlockSpec((nrow, LANE), lambda b, h, tb, vl: (0, kn_ch + h)),
               pl.BlockSpec((nrow, LANE), lambda b, h, tb, vl: (0, vn_ch + h))],
            out_specs=pl.BlockSpec((None, None, SUBLANE, HEAD_DIM), lambda b, h, tb, vl: (b, h, 0, 0)),
        ),
        out_shape=jax.ShapeDtypeStruct((nb, kvh, SUBLANE, HEAD_DIM), F32),
        compiler_params=_cparams(("parallel", "parallel")),
        name=name,
    )(tbl, vld, ys, *([cache] * (2 * nblk)), ys, ys)


def _mix_body(oc_ref, os_ref, ow_ref, ob_ref, soc_ref, sos_ref, sow_ref, sob_ref, gate_ref, o_ref, *, n_p_tiles, n_s):
    i = pl.program_id(0)
    w = NSA_HEADS * HEAD_DIM

    def emit(oc, os_, ow, ob, rows):
        g = jax.nn.sigmoid(gate_ref[0:rows, :])
        for h in range(NSA_HEADS):
            sl = slice(h * HEAD_DIM, (h + 1) * HEAD_DIM)
            o = (g[:, 3 * h:3 * h + 1] * oc[:, sl] + g[:, 3 * h + 1:3 * h + 2] * os_[:, sl]
                 + g[:, 3 * h + 2:3 * h + 3] * ow[:, sl])
            o_ref[0:rows, sl] = o.astype(o_ref.dtype)
        o_ref[0:rows, w:] = ob[...].astype(o_ref.dtype)

    @pl.when(i < n_p_tiles)
    def _():
        emit(oc_ref, os_ref, ow_ref, ob_ref, o_ref.shape[0])

    @pl.when(i >= n_p_tiles)
    def _():
        emit(soc_ref, sos_ref, sow_ref, sob_ref, n_s)


def _mix(prompt_parts, sample_parts, y, tm=256):
    n_prompt = prompt_parts[0].shape[0]
    n_s = sample_parts[0].shape[0]
    m = y.shape[0]
    tm = min(tm, n_prompt)
    assert n_prompt % tm == 0 and m == n_prompt + n_s and n_s <= tm
    n_p_tiles = n_prompt // tm
    wa = NSA_HEADS * HEAD_DIM
    wb = MOBA_HEADS * HEAD_DIM
    pmap = lambda i: (jnp.minimum(i, n_p_tiles - 1), 0)
    body = functools.partial(_mix_body, n_p_tiles=n_p_tiles, n_s=n_s)
    return pl.pallas_call(
        body,
        grid=(n_p_tiles + 1,),
        in_specs=[pl.BlockSpec((tm, wa), pmap)] * 3 + [pl.BlockSpec((tm, wb), pmap)]
        + [pl.BlockSpec((n_s, wa), lambda i: (0, 0))] * 3 + [pl.BlockSpec((n_s, wb), lambda i: (0, 0))]
        + [pl.BlockSpec((tm, LANE), lambda i: (i, CH_GATE))],
        out_specs=pl.BlockSpec((tm, wa + wb), lambda i: (i, 0)),
        out_shape=jax.ShapeDtypeStruct((m, wa + wb), BF),
        compiler_params=_cparams(("parallel",)),
        name="attn_mix",
    )(*prompt_parts, *sample_parts, y)


def _convmix_body(up_ref, hp_ref, uc_ref, hc_ref, pw_ref, ps_ref, dw_ref, dwb_ref, lng_ref, lnb_ref, pww_ref,
                  o_ref, zp_ref, zs_ref, cb_ref, *, ts, nt, zero_first, avail0, rc):
    s = pl.program_id(0)
    hp_rows = hp_ref.shape[0]
    hc_rows = hc_ref.shape[0]
    width = up_ref.shape[1]
    if zero_first:
        keep = jnp.where(s % nt == 0, 0.0, 1.0)
        zp_ref[0:hp_rows, :] = hp_ref[...] * keep
        zs_ref[0, 0:hc_rows, :] = hc_ref[...] * keep
        avail = (s % nt) * ts + avail0
    else:
        zp_ref[0:hp_rows, :] = hp_ref[...]
        zs_ref[0, 0:hc_rows, :] = hc_ref[...]
        avail = avail0
    zp_ref[hp_rows:, :] = up_ref[...]
    zs_ref[0, hc_rows:, :] = uc_ref[...]
    span = hc_rows + ts - SUBLANE
    for k in range(1, SUBLANE):
        zs_ref[k, 0:span, :] = zs_ref[0, k:k + span, :]
    gwidth = width // len(POOL_WINDOWS)
    t_idx = lax.broadcasted_iota(jnp.int32, (ts, 1), 0) + avail + 1
    for gi, w in enumerate(POOL_WINDOWS):
        sl = slice(gi * gwidth, (gi + 1) * gwidth)
        cur = zp_ref[hp_rows:hp_rows + ts, sl]
        acc = cur
        for jj in range(1, w):
            acc = acc + zp_ref[hp_rows - jj:hp_rows - jj + ts, sl]
        cnt = jnp.minimum(t_idx, w).astype(F32)
        yg = acc / cnt - cur
        og = jnp.dot(yg.astype(BF), pw_ref[gi], preferred_element_type=F32) * ps_ref[:, sl]
        o_ref[:, sl] = og.astype(o_ref.dtype)
    base = hc_rows - (CONV_WIDTH - 1)
    for r in range(ts // rc):
        for c in range(width // LANE):
            cs = slice(c * LANE, (c + 1) * LANE)
            acc = jnp.zeros((rc, LANE), F32)
            for jj in range(CONV_WIDTH):
                k = (base + jj) % SUBLANE
                lo = r * rc + base + jj - k
                acc = acc + zs_ref[k, lo:lo + rc, cs] * dw_ref[jj:jj + 1, cs]
            cb_ref[r * rc:(r + 1) * rc, cs] = acc + dwb_ref[:, cs]
    cv = cb_ref[...]
    mu = jnp.mean(cv, axis=-1, keepdims=True)
    xc = cv - mu
    yn = xc * lax.rsqrt(jnp.mean(xc * xc, axis=-1, keepdims=True) + NORM_EPS) * lng_ref[...] + lnb_ref[...]
    act = yn * jax.nn.sigmoid(yn)
    o_ref[:, width:] = jnp.dot(act.astype(BF), pww_ref[...], preferred_element_type=F32).astype(o_ref.dtype)


def _convmix(u_pool, halo_pool, u_conv, halo_conv, prm, *, ts, nt, n_tiles, zero_first, avail0, hp_rows, hc_rows,
             halo_blk):
    pw, ps, dw, dwb, lng, lnb, pww = prm
    width = u_pool.shape[1]
    rc = min(64, ts)
    body = functools.partial(_convmix_body, ts=ts, nt=nt, zero_first=zero_first, avail0=avail0, rc=rc)
    full = lambda a: pl.BlockSpec(a.shape, lambda s: (0,) * a.ndim)
    return pl.pallas_call(
        body,
        grid=(n_tiles,),
        in_specs=[pl.BlockSpec((ts, width), lambda s: (s, 0)),
                  pl.BlockSpec((hp_rows, width), lambda s: (halo_blk(s, hp_rows), 0)),
                  pl.BlockSpec((ts, width), lambda s: (s, 0)),
                  pl.BlockSpec((hc_rows, width), lambda s: (halo_blk(s, hc_rows), 0)),
                  full(pw), full(ps), full(dw), full(dwb), full(lng), full(lnb), full(pww)],
        out_specs=pl.BlockSpec((ts, 2 * width), lambda s: (s, 0)),
        out_shape=jax.ShapeDtypeStruct((n_tiles * ts, 2 * width), BF),
        scratch_shapes=[pltpu.VMEM((hp_rows + ts, width), F32), pltpu.VMEM((SUBLANE, hc_rows + ts, width), F32),
                        pltpu.VMEM((ts, width), F32)],
        compiler_params=_cparams(("parallel",)),
        name="conv_mix",
    )(u_pool, halo_pool, u_conv, halo_conv, pw, ps, dw, dwb, lng, lnb, pww)


def _router_body(x_ref, g_ref, rw_ref, r_ref, xn_ref):
    x = x_ref[...]
    xn = x * lax.rsqrt(jnp.mean(x * x, axis=-1, keepdims=True) + NORM_EPS) * g_ref[...]
    xn_ref[...] = xn.astype(xn_ref.dtype)
    logits = _dot3(xn, rw_ref[...])
    lane = lax.broadcasted_iota(jnp.int32, logits.shape, 1)
    l1 = jnp.where(lane < N_EXPERTS, logits, NEG_INF)
    m1 = jnp.max(l1, axis=-1, keepdims=True)
    i1 = jnp.min(jnp.where(l1 == m1, lane, LANE), axis=-1, keepdims=True)
    l2 = jnp.where(lane == i1, NEG_INF, l1)
    m2 = jnp.max(l2, axis=-1, keepdims=True)
    i2 = jnp.min(jnp.where(l2 == m2, lane, LANE), axis=-1, keepdims=True)
    e = jnp.exp(m2 - m1)
    w1 = 1.0 / (1.0 + e)
    w2 = e / (1.0 + e)
    r_ref[...] = jnp.where(lane == 0, i1.astype(F32),
                           jnp.where(lane == 1, i2.astype(F32), jnp.where(lane == 2, w1, jnp.where(lane == 3, w2, 0.0))))


def _router(x, g, rw, tm=256):
    m, d = x.shape
    rw_pad = jnp.pad(rw, ((0, 0), (0, LANE - rw.shape[1])))
    return pl.pallas_call(
        _router_body,
        grid=(pl.cdiv(m, tm),),
        in_specs=[pl.BlockSpec((tm, d), lambda i: (i, 0)), pl.BlockSpec((1, d), lambda i: (0, 0)),
                  pl.BlockSpec((d, LANE), lambda i: (0, 0))],
        out_specs=[pl.BlockSpec((tm, LANE), lambda i: (i, 0)), pl.BlockSpec((tm, d), lambda i: (i, 0))],
        out_shape=[jax.ShapeDtypeStruct((m, LANE), F32), jax.ShapeDtypeStruct((m, d), BF)],
        compiler_params=_cparams(("parallel",)),
        name="moe_router",
    )(x, g.reshape(1, d), rw_pad)


def _row_copy(src_hbm, row, dst, drow, sem):
    return pltpu.make_async_copy(src_hbm.at[pl.ds(row, 1), :], dst.at[pl.ds(drow, 1), :], sem)


def _gather_rows_body(c0_ref, nc_ref, src_ref, xn_hbm, o_ref, buf_ref, acc_ref, first_ref, sem, *, tc):
    i = pl.program_id(0)
    c0 = c0_ref[i]
    n = nc_ref[i]
    i_next = jnp.minimum(i + 1, pl.num_programs(0) - 1)
    n_next = jnp.where(i + 1 < pl.num_programs(0), nc_ref[i_next], 0)

    def chunk_copy(c, slot):
        return pltpu.make_async_copy(xn_hbm.at[pl.ds(c * tc, tc), :], buf_ref.at[slot], sem.at[slot])

    acc_ref[...] = jnp.zeros(acc_ref.shape, F32)

    @pl.when(i == 0)
    def _():
        first_ref[0] = 0

        @pl.when(n > 0)
        def _():
            chunk_copy(c0, 0).start()

    first = first_ref[0]
    src = src_ref[...]

    def body(k, carry):
        slot = (first + k) % 2
        chunk_copy(c0 + k, slot).wait()

        @pl.when(k + 1 < n)
        def _():
            chunk_copy(c0 + k + 1, 1 - slot).start()

        @pl.when(jnp.logical_and(k + 1 == n, n_next > 0))
        def _():
            chunk_copy(c0_ref[i_next], 1 - slot).start()
            first_ref[0] = 1 - slot

        tok = (c0 + k) * tc + lax.broadcasted_iota(jnp.int32, (1, tc), 1)
        onehot = jnp.where(src == tok, 1.0, 0.0).astype(buf_ref.dtype)
        acc_ref[...] += jnp.dot(onehot, buf_ref[slot], preferred_element_type=F32)
        return carry

    lax.fori_loop(0, n, body, 0)
    o_ref[...] = acc_ref[...].astype(o_ref.dtype)


def _gather_rows(xn, src, c0, nc, tm, tc):
    n_tok, d = xn.shape
    r_tot = src.shape[0]
    body = functools.partial(_gather_rows_body, tc=tc)
    return pl.pallas_call(
        body,
        grid_spec=pltpu.PrefetchScalarGridSpec(
            num_scalar_prefetch=2,
            grid=(r_tot // tm,),
            in_specs=[pl.BlockSpec((tm, 1), lambda i, a, b: (i, 0)), pl.BlockSpec(memory_space=pl.ANY)],
            out_specs=pl.BlockSpec((tm, d), lambda i, a, b: (i, 0)),
            scratch_shapes=[pltpu.VMEM((2, tc, d), xn.dtype), pltpu.VMEM((tm, d), F32), pltpu.SMEM((1,), jnp.int32),
                            pltpu.SemaphoreType.DMA((2,))],
        ),
        out_shape=jax.ShapeDtypeStruct((r_tot, d), xn.dtype),
        compiler_params=_cparams(("arbitrary",)),
        name="moe_gather",
    )(c0, nc, src.reshape(r_tot, 1), xn)


def _combine_body(p1_ref, p2_ref, eo_hbm, x_ref, r_ref, o_ref, ot_ref, b1_ref, b2_ref, sem, *, tm, n_main, n_tail):
    i = pl.program_id(0)
    base = i * tm
    n = jnp.where(i < n_main // tm, tm, n_tail)

    def issue(rq, c):
        for u in range(4):
            r = rq * 4 + u
            _row_copy(eo_hbm, p1_ref[base + r], b1_ref, r, sem).start()
            _row_copy(eo_hbm, p2_ref[base + r], b2_ref, r, sem).start()
        return c

    lax.fori_loop(0, n // 4, issue, 0)

    def wait(r, c):
        _row_copy(eo_hbm, 0, b1_ref, r, sem).wait()
        _row_copy(eo_hbm, 0, b2_ref, r, sem).wait()
        return c

    lax.fori_loop(0, n, wait, 0)

    def rows(r):
        rr = r_ref[0:r, :]
        return x_ref[0:r, :] + rr[:, 2:3] * b1_ref[0:r, :] + rr[:, 3:4] * b2_ref[0:r, :]

    @pl.when(i < n_main // tm)
    def _():
        o_ref[...] = rows(tm)

    @pl.when(i == n_main // tm)
    def _():
        ot_ref[...] = rows(n_tail)


def _combine(x, eo, routing, p1, p2, n_main, tm=256):
    m, d = x.shape
    n_tail = m - n_main
    assert n_main % tm == 0 and 0 < n_tail <= tm and n_tail % SUBLANE == 0
    n_full = n_main // tm
    body = functools.partial(_combine_body, tm=tm, n_main=n_main, n_tail=n_tail)
    return pl.pallas_call(
        body,
        grid_spec=pltpu.PrefetchScalarGridSpec(
            num_scalar_prefetch=2,
            grid=(n_full + 1,),
            in_specs=[pl.BlockSpec(memory_space=pl.ANY), pl.BlockSpec((tm, d), lambda i, a, b: (i, 0)),
                      pl.BlockSpec((tm, LANE), lambda i, a, b: (i, 0))],
            out_specs=[pl.BlockSpec((tm, d), lambda i, a, b: (jnp.minimum(i, n_full - 1), 0)),
                       pl.BlockSpec((n_tail, d), lambda i, a, b: (0, 0))],
            scratch_shapes=[pltpu.VMEM((tm, d), F32), pltpu.VMEM((tm, d), F32), pltpu.SemaphoreType.DMA(())],
        ),
        out_shape=[jax.ShapeDtypeStruct((n_main, d), F32), jax.ShapeDtypeStruct((n_tail, d), F32)],
        compiler_params=_cparams(("arbitrary",)),
        name="moe_combine",
    )(p1, p2, eo, x, routing)


def _moe_plan(e_idx, tm, tc):
    n = e_idx.shape[0]
    n_asg = n * TOP_K
    flat_e = e_idx.reshape(-1)
    onehot = (flat_e[:, None] == jnp.arange(N_EXPERTS, dtype=jnp.int32)[None, :]).astype(jnp.int32)
    cnt = jnp.sum(onehot, axis=0)
    rank = jnp.take_along_axis(jnp.cumsum(onehot, axis=0) - onehot, flat_e[:, None], axis=1)[:, 0]
    cnt_p = ((cnt + tm - 1) // tm) * tm
    ends = jnp.cumsum(cnt_p)
    off = ends - cnt_p
    pos = off[flat_e] + rank
    n_tiles = (n_asg + N_EXPERTS * (tm - 1) + tm - 1) // tm
    r_tot = n_tiles * tm
    src = jnp.full((r_tot,), -1, jnp.int32).at[pos].set(jnp.arange(n_asg, dtype=jnp.int32) // TOP_K)
    src_t = src.reshape(n_tiles, tm)
    lo = jnp.min(jnp.where(src_t >= 0, src_t, n), axis=1)
    hi = jnp.max(src_t, axis=1)
    c0 = jnp.where(hi >= 0, lo // tc, 0).astype(jnp.int32)
    nc = jnp.where(hi >= 0, hi // tc + 1 - lo // tc, 0).astype(jnp.int32)
    tile_start = jnp.arange(n_tiles, dtype=jnp.int32) * tm
    tval = (tile_start < ends[-1]).astype(jnp.int32)
    texp = jnp.minimum(jnp.sum((tile_start[:, None] >= ends[None, :]).astype(jnp.int32), axis=1), N_EXPERTS - 1)
    last = jnp.max(jnp.where(tval > 0, texp, 0))
    texp = jnp.where(tval > 0, texp, last)
    pos2 = pos.reshape(n, TOP_K)
    return src, c0, nc, pos2[:, 0], pos2[:, 1], texp, tval


def _moe(x, g, rw, wg, wu, wd, n_main, tm=256):
    n = x.shape[0]
    tc = max(t for t in range(2 * SUBLANE, 513, 2 * SUBLANE) if n % t == 0)
    routing, xn = _router(x, g, rw)
    e_idx = routing[:, :TOP_K].astype(jnp.int32)
    src, c0, nc, p1, p2, texp, tval = _moe_plan(e_idx, tm, tc)
    xs = _gather_rows(xn, src, c0, nc, tm, tc)
    hid = _gmm(xs, [(wg, 0), (wu, 0)], wg.shape[2], epi="swiglu", texp=texp, tval=tval, out_dtype=BF, tm=tm, tn=1024,
               name="moe_up")
    eo = _gmm(hid, [(wd, 0)], wd.shape[2], texp=texp, tval=tval, tm=tm, name="moe_down")
    return _combine(x, eo, routing, p1, p2, n_main)


def _seq_tails(a, nb, t, k):
    return jnp.stack([a[(b + 1) * t - k:(b + 1) * t] for b in range(nb)])


def _attn_layer(x, n_prompt, nb_p, t_p, nb_s, li, prm, caches, page_table):
    (attn_norm, w_in, qk_g, cmp_pe, cmp_w1, cmp_w2, moba_g, w_out, ffn_norm, w_gate, w_up, w_down) = prm
    cache_cmp, cache_sel, state_win, cache_moba = caches
    d = x[0].shape[1]
    past_len = page_table.shape[1] * cache_moba.shape[2]

    g_lo = NSA_CHUNKS * LANE
    g_hi = g_lo + 3 * NSA_HEADS
    assert w_in.shape[1] == g_hi + MOBA_CHUNKS * LANE
    w_t = w_in.T
    w_moba = w_t[g_hi:][None]
    w_gate_cols = jnp.pad(w_t[g_lo:g_hi], ((0, LANE - 3 * NSA_HEADS), (0, 0)))[None]
    ones = jnp.ones((HEAD_DIM,), F32)
    zeros = jnp.zeros((HEAD_DIM,), F32)
    gain_a = jnp.concatenate([qk_g[0]] * 8 + [ones] * 4 + [qk_g[2]] * 2 + [ones] * 2 + [qk_g[3]] * 2 + [ones] * 2)[None]
    flag_a = jnp.concatenate([ones] * 8 + [zeros] * 4 + [ones] * 2 + [zeros] * 2 + [ones] * 2 + [zeros] * 2)[None]
    gain_b = jnp.concatenate([moba_g[0]] * 8 + [moba_g[1]] * 8 + [ones] * 8)[None]
    flag_b = jnp.concatenate([ones] * 16 + [zeros] * 8)[None]
    xn = _rmsnorm(x, attn_norm, tm=STREAM_TM)
    ya, ya16 = _gmm(xn, [(w_t[None], 0)], NSA_CHUNKS * LANE, epi="headnorm", gain=gain_a, flag=flag_a,
                    out_dtype=(F32, BF), tm=STREAM_TM, tn=10 * LANE, w_t=True, name="attn_in_nsa")
    yb, yb16 = _gmm(xn, [(w_moba, 0)], MOBA_CHUNKS * LANE, epi="headnorm", gain=gain_b, flag=flag_b,
                    out_dtype=(F32, BF), tm=STREAM_TM, tn=8 * LANE, w_t=True, name="attn_in_moba")
    yg = _gmm(xn, [(w_gate_cols, 0)], LANE, tm=STREAM_TM, tn=LANE, w_t=True, name="attn_in_gate")

    r = NSA_CMP_LEN // NSA_CMP_STRIDE
    wcat = cmp_w1.reshape(2, r, NSA_CMP_STRIDE, HEAD_DIM, NSA_CMP_HIDDEN).transpose(0, 2, 3, 1, 4)
    wcat = wcat.reshape(2, NSA_CMP_STRIDE, HEAD_DIM, r * NSA_CMP_HIDDEN).astype(BF)
    pe_rows = cmp_pe.reshape(2, r, NSA_CMP_STRIDE, HEAD_DIM).transpose(0, 2, 1, 3)
    pe_rows = jnp.pad(pe_rows, ((0, 0), (0, 0), (0, SUBLANE - r), (0, 0))).astype(BF)
    w2 = cmp_w2.astype(BF)
    gk = qk_g[1][None]

    kvc_p = _compress_c2(_compress_c1_prompt(ya, wcat, nb_p, t_p), pe_rows, wcat, w2, gk)
    rows = n_prompt
    oc, sel_a = _cattn(ya, 0, 0, kvc_p, nb_p, t_p, min(512, t_p), t_p, 0, rows)
    sel_b = _moba_gate_prompt(yb, nb_p, t_p, min(1024, t_p))
    os_ = _flash(ya, ya16, sel_a, nb_p, t_p, q_ch=CH_QA, k_ch=CH_KS, v_ch=CH_VS, kvh=NSA_KV, nh=NSA_HPG,
                 blk=NSA_SEL_BLOCK, window=0, out_rows=rows)
    ow = _flash(ya, ya16, None, nb_p, t_p, q_ch=CH_QA, k_ch=CH_KW, v_ch=CH_VW, kvh=NSA_KV, nh=NSA_HPG, blk=0,
                window=NSA_WINDOW, out_rows=rows)
    ob = _flash(yb, yb16, sel_b, nb_p, t_p, q_ch=CH_QB, k_ch=CH_KB, v_ch=CH_VB, kvh=MOBA_HEADS, nh=1,
                blk=MOBA_BLOCK, window=0, out_rows=rows)

    n_pad = x[2] - n_prompt
    ys = ya[n_prompt:]
    ysb = yb[n_prompt:]
    kvc_s = _compress_c2(_compress_c1_sample(cache_cmp, li, page_table, wcat), pe_rows, wcat, w2, gk)
    q_pad = jnp.zeros((nb_s, SUBLANE, NSA_HEADS * HEAD_DIM), F32).at[:, 0].set(ys[:nb_s, :NSA_HEADS * HEAD_DIM])
    oc_s, sel_s = _cattn_sample(q_pad.reshape(nb_s * SUBLANE, -1), kvc_s, nb_s, past_len + 1, past_len)
    oc_s = oc_s.reshape(nb_s, SUBLANE, -1)[:, 0]
    n_sel = (past_len + 1) // NSA_SEL_BLOCK
    k_sel = min(NSA_SEL_COUNT - 1, n_sel)
    mask_a = sel_s[:, :, 0, :n_sel]
    idx_a = jnp.argsort(-mask_a, axis=-1, stable=True)[..., :k_sel].astype(jnp.int32)
    vld_a = (jnp.take_along_axis(mask_a, idx_a, axis=-1) > 0).astype(jnp.int32)
    page = cache_sel.shape[2]
    bpp = page // NSA_SEL_BLOCK
    pt_b = page_table[:, None, :]
    tbl_a = jnp.take_along_axis(jnp.broadcast_to(pt_b, (nb_s, NSA_KV, pt_b.shape[-1])), idx_a // bpp, axis=-1) * bpp + idx_a % bpp
    os_s = _decode_attn(ys, cache_sel, li, tbl_a.astype(jnp.int32), vld_a, q_ch=CH_QA, kn_ch=CH_KS, vn_ch=CH_VS,
                        nh=NSA_HPG, rows=NSA_SEL_BLOCK, kvh=NSA_KV, r_min=0,
                        name="dec_sel")
    win_buf = state_win.shape[2]
    tbl_w = jnp.broadcast_to(jnp.arange(nb_s, dtype=jnp.int32)[:, None, None], (nb_s, NSA_KV, 1))
    ow_s = _decode_attn(ys, state_win, li, tbl_w, jnp.ones_like(tbl_w), q_ch=CH_QA, kn_ch=CH_KW, vn_ch=CH_VW,
                        nh=NSA_HPG, rows=win_buf, kvh=NSA_KV, r_min=max(0, win_buf - (NSA_WINDOW - 1)),
                        name="dec_win")
    means = _moba_means_sample(cache_moba, li, page_table)
    qb_pad = jnp.zeros((nb_s, SUBLANE, MOBA_HEADS * HEAD_DIM), F32).at[:, 0].set(
        ysb[:nb_s, CH_QB * LANE:(CH_QB + MOBA_HEADS) * LANE])
    sel_m = _moba_gate_sample(qb_pad.reshape(nb_s * SUBLANE, -1), means, past_len)
    n_full = means.shape[2]
    k_top = min(MOBA_TOPK, n_full)
    mask_b = sel_m[:, :, 0, :]
    idx_b = jnp.argsort(-mask_b, axis=-1, stable=True)[..., :k_top].astype(jnp.int32)
    vld_b = (jnp.take_along_axis(mask_b, idx_b, axis=-1) > 0).astype(jnp.int32)
    ppb = MOBA_BLOCK // page
    pg_b = (idx_b[..., None] * ppb + jnp.arange(ppb, dtype=jnp.int32)).reshape(nb_s, MOBA_HEADS, k_top * ppb)
    tbl_b = jnp.take_along_axis(jnp.broadcast_to(pt_b, (nb_s, MOBA_HEADS, pt_b.shape[-1])), pg_b, axis=-1)
    vld_b = jnp.repeat(vld_b, ppb, axis=-1)
    ob_s = _decode_attn(ysb, cache_moba, li, tbl_b.astype(jnp.int32), vld_b, q_ch=CH_QB, kn_ch=CH_KB, vn_ch=CH_VB,
                        nh=1, rows=page, kvh=MOBA_HEADS, r_min=0,
                        name="dec_moba")

    def rows_s(part, nh):
        return jnp.pad(part[:, :, :nh].reshape(nb_s, -1), ((0, n_pad - nb_s), (0, 0)))

    sample_parts = (jnp.pad(oc_s, ((0, n_pad - nb_s), (0, 0))), rows_s(os_s, NSA_HPG), rows_s(ow_s, NSA_HPG),
                    rows_s(ob_s, 1))
    mixed = _mix((oc, os_, ow, ob), sample_parts, yg)
    x = _gmm(mixed, [(w_out[None], 0)], d, resid=x, tm=STREAM_TM, tn=1024, name="attn_out")
    hid = _gmm(_rmsnorm(x, ffn_norm), [(w_gate[None], 0), (w_up[None], 0)], w_gate.shape[1], epi="swiglu",
               out_dtype=BF, tm=STREAM_TM, tn=512, name="ffn_up")
    x = _gmm(hid, [(w_down[None], 0)], d, resid=x, tm=STREAM_TM, tn=512, name="ffn_down")

    def rows_of(lo_ch, n_ch, y=ya):
        return y[:, lo_ch * LANE:(lo_ch + n_ch) * LANE]

    def split(a, kvh):
        ap = a[:n_prompt].reshape(nb_p, t_p, 2, kvh, HEAD_DIM)
        as_ = a[n_prompt:n_prompt + nb_s].reshape(nb_s, 1, 2, kvh, HEAD_DIM)
        return ap, as_

    cmp_p, cmp_s = split(rows_of(CH_KC, 4), NSA_KV)
    sel_p, sel_s_rows = split(rows_of(CH_KS, 4), NSA_KV)
    win_rows = rows_of(CH_KW, 4)
    win_s = win_rows[n_prompt:n_prompt + nb_s].reshape(nb_s, 1, 2, NSA_KV, HEAD_DIM)
    moba_p, moba_s = split(rows_of(CH_KB, 16, yb), MOBA_HEADS)
    keep_p = min(NSA_WINDOW, t_p)
    new_win_p = _seq_tails(win_rows, nb_p, t_p, keep_p).reshape(nb_p, keep_p, 2, NSA_KV, HEAD_DIM)
    win_all = jnp.concatenate([state_win[li], win_s], axis=1)
    keep = min(NSA_WINDOW, past_len + 1)
    new_win_s = win_all[:, win_all.shape[1] - keep:]
    return x, (cmp_p, cmp_s, sel_p, sel_s_rows, new_win_p, new_win_s, moba_p, moba_s)


def _conv_layer(x, n_prompt, nb_p, t_p, nb_s, prm, states):
    (conv_norm, w_in, pool_w, pool_scale, dw, dw_b, ln_g, ln_b, pw, w_out, moe_norm, router_w, wg, wu, wd) = prm
    state_pool, state_conv = states
    d = x.shape[1]
    width = pool_w.shape[0] * pool_w.shape[1]
    xn = _rmsnorm(x, conv_norm)
    w3 = w_in[None]
    u_pool = _gmm(xn, [(w3, 0)], width, tm=512, tn=width, name="conv_in_pool")
    u_conv = _gmm(xn, [(w3, width), (w3, 2 * width)], width, epi="glu", tm=512, tn=512, name="conv_in_glu")
    prm_mix = (pool_w.astype(BF), pool_scale[None], jnp.pad(dw, ((0, 1), (0, 0))), dw_b[None], ln_g[None], ln_b[None],
               pw.astype(BF))
    pool_buf = max(POOL_WINDOWS) - 1
    conv_buf = CONV_WIDTH - 1
    hp_rows, hc_rows = 16, 32
    ts = min(256, t_p)
    nt = t_p // ts
    mixed_p = _convmix(u_pool, u_pool, u_conv, u_conv, prm_mix, ts=ts, nt=nt, n_tiles=nb_p * nt, zero_first=True,
                       avail0=0, hp_rows=hp_rows, hc_rows=hc_rows,
                       halo_blk=lambda s, hr: jnp.maximum(s * (ts // hr) - 1, 0))
    rows = x.shape[0]
    us_pool = jnp.zeros((nb_s, SUBLANE, width), F32).at[:, 0].set(u_pool[n_prompt:n_prompt + nb_s])
    us_conv = jnp.zeros((nb_s, SUBLANE, width), F32).at[:, 0].set(u_conv[n_prompt:n_prompt + nb_s])
    hs_pool = jnp.pad(state_pool, ((0, 0), (hp_rows - pool_buf, 0), (0, 0))).reshape(nb_s * hp_rows, width)
    hs_conv = jnp.pad(state_conv, ((0, 0), (hc_rows - conv_buf, 0), (0, 0))).reshape(nb_s * hc_rows, width)
    mixed_s = _convmix(us_pool.reshape(nb_s * SUBLANE, width), hs_pool, us_conv.reshape(nb_s * SUBLANE, width), hs_conv,
                       prm_mix, ts=SUBLANE, nt=1, n_tiles=nb_s, zero_first=False, avail0=pool_buf, hp_rows=hp_rows,
                       hc_rows=hc_rows, halo_blk=lambda s, hr: s)
    mixed_s = mixed_s.reshape(nb_s, SUBLANE, 2 * width)[:, 0]
    mixed = (mixed_p, jnp.pad(mixed_s, ((0, STREAM_TM - nb_s), (0, 0))), rows)
    x = _gmm(mixed, [(w_out[None], 0)], d, resid=x, tm=STREAM_TM, tn=1024, name="conv_out")
    x = _moe(x, moe_norm, router_w, wg, wu, wd, n_prompt)

    assert t_p >= conv_buf and t_p >= pool_buf
    new_pool_p = _seq_tails(u_pool, nb_p, t_p, pool_buf)
    new_conv_p = _seq_tails(u_conv, nb_p, t_p, conv_buf)
    new_pool_s = jnp.concatenate([state_pool, u_pool[n_prompt:n_prompt + nb_s][:, None]], axis=1)[:, 1:]
    new_conv_s = jnp.concatenate([state_conv, u_conv[n_prompt:n_prompt + nb_s][:, None]], axis=1)[:, 1:]
    return x, (new_pool_p, new_pool_s, new_conv_p, new_conv_s)


def kernel(x_prompt, x_sample, cache_nsa_cmp, cache_nsa_sel, state_nsa_win, cache_moba, state_pool, state_conv, page_table, attn_norm, w_attn_in, nsa_qk_norm, nsa_cmp_pe, nsa_cmp_w1, nsa_cmp_w2, moba_qk_norm, w_attn_out, ffn_norm, ffn_w_gate, ffn_w_up, ffn_w_down, conv_norm, w_conv_in, pool_w, pool_scale, conv_dw, conv_dw_b, conv_ln_g, conv_ln_b, conv_pw, w_conv_out, moe_norm, router_w, moe_w_gate, moe_w_up, moe_w_down):
    nb_p, t_p, d = x_prompt.shape
    nb_s = x_sample.shape[0]
    n_prompt = nb_p * t_p
    n_pad = 2 * SUBLANE
    assert x_sample.shape[1] == 1 and nb_s <= n_pad
    x = (x_prompt.reshape(n_prompt, d), jnp.pad(x_sample.reshape(nb_s, d), ((0, STREAM_TM - nb_s), (0, 0))),
         n_prompt + n_pad)
    li = 0
    prm_a = (attn_norm[li], w_attn_in[li], nsa_qk_norm[li], nsa_cmp_pe[li], nsa_cmp_w1[li], nsa_cmp_w2[li],
             moba_qk_norm[li], w_attn_out[li], ffn_norm[li], ffn_w_gate[li], ffn_w_up[li], ffn_w_down[li])
    x, attn_new = _attn_layer(x, n_prompt, nb_p, t_p, nb_s, li, prm_a,
                              (cache_nsa_cmp, cache_nsa_sel, state_nsa_win, cache_moba), page_table)
    prm_c = (conv_norm[li], w_conv_in[li], pool_w[li], pool_scale[li], conv_dw[li], conv_dw_b[li], conv_ln_g[li],
             conv_ln_b[li], conv_pw[li], w_conv_out[li], moe_norm[li], router_w[li], moe_w_gate[li], moe_w_up[li],
             moe_w_down[li])
    x, conv_new = _conv_layer(x, n_prompt, nb_p, t_p, nb_s, prm_c, (state_pool[li], state_conv[li]))
    cmp_p, cmp_s, sel_p, sel_s, win_p, win_s, moba_p, moba_s = attn_new
    pool_p, pool_s, conv_p, conv_s = conv_new
    y_p = x[0].reshape(nb_p, t_p, d)
    y_s = x[1][:nb_s].reshape(nb_s, 1, d)
    st = lambda a: a[None]
    return (y_p, y_s, st(cmp_p), st(cmp_s), st(sel_p), st(sel_s), st(win_p), st(win_s), st(moba_p), st(moba_s),
            st(pool_p), st(pool_s), st(conv_p), st(conv_s))
```

```python
import functools
import math

import jax
import jax.numpy as jnp
from jax import lax
from jax.experimental import pallas as pl
from jax.experimental.pallas import tpu as pltpu

F32 = jnp.float32
BF = jnp.bfloat16
NEG_INF = float("-inf")

HEAD_DIM = 128
NORM_EPS = 1e-6
NSA_HEADS = 8
NSA_KV = 2
NSA_HPG = NSA_HEADS // NSA_KV
NSA_CMP_LEN = 32
NSA_CMP_STRIDE = 16
NSA_CMP_HIDDEN = 2 * HEAD_DIM
NSA_SEL_BLOCK = 64
NSA_SEL_COUNT = 16
NSA_WINDOW = 512
NSA_SEL_FORCE = 1.0e4
MOBA_HEADS = 8
MOBA_BLOCK = 256
MOBA_TOPK = 3
POOL_WINDOWS = (2, 4, 8, 16)
CONV_WIDTH = 31
N_EXPERTS = 8
TOP_K = 2

LANE = 128
SUBLANE = 8
VMEM_LIMIT = 58 * 1024 * 1024
STREAM_TM = 512

CH_QA, CH_KC, CH_VC, CH_KS, CH_VS, CH_KW, CH_VW = 0, 8, 10, 12, 14, 16, 18
NSA_CHUNKS = 20
CH_QB, CH_KB, CH_VB = 0, 8, 16
MOBA_CHUNKS = 24
CH_GATE = 0

NT_DIMS = (((1,), (1,)), ((), ()))


def _cparams(sem):
    return pltpu.CompilerParams(dimension_semantics=sem, vmem_limit_bytes=VMEM_LIMIT)


def _split_bf16(a):
    hi = a.astype(BF)
    lo = (a - hi.astype(F32)).astype(BF)
    return hi, lo


def _dot3(a, b, dims=(((1,), (0,)), ((), ()))):
    ah, al = _split_bf16(a)
    bh, bl = _split_bf16(b)
    d = lambda x, y: lax.dot_general(x, y, dims, preferred_element_type=F32)
    return d(ah, bh) + d(ah, bl) + d(al, bh)


def _masked_softmax(s, mask, axis):
    s = jnp.where(mask, s, NEG_INF)
    m = jnp.max(s, axis=axis, keepdims=True)
    m = jnp.where(m == NEG_INF, 0.0, m)
    p = jnp.exp(s - m)
    d = jnp.sum(p, axis=axis, keepdims=True)
    return p / jnp.where(d > 0, d, 1.0)


def _split_stream(a, tm):
    if not isinstance(a, tuple):
        return [a], [lambda i: i], None, a.shape[0]
    main, tail, m = a
    n_full = main.shape[0] // tm
    assert n_full * tm == main.shape[0] and tail.shape[0] == tm and n_full * tm < m <= (n_full + 1) * tm
    return [main, tail], [lambda i: jnp.minimum(i, n_full - 1), lambda i: 0], n_full, m


def _pick_tile(refs, i, n_full):
    return refs[0][...] if n_full is None else jnp.where(i < n_full, refs[0][...], refs[1][...])


def _rmsnorm_body(*refs, n_full):
    x = _pick_tile(refs[:-2], pl.program_id(0), n_full)
    g_ref, o_ref = refs[-2:]
    ms = jnp.mean(x * x, axis=-1, keepdims=True)
    o_ref[...] = (x * lax.rsqrt(ms + NORM_EPS) * g_ref[...]).astype(o_ref.dtype)


def _rmsnorm(x, g, tm=256):
    arrs, maps, n_full, m = _split_stream(x, tm)
    d = arrs[0].shape[1]
    return pl.pallas_call(
        functools.partial(_rmsnorm_body, n_full=n_full),
        grid=(pl.cdiv(m, tm),),
        in_specs=[pl.BlockSpec((tm, d), lambda i, f=f: (f(i), 0)) for f in maps] + [pl.BlockSpec((1, d), lambda i: (0, 0))],
        out_specs=pl.BlockSpec((tm, d), lambda i: (i, 0)),
        out_shape=jax.ShapeDtypeStruct((m, d), BF),
        compiler_params=_cparams(("parallel",)),
        name="rmsnorm",
    )(*arrs, g.reshape(1, d))


def _gmm_body(texp_ref, tval_ref, *refs, n_x, x_full, n_w, cast_w, w_t, epi, n_r, r_full, tn, n_o):
    mm = (lambda x, w: lax.dot_general(x, w, NT_DIMS, preferred_element_type=F32)) if w_t else (
        lambda x, w: jnp.dot(x, w, preferred_element_type=F32))
    x_refs = refs[:n_x]
    w_refs = refs[n_x:n_x + n_w]
    pos = n_x + n_w
    if epi == "headnorm":
        gain_ref, flag_ref = refs[pos], refs[pos + 1]
        pos += 2
    resid_refs = refs[pos:pos + n_r]
    pos += n_r
    o_refs = refs[pos:pos + n_o]
    wb_refs = refs[pos + n_o:pos + n_o + n_w] if cast_w else w_refs

    i = pl.program_id(1)
    if cast_w:
        changed = jnp.logical_or(i == 0, texp_ref[i] != texp_ref[jnp.maximum(i - 1, 0)])

        @pl.when(changed)
        def _():
            for k in range(n_w):
                wb_refs[k][...] = w_refs[k][...].astype(BF)

    @pl.when(tval_ref[i] == 0)
    def _():
        for o in o_refs:
            o[...] = jnp.zeros(o.shape, o.dtype)

    @pl.when(tval_ref[i] > 0)
    def _():
        x = _pick_tile(x_refs, i, x_full)
        a = mm(x, wb_refs[0][...])
        if epi == "swiglu":
            b = mm(x, wb_refs[1][...])
            y = a * jax.nn.sigmoid(a) * b
        elif epi == "glu":
            b = mm(x, wb_refs[1][...])
            y = a * jax.nn.sigmoid(b)
        else:
            y = a
        if n_r:
            y = y + _pick_tile(resid_refs, i, r_full)
        if epi == "headnorm":
            for c in range(tn // LANE):
                sl = slice(c * LANE, (c + 1) * LANE)
                yc = y[:, sl]
                r = lax.rsqrt(jnp.mean(yc * yc, axis=-1, keepdims=True) + NORM_EPS)
                f = flag_ref[:, sl]
                yn = yc * (f * r + (1.0 - f)) * gain_ref[:, sl]
                for o in o_refs:
                    o[:, sl] = yn.astype(o.dtype)
        else:
            for o in o_refs:
                o[...] = y.astype(o.dtype)


def _gmm(x, ws, n_out, *, epi="none", texp=None, tval=None, resid=None, gain=None, flag=None,
         out_dtype=F32, tm=256, tn=512, w_t=False, name="gmm"):
    x_arrs, x_maps, x_full, m = _split_stream(x, tm)
    kdim = x_arrs[0].shape[1]
    tn = min(tn, n_out)
    n_m = pl.cdiv(m, tm)
    n_n = n_out // tn
    assert n_n * tn == n_out
    if texp is None:
        texp = jnp.zeros((n_m,), jnp.int32)
        tval = jnp.ones((n_m,), jnp.int32)
    n_w = len(ws)
    cast_w = ws[0][0].dtype != BF
    in_specs = [pl.BlockSpec((tm, kdim), lambda j, i, te, tv, f=f: (f(i), 0)) for f in x_maps]
    args = list(x_arrs)
    for w, off in ws:
        assert off % tn == 0 and w.shape[2 if w_t else 1] == kdim and (w.dtype != BF) == cast_w
        ob = off // tn
        if w_t:
            in_specs.append(pl.BlockSpec((None, tn, kdim), lambda j, i, te, tv, ob=ob: (te[i], j + ob, 0)))
        else:
            in_specs.append(pl.BlockSpec((None, kdim, tn), lambda j, i, te, tv, ob=ob: (te[i], 0, j + ob)))
        args.append(w)
    if epi == "headnorm":
        in_specs += [pl.BlockSpec((1, tn), lambda j, i, te, tv: (0, j))] * 2
        args += [gain, flag]
    n_r, r_full = 0, None
    if resid is not None:
        r_arrs, r_maps, r_full, r_m = _split_stream(resid, tm)
        assert r_m == m
        n_r = len(r_arrs)
        in_specs += [pl.BlockSpec((tm, tn), lambda j, i, te, tv, f=f: (f(i), j)) for f in r_maps]
        args += r_arrs
    dtypes = out_dtype if isinstance(out_dtype, tuple) else (out_dtype,)
    body = functools.partial(_gmm_body, n_x=len(x_arrs), x_full=x_full, n_w=n_w, cast_w=cast_w, w_t=w_t, epi=epi,
                             n_r=n_r, r_full=r_full, tn=tn, n_o=len(dtypes))
    outs = pl.pallas_call(
        body,
        grid_spec=pltpu.PrefetchScalarGridSpec(
            num_scalar_prefetch=2,
            grid=(n_n, n_m),
            in_specs=in_specs,
            out_specs=[pl.BlockSpec((tm, tn), lambda j, i, te, tv: (i, j)) for _ in dtypes],
            scratch_shapes=[pltpu.VMEM((tn, kdim) if w_t else (kdim, tn), BF) for _ in range(n_w if cast_w else 0)],
        ),
        out_shape=[jax.ShapeDtypeStruct((m, n_out), dt) for dt in dtypes],
        compiler_params=_cparams(("arbitrary", "arbitrary")),
        name=name,
    )(texp, tval, *args)
    return outs if isinstance(out_dtype, tuple) else outs[0]


def _c1_body(x_ref, w_ref, o_ref):
    nsb = x_ref.shape[0] // NSA_CMP_STRIDE
    acc = jnp.zeros((nsb, 2 * NSA_CMP_HIDDEN), F32)
    for s in range(NSA_CMP_STRIDE):
        xs = x_ref[pl.ds(s, nsb, stride=NSA_CMP_STRIDE), :]
        acc = acc + jnp.dot(xs.astype(BF), w_ref[s], preferred_element_type=F32)
    o_ref[...] = acc


def _compress_c1_prompt(y, wcat, nb, t):
    nsub = t // NSA_CMP_STRIDE
    return pl.pallas_call(
        _c1_body,
        grid=(nb, 2, NSA_KV),
        in_specs=[pl.BlockSpec((t, LANE), lambda b, kv, g: (b, CH_KC + 2 * kv + g)),
                  pl.BlockSpec((None, NSA_CMP_STRIDE, HEAD_DIM, 2 * NSA_CMP_HIDDEN), lambda b, kv, g: (kv, 0, 0, 0))],
        out_specs=pl.BlockSpec((None, None, None, nsub, 2 * NSA_CMP_HIDDEN), lambda b, kv, g: (b, kv, g, 0, 0)),
        out_shape=jax.ShapeDtypeStruct((nb, 2, NSA_KV, nsub, 2 * NSA_CMP_HIDDEN), F32),
        compiler_params=_cparams(("parallel", "parallel", "parallel")),
        name="cmp_c1_prompt",
    )(y, wcat)


C1_PAGES = 16


def _c1_sample_body(pt_ref, *refs, n_in, page):
    x_refs = refs[:n_in]
    w_ref, o_ref, scr = refs[n_in:n_in + 3]
    nsb = page // NSA_CMP_STRIDE
    rows = n_in * nsb
    for kv in range(2):
        acc = jnp.zeros((rows * NSA_KV, 2 * NSA_CMP_HIDDEN), F32)

        def rows_at(s):
            parts = [xr[pl.ds(s, nsb, stride=NSA_CMP_STRIDE), kv, :, :].reshape(nsb * NSA_KV, HEAD_DIM) for xr in x_refs]
            return (parts[0] if n_in == 1 else jnp.concatenate(parts, axis=0)).astype(BF)

        for s in range(0, NSA_CMP_STRIDE, 2):
            xs = jnp.concatenate([rows_at(s), rows_at(s + 1)], axis=1)
            w = w_ref[kv, s:s + 2].reshape(2 * HEAD_DIM, 2 * NSA_CMP_HIDDEN)
            acc = acc + jnp.dot(xs, w, preferred_element_type=F32)
        for c in range(scr.shape[0]):
            scr[c] = acc[:, c * LANE:(c + 1) * LANE]
        for g in range(NSA_KV):
            for c in range(scr.shape[0]):
                o_ref[kv, g, :, c * LANE:(c + 1) * LANE] = scr[c, pl.ds(g, rows, stride=NSA_KV), :]


def _compress_c1_sample(cache, li, page_table, wcat):
    nb, n_pages = page_table.shape
    page = cache.shape[2]
    npp = math.gcd(C1_PAGES, n_pages)
    nsb = page // NSA_CMP_STRIDE
    nsub = n_pages * nsb
    body = functools.partial(_c1_sample_body, n_in=npp, page=page)
    in_specs = [
        pl.BlockSpec((None, None, page, 2, NSA_KV, HEAD_DIM), lambda b, j, pt, i=i: (li, pt[b, j * npp + i], 0, 0, 0, 0))
        for i in range(npp)
    ]
    in_specs.append(pl.BlockSpec(wcat.shape, lambda b, j, pt: (0, 0, 0, 0)))
    return pl.pallas_call(
        body,
        grid_spec=pltpu.PrefetchScalarGridSpec(
            num_scalar_prefetch=1,
            grid=(nb, n_pages // npp),
            in_specs=in_specs,
            out_specs=pl.BlockSpec((None, 2, NSA_KV, npp * nsb, 2 * NSA_CMP_HIDDEN), lambda b, j, pt: (b, 0, 0, j, 0)),
            scratch_shapes=[pltpu.VMEM((2 * NSA_CMP_HIDDEN // LANE, npp * nsb * NSA_KV, LANE), F32)],
        ),
        out_shape=jax.ShapeDtypeStruct((nb, 2, NSA_KV, nsub, 2 * NSA_CMP_HIDDEN), F32),
        compiler_params=_cparams(("parallel", "parallel")),
        name="cmp_c1_sample",
    )(page_table, *([cache] * npp), wcat)


def _c2_body(p_ref, pe_ref, wcat_ref, w2_ref, g_ref, o_ref, *, nsub, n_c):
    kv = pl.program_id(1)
    pep = jnp.dot(pe_ref[...], wcat_ref[...].reshape(NSA_CMP_STRIDE * HEAD_DIM, 2 * NSA_CMP_HIDDEN),
                  preferred_element_type=F32)
    bias = pep[0:1, :NSA_CMP_HIDDEN] + pep[1:2, NSA_CMP_HIDDEN:]
    nxt = pltpu.roll(p_ref[:, NSA_CMP_HIDDEN:], nsub - 1, 0)
    h = p_ref[:, :NSA_CMP_HIDDEN] + nxt + bias
    h = h * jax.nn.sigmoid(h)
    o = jnp.dot(h.astype(BF), w2_ref[...], preferred_element_type=F32)
    r = lax.rsqrt(jnp.mean(o * o, axis=-1, keepdims=True) + NORM_EPS)
    o = jnp.where(kv == 0, o * r * g_ref[...], o)
    row = lax.broadcasted_iota(jnp.int32, (nsub, 1), 0)
    o_ref[...] = jnp.where(row < n_c, o, 0.0)


def _compress_c2(p, pe_rows, wcat, w2, gk):
    nb, _, _, nsub, _ = p.shape
    n_c = nsub - NSA_CMP_LEN // NSA_CMP_STRIDE + 1
    body = functools.partial(_c2_body, nsub=nsub, n_c=n_c)
    return pl.pallas_call(
        body,
        grid=(nb, 2, NSA_KV),
        in_specs=[pl.BlockSpec((None, None, None, nsub, 2 * NSA_CMP_HIDDEN), lambda b, kv, g: (b, kv, g, 0, 0)),
                  pl.BlockSpec((None, SUBLANE, NSA_CMP_STRIDE * HEAD_DIM), lambda b, kv, g: (kv, 0, 0)),
                  pl.BlockSpec((None, NSA_CMP_STRIDE, HEAD_DIM, 2 * NSA_CMP_HIDDEN), lambda b, kv, g: (kv, 0, 0, 0)),
                  pl.BlockSpec((None, NSA_CMP_HIDDEN, HEAD_DIM), lambda b, kv, g: (kv, 0, 0)),
                  pl.BlockSpec((1, HEAD_DIM), lambda b, kv, g: (0, 0))],
        out_specs=pl.BlockSpec((None, None, None, nsub, HEAD_DIM), lambda b, kv, g: (b, kv, g, 0, 0)),
        out_shape=jax.ShapeDtypeStruct((nb, 2, NSA_KV, nsub, HEAD_DIM), F32),
        compiler_params=_cparams(("parallel", "parallel", "parallel")),
        name="cmp_c2",
    )(p, pe_rows, wcat, w2, gk)


def _rank_select(sc_ref, n_iter, k):
    score = sc_ref[...]
    brow = lax.broadcasted_iota(jnp.int32, score.shape, 0)

    def body(j, rank):
        r = sc_ref[pl.ds(j, 1), :]
        beats = jnp.logical_or(r > score, jnp.logical_and(r == score, j < brow))
        return rank + jnp.where(beats, 1.0, 0.0)

    rank = lax.fori_loop(0, n_iter, body, jnp.zeros(score.shape, F32))
    return rank < k


def _cattn_body(q_ref, kc_ref, vc_ref, oc_ref, sel_ref, sc_ref, *, tq, ncp, n_c, ns, nsp, pos0, ksel):
    t0 = pos0 + pl.program_id(2) * tq
    scale = HEAD_DIM ** -0.5
    kc = kc_ref[...].astype(BF)
    vc = vc_ref[...].astype(BF)
    trow = t0 + lax.broadcasted_iota(jnp.int32, (tq, 1), 0)
    ncol = lax.broadcasted_iota(jnp.int32, (1, ncp), 1)
    valid = jnp.logical_and(ncol * NSA_CMP_STRIDE + (NSA_CMP_LEN - 1) <= trow, ncol < n_c)
    tcol = t0 + lax.broadcasted_iota(jnp.int32, (1, tq), 1)
    nrow = lax.broadcasted_iota(jnp.int32, (ncp, 1), 0)
    valid_t = jnp.logical_and(nrow * NSA_CMP_STRIDE + (NSA_CMP_LEN - 1) <= tcol, nrow < n_c)
    psum_t = jnp.zeros((ncp, tq), F32)
    for z in range(NSA_HPG):
        sl = slice(z * HEAD_DIM, (z + 1) * HEAD_DIM)
        q = (q_ref[:, sl] * scale).astype(BF)
        s = lax.dot_general(q, kc, NT_DIMS, preferred_element_type=F32)
        p = _masked_softmax(s, valid, -1)
        oc_ref[:, sl] = jnp.dot(p.astype(BF), vc, preferred_element_type=F32)
        s_t = lax.dot_general(kc, q, NT_DIMS, preferred_element_type=F32)
        psum_t = psum_t + _masked_softmax(s_t, valid_t, 0)
    r = NSA_SEL_BLOCK // NSA_CMP_STRIDE
    brow = lax.broadcasted_iota(jnp.int32, (nsp, 1), 0)
    lo = r * brow - 1
    inside = jnp.logical_and(ncol >= lo, ncol <= lo + r)
    edge = jnp.logical_or(ncol == lo, ncol == lo + r)
    m_t = jnp.where(inside, jnp.where(edge, 0.5, 1.0), 0.0).astype(BF)
    hi = psum_t.astype(BF)
    mid = (psum_t - hi.astype(F32)).astype(BF)
    low = (psum_t - hi.astype(F32) - mid.astype(F32)).astype(BF)
    imp_t = (jnp.dot(m_t, hi, preferred_element_type=F32) + jnp.dot(m_t, mid, preferred_element_type=F32)
             + jnp.dot(m_t, low, preferred_element_type=F32))
    own = tcol // NSA_SEL_BLOCK
    past = jnp.logical_and(brow < own, brow < ns)
    forced = jnp.logical_or(brow == 0, brow == own - 1)
    sc_ref[...] = jnp.where(past, jnp.where(forced, NSA_SEL_FORCE, imp_t), NEG_INF)
    n_live = jnp.minimum(ns, (t0 + tq - 1) // NSA_SEL_BLOCK)
    chosen = jnp.logical_and(past, _rank_select(sc_ref, n_live, ksel))
    sel = jnp.where(jnp.logical_or(chosen, brow == own), 1.0, 0.0)
    rows = sel_ref.shape[0]
    if rows > nsp:
        sel_ref[...] = jnp.zeros(sel_ref.shape, F32)
    sel_ref[0:nsp, :] = sel


def _cattn(q_arr, q_row_blk0, q_col_blk0, kvc, nb, tlen, tq, length, pos0, out_rows):
    nq = tlen // tq
    ncp = kvc.shape[3]
    n_c = ncp - NSA_CMP_LEN // NSA_CMP_STRIDE + 1
    ns = length // NSA_SEL_BLOCK
    ksel = min(NSA_SEL_COUNT - 1, ns)
    nsp = -(-ns // SUBLANE) * SUBLANE
    sel_rows = max(nsp, LANE)
    body = functools.partial(_cattn_body, tq=tq, ncp=ncp, n_c=n_c, ns=ns, nsp=nsp, pos0=pos0, ksel=ksel)
    gw = NSA_HPG * HEAD_DIM
    return pl.pallas_call(
        body,
        grid=(nb, NSA_KV, nq),
        in_specs=[pl.BlockSpec((tq, gw), lambda b, g, qi: (q_row_blk0 + b * nq + qi, q_col_blk0 + g)),
                  pl.BlockSpec((None, None, None, ncp, HEAD_DIM), lambda b, g, qi: (b, 0, g, 0, 0)),
                  pl.BlockSpec((None, None, None, ncp, HEAD_DIM), lambda b, g, qi: (b, 1, g, 0, 0))],
        out_specs=[pl.BlockSpec((tq, gw), lambda b, g, qi: (b * nq + qi, g)),
                   pl.BlockSpec((None, None, sel_rows, tq), lambda b, g, qi: (b, g, 0, qi))],
        out_shape=[jax.ShapeDtypeStruct((out_rows, NSA_KV * gw), F32),
                   jax.ShapeDtypeStruct((nb, NSA_KV, sel_rows, tlen), F32)],
        scratch_shapes=[pltpu.VMEM((nsp, tq), F32)],
        compiler_params=_cparams(("parallel", "parallel", "parallel")),
        name="nsa_cmp_attn",
    )(q_arr, kvc, kvc)


def _cattn_row_body(q_ref, kc_ref, vc_ref, oc_ref, sel_ref, *, tq, ncp, n_c, ns, nsl, pos0, ksel):
    scale = HEAD_DIM ** -0.5
    kc = kc_ref[...].astype(BF)
    vc = vc_ref[...].astype(BF)
    trow = pos0 + lax.broadcasted_iota(jnp.int32, (tq, 1), 0)
    ncol = lax.broadcasted_iota(jnp.int32, (1, ncp), 1)
    valid = jnp.logical_and(ncol * NSA_CMP_STRIDE + (NSA_CMP_LEN - 1) <= trow, ncol < n_c)
    psum = jnp.zeros((tq, ncp), F32)
    for z in range(NSA_HPG):
        sl = slice(z * HEAD_DIM, (z + 1) * HEAD_DIM)
        q = (q_ref[:, sl] * scale).astype(BF)
        p = _masked_softmax(lax.dot_general(q, kc, NT_DIMS, preferred_element_type=F32), valid, -1)
        oc_ref[:, sl] = jnp.dot(p.astype(BF), vc, preferred_element_type=F32)
        psum = psum + p
    r = NSA_SEL_BLOCK // NSA_CMP_STRIDE
    nrow = lax.broadcasted_iota(jnp.int32, (ncp, 1), 0)
    bcol = lax.broadcasted_iota(jnp.int32, (1, nsl), 1)
    lo = r * bcol - 1
    inside = jnp.logical_and(nrow >= lo, nrow <= lo + r)
    edge = jnp.logical_or(nrow == lo, nrow == lo + r)
    m = jnp.where(inside, jnp.where(edge, 0.5, 1.0), 0.0).astype(BF)
    hi = psum.astype(BF)
    mid = (psum - hi.astype(F32)).astype(BF)
    low = (psum - hi.astype(F32) - mid.astype(F32)).astype(BF)
    imp = (jnp.dot(hi, m, preferred_element_type=F32) + jnp.dot(mid, m, preferred_element_type=F32)
           + jnp.dot(low, m, preferred_element_type=F32))
    own = trow // NSA_SEL_BLOCK
    past = jnp.logical_and(bcol < own, bcol < ns)
    forced = jnp.logical_or(bcol == 0, bcol == own - 1)
    score = jnp.where(past, jnp.where(forced, NSA_SEL_FORCE, imp), NEG_INF)
    rank = jnp.zeros((tq, nsl), F32)
    for j in range(ns):
        cj = score[:, j:j + 1]
        beats = jnp.logical_or(cj > score, jnp.logical_and(cj == score, j < bcol))
        rank = rank + jnp.where(beats, 1.0, 0.0)
    chosen = jnp.logical_and(past, rank < ksel)
    sel_ref[...] = jnp.where(jnp.logical_or(chosen, bcol == own), 1.0, 0.0)


def _cattn_sample(q_pad, kvc, nb, length, pos0):
    tq = SUBLANE
    ncp = kvc.shape[3]
    n_c = ncp - NSA_CMP_LEN // NSA_CMP_STRIDE + 1
    ns = length // NSA_SEL_BLOCK
    ksel = min(NSA_SEL_COUNT - 1, ns)
    nsl = -(-ns // LANE) * LANE
    body = functools.partial(_cattn_row_body, tq=tq, ncp=ncp, n_c=n_c, ns=ns, nsl=nsl, pos0=pos0, ksel=ksel)
    gw = NSA_HPG * HEAD_DIM
    return pl.pallas_call(
        body,
        grid=(nb, NSA_KV),
        in_specs=[pl.BlockSpec((tq, gw), lambda b, g: (b, g)),
                  pl.BlockSpec((None, None, None, ncp, HEAD_DIM), lambda b, g: (b, 0, g, 0, 0)),
                  pl.BlockSpec((None, None, None, ncp, HEAD_DIM), lambda b, g: (b, 1, g, 0, 0))],
        out_specs=[pl.BlockSpec((tq, gw), lambda b, g: (b, g)),
                   pl.BlockSpec((None, None, tq, nsl), lambda b, g: (b, g, 0, 0))],
        out_shape=[jax.ShapeDtypeStruct((nb * tq, NSA_KV * gw), F32),
                   jax.ShapeDtypeStruct((nb, NSA_KV, tq, nsl), F32)],
        compiler_params=_cparams(("parallel", "parallel")),
        name="nsa_cmp_attn_sample",
    )(q_pad, kvc, kvc)


def _gate_body(q_ref, k_ref, sel_ref, mean_ref, sc_ref, *, tq, nb, nbp, pos0):
    qi = pl.program_id(2)

    @pl.when(qi == 0)
    def _():
        mean_ref[...] = jnp.zeros(mean_ref.shape, F32)
        for j in range(nb):
            blk = k_ref[j * MOBA_BLOCK:(j + 1) * MOBA_BLOCK, :]
            mean_ref[j:j + 1, :] = jnp.sum(blk, axis=0, keepdims=True) * (1.0 / MOBA_BLOCK)

    g_t = _dot3(mean_ref[...], q_ref[...], NT_DIMS)
    tcol = pos0 + qi * tq + lax.broadcasted_iota(jnp.int32, (1, tq), 1)
    own = tcol // MOBA_BLOCK
    brow = lax.broadcasted_iota(jnp.int32, (nbp, 1), 0)
    past = jnp.logical_and(brow < own, brow < nb)
    sc_ref[...] = jnp.where(past, g_t, NEG_INF)
    chosen = jnp.logical_and(past, _rank_select(sc_ref, nb, min(MOBA_TOPK, nb)))
    sel = jnp.where(jnp.logical_or(chosen, brow == own), 1.0, 0.0)
    if sel_ref.shape[0] > nbp:
        sel_ref[...] = jnp.zeros(sel_ref.shape, F32)
    sel_ref[0:nbp, :] = sel


def _moba_gate_prompt(y, nb, t, tq):
    nq = t // tq
    n_full = t // MOBA_BLOCK
    nbp = -(-n_full // SUBLANE) * SUBLANE
    body = functools.partial(_gate_body, tq=tq, nb=n_full, nbp=nbp, pos0=0)
    return pl.pallas_call(
        body,
        grid=(nb, MOBA_HEADS, nq),
        in_specs=[pl.BlockSpec((tq, LANE), lambda b, h, qi: (b * nq + qi, CH_QB + h)),
                  pl.BlockSpec((t, LANE), lambda b, h, qi: (b, CH_KB + h))],
        out_specs=pl.BlockSpec((None, None, LANE, tq), lambda b, h, qi: (b, h, 0, qi)),
        out_shape=jax.ShapeDtypeStruct((nb, MOBA_HEADS, LANE, t), F32),
        scratch_shapes=[pltpu.VMEM((nbp, HEAD_DIM), F32), pltpu.VMEM((nbp, tq), F32)],
        compiler_params=_cparams(("parallel", "parallel", "arbitrary")),
        name="moba_gate_prompt",
    )(y, y)


def _gate_sample_body(q_ref, m_ref, sel_ref, *, nb, pos0):
    tq = q_ref.shape[0]
    own = (pos0 + lax.broadcasted_iota(jnp.int32, (tq, 1), 0)) // MOBA_BLOCK
    bcol = lax.broadcasted_iota(jnp.int32, (1, nb), 1)
    past = bcol < own
    for h in range(MOBA_HEADS):
        g = _dot3(q_ref[:, h * HEAD_DIM:(h + 1) * HEAD_DIM], m_ref[h], NT_DIMS)
        score = jnp.where(past, g, NEG_INF)
        rank = jnp.zeros((tq, nb), F32)
        for j in range(nb):
            cj = score[:, j:j + 1]
            beats = jnp.logical_or(cj > score, jnp.logical_and(cj == score, j < bcol))
            rank = rank + jnp.where(beats, 1.0, 0.0)
        chosen = jnp.logical_and(past, rank < min(MOBA_TOPK, nb))
        sel_ref[h] = jnp.where(jnp.logical_or(chosen, bcol == own), 1.0, 0.0)


def _moba_gate_sample(q_pad, means, pos0):
    nb, _, n_full, _ = means.shape
    tq = SUBLANE
    body = functools.partial(_gate_sample_body, nb=n_full, pos0=pos0)
    return pl.pallas_call(
        body,
        grid=(nb,),
        in_specs=[pl.BlockSpec((tq, MOBA_HEADS * HEAD_DIM), lambda b: (b, 0)),
                  pl.BlockSpec((None, MOBA_HEADS, n_full, HEAD_DIM), lambda b: (b, 0, 0, 0))],
        out_specs=pl.BlockSpec((None, MOBA_HEADS, tq, n_full), lambda b: (b, 0, 0, 0)),
        out_shape=jax.ShapeDtypeStruct((nb, MOBA_HEADS, tq, n_full), F32),
        compiler_params=_cparams(("parallel",)),
        name="moba_gate_sample",
    )(q_pad, means)


MEANS_BLOCKS = 4


def _means_body(pt_ref, *refs, ppb, bps):
    c_refs = refs[:bps * ppb]
    o_ref = refs[bps * ppb]
    j = pl.program_id(1)
    for q in range(bps):
        acc = jnp.sum(c_refs[q * ppb][...], axis=0)
        for i in range(1, ppb):
            acc = acc + jnp.sum(c_refs[q * ppb + i][...], axis=0)
        m = acc * (1.0 / MOBA_BLOCK)
        for h in range(MOBA_HEADS):
            o_ref[h, pl.ds(j * bps + q, 1), :] = m[h:h + 1, :]


def _moba_means_sample(cache, li, page_table):
    nb, n_pages = page_table.shape
    page = cache.shape[2]
    ppb = MOBA_BLOCK // page
    n_full = n_pages // ppb
    bps = math.gcd(MEANS_BLOCKS, n_full)
    body = functools.partial(_means_body, ppb=ppb, bps=bps)
    in_specs = [
        pl.BlockSpec((None, None, page, None, MOBA_HEADS, HEAD_DIM),
                     lambda b, j, pt, i=i: (li, pt[b, j * bps * ppb + i], 0, 0, 0, 0))
        for i in range(bps * ppb)
    ]
    return pl.pallas_call(
        body,
        grid_spec=pltpu.PrefetchScalarGridSpec(
            num_scalar_prefetch=1,
            grid=(nb, n_full // bps),
            in_specs=in_specs,
            out_specs=pl.BlockSpec((None, MOBA_HEADS, n_full, HEAD_DIM), lambda b, j, pt: (b, 0, 0, 0)),
        ),
        out_shape=jax.ShapeDtypeStruct((nb, MOBA_HEADS, n_full, HEAD_DIM), F32),
        compiler_params=_cparams(("parallel", "arbitrary")),
        name="moba_means_sample",
    )(page_table, *([cache] * (bps * ppb)))


FLASH_UNIT_ROWS = 256
LOG2_E = 1.4426950408889634
MASK_BIG = 2.0 ** 60


def _flash_body(*refs, tq, tk, nh, blk, window, masked, ur):
    q_ref, k_ref, v_ref = refs[:3]
    pos = 3
    if masked:
        sel_ref = refs[pos]
        pos += 1
    o_ref, qs_ref, kt_ref = refs[pos:pos + 3]
    pos += 1
    units = [(z, r0) for z in range(nh) for r0 in range(0, tq, ur)]
    stat = refs[pos + 2:pos + 2 + 2 * len(units)]
    m_refs, acc_refs = stat[0::2], stat[1::2]
    qi = pl.program_id(2)
    t0 = qi * tq
    scale = HEAD_DIM ** -0.5 * LOG2_E
    if masked:
        bias = ((jnp.transpose(sel_ref[...]) - 1.0) * MASK_BIG).astype(BF)
    for z in range(nh):
        qz = (q_ref[:, z * HEAD_DIM:(z + 1) * HEAD_DIM] * scale).astype(BF)
        qs_ref[z] = jnp.concatenate([qz, bias], axis=1) if masked else qz
    for u in range(len(units)):
        m_refs[u][...] = jnp.full(m_refs[u].shape, NEG_INF, F32)
        acc_refs[u][...] = jnp.zeros(acc_refs[u].shape, F32)
    trow = t0 + lax.broadcasted_iota(jnp.int32, (tq, 1), 0)

    @pl.when(qi == 0)
    def _():
        for c in range(kt_ref.shape[0]):
            kc = k_ref[c * tk:(c + 1) * tk, :]
            if masked:
                prow = c * tk + lax.broadcasted_iota(jnp.int32, (tk, 1), 0)
                jcol = lax.broadcasted_iota(jnp.int32, (1, LANE), 1)
                kc = jnp.concatenate([kc, jnp.where(prow // blk == jcol, 1.0, 0.0).astype(BF)], axis=1)
            kt_ref[c] = kc.T

    def process(c, edge):
        ks = pl.multiple_of(c * tk, tk)
        kt = kt_ref[c]
        vc = jnp.concatenate([v_ref[pl.ds(ks, tk), :], jnp.ones((tk, LANE), BF)], axis=1)
        if edge:
            pcol = ks + lax.broadcasted_iota(jnp.int32, (1, tk), 1)
            valid = pcol <= trow
            if window:
                valid = jnp.logical_and(valid, pcol > trow - window)
        for u, (z, r0) in enumerate(units):
            s = jnp.dot(qs_ref[z, r0:r0 + ur, :], kt, preferred_element_type=F32)
            m_old = m_refs[u][...]
            if edge:
                s = jnp.where(valid[r0:r0 + ur], s, NEG_INF)
                m_new = jnp.maximum(m_old, jnp.max(s, axis=-1, keepdims=True))
                m_use = jnp.where(m_new == NEG_INF, 0.0, m_new)
            else:
                m_new = jnp.maximum(m_old, jnp.max(s, axis=-1, keepdims=True))
                m_use = m_new
            alpha = jnp.exp2(m_old - m_use)
            p = jnp.exp2(s - m_use)
            acc_refs[u][...] = alpha * acc_refs[u][...] + jnp.dot(p.astype(BF), vc, preferred_element_type=F32)
            m_refs[u][...] = m_new

    if masked:
        def pair(c2, carry):
            process(2 * c2, False)
            process(2 * c2 + 1, False)
            return carry

        lax.fori_loop(0, qi // 2, pair, 0)

        @pl.when(qi % 2 == 1)
        def _():
            process(qi - 1, False)

        process(qi, True)
    else:
        c_hi = (t0 + tq + tk - 1) // tk
        c_lo = jnp.maximum(t0 - window + 1, 0) // tk
        lax.fori_loop(c_lo, c_hi, lambda c, carry: (process(c, True), carry)[1], 0)
    for u, (z, r0) in enumerate(units):
        l = acc_refs[u][:, HEAD_DIM:HEAD_DIM + 1]
        o_ref[r0:r0 + ur, z * HEAD_DIM:(z + 1) * HEAD_DIM] = acc_refs[u][:, :HEAD_DIM] / jnp.where(l > 0, l, 1.0)


def _flash(y, y16, sel, nb, t, *, q_ch, k_ch, v_ch, kvh, nh, blk, window, out_rows, tq=512, tk=512):
    tq = min(tq, t)
    tk = min(tk, t)
    nq = t // tq
    masked = sel is not None
    gw = nh * HEAD_DIM
    assert q_ch % nh == 0 and (tq == tk or not masked)
    in_specs = [pl.BlockSpec((tq, gw), lambda b, h, qi: (b * nq + qi, q_ch // nh + h)),
                pl.BlockSpec((t, LANE), lambda b, h, qi: (b, k_ch + h)),
                pl.BlockSpec((t, LANE), lambda b, h, qi: (b, v_ch + h))]
    args = [y, y16, y16]
    if masked:
        in_specs.append(pl.BlockSpec((None, None, LANE, tq), lambda b, h, qi: (b, h, 0, qi)))
        args.append(sel)
    ur = min(FLASH_UNIT_ROWS, tq)
    n_units = nh * (tq // ur)
    kd = (2 if masked else 1) * HEAD_DIM
    body = functools.partial(_flash_body, tq=tq, tk=tk, nh=nh, blk=blk, window=window, masked=masked, ur=ur)
    unit_scratch = [pltpu.VMEM((ur, 1), F32), pltpu.VMEM((ur, 2 * HEAD_DIM), F32)]
    return pl.pallas_call(
        body,
        grid=(nb, kvh, nq),
        in_specs=in_specs,
        out_specs=pl.BlockSpec((tq, gw), lambda b, h, qi: (b * nq + qi, h)),
        out_shape=jax.ShapeDtypeStruct((out_rows, kvh * gw), F32),
        scratch_shapes=[pltpu.VMEM((nh, tq, kd), BF), pltpu.VMEM((t // tk, kd, tk), BF)] + unit_scratch * n_units,
        compiler_params=_cparams(("parallel", "parallel", "arbitrary")),
        name="flash_" + ("win" if window else "blk%d" % blk),
    )(*args)


def _dec_body(tbl_ref, vld_ref, q_ref, *refs, nh, rows, kvh, nblk, r_min):
    k_refs = refs[:nblk]
    v_refs = refs[nblk:2 * nblk]
    kn_ref, vn_ref, o_ref = refs[2 * nblk:2 * nblk + 3]
    b = pl.program_id(0)
    h = pl.program_id(1)
    scale = HEAD_DIM ** -0.5
    qrow = q_ref[pl.ds(b, 1), :] * scale
    zrow = lax.broadcasted_iota(jnp.int32, (SUBLANE, 1), 0)
    qm = jnp.zeros((SUBLANE, HEAD_DIM), F32)
    for z in range(nh):
        qm = jnp.where(zrow == z, qrow[:, z * HEAD_DIM:(z + 1) * HEAD_DIM], qm)
    qb = qm.astype(BF)
    col = lax.broadcasted_iota(jnp.int32, (1, rows * kvh), 1)
    valid = jnp.logical_and(col % kvh == h, col // kvh >= r_min)
    kn = kn_ref[pl.ds(b, 1), :]
    vn = vn_ref[pl.ds(b, 1), :]
    s_new = jnp.sum(qm * kn, axis=-1, keepdims=True)
    scores = []
    m = s_new
    for i in range(nblk):
        kk = k_refs[i][...].reshape(rows * kvh, HEAD_DIM).astype(BF)
        s = lax.dot_general(qb, kk, NT_DIMS, preferred_element_type=F32)
        s = jnp.where(jnp.logical_and(valid, vld_ref[b, h, i] > 0), s, NEG_INF)
        scores.append(s)
        m = jnp.maximum(m, jnp.max(s, axis=-1, keepdims=True))
    p_new = jnp.exp(s_new - m)
    l = p_new
    acc = p_new * vn
    for i in range(nblk):
        p = jnp.exp(scores[i] - m)
        l = l + jnp.sum(p, axis=-1, keepdims=True)
        vv = v_refs[i][...].reshape(rows * kvh, HEAD_DIM).astype(BF)
        acc = acc + jnp.dot(p.astype(BF), vv, preferred_element_type=F32)
    o_ref[...] = acc / l


def _decode_attn(ys, cache, li_fixed, tbl, vld, *, q_ch, kn_ch, vn_ch, nh, rows, kvh, r_min, name):
    nb, _, nblk = tbl.shape
    page = cache.shape[2]
    bpp = page // rows
    gw = nh * HEAD_DIM
    body = functools.partial(_dec_body, nh=nh, rows=rows, kvh=kvh, nblk=nblk, r_min=r_min)

    def kv_spec(kv, i):
        def imap(b, h, tb, vl):
            t = tb[b, h, i]
            return (li_fixed, t // bpp, t % bpp, kv, 0, 0)

        return pl.BlockSpec((None, None, rows, None, kvh, HEAD_DIM), imap)

    nrow = ys.shape[0]
    return pl.pallas_call(
        body,
        grid_spec=pltpu.PrefetchScalarGridSpec(
            num_scalar_prefetch=2,
            grid=(nb, kvh),
            in_specs=[pl.BlockSpec((nrow, gw), lambda b, h, tb, vl: (0, q_ch // nh + h))]
            + [kv_spec(0, i) for i in range(nblk)] + [kv_spec(1, i) for i in range(nblk)]
            + [pl.BlockSpec((nrow, LANE), lambda b, h, tb, vl: (0, kn_ch + h)),
               pl.BlockSpec((nrow, LANE), lambda b, h, tb, vl: (0, vn_ch + h))],
            out_specs=pl.BlockSpec((None, None, SUBLANE, HEAD_DIM), lambda b, h, tb, vl: (b, h, 0, 0)),
        ),
        out_shape=jax.ShapeDtypeStruct((nb, kvh, SUBLANE, HEAD_DIM), F32),
        compiler_params=_cparams(("parallel", "parallel")),
        name=name,
    )(tbl, vld, ys, *([cache] * (2 * nblk)), ys, ys)


def _mix_body(oc_ref, os_ref, ow_ref, ob_ref, soc_ref, sos_ref, sow_ref, sob_ref, gate_ref, o_ref, *, n_p_tiles, n_s):
    i = pl.program_id(0)
    w = NSA_HEADS * HEAD_DIM

    def emit(oc, os_, ow, ob, rows):
        g = jax.nn.sigmoid(gate_ref[0:rows, :])
        for h in range(NSA_HEADS):
            sl = slice(h * HEAD_DIM, (h + 1) * HEAD_DIM)
            o = (g[:, 3 * h:3 * h + 1] * oc[:, sl] + g[:, 3 * h + 1:3 * h + 2] * os_[:, sl]
                 + g[:, 3 * h + 2:3 * h + 3] * ow[:, sl])
            o_ref[0:rows, sl] = o.astype(o_ref.dtype)
        o_ref[0:rows, w:] = ob[...].astype(o_ref.dtype)

    @pl.when(i < n_p_tiles)
    def _():
        emit(oc_ref, os_ref, ow_ref, ob_ref, o_ref.shape[0])

    @pl.when(i >= n_p_tiles)
    def _():
        emit(soc_ref, sos_ref, sow_ref, sob_ref, n_s)


def _mix(prompt_parts, sample_parts, y, tm=256):
    n_prompt = prompt_parts[0].shape[0]
    n_s = sample_parts[0].shape[0]
    m = y.shape[0]
    tm = min(tm, n_prompt)
    assert n_prompt % tm == 0 and m == n_prompt + n_s and n_s <= tm
    n_p_tiles = n_prompt // tm
    wa = NSA_HEADS * HEAD_DIM
    wb = MOBA_HEADS * HEAD_DIM
    pmap = lambda i: (jnp.minimum(i, n_p_tiles - 1), 0)
    body = functools.partial(_mix_body, n_p_tiles=n_p_tiles, n_s=n_s)
    return pl.pallas_call(
        body,
        grid=(n_p_tiles + 1,),
        in_specs=[pl.BlockSpec((tm, wa), pmap)] * 3 + [pl.BlockSpec((tm, wb), pmap)]
        + [pl.BlockSpec((n_s, wa), lambda i: (0, 0))] * 3 + [pl.BlockSpec((n_s, wb), lambda i: (0, 0))]
        + [pl.BlockSpec((tm, LANE), lambda i: (i, CH_GATE))],
        out_specs=pl.BlockSpec((tm, wa + wb), lambda i: (i, 0)),
        out_shape=jax.ShapeDtypeStruct((m, wa + wb), BF),
        compiler_params=_cparams(("parallel",)),
        name="attn_mix",
    )(*prompt_parts, *sample_parts, y)


def _convmix_body(up_ref, hp_ref, uc_ref, hc_ref, pw_ref, ps_ref, dw_ref, dwb_ref, lng_ref, lnb_ref, pww_ref,
                  o_ref, zp_ref, zs_ref, cb_ref, *, ts, nt, zero_first, avail0, rc):
    s = pl.program_id(0)
    hp_rows = hp_ref.shape[0]
    hc_rows = hc_ref.shape[0]
    width = up_ref.shape[1]
    if zero_first:
        keep = jnp.where(s % nt == 0, 0.0, 1.0)
        zp_ref[0:hp_rows, :] = hp_ref[...] * keep
        zs_ref[0, 0:hc_rows, :] = hc_ref[...] * keep
        avail = (s % nt) * ts + avail0
    else:
        zp_ref[0:hp_rows, :] = hp_ref[...]
        zs_ref[0, 0:hc_rows, :] = hc_ref[...]
        avail = avail0
    zp_ref[hp_rows:, :] = up_ref[...]
    zs_ref[0, hc_rows:, :] = uc_ref[...]
    span = hc_rows + ts - SUBLANE
    for k in range(1, SUBLANE):
        zs_ref[k, 0:span, :] = zs_ref[0, k:k + span, :]
    gwidth = width // len(POOL_WINDOWS)
    t_idx = lax.broadcasted_iota(jnp.int32, (ts, 1), 0) + avail + 1
    for gi, w in enumerate(POOL_WINDOWS):
        sl = slice(gi * gwidth, (gi + 1) * gwidth)
        cur = zp_ref[hp_rows:hp_rows + ts, sl]
        acc = cur
        for jj in range(1, w):
            acc = acc + zp_ref[hp_rows - jj:hp_rows - jj + ts, sl]
        cnt = jnp.minimum(t_idx, w).astype(F32)
        yg = acc / cnt - cur
        og = jnp.dot(yg.astype(BF), pw_ref[gi], preferred_element_type=F32) * ps_ref[:, sl]
        o_ref[:, sl] = og.astype(o_ref.dtype)
    base = hc_rows - (CONV_WIDTH - 1)
    for r in range(ts // rc):
        for c in range(width // LANE):
            cs = slice(c * LANE, (c + 1) * LANE)
            acc = jnp.zeros((rc, LANE), F32)
            for jj in range(CONV_WIDTH):
                k = (base + jj) % SUBLANE
                lo = r * rc + base + jj - k
                acc = acc + zs_ref[k, lo:lo + rc, cs] * dw_ref[jj:jj + 1, cs]
            cb_ref[r * rc:(r + 1) * rc, cs] = acc + dwb_ref[:, cs]
    cv = cb_ref[...]
    mu = jnp.mean(cv, axis=-1, keepdims=True)
    xc = cv - mu
    yn = xc * lax.rsqrt(jnp.mean(xc * xc, axis=-1, keepdims=True) + NORM_EPS) * lng_ref[...] + lnb_ref[...]
    act = yn * jax.nn.sigmoid(yn)
    o_ref[:, width:] = jnp.dot(act.astype(BF), pww_ref[...], preferred_element_type=F32).astype(o_ref.dtype)


def _convmix(u_pool, halo_pool, u_conv, halo_conv, prm, *, ts, nt, n_tiles, zero_first, avail0, hp_rows, hc_rows,
             halo_blk):
    pw, ps, dw, dwb, lng, lnb, pww = prm
    width = u_pool.shape[1]
    rc = min(64, ts)
    body = functools.partial(_convmix_body, ts=ts, nt=nt, zero_first=zero_first, avail0=avail0, rc=rc)
    full = lambda a: pl.BlockSpec(a.shape, lambda s: (0,) * a.ndim)
    return pl.pallas_call(
        body,
        grid=(n_tiles,),
        in_specs=[pl.BlockSpec((ts, width), lambda s: (s, 0)),
                  pl.BlockSpec((hp_rows, width), lambda s: (halo_blk(s, hp_rows), 0)),
                  pl.BlockSpec((ts, width), lambda s: (s, 0)),
                  pl.BlockSpec((hc_rows, width), lambda s: (halo_blk(s, hc_rows), 0)),
                  full(pw), full(ps), full(dw), full(dwb), full(lng), full(lnb), full(pww)],
        out_specs=pl.BlockSpec((ts, 2 * width), lambda s: (s, 0)),
        out_shape=jax.ShapeDtypeStruct((n_tiles * ts, 2 * width), BF),
        scratch_shapes=[pltpu.VMEM((hp_rows + ts, width), F32), pltpu.VMEM((SUBLANE, hc_rows + ts, width), F32),
                        pltpu.VMEM((ts, width), F32)],
        compiler_params=_cparams(("parallel",)),
        name="conv_mix",
    )(u_pool, halo_pool, u_conv, halo_conv, pw, ps, dw, dwb, lng, lnb, pww)


def _router_body(x_ref, g_ref, rw_ref, r_ref, xn_ref):
    x = x_ref[...]
    xn = x * lax.rsqrt(jnp.mean(x * x, axis=-1, keepdims=True) + NORM_EPS) * g_ref[...]
    xn_ref[...] = xn.astype(xn_ref.dtype)
    logits = _dot3(xn, rw_ref[...])
    lane = lax.broadcasted_iota(jnp.int32, logits.shape, 1)
    l1 = jnp.where(lane < N_EXPERTS, logits, NEG_INF)
    m1 = jnp.max(l1, axis=-1, keepdims=True)
    i1 = jnp.min(jnp.where(l1 == m1, lane, LANE), axis=-1, keepdims=True)
    l2 = jnp.where(lane == i1, NEG_INF, l1)
    m2 = jnp.max(l2, axis=-1, keepdims=True)
    i2 = jnp.min(jnp.where(l2 == m2, lane, LANE), axis=-1, keepdims=True)
    e = jnp.exp(m2 - m1)
    w1 = 1.0 / (1.0 + e)
    w2 = e / (1.0 + e)
    r_ref[...] = jnp.where(lane == 0, i1.astype(F32),
                           jnp.where(lane == 1, i2.astype(F32), jnp.where(lane == 2, w1, jnp.where(lane == 3, w2, 0.0))))


def _router(x, g, rw, tm=256):
    m, d = x.shape
    rw_pad = jnp.pad(rw, ((0, 0), (0, LANE - rw.shape[1])))
    return pl.pallas_call(
        _router_body,
        grid=(pl.cdiv(m, tm),),
        in_specs=[pl.BlockSpec((tm, d), lambda i: (i, 0)), pl.BlockSpec((1, d), lambda i: (0, 0)),
                  pl.BlockSpec((d, LANE), lambda i: (0, 0))],
        out_specs=[pl.BlockSpec((tm, LANE), lambda i: (i, 0)), pl.BlockSpec((tm, d), lambda i: (i, 0))],
        out_shape=[jax.ShapeDtypeStruct((m, LANE), F32), jax.ShapeDtypeStruct((m, d), BF)],
        compiler_params=_cparams(("parallel",)),
        name="moe_router",
    )(x, g.reshape(1, d), rw_pad)


def _row_copy(src_hbm, row, dst, drow, sem):
    return pltpu.make_async_copy(src_hbm.at[pl.ds(row, 1), :], dst.at[pl.ds(drow, 1), :], sem)


def _gather_rows_body(c0_ref, nc_ref, src_ref, xn_hbm, o_ref, buf_ref, acc_ref, first_ref, sem, *, tc):
    i = pl.program_id(0)
    c0 = c0_ref[i]
    n = nc_ref[i]
    i_next = jnp.minimum(i + 1, pl.num_programs(0) - 1)
    n_next = jnp.where(i + 1 < pl.num_programs(0), nc_ref[i_next], 0)

    def chunk_copy(c, slot):
        return pltpu.make_async_copy(xn_hbm.at[pl.ds(c * tc, tc), :], buf_ref.at[slot], sem.at[slot])

    acc_ref[...] = jnp.zeros(acc_ref.shape, F32)

    @pl.when(i == 0)
    def _():
        first_ref[0] = 0

        @pl.when(n > 0)
        def _():
            chunk_copy(c0, 0).start()

    first = first_ref[0]
    src = src_ref[...]

    def body(k, carry):
        slot = (first + k) % 2
        chunk_copy(c0 + k, slot).wait()

        @pl.when(k + 1 < n)
        def _():
            chunk_copy(c0 + k + 1, 1 - slot).start()

        @pl.when(jnp.logical_and(k + 1 == n, n_next > 0))
        def _():
            chunk_copy(c0_ref[i_next], 1 - slot).start()
            first_ref[0] = 1 - slot

        tok = (c0 + k) * tc + lax.broadcasted_iota(jnp.int32, (1, tc), 1)
        onehot = jnp.where(src == tok, 1.0, 0.0).astype(buf_ref.dtype)
        acc_ref[...] += jnp.dot(onehot, buf_ref[slot], preferred_element_type=F32)
        return carry

    lax.fori_loop(0, n, body, 0)
    o_ref[...] = acc_ref[...].astype(o_ref.dtype)


def _gather_rows(xn, src, c0, nc, tm, tc):
    n_tok, d = xn.shape
    r_tot = src.shape[0]
    body = functools.partial(_gather_rows_body, tc=tc)
    return pl.pallas_call(
        body,
        grid_spec=pltpu.PrefetchScalarGridSpec(
            num_scalar_prefetch=2,
            grid=(r_tot // tm,),
            in_specs=[pl.BlockSpec((tm, 1), lambda i, a, b: (i, 0)), pl.BlockSpec(memory_space=pl.ANY)],
            out_specs=pl.BlockSpec((tm, d), lambda i, a, b: (i, 0)),
            scratch_shapes=[pltpu.VMEM((2, tc, d), xn.dtype), pltpu.VMEM((tm, d), F32), pltpu.SMEM((1,), jnp.int32),
                            pltpu.SemaphoreType.DMA((2,))],
        ),
        out_shape=jax.ShapeDtypeStruct((r_tot, d), xn.dtype),
        compiler_params=_cparams(("arbitrary",)),
        name="moe_gather",
    )(c0, nc, src.reshape(r_tot, 1), xn)


def _combine_body(p1_ref, p2_ref, eo_hbm, x_ref, r_ref, o_ref, ot_ref, b1_ref, b2_ref, sem, *, tm, n_main, n_tail):
    i = pl.program_id(0)
    base = i * tm
    n = jnp.where(i < n_main // tm, tm, n_tail)

    def issue(rq, c):
        for u in range(4):
            r = rq * 4 + u
            _row_copy(eo_hbm, p1_ref[base + r], b1_ref, r, sem).start()
            _row_copy(eo_hbm, p2_ref[base + r], b2_ref, r, sem).start()
        return c

    lax.fori_loop(0, n // 4, issue, 0)

    def wait(r, c):
        _row_copy(eo_hbm, 0, b1_ref, r, sem).wait()
        _row_copy(eo_hbm, 0, b2_ref, r, sem).wait()
        return c

    lax.fori_loop(0, n, wait, 0)

    def rows(r):
        rr = r_ref[0:r, :]
        return x_ref[0:r, :] + rr[:, 2:3] * b1_ref[0:r, :] + rr[:, 3:4] * b2_ref[0:r, :]

    @pl.when(i < n_main // tm)
    def _():
        o_ref[...] = rows(tm)

    @pl.when(i == n_main // tm)
    def _():
        ot_ref[...] = rows(n_tail)


def _combine(x, eo, routing, p1, p2, n_main, tm=256):
    m, d = x.shape
    n_tail = m - n_main
    assert n_main % tm == 0 and 0 < n_tail <= tm and n_tail % SUBLANE == 0
    n_full = n_main // tm
    body = functools.partial(_combine_body, tm=tm, n_main=n_main, n_tail=n_tail)
    return pl.pallas_call(
        body,
        grid_spec=pltpu.PrefetchScalarGridSpec(
            num_scalar_prefetch=2,
            grid=(n_full + 1,),
            in_specs=[pl.BlockSpec(memory_space=pl.ANY), pl.BlockSpec((tm, d), lambda i, a, b: (i, 0)),
                      pl.BlockSpec((tm, LANE), lambda i, a, b: (i, 0))],
            out_specs=[pl.BlockSpec((tm, d), lambda i, a, b: (jnp.minimum(i, n_full - 1), 0)),
                       pl.BlockSpec((n_tail, d), lambda i, a, b: (0, 0))],
            scratch_shapes=[pltpu.VMEM((tm, d), F32), pltpu.VMEM((tm, d), F32), pltpu.SemaphoreType.DMA(())],
        ),
        out_shape=[jax.ShapeDtypeStruct((n_main, d), F32), jax.ShapeDtypeStruct((n_tail, d), F32)],
        compiler_params=_cparams(("arbitrary",)),
        name="moe_combine",
    )(p1, p2, eo, x, routing)


def _moe_plan(e_idx, tm, tc):
    n = e_idx.shape[0]
    n_asg = n * TOP_K
    flat_e = e_idx.reshape(-1)
    onehot = (flat_e[:, None] == jnp.arange(N_EXPERTS, dtype=jnp.int32)[None, :]).astype(jnp.int32)
    cnt = jnp.sum(onehot, axis=0)
    rank = jnp.take_along_axis(jnp.cumsum(onehot, axis=0) - onehot, flat_e[:, None], axis=1)[:, 0]
    cnt_p = ((cnt + tm - 1) // tm) * tm
    ends = jnp.cumsum(cnt_p)
    off = ends - cnt_p
    pos = off[flat_e] + rank
    n_tiles = (n_asg + N_EXPERTS * (tm - 1) + tm - 1) // tm
    r_tot = n_tiles * tm
    src = jnp.full((r_tot,), -1, jnp.int32).at[pos].set(jnp.arange(n_asg, dtype=jnp.int32) // TOP_K)
    src_t = src.reshape(n_tiles, tm)
    lo = jnp.min(jnp.where(src_t >= 0, src_t, n), axis=1)
    hi = jnp.max(src_t, axis=1)
    c0 = jnp.where(hi >= 0, lo // tc, 0).astype(jnp.int32)
    nc = jnp.where(hi >= 0, hi // tc + 1 - lo // tc, 0).astype(jnp.int32)
    tile_start = jnp.arange(n_tiles, dtype=jnp.int32) * tm
    tval = (tile_start < ends[-1]).astype(jnp.int32)
    texp = jnp.minimum(jnp.sum((tile_start[:, None] >= ends[None, :]).astype(jnp.int32), axis=1), N_EXPERTS - 1)
    last = jnp.max(jnp.where(tval > 0, texp, 0))
    texp = jnp.where(tval > 0, texp, last)
    pos2 = pos.reshape(n, TOP_K)
    return src, c0, nc, pos2[:, 0], pos2[:, 1], texp, tval


def _moe(x, g, rw, wg, wu, wd, n_main, tm=256):
    n = x.shape[0]
    tc = max(t for t in range(2 * SUBLANE, 513, 2 * SUBLANE) if n % t == 0)
    routing, xn = _router(x, g, rw)
    e_idx = routing[:, :TOP_K].astype(jnp.int32)
    src, c0, nc, p1, p2, texp, tval = _moe_plan(e_idx, tm, tc)
    xs = _gather_rows(xn, src, c0, nc, tm, tc)
    hid = _gmm(xs, [(wg, 0), (wu, 0)], wg.shape[2], epi="swiglu", texp=texp, tval=tval, out_dtype=BF, tm=tm, tn=1024,
               name="moe_up")
    eo = _gmm(hid, [(wd, 0)], wd.shape[2], texp=texp, tval=tval, tm=tm, name="moe_down")
    return _combine(x, eo, routing, p1, p2, n_main)


def _seq_tails(a, nb, t, k):
    return jnp.stack([a[(b + 1) * t - k:(b + 1) * t] for b in range(nb)])


def _attn_layer(x, n_prompt, nb_p, t_p, nb_s, li, prm, caches, page_table):
    (attn_norm, w_in, qk_g, cmp_pe, cmp_w1, cmp_w2, moba_g, w_out, ffn_norm, w_gate, w_up, w_down) = prm
    cache_cmp, cache_sel, state_win, cache_moba = caches
    d = x[0].shape[1]
    past_len = page_table.shape[1] * cache_moba.shape[2]

    g_lo = NSA_CHUNKS * LANE
    g_hi = g_lo + 3 * NSA_HEADS
    assert w_in.shape[1] == g_hi + MOBA_CHUNKS * LANE
    w_t = w_in.T
    w_moba = w_t[g_hi:][None]
    w_gate_cols = jnp.pad(w_t[g_lo:g_hi], ((0, LANE - 3 * NSA_HEADS), (0, 0)))[None]
    ones = jnp.ones((HEAD_DIM,), F32)
    zeros = jnp.zeros((HEAD_DIM,), F32)
    gain_a = jnp.concatenate([qk_g[0]] * 8 + [ones] * 4 + [qk_g[2]] * 2 + [ones] * 2 + [qk_g[3]] * 2 + [ones] * 2)[None]
    flag_a = jnp.concatenate([ones] * 8 + [zeros] * 4 + [ones] * 2 + [zeros] * 2 + [ones] * 2 + [zeros] * 2)[None]
    gain_b = jnp.concatenate([moba_g[0]] * 8 + [moba_g[1]] * 8 + [ones] * 8)[None]
    flag_b = jnp.concatenate([ones] * 16 + [zeros] * 8)[None]
    xn = _rmsnorm(x, attn_norm, tm=STREAM_TM)
    ya, ya16 = _gmm(xn, [(w_t[None], 0)], NSA_CHUNKS * LANE, epi="headnorm", gain=gain_a, flag=flag_a,
                    out_dtype=(F32, BF), tm=STREAM_TM, tn=10 * LANE, w_t=True, name="attn_in_nsa")
    yb, yb16 = _gmm(xn, [(w_moba, 0)], MOBA_CHUNKS * LANE, epi="headnorm", gain=gain_b, flag=flag_b,
                    out_dtype=(F32, BF), tm=STREAM_TM, tn=8 * LANE, w_t=True, name="attn_in_moba")
    yg = _gmm(xn, [(w_gate_cols, 0)], LANE, tm=STREAM_TM, tn=LANE, w_t=True, name="attn_in_gate")

    r = NSA_CMP_LEN // NSA_CMP_STRIDE
    wcat = cmp_w1.reshape(2, r, NSA_CMP_STRIDE, HEAD_DIM, NSA_CMP_HIDDEN).transpose(0, 2, 3, 1, 4)
    wcat = wcat.reshape(2, NSA_CMP_STRIDE, HEAD_DIM, r * NSA_CMP_HIDDEN).astype(BF)
    pe_rows = jnp.pad(cmp_pe.reshape(2, r, NSA_CMP_STRIDE * HEAD_DIM), ((0, 0), (0, SUBLANE - r), (0, 0))).astype(BF)
    w2 = cmp_w2.astype(BF)
    gk = qk_g[1][None]

    kvc_p = _compress_c2(_compress_c1_prompt(ya, wcat, nb_p, t_p), pe_rows, wcat, w2, gk)
    rows = n_prompt
    oc, sel_a = _cattn(ya, 0, 0, kvc_p, nb_p, t_p, min(512, t_p), t_p, 0, rows)
    sel_b = _moba_gate_prompt(yb, nb_p, t_p, min(1024, t_p))
    os_ = _flash(ya, ya16, sel_a, nb_p, t_p, q_ch=CH_QA, k_ch=CH_KS, v_ch=CH_VS, kvh=NSA_KV, nh=NSA_HPG,
                 blk=NSA_SEL_BLOCK, window=0, out_rows=rows)
    ow = _flash(ya, ya16, None, nb_p, t_p, q_ch=CH_QA, k_ch=CH_KW, v_ch=CH_VW, kvh=NSA_KV, nh=NSA_HPG, blk=0,
                window=NSA_WINDOW, out_rows=rows)
    ob = _flash(yb, yb16, sel_b, nb_p, t_p, q_ch=CH_QB, k_ch=CH_KB, v_ch=CH_VB, kvh=MOBA_HEADS, nh=1,
                blk=MOBA_BLOCK, window=0, out_rows=rows)

    n_pad = x[2] - n_prompt
    ys = ya[n_prompt:]
    ysb = yb[n_prompt:]
    kvc_s = _compress_c2(_compress_c1_sample(cache_cmp, li, page_table, wcat), pe_rows, wcat, w2, gk)
    q_pad = jnp.zeros((nb_s, SUBLANE, NSA_HEADS * HEAD_DIM), F32).at[:, 0].set(ys[:nb_s, :NSA_HEADS * HEAD_DIM])
    oc_s, sel_s = _cattn_sample(q_pad.reshape(nb_s * SUBLANE, -1), kvc_s, nb_s, past_len + 1, past_len)
    oc_s = oc_s.reshape(nb_s, SUBLANE, -1)[:, 0]
    n_sel = (past_len + 1) // NSA_SEL_BLOCK
    k_sel = min(NSA_SEL_COUNT - 1, n_sel)
    mask_a = sel_s[:, :, 0, :n_sel]
    idx_a = jnp.argsort(-mask_a, axis=-1, stable=True)[..., :k_sel].astype(jnp.int32)
    vld_a = (jnp.take_along_axis(mask_a, idx_a, axis=-1) > 0).astype(jnp.int32)
    page = cache_sel.shape[2]
    bpp = page // NSA_SEL_BLOCK
    pt_b = page_table[:, None, :]
    tbl_a = jnp.take_along_axis(jnp.broadcast_to(pt_b, (nb_s, NSA_KV, pt_b.shape[-1])), idx_a // bpp, axis=-1) * bpp + idx_a % bpp
    os_s = _decode_attn(ys, cache_sel, li, tbl_a.astype(jnp.int32), vld_a, q_ch=CH_QA, kn_ch=CH_KS, vn_ch=CH_VS,
                        nh=NSA_HPG, rows=NSA_SEL_BLOCK, kvh=NSA_KV, r_min=0,
                        name="dec_sel")
    win_buf = state_win.shape[2]
    tbl_w = jnp.broadcast_to(jnp.arange(nb_s, dtype=jnp.int32)[:, None, None], (nb_s, NSA_KV, 1))
    ow_s = _decode_attn(ys, state_win, li, tbl_w, jnp.ones_like(tbl_w), q_ch=CH_QA, kn_ch=CH_KW, vn_ch=CH_VW,
                        nh=NSA_HPG, rows=win_buf, kvh=NSA_KV, r_min=max(0, win_buf - (NSA_WINDOW - 1)),
                        name="dec_win")
    means = _moba_means_sample(cache_moba, li, page_table)
    qb_pad = jnp.zeros((nb_s, SUBLANE, MOBA_HEADS * HEAD_DIM), F32).at[:, 0].set(
        ysb[:nb_s, CH_QB * LANE:(CH_QB + MOBA_HEADS) * LANE])
    sel_m = _moba_gate_sample(qb_pad.reshape(nb_s * SUBLANE, -1), means, past_len)
    n_full = means.shape[2]
    k_top = min(MOBA_TOPK, n_full)
    mask_b = sel_m[:, :, 0, :]
    idx_b = jnp.argsort(-mask_b, axis=-1, stable=True)[..., :k_top].astype(jnp.int32)
    vld_b = (jnp.take_along_axis(mask_b, idx_b, axis=-1) > 0).astype(jnp.int32)
    ppb = MOBA_BLOCK // page
    pg_b = (idx_b[..., None] * ppb + jnp.arange(ppb, dtype=jnp.int32)).reshape(nb_s, MOBA_HEADS, k_top * ppb)
    tbl_b = jnp.take_along_axis(jnp.broadcast_to(pt_b, (nb_s, MOBA_HEADS, pt_b.shape[-1])), pg_b, axis=-1)
    vld_b = jnp.repeat(vld_b, ppb, axis=-1)
    ob_s = _decode_attn(ysb, cache_moba, li, tbl_b.astype(jnp.int32), vld_b, q_ch=CH_QB, kn_ch=CH_KB, vn_ch=CH_VB,
                        nh=1, rows=page, kvh=MOBA_HEADS, r_min=0,
                        name="dec_moba")

    def rows_s(part, nh):
        return jnp.pad(part[:, :, :nh].reshape(nb_s, -1), ((0, n_pad - nb_s), (0, 0)))

    sample_parts = (jnp.pad(oc_s, ((0, n_pad - nb_s), (0, 0))), rows_s(os_s, NSA_HPG), rows_s(ow_s, NSA_HPG),
                    rows_s(ob_s, 1))
    mixed = _mix((oc, os_, ow, ob), sample_parts, yg)
    x = _gmm(mixed, [(w_out[None], 0)], d, resid=x, tm=STREAM_TM, tn=1024, name="attn_out")
    hid = _gmm(_rmsnorm(x, ffn_norm), [(w_gate[None], 0), (w_up[None], 0)], w_gate.shape[1], epi="swiglu",
               out_dtype=BF, tm=STREAM_TM, tn=512, name="ffn_up")
    x = _gmm(hid, [(w_down[None], 0)], d, resid=x, tm=STREAM_TM, tn=512, name="ffn_down")

    def rows_of(lo_ch, n_ch, y=ya):
        return y[:, lo_ch * LANE:(lo_ch + n_ch) * LANE]

    def split(a, kvh):
        ap = a[:n_prompt].reshape(nb_p, t_p, 2, kvh, HEAD_DIM)
        as_ = a[n_prompt:n_prompt + nb_s].reshape(nb_s, 1, 2, kvh, HEAD_DIM)
        return ap, as_

    cmp_p, cmp_s = split(rows_of(CH_KC, 4), NSA_KV)
    sel_p, sel_s_rows = split(rows_of(CH_KS, 4), NSA_KV)
    win_rows = rows_of(CH_KW, 4)
    win_s = win_rows[n_prompt:n_prompt + nb_s].reshape(nb_s, 1, 2, NSA_KV, HEAD_DIM)
    moba_p, moba_s = split(rows_of(CH_KB, 16, yb), MOBA_HEADS)
    keep_p = min(NSA_WINDOW, t_p)
    new_win_p = _seq_tails(win_rows, nb_p, t_p, keep_p).reshape(nb_p, keep_p, 2, NSA_KV, HEAD_DIM)
    win_all = jnp.concatenate([state_win[li], win_s], axis=1)
    keep = min(NSA_WINDOW, past_len + 1)
    new_win_s = win_all[:, win_all.shape[1] - keep:]
    return x, (cmp_p, cmp_s, sel_p, sel_s_rows, new_win_p, new_win_s, moba_p, moba_s)


def _conv_layer(x, n_prompt, nb_p, t_p, nb_s, prm, states):
    (conv_norm, w_in, pool_w, pool_scale, dw, dw_b, ln_g, ln_b, pw, w_out, moe_norm, router_w, wg, wu, wd) = prm
    state_pool, state_conv = states
    d = x.shape[1]
    width = pool_w.shape[0] * pool_w.shape[1]
    xn = _rmsnorm(x, conv_norm)
    w3 = w_in[None]
    u_pool = _gmm(xn, [(w3, 0)], width, tm=512, tn=width, name="conv_in_pool")
    u_conv = _gmm(xn, [(w3, width), (w3, 2 * width)], width, epi="glu", tm=512, tn=512, name="conv_in_glu")
    prm_mix = (pool_w.astype(BF), pool_scale[None], jnp.pad(dw, ((0, 1), (0, 0))), dw_b[None], ln_g[None], ln_b[None],
               pw.astype(BF))
    pool_buf = max(POOL_WINDOWS) - 1
    conv_buf = CONV_WIDTH - 1
    hp_rows, hc_rows = 16, 32
    ts = min(256, t_p)
    nt = t_p // ts
    mixed_p = _convmix(u_pool, u_pool, u_conv, u_conv, prm_mix, ts=ts, nt=nt, n_tiles=nb_p * nt, zero_first=True,
                       avail0=0, hp_rows=hp_rows, hc_rows=hc_rows,
                       halo_blk=lambda s, hr: jnp.maximum(s * (ts // hr) - 1, 0))
    rows = x.shape[0]
    us_pool = jnp.zeros((nb_s, SUBLANE, width), F32).at[:, 0].set(u_pool[n_prompt:n_prompt + nb_s])
    us_conv = jnp.zeros((nb_s, SUBLANE, width), F32).at[:, 0].set(u_conv[n_prompt:n_prompt + nb_s])
    hs_pool = jnp.pad(state_pool, ((0, 0), (hp_rows - pool_buf, 0), (0, 0))).reshape(nb_s * hp_rows, width)
    hs_conv = jnp.pad(state_conv, ((0, 0), (hc_rows - conv_buf, 0), (0, 0))).reshape(nb_s * hc_rows, width)
    mixed_s = _convmix(us_pool.reshape(nb_s * SUBLANE, width), hs_pool, us_conv.reshape(nb_s * SUBLANE, width), hs_conv,
                       prm_mix, ts=SUBLANE, nt=1, n_tiles=nb_s, zero_first=False, avail0=pool_buf, hp_rows=hp_rows,
                       hc_rows=hc_rows, halo_blk=lambda s, hr: s)
    mixed_s = mixed_s.reshape(nb_s, SUBLANE, 2 * width)[:, 0]
    mixed = (mixed_p, jnp.pad(mixed_s, ((0, STREAM_TM - nb_s), (0, 0))), rows)
    x = _gmm(mixed, [(w_out[None], 0)], d, resid=x, tm=STREAM_TM, tn=1024, name="conv_out")
    x = _moe(x, moe_norm, router_w, wg, wu, wd, n_prompt)

    assert t_p >= conv_buf and t_p >= pool_buf
    new_pool_p = _seq_tails(u_pool, nb_p, t_p, pool_buf)
    new_conv_p = _seq_tails(u_conv, nb_p, t_p, conv_buf)
    new_pool_s = jnp.concatenate([state_pool, u_pool[n_prompt:n_prompt + nb_s][:, None]], axis=1)[:, 1:]
    new_conv_s = jnp.concatenate([state_conv, u_conv[n_prompt:n_prompt + nb_s][:, None]], axis=1)[:, 1:]
    return x, (new_pool_p, new_pool_s, new_conv_p, new_conv_s)


def kernel(x_prompt, x_sample, cache_nsa_cmp, cache_nsa_sel, state_nsa_win, cache_moba, state_pool, state_conv, page_table, attn_norm, w_attn_in, nsa_qk_norm, nsa_cmp_pe, nsa_cmp_w1, nsa_cmp_w2, moba_qk_norm, w_attn_out, ffn_norm, ffn_w_gate, ffn_w_up, ffn_w_down, conv_norm, w_conv_in, pool_w, pool_scale, conv_dw, conv_dw_b, conv_ln_g, conv_ln_b, conv_pw, w_conv_out, moe_norm, router_w, moe_w_gate, moe_w_up, moe_w_down):
    nb_p, t_p, d = x_prompt.shape
    nb_s = x_sample.shape[0]
    n_prompt = nb_p * t_p
    n_pad = 2 * SUBLANE
    assert x_sample.shape[1] == 1 and nb_s <= n_pad
    x = (x_prompt.reshape(n_prompt, d), jnp.pad(x_sample.reshape(nb_s, d), ((0, STREAM_TM - nb_s), (0, 0))),
         n_prompt + n_pad)
    li = 0
    prm_a = (attn_norm[li], w_attn_in[li], nsa_qk_norm[li], nsa_cmp_pe[li], nsa_cmp_w1[li], nsa_cmp_w2[li],
             moba_qk_norm[li], w_attn_out[li], ffn_norm[li], ffn_w_gate[li], ffn_w_up[li], ffn_w_down[li])
    x, attn_new = _attn_layer(x, n_prompt, nb_p, t_p, nb_s, li, prm_a,
                              (cache_nsa_cmp, cache_nsa_sel, state_nsa_win, cache_moba), page_table)
    prm_c = (conv_norm[li], w_conv_in[li], pool_w[li], pool_scale[li], conv_dw[li], conv_dw_b[li], conv_ln_g[li],
             conv_ln_b[li], conv_pw[li], w_conv_out[li], moe_norm[li], router_w[li], moe_w_gate[li], moe_w_up[li],
             moe_w_down[li])
    x, conv_new = _conv_layer(x, n_prompt, nb_p, t_p, nb_s, prm_c, (state_pool[li], state_conv[li]))
    cmp_p, cmp_s, sel_p, sel_s, win_p, win_s, moba_p, moba_s = attn_new
    pool_p, pool_s, conv_p, conv_s = conv_new
    y_p = x[0].reshape(nb_p, t_p, d)
    y_s = x[1][:nb_s].reshape(nb_s, 1, d)
    st = lambda a: a[None]
    return (y_p, y_s, st(cmp_p), st(cmp_s), st(sel_p), st(sel_s), st(win_p), st(win_s), st(moba_p), st(moba_s),
            st(pool_p), st(pool_s), st(conv_p), st(conv_s))
```

```python
import functools
import math

import jax
import jax.numpy as jnp
from jax import lax
from jax.experimental import pallas as pl
from jax.experimental.pallas import tpu as pltpu

F32 = jnp.float32
BF = jnp.bfloat16
NEG_INF = float("-inf")

HEAD_DIM = 128
NORM_EPS = 1e-6
NSA_HEADS = 8
NSA_KV = 2
NSA_HPG = NSA_HEADS // NSA_KV
NSA_CMP_LEN = 32
NSA_CMP_STRIDE = 16
NSA_CMP_HIDDEN = 2 * HEAD_DIM
NSA_SEL_BLOCK = 64
NSA_SEL_COUNT = 16
NSA_WINDOW = 512
NSA_SEL_FORCE = 1.0e4
MOBA_HEADS = 8
MOBA_BLOCK = 256
MOBA_TOPK = 3
POOL_WINDOWS = (2, 4, 8, 16)
CONV_WIDTH = 31
N_EXPERTS = 8
TOP_K = 2

LANE = 128
SUBLANE = 8
VMEM_LIMIT = 58 * 1024 * 1024
STREAM_TM = 512

CH_QA, CH_KC, CH_VC, CH_KS, CH_VS, CH_KW, CH_VW = 0, 8, 10, 12, 14, 16, 18
NSA_CHUNKS = 20
CH_QB, CH_KB, CH_VB = 0, 8, 16
MOBA_CHUNKS = 24
CH_GATE = 0

NT_DIMS = (((1,), (1,)), ((), ()))


def _cparams(sem):
    return pltpu.CompilerParams(dimension_semantics=sem, vmem_limit_bytes=VMEM_LIMIT)


def _split_bf16(a):
    hi = a.astype(BF)
    lo = (a - hi.astype(F32)).astype(BF)
    return hi, lo


def _dot3(a, b, dims=(((1,), (0,)), ((), ()))):
    ah, al = _split_bf16(a)
    bh, bl = _split_bf16(b)
    d = lambda x, y: lax.dot_general(x, y, dims, preferred_element_type=F32)
    return d(ah, bh) + d(ah, bl) + d(al, bh)


def _masked_softmax(s, mask, axis):
    s = jnp.where(mask, s, NEG_INF)
    m = jnp.max(s, axis=axis, keepdims=True)
    m = jnp.where(m == NEG_INF, 0.0, m)
    p = jnp.exp(s - m)
    d = jnp.sum(p, axis=axis, keepdims=True)
    return p / jnp.where(d > 0, d, 1.0)


def _split_stream(a, tm):
    if not isinstance(a, tuple):
        return [a], [lambda i: i], None, a.shape[0]
    main, tail, m = a
    n_full = main.shape[0] // tm
    assert n_full * tm == main.shape[0] and tail.shape[0] == tm and n_full * tm < m <= (n_full + 1) * tm
    return [main, tail], [lambda i: jnp.minimum(i, n_full - 1), lambda i: 0], n_full, m


def _pick_tile(refs, i, n_full):
    return refs[0][...] if n_full is None else jnp.where(i < n_full, refs[0][...], refs[1][...])


def _rmsnorm_body(*refs, n_full):
    x = _pick_tile(refs[:-2], pl.program_id(0), n_full)
    g_ref, o_ref = refs[-2:]
    ms = jnp.mean(x * x, axis=-1, keepdims=True)
    o_ref[...] = (x * lax.rsqrt(ms + NORM_EPS) * g_ref[...]).astype(o_ref.dtype)


def _rmsnorm(x, g, tm=256):
    arrs, maps, n_full, m = _split_stream(x, tm)
    d = arrs[0].shape[1]
    return pl.pallas_call(
        functools.partial(_rmsnorm_body, n_full=n_full),
        grid=(pl.cdiv(m, tm),),
        in_specs=[pl.BlockSpec((tm, d), lambda i, f=f: (f(i), 0)) for f in maps] + [pl.BlockSpec((1, d), lambda i: (0, 0))],
        out_specs=pl.BlockSpec((tm, d), lambda i: (i, 0)),
        out_shape=jax.ShapeDtypeStruct((m, d), BF),
        compiler_params=_cparams(("parallel",)),
        name="rmsnorm",
    )(*arrs, g.reshape(1, d))


def _gmm_body(texp_ref, tval_ref, *refs, n_x, x_full, n_w, cast_w, w_t, epi, n_r, r_full, tn, n_o):
    mm = (lambda x, w: lax.dot_general(x, w, NT_DIMS, preferred_element_type=F32)) if w_t else (
        lambda x, w: jnp.dot(x, w, preferred_element_type=F32))
    x_refs = refs[:n_x]
    w_refs = refs[n_x:n_x + n_w]
    pos = n_x + n_w
    if epi == "headnorm":
        gain_ref, flag_ref = refs[pos], refs[pos + 1]
        pos += 2
    resid_refs = refs[pos:pos + n_r]
    pos += n_r
    o_refs = refs[pos:pos + n_o]
    wb_refs = refs[pos + n_o:pos + n_o + n_w] if cast_w else w_refs

    i = pl.program_id(1)
    if cast_w:
        changed = jnp.logical_or(i == 0, texp_ref[i] != texp_ref[jnp.maximum(i - 1, 0)])

        @pl.when(changed)
        def _():
            for k in range(n_w):
                wb_refs[k][...] = w_refs[k][...].astype(BF)

    @pl.when(tval_ref[i] == 0)
    def _():
        for o in o_refs:
            o[...] = jnp.zeros(o.shape, o.dtype)

    @pl.when(tval_ref[i] > 0)
    def _():
        x = _pick_tile(x_refs, i, x_full)
        a = mm(x, wb_refs[0][...])
        if epi == "swiglu":
            b = mm(x, wb_refs[1][...])
            y = a * jax.nn.sigmoid(a) * b
        elif epi == "glu":
            b = mm(x, wb_refs[1][...])
            y = a * jax.nn.sigmoid(b)
        else:
            y = a
        if n_r:
            y = y + _pick_tile(resid_refs, i, r_full)
        if epi == "headnorm":
            for c in range(tn // LANE):
                sl = slice(c * LANE, (c + 1) * LANE)
                yc = y[:, sl]
                r = lax.rsqrt(jnp.mean(yc * yc, axis=-1, keepdims=True) + NORM_EPS)
                f = flag_ref[:, sl]
                yn = yc * (f * r + (1.0 - f)) * gain_ref[:, sl]
                for o in o_refs:
                    o[:, sl] = yn.astype(o.dtype)
        else:
            for o in o_refs:
                o[...] = y.astype(o.dtype)


def _gmm(x, ws, n_out, *, epi="none", texp=None, tval=None, resid=None, gain=None, flag=None,
         out_dtype=F32, tm=256, tn=512, w_t=False, name="gmm"):
    x_arrs, x_maps, x_full, m = _split_stream(x, tm)
    kdim = x_arrs[0].shape[1]
    tn = min(tn, n_out)
    n_m = pl.cdiv(m, tm)
    n_n = n_out // tn
    assert n_n * tn == n_out
    if texp is None:
        texp = jnp.zeros((n_m,), jnp.int32)
        tval = jnp.ones((n_m,), jnp.int32)
    n_w = len(ws)
    cast_w = ws[0][0].dtype != BF
    in_specs = [pl.BlockSpec((tm, kdim), lambda j, i, te, tv, f=f: (f(i), 0)) for f in x_maps]
    args = list(x_arrs)
    for w, off in ws:
        assert off % tn == 0 and w.shape[2 if w_t else 1] == kdim and (w.dtype != BF) == cast_w
        ob = off // tn
        if w_t:
            in_specs.append(pl.BlockSpec((None, tn, kdim), lambda j, i, te, tv, ob=ob: (te[i], j + ob, 0)))
        else:
            in_specs.append(pl.BlockSpec((None, kdim, tn), lambda j, i, te, tv, ob=ob: (te[i], 0, j + ob)))
        args.append(w)
    if epi == "headnorm":
        in_specs += [pl.BlockSpec((1, tn), lambda j, i, te, tv: (0, j))] * 2
        args += [gain, flag]
    n_r, r_full = 0, None
    if resid is not None:
        r_arrs, r_maps, r_full, r_m = _split_stream(resid, tm)
        assert r_m == m
        n_r = len(r_arrs)
        in_specs += [pl.BlockSpec((tm, tn), lambda j, i, te, tv, f=f: (f(i), j)) for f in r_maps]
        args += r_arrs
    dtypes = out_dtype if isinstance(out_dtype, tuple) else (out_dtype,)
    body = functools.partial(_gmm_body, n_x=len(x_arrs), x_full=x_full, n_w=n_w, cast_w=cast_w, w_t=w_t, epi=epi,
                             n_r=n_r, r_full=r_full, tn=tn, n_o=len(dtypes))
    outs = pl.pallas_call(
        body,
        grid_spec=pltpu.PrefetchScalarGridSpec(
            num_scalar_prefetch=2,
            grid=(n_n, n_m),
            in_specs=in_specs,
            out_specs=[pl.BlockSpec((tm, tn), lambda j, i, te, tv: (i, j)) for _ in dtypes],
            scratch_shapes=[pltpu.VMEM((tn, kdim) if w_t else (kdim, tn), BF) for _ in range(n_w if cast_w else 0)],
        ),
        out_shape=[jax.ShapeDtypeStruct((m, n_out), dt) for dt in dtypes],
        compiler_params=_cparams(("arbitrary", "arbitrary")),
        name=name,
    )(texp, tval, *args)
    return outs if isinstance(out_dtype, tuple) else outs[0]


def _c1_body(x_ref, w_ref, o_ref):
    nsb = x_ref.shape[0] // NSA_CMP_STRIDE
    acc = jnp.zeros((nsb, 2 * NSA_CMP_HIDDEN), F32)
    for s in range(NSA_CMP_STRIDE):
        xs = x_ref[pl.ds(s, nsb, stride=NSA_CMP_STRIDE), :]
        acc = acc + jnp.dot(xs.astype(BF), w_ref[s], preferred_element_type=F32)
    o_ref[...] = acc


def _compress_c1_prompt(y, wcat, nb, t):
    nsub = t // NSA_CMP_STRIDE
    return pl.pallas_call(
        _c1_body,
        grid=(nb, 2, NSA_KV),
        in_specs=[pl.BlockSpec((t, LANE), lambda b, kv, g: (b, CH_KC + 2 * kv + g)),
                  pl.BlockSpec((None, NSA_CMP_STRIDE, HEAD_DIM, 2 * NSA_CMP_HIDDEN), lambda b, kv, g: (kv, 0, 0, 0))],
        out_specs=pl.BlockSpec((None, None, None, nsub, 2 * NSA_CMP_HIDDEN), lambda b, kv, g: (b, kv, g, 0, 0)),
        out_shape=jax.ShapeDtypeStruct((nb, 2, NSA_KV, nsub, 2 * NSA_CMP_HIDDEN), F32),
        compiler_params=_cparams(("parallel", "parallel", "parallel")),
        name="cmp_c1_prompt",
    )(y, wcat)


C1_PAGES = 16


def _c1_sample_body(pt_ref, *refs, n_in, page):
    x_refs = refs[:n_in]
    w_ref, o_ref, scr = refs[n_in:n_in + 3]
    nsb = page // NSA_CMP_STRIDE
    rows = n_in * nsb
    for kv in range(2):
        acc = jnp.zeros((rows * NSA_KV, 2 * NSA_CMP_HIDDEN), F32)

        def rows_at(s):
            parts = [xr[pl.ds(s, nsb, stride=NSA_CMP_STRIDE), kv, :, :].reshape(nsb * NSA_KV, HEAD_DIM) for xr in x_refs]
            return (parts[0] if n_in == 1 else jnp.concatenate(parts, axis=0)).astype(BF)

        for s in range(0, NSA_CMP_STRIDE, 2):
            xs = jnp.concatenate([rows_at(s), rows_at(s + 1)], axis=1)
            w = w_ref[kv, s:s + 2].reshape(2 * HEAD_DIM, 2 * NSA_CMP_HIDDEN)
            acc = acc + jnp.dot(xs, w, preferred_element_type=F32)
        for c in range(scr.shape[0]):
            scr[c] = acc[:, c * LANE:(c + 1) * LANE]
        for g in range(NSA_KV):
            for c in range(scr.shape[0]):
                o_ref[kv, g, :, c * LANE:(c + 1) * LANE] = scr[c, pl.ds(g, rows, stride=NSA_KV), :]


def _compress_c1_sample(cache, li, page_table, wcat):
    nb, n_pages = page_table.shape
    page = cache.shape[2]
    npp = math.gcd(C1_PAGES, n_pages)
    nsb = page // NSA_CMP_STRIDE
    nsub = n_pages * nsb
    body = functools.partial(_c1_sample_body, n_in=npp, page=page)
    in_specs = [
        pl.BlockSpec((None, None, page, 2, NSA_KV, HEAD_DIM), lambda b, j, pt, i=i: (li, pt[b, j * npp + i], 0, 0, 0, 0))
        for i in range(npp)
    ]
    in_specs.append(pl.BlockSpec(wcat.shape, lambda b, j, pt: (0, 0, 0, 0)))
    return pl.pallas_call(
        body,
        grid_spec=pltpu.PrefetchScalarGridSpec(
            num_scalar_prefetch=1,
            grid=(nb, n_pages // npp),
            in_specs=in_specs,
            out_specs=pl.BlockSpec((None, 2, NSA_KV, npp * nsb, 2 * NSA_CMP_HIDDEN), lambda b, j, pt: (b, 0, 0, j, 0)),
            scratch_shapes=[pltpu.VMEM((2 * NSA_CMP_HIDDEN // LANE, npp * nsb * NSA_KV, LANE), F32)],
        ),
        out_shape=jax.ShapeDtypeStruct((nb, 2, NSA_KV, nsub, 2 * NSA_CMP_HIDDEN), F32),
        compiler_params=_cparams(("parallel", "parallel")),
        name="cmp_c1_sample",
    )(page_table, *([cache] * npp), wcat)


def _c2_body(p_ref, pe_ref, wcat_ref, w2_ref, g_ref, o_ref, *, nsub, n_c):
    kv = pl.program_id(1)
    pep = jnp.zeros((SUBLANE, 2 * NSA_CMP_HIDDEN), F32)
    for s in range(NSA_CMP_STRIDE):
        pep = pep + jnp.dot(pe_ref[s], wcat_ref[s], preferred_element_type=F32)
    bias = pep[0:1, :NSA_CMP_HIDDEN] + pep[1:2, NSA_CMP_HIDDEN:]
    nxt = pltpu.roll(p_ref[:, NSA_CMP_HIDDEN:], nsub - 1, 0)
    h = p_ref[:, :NSA_CMP_HIDDEN] + nxt + bias
    h = h * jax.nn.sigmoid(h)
    o = jnp.dot(h.astype(BF), w2_ref[...], preferred_element_type=F32)
    r = lax.rsqrt(jnp.mean(o * o, axis=-1, keepdims=True) + NORM_EPS)
    o = jnp.where(kv == 0, o * r * g_ref[...], o)
    row = lax.broadcasted_iota(jnp.int32, (nsub, 1), 0)
    o_ref[...] = jnp.where(row < n_c, o, 0.0)


def _compress_c2(p, pe_rows, wcat, w2, gk):
    nb, _, _, nsub, _ = p.shape
    n_c = nsub - NSA_CMP_LEN // NSA_CMP_STRIDE + 1
    body = functools.partial(_c2_body, nsub=nsub, n_c=n_c)
    return pl.pallas_call(
        body,
        grid=(nb, 2, NSA_KV),
        in_specs=[pl.BlockSpec((None, None, None, nsub, 2 * NSA_CMP_HIDDEN), lambda b, kv, g: (b, kv, g, 0, 0)),
                  pl.BlockSpec((None, NSA_CMP_STRIDE, SUBLANE, HEAD_DIM), lambda b, kv, g: (kv, 0, 0, 0)),
                  pl.BlockSpec((None, NSA_CMP_STRIDE, HEAD_DIM, 2 * NSA_CMP_HIDDEN), lambda b, kv, g: (kv, 0, 0, 0)),
                  pl.BlockSpec((None, NSA_CMP_HIDDEN, HEAD_DIM), lambda b, kv, g: (kv, 0, 0)),
                  pl.BlockSpec((1, HEAD_DIM), lambda b, kv, g: (0, 0))],
        out_specs=pl.BlockSpec((None, None, None, nsub, HEAD_DIM), lambda b, kv, g: (b, kv, g, 0, 0)),
        out_shape=jax.ShapeDtypeStruct((nb, 2, NSA_KV, nsub, HEAD_DIM), F32),
        compiler_params=_cparams(("parallel", "parallel", "parallel")),
        name="cmp_c2",
    )(p, pe_rows, wcat, w2, gk)


def _rank_select(sc_ref, n_iter, k):
    score = sc_ref[...]
    brow = lax.broadcasted_iota(jnp.int32, score.shape, 0)

    def body(j, rank):
        r = sc_ref[pl.ds(j, 1), :]
        beats = jnp.logical_or(r > score, jnp.logical_and(r == score, j < brow))
        return rank + jnp.where(beats, 1.0, 0.0)

    rank = lax.fori_loop(0, n_iter, body, jnp.zeros(score.shape, F32))
    return rank < k


def _cattn_body(q_ref, kc_ref, vc_ref, oc_ref, sel_ref, sc_ref, *, tq, ncp, n_c, ns, nsp, pos0, ksel):
    t0 = pos0 + pl.program_id(2) * tq
    scale = HEAD_DIM ** -0.5
    kc = kc_ref[...].astype(BF)
    vc = vc_ref[...].astype(BF)
    trow = t0 + lax.broadcasted_iota(jnp.int32, (tq, 1), 0)
    ncol = lax.broadcasted_iota(jnp.int32, (1, ncp), 1)
    valid = jnp.logical_and(ncol * NSA_CMP_STRIDE + (NSA_CMP_LEN - 1) <= trow, ncol < n_c)
    tcol = t0 + lax.broadcasted_iota(jnp.int32, (1, tq), 1)
    nrow = lax.broadcasted_iota(jnp.int32, (ncp, 1), 0)
    valid_t = jnp.logical_and(nrow * NSA_CMP_STRIDE + (NSA_CMP_LEN - 1) <= tcol, nrow < n_c)
    psum_t = jnp.zeros((ncp, tq), F32)
    for z in range(NSA_HPG):
        sl = slice(z * HEAD_DIM, (z + 1) * HEAD_DIM)
        q = (q_ref[:, sl] * scale).astype(BF)
        s = lax.dot_general(q, kc, NT_DIMS, preferred_element_type=F32)
        p = _masked_softmax(s, valid, -1)
        oc_ref[:, sl] = jnp.dot(p.astype(BF), vc, preferred_element_type=F32)
        s_t = lax.dot_general(kc, q, NT_DIMS, preferred_element_type=F32)
        psum_t = psum_t + _masked_softmax(s_t, valid_t, 0)
    r = NSA_SEL_BLOCK // NSA_CMP_STRIDE
    brow = lax.broadcasted_iota(jnp.int32, (nsp, 1), 0)
    lo = r * brow - 1
    inside = jnp.logical_and(ncol >= lo, ncol <= lo + r)
    edge = jnp.logical_or(ncol == lo, ncol == lo + r)
    m_t = jnp.where(inside, jnp.where(edge, 0.5, 1.0), 0.0).astype(BF)
    hi = psum_t.astype(BF)
    mid = (psum_t - hi.astype(F32)).astype(BF)
    low = (psum_t - hi.astype(F32) - mid.astype(F32)).astype(BF)
    imp_t = (jnp.dot(m_t, hi, preferred_element_type=F32) + jnp.dot(m_t, mid, preferred_element_type=F32)
             + jnp.dot(m_t, low, preferred_element_type=F32))
    own = tcol // NSA_SEL_BLOCK
    past = jnp.logical_and(brow < own, brow < ns)
    forced = jnp.logical_or(brow == 0, brow == own - 1)
    sc_ref[...] = jnp.where(past, jnp.where(forced, NSA_SEL_FORCE, imp_t), NEG_INF)
    n_live = jnp.minimum(ns, (t0 + tq - 1) // NSA_SEL_BLOCK)
    chosen = jnp.logical_and(past, _rank_select(sc_ref, n_live, ksel))
    sel = jnp.where(jnp.logical_or(chosen, brow == own), 1.0, 0.0)
    rows = sel_ref.shape[0]
    if rows > nsp:
        sel_ref[...] = jnp.zeros(sel_ref.shape, F32)
    sel_ref[0:nsp, :] = sel


def _cattn(q_arr, q_row_blk0, q_col_blk0, kvc, nb, tlen, tq, length, pos0, out_rows):
    nq = tlen // tq
    ncp = kvc.shape[3]
    n_c = ncp - NSA_CMP_LEN // NSA_CMP_STRIDE + 1
    ns = length // NSA_SEL_BLOCK
    ksel = min(NSA_SEL_COUNT - 1, ns)
    nsp = -(-ns // SUBLANE) * SUBLANE
    sel_rows = max(nsp, LANE)
    body = functools.partial(_cattn_body, tq=tq, ncp=ncp, n_c=n_c, ns=ns, nsp=nsp, pos0=pos0, ksel=ksel)
    gw = NSA_HPG * HEAD_DIM
    return pl.pallas_call(
        body,
        grid=(nb, NSA_KV, nq),
        in_specs=[pl.BlockSpec((tq, gw), lambda b, g, qi: (q_row_blk0 + b * nq + qi, q_col_blk0 + g)),
                  pl.BlockSpec((None, None, None, ncp, HEAD_DIM), lambda b, g, qi: (b, 0, g, 0, 0)),
                  pl.BlockSpec((None, None, None, ncp, HEAD_DIM), lambda b, g, qi: (b, 1, g, 0, 0))],
        out_specs=[pl.BlockSpec((tq, gw), lambda b, g, qi: (b * nq + qi, g)),
                   pl.BlockSpec((None, None, sel_rows, tq), lambda b, g, qi: (b, g, 0, qi))],
        out_shape=[jax.ShapeDtypeStruct((out_rows, NSA_KV * gw), F32),
                   jax.ShapeDtypeStruct((nb, NSA_KV, sel_rows, tlen), F32)],
        scratch_shapes=[pltpu.VMEM((nsp, tq), F32)],
        compiler_params=_cparams(("parallel", "parallel", "parallel")),
        name="nsa_cmp_attn",
    )(q_arr, kvc, kvc)


def _cattn_row_body(q_ref, kc_ref, vc_ref, oc_ref, sel_ref, *, tq, ncp, n_c, ns, nsl, pos0, ksel):
    scale = HEAD_DIM ** -0.5
    kc = kc_ref[...].astype(BF)
    vc = vc_ref[...].astype(BF)
    trow = pos0 + lax.broadcasted_iota(jnp.int32, (tq, 1), 0)
    ncol = lax.broadcasted_iota(jnp.int32, (1, ncp), 1)
    valid = jnp.logical_and(ncol * NSA_CMP_STRIDE + (NSA_CMP_LEN - 1) <= trow, ncol < n_c)
    psum = jnp.zeros((tq, ncp), F32)
    for z in range(NSA_HPG):
        sl = slice(z * HEAD_DIM, (z + 1) * HEAD_DIM)
        q = (q_ref[:, sl] * scale).astype(BF)
        p = _masked_softmax(lax.dot_general(q, kc, NT_DIMS, preferred_element_type=F32), valid, -1)
        oc_ref[:, sl] = jnp.dot(p.astype(BF), vc, preferred_element_type=F32)
        psum = psum + p
    r = NSA_SEL_BLOCK // NSA_CMP_STRIDE
    nrow = lax.broadcasted_iota(jnp.int32, (ncp, 1), 0)
    bcol = lax.broadcasted_iota(jnp.int32, (1, nsl), 1)
    lo = r * bcol - 1
    inside = jnp.logical_and(nrow >= lo, nrow <= lo + r)
    edge = jnp.logical_or(nrow == lo, nrow == lo + r)
    m = jnp.where(inside, jnp.where(edge, 0.5, 1.0), 0.0).astype(BF)
    hi = psum.astype(BF)
    mid = (psum - hi.astype(F32)).astype(BF)
    low = (psum - hi.astype(F32) - mid.astype(F32)).astype(BF)
    imp = (jnp.dot(hi, m, preferred_element_type=F32) + jnp.dot(mid, m, preferred_element_type=F32)
           + jnp.dot(low, m, preferred_element_type=F32))
    own = trow // NSA_SEL_BLOCK
    past = jnp.logical_and(bcol < own, bcol < ns)
    forced = jnp.logical_or(bcol == 0, bcol == own - 1)
    score = jnp.where(past, jnp.where(forced, NSA_SEL_FORCE, imp), NEG_INF)
    rank = jnp.zeros((tq, nsl), F32)
    for j in range(ns):
        cj = score[:, j:j + 1]
        beats = jnp.logical_or(cj > score, jnp.logical_and(cj == score, j < bcol))
        rank = rank + jnp.where(beats, 1.0, 0.0)
    chosen = jnp.logical_and(past, rank < ksel)
    sel_ref[...] = jnp.where(jnp.logical_or(chosen, bcol == own), 1.0, 0.0)


def _cattn_sample(q_pad, kvc, nb, length, pos0):
    tq = SUBLANE
    ncp = kvc.shape[3]
    n_c = ncp - NSA_CMP_LEN // NSA_CMP_STRIDE + 1
    ns = length // NSA_SEL_BLOCK
    ksel = min(NSA_SEL_COUNT - 1, ns)
    nsl = -(-ns // LANE) * LANE
    body = functools.partial(_cattn_row_body, tq=tq, ncp=ncp, n_c=n_c, ns=ns, nsl=nsl, pos0=pos0, ksel=ksel)
    gw = NSA_HPG * HEAD_DIM
    return pl.pallas_call(
        body,
        grid=(nb, NSA_KV),
        in_specs=[pl.BlockSpec((tq, gw), lambda b, g: (b, g)),
                  pl.BlockSpec((None, None, None, ncp, HEAD_DIM), lambda b, g: (b, 0, g, 0, 0)),
                  pl.BlockSpec((None, None, None, ncp, HEAD_DIM), lambda b, g: (b, 1, g, 0, 0))],
        out_specs=[pl.BlockSpec((tq, gw), lambda b, g: (b, g)),
                   pl.BlockSpec((None, None, tq, nsl), lambda b, g: (b, g, 0, 0))],
        out_shape=[jax.ShapeDtypeStruct((nb * tq, NSA_KV * gw), F32),
                   jax.ShapeDtypeStruct((nb, NSA_KV, tq, nsl), F32)],
        compiler_params=_cparams(("parallel", "parallel")),
        name="nsa_cmp_attn_sample",
    )(q_pad, kvc, kvc)


def _gate_body(q_ref, k_ref, sel_ref, mean_ref, sc_ref, *, tq, nb, nbp, pos0):
    qi = pl.program_id(2)

    @pl.when(qi == 0)
    def _():
        mean_ref[...] = jnp.zeros(mean_ref.shape, F32)
        for j in range(nb):
            blk = k_ref[j * MOBA_BLOCK:(j + 1) * MOBA_BLOCK, :]
            mean_ref[j:j + 1, :] = jnp.sum(blk, axis=0, keepdims=True) * (1.0 / MOBA_BLOCK)

    g_t = _dot3(mean_ref[...], q_ref[...], NT_DIMS)
    tcol = pos0 + qi * tq + lax.broadcasted_iota(jnp.int32, (1, tq), 1)
    own = tcol // MOBA_BLOCK
    brow = lax.broadcasted_iota(jnp.int32, (nbp, 1), 0)
    past = jnp.logical_and(brow < own, brow < nb)
    sc_ref[...] = jnp.where(past, g_t, NEG_INF)
    chosen = jnp.logical_and(past, _rank_select(sc_ref, nb, min(MOBA_TOPK, nb)))
    sel = jnp.where(jnp.logical_or(chosen, brow == own), 1.0, 0.0)
    if sel_ref.shape[0] > nbp:
        sel_ref[...] = jnp.zeros(sel_ref.shape, F32)
    sel_ref[0:nbp, :] = sel


def _moba_gate_prompt(y, nb, t, tq):
    nq = t // tq
    n_full = t // MOBA_BLOCK
    nbp = -(-n_full // SUBLANE) * SUBLANE
    body = functools.partial(_gate_body, tq=tq, nb=n_full, nbp=nbp, pos0=0)
    return pl.pallas_call(
        body,
        grid=(nb, MOBA_HEADS, nq),
        in_specs=[pl.BlockSpec((tq, LANE), lambda b, h, qi: (b * nq + qi, CH_QB + h)),
                  pl.BlockSpec((t, LANE), lambda b, h, qi: (b, CH_KB + h))],
        out_specs=pl.BlockSpec((None, None, LANE, tq), lambda b, h, qi: (b, h, 0, qi)),
        out_shape=jax.ShapeDtypeStruct((nb, MOBA_HEADS, LANE, t), F32),
        scratch_shapes=[pltpu.VMEM((nbp, HEAD_DIM), F32), pltpu.VMEM((nbp, tq), F32)],
        compiler_params=_cparams(("parallel", "parallel", "arbitrary")),
        name="moba_gate_prompt",
    )(y, y)


def _gate_sample_body(q_ref, m_ref, sel_ref, *, nb, pos0):
    tq = q_ref.shape[0]
    own = (pos0 + lax.broadcasted_iota(jnp.int32, (tq, 1), 0)) // MOBA_BLOCK
    bcol = lax.broadcasted_iota(jnp.int32, (1, nb), 1)
    past = bcol < own
    for h in range(MOBA_HEADS):
        g = _dot3(q_ref[:, h * HEAD_DIM:(h + 1) * HEAD_DIM], m_ref[h], NT_DIMS)
        score = jnp.where(past, g, NEG_INF)
        rank = jnp.zeros((tq, nb), F32)
        for j in range(nb):
            cj = score[:, j:j + 1]
            beats = jnp.logical_or(cj > score, jnp.logical_and(cj == score, j < bcol))
            rank = rank + jnp.where(beats, 1.0, 0.0)
        chosen = jnp.logical_and(past, rank < min(MOBA_TOPK, nb))
        sel_ref[h] = jnp.where(jnp.logical_or(chosen, bcol == own), 1.0, 0.0)


def _moba_gate_sample(q_pad, means, pos0):
    nb, _, n_full, _ = means.shape
    tq = SUBLANE
    body = functools.partial(_gate_sample_body, nb=n_full, pos0=pos0)
    return pl.pallas_call(
        body,
        grid=(nb,),
        in_specs=[pl.BlockSpec((tq, MOBA_HEADS * HEAD_DIM), lambda b: (b, 0)),
                  pl.BlockSpec((None, MOBA_HEADS, n_full, HEAD_DIM), lambda b: (b, 0, 0, 0))],
        out_specs=pl.BlockSpec((None, MOBA_HEADS, tq, n_full), lambda b: (b, 0, 0, 0)),
        out_shape=jax.ShapeDtypeStruct((nb, MOBA_HEADS, tq, n_full), F32),
        compiler_params=_cparams(("parallel",)),
        name="moba_gate_sample",
    )(q_pad, means)


MEANS_BLOCKS = 4


def _means_body(pt_ref, *refs, ppb, bps):
    c_refs = refs[:bps * ppb]
    o_ref = refs[bps * ppb]
    j = pl.program_id(1)
    for q in range(bps):
        acc = jnp.sum(c_refs[q * ppb][...], axis=0)
        for i in range(1, ppb):
            acc = acc + jnp.sum(c_refs[q * ppb + i][...], axis=0)
        m = acc * (1.0 / MOBA_BLOCK)
        for h in range(MOBA_HEADS):
            o_ref[h, pl.ds(j * bps + q, 1), :] = m[h:h + 1, :]


def _moba_means_sample(cache, li, page_table):
    nb, n_pages = page_table.shape
    page = cache.shape[2]
    ppb = MOBA_BLOCK // page
    n_full = n_pages // ppb
    bps = math.gcd(MEANS_BLOCKS, n_full)
    body = functools.partial(_means_body, ppb=ppb, bps=bps)
    in_specs = [
        pl.BlockSpec((None, None, page, None, MOBA_HEADS, HEAD_DIM),
                     lambda b, j, pt, i=i: (li, pt[b, j * bps * ppb + i], 0, 0, 0, 0))
        for i in range(bps * ppb)
    ]
    return pl.pallas_call(
        body,
        grid_spec=pltpu.PrefetchScalarGridSpec(
            num_scalar_prefetch=1,
            grid=(nb, n_full // bps),
            in_specs=in_specs,
            out_specs=pl.BlockSpec((None, MOBA_HEADS, n_full, HEAD_DIM), lambda b, j, pt: (b, 0, 0, 0)),
        ),
        out_shape=jax.ShapeDtypeStruct((nb, MOBA_HEADS, n_full, HEAD_DIM), F32),
        compiler_params=_cparams(("parallel", "arbitrary")),
        name="moba_means_sample",
    )(page_table, *([cache] * (bps * ppb)))


FLASH_UNIT_ROWS = 256
LOG2_E = 1.4426950408889634
MASK_BIG = 2.0 ** 60


def _flash_body(*refs, tq, tk, nh, blk, window, masked, ur):
    q_ref, k_ref, v_ref = refs[:3]
    pos = 3
    if masked:
        sel_ref = refs[pos]
        pos += 1
    o_ref, qs_ref, kt_ref = refs[pos:pos + 3]
    pos += 1
    units = [(z, r0) for z in range(nh) for r0 in range(0, tq, ur)]
    stat = refs[pos + 2:pos + 2 + 2 * len(units)]
    m_refs, acc_refs = stat[0::2], stat[1::2]
    qi = pl.program_id(2)
    t0 = qi * tq
    scale = HEAD_DIM ** -0.5 * LOG2_E
    if masked:
        bias = ((jnp.transpose(sel_ref[...]) - 1.0) * MASK_BIG).astype(BF)
    for z in range(nh):
        qz = (q_ref[:, z * HEAD_DIM:(z + 1) * HEAD_DIM] * scale).astype(BF)
        qs_ref[z] = jnp.concatenate([qz, bias], axis=1) if masked else qz
    for u in range(len(units)):
        m_refs[u][...] = jnp.full(m_refs[u].shape, NEG_INF, F32)
        acc_refs[u][...] = jnp.zeros(acc_refs[u].shape, F32)
    trow = t0 + lax.broadcasted_iota(jnp.int32, (tq, 1), 0)

    @pl.when(qi == 0)
    def _():
        for c in range(kt_ref.shape[0]):
            kc = k_ref[c * tk:(c + 1) * tk, :]
            if masked:
                prow = c * tk + lax.broadcasted_iota(jnp.int32, (tk, 1), 0)
                jcol = lax.broadcasted_iota(jnp.int32, (1, LANE), 1)
                kc = jnp.concatenate([kc, jnp.where(prow // blk == jcol, 1.0, 0.0).astype(BF)], axis=1)
            kt_ref[c] = kc.T

    def process(c, edge):
        ks = pl.multiple_of(c * tk, tk)
        kt = kt_ref[c]
        vc = jnp.concatenate([v_ref[pl.ds(ks, tk), :], jnp.ones((tk, LANE), BF)], axis=1)
        if edge:
            pcol = ks + lax.broadcasted_iota(jnp.int32, (1, tk), 1)
            valid = pcol <= trow
            if window:
                valid = jnp.logical_and(valid, pcol > trow - window)
        for u, (z, r0) in enumerate(units):
            s = jnp.dot(qs_ref[z, r0:r0 + ur, :], kt, preferred_element_type=F32)
            m_old = m_refs[u][...]
            if edge:
                s = jnp.where(valid[r0:r0 + ur], s, NEG_INF)
                m_new = jnp.maximum(m_old, jnp.max(s, axis=-1, keepdims=True))
                m_use = jnp.where(m_new == NEG_INF, 0.0, m_new)
            else:
                m_new = jnp.maximum(m_old, jnp.max(s, axis=-1, keepdims=True))
                m_use = m_new
            alpha = jnp.exp2(m_old - m_use)
            p = jnp.exp2(s - m_use)
            acc_refs[u][...] = alpha * acc_refs[u][...] + jnp.dot(p.astype(BF), vc, preferred_element_type=F32)
            m_refs[u][...] = m_new

    if masked:
        def pair(c2, carry):
            process(2 * c2, False)
            process(2 * c2 + 1, False)
            return carry

        lax.fori_loop(0, qi // 2, pair, 0)

        @pl.when(qi % 2 == 1)
        def _():
            process(qi - 1, False)

        process(qi, True)
    else:
        c_hi = (t0 + tq + tk - 1) // tk
        c_lo = jnp.maximum(t0 - window + 1, 0) // tk
        lax.fori_loop(c_lo, c_hi, lambda c, carry: (process(c, True), carry)[1], 0)
    for u, (z, r0) in enumerate(units):
        l = acc_refs[u][:, HEAD_DIM:HEAD_DIM + 1]
        o_ref[r0:r0 + ur, z * HEAD_DIM:(z + 1) * HEAD_DIM] = acc_refs[u][:, :HEAD_DIM] / jnp.where(l > 0, l, 1.0)


def _flash(y, y16, sel, nb, t, *, q_ch, k_ch, v_ch, kvh, nh, blk, window, out_rows, tq=512, tk=512):
    tq = min(tq, t)
    tk = min(tk, t)
    nq = t // tq
    masked = sel is not None
    gw = nh * HEAD_DIM
    assert q_ch % nh == 0 and (tq == tk or not masked)
    in_specs = [pl.BlockSpec((tq, gw), lambda b, h, qi: (b * nq + qi, q_ch // nh + h)),
                pl.BlockSpec((t, LANE), lambda b, h, qi: (b, k_ch + h)),
                pl.BlockSpec((t, LANE), lambda b, h, qi: (b, v_ch + h))]
    args = [y, y16, y16]
    if masked:
        in_specs.append(pl.BlockSpec((None, None, LANE, tq), lambda b, h, qi: (b, h, 0, qi)))
        args.append(sel)
    ur = min(FLASH_UNIT_ROWS, tq)
    n_units = nh * (tq // ur)
    kd = (2 if masked else 1) * HEAD_DIM
    body = functools.partial(_flash_body, tq=tq, tk=tk, nh=nh, blk=blk, window=window, masked=masked, ur=ur)
    unit_scratch = [pltpu.VMEM((ur, 1), F32), pltpu.VMEM((ur, 2 * HEAD_DIM), F32)]
    return pl.pallas_call(
        body,
        grid=(nb, kvh, nq),
        in_specs=in_specs,
        out_specs=pl.BlockSpec((tq, gw), lambda b, h, qi: (b * nq + qi, h)),
        out_shape=jax.ShapeDtypeStruct((out_rows, kvh * gw), F32),
        scratch_shapes=[pltpu.VMEM((nh, tq, kd), BF), pltpu.VMEM((t // tk, kd, tk), BF)] + unit_scratch * n_units,
        compiler_params=_cparams(("parallel", "parallel", "arbitrary")),
        name="flash_" + ("win" if window else "blk%d" % blk),
    )(*args)


def _dec_body(tbl_ref, vld_ref, q_ref, *refs, nh, rows, kvh, nblk, r_min):
    k_refs = refs[:nblk]
    v_refs = refs[nblk:2 * nblk]
    kn_ref, vn_ref, o_ref = refs[2 * nblk:2 * nblk + 3]
    b = pl.program_id(0)
    h = pl.program_id(1)
    scale = HEAD_DIM ** -0.5
    qrow = q_ref[pl.ds(b, 1), :] * scale
    zrow = lax.broadcasted_iota(jnp.int32, (SUBLANE, 1), 0)
    qm = jnp.zeros((SUBLANE, HEAD_DIM), F32)
    for z in range(nh):
        qm = jnp.where(zrow == z, qrow[:, z * HEAD_DIM:(z + 1) * HEAD_DIM], qm)
    qb = qm.astype(BF)
    col = lax.broadcasted_iota(jnp.int32, (1, rows * kvh), 1)
    valid = jnp.logical_and(col % kvh == h, col // kvh >= r_min)
    kn = kn_ref[pl.ds(b, 1), :]
    vn = vn_ref[pl.ds(b, 1), :]
    s_new = jnp.sum(qm * kn, axis=-1, keepdims=True)
    scores = []
    m = s_new
    for i in range(nblk):
        kk = k_refs[i][...].reshape(rows * kvh, HEAD_DIM).astype(BF)
        s = lax.dot_general(qb, kk, NT_DIMS, preferred_element_type=F32)
        s = jnp.where(jnp.logical_and(valid, vld_ref[b, h, i] > 0), s, NEG_INF)
        scores.append(s)
        m = jnp.maximum(m, jnp.max(s, axis=-1, keepdims=True))
    p_new = jnp.exp(s_new - m)
    l = p_new
    acc = p_new * vn
    for i in range(nblk):
        p = jnp.exp(scores[i] - m)
        l = l + jnp.sum(p, axis=-1, keepdims=True)
        vv = v_refs[i][...].reshape(rows * kvh, HEAD_DIM).astype(BF)
        acc = acc + jnp.dot(p.astype(BF), vv, preferred_element_type=F32)
    o_ref[...] = acc / l


def _decode_attn(ys, cache, li_fixed, tbl, vld, *, q_ch, kn_ch, vn_ch, nh, rows, kvh, r_min, name):
    nb, _, nblk = tbl.shape
    page = cache.shape[2]
    bpp = page // rows
    gw = nh * HEAD_DIM
    body = functools.partial(_dec_body, nh=nh, rows=rows, kvh=kvh, nblk=nblk, r_min=r_min)

    def kv_spec(kv, i):
        def imap(b, h, tb, vl):
            t = tb[b, h, i]
            return (li_fixed, t // bpp, t % bpp, kv, 0, 0)

        return pl.BlockSpec((None, None, rows, None, kvh, HEAD_DIM), imap)

    nrow = ys.shape[0]
    return pl.pallas_call(
        body,
        grid_spec=pltpu.PrefetchScalarGridSpec(
            num_scalar_prefetch=2,
            grid=(nb, kvh),
            in_specs=[pl.BlockSpec((nrow, gw), lambda b, h, tb, vl: (0, q_ch // nh + h))]
            + [kv_spec(0, i) for i in range(nblk)] + [kv_spec(1, i) for i in range(nblk)]
            + [pl.BlockSpec((nrow, LANE), lambda b, h, tb, vl: (0, kn_ch + h)),
               pl.BlockSpec((nrow, LANE), lambda b, h, tb, vl: (0, vn_ch + h))],
            out_specs=pl.BlockSpec((None, None, SUBLANE, HEAD_DIM), lambda b, h, tb, vl: (b, h, 0, 0)),
        ),
        out_shape=jax.ShapeDtypeStruct((nb, kvh, SUBLANE, HEAD_DIM), F32),
        compiler_params=_cparams(("parallel", "parallel")),
        name=name,
    )(tbl, vld, ys, *([cache] * (2 * nblk)), ys, ys)


def _mix_body(oc_ref, os_ref, ow_ref, ob_ref, soc_ref, sos_ref, sow_ref, sob_ref, gate_ref, o_ref, *, n_p_tiles, n_s):
    i = pl.program_id(0)
    w = NSA_HEADS * HEAD_DIM

    def emit(oc, os_, ow, ob, rows):
        g = jax.nn.sigmoid(gate_ref[0:rows, :])
        for h in range(NSA_HEADS):
            sl = slice(h * HEAD_DIM, (h + 1) * HEAD_DIM)
            o = (g[:, 3 * h:3 * h + 1] * oc[:, sl] + g[:, 3 * h + 1:3 * h + 2] * os_[:, sl]
                 + g[:, 3 * h + 2:3 * h + 3] * ow[:, sl])
            o_ref[0:rows, sl] = o.astype(o_ref.dtype)
        o_ref[0:rows, w:] = ob[...].astype(o_ref.dtype)

    @pl.when(i < n_p_tiles)
    def _():
        emit(oc_ref, os_ref, ow_ref, ob_ref, o_ref.shape[0])

    @pl.when(i >= n_p_tiles)
    def _():
        emit(soc_ref, sos_ref, sow_ref, sob_ref, n_s)


def _mix(prompt_parts, sample_parts, y, tm=256):
    n_prompt = prompt_parts[0].shape[0]
    n_s = sample_parts[0].shape[0]
    m = y.shape[0]
    tm = min(tm, n_prompt)
    assert n_prompt % tm == 0 and m == n_prompt + n_s and n_s <= tm
    n_p_tiles = n_prompt // tm
    wa = NSA_HEADS * HEAD_DIM
    wb = MOBA_HEADS * HEAD_DIM
    pmap = lambda i: (jnp.minimum(i, n_p_tiles - 1), 0)
    body = functools.partial(_mix_body, n_p_tiles=n_p_tiles, n_s=n_s)
    return pl.pallas_call(
        body,
        grid=(n_p_tiles + 1,),
        in_specs=[pl.BlockSpec((tm, wa), pmap)] * 3 + [pl.BlockSpec((tm, wb), pmap)]
        + [pl.BlockSpec((n_s, wa), lambda i: (0, 0))] * 3 + [pl.BlockSpec((n_s, wb), lambda i: (0, 0))]
        + [pl.BlockSpec((tm, LANE), lambda i: (i, CH_GATE))],
        out_specs=pl.BlockSpec((tm, wa + wb), lambda i: (i, 0)),
        out_shape=jax.ShapeDtypeStruct((m, wa + wb), BF),
        compiler_params=_cparams(("parallel",)),
        name="attn_mix",
    )(*prompt_parts, *sample_parts, y)


def _convmix_body(up_ref, hp_ref, uc_ref, hc_ref, pw_ref, ps_ref, dw_ref, dwb_ref, lng_ref, lnb_ref, pww_ref,
                  o_ref, zp_ref, zs_ref, cb_ref, *, ts, nt, zero_first, avail0, rc):
    s = pl.program_id(0)
    hp_rows = hp_ref.shape[0]
    hc_rows = hc_ref.shape[0]
    width = up_ref.shape[1]
    if zero_first:
        keep = jnp.where(s % nt == 0, 0.0, 1.0)
        zp_ref[0:hp_rows, :] = hp_ref[...] * keep
        zs_ref[0, 0:hc_rows, :] = hc_ref[...] * keep
        avail = (s % nt) * ts + avail0
    else:
        zp_ref[0:hp_rows, :] = hp_ref[...]
        zs_ref[0, 0:hc_rows, :] = hc_ref[...]
        avail = avail0
    zp_ref[hp_rows:, :] = up_ref[...]
    zs_ref[0, hc_rows:, :] = uc_ref[...]
    span = hc_rows + ts - SUBLANE
    for k in range(1, SUBLANE):
        zs_ref[k, 0:span, :] = zs_ref[0, k:k + span, :]
    gwidth = width // len(POOL_WINDOWS)
    t_idx = lax.broadcasted_iota(jnp.int32, (ts, 1), 0) + avail + 1
    for gi, w in enumerate(POOL_WINDOWS):
        sl = slice(gi * gwidth, (gi + 1) * gwidth)
        cur = zp_ref[hp_rows:hp_rows + ts, sl]
        acc = cur
        for jj in range(1, w):
            acc = acc + zp_ref[hp_rows - jj:hp_rows - jj + ts, sl]
        cnt = jnp.minimum(t_idx, w).astype(F32)
        yg = acc / cnt - cur
        og = jnp.dot(yg.astype(BF), pw_ref[gi], preferred_element_type=F32) * ps_ref[:, sl]
        o_ref[:, sl] = og.astype(o_ref.dtype)
    base = hc_rows - (CONV_WIDTH - 1)
    for r in range(ts // rc):
        for c in range(width // LANE):
            cs = slice(c * LANE, (c + 1) * LANE)
            acc = jnp.zeros((rc, LANE), F32)
            for jj in range(CONV_WIDTH):
                k = (base + jj) % SUBLANE
                lo = r * rc + base + jj - k
                acc = acc + zs_ref[k, lo:lo + rc, cs] * dw_ref[jj:jj + 1, cs]
            cb_ref[r * rc:(r + 1) * rc, cs] = acc + dwb_ref[:, cs]
    cv = cb_ref[...]
    mu = jnp.mean(cv, axis=-1, keepdims=True)
    xc = cv - mu
    yn = xc * lax.rsqrt(jnp.mean(xc * xc, axis=-1, keepdims=True) + NORM_EPS) * lng_ref[...] + lnb_ref[...]
    act = yn * jax.nn.sigmoid(yn)
    o_ref[:, width:] = jnp.dot(act.astype(BF), pww_ref[...], preferred_element_type=F32).astype(o_ref.dtype)


def _convmix(u_pool, halo_pool, u_conv, halo_conv, prm, *, ts, nt, n_tiles, zero_first, avail0, hp_rows, hc_rows,
             halo_blk):
    pw, ps, dw, dwb, lng, lnb, pww = prm
    width = u_pool.shape[1]
    rc = min(64, ts)
    body = functools.partial(_convmix_body, ts=ts, nt=nt, zero_first=zero_first, avail0=avail0, rc=rc)
    full = lambda a: pl.BlockSpec(a.shape, lambda s: (0,) * a.ndim)
    return pl.pallas_call(
        body,
        grid=(n_tiles,),
        in_specs=[pl.BlockSpec((ts, width), lambda s: (s, 0)),
                  pl.BlockSpec((hp_rows, width), lambda s: (halo_blk(s, hp_rows), 0)),
                  pl.BlockSpec((ts, width), lambda s: (s, 0)),
                  pl.BlockSpec((hc_rows, width), lambda s: (halo_blk(s, hc_rows), 0)),
                  full(pw), full(ps), full(dw), full(dwb), full(lng), full(lnb), full(pww)],
        out_specs=pl.BlockSpec((ts, 2 * width), lambda s: (s, 0)),
        out_shape=jax.ShapeDtypeStruct((n_tiles * ts, 2 * width), BF),
        scratch_shapes=[pltpu.VMEM((hp_rows + ts, width), F32), pltpu.VMEM((SUBLANE, hc_rows + ts, width), F32),
                        pltpu.VMEM((ts, width), F32)],
        compiler_params=_cparams(("parallel",)),
        name="conv_mix",
    )(u_pool, halo_pool, u_conv, halo_conv, pw, ps, dw, dwb, lng, lnb, pww)


def _router_body(x_ref, g_ref, rw_ref, r_ref, xn_ref):
    x = x_ref[...]
    xn = x * lax.rsqrt(jnp.mean(x * x, axis=-1, keepdims=True) + NORM_EPS) * g_ref[...]
    xn_ref[...] = xn.astype(xn_ref.dtype)
    logits = _dot3(xn, rw_ref[...])
    lane = lax.broadcasted_iota(jnp.int32, logits.shape, 1)
    l1 = jnp.where(lane < N_EXPERTS, logits, NEG_INF)
    m1 = jnp.max(l1, axis=-1, keepdims=True)
    i1 = jnp.min(jnp.where(l1 == m1, lane, LANE), axis=-1, keepdims=True)
    l2 = jnp.where(lane == i1, NEG_INF, l1)
    m2 = jnp.max(l2, axis=-1, keepdims=True)
    i2 = jnp.min(jnp.where(l2 == m2, lane, LANE), axis=-1, keepdims=True)
    e = jnp.exp(m2 - m1)
    w1 = 1.0 / (1.0 + e)
    w2 = e / (1.0 + e)
    r_ref[...] = jnp.where(lane == 0, i1.astype(F32),
                           jnp.where(lane == 1, i2.astype(F32), jnp.where(lane == 2, w1, jnp.where(lane == 3, w2, 0.0))))


def _router(x, g, rw, tm=256):
    m, d = x.shape
    rw_pad = jnp.pad(rw, ((0, 0), (0, LANE - rw.shape[1])))
    return pl.pallas_call(
        _router_body,
        grid=(pl.cdiv(m, tm),),
        in_specs=[pl.BlockSpec((tm, d), lambda i: (i, 0)), pl.BlockSpec((1, d), lambda i: (0, 0)),
                  pl.BlockSpec((d, LANE), lambda i: (0, 0))],
        out_specs=[pl.BlockSpec((tm, LANE), lambda i: (i, 0)), pl.BlockSpec((tm, d), lambda i: (i, 0))],
        out_shape=[jax.ShapeDtypeStruct((m, LANE), F32), jax.ShapeDtypeStruct((m, d), BF)],
        compiler_params=_cparams(("parallel",)),
        name="moe_router",
    )(x, g.reshape(1, d), rw_pad)


def _row_copy(src_hbm, row, dst, drow, sem):
    return pltpu.make_async_copy(src_hbm.at[pl.ds(row, 1), :], dst.at[pl.ds(drow, 1), :], sem)


def _gather_rows_body(c0_ref, nc_ref, src_ref, xn_hbm, o_ref, buf_ref, acc_ref, first_ref, sem, *, tc):
    i = pl.program_id(0)
    c0 = c0_ref[i]
    n = nc_ref[i]
    i_next = jnp.minimum(i + 1, pl.num_programs(0) - 1)
    n_next = jnp.where(i + 1 < pl.num_programs(0), nc_ref[i_next], 0)

    def chunk_copy(c, slot):
        return pltpu.make_async_copy(xn_hbm.at[pl.ds(c * tc, tc), :], buf_ref.at[slot], sem.at[slot])

    acc_ref[...] = jnp.zeros(acc_ref.shape, F32)

    @pl.when(i == 0)
    def _():
        first_ref[0] = 0

        @pl.when(n > 0)
        def _():
            chunk_copy(c0, 0).start()

    first = first_ref[0]
    src = src_ref[...]

    def body(k, carry):
        slot = (first + k) % 2
        chunk_copy(c0 + k, slot).wait()

        @pl.when(k + 1 < n)
        def _():
            chunk_copy(c0 + k + 1, 1 - slot).start()

        @pl.when(jnp.logical_and(k + 1 == n, n_next > 0))
        def _():
            chunk_copy(c0_ref[i_next], 1 - slot).start()
            first_ref[0] = 1 - slot

        tok = (c0 + k) * tc + lax.broadcasted_iota(jnp.int32, (1, tc), 1)
        onehot = jnp.where(src == tok, 1.0, 0.0).astype(buf_ref.dtype)
        acc_ref[...] += jnp.dot(onehot, buf_ref[slot], preferred_element_type=F32)
        return carry

    lax.fori_loop(0, n, body, 0)
    o_ref[...] = acc_ref[...].astype(o_ref.dtype)


def _gather_rows(xn, src, c0, nc, tm, tc):
    n_tok, d = xn.shape
    r_tot = src.shape[0]
    body = functools.partial(_gather_rows_body, tc=tc)
    return pl.pallas_call(
        body,
        grid_spec=pltpu.PrefetchScalarGridSpec(
            num_scalar_prefetch=2,
            grid=(r_tot // tm,),
            in_specs=[pl.BlockSpec((tm, 1), lambda i, a, b: (i, 0)), pl.BlockSpec(memory_space=pl.ANY)],
            out_specs=pl.BlockSpec((tm, d), lambda i, a, b: (i, 0)),
            scratch_shapes=[pltpu.VMEM((2, tc, d), xn.dtype), pltpu.VMEM((tm, d), F32), pltpu.SMEM((1,), jnp.int32),
                            pltpu.SemaphoreType.DMA((2,))],
        ),
        out_shape=jax.ShapeDtypeStruct((r_tot, d), xn.dtype),
        compiler_params=_cparams(("arbitrary",)),
        name="moe_gather",
    )(c0, nc, src.reshape(r_tot, 1), xn)


def _combine_body(p1_ref, p2_ref, eo_hbm, x_ref, r_ref, o_ref, ot_ref, b1_ref, b2_ref, sem, *, tm, n_main, n_tail):
    i = pl.program_id(0)
    base = i * tm
    n = jnp.where(i < n_main // tm, tm, n_tail)

    def issue(rq, c):
        for u in range(4):
            r = rq * 4 + u
            _row_copy(eo_hbm, p1_ref[base + r], b1_ref, r, sem).start(priority=0)
            _row_copy(eo_hbm, p2_ref[base + r], b2_ref, r, sem).start(priority=1)
        return c

    lax.fori_loop(0, n // 4, issue, 0)

    def wait(r, c):
        _row_copy(eo_hbm, 0, b1_ref, r, sem).wait()
        _row_copy(eo_hbm, 0, b2_ref, r, sem).wait()
        return c

    lax.fori_loop(0, n, wait, 0)

    def rows(r):
        rr = r_ref[0:r, :]
        return x_ref[0:r, :] + rr[:, 2:3] * b1_ref[0:r, :] + rr[:, 3:4] * b2_ref[0:r, :]

    @pl.when(i < n_main // tm)
    def _():
        o_ref[...] = rows(tm)

    @pl.when(i == n_main // tm)
    def _():
        ot_ref[...] = rows(n_tail)


def _combine(x, eo, routing, p1, p2, n_main, tm=256):
    m, d = x.shape
    n_tail = m - n_main
    assert n_main % tm == 0 and 0 < n_tail <= tm and n_tail % SUBLANE == 0
    n_full = n_main // tm
    body = functools.partial(_combine_body, tm=tm, n_main=n_main, n_tail=n_tail)
    return pl.pallas_call(
        body,
        grid_spec=pltpu.PrefetchScalarGridSpec(
            num_scalar_prefetch=2,
            grid=(n_full + 1,),
            in_specs=[pl.BlockSpec(memory_space=pl.ANY), pl.BlockSpec((tm, d), lambda i, a, b: (i, 0)),
                      pl.BlockSpec((tm, LANE), lambda i, a, b: (i, 0))],
            out_specs=[pl.BlockSpec((tm, d), lambda i, a, b: (jnp.minimum(i, n_full - 1), 0)),
                       pl.BlockSpec((n_tail, d), lambda i, a, b: (0, 0))],
            scratch_shapes=[pltpu.VMEM((tm, d), F32), pltpu.VMEM((tm, d), F32), pltpu.SemaphoreType.DMA(())],
        ),
        out_shape=[jax.ShapeDtypeStruct((n_main, d), F32), jax.ShapeDtypeStruct((n_tail, d), F32)],
        compiler_params=_cparams(("arbitrary",)),
        name="moe_combine",
    )(p1, p2, eo, x, routing)


def _moe_plan(e_idx, tm, tc):
    n = e_idx.shape[0]
    n_asg = n * TOP_K
    flat_e = e_idx.reshape(-1)
    onehot = (flat_e[:, None] == jnp.arange(N_EXPERTS, dtype=jnp.int32)[None, :]).astype(jnp.int32)
    cnt = jnp.sum(onehot, axis=0)
    rank = jnp.take_along_axis(jnp.cumsum(onehot, axis=0) - onehot, flat_e[:, None], axis=1)[:, 0]
    cnt_p = ((cnt + tm - 1) // tm) * tm
    ends = jnp.cumsum(cnt_p)
    off = ends - cnt_p
    pos = off[flat_e] + rank
    n_tiles = (n_asg + N_EXPERTS * (tm - 1) + tm - 1) // tm
    r_tot = n_tiles * tm
    src = jnp.full((r_tot,), -1, jnp.int32).at[pos].set(jnp.arange(n_asg, dtype=jnp.int32) // TOP_K)
    src_t = src.reshape(n_tiles, tm)
    lo = jnp.min(jnp.where(src_t >= 0, src_t, n), axis=1)
    hi = jnp.max(src_t, axis=1)
    c0 = jnp.where(hi >= 0, lo // tc, 0).astype(jnp.int32)
    nc = jnp.where(hi >= 0, hi // tc + 1 - lo // tc, 0).astype(jnp.int32)
    tile_start = jnp.arange(n_tiles, dtype=jnp.int32) * tm
    tval = (tile_start < ends[-1]).astype(jnp.int32)
    texp = jnp.minimum(jnp.sum((tile_start[:, None] >= ends[None, :]).astype(jnp.int32), axis=1), N_EXPERTS - 1)
    last = jnp.max(jnp.where(tval > 0, texp, 0))
    texp = jnp.where(tval > 0, texp, last)
    pos2 = pos.reshape(n, TOP_K)
    return src, c0, nc, pos2[:, 0], pos2[:, 1], texp, tval


def _moe(x, g, rw, wg, wu, wd, n_main, tm=256):
    n = x.shape[0]
    tc = max(t for t in range(2 * SUBLANE, 513, 2 * SUBLANE) if n % t == 0)
    routing, xn = _router(x, g, rw)
    e_idx = routing[:, :TOP_K].astype(jnp.int32)
    src, c0, nc, p1, p2, texp, tval = _moe_plan(e_idx, tm, tc)
    xs = _gather_rows(xn, src, c0, nc, tm, tc)
    hid = _gmm(xs, [(wg, 0), (wu, 0)], wg.shape[2], epi="swiglu", texp=texp, tval=tval, out_dtype=BF, tm=tm, tn=1024,
               name="moe_up")
    eo = _gmm(hid, [(wd, 0)], wd.shape[2], texp=texp, tval=tval, tm=tm, name="moe_down")
    return _combine(x, eo, routing, p1, p2, n_main)


def _seq_tails(a, nb, t, k):
    return jnp.stack([a[(b + 1) * t - k:(b + 1) * t] for b in range(nb)])


def _attn_layer(x, n_prompt, nb_p, t_p, nb_s, li, prm, caches, page_table):
    (attn_norm, w_in, qk_g, cmp_pe, cmp_w1, cmp_w2, moba_g, w_out, ffn_norm, w_gate, w_up, w_down) = prm
    cache_cmp, cache_sel, state_win, cache_moba = caches
    d = x[0].shape[1]
    past_len = page_table.shape[1] * cache_moba.shape[2]

    g_lo = NSA_CHUNKS * LANE
    g_hi = g_lo + 3 * NSA_HEADS
    assert w_in.shape[1] == g_hi + MOBA_CHUNKS * LANE
    w_t = w_in.T
    w_moba = w_t[g_hi:][None]
    w_gate_cols = jnp.pad(w_t[g_lo:g_hi], ((0, LANE - 3 * NSA_HEADS), (0, 0)))[None]
    ones = jnp.ones((HEAD_DIM,), F32)
    zeros = jnp.zeros((HEAD_DIM,), F32)
    gain_a = jnp.concatenate([qk_g[0]] * 8 + [ones] * 4 + [qk_g[2]] * 2 + [ones] * 2 + [qk_g[3]] * 2 + [ones] * 2)[None]
    flag_a = jnp.concatenate([ones] * 8 + [zeros] * 4 + [ones] * 2 + [zeros] * 2 + [ones] * 2 + [zeros] * 2)[None]
    gain_b = jnp.concatenate([moba_g[0]] * 8 + [moba_g[1]] * 8 + [ones] * 8)[None]
    flag_b = jnp.concatenate([ones] * 16 + [zeros] * 8)[None]
    xn = _rmsnorm(x, attn_norm, tm=STREAM_TM)
    ya, ya16 = _gmm(xn, [(w_t[None], 0)], NSA_CHUNKS * LANE, epi="headnorm", gain=gain_a, flag=flag_a,
                    out_dtype=(F32, BF), tm=STREAM_TM, tn=10 * LANE, w_t=True, name="attn_in_nsa")
    yb, yb16 = _gmm(xn, [(w_moba, 0)], MOBA_CHUNKS * LANE, epi="headnorm", gain=gain_b, flag=flag_b,
                    out_dtype=(F32, BF), tm=STREAM_TM, tn=8 * LANE, w_t=True, name="attn_in_moba")
    yg = _gmm(xn, [(w_gate_cols, 0)], LANE, tm=STREAM_TM, tn=LANE, w_t=True, name="attn_in_gate")

    r = NSA_CMP_LEN // NSA_CMP_STRIDE
    wcat = cmp_w1.reshape(2, r, NSA_CMP_STRIDE, HEAD_DIM, NSA_CMP_HIDDEN).transpose(0, 2, 3, 1, 4)
    wcat = wcat.reshape(2, NSA_CMP_STRIDE, HEAD_DIM, r * NSA_CMP_HIDDEN).astype(BF)
    pe_rows = cmp_pe.reshape(2, r, NSA_CMP_STRIDE, HEAD_DIM).transpose(0, 2, 1, 3)
    pe_rows = jnp.pad(pe_rows, ((0, 0), (0, 0), (0, SUBLANE - r), (0, 0))).astype(BF)
    w2 = cmp_w2.astype(BF)
    gk = qk_g[1][None]

    kvc_p = _compress_c2(_compress_c1_prompt(ya, wcat, nb_p, t_p), pe_rows, wcat, w2, gk)
    rows = n_prompt
    oc, sel_a = _cattn(ya, 0, 0, kvc_p, nb_p, t_p, min(512, t_p), t_p, 0, rows)
    sel_b = _moba_gate_prompt(yb, nb_p, t_p, min(1024, t_p))
    os_ = _flash(ya, ya16, sel_a, nb_p, t_p, q_ch=CH_QA, k_ch=CH_KS, v_ch=CH_VS, kvh=NSA_KV, nh=NSA_HPG,
                 blk=NSA_SEL_BLOCK, window=0, out_rows=rows)
    ow = _flash(ya, ya16, None, nb_p, t_p, q_ch=CH_QA, k_ch=CH_KW, v_ch=CH_VW, kvh=NSA_KV, nh=NSA_HPG, blk=0,
                window=NSA_WINDOW, out_rows=rows)
    ob = _flash(yb, yb16, sel_b, nb_p, t_p, q_ch=CH_QB, k_ch=CH_KB, v_ch=CH_VB, kvh=MOBA_HEADS, nh=1,
                blk=MOBA_BLOCK, window=0, out_rows=rows)

    n_pad = x[2] - n_prompt
    ys = ya[n_prompt:]
    ysb = yb[n_prompt:]
    kvc_s = _compress_c2(_compress_c1_sample(cache_cmp, li, page_table, wcat), pe_rows, wcat, w2, gk)
    q_pad = jnp.zeros((nb_s, SUBLANE, NSA_HEADS * HEAD_DIM), F32).at[:, 0].set(ys[:nb_s, :NSA_HEADS * HEAD_DIM])
    oc_s, sel_s = _cattn_sample(q_pad.reshape(nb_s * SUBLANE, -1), kvc_s, nb_s, past_len + 1, past_len)
    oc_s = oc_s.reshape(nb_s, SUBLANE, -1)[:, 0]
    n_sel = (past_len + 1) // NSA_SEL_BLOCK
    k_sel = min(NSA_SEL_COUNT - 1, n_sel)
    mask_a = sel_s[:, :, 0, :n_sel]
    idx_a = jnp.argsort(-mask_a, axis=-1, stable=True)[..., :k_sel].astype(jnp.int32)
    vld_a = (jnp.take_along_axis(mask_a, idx_a, axis=-1) > 0).astype(jnp.int32)
    page = cache_sel.shape[2]
    bpp = page // NSA_SEL_BLOCK
    pt_b = page_table[:, None, :]
    tbl_a = jnp.take_along_axis(jnp.broadcast_to(pt_b, (nb_s, NSA_KV, pt_b.shape[-1])), idx_a // bpp, axis=-1) * bpp + idx_a % bpp
    os_s = _decode_attn(ys, cache_sel, li, tbl_a.astype(jnp.int32), vld_a, q_ch=CH_QA, kn_ch=CH_KS, vn_ch=CH_VS,
                        nh=NSA_HPG, rows=NSA_SEL_BLOCK, kvh=NSA_KV, r_min=0,
                        name="dec_sel")
    win_buf = state_win.shape[2]
    tbl_w = jnp.broadcast_to(jnp.arange(nb_s, dtype=jnp.int32)[:, None, None], (nb_s, NSA_KV, 1))
    ow_s = _decode_attn(ys, state_win, li, tbl_w, jnp.ones_like(tbl_w), q_ch=CH_QA, kn_ch=CH_KW, vn_ch=CH_VW,
                        nh=NSA_HPG, rows=win_buf, kvh=NSA_KV, r_min=max(0, win_buf - (NSA_WINDOW - 1)),
                        name="dec_win")
    means = _moba_means_sample(cache_moba, li, page_table)
    qb_pad = jnp.zeros((nb_s, SUBLANE, MOBA_HEADS * HEAD_DIM), F32).at[:, 0].set(
        ysb[:nb_s, CH_QB * LANE:(CH_QB + MOBA_HEADS) * LANE])
    sel_m = _moba_gate_sample(qb_pad.reshape(nb_s * SUBLANE, -1), means, past_len)
    n_full = means.shape[2]
    k_top = min(MOBA_TOPK, n_full)
    mask_b = sel_m[:, :, 0, :]
    idx_b = jnp.argsort(-mask_b, axis=-1, stable=True)[..., :k_top].astype(jnp.int32)
    vld_b = (jnp.take_along_axis(mask_b, idx_b, axis=-1) > 0).astype(jnp.int32)
    ppb = MOBA_BLOCK // page
    pg_b = (idx_b[..., None] * ppb + jnp.arange(ppb, dtype=jnp.int32)).reshape(nb_s, MOBA_HEADS, k_top * ppb)
    tbl_b = jnp.take_along_axis(jnp.broadcast_to(pt_b, (nb_s, MOBA_HEADS, pt_b.shape[-1])), pg_b, axis=-1)
    vld_b = jnp.repeat(vld_b, ppb, axis=-1)
    ob_s = _decode_attn(ysb, cache_moba, li, tbl_b.astype(jnp.int32), vld_b, q_ch=CH_QB, kn_ch=CH_KB, vn_ch=CH_VB,
                        nh=1, rows=page, kvh=MOBA_HEADS, r_min=0,
                        name="dec_moba")

    def rows_s(part, nh):
        return jnp.pad(part[:, :, :nh].reshape(nb_s, -1), ((0, n_pad - nb_s), (0, 0)))

    sample_parts = (jnp.pad(oc_s, ((0, n_pad - nb_s), (0, 0))), rows_s(os_s, NSA_HPG), rows_s(ow_s, NSA_HPG),
                    rows_s(ob_s, 1))
    mixed = _mix((oc, os_, ow, ob), sample_parts, yg)
    x = _gmm(mixed, [(w_out[None], 0)], d, resid=x, tm=STREAM_TM, tn=1024, name="attn_out")
    hid = _gmm(_rmsnorm(x, ffn_norm), [(w_gate[None], 0), (w_up[None], 0)], w_gate.shape[1], epi="swiglu",
               out_dtype=BF, tm=STREAM_TM, tn=512, name="ffn_up")
    x = _gmm(hid, [(w_down[None], 0)], d, resid=x, tm=STREAM_TM, tn=512, name="ffn_down")

    def rows_of(lo_ch, n_ch, y=ya):
        return y[:, lo_ch * LANE:(lo_ch + n_ch) * LANE]

    def split(a, kvh):
        ap = a[:n_prompt].reshape(nb_p, t_p, 2, kvh, HEAD_DIM)
        as_ = a[n_prompt:n_prompt + nb_s].reshape(nb_s, 1, 2, kvh, HEAD_DIM)
        return ap, as_

    cmp_p, cmp_s = split(rows_of(CH_KC, 4), NSA_KV)
    sel_p, sel_s_rows = split(rows_of(CH_KS, 4), NSA_KV)
    win_rows = rows_of(CH_KW, 4)
    win_s = win_rows[n_prompt:n_prompt + nb_s].reshape(nb_s, 1, 2, NSA_KV, HEAD_DIM)
    moba_p, moba_s = split(rows_of(CH_KB, 16, yb), MOBA_HEADS)
    keep_p = min(NSA_WINDOW, t_p)
    new_win_p = _seq_tails(win_rows, nb_p, t_p, keep_p).reshape(nb_p, keep_p, 2, NSA_KV, HEAD_DIM)
    win_all = jnp.concatenate([state_win[li], win_s], axis=1)
    keep = min(NSA_WINDOW, past_len + 1)
    new_win_s = win_all[:, win_all.shape[1] - keep:]
    return x, (cmp_p, cmp_s, sel_p, sel_s_rows, new_win_p, new_win_s, moba_p, moba_s)


def _conv_layer(x, n_prompt, nb_p, t_p, nb_s, prm, states):
    (conv_norm, w_in, pool_w, pool_scale, dw, dw_b, ln_g, ln_b, pw, w_out, moe_norm, router_w, wg, wu, wd) = prm
    state_pool, state_conv = states
    d = x.shape[1]
    width = pool_w.shape[0] * pool_w.shape[1]
    xn = _rmsnorm(x, conv_norm)
    w3 = w_in[None]
    u_pool = _gmm(xn, [(w3, 0)], width, tm=512, tn=width, name="conv_in_pool")
    u_conv = _gmm(xn, [(w3, width), (w3, 2 * width)], width, epi="glu", tm=512, tn=512, name="conv_in_glu")
    prm_mix = (pool_w.astype(BF), pool_scale[None], jnp.pad(dw, ((0, 1), (0, 0))), dw_b[None], ln_g[None], ln_b[None],
               pw.astype(BF))
    pool_buf = max(POOL_WINDOWS) - 1
    conv_buf = CONV_WIDTH - 1
    hp_rows, hc_rows = 16, 32
    ts = min(256, t_p)
    nt = t_p // ts
    mixed_p = _convmix(u_pool, u_pool, u_conv, u_conv, prm_mix, ts=ts, nt=nt, n_tiles=nb_p * nt, zero_first=True,
                       avail0=0, hp_rows=hp_rows, hc_rows=hc_rows,
                       halo_blk=lambda s, hr: jnp.maximum(s * (ts // hr) - 1, 0))
    rows = x.shape[0]
    us_pool = jnp.zeros((nb_s, SUBLANE, width), F32).at[:, 0].set(u_pool[n_prompt:n_prompt + nb_s])
    us_conv = jnp.zeros((nb_s, SUBLANE, width), F32).at[:, 0].set(u_conv[n_prompt:n_prompt + nb_s])
    hs_pool = jnp.pad(state_pool, ((0, 0), (hp_rows - pool_buf, 0), (0, 0))).reshape(nb_s * hp_rows, width)
    hs_conv = jnp.pad(state_conv, ((0, 0), (hc_rows - conv_buf, 0), (0, 0))).reshape(nb_s * hc_rows, width)
    mixed_s = _convmix(us_pool.reshape(nb_s * SUBLANE, width), hs_pool, us_conv.reshape(nb_s * SUBLANE, width), hs_conv,
                       prm_mix, ts=SUBLANE, nt=1, n_tiles=nb_s, zero_first=False, avail0=pool_buf, hp_rows=hp_rows,
                       hc_rows=hc_rows, halo_blk=lambda s, hr: s)
    mixed_s = mixed_s.reshape(nb_s, SUBLANE, 2 * width)[:, 0]
    mixed = (mixed_p, jnp.pad(mixed_s, ((0, STREAM_TM - nb_s), (0, 0))), rows)
    x = _gmm(mixed, [(w_out[None], 0)], d, resid=x, tm=STREAM_TM, tn=1024, name="conv_out")
    x = _moe(x, moe_norm, router_w, wg, wu, wd, n_prompt)

    assert t_p >= conv_buf and t_p >= pool_buf
    new_pool_p = _seq_tails(u_pool, nb_p, t_p, pool_buf)
    new_conv_p = _seq_tails(u_conv, nb_p, t_p, conv_buf)
    new_pool_s = jnp.concatenate([state_pool, u_pool[n_prompt:n_prompt + nb_s][:, None]], axis=1)[:, 1:]
    new_conv_s = jnp.concatenate([state_conv, u_conv[n_prompt:n_prompt + nb_s][:, None]], axis=1)[:, 1:]
    return x, (new_pool_p, new_pool_s, new_conv_p, new_conv_s)


def kernel(x_prompt, x_sample, cache_nsa_cmp, cache_nsa_sel, state_nsa_win, cache_moba, state_pool, state_conv, page_table, attn_norm, w_attn_in, nsa_qk_norm, nsa_cmp_pe, nsa_cmp_w1, nsa_cmp_w2, moba_qk_norm, w_attn_out, ffn_norm, ffn_w_gate, ffn_w_up, ffn_w_down, conv_norm, w_conv_in, pool_w, pool_scale, conv_dw, conv_dw_b, conv_ln_g, conv_ln_b, conv_pw, w_conv_out, moe_norm, router_w, moe_w_gate, moe_w_up, moe_w_down):
    nb_p, t_p, d = x_prompt.shape
    nb_s = x_sample.shape[0]
    n_prompt = nb_p * t_p
    n_pad = 2 * SUBLANE
    assert x_sample.shape[1] == 1 and nb_s <= n_pad
    x = (x_prompt.reshape(n_prompt, d), jnp.pad(x_sample.reshape(nb_s, d), ((0, STREAM_TM - nb_s), (0, 0))),
         n_prompt + n_pad)
    li = 0
    prm_a = (attn_norm[li], w_attn_in[li], nsa_qk_norm[li], nsa_cmp_pe[li], nsa_cmp_w1[li], nsa_cmp_w2[li],
             moba_qk_norm[li], w_attn_out[li], ffn_norm[li], ffn_w_gate[li], ffn_w_up[li], ffn_w_down[li])
    x, attn_new = _attn_layer(x, n_prompt, nb_p, t_p, nb_s, li, prm_a,
                              (cache_nsa_cmp, cache_nsa_sel, state_nsa_win, cache_moba), page_table)
    prm_c = (conv_norm[li], w_conv_in[li], pool_w[li], pool_scale[li], conv_dw[li], conv_dw_b[li], conv_ln_g[li],
             conv_ln_b[li], conv_pw[li], w_conv_out[li], moe_norm[li], router_w[li], moe_w_gate[li], moe_w_up[li],
             moe_w_down[li])
    x, conv_new = _conv_layer(x, n_prompt, nb_p, t_p, nb_s, prm_c, (state_pool[li], state_conv[li]))
    cmp_p, cmp_s, sel_p, sel_s, win_p, win_s, moba_p, moba_s = attn_new
    pool_p, pool_s, conv_p, conv_s = conv_new
    y_p = x[0].reshape(nb_p, t_p, d)
    y_s = x[1][:nb_s].reshape(nb_s, 1, d)
    st = lambda a: a[None]
    return (y_p, y_s, st(cmp_p), st(cmp_s), st(sel_p), st(sel_s), st(win_p), st(win_s), st(moba_p), st(moba_s),
            st(pool_p), st(pool_s), st(conv_p), st(conv_s))
```
